```python
import math
import jax, jax.numpy as jnp
from jax import lax
import numpy as np

D_MODEL = 1024
BATCH = 8
SEQ = 8192
DEPTH = 2

HEAD_DIM = 64
BLOCK = 128
A_GROUPS = ((128, 1), (512, 4), (2048, 16))
A_HEADS = 8
A_TOTAL_HEADS = len(A_GROUPS) * A_HEADS
B_Q_HEADS = 8
B_KV_HEADS = 2
B_WINDOW = 128
C_HEADS = 8
C_Q_RANK = 256
C_KV_RANK = 128
C_NOPE = 64
C_ROPE = 32
C_V = 64
ROPE_BASE = 10000.0
REL_BUCKETS = 32
REL_MAX_DIST = 2048
N_REL_HEADS = A_TOTAL_HEADS + B_Q_HEADS
N_BRANCH = 3
BRANCH_WIDTH = 512
D_FF = 2816
CONV_WIDTH = 3
ALPHA = (2 * DEPTH) ** 0.25
BETA = (8 * DEPTH) ** -0.25
LN_EPS = 1e-5
RMS_EPS = 1e-6
NEG = -1e30

A_QKV_COLS = len(A_GROUPS) * 3 * A_HEADS * HEAD_DIM
B_Q_COLS = B_Q_HEADS * HEAD_DIM
B_KV_COLS = 2 * B_KV_HEADS * HEAD_DIM
C_DQ_COLS = C_Q_RANK
C_DKV_COLS = C_KV_RANK + C_ROPE
GATE_COLS = N_BRANCH * D_MODEL
IN_SPLITS = (A_QKV_COLS,
             A_QKV_COLS + B_Q_COLS,
             A_QKV_COLS + B_Q_COLS + B_KV_COLS,
             A_QKV_COLS + B_Q_COLS + B_KV_COLS + C_DQ_COLS,
             A_QKV_COLS + B_Q_COLS + B_KV_COLS + C_DQ_COLS + C_DKV_COLS)
D_IN = IN_SPLITS[-1] + GATE_COLS

kernel_name = "hybrid_dilated_swa_mla_convglu_block"


def layer_norm(x, g, b):
    xf = x.astype(jnp.float32)
    mu = xf.mean(-1, keepdims=True)
    var = jnp.square(xf - mu).mean(-1, keepdims=True)
    y = (xf - mu) * lax.rsqrt(var + LN_EPS) * g.astype(jnp.float32) + b.astype(jnp.float32)
    return y.astype(x.dtype)


def rms_norm(x, g):
    xf = x.astype(jnp.float32)
    y = xf * lax.rsqrt(jnp.mean(xf * xf, -1, keepdims=True) + RMS_EPS) * g.astype(jnp.float32)
    return y.astype(x.dtype)


def t5_bucket(dist):
    n = jnp.maximum(dist, 0)
    max_exact = REL_BUCKETS // 2
    scaled = jnp.log(jnp.maximum(n, 1).astype(jnp.float32) / max_exact) / math.log(REL_MAX_DIST / max_exact)
    large = max_exact + (scaled * (REL_BUCKETS - max_exact)).astype(jnp.int32)
    return jnp.where(n < max_exact, n, jnp.minimum(large, REL_BUCKETS - 1))


def apply_rope(x, cos, sin):
    x1, x2 = jnp.split(x, 2, axis=-1)
    c = cos[:, None, :].astype(x.dtype)
    s = sin[:, None, :].astype(x.dtype)
    return jnp.concatenate([x1 * c - x2 * s, x1 * s + x2 * c], axis=-1)


def dilated_group_attention(q, k, v, bias_table, window, dilation):
    b, s, h, dh = q.shape
    w = window // dilation
    span = w * dilation
    s_pad = -(-s // span) * span
    n_sub = s_pad // dilation
    nb = n_sub // w

    def to_blocks(t):
        t = jnp.pad(t, ((0, 0), (0, s_pad - s), (0, 0), (0, 0)))
        t = t.reshape(b, n_sub, dilation, h, dh).transpose(0, 2, 1, 3, 4)
        return t.reshape(b, dilation, nb, w, h, dh)

    def with_prev(t):
        prev = jnp.pad(t[:, :, :-1], ((0, 0), (0, 0), (1, 0), (0, 0), (0, 0), (0, 0)))
        return jnp.concatenate([prev, t], axis=3)

    qb = to_blocks(q)
    kb = with_prev(to_blocks(k))
    vb = with_prev(to_blocks(v))
    logits = jnp.einsum('brnqhd,brnkhd->brnhqk', qb, kb).astype(jnp.float32) * (dh ** -0.5)
    qi = jnp.arange(w)[:, None]
    ki = jnp.arange(2 * w)[None, :]
    step = w + qi - ki
    band = (step >= 0) & (step <= w)
    valid = band[None] & ((jnp.arange(nb)[:, None, None] > 0) | (ki >= w)[None])
    bias = bias_table[t5_bucket(step * dilation)].astype(jnp.float32).transpose(2, 0, 1)
    logits = jnp.where(valid[None, None, :, None], logits + bias, NEG)
    m = logits.max(-1, keepdims=True)
    p = jnp.exp(logits - m)
    l = p.sum(-1, keepdims=True)
    o = jnp.einsum('brnhqk,brnkhd->brnqhd', (p / l).astype(v.dtype), vb)
    lse = (m + jnp.log(l))[..., 0]
    o = o.reshape(b, dilation, n_sub, h, dh).transpose(0, 2, 1, 3, 4).reshape(b, s_pad, h, dh)[:, :s]
    lse = lse.transpose(0, 1, 2, 4, 3).reshape(b, dilation, n_sub, h).transpose(0, 2, 1, 3)
    lse = lse.reshape(b, s_pad, h)[:, :s]
    return o, lse


def sliding_window_sink_attention(q, k, v, sinks, bias_table):
    b, s, hq, dh = q.shape
    hkv = k.shape[2]
    g = hq // hkv
    nb = s // BLOCK
    qb = q.reshape(b, nb, BLOCK, hkv, g, dh)

    def with_prev(t):
        t = t.reshape(b, nb, BLOCK, hkv, dh)
        prev = jnp.pad(t[:, :-1], ((0, 0), (1, 0), (0, 0), (0, 0), (0, 0)))
        return jnp.concatenate([prev, t], axis=2)

    kb = with_prev(k)
    vb = with_prev(v)
    logits = jnp.einsum('bnqhgd,bnchd->bnhgqc', qb, kb).astype(jnp.float32) * (dh ** -0.5)
    qi = jnp.arange(BLOCK)[:, None]
    ci = jnp.arange(2 * BLOCK)[None, :]
    dist = BLOCK + qi - ci
    valid = ((dist >= 0) & (dist < B_WINDOW))[None] & ((jnp.arange(nb)[:, None, None] > 0) | (ci >= BLOCK)[None])
    bias = bias_table[t5_bucket(dist)].astype(jnp.float32).transpose(2, 0, 1).reshape(hkv, g, BLOCK, 2 * BLOCK)
    logits = jnp.where(valid[None, :, None, None], logits + bias, NEG)
    sink = sinks.astype(jnp.float32).reshape(1, 1, hkv, g, 1, 1)
    m = jnp.maximum(logits.max(-1, keepdims=True), sink)
    p = jnp.exp(logits - m)
    denom = p.sum(-1, keepdims=True) + jnp.exp(sink - m)
    o = jnp.einsum('bnhgqc,bnchd->bnqhgd', (p / denom).astype(v.dtype), vb)
    return o.reshape(b, s, hq * dh)


def mla_attention(cq, c_kv, k_rope, q_norm_g, kv_norm_g, w_uq, w_ukv, cos, sin):
    b, s, _ = cq.shape
    q = (rms_norm(cq, q_norm_g) @ w_uq).reshape(b, s, C_HEADS, C_NOPE + C_ROPE)
    q_nope = q[..., :C_NOPE]
    q_rope = apply_rope(q[..., C_NOPE:], cos, sin)
    kv = (rms_norm(c_kv, kv_norm_g) @ w_ukv).reshape(b, s, C_HEADS, C_NOPE + C_V)
    k_nope = kv[..., :C_NOPE]
    v = kv[..., C_NOPE:]
    k_r = apply_rope(k_rope[:, :, None, :], cos, sin)[:, :, 0]
    nb = s // BLOCK
    qn_b = q_nope.reshape(b, nb, BLOCK, C_HEADS, C_NOPE).swapaxes(0, 1)
    qr_b = q_rope.reshape(b, nb, BLOCK, C_HEADS, C_ROPE).swapaxes(0, 1)
    kpos = jnp.arange(s)
    scale = (C_NOPE + C_ROPE) ** -0.5

    def one_block(args):
        i, qn, qr = args
        logits = (jnp.einsum('bqhd,bkhd->bhqk', qn, k_nope)
                  + jnp.einsum('bqhr,bkr->bhqk', qr, k_r)).astype(jnp.float32) * scale
        qpos = i * BLOCK + jnp.arange(BLOCK)
        logits = jnp.where(kpos[None, :] <= qpos[:, None], logits, NEG)
        p = jax.nn.softmax(logits, axis=-1).astype(v.dtype)
        return jnp.einsum('bhqk,bkhd->bqhd', p, v)

    o = lax.map(one_block, (jnp.arange(nb), qn_b, qr_b))
    return o.swapaxes(0, 1).reshape(b, s, C_HEADS * C_V)


def hybrid_mixer(x, w_in, b_gate, sinks, q_norm_g, kv_norm_g, w_uq, w_ukv, w_branch, w_out,
                 rel_table, cos, sin):
    b, s, _ = x.shape
    proj = x @ w_in
    a_qkv, b_q, b_kv, c_q, c_dkv, gate_pre = jnp.split(proj, IN_SPLITS, axis=-1)

    a_qkv = a_qkv.reshape(b, s, len(A_GROUPS), 3, A_HEADS, HEAD_DIM)
    outs, lses = [], []
    for gi, (window, dil) in enumerate(A_GROUPS):
        o_g, lse_g = dilated_group_attention(a_qkv[:, :, gi, 0], a_qkv[:, :, gi, 1], a_qkv[:, :, gi, 2],
                                             rel_table[:, gi * A_HEADS:(gi + 1) * A_HEADS], window, dil)
        outs.append(o_g)
        lses.append(lse_g)
    wts = jax.nn.softmax(jnp.stack(lses, axis=0), axis=0)
    o_a = jnp.einsum('gbsh,gbshd->bshd', wts, jnp.stack(outs, axis=0).astype(jnp.float32))
    o_a = o_a.astype(x.dtype).reshape(b, s, A_HEADS * HEAD_DIM)

    b_k, b_v = jnp.split(b_kv, 2, axis=-1)
    o_b = sliding_window_sink_attention(b_q.reshape(b, s, B_Q_HEADS, HEAD_DIM),
                                        b_k.reshape(b, s, B_KV_HEADS, HEAD_DIM),
                                        b_v.reshape(b, s, B_KV_HEADS, HEAD_DIM),
                                        sinks, rel_table[:, A_TOTAL_HEADS:])

    c_kv, k_rope = jnp.split(c_dkv, [C_KV_RANK], axis=-1)
    o_c = mla_attention(c_q, c_kv, k_rope, q_norm_g, kv_norm_g, w_uq, w_ukv, cos, sin)

    gates = jax.nn.sigmoid((gate_pre + b_gate).astype(jnp.float32)).astype(x.dtype)
    gates = gates.reshape(b, s, N_BRANCH, D_MODEL)
    merged = (gates[:, :, 0] * (o_a @ w_branch[0])
              + gates[:, :, 1] * (o_b @ w_branch[1])
              + gates[:, :, 2] * (o_c @ w_branch[2]))
    return merged @ w_out


def conv_glu_ffn(x, w_up, conv_w, conv_b, w_down):
    u = x @ w_up
    c = u.shape[-1]
    u = lax.conv_general_dilated(u, conv_w[:, None, :].astype(u.dtype), window_strides=(1,),
                                 padding=[(CONV_WIDTH - 1, 0)],
                                 dimension_numbers=('NWC', 'WIO', 'NWC'),
                                 feature_group_count=c) + conv_b
    gate, val = jnp.split(u, 2, axis=-1)
    return (jax.nn.silu(gate) * val) @ w_down


def _fwd_setup_inputs(seed: int = 0) -> dict:
    key = jax.random.key(seed)
    ks = jax.random.split(key, 20)
    f32 = jnp.float32
    nrm = lambda k, shape, scale: jax.random.normal(k, shape, f32) * scale
    return {
        'x': jax.random.normal(ks[0], (BATCH, SEQ, D_MODEL), f32),
        'rel_table': nrm(ks[1], (REL_BUCKETS, N_REL_HEADS), 0.2),
        'w_in': nrm(ks[2], (DEPTH, D_MODEL, D_IN), D_MODEL ** -0.5),
        'b_gate': nrm(ks[3], (DEPTH, GATE_COLS), 0.01),
        'sinks': nrm(ks[4], (DEPTH, B_Q_HEADS), 0.5),
        'q_norm_g': 1.0 + nrm(ks[5], (DEPTH, C_Q_RANK), 0.05),
        'kv_norm_g': 1.0 + nrm(ks[6], (DEPTH, C_KV_RANK), 0.05),
        'w_uq': nrm(ks[7], (DEPTH, C_Q_RANK, C_HEADS * (C_NOPE + C_ROPE)), C_Q_RANK ** -0.5),
        'w_ukv': nrm(ks[8], (DEPTH, C_KV_RANK, C_HEADS * (C_NOPE + C_V)), C_KV_RANK ** -0.5),
        'w_branch': nrm(ks[9], (DEPTH, N_BRANCH, BRANCH_WIDTH, D_MODEL), BRANCH_WIDTH ** -0.5 * BETA),
        'w_out': nrm(ks[10], (DEPTH, D_MODEL, D_MODEL), D_MODEL ** -0.5 * BETA),
        'ln1_g': 1.0 + nrm(ks[11], (DEPTH, D_MODEL), 0.05),
        'ln1_b': nrm(ks[12], (DEPTH, D_MODEL), 0.01),
        'w_ffn_up': nrm(ks[13], (DEPTH, D_MODEL, 2 * D_FF), D_MODEL ** -0.5),
        'conv_w': nrm(ks[14], (DEPTH, CONV_WIDTH, 2 * D_FF), CONV_WIDTH ** -0.5),
        'conv_b': nrm(ks[15], (DEPTH, 2 * D_FF), 0.01),
        'w_ffn_down': nrm(ks[16], (DEPTH, D_FF, D_MODEL), D_FF ** -0.5 * BETA),
        'ln2_g': 1.0 + nrm(ks[17], (DEPTH, D_MODEL), 0.05),
        'ln2_b': nrm(ks[18], (DEPTH, D_MODEL), 0.01),
    }


def _fwd_reference(x, rel_table, w_in, b_gate, sinks, q_norm_g, kv_norm_g, w_uq, w_ukv, w_branch, w_out,
              ln1_g, ln1_b, w_ffn_up, conv_w, conv_b, w_ffn_down, ln2_g, ln2_b):
    s = x.shape[1]
    pos = jnp.arange(s, dtype=jnp.float32)
    inv_freq = ROPE_BASE ** (-jnp.arange(0, C_ROPE, 2, dtype=jnp.float32) / C_ROPE)
    ang = pos[:, None] * inv_freq[None, :]
    cos, sin = jnp.cos(ang), jnp.sin(ang)
    for l in range(DEPTH):
        mix = hybrid_mixer(x, w_in[l], b_gate[l], sinks[l], q_norm_g[l], kv_norm_g[l], w_uq[l], w_ukv[l],
                           w_branch[l], w_out[l], rel_table, cos, sin)
        x = layer_norm(ALPHA * x + mix, ln1_g[l], ln1_b[l])
        ff = conv_glu_ffn(x, w_ffn_up[l], conv_w[l], conv_b[l], w_ffn_down[l])
        x = layer_norm(ALPHA * x + ff, ln2_g[l], ln2_b[l])
    return x


import jax as _jax
import jax.numpy as _jnp

TWIN_FORMAT = 'train_step'
FWD_PARAMS = ['x', 'rel_table', 'w_in', 'b_gate', 'sinks', 'q_norm_g', 'kv_norm_g', 'w_uq', 'w_ukv', 'w_branch', 'w_out', 'ln1_g', 'ln1_b', 'w_ffn_up', 'conv_w', 'conv_b', 'w_ffn_down', 'ln2_g', 'ln2_b']
TWIN_WEIGHTS = ['rel_table', 'w_in', 'b_gate', 'sinks', 'q_norm_g', 'kv_norm_g', 'w_uq', 'w_ukv', 'w_branch', 'w_out', 'ln1_g', 'ln1_b', 'w_ffn_up', 'conv_w', 'conv_b', 'w_ffn_down', 'ln2_g', 'ln2_b']
TWIN_DIFF_INPUT = 'x'
TWIN_INPUTS = ['x', 'rel_table', 'w_in', 'b_gate', 'sinks', 'q_norm_g', 'kv_norm_g', 'w_uq', 'w_ukv', 'w_branch', 'w_out', 'ln1_g', 'ln1_b', 'w_ffn_up', 'conv_w', 'conv_b', 'w_ffn_down', 'ln2_g', 'ln2_b', 'loss_target', 'm_rel_table', 'm_w_in', 'm_b_gate', 'm_sinks', 'm_q_norm_g', 'm_kv_norm_g', 'm_w_uq', 'm_w_ukv', 'm_w_branch', 'm_w_out', 'm_ln1_g', 'm_ln1_b', 'm_w_ffn_up', 'm_conv_w', 'm_conv_b', 'm_w_ffn_down', 'm_ln2_g', 'm_ln2_b', 'v_rel_table', 'v_w_in', 'v_b_gate', 'v_sinks', 'v_q_norm_g', 'v_kv_norm_g', 'v_w_uq', 'v_w_ukv', 'v_w_branch', 'v_w_out', 'v_ln1_g', 'v_ln1_b', 'v_w_ffn_up', 'v_conv_w', 'v_conv_b', 'v_w_ffn_down', 'v_ln2_g', 'v_ln2_b']
TWIN_OUTPUTS = ['loss', 'grad_x', 'grad_rel_table', 'grad_w_in', 'grad_b_gate', 'grad_sinks', 'grad_q_norm_g', 'grad_kv_norm_g', 'grad_w_uq', 'grad_w_ukv', 'grad_w_branch', 'grad_w_out', 'grad_ln1_g', 'grad_ln1_b', 'grad_w_ffn_up', 'grad_conv_w', 'grad_conv_b', 'grad_w_ffn_down', 'grad_ln2_g', 'grad_ln2_b', 'delta_rel_table', 'delta_w_in', 'delta_b_gate', 'delta_sinks', 'delta_q_norm_g', 'delta_kv_norm_g', 'delta_w_uq', 'delta_w_ukv', 'delta_w_branch', 'delta_w_out', 'delta_ln1_g', 'delta_ln1_b', 'delta_w_ffn_up', 'delta_conv_w', 'delta_conv_b', 'delta_w_ffn_down', 'delta_ln2_g', 'delta_ln2_b', 'new_m_rel_table', 'new_m_w_in', 'new_m_b_gate', 'new_m_sinks', 'new_m_q_norm_g', 'new_m_kv_norm_g', 'new_m_w_uq', 'new_m_w_ukv', 'new_m_w_branch', 'new_m_w_out', 'new_m_ln1_g', 'new_m_ln1_b', 'new_m_w_ffn_up', 'new_m_conv_w', 'new_m_conv_b', 'new_m_w_ffn_down', 'new_m_ln2_g', 'new_m_ln2_b', 'new_v_rel_table', 'new_v_w_in', 'new_v_b_gate', 'new_v_sinks', 'new_v_q_norm_g', 'new_v_kv_norm_g', 'new_v_w_uq', 'new_v_w_ukv', 'new_v_w_branch', 'new_v_w_out', 'new_v_ln1_g', 'new_v_ln1_b', 'new_v_w_ffn_up', 'new_v_conv_w', 'new_v_conv_b', 'new_v_w_ffn_down', 'new_v_ln2_g', 'new_v_ln2_b']
TWIN_LEAF_KINDS = {'loss': 'loss', 'grad_x': 'grad_x', 'grad_rel_table': 'grad_w', 'grad_w_in': 'grad_w', 'grad_b_gate': 'grad_w', 'grad_sinks': 'grad_w', 'grad_q_norm_g': 'grad_w', 'grad_kv_norm_g': 'grad_w', 'grad_w_uq': 'grad_w', 'grad_w_ukv': 'grad_w', 'grad_w_branch': 'grad_w', 'grad_w_out': 'grad_w', 'grad_ln1_g': 'grad_w', 'grad_ln1_b': 'grad_w', 'grad_w_ffn_up': 'grad_w', 'grad_conv_w': 'grad_w', 'grad_conv_b': 'grad_w', 'grad_w_ffn_down': 'grad_w', 'grad_ln2_g': 'grad_w', 'grad_ln2_b': 'grad_w', 'delta_rel_table': 'delta_w', 'delta_w_in': 'delta_w', 'delta_b_gate': 'delta_w', 'delta_sinks': 'delta_w', 'delta_q_norm_g': 'delta_w', 'delta_kv_norm_g': 'delta_w', 'delta_w_uq': 'delta_w', 'delta_w_ukv': 'delta_w', 'delta_w_branch': 'delta_w', 'delta_w_out': 'delta_w', 'delta_ln1_g': 'delta_w', 'delta_ln1_b': 'delta_w', 'delta_w_ffn_up': 'delta_w', 'delta_conv_w': 'delta_w', 'delta_conv_b': 'delta_w', 'delta_w_ffn_down': 'delta_w', 'delta_ln2_g': 'delta_w', 'delta_ln2_b': 'delta_w', 'new_m_rel_table': 'new_m', 'new_m_w_in': 'new_m', 'new_m_b_gate': 'new_m', 'new_m_sinks': 'new_m', 'new_m_q_norm_g': 'new_m', 'new_m_kv_norm_g': 'new_m', 'new_m_w_uq': 'new_m', 'new_m_w_ukv': 'new_m', 'new_m_w_branch': 'new_m', 'new_m_w_out': 'new_m', 'new_m_ln1_g': 'new_m', 'new_m_ln1_b': 'new_m', 'new_m_w_ffn_up': 'new_m', 'new_m_conv_w': 'new_m', 'new_m_conv_b': 'new_m', 'new_m_w_ffn_down': 'new_m', 'new_m_ln2_g': 'new_m', 'new_m_ln2_b': 'new_m', 'new_v_rel_table': 'new_v', 'new_v_w_in': 'new_v', 'new_v_b_gate': 'new_v', 'new_v_sinks': 'new_v', 'new_v_q_norm_g': 'new_v', 'new_v_kv_norm_g': 'new_v', 'new_v_w_uq': 'new_v', 'new_v_w_ukv': 'new_v', 'new_v_w_branch': 'new_v', 'new_v_w_out': 'new_v', 'new_v_ln1_g': 'new_v', 'new_v_ln1_b': 'new_v', 'new_v_w_ffn_up': 'new_v', 'new_v_conv_w': 'new_v', 'new_v_conv_b': 'new_v', 'new_v_w_ffn_down': 'new_v', 'new_v_ln2_g': 'new_v', 'new_v_ln2_b': 'new_v'}


def _forward(args):
    return _fwd_reference(*[args[k] for k in FWD_PARAMS])


def _output_shape():
    def fwd():
        inp = _fwd_setup_inputs(0)
        return _fwd_reference(*[inp[k] for k in FWD_PARAMS])
    out = _jax.eval_shape(fwd)
    return out.shape, out.dtype

N_MICROBATCH = 1
ADAM_LR = 0.001
ADAM_B1 = 0.9
ADAM_B2 = 0.999
ADAM_EPS = 1e-08
ADAM_WD = 0.01
ADAM_STEP = 10
PER_EXAMPLE_BATCH_AXIS = {'x': 0, 'loss_target': 0}
SHARED_INPUTS = []
_WEIGHT_DTYPES = {'rel_table': _jnp.float32, 'w_in': _jnp.float32, 'b_gate': _jnp.float32, 'sinks': _jnp.float32, 'q_norm_g': _jnp.float32, 'kv_norm_g': _jnp.float32, 'w_uq': _jnp.float32, 'w_ukv': _jnp.float32, 'w_branch': _jnp.float32, 'w_out': _jnp.float32, 'ln1_g': _jnp.float32, 'ln1_b': _jnp.float32, 'w_ffn_up': _jnp.float32, 'conv_w': _jnp.float32, 'conv_b': _jnp.float32, 'w_ffn_down': _jnp.float32, 'ln2_g': _jnp.float32, 'ln2_b': _jnp.float32}
MOMENT_SCALE = {'rel_table': 7.843819e-03, 'w_in': 4.960174e-03, 'b_gate': 2.051620e-03, 'sinks': 5.498038e-03, 'q_norm_g': 9.283391e-03, 'kv_norm_g': 2.095322e-02, 'w_uq': 5.425811e-03, 'w_ukv': 7.002047e-03, 'w_branch': 1.059538e-02, 'w_out': 1.829817e-02, 'ln1_g': 5.917871e+00, 'ln1_b': 4.886930e-01, 'w_ffn_up': 3.268367e-02, 'conv_w': 3.308917e-02, 'conv_b': 3.454084e-02, 'w_ffn_down': 1.074907e-01, 'ln2_g': 4.601451e+01, 'ln2_b': 7.518248e-01}


def _to_microbatches(a, axis):
    t = _jnp.moveaxis(a, axis, 0)
    t = t.reshape((N_MICROBATCH, t.shape[0] // N_MICROBATCH) + t.shape[1:])
    return _jnp.moveaxis(t, 1, axis + 1)


def setup_inputs(seed: int = 0) -> dict:
    inp = _fwd_setup_inputs(seed)
    key = _jax.random.fold_in(_jax.random.key(seed), 7919)
    shape, _ = _output_shape()
    out = dict(inp)
    out["loss_target"] = _jax.random.normal(_jax.random.fold_in(key, 0), shape, _jnp.float32)
    for i, name in enumerate(TWIN_WEIGHTS):
        w = inp[name].astype(_jnp.float32)
        if MOMENT_SCALE is None:
            s = _jnp.sqrt(_jnp.mean(_jnp.square(w)) + 1e-30)
        else:
            s = MOMENT_SCALE[name]
        km, kv = _jax.random.split(_jax.random.fold_in(key, i + 1))
        out[name] = w
        out["m_" + name] = s * _jax.random.normal(km, w.shape, _jnp.float32)
        out["v_" + name] = (s * s) * _jax.random.uniform(kv, w.shape, _jnp.float32, 0.5, 1.5)
    if N_MICROBATCH > 1:
        for name, axis in PER_EXAMPLE_BATCH_AXIS.items():
            out[name] = _to_microbatches(out[name], axis)
    return {'x': out['x'], 'rel_table': out['rel_table'], 'w_in': out['w_in'], 'b_gate': out['b_gate'], 'sinks': out['sinks'], 'q_norm_g': out['q_norm_g'], 'kv_norm_g': out['kv_norm_g'], 'w_uq': out['w_uq'], 'w_ukv': out['w_ukv'], 'w_branch': out['w_branch'], 'w_out': out['w_out'], 'ln1_g': out['ln1_g'], 'ln1_b': out['ln1_b'], 'w_ffn_up': out['w_ffn_up'], 'conv_w': out['conv_w'], 'conv_b': out['conv_b'], 'w_ffn_down': out['w_ffn_down'], 'ln2_g': out['ln2_g'], 'ln2_b': out['ln2_b'], 'loss_target': out['loss_target'], 'm_rel_table': out['m_rel_table'], 'm_w_in': out['m_w_in'], 'm_b_gate': out['m_b_gate'], 'm_sinks': out['m_sinks'], 'm_q_norm_g': out['m_q_norm_g'], 'm_kv_norm_g': out['m_kv_norm_g'], 'm_w_uq': out['m_w_uq'], 'm_w_ukv': out['m_w_ukv'], 'm_w_branch': out['m_w_branch'], 'm_w_out': out['m_w_out'], 'm_ln1_g': out['m_ln1_g'], 'm_ln1_b': out['m_ln1_b'], 'm_w_ffn_up': out['m_w_ffn_up'], 'm_conv_w': out['m_conv_w'], 'm_conv_b': out['m_conv_b'], 'm_w_ffn_down': out['m_w_ffn_down'], 'm_ln2_g': out['m_ln2_g'], 'm_ln2_b': out['m_ln2_b'], 'v_rel_table': out['v_rel_table'], 'v_w_in': out['v_w_in'], 'v_b_gate': out['v_b_gate'], 'v_sinks': out['v_sinks'], 'v_q_norm_g': out['v_q_norm_g'], 'v_kv_norm_g': out['v_kv_norm_g'], 'v_w_uq': out['v_w_uq'], 'v_w_ukv': out['v_w_ukv'], 'v_w_branch': out['v_w_branch'], 'v_w_out': out['v_w_out'], 'v_ln1_g': out['v_ln1_g'], 'v_ln1_b': out['v_ln1_b'], 'v_w_ffn_up': out['v_w_ffn_up'], 'v_conv_w': out['v_conv_w'], 'v_conv_b': out['v_conv_b'], 'v_w_ffn_down': out['v_w_ffn_down'], 'v_ln2_g': out['v_ln2_g'], 'v_ln2_b': out['v_ln2_b']}


def _loss(weights, diff, rest, loss_target):
    with _jax.named_scope("forward"):
        args = {**rest, TWIN_DIFF_INPUT: diff, **{k: w.astype(_WEIGHT_DTYPES[k]) for k, w in weights.items()}}
        y = _forward(args)
    with _jax.named_scope("loss_head"):
        err = _jnp.square(y.astype(_jnp.float32) - loss_target)
        return 0.5 * _jnp.sum(_jnp.mean(err, axis=-1)) if err.ndim else 0.5 * err


def _adamw(w, g, m, v):
    m = ADAM_B1 * m + (1.0 - ADAM_B1) * g
    v = ADAM_B2 * v + (1.0 - ADAM_B2) * _jnp.square(g)
    m_hat = m / (1.0 - ADAM_B1 ** ADAM_STEP)
    v_hat = v / (1.0 - ADAM_B2 ** ADAM_STEP)
    delta = -ADAM_LR * (m_hat / (_jnp.sqrt(v_hat) + ADAM_EPS) + ADAM_WD * w)
    return delta, m, v


def reference(x, rel_table, w_in, b_gate, sinks, q_norm_g, kv_norm_g, w_uq, w_ukv, w_branch, w_out, ln1_g, ln1_b, w_ffn_up, conv_w, conv_b, w_ffn_down, ln2_g, ln2_b, loss_target, m_rel_table, m_w_in, m_b_gate, m_sinks, m_q_norm_g, m_kv_norm_g, m_w_uq, m_w_ukv, m_w_branch, m_w_out, m_ln1_g, m_ln1_b, m_w_ffn_up, m_conv_w, m_conv_b, m_w_ffn_down, m_ln2_g, m_ln2_b, v_rel_table, v_w_in, v_b_gate, v_sinks, v_q_norm_g, v_kv_norm_g, v_w_uq, v_w_ukv, v_w_branch, v_w_out, v_ln1_g, v_ln1_b, v_w_ffn_up, v_conv_w, v_conv_b, v_w_ffn_down, v_ln2_g, v_ln2_b):
    given = dict(x=x, rel_table=rel_table, w_in=w_in, b_gate=b_gate, sinks=sinks, q_norm_g=q_norm_g, kv_norm_g=kv_norm_g, w_uq=w_uq, w_ukv=w_ukv, w_branch=w_branch, w_out=w_out, ln1_g=ln1_g, ln1_b=ln1_b, w_ffn_up=w_ffn_up, conv_w=conv_w, conv_b=conv_b, w_ffn_down=w_ffn_down, ln2_g=ln2_g, ln2_b=ln2_b, loss_target=loss_target, m_rel_table=m_rel_table, m_w_in=m_w_in, m_b_gate=m_b_gate, m_sinks=m_sinks, m_q_norm_g=m_q_norm_g, m_kv_norm_g=m_kv_norm_g, m_w_uq=m_w_uq, m_w_ukv=m_w_ukv, m_w_branch=m_w_branch, m_w_out=m_w_out, m_ln1_g=m_ln1_g, m_ln1_b=m_ln1_b, m_w_ffn_up=m_w_ffn_up, m_conv_w=m_conv_w, m_conv_b=m_conv_b, m_w_ffn_down=m_w_ffn_down, m_ln2_g=m_ln2_g, m_ln2_b=m_ln2_b, v_rel_table=v_rel_table, v_w_in=v_w_in, v_b_gate=v_b_gate, v_sinks=v_sinks, v_q_norm_g=v_q_norm_g, v_kv_norm_g=v_kv_norm_g, v_w_uq=v_w_uq, v_w_ukv=v_w_ukv, v_w_branch=v_w_branch, v_w_out=v_w_out, v_ln1_g=v_ln1_g, v_ln1_b=v_ln1_b, v_w_ffn_up=v_w_ffn_up, v_conv_w=v_conv_w, v_conv_b=v_conv_b, v_w_ffn_down=v_w_ffn_down, v_ln2_g=v_ln2_g, v_ln2_b=v_ln2_b)
    weights = {n: given[n] for n in TWIN_WEIGHTS}
    shared = {n: given[n] for n in SHARED_INPUTS}
    per_example = {n: given[n] for n in ['x']}
    grad_fn = _jax.value_and_grad(_loss, argnums=(0, 1))

    def one_microbatch(ex, loss_target):
        ex = dict(ex)
        diff = ex.pop(TWIN_DIFF_INPUT)
        return grad_fn(weights, diff, {**shared, **ex}, loss_target)

    if N_MICROBATCH == 1:
        loss, (grad_w, grad_x) = one_microbatch(per_example, given["loss_target"])
    else:
        def body(carry, xs):
            loss_sum, grad_sum = carry
            l_k, (gw_k, gx_k) = one_microbatch(xs[0], xs[1])
            with _jax.named_scope("update"):
                return (loss_sum + l_k, _jax.tree.map(_jnp.add, grad_sum, gw_k)), gx_k

        init = (_jnp.zeros((), _jnp.float32), _jax.tree.map(_jnp.zeros_like, weights))
        (loss, grad_w), grad_x = _jax.lax.scan(body, init, (per_example, given["loss_target"]))
    with _jax.named_scope("update"):
        delta_w, new_m, new_v = {}, {}, {}
        for n in TWIN_WEIGHTS:
            delta_w[n], new_m[n], new_v[n] = _adamw(weights[n], grad_w[n], given["m_" + n], given["v_" + n])
    return (loss, grad_x, *[grad_w[n] for n in TWIN_WEIGHTS], *[delta_w[n] for n in TWIN_WEIGHTS],
            *[new_m[n] for n in TWIN_WEIGHTS], *[new_v[n] for n in TWIN_WEIGHTS])
```

```python
import math

import jax
import jax.numpy as jnp
from jax import lax
from jax.experimental import pallas as pl
from jax.experimental.pallas import tpu as pltpu

F32 = jnp.float32
BF16 = jnp.bfloat16
MESH = pl.DeviceIdType.MESH

D_MODEL = 1024
DEPTH = 2
HEAD_DIM = 64
N_HEADS = 8
A_DILS = (1, 4, 16)
C_Q_RANK = 256
C_KV_RANK = 128
C_NOPE = 64
C_ROPE = 32
ROPE_BASE = 10000.0
REL_BUCKETS = 32
REL_MAX_DIST = 2048
D_FF = 2816
ALPHA = (2 * DEPTH) ** 0.25
LN_EPS = 1e-5
RMS_EPS = 1e-6
NEG = -1e30
ADAM_LR, ADAM_B1, ADAM_B2, ADAM_EPS, ADAM_WD, ADAM_STEP = 0.001, 0.9, 0.999, 1e-08, 0.01, 10

VMEM_LIMIT_BYTES = 56 * 1024 * 1024
LANE = 128
BLK = 128
TQ = 512
HP = 128

COL_GATE, COL_A, COL_BQ, COL_BKV, COL_CQ, COL_CDKV, D_IN_P = 0, 3072, 7680, 8192, 8448, 8704, 8960
D_IN = 8864
ORIG_GATE = 5792

N_CHIP = 4
MATS = (
    ("w_in", (D_MODEL, D_IN), 1),
    ("w_uq", (C_Q_RANK, 768), 1),
    ("w_ukv", (C_KV_RANK, 1024), 1),
    ("w_branch", (3, 512, D_MODEL), 2),
    ("w_out", (D_MODEL, D_MODEL), 0),
    ("w_ffn_up", (D_MODEL, 2 * D_FF), 1),
    ("conv_w", (3, 2 * D_FF), 1),
    ("w_ffn_down", (D_FF, D_MODEL), 0),
)
SMALL = ("b_gate", "sinks", "q_norm_g", "kv_norm_g", "ln1_g", "ln1_b", "conv_b", "ln2_g", "ln2_b")
SMALL_SIZES = {"b_gate": 3072, "sinks": 8, "q_norm_g": 256, "kv_norm_g": 128, "ln1_g": 1024, "ln1_b": 1024,
               "conv_b": 5632, "ln2_g": 1024, "ln2_b": 1024}
WEIGHT_ORDER = ("rel_table", "w_in", "b_gate", "sinks", "q_norm_g", "kv_norm_g", "w_uq", "w_ukv", "w_branch",
                "w_out", "ln1_g", "ln1_b", "w_ffn_up", "conv_w", "conv_b", "w_ffn_down", "ln2_g", "ln2_b")


def _cparams(sem):
    return pltpu.CompilerParams(dimension_semantics=sem, vmem_limit_bytes=VMEM_LIMIT_BYTES)


def _shard_shape(shape, ax):
    s = list(shape)
    s[ax] //= N_CHIP
    return tuple(s)


def _ceil_to(n, m):
    return -(-n // m) * m


def _pick(n, target):
    if n <= target:
        return n
    best = None
    for t in range(LANE, target + 1, LANE):
        if n % t == 0:
            best = t
    assert best is not None, (n, target)
    return best


def _mm(a, b, *, tb=False, out_dtype=F32, tm=512, tn=1024, tk=2048, name):
    m, k = a.shape
    n = b.shape[0] if tb else b.shape[1]
    assert (b.shape[1] if tb else b.shape[0]) == k
    tm, tn, tk = _pick(m, tm), _pick(n, tn), _pick(k, tk)
    nk = k // tk
    dn = (((1,), (1,)), ((), ())) if tb else (((1,), (0,)), ((), ()))

    def body(a_ref, b_ref, o_ref, acc_ref):
        part = lax.dot_general(a_ref[...].astype(BF16), b_ref[...].astype(BF16), dn, preferred_element_type=F32)
        if nk == 1:
            o_ref[...] = part.astype(o_ref.dtype)
        else:
            kk = pl.program_id(2)

            @pl.when(kk == 0)
            def _():
                acc_ref[...] = part

            @pl.when(kk > 0)
            def _():
                acc_ref[...] += part

            @pl.when(kk == nk - 1)
            def _():
                o_ref[...] = acc_ref[...].astype(o_ref.dtype)

    b_spec = pl.BlockSpec((tn, tk), lambda i, j, kk: (j, kk)) if tb else pl.BlockSpec((tk, tn), lambda i, j, kk: (kk, j))
    return pl.pallas_call(
        body, name=name, grid=(m // tm, n // tn, nk),
        in_specs=[pl.BlockSpec((tm, tk), lambda i, j, kk: (i, kk)), b_spec],
        out_specs=pl.BlockSpec((tm, tn), lambda i, j, kk: (i, j)),
        out_shape=jax.ShapeDtypeStruct((m, n), out_dtype),
        scratch_shapes=[pltpu.VMEM((tm, tn) if nk > 1 else (8, LANE), F32)],
        compiler_params=_cparams(("parallel", "parallel", "arbitrary")),
    )(a, b)


def _rowwise(fn, rows, *, pars=(), halos=(), outs=(), accs=(), tm, name, ncol=1, t=None):
    nb = rows[0][0].shape[0]
    t = rows[0][0].shape[1] if t is None else t
    tm = min(tm, t)
    assert t % tm == 0 and tm % 8 == 0
    nt = t // tm
    in_specs, args = [], []
    for spec in rows:
        arr, c, off = spec[:3]
        rb = spec[3] if len(spec) > 3 else 0
        in_specs.append(pl.BlockSpec((1, tm, c), lambda b, cc, i, off=off, rb=rb: (b, i + rb, off + cc)))
        args.append(arr)
    for arr, c, off, kind in halos:
        if kind == "prev":
            im = lambda b, cc, i, off=off: (b, jnp.maximum(i * (tm // 8) - 1, 0), off + cc)
        else:
            im = lambda b, cc, i, off=off: (b, jnp.minimum((i + 1) * (tm // 8), t // 8 - 1), off + cc)
        in_specs.append(pl.BlockSpec((1, 8, c), im))
        args.append(arr)
    for arr, c, off in pars:
        bp, r = arr.shape[:2]
        if bp > 1:
            im = lambda b, cc, i, off=off: (b, 0, off + cc)
        else:
            im = lambda b, cc, i, off=off: (0, 0, off + cc)
        in_specs.append(pl.BlockSpec((1, r, c), im))
        args.append(arr)
    out_specs, out_shapes = [], []
    for ctot, c, off, dt in outs:
        out_specs.append(pl.BlockSpec((1, tm, c), lambda b, cc, i, off=off: (b, i, off + cc)))
        out_shapes.append(jax.ShapeDtypeStruct((nb, t, ctot), dt))
    for r, ctot, c, off in accs:
        out_specs.append(pl.BlockSpec((1, r, c), lambda b, cc, i, off=off: (b, 0, off + cc)))
        out_shapes.append(jax.ShapeDtypeStruct((nb, r, ctot), F32))
    n_in, n_out = len(args), len(outs)

    def body(*refs):
        i = pl.program_id(2)
        res = fn(i, nt, *[r[0] for r in refs[:n_in]])
        if not isinstance(res, (tuple, list)):
            res = (res,)
        for o_ref, val in zip(refs[n_in:n_in + n_out], res[:n_out]):
            o_ref[0] = val.astype(o_ref.dtype)
        for a_ref, val in zip(refs[n_in + n_out:], res[n_out:]):
            @pl.when(i == 0)
            def _(a_ref=a_ref, val=val):
                a_ref[0] = val

            @pl.when(i > 0)
            def _(a_ref=a_ref, val=val):
                a_ref[0] += val

    res = pl.pallas_call(
        body, name=name, grid=(nb, ncol, nt), in_specs=in_specs, out_specs=out_specs, out_shape=out_shapes,
        compiler_params=_cparams(("parallel", "parallel", "arbitrary")),
    )(*args)
    return res


def _lanewise(fn, ins, outs, *, tl, name):
    nb, _, t = ins[0].shape
    tl = min(tl, t)
    assert t % tl == 0
    n_in = len(ins)

    def body(*refs):
        res = fn(*[r[0] for r in refs[:n_in]])
        if not isinstance(res, (tuple, list)):
            res = (res,)
        for o_ref, val in zip(refs[n_in:], res):
            o_ref[0] = val.astype(o_ref.dtype)

    return pl.pallas_call(
        body, name=name, grid=(nb, t // tl),
        in_specs=[pl.BlockSpec((1, a.shape[1], tl), lambda b, i: (b, 0, i)) for a in ins],
        out_specs=[pl.BlockSpec((1, r, tl), lambda b, i: (b, 0, i)) for r, _ in outs],
        out_shape=[jax.ShapeDtypeStruct((nb, r, t), dt) for r, dt in outs],
        compiler_params=_cparams(("parallel", "parallel")),
    )(*ins)


def _dot(a, b):
    return lax.dot_general(a, b, (((1,), (0,)), ((), ())), preferred_element_type=F32)


def _dot_nt(a, b):
    return lax.dot_general(a, b, (((1,), (1,)), ((), ())), preferred_element_type=F32)


def _band_masks(lim):
    ki = lax.broadcasted_iota(jnp.int32, (BLK, BLK), 0)
    qi = lax.broadcasted_iota(jnp.int32, (BLK, BLK), 1)
    return ki <= qi, (BLK + qi - ki) <= lim


def _band_fwd(q, k, vt, bias_t, nbs, lims, sinks, *, name):
    nh, nblk = q.shape[:2]
    scale = HEAD_DIM ** -0.5

    def body(nb_ref, lim_ref, sink_ref, q_ref, k_ref, vt_ref, b_ref, ot_ref, lse_ref):
        h = pl.program_id(0)
        nb, lim, sink = nb_ref[h], lim_ref[h], sink_ref[h]
        cur_ok, prev_ok = _band_masks(lim)
        b_prev, b_cur = b_ref[0, 0:BLK, :], b_ref[0, BLK:2 * BLK, :]

        def blk(b, carry):
            p = jnp.maximum(b - 1, 0)
            qb = q_ref[0, b]
            first = jnp.where(lax.rem(b, nb) == 0, NEG, 0.0).astype(F32)
            sc = jnp.where(cur_ok, _dot_nt(k_ref[0, b], qb) * scale + b_cur, NEG)
            sp = jnp.where(prev_ok, _dot_nt(k_ref[0, p], qb) * scale + b_prev, NEG) + first
            m = jnp.maximum(jnp.maximum(jnp.max(sc, axis=0, keepdims=True), jnp.max(sp, axis=0, keepdims=True)), sink)
            pc, pp = jnp.exp(sc - m), jnp.exp(sp - m)
            l = jnp.sum(pc, axis=0, keepdims=True) + jnp.sum(pp, axis=0, keepdims=True) + jnp.exp(sink - m)
            inv = 1.0 / l
            ot_ref[0, b] = _dot(vt_ref[0, b], (pc * inv).astype(BF16)) + _dot(vt_ref[0, p], (pp * inv).astype(BF16))
            lse_ref[0, b] = m + jnp.log(l)
            return carry

        lax.fori_loop(0, nblk, blk, 0)

    smem = pl.BlockSpec(memory_space=pltpu.SMEM)
    whole = lambda shp: pl.BlockSpec((1,) + shp, lambda h: (h,) + (0,) * len(shp))
    return pl.pallas_call(
        body, name=name, grid=(nh,),
        in_specs=[smem, smem, smem, whole((nblk, BLK, HP)), whole((nblk, BLK, HP)), whole((nblk, HP, BLK)),
                  whole((2 * BLK, BLK))],
        out_specs=[whole((nblk, HP, BLK)), whole((nblk, 1, BLK))],
        out_shape=[jax.ShapeDtypeStruct((nh, nblk, HP, BLK), F32), jax.ShapeDtypeStruct((nh, nblk, 1, BLK), F32)],
        compiler_params=_cparams(("parallel",)),
    )(nbs, lims, sinks, q, k, vt, bias_t)


def _band_bwd(q, k, kt, v, do, lse, dpr, bias_t, nbs, lims, sinks, *, name):
    nh, nblk = q.shape[:2]
    scale = HEAD_DIM ** -0.5

    def body(nb_ref, lim_ref, sink_ref, q_ref, k_ref, kt_ref, v_ref, do_ref, lse_ref, dpr_ref, b_ref,
             dqt_ref, dk_ref, dv_ref, ds_ref, dsink_ref, sink_acc):
        h = pl.program_id(0)
        nb, lim, sink = nb_ref[h], lim_ref[h], sink_ref[h]
        cur_ok, prev_ok = _band_masks(lim)
        b_prev, b_cur = b_ref[0, 0:BLK, :], b_ref[0, BLK:2 * BLK, :]
        ds_ref[...] = jnp.zeros_like(ds_ref)
        sink_acc[...] = jnp.zeros_like(sink_acc)

        def blk(b, carry):
            p = jnp.maximum(b - 1, 0)
            qb, dob = q_ref[0, b], do_ref[0, b]
            lse_row, dpr_row = lse_ref[0, b], dpr_ref[0, b]
            first = jnp.where(lax.rem(b, nb) == 0, NEG, 0.0).astype(F32)
            sc = jnp.where(cur_ok, _dot_nt(k_ref[0, b], qb) * scale + b_cur, NEG)
            sp = jnp.where(prev_ok, _dot_nt(k_ref[0, p], qb) * scale + b_prev, NEG) + first
            pc, pp = jnp.exp(sc - lse_row), jnp.exp(sp - lse_row)
            dsc = pc * (_dot_nt(v_ref[0, b], dob) - dpr_row)
            dsp = pp * (_dot_nt(v_ref[0, p], dob) - dpr_row)
            ds_ref[0, BLK:2 * BLK, :] += dsc
            ds_ref[0, 0:BLK, :] += dsp
            sink_acc[...] += jnp.exp(sink - lse_row) * dpr_row
            dsc_b, dsp_b = (dsc * scale).astype(BF16), (dsp * scale).astype(BF16)
            dqt_ref[0, b] = _dot(kt_ref[0, b], dsc_b) + _dot(kt_ref[0, p], dsp_b)
            dk_ref[0, b] = _dot(dsc_b, qb)
            dv_ref[0, b] = _dot(pc.astype(BF16), dob)
            dk_ref[0, p] += _dot(dsp_b, qb)
            dv_ref[0, p] += _dot(pp.astype(BF16), dob)
            return carry

        lax.fori_loop(0, nblk, blk, 0)
        dsink_ref[0] = jnp.zeros((1, BLK), F32) - jnp.sum(sink_acc[...], axis=1, keepdims=True)

    smem = pl.BlockSpec(memory_space=pltpu.SMEM)
    whole = lambda shp: pl.BlockSpec((1,) + shp, lambda h: (h,) + (0,) * len(shp))
    td, dt, st = (nblk, BLK, HP), (nblk, HP, BLK), (nblk, 1, BLK)
    return pl.pallas_call(
        body, name=name, grid=(nh,),
        in_specs=[smem, smem, smem, whole(td), whole(td), whole(dt), whole(td), whole(td), whole(st), whole(st),
                  whole((2 * BLK, BLK))],
        out_specs=[whole(dt), whole(td), whole(td), whole((2 * BLK, BLK)), whole((1, BLK))],
        out_shape=[jax.ShapeDtypeStruct((nh,) + dt, F32), jax.ShapeDtypeStruct((nh,) + td, F32),
                   jax.ShapeDtypeStruct((nh,) + td, F32), jax.ShapeDtypeStruct((nh, 2 * BLK, BLK), F32),
                   jax.ShapeDtypeStruct((nh, 1, BLK), F32)],
        scratch_shapes=[pltpu.VMEM((1, BLK), F32)],
        compiler_params=_cparams(("parallel",)),
    )(nbs, lims, sinks, q, k, kt, v, do, lse, dpr, bias_t)


def _causal_mask(kj, qi):
    kidx = kj * TQ + lax.broadcasted_iota(jnp.int32, (TQ, TQ), 0)
    qidx = qi * TQ + lax.broadcasted_iota(jnp.int32, (TQ, TQ), 1)
    return kidx <= qidx


def _mla_fwd(q, k, vt, *, name):
    nh, n = q.shape[:2]
    scale = (C_NOPE + C_ROPE) ** -0.5

    def body(q_ref, k_ref, vt_ref, ot_ref, lse_ref, m_ref, l_ref, acc_ref):
        def qloop(qi, carry):
            qb = q_ref[0, qi]
            m_ref[...] = jnp.full_like(m_ref, NEG)
            l_ref[...] = jnp.zeros_like(l_ref)
            acc_ref[...] = jnp.zeros_like(acc_ref)

            def kloop(kj, c2):
                s = jnp.where(_causal_mask(kj, qi), _dot_nt(k_ref[0, kj], qb) * scale, NEG)
                m_prev = m_ref[...]
                m_new = jnp.maximum(m_prev, jnp.max(s, axis=0, keepdims=True))
                a = jnp.exp(m_prev - m_new)
                p = jnp.exp(s - m_new)
                l_ref[...] = a * l_ref[...] + jnp.sum(p, axis=0, keepdims=True)
                acc_ref[...] = a * acc_ref[...] + _dot(vt_ref[0, kj], p.astype(BF16))
                m_ref[...] = m_new
                return c2

            lax.fori_loop(0, qi + 1, kloop, 0)
            ot_ref[0, qi] = acc_ref[...] * (1.0 / l_ref[...])
            lse_ref[0, qi] = m_ref[...] + jnp.log(l_ref[...])
            return carry

        lax.fori_loop(0, n, qloop, 0)

    whole = lambda shp: pl.BlockSpec((1,) + shp, lambda h: (h,) + (0,) * len(shp))
    return pl.pallas_call(
        body, name=name, grid=(nh,),
        in_specs=[whole((n, TQ, HP)), whole((n, TQ, HP)), whole((n, HP, TQ))],
        out_specs=[whole((n, HP, TQ)), whole((n, 1, TQ))],
        out_shape=[jax.ShapeDtypeStruct((nh, n, HP, TQ), F32), jax.ShapeDtypeStruct((nh, n, 1, TQ), F32)],
        scratch_shapes=[pltpu.VMEM((1, TQ), F32), pltpu.VMEM((1, TQ), F32), pltpu.VMEM((HP, TQ), F32)],
        compiler_params=_cparams(("parallel",)),
    )(q, k, vt)


def _mla_bwd(q, k, kt, v, do, lse, delta, *, name):
    nh, n = q.shape[:2]
    scale = (C_NOPE + C_ROPE) ** -0.5

    def body(q_ref, k_ref, kt_ref, v_ref, do_ref, lse_ref, dl_ref, dqt_ref, dk_ref, dv_ref, dk_acc, dv_acc):
        dqt_ref[...] = jnp.zeros_like(dqt_ref)

        def kloop(kj, carry):
            kb, vb, ktb = k_ref[0, kj], v_ref[0, kj], kt_ref[0, kj]
            dk_acc[...] = jnp.zeros_like(dk_acc)
            dv_acc[...] = jnp.zeros_like(dv_acc)

            def qloop(qi, c2):
                qb, dob = q_ref[0, qi], do_ref[0, qi]
                s = jnp.where(_causal_mask(kj, qi), _dot_nt(kb, qb) * scale, NEG)
                p = jnp.exp(s - lse_ref[0, qi])
                ds = (p * (_dot_nt(vb, dob) - dl_ref[0, qi]) * scale).astype(BF16)
                dv_acc[...] += _dot(p.astype(BF16), dob)
                dk_acc[...] += _dot(ds, qb)
                dqt_ref[0, qi] += _dot(ktb, ds)
                return c2

            lax.fori_loop(kj, n, qloop, 0)
            dk_ref[0, kj] = dk_acc[...]
            dv_ref[0, kj] = dv_acc[...]
            return carry

        lax.fori_loop(0, n, kloop, 0)

    whole = lambda shp: pl.BlockSpec((1,) + shp, lambda h: (h,) + (0,) * len(shp))
    td, dt, st = (n, TQ, HP), (n, HP, TQ), (n, 1, TQ)
    return pl.pallas_call(
        body, name=name, grid=(nh,),
        in_specs=[whole(td), whole(td), whole(dt), whole(td), whole(td), whole(st), whole(st)],
        out_specs=[whole(dt), whole(td), whole(td)],
        out_shape=[jax.ShapeDtypeStruct((nh,) + dt, F32), jax.ShapeDtypeStruct((nh,) + td, F32),
                   jax.ShapeDtypeStruct((nh,) + td, F32)],
        scratch_shapes=[pltpu.VMEM((TQ, HP), F32), pltpu.VMEM((TQ, HP), F32)],
        compiler_params=_cparams(("parallel",)),
    )(q, k, kt, v, do, lse, delta)


def _bias_lookup(bucket, table_t, *, name):
    nh, npos = bucket.shape
    tp = 4096

    def body(b_ref, t_ref, o_ref):
        bk, tab = b_ref[...], t_ref[...]
        acc = jnp.zeros(bk.shape, F32)
        for i in range(REL_BUCKETS):
            acc = jnp.where(bk == i, tab[:, i:i + 1], acc)
        o_ref[...] = acc

    return pl.pallas_call(
        body, name=name, grid=(npos // tp,),
        in_specs=[pl.BlockSpec((nh, tp), lambda i: (0, i)), pl.BlockSpec((nh, REL_BUCKETS), lambda i: (0, 0))],
        out_specs=pl.BlockSpec((nh, tp), lambda i: (0, i)),
        out_shape=jax.ShapeDtypeStruct((nh, npos), F32),
        compiler_params=_cparams(("parallel",)),
    )(bucket, table_t)


def _bias_grad(bucket, ds0, ds1, *, name):
    nh, npos = bucket.shape
    tp = 4096

    def body(b_ref, a_ref, c_ref, o_ref):
        i = pl.program_id(0)
        bk, ds = b_ref[...], a_ref[...] + c_ref[...]
        lane = lax.broadcasted_iota(jnp.int32, (nh, REL_BUCKETS), 1)
        acc = jnp.zeros((nh, REL_BUCKETS), F32)
        for j in range(REL_BUCKETS):
            col = jnp.sum(jnp.where(bk == j, ds, 0.0), axis=1, keepdims=True)
            acc = acc + jnp.where(lane == j, col, 0.0)

        @pl.when(i == 0)
        def _():
            o_ref[...] = acc

        @pl.when(i > 0)
        def _():
            o_ref[...] += acc

    return pl.pallas_call(
        body, name=name, grid=(npos // tp,),
        in_specs=[pl.BlockSpec((nh, tp), lambda i: (0, i))] * 3,
        out_specs=pl.BlockSpec((nh, REL_BUCKETS), lambda i: (0, 0)),
        out_shape=jax.ShapeDtypeStruct((nh, REL_BUCKETS), F32),
        compiler_params=_cparams(("arbitrary",)),
    )(bucket, ds0, ds1)


def _t5_bucket(dist):
    n = jnp.maximum(dist, 0)
    max_exact = REL_BUCKETS // 2
    scaled = jnp.log(jnp.maximum(n, 1).astype(F32) / max_exact) / math.log(REL_MAX_DIST / max_exact)
    large = max_exact + (scaled * (REL_BUCKETS - max_exact)).astype(jnp.int32)
    return jnp.where(n < max_exact, n, jnp.minimum(large, REL_BUCKETS - 1))


def _bucket_index():
    ci = jnp.arange(2 * BLK)[:, None]
    qi = jnp.arange(BLK)[None, :]
    step = BLK + qi - ci
    per_group = [_t5_bucket(step * d).reshape(1, -1) for d in A_DILS + (1,)]
    return jnp.concatenate([jnp.tile(b, (N_HEADS, 1)) for b in per_group], axis=0).astype(jnp.int32)


def _sigmoid(x):
    return 1.0 / (1.0 + jnp.exp(-x))


def _ln_stats(z):
    mu = jnp.mean(z, axis=-1, keepdims=True)
    zc = z - mu
    var = jnp.mean(zc * zc, axis=-1, keepdims=True)
    return zc * lax.rsqrt(var + LN_EPS)


def _ln_fwd(x, mix, g, b, *, name):
    def fn(i, nt, xv, mv, gv, bv):
        z = ALPHA * xv + mv
        y = _ln_stats(z) * gv + bv
        return y, y, z

    c = x.shape[-1]
    y, yb, z = _rowwise(fn, [(x[None], c, 0), (mix[None], c, 0)], pars=[(g.reshape(1, 1, c), c, 0), (b.reshape(1, 1, c), c, 0)],
                        outs=[(c, c, 0, F32), (c, c, 0, BF16), (c, c, 0, F32)], tm=512, name=name)
    return y[0], yb[0], z[0]


def _ln_bwd(z, g, dys, coefs, *, name):
    n = len(dys)

    def fn(i, nt, zv, *rest):
        gv = rest[n]
        dy = coefs[0] * rest[0]
        for cf, t in zip(coefs[1:], rest[1:n]):
            dy = dy + cf * t
        mu = jnp.mean(zv, axis=-1, keepdims=True)
        zc = zv - mu
        r = lax.rsqrt(jnp.mean(zc * zc, axis=-1, keepdims=True) + LN_EPS)
        xh = zc * r
        dxh = dy * gv
        dz = r * (dxh - jnp.mean(dxh, axis=-1, keepdims=True) - xh * jnp.mean(dxh * xh, axis=-1, keepdims=True))
        return dz, dz, jnp.sum(dy * xh, axis=0, keepdims=True), jnp.sum(dy, axis=0, keepdims=True)

    c = z.shape[-1]
    dz, dzb, dg, db = _rowwise(fn, [(z[None], c, 0)] + [(d[None], c, 0) for d in dys], pars=[(g.reshape(1, 1, c), c, 0)],
                               outs=[(c, c, 0, F32), (c, c, 0, BF16)], accs=[(1, c, c, 0), (1, c, c, 0)], tm=512, name=name)
    return dz[0], dzb[0], dg.reshape(c), db.reshape(c)


def _rms_fwd(src, c, off, g, *, name):
    def fn(i, nt, xv, gv):
        return xv * lax.rsqrt(jnp.mean(xv * xv, axis=-1, keepdims=True) + RMS_EPS) * gv

    return _rowwise(fn, [(src[None], c, off)], pars=[(g.reshape(1, 1, c), c, 0)], outs=[(c, c, 0, BF16)], tm=1024, name=name)[0][0]


def _rms_bwd(src, c, off, g, dy, *, name):
    def fn(i, nt, xv, dyv, gv):
        r = lax.rsqrt(jnp.mean(xv * xv, axis=-1, keepdims=True) + RMS_EPS)
        gd = gv * dyv
        dx = gd * r - xv * (r * r * r) * jnp.mean(gd * xv, axis=-1, keepdims=True)
        return dx, jnp.sum(dyv * xv * r, axis=0, keepdims=True)

    dx, dg = _rowwise(fn, [(src[None], c, off), (dy[None], c, 0)], pars=[(g.reshape(1, 1, c), c, 0)],
                      outs=[(c, c, 0, BF16)], accs=[(1, c, c, 0)], tm=1024, name=name)
    return dx[0], dg.reshape(c)


def _rope(x1, x2, cos, sin, *, name):
    def fn(i, nt, a, b, c, s):
        return a * c - b * s, a * s + b * c

    w = x1.shape[-1]
    y1, y2 = _rowwise(fn, [(x1[None], w, 0), (x2[None], w, 0), (cos[None], w, 0), (sin[None], w, 0)],
                      outs=[(w, w, 0, F32), (w, w, 0, F32)], tm=1024, name=name)
    return y1[0], y2[0]


def _merge_fwd(proj, b_gate, ys, *, name):
    def fn(i, nt, g0, g1, g2, ya, yb, yc, bg):
        return (_sigmoid(g0 + bg[:, 0:1024]) * ya + _sigmoid(g1 + bg[:, 1024:2048]) * yb
                + _sigmoid(g2 + bg[:, 2048:3072]) * yc)

    rows = [(proj[None], 1024, j) for j in range(3)] + [(y[None], 1024, 0) for y in ys]
    return _rowwise(fn, rows, pars=[(b_gate.reshape(1, 1, 3072), 3072, 0)], outs=[(1024, 1024, 0, BF16)], tm=512, name=name)[0][0]


def _merge_bwd(proj, b_gate, ys, dm, *, name):
    def fn(i, nt, g0, g1, g2, ya, yb, yc, dmv, bg):
        outs, dgs = [], []
        for j, (gp, y) in enumerate(((g0, ya), (g1, yb), (g2, yc))):
            s = _sigmoid(gp + bg[:, j * 1024:(j + 1) * 1024])
            outs.append(s * dmv)
            dgs.append(dmv * y * s * (1.0 - s))
        return outs + dgs + [jnp.sum(d, axis=0, keepdims=True) for d in dgs]

    rows = [(proj[None], 1024, j) for j in range(3)] + [(y[None], 1024, 0) for y in ys] + [(dm[None], 1024, 0)]
    res = _rowwise(fn, rows, pars=[(b_gate.reshape(1, 1, 3072), 3072, 0)], outs=[(1024, 1024, 0, BF16)] * 6,
                   accs=[(1, 1024, 1024, 0)] * 3, tm=256, name=name)
    dys = [r[0] for r in res[0:3]]
    dgp = [r[0] for r in res[3:6]]
    dbg = jnp.concatenate([r.reshape(1024) for r in res[6:9]])
    return dys, dgp, dbg


def _shift_down(u, halo, i, k):
    ext = jnp.concatenate([jnp.where(i > 0, halo, 0.0), u], axis=0)
    return pltpu.roll(ext, k, axis=0)[8:]


def _shift_up(u, halo, i, nt, k):
    ext = jnp.concatenate([u, jnp.where(i < nt - 1, halo, 0.0)], axis=0)
    n = ext.shape[0]
    return pltpu.roll(ext, n - k, axis=0)[:n - 8]


GLU_C = D_FF // 2


def _conv(u, halo, i, w, b):
    return w[0:1] * _shift_down(u, halo, i, 2) + w[1:2] * _shift_down(u, halo, i, 1) + w[2:3] * u + b


def _glu_fwd(ug, uv, conv_w, conv_b, *, name):
    def fn(i, nt, g, v, hg, hv, wg, wv, bg, bv):
        cg, cv = _conv(g, hg, i, wg, bg), _conv(v, hv, i, wv, bv)
        return cg * _sigmoid(cg) * cv

    w3, b3 = conv_w[None], conv_b.reshape(1, 1, -1)
    c = GLU_C
    return _rowwise(fn, [(ug[None], c, 0), (uv[None], c, 0)], halos=[(ug[None], c, 0, "prev"), (uv[None], c, 0, "prev")],
                    pars=[(w3, c, 0), (w3, c, 2), (b3, c, 0), (b3, c, 2)], outs=[(D_FF, c, 0, BF16)], tm=256, ncol=2, name=name)[0][0]


def _glu_bwd_a(ug, uv, conv_w, conv_b, dh, *, name):
    def fn(i, nt, g, v, dhv, hg, hv, wg, wv, bg, bv):
        g1, g2 = _shift_down(g, hg, i, 1), _shift_down(g, hg, i, 2)
        v1, v2 = _shift_down(v, hv, i, 1), _shift_down(v, hv, i, 2)
        cg = wg[0:1] * g2 + wg[1:2] * g1 + wg[2:3] * g + bg
        cv = wv[0:1] * v2 + wv[1:2] * v1 + wv[2:3] * v + bv
        s = _sigmoid(cg)
        dcv = dhv * cg * s
        dcg = dhv * cv * (s * (1.0 + cg * (1.0 - s)))
        red = lambda a: jnp.sum(a, axis=0, keepdims=True)
        return (dcg, dcv, red(dcg), red(dcv), red(dcg * g2), red(dcg * g1), red(dcg * g),
                red(dcv * v2), red(dcv * v1), red(dcv * v))

    w3, b3 = conv_w[None], conv_b.reshape(1, 1, -1)
    c = GLU_C
    res = _rowwise(fn, [(ug[None], c, 0), (uv[None], c, 0), (dh[None], c, 0)],
                   halos=[(ug[None], c, 0, "prev"), (uv[None], c, 0, "prev")],
                   pars=[(w3, c, 0), (w3, c, 2), (b3, c, 0), (b3, c, 2)],
                   outs=[(D_FF, c, 0, F32), (D_FF, c, 0, F32)], accs=[(1, D_FF, c, 0)] * 8, tm=256, ncol=2, name=name)
    dcg, dcv = res[0][0], res[1][0]
    dconv_b = jnp.concatenate([res[2].reshape(D_FF), res[3].reshape(D_FF)])
    dconv_w = jnp.concatenate([jnp.concatenate([res[4 + j].reshape(1, D_FF) for j in range(3)], axis=0),
                               jnp.concatenate([res[7 + j].reshape(1, D_FF) for j in range(3)], axis=0)], axis=1)
    return dcg, dcv, dconv_w, dconv_b


def _glu_bwd_b(dc, conv_w, half, *, name):
    def fn(i, nt, d, hd, w):
        return w[2:3] * d + w[1:2] * _shift_up(d, hd, i, nt, 1) + w[0:1] * _shift_up(d, hd, i, nt, 2)

    c = GLU_C
    return _rowwise(fn, [(dc[None], c, 0)], halos=[(dc[None], c, 0, "next")], pars=[(conv_w[None], c, 2 * half)],
                    outs=[(D_FF, c, 0, BF16)], tm=256, ncol=2, name=name)[0][0]


def _loss_and_grad(y, tgt, *, name):
    def fn(i, nt, yv, tv):
        err = yv - tv
        part = jnp.sum(jnp.sum(err * err, axis=0, keepdims=True), axis=1, keepdims=True) * (0.5 / D_MODEL)
        return err * (1.0 / D_MODEL), jnp.zeros((1, LANE), F32) + part

    dy, part = _rowwise(fn, [(y[None], D_MODEL, 0), (tgt[None], D_MODEL, 0)], outs=[(D_MODEL, D_MODEL, 0, F32)],
                        accs=[(1, LANE, LANE, 0)], tm=512, name=name)
    return dy[0], part.reshape(LANE)


def _axpy(a, x, y, *, name):
    def fn(i, nt, xv, yv):
        return a * xv + yv

    c = x.shape[-1]
    return _rowwise(fn, [(x[None], c, 0), (y[None], c, 0)], outs=[(c, c, 0, F32)], tm=512, name=name)[0][0]


def _sum_rows(terms, *, tm, name, dtype=F32):
    def fn(i, nt, *vs):
        acc = vs[0]
        for v in vs[1:]:
            acc = acc + v
        return acc

    c = terms[0].shape[-1]
    return _rowwise(fn, [(t, c, 0) for t in terms], outs=[(c, c, 0, dtype)], tm=tm, name=name)[0]


def _combine_fwd(ots, lses, *, name):
    def fn(o0, o1, o2, l0, l1, l2):
        m = jnp.maximum(jnp.maximum(l0, l1), l2)
        e0, e1, e2 = jnp.exp(l0 - m), jnp.exp(l1 - m), jnp.exp(l2 - m)
        inv = 1.0 / (e0 + e1 + e2)
        return (e0 * inv) * o0 + (e1 * inv) * o1 + (e2 * inv) * o2

    return _lanewise(fn, list(ots) + list(lses), [(HEAD_DIM, F32)], tl=2048, name=name)[0]


def _combine_bwd(ots, lses, dot_a, *, name):
    def fn(o0, o1, o2, l0, l1, l2, da):
        m = jnp.maximum(jnp.maximum(l0, l1), l2)
        es = [jnp.exp(l - m) for l in (l0, l1, l2)]
        inv = 1.0 / (es[0] + es[1] + es[2])
        ws = [e * inv for e in es]
        dws = [jnp.sum(da * o, axis=0, keepdims=True) for o in (o0, o1, o2)]
        mean = ws[0] * dws[0] + ws[1] * dws[1] + ws[2] * dws[2]
        dos = [w * da for w in ws]
        dprs = [w * dw - w * (dw - mean) for w, dw in zip(ws, dws)]
        return dos + dprs

    res = _lanewise(fn, list(ots) + list(lses) + [dot_a], [(HEAD_DIM, BF16)] * 3 + [(1, F32)] * 3, tl=2048, name=name)
    return res[0:3], res[3:6]


def _rowdot(at, bt, *, name):
    def fn(a, b):
        return jnp.sum(a * b, axis=0, keepdims=True)

    return _lanewise(fn, [at, bt], [(1, F32)], tl=2048, name=name)[0]


def _adamw(w, g, m, v, *, name):
    c1 = 1.0 - ADAM_B1 ** ADAM_STEP
    c2 = 1.0 - ADAM_B2 ** ADAM_STEP

    def fn(i, nt, wv, gv, mv, vv):
        mn = ADAM_B1 * mv + (1.0 - ADAM_B1) * gv
        vn = ADAM_B2 * vv + (1.0 - ADAM_B2) * (gv * gv)
        delta = -ADAM_LR * ((mn / c1) / (jnp.sqrt(vn / c2) + ADAM_EPS) + ADAM_WD * wv)
        return delta, mn, vn

    r, c = w.shape
    rp = _ceil_to(r, 8)
    pad = lambda a: jnp.pad(a, ((0, rp - r), (0, 0))) if rp != r else a
    tm = rp
    for cand in (128, 64, 32, 16, 8):
        if rp % cand == 0:
            tm = cand
            break
    res = _rowwise(fn, [(pad(a)[None], c, 0) for a in (w, g, m, v)], outs=[(c, c, 0, F32)] * 3, tm=tm, name=name)
    return [x[0][:r] for x in res]


ANY = pl.BlockSpec(memory_space=pl.ANY)


def _place():
    x, y, c = lax.axis_index("x"), lax.axis_index("y"), lax.axis_index("c")
    chips = [(1 - x, y), (x, 1 - y), (1 - x, 1 - y)]
    return x, y, c, chips


def _allgather_weights(wp):
    _, rh, w = wp.shape

    def body(w_ref, out_ref, send_sems, recv_sems, local_sem):
        x, y, c, chips = _place()
        j = 2 * x + y

        def cp(k, src, chip_idx, half, to):
            return pltpu.make_async_remote_copy(src_ref=src, dst_ref=out_ref.at[chip_idx, half], send_sem=send_sems.at[k],
                                                recv_sem=recv_sems.at[k], device_id=to, device_id_type=MESH)

        mine = pltpu.make_async_copy(w_ref, out_ref.at[j], local_sem)
        mine.start()
        first = [cp(r, w_ref.at[c], j, c, (cx, cy, c)) for r, (cx, cy) in enumerate(chips)]
        for d in first:
            d.start()
        passed = [cp(3 + r, out_ref.at[2 * cx + cy, c], 2 * cx + cy, c, (x, y, 1 - c)) for r, (cx, cy) in enumerate(chips)]
        for r, (cx, cy) in enumerate(chips):
            cp(r, w_ref.at[c], 2 * cx + cy, c, (x, y, c)).wait_recv()
            passed[r].start()
        for r, (cx, cy) in enumerate(chips):
            cp(3 + r, w_ref.at[c], 2 * cx + cy, 1 - c, (x, y, c)).wait_recv()
        for d in first + passed:
            d.wait_send()
        mine.wait()

    return pl.pallas_call(
        body, name="allgather_weights", in_specs=[ANY], out_specs=ANY,
        out_shape=jax.ShapeDtypeStruct((N_CHIP, 2, rh, w), wp.dtype),
        scratch_shapes=[pltpu.SemaphoreType.DMA((6,)), pltpu.SemaphoreType.DMA((6,)), pltpu.SemaphoreType.DMA],
    )(wp)


def _sibling_swap(g):
    n, _, rh, w = g.shape

    def body(g_ref, mine_ref, got_ref, send_sems, recv_sems, local_sem):
        x, y, c, _ = _place()
        mine = [pltpu.make_async_copy(g_ref.at[k, c], mine_ref.at[k], local_sem.at[k]) for k in range(n)]
        cps = [pltpu.make_async_remote_copy(src_ref=g_ref.at[k, 1 - c], dst_ref=got_ref.at[k], send_sem=send_sems.at[k],
                                            recv_sem=recv_sems.at[k], device_id=(x, y, 1 - c), device_id_type=MESH)
               for k in range(n)]
        for d in mine + cps:
            d.start()
        for d in cps:
            d.wait_recv()
        for d in cps:
            d.wait_send()
        for d in mine:
            d.wait()

    return pl.pallas_call(
        body, name="grad_sibling_swap", in_specs=[ANY], out_specs=[ANY, ANY],
        out_shape=[jax.ShapeDtypeStruct((n, rh, w), g.dtype)] * 2,
        scratch_shapes=[pltpu.SemaphoreType.DMA((n,)), pltpu.SemaphoreType.DMA((n,)), pltpu.SemaphoreType.DMA((n,))],
    )(g)


def _chip_scatter(p):
    n, rh, w = p.shape

    def body(p_ref, out_ref, send_sems, recv_sems, local_sem):
        x, y, c, chips = _place()
        j = 2 * x + y
        mine = pltpu.make_async_copy(p_ref.at[j], out_ref.at[j], local_sem)
        mine.start()

        def cp(r, src_idx, dst_idx, to):
            return pltpu.make_async_remote_copy(src_ref=p_ref.at[src_idx], dst_ref=out_ref.at[dst_idx], send_sem=send_sems.at[r],
                                                recv_sem=recv_sems.at[r], device_id=to, device_id_type=MESH)

        sends = [cp(r, 2 * cx + cy, j, (cx, cy, c)) for r, (cx, cy) in enumerate(chips)]
        for d in sends:
            d.start()
        for r, (cx, cy) in enumerate(chips):
            cp(r, j, 2 * cx + cy, (x, y, c)).wait_recv()
        for d in sends:
            d.wait_send()
        mine.wait()

    return pl.pallas_call(
        body, name="grad_chip_scatter", in_specs=[ANY], out_specs=ANY,
        out_shape=jax.ShapeDtypeStruct((n, rh, w), p.dtype),
        scratch_shapes=[pltpu.SemaphoreType.DMA((3,)), pltpu.SemaphoreType.DMA((3,)), pltpu.SemaphoreType.DMA],
    )(p)


def _sibling_share(r):
    rh, w = r.shape

    def body(r_ref, out_ref, send_sem, recv_sem, local_sem):
        x, y, c, _ = _place()
        mine = pltpu.make_async_copy(r_ref, out_ref.at[c], local_sem)
        mine.start()
        send = pltpu.make_async_remote_copy(src_ref=r_ref, dst_ref=out_ref.at[c], send_sem=send_sem, recv_sem=recv_sem,
                                            device_id=(x, y, 1 - c), device_id_type=MESH)
        send.start()
        pltpu.make_async_remote_copy(src_ref=r_ref, dst_ref=out_ref.at[1 - c], send_sem=send_sem, recv_sem=recv_sem,
                                     device_id=(x, y, c), device_id_type=MESH).wait_recv()
        send.wait_send()
        mine.wait()

    return pl.pallas_call(
        body, name="grad_sibling_share", in_specs=[ANY], out_specs=ANY,
        out_shape=jax.ShapeDtypeStruct((2, rh, w), r.dtype),
        scratch_shapes=[pltpu.SemaphoreType.DMA, pltpu.SemaphoreType.DMA, pltpu.SemaphoreType.DMA],
    )(r)


def _allreduce_small(s):
    rows, w = s.shape
    n_dev = 8

    def body(s_ref, out_ref, slots, send_sems, recv_sems):
        x, y, c, _ = _place()
        me = 4 * x + 2 * y + c
        slots[me] = s_ref[...]
        peers = []
        for r in range(1, n_dev):
            px = 1 - x if r & 4 else x
            py = 1 - y if r & 2 else y
            pc = 1 - c if r & 1 else c
            peers.append((px, py, pc))
        sends = [pltpu.make_async_remote_copy(src_ref=s_ref, dst_ref=slots.at[me], send_sem=send_sems.at[r], recv_sem=recv_sems.at[r],
                                              device_id=peer, device_id_type=MESH) for r, peer in enumerate(peers)]
        for d in sends:
            d.start()
        for r, (px, py, pc) in enumerate(peers):
            pltpu.make_async_remote_copy(src_ref=s_ref, dst_ref=slots.at[4 * px + 2 * py + pc], send_sem=send_sems.at[r],
                                         recv_sem=recv_sems.at[r], device_id=(x, y, c), device_id_type=MESH).wait_recv()
        for d in sends:
            d.wait_send()
        acc = slots[0]
        for k in range(1, n_dev):
            acc = acc + slots[k]
        out_ref[...] = acc

    vm = pl.BlockSpec(memory_space=pltpu.VMEM)
    return pl.pallas_call(
        body, name="allreduce_small", in_specs=[vm], out_specs=vm, out_shape=jax.ShapeDtypeStruct((rows, w), F32),
        scratch_shapes=[pltpu.VMEM((n_dev, rows, w), F32), pltpu.SemaphoreType.DMA((n_dev - 1,)), pltpu.SemaphoreType.DMA((n_dev - 1,))],
    )(s)


PACK_W = 1024
PACK_TILE = 512


def _pack_weight_shards(shards):
    parts = []
    for l in range(DEPTH):
        for name, _, _ in MATS:
            a = shards[name][l]
            if name == "conv_w":
                parts.append(lax.bitcast_convert_type(a, BF16).reshape(-1))
            else:
                parts.append(a.astype(BF16).reshape(-1))
    flat = jnp.concatenate(parts)
    rows = _ceil_to(-(-flat.shape[0] // PACK_W), 2 * PACK_TILE)
    flat = jnp.pad(flat, (0, rows * PACK_W - flat.shape[0]))
    return flat.reshape(2, rows // 2, PACK_W)


def _unpack_gathered(gathered):
    out, off = [], 0
    for l in range(DEPTH):
        d = {}
        for name, shape, ax in MATS:
            ss = _shard_shape(shape, ax)
            n = math.prod(ss) * (2 if name == "conv_w" else 1)
            seg = gathered[:, off:off + n]
            off += n
            if name == "conv_w":
                seg = lax.bitcast_convert_type(seg.reshape((N_CHIP,) + ss + (2,)), F32)
            else:
                seg = seg.reshape((N_CHIP,) + ss)
            d[name] = jnp.moveaxis(seg, 0, ax).reshape(shape)
        out.append(d)
    return out


def _pack_grads(grads):
    parts = []
    for l in range(DEPTH):
        for name, shape, ax in MATS:
            g = grads[l][name]
            split = shape[:ax] + (N_CHIP, shape[ax] // N_CHIP) + shape[ax + 1:]
            parts.append(jnp.moveaxis(g.reshape(split), ax, 0).reshape(N_CHIP, -1))
    flat = jnp.concatenate(parts, axis=1)
    rows = _ceil_to(-(-flat.shape[1] // PACK_W), 2 * PACK_TILE)
    flat = jnp.pad(flat, ((0, 0), (0, rows * PACK_W - flat.shape[1])))
    return flat.reshape(N_CHIP, 2, rows // 2, PACK_W)


def _unpack_grad_shard(flat):
    out, off = {name: [] for name, _, _ in MATS}, 0
    for l in range(DEPTH):
        for name, shape, ax in MATS:
            ss = _shard_shape(shape, ax)
            n = math.prod(ss)
            out[name].append(flat[off:off + n].reshape(ss))
            off += n
    return {k: jnp.stack(v) for k, v in out.items()}


def _pack_small(rel, small, extra):
    parts = [rel.reshape(-1)]
    for l in range(DEPTH):
        for name in SMALL:
            parts.append(small[name][l].reshape(-1))
    parts.append(extra)
    flat = jnp.concatenate(parts)
    rows = _ceil_to(-(-flat.shape[0] // LANE), 8)
    return jnp.pad(flat, (0, rows * LANE - flat.shape[0])).reshape(rows, LANE)


def _unpack_small(buf):
    flat = buf.reshape(-1)
    rel = flat[:REL_BUCKETS * 32].reshape(REL_BUCKETS, 32)
    off = REL_BUCKETS * 32
    small = {name: [] for name in SMALL}
    for l in range(DEPTH):
        for name in SMALL:
            n = SMALL_SIZES[name]
            small[name].append(flat[off:off + n])
            off += n
    return rel, {k: jnp.stack(v) for k, v in small.items()}, flat[off:off + LANE]


def _perm(a, d):
    if d == 1:
        return a
    t = a.shape[0]
    return jnp.swapaxes(a.reshape((t // d, d) + a.shape[1:]), 0, 1).reshape(a.shape)


def _unperm(a, d):
    if d == 1:
        return a
    t = a.shape[0]
    return jnp.swapaxes(a.reshape((d, t // d) + a.shape[1:]), 0, 1).reshape(a.shape)


def _pad_lanes(a, w=HP):
    return jnp.pad(a, [(0, 0)] * (a.ndim - 1) + [(0, w - a.shape[-1])])


def _heads_blocks(a, blk):
    t, h, _ = a.shape
    return jnp.transpose(_pad_lanes(a), (1, 0, 2)).astype(BF16).reshape(h, t // blk, blk, HP)


def _heads_blocks_t(a, blk):
    return jnp.swapaxes(_heads_blocks(a, blk), -1, -2)


def _from_blocks_t(a):
    h, n, d, blk = a.shape
    return jnp.transpose(a, (1, 3, 0, 2)).reshape(n * blk, h, d)


def _from_blocks(a):
    h, n, blk, d = a.shape
    return jnp.transpose(a, (1, 2, 0, 3)).reshape(n * blk, h, d)


def _to_hdt(a):
    return jnp.transpose(a, (1, 2, 0))


def _rope_tables(t):
    pos = jnp.arange(t, dtype=F32)
    inv_freq = ROPE_BASE ** (-jnp.arange(0, C_ROPE, 2, dtype=F32) / C_ROPE)
    ang = pos[:, None] * inv_freq[None, :]
    cos, sin = jnp.cos(ang), jnp.sin(ang)
    half = C_ROPE // 2
    wide = lambda a: jnp.concatenate([jnp.tile(a, (1, N_HEADS)), a, jnp.zeros((t, 2 * LANE - (N_HEADS + 1) * half), F32)], axis=1)
    return wide(cos), wide(sin)


def _rope_pack(q_part, k_part):
    t = k_part.shape[0]
    return jnp.concatenate([q_part.reshape(t, -1), k_part, jnp.zeros((t, 2 * LANE - 9 * (C_ROPE // 2)), F32)], axis=1)


def _band_params(t, sinks):
    nbs = jnp.array(sum([[t // (d * BLK)] * N_HEADS for d in A_DILS + (1,)], []), jnp.int32)
    lims = jnp.array([BLK] * (3 * N_HEADS) + [BLK - 1] * N_HEADS, jnp.int32)
    sink_all = jnp.concatenate([jnp.full((3 * N_HEADS,), NEG, F32), sinks.astype(F32)])
    return nbs, lims, sink_all


def _layer_fwd(l, x, xb, w, p, bias_t, rope_cs):
    t = x.shape[0]
    nblk = t // BLK
    n = f"l{l}_"
    s = {"xb": xb}
    proj = _mm(xb, w["win_p"], tn=1280, name=n + "proj")
    s["proj"] = proj

    a = proj[:, COL_A:COL_BQ].reshape(t, 3, 3, N_HEADS, HEAD_DIM)
    qs, ks, vs = [], [], []
    for gi, d in enumerate(A_DILS):
        ag = _perm(a[:, gi], d)
        qs.append(ag[:, 0]); ks.append(ag[:, 1]); vs.append(ag[:, 2])
    bq = proj[:, COL_BQ:COL_BKV].reshape(t, N_HEADS, HEAD_DIM)
    bkv = proj[:, COL_BKV:COL_CQ].reshape(t, 2, 2, HEAD_DIM)
    qs.append(bq); ks.append(jnp.repeat(bkv[:, 0], 4, axis=1)); vs.append(jnp.repeat(bkv[:, 1], 4, axis=1))
    q_all = _heads_blocks(jnp.concatenate(qs, axis=1), BLK)
    k_cat, v_cat = jnp.concatenate(ks, axis=1), jnp.concatenate(vs, axis=1)
    k_all, v_all = _heads_blocks(k_cat, BLK), _heads_blocks(v_cat, BLK)
    kt_all, vt_all = jnp.swapaxes(k_all, -1, -2), jnp.swapaxes(v_all, -1, -2)
    nbs, lims, sink_all = _band_params(t, p["sinks"])
    ot_all, lse_all = _band_fwd(q_all, k_all, vt_all, bias_t, nbs, lims, sink_all, name=n + "band_fwd")
    s.update(q_all=q_all, k_all=k_all, kt_all=kt_all, v_all=v_all, lse_all=lse_all, band=(nbs, lims, sink_all))
    o_perm = _from_blocks_t(ot_all)[:, :, :HEAD_DIM]
    lse_perm = lse_all.reshape(4 * N_HEADS, t).T
    ots = [_to_hdt(_unperm(o_perm[:, gi * 8:(gi + 1) * 8], d)) for gi, d in enumerate(A_DILS)]
    lses = [_unperm(lse_perm[:, gi * 8:(gi + 1) * 8], d).T[:, None, :] for gi, d in enumerate(A_DILS)]
    ot_a = _combine_fwd(ots, lses, name=n + "combine_fwd")
    s.update(ots=ots, lses=lses)
    o_a = jnp.transpose(ot_a, (2, 0, 1)).reshape(t, 512).astype(BF16)
    o_b = o_perm[:, 24:32].reshape(t, 512).astype(BF16)
    s["ot_b"] = _to_hdt(o_perm[:, 24:32])

    rq = _rms_fwd(proj, C_Q_RANK, COL_CQ // C_Q_RANK, p["q_norm_g"], name=n + "rms_q")
    rkv = _rms_fwd(proj, C_KV_RANK, COL_CDKV // C_KV_RANK, p["kv_norm_g"], name=n + "rms_kv")
    q_c = _mm(rq, w["w_uq"], name=n + "uq").reshape(t, N_HEADS, C_NOPE + C_ROPE)
    kv_c = _mm(rkv, w["w_ukv"], name=n + "ukv").reshape(t, N_HEADS, 2 * C_NOPE)
    k_rope = proj[:, COL_CDKV + C_KV_RANK:COL_CDKV + C_KV_RANK + C_ROPE]
    hr = C_ROPE // 2
    x1 = _rope_pack(q_c[:, :, C_NOPE:C_NOPE + hr], k_rope[:, :hr])
    x2 = _rope_pack(q_c[:, :, C_NOPE + hr:], k_rope[:, hr:])
    y1, y2 = _rope(x1, x2, rope_cs[0], rope_cs[1], name=n + "rope")
    qy1, qy2 = y1[:, :LANE].reshape(t, N_HEADS, hr), y2[:, :LANE].reshape(t, N_HEADS, hr)
    ky = jnp.concatenate([y1[:, LANE:LANE + hr], y2[:, LANE:LANE + hr]], axis=1)
    q_full = jnp.concatenate([q_c[:, :, :C_NOPE], qy1, qy2], axis=2)
    k_full = jnp.concatenate([kv_c[:, :, :C_NOPE], jnp.broadcast_to(ky[:, None, :], (t, N_HEADS, C_ROPE))], axis=2)
    qm, km, vm = _heads_blocks(q_full, TQ), _heads_blocks(k_full, TQ), _heads_blocks(kv_c[:, :, C_NOPE:], TQ)
    ot_c, lse_c = _mla_fwd(qm, km, jnp.swapaxes(vm, -1, -2), name=n + "mla_fwd")
    o_c3 = _from_blocks_t(ot_c)[:, :, :HEAD_DIM]
    o_c = o_c3.reshape(t, 512).astype(BF16)
    s.update(rq=rq, rkv=rkv, qm=qm, km=km, vm=vm, lse_c=lse_c, ot_c=_to_hdt(o_c3))

    obs = [o_a, o_b, o_c]
    ys = [_mm(o, w["w_branch"][i], name=n + f"branch{i}") for i, o in enumerate(obs)]
    merged = _merge_fwd(proj, p["b_gate"], ys, name=n + "merge")
    mix = _mm(merged, w["w_out"], name=n + "out")
    x1f, x1b, z1 = _ln_fwd(x, mix, p["ln1_g"], p["ln1_b"], name=n + "ln1")
    s.update(obs=obs, ys=ys, merged=merged, z1=z1, x1b=x1b)

    ug = _mm(x1b, w["wup_g"], tn=1408, name=n + "up_g")
    uv = _mm(x1b, w["wup_v"], tn=1408, name=n + "up_v")
    h = _glu_fwd(ug, uv, w["conv_w"], p["conv_b"], name=n + "glu")
    ff = _mm(h, w["w_ffn_down"], tk=1408, name=n + "down")
    x2f, x2b, z2 = _ln_fwd(x1f, ff, p["ln2_g"], p["ln2_b"], name=n + "ln2")
    s.update(ug=ug, uv=uv, h=h, z2=z2)
    return x2f, x2b, s


def _layer_bwd(l, s, dys, coefs, w, p, bias_t, rope_cs):
    n = f"l{l}b_"
    t = s["z2"].shape[0]
    gw, gs = {}, {}
    tr = lambda a: a.T

    dz2, dz2b, gs["ln2_g"], gs["ln2_b"] = _ln_bwd(s["z2"], p["ln2_g"], dys, coefs, name=n + "ln2")
    dh = _mm(dz2b, w["w_ffn_down"], tb=True, tn=1408, name=n + "d_h")
    gw["w_ffn_down"] = _mm(tr(s["h"]), dz2b, tm=704, name=n + "g_down")
    dcg, dcv, gw["conv_w"], gs["conv_b"] = _glu_bwd_a(s["ug"], s["uv"], w["conv_w"], p["conv_b"], dh, name=n + "glu_a")
    dug = _glu_bwd_b(dcg, w["conv_w"], 0, name=n + "glu_bg")
    duv = _glu_bwd_b(dcv, w["conv_w"], 1, name=n + "glu_bv")
    dx1_g = _mm(dug, w["wup_g"], tb=True, tk=1408, name=n + "d_x1g")
    dx1_v = _mm(duv, w["wup_v"], tb=True, tk=1408, name=n + "d_x1v")
    x1t = tr(s["x1b"])
    gw["w_ffn_up"] = jnp.concatenate([_mm(x1t, dug, tn=1408, name=n + "g_upg"), _mm(x1t, duv, tn=1408, name=n + "g_upv")], axis=1)

    dz1, dz1b, gs["ln1_g"], gs["ln1_b"] = _ln_bwd(s["z1"], p["ln1_g"], [dz2, dx1_g, dx1_v], [ALPHA, 1.0, 1.0], name=n + "ln1")
    dmerged = _mm(dz1b, w["w_out"], tb=True, name=n + "d_merged")
    gw["w_out"] = _mm(tr(s["merged"]), dz1b, name=n + "g_out")
    dys_b, dgp, gs["b_gate"] = _merge_bwd(s["proj"], p["b_gate"], s["ys"], dmerged, name=n + "merge")
    dos = [_mm(dy, w["w_branch"][i], tb=True, name=n + f"d_o{i}") for i, dy in enumerate(dys_b)]
    gw["w_branch"] = jnp.stack([_mm(tr(o), dy, name=n + f"g_branch{i}") for i, (o, dy) in enumerate(zip(s["obs"], dys_b))])

    do_a3 = dos[0].reshape(t, N_HEADS, HEAD_DIM)
    dots, dprs = _combine_bwd(s["ots"], s["lses"], _to_hdt(do_a3), name=n + "combine")
    do_b3 = dos[1].reshape(t, N_HEADS, HEAD_DIM)
    dpr_b = _rowdot(_to_hdt(do_b3), s["ot_b"], name=n + "delta_b")
    do_list = [_perm(jnp.transpose(dt_, (2, 0, 1)), d) for dt_, d in zip(dots, A_DILS)] + [do_b3]
    dpr_list = [_perm(dp_[:, 0, :].T, d) for dp_, d in zip(dprs, A_DILS)] + [dpr_b[:, 0, :].T]
    do_all = _heads_blocks(jnp.concatenate(do_list, axis=1), BLK)
    nblk = t // BLK
    dpr_all = jnp.concatenate(dpr_list, axis=1).T.reshape(4 * N_HEADS, nblk, 1, BLK)
    nbs, lims, sink_all = s["band"]
    dqt, dk, dv, ds_sum, dsink = _band_bwd(s["q_all"], s["k_all"], s["kt_all"], s["v_all"], do_all, s["lse_all"], dpr_all,
                                           bias_t, nbs, lims, sink_all, name=n + "band")
    gs["sinks"] = dsink[3 * N_HEADS:, 0, 0]
    dq_p = _from_blocks_t(dqt)[:, :, :HEAD_DIM]
    dk_p, dv_p = _from_blocks(dk)[:, :, :HEAD_DIM], _from_blocks(dv)[:, :, :HEAD_DIM]
    groups = []
    for gi, d in enumerate(A_DILS):
        sl = slice(gi * 8, (gi + 1) * 8)
        groups.append(jnp.stack([_unperm(dq_p[:, sl], d), _unperm(dk_p[:, sl], d), _unperm(dv_p[:, sl], d)], axis=1))
    da_qkv = jnp.stack(groups, axis=1).reshape(t, 9 * 512).astype(BF16)
    dbq = dq_p[:, 24:32].reshape(t, 512).astype(BF16)
    kv_terms = []
    for part in (dk_p, dv_p):
        hm = jnp.transpose(part[:, 24:32], (1, 0, 2)).reshape(2, 4, t, HEAD_DIM)
        kv_terms.append([hm[:, i] for i in range(4)])
    dkb = _sum_rows(kv_terms[0], tm=1024, name=n + "gqa_dk", dtype=BF16)
    dvb = _sum_rows(kv_terms[1], tm=1024, name=n + "gqa_dv", dtype=BF16)
    dbkv = jnp.transpose(jnp.stack([dkb, dvb]), (2, 0, 1, 3)).reshape(t, 256)

    do_c3 = dos[2].reshape(t, N_HEADS, HEAD_DIM)
    delta_c = _rowdot(_to_hdt(do_c3), s["ot_c"], name=n + "delta_c").reshape(N_HEADS, t // TQ, 1, TQ)
    dqt_c, dk_c, dv_c = _mla_bwd(s["qm"], s["km"], jnp.swapaxes(s["km"], -1, -2), s["vm"], _heads_blocks(do_c3, TQ),
                                 s["lse_c"], delta_c, name=n + "mla")
    dq_full, dk_full, dv_full = _from_blocks_t(dqt_c), _from_blocks(dk_c), _from_blocks(dv_c)
    hr = C_ROPE // 2
    dk_sum = _sum_rows([dk_full[None, :, hh, :] for hh in range(N_HEADS)], tm=1024, name=n + "krope_sum")[0]
    dy1 = _rope_pack(dq_full[:, :, C_NOPE:C_NOPE + hr], dk_sum[:, C_NOPE:C_NOPE + hr])
    dy2 = _rope_pack(dq_full[:, :, C_NOPE + hr:C_NOPE + C_ROPE], dk_sum[:, C_NOPE + hr:C_NOPE + C_ROPE])
    dx1, dx2 = _rope(dy1, dy2, rope_cs[0], -rope_cs[1], name=n + "rope")
    dq_c = jnp.concatenate([dq_full[:, :, :C_NOPE], dx1[:, :LANE].reshape(t, N_HEADS, hr), dx2[:, :LANE].reshape(t, N_HEADS, hr)],
                           axis=2).reshape(t, 768).astype(BF16)
    dkv_c = jnp.concatenate([dk_full[:, :, :C_NOPE], dv_full[:, :, :C_NOPE]], axis=2).reshape(t, 1024).astype(BF16)
    d_rq = _mm(dq_c, w["w_uq"], tb=True, name=n + "d_rq")
    d_rkv = _mm(dkv_c, w["w_ukv"], tb=True, name=n + "d_rkv")
    gw["w_uq"] = _mm(tr(s["rq"]), dq_c, name=n + "g_uq")
    gw["w_ukv"] = _mm(tr(s["rkv"]), dkv_c, name=n + "g_ukv")
    dcq, gs["q_norm_g"] = _rms_bwd(s["proj"], C_Q_RANK, COL_CQ // C_Q_RANK, p["q_norm_g"], d_rq, name=n + "rms_q")
    dckv, gs["kv_norm_g"] = _rms_bwd(s["proj"], C_KV_RANK, COL_CDKV // C_KV_RANK, p["kv_norm_g"], d_rkv, name=n + "rms_kv")
    dcdkv = jnp.concatenate([dckv, dx1[:, LANE:LANE + hr].astype(BF16), dx2[:, LANE:LANE + hr].astype(BF16),
                             jnp.zeros((t, D_IN_P - D_IN), BF16)], axis=1)

    dproj = jnp.concatenate(dgp + [da_qkv, dbq, dbkv, dcq, dcdkv], axis=1)
    dx_proj = _mm(dproj, w["win_p"], tb=True, tk=1280, name=n + "d_x")
    gwin_p = _mm(tr(s["xb"]), dproj, tn=1280, name=n + "g_in")
    gw["w_in"] = jnp.concatenate([gwin_p[:, COL_A:D_IN], gwin_p[:, :COL_A]], axis=1)
    return [dz1, dx_proj], [ALPHA, 1.0], gw, gs, ds_sum


def _prep_weights(full):
    w_in = full["w_in"]
    win_p = jnp.concatenate([w_in[:, ORIG_GATE:], w_in[:, :ORIG_GATE], jnp.zeros((D_MODEL, D_IN_P - D_IN), BF16)], axis=1)
    return {"win_p": win_p, "w_uq": full["w_uq"], "w_ukv": full["w_ukv"], "w_branch": full["w_branch"], "w_out": full["w_out"],
            "wup_g": full["w_ffn_up"][:, :D_FF], "wup_v": full["w_ffn_up"][:, D_FF:], "conv_w": full["conv_w"],
            "w_ffn_down": full["w_ffn_down"]}


def _local_step(x, target, full_weights, rel_table, small):
    t = x.shape[0]
    ws = [_prep_weights(f) for f in full_weights]
    ps = [{k: small[k][l] for k in SMALL} for l in range(DEPTH)]
    bucket = _bucket_index()
    bias_t = _bias_lookup(bucket, rel_table.T, name="bias_lookup").reshape(4 * N_HEADS, 2 * BLK, BLK)
    rope_cs = _rope_tables(t)

    saved, h, hb = [], x, x.astype(BF16)
    for l in range(DEPTH):
        h, hb, s = _layer_fwd(l, h, hb, ws[l], ps[l], bias_t, rope_cs)
        saved.append(s)
    dy, loss_part = _loss_and_grad(h, target, name="loss")

    dys, coefs = [dy], [1.0]
    gws, gss, dss = [None] * DEPTH, [None] * DEPTH, [None] * DEPTH
    for l in reversed(range(DEPTH)):
        dys, coefs, gws[l], gss[l], dss[l] = _layer_bwd(l, saved[l], dys, coefs, ws[l], ps[l], bias_t, rope_cs)
    grad_x = _axpy(coefs[0], dys[0], dys[1], name="grad_x")
    npos = 2 * BLK * BLK
    g_rel = _bias_grad(bucket, dss[0].reshape(4 * N_HEADS, npos), dss[1].reshape(4 * N_HEADS, npos), name="bias_grad").T
    gsmall = {k: jnp.stack([gss[l][k] for l in range(DEPTH)]) for k in SMALL}
    return loss_part, grad_x, gws, gsmall, g_rel


def kernel(x, rel_table, w_in, b_gate, sinks, q_norm_g, kv_norm_g, w_uq, w_ukv, w_branch, w_out, ln1_g, ln1_b, w_ffn_up, conv_w, conv_b, w_ffn_down, ln2_g, ln2_b, loss_target, m_rel_table, m_w_in, m_b_gate, m_sinks, m_q_norm_g, m_kv_norm_g, m_w_uq, m_w_ukv, m_w_branch, m_w_out, m_ln1_g, m_ln1_b, m_w_ffn_up, m_conv_w, m_conv_b, m_w_ffn_down, m_ln2_g, m_ln2_b, v_rel_table, v_w_in, v_b_gate, v_sinks, v_q_norm_g, v_kv_norm_g, v_w_uq, v_w_ukv, v_w_branch, v_w_out, v_ln1_g, v_ln1_b, v_w_ffn_up, v_conv_w, v_conv_b, v_w_ffn_down, v_ln2_g, v_ln2_b):
    wts = dict(rel_table=rel_table, w_in=w_in, b_gate=b_gate, sinks=sinks, q_norm_g=q_norm_g, kv_norm_g=kv_norm_g, w_uq=w_uq,
               w_ukv=w_ukv, w_branch=w_branch, w_out=w_out, ln1_g=ln1_g, ln1_b=ln1_b, w_ffn_up=w_ffn_up, conv_w=conv_w,
               conv_b=conv_b, w_ffn_down=w_ffn_down, ln2_g=ln2_g, ln2_b=ln2_b)
    ms = dict(rel_table=m_rel_table, w_in=m_w_in, b_gate=m_b_gate, sinks=m_sinks, q_norm_g=m_q_norm_g, kv_norm_g=m_kv_norm_g,
              w_uq=m_w_uq, w_ukv=m_w_ukv, w_branch=m_w_branch, w_out=m_w_out, ln1_g=m_ln1_g, ln1_b=m_ln1_b, w_ffn_up=m_w_ffn_up,
              conv_w=m_conv_w, conv_b=m_conv_b, w_ffn_down=m_w_ffn_down, ln2_g=m_ln2_g, ln2_b=m_ln2_b)
    vs = dict(rel_table=v_rel_table, w_in=v_w_in, b_gate=v_b_gate, sinks=v_sinks, q_norm_g=v_q_norm_g, kv_norm_g=v_kv_norm_g,
              w_uq=v_w_uq, w_ukv=v_w_ukv, w_branch=v_w_branch, w_out=v_w_out, ln1_g=v_ln1_g, ln1_b=v_ln1_b, w_ffn_up=v_w_ffn_up,
              conv_w=v_conv_w, conv_b=v_conv_b, w_ffn_down=v_w_ffn_down, ln2_g=v_ln2_g, ln2_b=v_ln2_b)

    packed = _pack_weight_shards({name: wts[name] for name, _, _ in MATS})
    gathered = _allgather_weights(packed)
    full_weights = _unpack_gathered(gathered.reshape(N_CHIP, -1))

    small = {k: wts[k] for k in SMALL}
    loss_part, grad_x, gws, gsmall, g_rel = _local_step(x[0], loss_target[0], full_weights, rel_table, small)

    g = _pack_grads(gws)
    mine, theirs = _sibling_swap(g)
    pair = _sum_rows([mine, theirs], tm=PACK_TILE, name="grad_pair_sum")
    parts = _chip_scatter(pair)
    rh = parts.shape[1]
    slabs = [(parts.reshape(1, N_CHIP * rh, PACK_W), PACK_W, 0, k * (rh // PACK_TILE)) for k in range(N_CHIP)]
    red = _rowwise(lambda i, nt, a, b, c, d: ((a + b) + c) + d, slabs, outs=[(PACK_W, PACK_W, 0, F32)], tm=PACK_TILE, t=rh,
                   name="grad_chip_sum")[0][0]
    both = _sibling_share(red)
    gshard = _unpack_grad_shard(both.reshape(-1))

    small_red = _allreduce_small(_pack_small(g_rel, gsmall, loss_part))
    g_rel_r, gsmall_r, loss_vec = _unpack_small(small_red)
    loss = loss_vec[0]

    grads = dict(gshard)
    grads.update(gsmall_r)
    grads["rel_table"] = g_rel_r
    deltas, new_m, new_v = {}, {}, {}
    for name, _, _ in MATS:
        shp = wts[name].shape
        v2 = lambda a: a.reshape(-1, shp[-1])
        d_, m_, v_ = _adamw(v2(wts[name]), v2(grads[name]), v2(ms[name]), v2(vs[name]), name="adamw_" + name)
        deltas[name], new_m[name], new_v[name] = d_.reshape(shp), m_.reshape(shp), v_.reshape(shp)
    zero = jnp.zeros((LANE,), F32)
    sw = _pack_small(wts["rel_table"], {k: wts[k] for k in SMALL}, zero)
    sm = _pack_small(ms["rel_table"], {k: ms[k] for k in SMALL}, zero)
    sv = _pack_small(vs["rel_table"], {k: vs[k] for k in SMALL}, zero + 1.0)
    sg = _pack_small(g_rel_r, gsmall_r, zero)
    sd, smn, svn = _adamw(sw, sg, sm, sv, name="adamw_small")
    for res, buf in ((deltas, sd), (new_m, smn), (new_v, svn)):
        rel_, sm_, _ = _unpack_small(buf)
        res["rel_table"] = rel_
        res.update(sm_)

    return (loss, grad_x[None], *[grads[k] for k in WEIGHT_ORDER], *[deltas[k] for k in WEIGHT_ORDER],
            *[new_m[k] for k in WEIGHT_ORDER], *[new_v[k] for k in WEIGHT_ORDER])
```

```python
import math

import jax
import jax.numpy as jnp
from jax import lax
from jax.experimental import pallas as pl
from jax.experimental.pallas import tpu as pltpu

F32 = jnp.float32
BF16 = jnp.bfloat16
MESH = pl.DeviceIdType.MESH

D_MODEL = 1024
DEPTH = 2
HEAD_DIM = 64
N_HEADS = 8
A_DILS = (1, 4, 16)
C_Q_RANK = 256
C_KV_RANK = 128
C_NOPE = 64
C_ROPE = 32
ROPE_BASE = 10000.0
REL_BUCKETS = 32
REL_MAX_DIST = 2048
D_FF = 2816
ALPHA = (2 * DEPTH) ** 0.25
LN_EPS = 1e-5
RMS_EPS = 1e-6
NEG = -1e30
ADAM_LR, ADAM_B1, ADAM_B2, ADAM_EPS, ADAM_WD, ADAM_STEP = 0.001, 0.9, 0.999, 1e-08, 0.01, 10

VMEM_LIMIT_BYTES = 56 * 1024 * 1024
LANE = 128
BLK = 128
TQ = 512
HP = 128

COL_GATE, COL_A, COL_BQ, COL_BKV, COL_CQ, COL_CDKV, D_IN_P = 0, 3072, 7680, 8192, 8448, 8704, 8960
D_IN = 8864
ORIG_GATE = 5792

N_CHIP = 4
MATS = (
    ("w_in", (D_MODEL, D_IN), 1),
    ("w_uq", (C_Q_RANK, 768), 1),
    ("w_ukv", (C_KV_RANK, 1024), 1),
    ("w_branch", (3, 512, D_MODEL), 2),
    ("w_out", (D_MODEL, D_MODEL), 0),
    ("w_ffn_up", (D_MODEL, 2 * D_FF), 1),
    ("conv_w", (3, 2 * D_FF), 1),
    ("w_ffn_down", (D_FF, D_MODEL), 0),
)
SMALL = ("b_gate", "sinks", "q_norm_g", "kv_norm_g", "ln1_g", "ln1_b", "conv_b", "ln2_g", "ln2_b")
SMALL_SIZES = {"b_gate": 3072, "sinks": 8, "q_norm_g": 256, "kv_norm_g": 128, "ln1_g": 1024, "ln1_b": 1024,
               "conv_b": 5632, "ln2_g": 1024, "ln2_b": 1024}
WEIGHT_ORDER = ("rel_table", "w_in", "b_gate", "sinks", "q_norm_g", "kv_norm_g", "w_uq", "w_ukv", "w_branch",
                "w_out", "ln1_g", "ln1_b", "w_ffn_up", "conv_w", "conv_b", "w_ffn_down", "ln2_g", "ln2_b")


def _cparams(sem):
    return pltpu.CompilerParams(dimension_semantics=sem, vmem_limit_bytes=VMEM_LIMIT_BYTES)


def _shard_shape(shape, ax):
    s = list(shape)
    s[ax] //= N_CHIP
    return tuple(s)


def _ceil_to(n, m):
    return -(-n // m) * m


def _pick(n, target):
    if n <= target:
        return n
    best = None
    for t in range(LANE, target + 1, LANE):
        if n % t == 0:
            best = t
    assert best is not None, (n, target)
    return best


def _mm(a, b, *, tb=False, out_dtype=F32, tm=512, tn=1024, tk=2048, name):
    m, k = a.shape
    n = b.shape[0] if tb else b.shape[1]
    assert (b.shape[1] if tb else b.shape[0]) == k
    tm, tn, tk = _pick(m, tm), _pick(n, tn), _pick(k, tk)
    nk = k // tk
    dn = (((1,), (1,)), ((), ())) if tb else (((1,), (0,)), ((), ()))

    def body(a_ref, b_ref, o_ref, acc_ref):
        part = lax.dot_general(a_ref[...].astype(BF16), b_ref[...].astype(BF16), dn, preferred_element_type=F32)
        if nk == 1:
            o_ref[...] = part.astype(o_ref.dtype)
        else:
            kk = pl.program_id(2)

            @pl.when(kk == 0)
            def _():
                acc_ref[...] = part

            @pl.when(kk > 0)
            def _():
                acc_ref[...] += part

            @pl.when(kk == nk - 1)
            def _():
                o_ref[...] = acc_ref[...].astype(o_ref.dtype)

    b_spec = pl.BlockSpec((tn, tk), lambda i, j, kk: (j, kk)) if tb else pl.BlockSpec((tk, tn), lambda i, j, kk: (kk, j))
    return pl.pallas_call(
        body, name=name, grid=(m // tm, n // tn, nk),
        in_specs=[pl.BlockSpec((tm, tk), lambda i, j, kk: (i, kk)), b_spec],
        out_specs=pl.BlockSpec((tm, tn), lambda i, j, kk: (i, j)),
        out_shape=jax.ShapeDtypeStruct((m, n), out_dtype),
        scratch_shapes=[pltpu.VMEM((tm, tn) if nk > 1 else (8, LANE), F32)],
        compiler_params=_cparams(("parallel", "parallel", "arbitrary")),
    )(a, b)


def _rowwise(fn, rows, *, pars=(), halos=(), outs=(), accs=(), tm, name, ncol=1, t=None):
    nb = rows[0][0].shape[0]
    t = rows[0][0].shape[1] if t is None else t
    tm = min(tm, t)
    assert t % tm == 0 and tm % 8 == 0
    nt = t // tm
    in_specs, args = [], []
    for spec in rows:
        arr, c, off = spec[:3]
        rb = spec[3] if len(spec) > 3 else 0
        in_specs.append(pl.BlockSpec((1, tm, c), lambda b, cc, i, off=off, rb=rb: (b, i + rb, off + cc)))
        args.append(arr)
    for arr, c, off, kind in halos:
        if kind == "prev":
            im = lambda b, cc, i, off=off: (b, jnp.maximum(i * (tm // 8) - 1, 0), off + cc)
        else:
            im = lambda b, cc, i, off=off: (b, jnp.minimum((i + 1) * (tm // 8), t // 8 - 1), off + cc)
        in_specs.append(pl.BlockSpec((1, 8, c), im))
        args.append(arr)
    for arr, c, off in pars:
        bp, r = arr.shape[:2]
        if bp > 1:
            im = lambda b, cc, i, off=off: (b, 0, off + cc)
        else:
            im = lambda b, cc, i, off=off: (0, 0, off + cc)
        in_specs.append(pl.BlockSpec((1, r, c), im))
        args.append(arr)
    out_specs, out_shapes = [], []
    for ctot, c, off, dt in outs:
        out_specs.append(pl.BlockSpec((1, tm, c), lambda b, cc, i, off=off: (b, i, off + cc)))
        out_shapes.append(jax.ShapeDtypeStruct((nb, t, ctot), dt))
    for r, ctot, c, off in accs:
        out_specs.append(pl.BlockSpec((1, r, c), lambda b, cc, i, off=off: (b, 0, off + cc)))
        out_shapes.append(jax.ShapeDtypeStruct((nb, r, ctot), F32))
    n_in, n_out = len(args), len(outs)

    def body(*refs):
        i = pl.program_id(2)
        res = fn(i, nt, *[r[0] for r in refs[:n_in]])
        if not isinstance(res, (tuple, list)):
            res = (res,)
        for o_ref, val in zip(refs[n_in:n_in + n_out], res[:n_out]):
            o_ref[0] = val.astype(o_ref.dtype)
        for a_ref, val in zip(refs[n_in + n_out:], res[n_out:]):
            @pl.when(i == 0)
            def _(a_ref=a_ref, val=val):
                a_ref[0] = val

            @pl.when(i > 0)
            def _(a_ref=a_ref, val=val):
                a_ref[0] += val

    res = pl.pallas_call(
        body, name=name, grid=(nb, ncol, nt), in_specs=in_specs, out_specs=out_specs, out_shape=out_shapes,
        compiler_params=_cparams(("parallel", "parallel", "arbitrary")),
    )(*args)
    return res


def _lanewise(fn, ins, outs, *, tl, name):
    nb, _, t = ins[0].shape
    tl = min(tl, t)
    assert t % tl == 0
    n_in = len(ins)

    def body(*refs):
        res = fn(*[r[0] for r in refs[:n_in]])
        if not isinstance(res, (tuple, list)):
            res = (res,)
        for o_ref, val in zip(refs[n_in:], res):
            o_ref[0] = val.astype(o_ref.dtype)

    return pl.pallas_call(
        body, name=name, grid=(nb, t // tl),
        in_specs=[pl.BlockSpec((1, a.shape[1], tl), lambda b, i: (b, 0, i)) for a in ins],
        out_specs=[pl.BlockSpec((1, r, tl), lambda b, i: (b, 0, i)) for r, _ in outs],
        out_shape=[jax.ShapeDtypeStruct((nb, r, t), dt) for r, dt in outs],
        compiler_params=_cparams(("parallel", "parallel")),
    )(*ins)


def _dot(a, b):
    return lax.dot_general(a, b, (((1,), (0,)), ((), ())), preferred_element_type=F32)


def _dot_nt(a, b):
    return lax.dot_general(a, b, (((1,), (1,)), ((), ())), preferred_element_type=F32)


def _band_masks(lim):
    ki = lax.broadcasted_iota(jnp.int32, (BLK, BLK), 0)
    qi = lax.broadcasted_iota(jnp.int32, (BLK, BLK), 1)
    return ki <= qi, (BLK + qi - ki) <= lim


def _band_fwd(q, k, vt, bias_t, nbs, lims, sinks, *, name):
    nh, nblk = q.shape[:2]
    scale = HEAD_DIM ** -0.5

    def body(nb_ref, lim_ref, sink_ref, q_ref, k_ref, vt_ref, b_ref, ot_ref, lse_ref):
        h = pl.program_id(0)
        nb, lim, sink = nb_ref[h], lim_ref[h], sink_ref[h]
        cur_ok, prev_ok = _band_masks(lim)
        b_prev, b_cur = b_ref[0, 0:BLK, :], b_ref[0, BLK:2 * BLK, :]

        def blk(b, carry):
            p = jnp.maximum(b - 1, 0)
            qb = q_ref[0, b]
            first = jnp.where(lax.rem(b, nb) == 0, NEG, 0.0).astype(F32)
            sc = jnp.where(cur_ok, _dot_nt(k_ref[0, b], qb) * scale + b_cur, NEG)
            sp = jnp.where(prev_ok, _dot_nt(k_ref[0, p], qb) * scale + b_prev, NEG) + first
            m = jnp.maximum(jnp.maximum(jnp.max(sc, axis=0, keepdims=True), jnp.max(sp, axis=0, keepdims=True)), sink)
            pc, pp = jnp.exp(sc - m), jnp.exp(sp - m)
            l = jnp.sum(pc, axis=0, keepdims=True) + jnp.sum(pp, axis=0, keepdims=True) + jnp.exp(sink - m)
            inv = 1.0 / l
            ot_ref[0, b] = _dot(vt_ref[0, b], (pc * inv).astype(BF16)) + _dot(vt_ref[0, p], (pp * inv).astype(BF16))
            lse_ref[0, b] = m + jnp.log(l)
            return carry

        lax.fori_loop(0, nblk, blk, 0)

    smem = pl.BlockSpec(memory_space=pltpu.SMEM)
    whole = lambda shp: pl.BlockSpec((1,) + shp, lambda h: (h,) + (0,) * len(shp))
    return pl.pallas_call(
        body, name=name, grid=(nh,),
        in_specs=[smem, smem, smem, whole((nblk, BLK, HP)), whole((nblk, BLK, HP)), whole((nblk, HP, BLK)),
                  whole((2 * BLK, BLK))],
        out_specs=[whole((nblk, HP, BLK)), whole((nblk, 1, BLK))],
        out_shape=[jax.ShapeDtypeStruct((nh, nblk, HP, BLK), F32), jax.ShapeDtypeStruct((nh, nblk, 1, BLK), F32)],
        compiler_params=_cparams(("parallel",)),
    )(nbs, lims, sinks, q, k, vt, bias_t)


def _band_bwd(q, k, kt, v, do, lse, dpr, bias_t, nbs, lims, sinks, *, name):
    nh, nblk = q.shape[:2]
    scale = HEAD_DIM ** -0.5

    def body(nb_ref, lim_ref, sink_ref, q_ref, k_ref, kt_ref, v_ref, do_ref, lse_ref, dpr_ref, b_ref,
             dqt_ref, dk_ref, dv_ref, ds_ref, dsink_ref, sink_acc):
        h = pl.program_id(0)
        nb, lim, sink = nb_ref[h], lim_ref[h], sink_ref[h]
        cur_ok, prev_ok = _band_masks(lim)
        b_prev, b_cur = b_ref[0, 0:BLK, :], b_ref[0, BLK:2 * BLK, :]
        ds_ref[...] = jnp.zeros_like(ds_ref)
        sink_acc[...] = jnp.zeros_like(sink_acc)

        def blk(b, carry):
            p = jnp.maximum(b - 1, 0)
            qb, dob = q_ref[0, b], do_ref[0, b]
            lse_row, dpr_row = lse_ref[0, b], dpr_ref[0, b]
            first = jnp.where(lax.rem(b, nb) == 0, NEG, 0.0).astype(F32)
            sc = jnp.where(cur_ok, _dot_nt(k_ref[0, b], qb) * scale + b_cur, NEG)
            sp = jnp.where(prev_ok, _dot_nt(k_ref[0, p], qb) * scale + b_prev, NEG) + first
            pc, pp = jnp.exp(sc - lse_row), jnp.exp(sp - lse_row)
            dsc = pc * (_dot_nt(v_ref[0, b], dob) - dpr_row)
            dsp = pp * (_dot_nt(v_ref[0, p], dob) - dpr_row)
            ds_ref[0, BLK:2 * BLK, :] += dsc
            ds_ref[0, 0:BLK, :] += dsp
            sink_acc[...] += jnp.exp(sink - lse_row) * dpr_row
            dsc_b, dsp_b = (dsc * scale).astype(BF16), (dsp * scale).astype(BF16)
            dqt_ref[0, b] = _dot(kt_ref[0, b], dsc_b) + _dot(kt_ref[0, p], dsp_b)
            dk_ref[0, b] = _dot(dsc_b, qb)
            dv_ref[0, b] = _dot(pc.astype(BF16), dob)
            dk_ref[0, p] += _dot(dsp_b, qb)
            dv_ref[0, p] += _dot(pp.astype(BF16), dob)
            return carry

        lax.fori_loop(0, nblk, blk, 0)
        dsink_ref[0] = jnp.zeros((1, BLK), F32) - jnp.sum(sink_acc[...], axis=1, keepdims=True)

    smem = pl.BlockSpec(memory_space=pltpu.SMEM)
    whole = lambda shp: pl.BlockSpec((1,) + shp, lambda h: (h,) + (0,) * len(shp))
    td, dt, st = (nblk, BLK, HP), (nblk, HP, BLK), (nblk, 1, BLK)
    return pl.pallas_call(
        body, name=name, grid=(nh,),
        in_specs=[smem, smem, smem, whole(td), whole(td), whole(dt), whole(td), whole(td), whole(st), whole(st),
                  whole((2 * BLK, BLK))],
        out_specs=[whole(dt), whole(td), whole(td), whole((2 * BLK, BLK)), whole((1, BLK))],
        out_shape=[jax.ShapeDtypeStruct((nh,) + dt, F32), jax.ShapeDtypeStruct((nh,) + td, F32),
                   jax.ShapeDtypeStruct((nh,) + td, F32), jax.ShapeDtypeStruct((nh, 2 * BLK, BLK), F32),
                   jax.ShapeDtypeStruct((nh, 1, BLK), F32)],
        scratch_shapes=[pltpu.VMEM((1, BLK), F32)],
        compiler_params=_cparams(("parallel",)),
    )(nbs, lims, sinks, q, k, kt, v, do, lse, dpr, bias_t)


def _causal_mask(kj, qi):
    kidx = kj * TQ + lax.broadcasted_iota(jnp.int32, (TQ, TQ), 0)
    qidx = qi * TQ + lax.broadcasted_iota(jnp.int32, (TQ, TQ), 1)
    return kidx <= qidx


def _mla_fwd(q, k, vt, *, name):
    nh, n = q.shape[:2]
    scale = (C_NOPE + C_ROPE) ** -0.5

    def body(q_ref, k_ref, vt_ref, ot_ref, lse_ref, m_ref, l_ref, acc_ref):
        def qloop(qi, carry):
            qb = q_ref[0, qi]
            m_ref[...] = jnp.full_like(m_ref, NEG)
            l_ref[...] = jnp.zeros_like(l_ref)
            acc_ref[...] = jnp.zeros_like(acc_ref)

            def kloop(kj, c2):
                s = jnp.where(_causal_mask(kj, qi), _dot_nt(k_ref[0, kj], qb) * scale, NEG)
                m_prev = m_ref[...]
                m_new = jnp.maximum(m_prev, jnp.max(s, axis=0, keepdims=True))
                a = jnp.exp(m_prev - m_new)
                p = jnp.exp(s - m_new)
                l_ref[...] = a * l_ref[...] + jnp.sum(p, axis=0, keepdims=True)
                acc_ref[...] = a * acc_ref[...] + _dot(vt_ref[0, kj], p.astype(BF16))
                m_ref[...] = m_new
                return c2

            lax.fori_loop(0, qi + 1, kloop, 0)
            ot_ref[0, qi] = acc_ref[...] * (1.0 / l_ref[...])
            lse_ref[0, qi] = m_ref[...] + jnp.log(l_ref[...])
            return carry

        lax.fori_loop(0, n, qloop, 0)

    whole = lambda shp: pl.BlockSpec((1,) + shp, lambda h: (h,) + (0,) * len(shp))
    return pl.pallas_call(
        body, name=name, grid=(nh,),
        in_specs=[whole((n, TQ, HP)), whole((n, TQ, HP)), whole((n, HP, TQ))],
        out_specs=[whole((n, HP, TQ)), whole((n, 1, TQ))],
        out_shape=[jax.ShapeDtypeStruct((nh, n, HP, TQ), F32), jax.ShapeDtypeStruct((nh, n, 1, TQ), F32)],
        scratch_shapes=[pltpu.VMEM((1, TQ), F32), pltpu.VMEM((1, TQ), F32), pltpu.VMEM((HP, TQ), F32)],
        compiler_params=_cparams(("parallel",)),
    )(q, k, vt)


def _mla_bwd(q, k, kt, v, do, lse, delta, *, name):
    nh, n = q.shape[:2]
    scale = (C_NOPE + C_ROPE) ** -0.5

    def body(q_ref, k_ref, kt_ref, v_ref, do_ref, lse_ref, dl_ref, dqt_ref, dk_ref, dv_ref, dk_acc, dv_acc):
        dqt_ref[...] = jnp.zeros_like(dqt_ref)

        def kloop(kj, carry):
            kb, vb, ktb = k_ref[0, kj], v_ref[0, kj], kt_ref[0, kj]
            dk_acc[...] = jnp.zeros_like(dk_acc)
            dv_acc[...] = jnp.zeros_like(dv_acc)

            def qloop(qi, c2):
                qb, dob = q_ref[0, qi], do_ref[0, qi]
                s = jnp.where(_causal_mask(kj, qi), _dot_nt(kb, qb) * scale, NEG)
                p = jnp.exp(s - lse_ref[0, qi])
                ds = (p * (_dot_nt(vb, dob) - dl_ref[0, qi]) * scale).astype(BF16)
                dv_acc[...] += _dot(p.astype(BF16), dob)
                dk_acc[...] += _dot(ds, qb)
                dqt_ref[0, qi] += _dot(ktb, ds)
                return c2

            lax.fori_loop(kj, n, qloop, 0)
            dk_ref[0, kj] = dk_acc[...]
            dv_ref[0, kj] = dv_acc[...]
            return carry

        lax.fori_loop(0, n, kloop, 0)

    whole = lambda shp: pl.BlockSpec((1,) + shp, lambda h: (h,) + (0,) * len(shp))
    td, dt, st = (n, TQ, HP), (n, HP, TQ), (n, 1, TQ)
    return pl.pallas_call(
        body, name=name, grid=(nh,),
        in_specs=[whole(td), whole(td), whole(dt), whole(td), whole(td), whole(st), whole(st)],
        out_specs=[whole(dt), whole(td), whole(td)],
        out_shape=[jax.ShapeDtypeStruct((nh,) + dt, F32), jax.ShapeDtypeStruct((nh,) + td, F32),
                   jax.ShapeDtypeStruct((nh,) + td, F32)],
        scratch_shapes=[pltpu.VMEM((TQ, HP), F32), pltpu.VMEM((TQ, HP), F32)],
        compiler_params=_cparams(("parallel",)),
    )(q, k, kt, v, do, lse, delta)


def _bias_lookup(bucket, table_t, *, name):
    nh, npos = bucket.shape
    tp = 4096

    def body(b_ref, t_ref, o_ref):
        bk, tab = b_ref[...], t_ref[...]
        acc = jnp.zeros(bk.shape, F32)
        for i in range(REL_BUCKETS):
            acc = jnp.where(bk == i, tab[:, i:i + 1], acc)
        o_ref[...] = acc

    return pl.pallas_call(
        body, name=name, grid=(npos // tp,),
        in_specs=[pl.BlockSpec((nh, tp), lambda i: (0, i)), pl.BlockSpec((nh, REL_BUCKETS), lambda i: (0, 0))],
        out_specs=pl.BlockSpec((nh, tp), lambda i: (0, i)),
        out_shape=jax.ShapeDtypeStruct((nh, npos), F32),
        compiler_params=_cparams(("parallel",)),
    )(bucket, table_t)


def _bias_grad(bucket, ds0, ds1, *, name):
    nh, npos = bucket.shape
    tp = 4096

    def body(b_ref, a_ref, c_ref, o_ref):
        i = pl.program_id(0)
        bk, ds = b_ref[...], a_ref[...] + c_ref[...]
        lane = lax.broadcasted_iota(jnp.int32, (nh, REL_BUCKETS), 1)
        acc = jnp.zeros((nh, REL_BUCKETS), F32)
        for j in range(REL_BUCKETS):
            col = jnp.sum(jnp.where(bk == j, ds, 0.0), axis=1, keepdims=True)
            acc = acc + jnp.where(lane == j, col, 0.0)

        @pl.when(i == 0)
        def _():
            o_ref[...] = acc

        @pl.when(i > 0)
        def _():
            o_ref[...] += acc

    return pl.pallas_call(
        body, name=name, grid=(npos // tp,),
        in_specs=[pl.BlockSpec((nh, tp), lambda i: (0, i))] * 3,
        out_specs=pl.BlockSpec((nh, REL_BUCKETS), lambda i: (0, 0)),
        out_shape=jax.ShapeDtypeStruct((nh, REL_BUCKETS), F32),
        compiler_params=_cparams(("arbitrary",)),
    )(bucket, ds0, ds1)


def _t5_bucket(dist):
    n = jnp.maximum(dist, 0)
    max_exact = REL_BUCKETS // 2
    scaled = jnp.log(jnp.maximum(n, 1).astype(F32) / max_exact) / math.log(REL_MAX_DIST / max_exact)
    large = max_exact + (scaled * (REL_BUCKETS - max_exact)).astype(jnp.int32)
    return jnp.where(n < max_exact, n, jnp.minimum(large, REL_BUCKETS - 1))


def _bucket_index():
    ci = jnp.arange(2 * BLK)[:, None]
    qi = jnp.arange(BLK)[None, :]
    step = BLK + qi - ci
    per_group = [_t5_bucket(step * d).reshape(1, -1) for d in A_DILS + (1,)]
    return jnp.concatenate([jnp.tile(b, (N_HEADS, 1)) for b in per_group], axis=0).astype(jnp.int32)


def _sigmoid(x):
    return 1.0 / (1.0 + jnp.exp(-x))


def _ln_stats(z):
    mu = jnp.mean(z, axis=-1, keepdims=True)
    zc = z - mu
    var = jnp.mean(zc * zc, axis=-1, keepdims=True)
    return zc * lax.rsqrt(var + LN_EPS)


def _ln_fwd(x, mix, g, b, *, name):
    def fn(i, nt, xv, mv, gv, bv):
        z = ALPHA * xv + mv
        y = _ln_stats(z) * gv + bv
        return y, y, z

    c = x.shape[-1]
    y, yb, z = _rowwise(fn, [(x[None], c, 0), (mix[None], c, 0)], pars=[(g.reshape(1, 1, c), c, 0), (b.reshape(1, 1, c), c, 0)],
                        outs=[(c, c, 0, F32), (c, c, 0, BF16), (c, c, 0, F32)], tm=512, name=name)
    return y[0], yb[0], z[0]


def _ln_bwd(z, g, dys, coefs, *, name):
    n = len(dys)

    def fn(i, nt, zv, *rest):
        gv = rest[n]
        dy = coefs[0] * rest[0]
        for cf, t in zip(coefs[1:], rest[1:n]):
            dy = dy + cf * t
        mu = jnp.mean(zv, axis=-1, keepdims=True)
        zc = zv - mu
        r = lax.rsqrt(jnp.mean(zc * zc, axis=-1, keepdims=True) + LN_EPS)
        xh = zc * r
        dxh = dy * gv
        dz = r * (dxh - jnp.mean(dxh, axis=-1, keepdims=True) - xh * jnp.mean(dxh * xh, axis=-1, keepdims=True))
        return dz, dz, jnp.sum(dy * xh, axis=0, keepdims=True), jnp.sum(dy, axis=0, keepdims=True)

    c = z.shape[-1]
    dz, dzb, dg, db = _rowwise(fn, [(z[None], c, 0)] + [(d[None], c, 0) for d in dys], pars=[(g.reshape(1, 1, c), c, 0)],
                               outs=[(c, c, 0, F32), (c, c, 0, BF16)], accs=[(1, c, c, 0), (1, c, c, 0)], tm=512, name=name)
    return dz[0], dzb[0], dg.reshape(c), db.reshape(c)


def _rms_fwd(src, c, off, g, *, name):
    def fn(i, nt, xv, gv):
        return xv * lax.rsqrt(jnp.mean(xv * xv, axis=-1, keepdims=True) + RMS_EPS) * gv

    return _rowwise(fn, [(src[None], c, off)], pars=[(g.reshape(1, 1, c), c, 0)], outs=[(c, c, 0, BF16)], tm=1024, name=name)[0][0]


def _rms_bwd(src, c, off, g, dy, *, name):
    def fn(i, nt, xv, dyv, gv):
        r = lax.rsqrt(jnp.mean(xv * xv, axis=-1, keepdims=True) + RMS_EPS)
        gd = gv * dyv
        dx = gd * r - xv * (r * r * r) * jnp.mean(gd * xv, axis=-1, keepdims=True)
        return dx, jnp.sum(dyv * xv * r, axis=0, keepdims=True)

    dx, dg = _rowwise(fn, [(src[None], c, off), (dy[None], c, 0)], pars=[(g.reshape(1, 1, c), c, 0)],
                      outs=[(c, c, 0, BF16)], accs=[(1, c, c, 0)], tm=1024, name=name)
    return dx[0], dg.reshape(c)


def _rope(x1, x2, cos, sin, *, name):
    def fn(i, nt, a, b, c, s):
        return a * c - b * s, a * s + b * c

    w = x1.shape[-1]
    y1, y2 = _rowwise(fn, [(x1[None], w, 0), (x2[None], w, 0), (cos[None], w, 0), (sin[None], w, 0)],
                      outs=[(w, w, 0, F32), (w, w, 0, F32)], tm=1024, name=name)
    return y1[0], y2[0]


def _merge_fwd(proj, b_gate, ys, *, name):
    def fn(i, nt, g0, g1, g2, ya, yb, yc, bg):
        return (_sigmoid(g0 + bg[:, 0:1024]) * ya + _sigmoid(g1 + bg[:, 1024:2048]) * yb
                + _sigmoid(g2 + bg[:, 2048:3072]) * yc)

    rows = [(proj[None], 1024, j) for j in range(3)] + [(y[None], 1024, 0) for y in ys]
    return _rowwise(fn, rows, pars=[(b_gate.reshape(1, 1, 3072), 3072, 0)], outs=[(1024, 1024, 0, BF16)], tm=512, name=name)[0][0]


def _merge_bwd(proj, b_gate, ys, dm, *, name):
    def fn(i, nt, g0, g1, g2, ya, yb, yc, dmv, bg):
        outs, dgs = [], []
        for j, (gp, y) in enumerate(((g0, ya), (g1, yb), (g2, yc))):
            s = _sigmoid(gp + bg[:, j * 1024:(j + 1) * 1024])
            outs.append(s * dmv)
            dgs.append(dmv * y * s * (1.0 - s))
        return outs + dgs + [jnp.sum(d, axis=0, keepdims=True) for d in dgs]

    rows = [(proj[None], 1024, j) for j in range(3)] + [(y[None], 1024, 0) for y in ys] + [(dm[None], 1024, 0)]
    res = _rowwise(fn, rows, pars=[(b_gate.reshape(1, 1, 3072), 3072, 0)], outs=[(1024, 1024, 0, BF16)] * 6,
                   accs=[(1, 1024, 1024, 0)] * 3, tm=256, name=name)
    dys = [r[0] for r in res[0:3]]
    dgp = [r[0] for r in res[3:6]]
    dbg = jnp.concatenate([r.reshape(1024) for r in res[6:9]])
    return dys, dgp, dbg


def _shift_down(u, halo, i, k):
    ext = jnp.concatenate([jnp.where(i > 0, halo, 0.0), u], axis=0)
    return pltpu.roll(ext, k, axis=0)[8:]


def _shift_up(u, halo, i, nt, k):
    ext = jnp.concatenate([u, jnp.where(i < nt - 1, halo, 0.0)], axis=0)
    n = ext.shape[0]
    return pltpu.roll(ext, n - k, axis=0)[:n - 8]


GLU_C = D_FF // 2


def _conv(u, halo, i, w, b):
    return w[0:1] * _shift_down(u, halo, i, 2) + w[1:2] * _shift_down(u, halo, i, 1) + w[2:3] * u + b


def _glu_fwd(ug, uv, conv_w, conv_b, *, name):
    def fn(i, nt, g, v, hg, hv, wg, wv, bg, bv):
        cg, cv = _conv(g, hg, i, wg, bg), _conv(v, hv, i, wv, bv)
        return cg * _sigmoid(cg) * cv

    w3, b3 = conv_w[None], conv_b.reshape(1, 1, -1)
    c = GLU_C
    return _rowwise(fn, [(ug[None], c, 0), (uv[None], c, 0)], halos=[(ug[None], c, 0, "prev"), (uv[None], c, 0, "prev")],
                    pars=[(w3, c, 0), (w3, c, 2), (b3, c, 0), (b3, c, 2)], outs=[(D_FF, c, 0, BF16)], tm=256, ncol=2, name=name)[0][0]


def _glu_bwd_a(ug, uv, conv_w, conv_b, dh, *, name):
    def fn(i, nt, g, v, dhv, hg, hv, wg, wv, bg, bv):
        g1, g2 = _shift_down(g, hg, i, 1), _shift_down(g, hg, i, 2)
        v1, v2 = _shift_down(v, hv, i, 1), _shift_down(v, hv, i, 2)
        cg = wg[0:1] * g2 + wg[1:2] * g1 + wg[2:3] * g + bg
        cv = wv[0:1] * v2 + wv[1:2] * v1 + wv[2:3] * v + bv
        s = _sigmoid(cg)
        dcv = dhv * cg * s
        dcg = dhv * cv * (s * (1.0 + cg * (1.0 - s)))
        red = lambda a: jnp.sum(a, axis=0, keepdims=True)
        return (dcg, dcv, red(dcg), red(dcv), red(dcg * g2), red(dcg * g1), red(dcg * g),
                red(dcv * v2), red(dcv * v1), red(dcv * v))

    w3, b3 = conv_w[None], conv_b.reshape(1, 1, -1)
    c = GLU_C
    res = _rowwise(fn, [(ug[None], c, 0), (uv[None], c, 0), (dh[None], c, 0)],
                   halos=[(ug[None], c, 0, "prev"), (uv[None], c, 0, "prev")],
                   pars=[(w3, c, 0), (w3, c, 2), (b3, c, 0), (b3, c, 2)],
                   outs=[(D_FF, c, 0, F32), (D_FF, c, 0, F32)], accs=[(1, D_FF, c, 0)] * 8, tm=256, ncol=2, name=name)
    dcg, dcv = res[0][0], res[1][0]
    dconv_b = jnp.concatenate([res[2].reshape(D_FF), res[3].reshape(D_FF)])
    dconv_w = jnp.concatenate([jnp.concatenate([res[4 + j].reshape(1, D_FF) for j in range(3)], axis=0),
                               jnp.concatenate([res[7 + j].reshape(1, D_FF) for j in range(3)], axis=0)], axis=1)
    return dcg, dcv, dconv_w, dconv_b


def _glu_bwd_b(dc, conv_w, half, *, name):
    def fn(i, nt, d, hd, w):
        return w[2:3] * d + w[1:2] * _shift_up(d, hd, i, nt, 1) + w[0:1] * _shift_up(d, hd, i, nt, 2)

    c = GLU_C
    return _rowwise(fn, [(dc[None], c, 0)], halos=[(dc[None], c, 0, "next")], pars=[(conv_w[None], c, 2 * half)],
                    outs=[(D_FF, c, 0, BF16)], tm=256, ncol=2, name=name)[0][0]


def _loss_and_grad(y, tgt, *, name):
    def fn(i, nt, yv, tv):
        err = yv - tv
        part = jnp.sum(jnp.sum(err * err, axis=0, keepdims=True), axis=1, keepdims=True) * (0.5 / D_MODEL)
        return err * (1.0 / D_MODEL), jnp.zeros((1, LANE), F32) + part

    dy, part = _rowwise(fn, [(y[None], D_MODEL, 0), (tgt[None], D_MODEL, 0)], outs=[(D_MODEL, D_MODEL, 0, F32)],
                        accs=[(1, LANE, LANE, 0)], tm=512, name=name)
    return dy[0], part.reshape(LANE)


def _axpy(a, x, y, *, name):
    def fn(i, nt, xv, yv):
        return a * xv + yv

    c = x.shape[-1]
    return _rowwise(fn, [(x[None], c, 0), (y[None], c, 0)], outs=[(c, c, 0, F32)], tm=512, name=name)[0][0]


def _sum_rows(terms, *, tm, name, dtype=F32):
    def fn(i, nt, *vs):
        acc = vs[0]
        for v in vs[1:]:
            acc = acc + v
        return acc

    c = terms[0].shape[-1]
    return _rowwise(fn, [(t, c, 0) for t in terms], outs=[(c, c, 0, dtype)], tm=tm, name=name)[0]


def _combine_fwd(ots, lses, *, name):
    def fn(o0, o1, o2, l0, l1, l2):
        m = jnp.maximum(jnp.maximum(l0, l1), l2)
        e0, e1, e2 = jnp.exp(l0 - m), jnp.exp(l1 - m), jnp.exp(l2 - m)
        inv = 1.0 / (e0 + e1 + e2)
        return (e0 * inv) * o0 + (e1 * inv) * o1 + (e2 * inv) * o2

    return _lanewise(fn, list(ots) + list(lses), [(HEAD_DIM, F32)], tl=2048, name=name)[0]


def _combine_bwd(ots, lses, dot_a, *, name):
    def fn(o0, o1, o2, l0, l1, l2, da):
        m = jnp.maximum(jnp.maximum(l0, l1), l2)
        es = [jnp.exp(l - m) for l in (l0, l1, l2)]
        inv = 1.0 / (es[0] + es[1] + es[2])
        ws = [e * inv for e in es]
        dws = [jnp.sum(da * o, axis=0, keepdims=True) for o in (o0, o1, o2)]
        mean = ws[0] * dws[0] + ws[1] * dws[1] + ws[2] * dws[2]
        dos = [w * da for w in ws]
        dprs = [w * dw - w * (dw - mean) for w, dw in zip(ws, dws)]
        return dos + dprs

    res = _lanewise(fn, list(ots) + list(lses) + [dot_a], [(HEAD_DIM, BF16)] * 3 + [(1, F32)] * 3, tl=2048, name=name)
    return res[0:3], res[3:6]


def _rowdot(at, bt, *, name):
    def fn(a, b):
        return jnp.sum(a * b, axis=0, keepdims=True)

    return _lanewise(fn, [at, bt], [(1, F32)], tl=2048, name=name)[0]


def _adamw(w, g, m, v, *, name):
    c1 = 1.0 - ADAM_B1 ** ADAM_STEP
    c2 = 1.0 - ADAM_B2 ** ADAM_STEP

    def fn(i, nt, wv, gv, mv, vv):
        mn = ADAM_B1 * mv + (1.0 - ADAM_B1) * gv
        vn = ADAM_B2 * vv + (1.0 - ADAM_B2) * (gv * gv)
        delta = -ADAM_LR * ((mn / c1) / (jnp.sqrt(vn / c2) + ADAM_EPS) + ADAM_WD * wv)
        return delta, mn, vn

    r, c = w.shape
    rp = _ceil_to(r, 8)
    pad = lambda a: jnp.pad(a, ((0, rp - r), (0, 0))) if rp != r else a
    tm = rp
    for cand in (128, 64, 32, 16, 8):
        if rp % cand == 0:
            tm = cand
            break
    res = _rowwise(fn, [(pad(a)[None], c, 0) for a in (w, g, m, v)], outs=[(c, c, 0, F32)] * 3, tm=tm, name=name)
    return [x[0][:r] for x in res]


ANY = pl.BlockSpec(memory_space=pl.ANY)


def _place():
    x, y, c = lax.axis_index("x"), lax.axis_index("y"), lax.axis_index("c")
    chips = [(1 - x, y), (x, 1 - y), (1 - x, 1 - y)]
    return x, y, c, chips


def _allgather_weights(arrs):
    n = len(arrs)

    def body(*refs):
        ins, outs, send_sems, recv_sems = refs[:n], refs[n:2 * n], refs[2 * n], refs[2 * n + 1]
        x, y, c, chips = _place()
        j = 2 * x + y

        def cp(i, k, src, chip_idx, half, to):
            return pltpu.make_async_remote_copy(src_ref=src, dst_ref=outs[i].at[chip_idx, half], send_sem=send_sems.at[k],
                                                recv_sem=recv_sems.at[k], device_id=to, device_id_type=MESH)

        first, passed = [], []
        for i in range(n):
            for r, (cx, cy) in enumerate(chips):
                first.append(cp(i, 3 * i + r, ins[i].at[c], j, c, (cx, cy, c)))
                passed.append(cp(i, 3 * (n + i) + r, outs[i].at[2 * cx + cy, c], 2 * cx + cy, c, (x, y, 1 - c)))
        for d in first:
            d.start()
        for i in range(n):
            for r, (cx, cy) in enumerate(chips):
                cp(i, 3 * i + r, ins[i].at[c], 2 * cx + cy, c, (x, y, c)).wait_recv()
                passed[3 * i + r].start()
        for i in range(n):
            for r, (cx, cy) in enumerate(chips):
                cp(i, 3 * (n + i) + r, ins[i].at[c], 2 * cx + cy, 1 - c, (x, y, c)).wait_recv()
        for d in first + passed:
            d.wait_send()

    return pl.pallas_call(
        body, name="allgather_weights", in_specs=[ANY] * n, out_specs=[ANY] * n,
        out_shape=[jax.ShapeDtypeStruct((N_CHIP,) + a.shape, a.dtype) for a in arrs],
        scratch_shapes=[pltpu.SemaphoreType.DMA((6 * n,)), pltpu.SemaphoreType.DMA((6 * n,))],
    )(*arrs)


def _sibling_swap(gs):
    n = len(gs)

    def body(*refs):
        ins, outs, send_sems, recv_sems = refs[:n], refs[n:2 * n], refs[2 * n], refs[2 * n + 1]
        x, y, c, _ = _place()
        cps = [pltpu.make_async_remote_copy(src_ref=ins[i].at[1 - c], dst_ref=outs[i], send_sem=send_sems.at[i],
                                            recv_sem=recv_sems.at[i], device_id=(x, y, 1 - c), device_id_type=MESH)
               for i in range(n)]
        for d in cps:
            d.start()
        for d in cps:
            d.wait_recv()
        for d in cps:
            d.wait_send()

    return pl.pallas_call(
        body, name="grad_sibling_swap", in_specs=[ANY] * n, out_specs=[ANY] * n,
        out_shape=[jax.ShapeDtypeStruct(g.shape[1:], g.dtype) for g in gs],
        scratch_shapes=[pltpu.SemaphoreType.DMA((n,)), pltpu.SemaphoreType.DMA((n,))],
    )(*gs)


def _chip_scatter(ps):
    n = len(ps)

    def body(*refs):
        ins, outs, send_sems, recv_sems = refs[:n], refs[n:2 * n], refs[2 * n], refs[2 * n + 1]
        x, y, c, chips = _place()
        sends = []
        for i in range(n):
            for r, (cx, cy) in enumerate(chips):
                sends.append(pltpu.make_async_remote_copy(src_ref=ins[i].at[2 * cx + cy], dst_ref=outs[i].at[r], send_sem=send_sems.at[3 * i + r],
                                                          recv_sem=recv_sems.at[3 * i + r], device_id=(cx, cy, c), device_id_type=MESH))
        for d in sends:
            d.start()
        for d in sends:
            d.wait_recv()
        for d in sends:
            d.wait_send()

    return pl.pallas_call(
        body, name="grad_chip_scatter", in_specs=[ANY] * n, out_specs=[ANY] * n,
        out_shape=[jax.ShapeDtypeStruct((3,) + p.shape[1:], p.dtype) for p in ps],
        scratch_shapes=[pltpu.SemaphoreType.DMA((3 * n,)), pltpu.SemaphoreType.DMA((3 * n,))],
    )(*ps)


def _sibling_share(rs):
    n = len(rs)

    def body(*refs):
        ins, outs, send_sems, recv_sems = refs[:n], refs[n:2 * n], refs[2 * n], refs[2 * n + 1]
        x, y, c, _ = _place()
        cps = [pltpu.make_async_remote_copy(src_ref=ins[i], dst_ref=outs[i], send_sem=send_sems.at[i], recv_sem=recv_sems.at[i],
                                            device_id=(x, y, 1 - c), device_id_type=MESH) for i in range(n)]
        for d in cps:
            d.start()
        for d in cps:
            d.wait_recv()
        for d in cps:
            d.wait_send()

    return pl.pallas_call(
        body, name="grad_sibling_share", in_specs=[ANY] * n, out_specs=[ANY] * n,
        out_shape=[jax.ShapeDtypeStruct(r.shape, r.dtype) for r in rs],
        scratch_shapes=[pltpu.SemaphoreType.DMA((n,)), pltpu.SemaphoreType.DMA((n,))],
    )(*rs)


def _allreduce_small(s):
    rows, w = s.shape
    n_dev = 8

    def body(s_ref, out_ref, slots, send_sems, recv_sems):
        x, y, c, _ = _place()
        me = 4 * x + 2 * y + c
        slots[me] = s_ref[...]
        peers = []
        for r in range(1, n_dev):
            px = 1 - x if r & 4 else x
            py = 1 - y if r & 2 else y
            pc = 1 - c if r & 1 else c
            peers.append((px, py, pc))
        sends = [pltpu.make_async_remote_copy(src_ref=s_ref, dst_ref=slots.at[me], send_sem=send_sems.at[r], recv_sem=recv_sems.at[r],
                                              device_id=peer, device_id_type=MESH) for r, peer in enumerate(peers)]
        for d in sends:
            d.start()
        for r, (px, py, pc) in enumerate(peers):
            pltpu.make_async_remote_copy(src_ref=s_ref, dst_ref=slots.at[4 * px + 2 * py + pc], send_sem=send_sems.at[r],
                                         recv_sem=recv_sems.at[r], device_id=(x, y, c), device_id_type=MESH).wait_recv()
        for d in sends:
            d.wait_send()
        acc = slots[0]
        for k in range(1, n_dev):
            acc = acc + slots[k]
        out_ref[...] = acc

    vm = pl.BlockSpec(memory_space=pltpu.VMEM)
    return pl.pallas_call(
        body, name="allreduce_small", in_specs=[vm], out_specs=vm, out_shape=jax.ShapeDtypeStruct((rows, w), F32),
        scratch_shapes=[pltpu.VMEM((n_dev, rows, w), F32), pltpu.SemaphoreType.DMA((n_dev - 1,)), pltpu.SemaphoreType.DMA((n_dev - 1,))],
    )(s)


W_IN_SHARD = D_IN // N_CHIP
W_IN_ROWS_G = 2304
GATE_SPLIT = ORIG_GATE - 2 * W_IN_SHARD
REDUCED = tuple(m for m in MATS if m[0] != "conv_w")
CONV_W_SIZE = 3 * 2 * D_FF


def _weight_send(name, a):
    if name == "w_in":
        return jnp.swapaxes(a, 1, 2).astype(BF16)
    return a if name == "conv_w" else a.astype(BF16)


def _full_weights(gathered, l):
    g = {k: v[:, l] for k, v in gathered.items()}
    s = g["w_in"].astype(F32)
    win_pt = jnp.concatenate([s[2, GATE_SPLIT:], s[3], s[0], s[1], s[2, :GATE_SPLIT],
                              jnp.zeros((D_IN_P - D_IN, D_MODEL), F32)], axis=0).astype(BF16)
    full = {name: jnp.moveaxis(g[name], 0, ax).reshape(shape) for name, shape, ax in MATS if name != "w_in"}
    return {"win_pt": win_pt, "w_uq": full["w_uq"], "w_ukv": full["w_ukv"], "w_branch": full["w_branch"], "w_out": full["w_out"],
            "wup_g": full["w_ffn_up"][:, :D_FF], "wup_v": full["w_ffn_up"][:, D_FF:], "conv_w": full["conv_w"],
            "w_ffn_down": full["w_ffn_down"]}


def _grad_send(name, g, shape, ax):
    if name == "w_in":
        gt = g.T
        rest = W_IN_SHARD - GATE_SPLIT
        slabs = [gt[COL_A:COL_A + W_IN_SHARD], gt[COL_A + W_IN_SHARD:COL_A + 2 * W_IN_SHARD],
                 jnp.concatenate([gt[COL_A + 2 * W_IN_SHARD:D_IN], gt[:rest]], axis=0), gt[rest:COL_A]]
        return jnp.pad(jnp.stack(slabs), ((0, 0), (0, W_IN_ROWS_G - W_IN_SHARD), (0, 0)))
    split = shape[:ax] + (N_CHIP, shape[ax] // N_CHIP) + shape[ax + 1:]
    return jnp.moveaxis(g.reshape(split), ax, 0)


def _grad_recv(name, r):
    return r[:W_IN_SHARD].T if name == "w_in" else r


def _pack_small(rel, small, conv_w, extra):
    parts = [rel.reshape(-1)]
    for l in range(DEPTH):
        for name in SMALL:
            parts.append(small[name][l].reshape(-1))
    parts += [conv_w.reshape(-1), extra]
    flat = jnp.concatenate(parts)
    rows = _ceil_to(-(-flat.shape[0] // LANE), 8)
    return jnp.pad(flat, (0, rows * LANE - flat.shape[0])).reshape(rows, LANE)


def _unpack_small(buf):
    flat = buf.reshape(-1)
    rel = flat[:REL_BUCKETS * 32].reshape(REL_BUCKETS, 32)
    off = REL_BUCKETS * 32
    small = {name: [] for name in SMALL}
    for l in range(DEPTH):
        for name in SMALL:
            n = SMALL_SIZES[name]
            small[name].append(flat[off:off + n])
            off += n
    conv_w = flat[off:off + DEPTH * CONV_W_SIZE].reshape(DEPTH, 3, 2 * D_FF)
    off += DEPTH * CONV_W_SIZE
    return rel, {k: jnp.stack(v) for k, v in small.items()}, conv_w, flat[off:off + LANE]


def _rows2d(a, lead):
    return a.reshape(a.shape[:lead] + (-1, a.shape[-1]))


def _row_tile(rows):
    for cand in (512, 256, 128, 64, 32, 16, 8):
        if rows % cand == 0:
            return cand
    raise ValueError(rows)


def _pair_add(g, got, core, *, name):
    g2, got2 = _rows2d(g, 1), _rows2d(got, 0)
    rows, c = got2.shape
    tm = _row_tile(rows)
    flag = jnp.zeros((1, 1, LANE), F32) + core.astype(F32)

    def fn(i, nt, a0, a1, b, f):
        return jnp.where(f[:, 0:1] == 0.0, a0, a1) + b

    stacked = g2.reshape(1, 2 * rows, c)
    out = _rowwise(fn, [(stacked, c, 0, 0), (stacked, c, 0, rows // tm), (got2[None], c, 0)], pars=[(flag, LANE, 0)],
                   outs=[(c, c, 0, BF16)], tm=tm, t=rows, name=name)[0][0]
    return out.reshape(got.shape)


def _chip_add(own, got, *, name):
    own2, got2 = _rows2d(own, 0), _rows2d(got, 1)
    rows, c = own2.shape
    tm = _row_tile(rows)

    def fn(i, nt, a, b0, b1, b2):
        return ((a.astype(F32) + b0.astype(F32)) + b1.astype(F32)) + b2.astype(F32)

    stacked = got2.reshape(1, 3 * rows, c)
    out = _rowwise(fn, [(own2[None], c, 0)] + [(stacked, c, 0, k * (rows // tm)) for k in range(3)],
                   outs=[(c, c, 0, F32)], tm=tm, t=rows, name=name)[0][0]
    return out.reshape(own.shape)


def _perm(a, d):
    if d == 1:
        return a
    t = a.shape[0]
    return jnp.swapaxes(a.reshape((t // d, d) + a.shape[1:]), 0, 1).reshape(a.shape)


def _unperm(a, d):
    if d == 1:
        return a
    t = a.shape[0]
    return jnp.swapaxes(a.reshape((d, t // d) + a.shape[1:]), 0, 1).reshape(a.shape)


def _pad_lanes(a, w=HP):
    return jnp.pad(a, [(0, 0)] * (a.ndim - 1) + [(0, w - a.shape[-1])])


def _heads_blocks(a, blk):
    t, h, _ = a.shape
    return jnp.transpose(_pad_lanes(a), (1, 0, 2)).astype(BF16).reshape(h, t // blk, blk, HP)


def _heads_blocks_t(a, blk):
    return jnp.swapaxes(_heads_blocks(a, blk), -1, -2)


def _from_blocks_t(a):
    h, n, d, blk = a.shape
    return jnp.transpose(a, (1, 3, 0, 2)).reshape(n * blk, h, d)


def _from_blocks(a):
    h, n, blk, d = a.shape
    return jnp.transpose(a, (1, 2, 0, 3)).reshape(n * blk, h, d)


def _to_hdt(a):
    return jnp.transpose(a, (1, 2, 0))


def _rope_tables(t):
    pos = jnp.arange(t, dtype=F32)
    inv_freq = ROPE_BASE ** (-jnp.arange(0, C_ROPE, 2, dtype=F32) / C_ROPE)
    ang = pos[:, None] * inv_freq[None, :]
    cos, sin = jnp.cos(ang), jnp.sin(ang)
    half = C_ROPE // 2
    wide = lambda a: jnp.concatenate([jnp.tile(a, (1, N_HEADS)), a, jnp.zeros((t, 2 * LANE - (N_HEADS + 1) * half), F32)], axis=1)
    return wide(cos), wide(sin)


def _rope_pack(q_part, k_part):
    t = k_part.shape[0]
    return jnp.concatenate([q_part.reshape(t, -1), k_part, jnp.zeros((t, 2 * LANE - 9 * (C_ROPE // 2)), F32)], axis=1)


def _band_params(t, sinks):
    nbs = jnp.array(sum([[t // (d * BLK)] * N_HEADS for d in A_DILS + (1,)], []), jnp.int32)
    lims = jnp.array([BLK] * (3 * N_HEADS) + [BLK - 1] * N_HEADS, jnp.int32)
    sink_all = jnp.concatenate([jnp.full((3 * N_HEADS,), NEG, F32), sinks.astype(F32)])
    return nbs, lims, sink_all


def _layer_fwd(l, x, xb, w, p, bias_t, rope_cs):
    t = x.shape[0]
    nblk = t // BLK
    n = f"l{l}_"
    s = {"xb": xb}
    proj = _mm(xb, w["win_pt"], tb=True, tn=1280, name=n + "proj")
    s["proj"] = proj

    a = proj[:, COL_A:COL_BQ].reshape(t, 3, 3, N_HEADS, HEAD_DIM)
    qs, ks, vs = [], [], []
    for gi, d in enumerate(A_DILS):
        ag = _perm(a[:, gi], d)
        qs.append(ag[:, 0]); ks.append(ag[:, 1]); vs.append(ag[:, 2])
    bq = proj[:, COL_BQ:COL_BKV].reshape(t, N_HEADS, HEAD_DIM)
    bkv = proj[:, COL_BKV:COL_CQ].reshape(t, 2, 2, HEAD_DIM)
    qs.append(bq); ks.append(jnp.repeat(bkv[:, 0], 4, axis=1)); vs.append(jnp.repeat(bkv[:, 1], 4, axis=1))
    q_all = _heads_blocks(jnp.concatenate(qs, axis=1), BLK)
    k_cat, v_cat = jnp.concatenate(ks, axis=1), jnp.concatenate(vs, axis=1)
    k_all, v_all = _heads_blocks(k_cat, BLK), _heads_blocks(v_cat, BLK)
    kt_all, vt_all = jnp.swapaxes(k_all, -1, -2), jnp.swapaxes(v_all, -1, -2)
    nbs, lims, sink_all = _band_params(t, p["sinks"])
    ot_all, lse_all = _band_fwd(q_all, k_all, vt_all, bias_t, nbs, lims, sink_all, name=n + "band_fwd")
    s.update(q_all=q_all, k_all=k_all, kt_all=kt_all, v_all=v_all, lse_all=lse_all, band=(nbs, lims, sink_all))
    o_perm = _from_blocks_t(ot_all)[:, :, :HEAD_DIM]
    lse_perm = lse_all.reshape(4 * N_HEADS, t).T
    ots = [_to_hdt(_unperm(o_perm[:, gi * 8:(gi + 1) * 8], d)) for gi, d in enumerate(A_DILS)]
    lses = [_unperm(lse_perm[:, gi * 8:(gi + 1) * 8], d).T[:, None, :] for gi, d in enumerate(A_DILS)]
    ot_a = _combine_fwd(ots, lses, name=n + "combine_fwd")
    s.update(ots=ots, lses=lses)
    o_a = jnp.transpose(ot_a, (2, 0, 1)).reshape(t, 512).astype(BF16)
    o_b = o_perm[:, 24:32].reshape(t, 512).astype(BF16)
    s["ot_b"] = _to_hdt(o_perm[:, 24:32])

    rq = _rms_fwd(proj, C_Q_RANK, COL_CQ // C_Q_RANK, p["q_norm_g"], name=n + "rms_q")
    rkv = _rms_fwd(proj, C_KV_RANK, COL_CDKV // C_KV_RANK, p["kv_norm_g"], name=n + "rms_kv")
    q_c = _mm(rq, w["w_uq"], name=n + "uq").reshape(t, N_HEADS, C_NOPE + C_ROPE)
    kv_c = _mm(rkv, w["w_ukv"], name=n + "ukv").reshape(t, N_HEADS, 2 * C_NOPE)
    k_rope = proj[:, COL_CDKV + C_KV_RANK:COL_CDKV + C_KV_RANK + C_ROPE]
    hr = C_ROPE // 2
    x1 = _rope_pack(q_c[:, :, C_NOPE:C_NOPE + hr], k_rope[:, :hr])
    x2 = _rope_pack(q_c[:, :, C_NOPE + hr:], k_rope[:, hr:])
    y1, y2 = _rope(x1, x2, rope_cs[0], rope_cs[1], name=n + "rope")
    qy1, qy2 = y1[:, :LANE].reshape(t, N_HEADS, hr), y2[:, :LANE].reshape(t, N_HEADS, hr)
    ky = jnp.concatenate([y1[:, LANE:LANE + hr], y2[:, LANE:LANE + hr]], axis=1)
    q_full = jnp.concatenate([q_c[:, :, :C_NOPE], qy1, qy2], axis=2)
    k_full = jnp.concatenate([kv_c[:, :, :C_NOPE], jnp.broadcast_to(ky[:, None, :], (t, N_HEADS, C_ROPE))], axis=2)
    qm, km, vm = _heads_blocks(q_full, TQ), _heads_blocks(k_full, TQ), _heads_blocks(kv_c[:, :, C_NOPE:], TQ)
    ot_c, lse_c = _mla_fwd(qm, km, jnp.swapaxes(vm, -1, -2), name=n + "mla_fwd")
    o_c3 = _from_blocks_t(ot_c)[:, :, :HEAD_DIM]
    o_c = o_c3.reshape(t, 512).astype(BF16)
    s.update(rq=rq, rkv=rkv, qm=qm, km=km, vm=vm, lse_c=lse_c, ot_c=_to_hdt(o_c3))

    obs = [o_a, o_b, o_c]
    ys = [_mm(o, w["w_branch"][i], name=n + f"branch{i}") for i, o in enumerate(obs)]
    merged = _merge_fwd(proj, p["b_gate"], ys, name=n + "merge")
    mix = _mm(merged, w["w_out"], name=n + "out")
    x1f, x1b, z1 = _ln_fwd(x, mix, p["ln1_g"], p["ln1_b"], name=n + "ln1")
    s.update(obs=obs, ys=ys, merged=merged, z1=z1, x1b=x1b)

    ug = _mm(x1b, w["wup_g"], tn=1408, name=n + "up_g")
    uv = _mm(x1b, w["wup_v"], tn=1408, name=n + "up_v")
    h = _glu_fwd(ug, uv, w["conv_w"], p["conv_b"], name=n + "glu")
    ff = _mm(h, w["w_ffn_down"], tk=1408, name=n + "down")
    x2f, x2b, z2 = _ln_fwd(x1f, ff, p["ln2_g"], p["ln2_b"], name=n + "ln2")
    s.update(ug=ug, uv=uv, h=h, z2=z2)
    return x2f, x2b, s


def _layer_bwd(l, s, dys, coefs, w, p, bias_t, rope_cs):
    n = f"l{l}b_"
    t = s["z2"].shape[0]
    gw, gs = {}, {}
    tr = lambda a: a.T

    dz2, dz2b, gs["ln2_g"], gs["ln2_b"] = _ln_bwd(s["z2"], p["ln2_g"], dys, coefs, name=n + "ln2")
    dh = _mm(dz2b, w["w_ffn_down"], tb=True, tn=1408, name=n + "d_h")
    gw["w_ffn_down"] = _mm(tr(s["h"]), dz2b, tm=704, name=n + "g_down")
    dcg, dcv, gw["conv_w"], gs["conv_b"] = _glu_bwd_a(s["ug"], s["uv"], w["conv_w"], p["conv_b"], dh, name=n + "glu_a")
    dug = _glu_bwd_b(dcg, w["conv_w"], 0, name=n + "glu_bg")
    duv = _glu_bwd_b(dcv, w["conv_w"], 1, name=n + "glu_bv")
    dx1_g = _mm(dug, w["wup_g"], tb=True, tk=1408, name=n + "d_x1g")
    dx1_v = _mm(duv, w["wup_v"], tb=True, tk=1408, name=n + "d_x1v")
    x1t = tr(s["x1b"])
    gw["w_ffn_up"] = jnp.concatenate([_mm(x1t, dug, tn=1408, name=n + "g_upg"), _mm(x1t, duv, tn=1408, name=n + "g_upv")], axis=1)

    dz1, dz1b, gs["ln1_g"], gs["ln1_b"] = _ln_bwd(s["z1"], p["ln1_g"], [dz2, dx1_g, dx1_v], [ALPHA, 1.0, 1.0], name=n + "ln1")
    dmerged = _mm(dz1b, w["w_out"], tb=True, name=n + "d_merged")
    gw["w_out"] = _mm(tr(s["merged"]), dz1b, name=n + "g_out")
    dys_b, dgp, gs["b_gate"] = _merge_bwd(s["proj"], p["b_gate"], s["ys"], dmerged, name=n + "merge")
    dos = [_mm(dy, w["w_branch"][i], tb=True, name=n + f"d_o{i}") for i, dy in enumerate(dys_b)]
    gw["w_branch"] = jnp.stack([_mm(tr(o), dy, name=n + f"g_branch{i}") for i, (o, dy) in enumerate(zip(s["obs"], dys_b))])

    do_a3 = dos[0].reshape(t, N_HEADS, HEAD_DIM)
    dots, dprs = _combine_bwd(s["ots"], s["lses"], _to_hdt(do_a3), name=n + "combine")
    do_b3 = dos[1].reshape(t, N_HEADS, HEAD_DIM)
    dpr_b = _rowdot(_to_hdt(do_b3), s["ot_b"], name=n + "delta_b")
    do_list = [_perm(jnp.transpose(dt_, (2, 0, 1)), d) for dt_, d in zip(dots, A_DILS)] + [do_b3]
    dpr_list = [_perm(dp_[:, 0, :].T, d) for dp_, d in zip(dprs, A_DILS)] + [dpr_b[:, 0, :].T]
    do_all = _heads_blocks(jnp.concatenate(do_list, axis=1), BLK)
    nblk = t // BLK
    dpr_all = jnp.concatenate(dpr_list, axis=1).T.reshape(4 * N_HEADS, nblk, 1, BLK)
    nbs, lims, sink_all = s["band"]
    dqt, dk, dv, ds_sum, dsink = _band_bwd(s["q_all"], s["k_all"], s["kt_all"], s["v_all"], do_all, s["lse_all"], dpr_all,
                                           bias_t, nbs, lims, sink_all, name=n + "band")
    gs["sinks"] = dsink[3 * N_HEADS:, 0, 0]
    dq_p = _from_blocks_t(dqt)[:, :, :HEAD_DIM]
    dk_p, dv_p = _from_blocks(dk)[:, :, :HEAD_DIM], _from_blocks(dv)[:, :, :HEAD_DIM]
    groups = []
    for gi, d in enumerate(A_DILS):
        sl = slice(gi * 8, (gi + 1) * 8)
        groups.append(jnp.stack([_unperm(dq_p[:, sl], d), _unperm(dk_p[:, sl], d), _unperm(dv_p[:, sl], d)], axis=1))
    da_qkv = jnp.stack(groups, axis=1).reshape(t, 9 * 512).astype(BF16)
    dbq = dq_p[:, 24:32].reshape(t, 512).astype(BF16)
    kv_terms = []
    for part in (dk_p, dv_p):
        hm = jnp.transpose(part[:, 24:32], (1, 0, 2)).reshape(2, 4, t, HEAD_DIM)
        kv_terms.append([hm[:, i] for i in range(4)])
    dkb = _sum_rows(kv_terms[0], tm=1024, name=n + "gqa_dk", dtype=BF16)
    dvb = _sum_rows(kv_terms[1], tm=1024, name=n + "gqa_dv", dtype=BF16)
    dbkv = jnp.transpose(jnp.stack([dkb, dvb]), (2, 0, 1, 3)).reshape(t, 256)

    do_c3 = dos[2].reshape(t, N_HEADS, HEAD_DIM)
    delta_c = _rowdot(_to_hdt(do_c3), s["ot_c"], name=n + "delta_c").reshape(N_HEADS, t // TQ, 1, TQ)
    dqt_c, dk_c, dv_c = _mla_bwd(s["qm"], s["km"], jnp.swapaxes(s["km"], -1, -2), s["vm"], _heads_blocks(do_c3, TQ),
                                 s["lse_c"], delta_c, name=n + "mla")
    dq_full, dk_full, dv_full = _from_blocks_t(dqt_c), _from_blocks(dk_c), _from_blocks(dv_c)
    hr = C_ROPE // 2
    dk_sum = _sum_rows([dk_full[None, :, hh, :] for hh in range(N_HEADS)], tm=1024, name=n + "krope_sum")[0]
    dy1 = _rope_pack(dq_full[:, :, C_NOPE:C_NOPE + hr], dk_sum[:, C_NOPE:C_NOPE + hr])
    dy2 = _rope_pack(dq_full[:, :, C_NOPE + hr:C_NOPE + C_ROPE], dk_sum[:, C_NOPE + hr:C_NOPE + C_ROPE])
    dx1, dx2 = _rope(dy1, dy2, rope_cs[0], -rope_cs[1], name=n + "rope")
    dq_c = jnp.concatenate([dq_full[:, :, :C_NOPE], dx1[:, :LANE].reshape(t, N_HEADS, hr), dx2[:, :LANE].reshape(t, N_HEADS, hr)],
                           axis=2).reshape(t, 768).astype(BF16)
    dkv_c = jnp.concatenate([dk_full[:, :, :C_NOPE], dv_full[:, :, :C_NOPE]], axis=2).reshape(t, 1024).astype(BF16)
    d_rq = _mm(dq_c, w["w_uq"], tb=True, name=n + "d_rq")
    d_rkv = _mm(dkv_c, w["w_ukv"], tb=True, name=n + "d_rkv")
    gw["w_uq"] = _mm(tr(s["rq"]), dq_c, name=n + "g_uq")
    gw["w_ukv"] = _mm(tr(s["rkv"]), dkv_c, name=n + "g_ukv")
    dcq, gs["q_norm_g"] = _rms_bwd(s["proj"], C_Q_RANK, COL_CQ // C_Q_RANK, p["q_norm_g"], d_rq, name=n + "rms_q")
    dckv, gs["kv_norm_g"] = _rms_bwd(s["proj"], C_KV_RANK, COL_CDKV // C_KV_RANK, p["kv_norm_g"], d_rkv, name=n + "rms_kv")
    dcdkv = jnp.concatenate([dckv, dx1[:, LANE:LANE + hr].astype(BF16), dx2[:, LANE:LANE + hr].astype(BF16),
                             jnp.zeros((t, D_IN_P - D_IN), BF16)], axis=1)

    dproj = jnp.concatenate(dgp + [da_qkv, dbq, dbkv, dcq, dcdkv], axis=1)
    dx_proj = _mm(dproj, w["win_pt"], tk=1280, name=n + "d_x")
    gw["w_in"] = _mm(tr(s["xb"]), dproj, tn=1280, name=n + "g_in")
    return [dz1, dx_proj], [ALPHA, 1.0], gw, gs, ds_sum


def _local_step(x, target, ws, rel_table, small):
    t = x.shape[0]
    ps = [{k: small[k][l] for k in SMALL} for l in range(DEPTH)]
    bucket = _bucket_index()
    bias_t = _bias_lookup(bucket, rel_table.T, name="bias_lookup").reshape(4 * N_HEADS, 2 * BLK, BLK)
    rope_cs = _rope_tables(t)

    saved, h, hb = [], x, x.astype(BF16)
    for l in range(DEPTH):
        h, hb, s = _layer_fwd(l, h, hb, ws[l], ps[l], bias_t, rope_cs)
        saved.append(s)
    dy, loss_part = _loss_and_grad(h, target, name="loss")

    dys, coefs = [dy], [1.0]
    gws, gss, dss = [None] * DEPTH, [None] * DEPTH, [None] * DEPTH
    for l in reversed(range(DEPTH)):
        dys, coefs, gws[l], gss[l], dss[l] = _layer_bwd(l, saved[l], dys, coefs, ws[l], ps[l], bias_t, rope_cs)
    grad_x = _axpy(coefs[0], dys[0], dys[1], name="grad_x")
    npos = 2 * BLK * BLK
    g_rel = _bias_grad(bucket, dss[0].reshape(4 * N_HEADS, npos), dss[1].reshape(4 * N_HEADS, npos), name="bias_grad").T
    gsmall = {k: jnp.stack([gss[l][k] for l in range(DEPTH)]) for k in SMALL}
    return loss_part, grad_x, gws, gsmall, g_rel


def kernel(x, rel_table, w_in, b_gate, sinks, q_norm_g, kv_norm_g, w_uq, w_ukv, w_branch, w_out, ln1_g, ln1_b, w_ffn_up, conv_w, conv_b, w_ffn_down, ln2_g, ln2_b, loss_target, m_rel_table, m_w_in, m_b_gate, m_sinks, m_q_norm_g, m_kv_norm_g, m_w_uq, m_w_ukv, m_w_branch, m_w_out, m_ln1_g, m_ln1_b, m_w_ffn_up, m_conv_w, m_conv_b, m_w_ffn_down, m_ln2_g, m_ln2_b, v_rel_table, v_w_in, v_b_gate, v_sinks, v_q_norm_g, v_kv_norm_g, v_w_uq, v_w_ukv, v_w_branch, v_w_out, v_ln1_g, v_ln1_b, v_w_ffn_up, v_conv_w, v_conv_b, v_w_ffn_down, v_ln2_g, v_ln2_b):
    wts = dict(rel_table=rel_table, w_in=w_in, b_gate=b_gate, sinks=sinks, q_norm_g=q_norm_g, kv_norm_g=kv_norm_g, w_uq=w_uq,
               w_ukv=w_ukv, w_branch=w_branch, w_out=w_out, ln1_g=ln1_g, ln1_b=ln1_b, w_ffn_up=w_ffn_up, conv_w=conv_w,
               conv_b=conv_b, w_ffn_down=w_ffn_down, ln2_g=ln2_g, ln2_b=ln2_b)
    ms = dict(rel_table=m_rel_table, w_in=m_w_in, b_gate=m_b_gate, sinks=m_sinks, q_norm_g=m_q_norm_g, kv_norm_g=m_kv_norm_g,
              w_uq=m_w_uq, w_ukv=m_w_ukv, w_branch=m_w_branch, w_out=m_w_out, ln1_g=m_ln1_g, ln1_b=m_ln1_b, w_ffn_up=m_w_ffn_up,
              conv_w=m_conv_w, conv_b=m_conv_b, w_ffn_down=m_w_ffn_down, ln2_g=m_ln2_g, ln2_b=m_ln2_b)
    vs = dict(rel_table=v_rel_table, w_in=v_w_in, b_gate=v_b_gate, sinks=v_sinks, q_norm_g=v_q_norm_g, kv_norm_g=v_kv_norm_g,
              w_uq=v_w_uq, w_ukv=v_w_ukv, w_branch=v_w_branch, w_out=v_w_out, ln1_g=v_ln1_g, ln1_b=v_ln1_b, w_ffn_up=v_w_ffn_up,
              conv_w=v_conv_w, conv_b=v_conv_b, w_ffn_down=v_w_ffn_down, ln2_g=v_ln2_g, ln2_b=v_ln2_b)

    core = lax.axis_index("c")
    chip = 2 * lax.axis_index("x") + lax.axis_index("y")

    names = [name for name, _, _ in MATS]
    sent = [_weight_send(name, wts[name]) for name in names]
    got = _allgather_weights(sent)
    gathered = {name: lax.dynamic_update_slice(g, s[None], (chip,) + (0,) * s.ndim) for name, g, s in zip(names, got, sent)}
    ws = [_full_weights(gathered, l) for l in range(DEPTH)]

    small = {k: wts[k] for k in SMALL}
    loss_part, grad_x, gws, gsmall, g_rel = _local_step(x[0], loss_target[0], ws, rel_table, small)

    rnames = [name for name, _, _ in REDUCED]
    gsend = [jnp.stack([_grad_send(name, gws[l][name], shape, ax) for l in range(DEPTH)]) for name, shape, ax in REDUCED]
    theirs = _sibling_swap(gsend)
    pairs = [_pair_add(g, t_, core, name="grad_pair_" + name) for name, g, t_ in zip(rnames, gsend, theirs)]
    arrived = _chip_scatter(pairs)
    reduced = [_chip_add(lax.dynamic_index_in_dim(p, chip, 0, keepdims=False), a, name="grad_chip_" + name)
               for name, p, a in zip(rnames, pairs, arrived)]
    others = _sibling_share(reduced)
    gshard = {}
    for name, mine, other in zip(rnames, reduced, others):
        layers = [jnp.where(core == l, mine, other) for l in range(DEPTH)]
        gshard[name] = jnp.stack([_grad_recv(name, a) for a in layers])

    conv_w_full = jnp.stack([gws[l]["conv_w"] for l in range(DEPTH)])
    small_red = _allreduce_small(_pack_small(g_rel, gsmall, conv_w_full, loss_part))
    g_rel_r, gsmall_r, conv_w_r, loss_vec = _unpack_small(small_red)
    loss = loss_vec[0]
    shard_w = 2 * D_FF // N_CHIP
    gshard["conv_w"] = lax.dynamic_slice_in_dim(conv_w_r, chip * shard_w, shard_w, axis=2)

    grads = dict(gshard)
    grads.update(gsmall_r)
    grads["rel_table"] = g_rel_r
    deltas, new_m, new_v = {}, {}, {}
    for name, _, _ in MATS:
        shp = wts[name].shape
        v2 = lambda a: a.reshape(-1, shp[-1])
        d_, m_, v_ = _adamw(v2(wts[name]), v2(grads[name]), v2(ms[name]), v2(vs[name]), name="adamw_" + name)
        deltas[name], new_m[name], new_v[name] = d_.reshape(shp), m_.reshape(shp), v_.reshape(shp)
    zero, none = jnp.zeros((LANE,), F32), jnp.zeros((0,), F32)
    sw = _pack_small(wts["rel_table"], {k: wts[k] for k in SMALL}, none, zero)
    sm = _pack_small(ms["rel_table"], {k: ms[k] for k in SMALL}, none, zero)
    sv = _pack_small(vs["rel_table"], {k: vs[k] for k in SMALL}, none, zero)
    sg = _pack_small(g_rel_r, gsmall_r, none, zero)
    sd, smn, svn = _adamw(sw, sg, sm, sv, name="adamw_small")
    for res, buf in ((deltas, sd), (new_m, smn), (new_v, svn)):
        rel_, sm_ = _unpack_small(jnp.pad(buf, ((0, small_red.shape[0] - buf.shape[0]), (0, 0))))[:2]
        res["rel_table"] = rel_
        res.update(sm_)

    return (loss, grad_x[None], *[grads[k] for k in WEIGHT_ORDER], *[deltas[k] for k in WEIGHT_ORDER],
            *[new_m[k] for k in WEIGHT_ORDER], *[new_v[k] for k in WEIGHT_ORDER])
```

```python
import math

import jax
import jax.numpy as jnp
from jax import lax
from jax.experimental import pallas as pl
from jax.experimental.pallas import tpu as pltpu

F32 = jnp.float32
BF16 = jnp.bfloat16
MESH = pl.DeviceIdType.MESH

D_MODEL = 1024
DEPTH = 2
HEAD_DIM = 64
N_HEADS = 8
A_DILS = (1, 4, 16)
C_Q_RANK = 256
C_KV_RANK = 128
C_NOPE = 64
C_ROPE = 32
ROPE_BASE = 10000.0
REL_BUCKETS = 32
REL_MAX_DIST = 2048
D_FF = 2816
ALPHA = (2 * DEPTH) ** 0.25
LN_EPS = 1e-5
RMS_EPS = 1e-6
NEG = -1e30
ADAM_LR, ADAM_B1, ADAM_B2, ADAM_EPS, ADAM_WD, ADAM_STEP = 0.001, 0.9, 0.999, 1e-08, 0.01, 10

VMEM_LIMIT_BYTES = 56 * 1024 * 1024
LANE = 128
BLK = 128
TQ = 512
HP = 128
BAND_UNROLL = 16

COL_GATE, COL_A, COL_BQ, COL_BKV, COL_CQ, COL_CDKV, D_IN_P = 0, 3072, 7680, 8192, 8448, 8704, 8960
D_IN = 8864
ORIG_GATE = 5792

N_CHIP = 4
MATS = (
    ("w_in", (D_MODEL, D_IN), 1),
    ("w_uq", (C_Q_RANK, 768), 1),
    ("w_ukv", (C_KV_RANK, 1024), 1),
    ("w_branch", (3, 512, D_MODEL), 2),
    ("w_out", (D_MODEL, D_MODEL), 0),
    ("w_ffn_up", (D_MODEL, 2 * D_FF), 1),
    ("conv_w", (3, 2 * D_FF), 1),
    ("w_ffn_down", (D_FF, D_MODEL), 0),
)
SMALL = ("b_gate", "sinks", "q_norm_g", "kv_norm_g", "ln1_g", "ln1_b", "conv_b", "ln2_g", "ln2_b")
SMALL_SIZES = {"b_gate": 3072, "sinks": 8, "q_norm_g": 256, "kv_norm_g": 128, "ln1_g": 1024, "ln1_b": 1024,
               "conv_b": 5632, "ln2_g": 1024, "ln2_b": 1024}
WEIGHT_ORDER = ("rel_table", "w_in", "b_gate", "sinks", "q_norm_g", "kv_norm_g", "w_uq", "w_ukv", "w_branch",
                "w_out", "ln1_g", "ln1_b", "w_ffn_up", "conv_w", "conv_b", "w_ffn_down", "ln2_g", "ln2_b")


def _cparams(sem):
    return pltpu.CompilerParams(dimension_semantics=sem, vmem_limit_bytes=VMEM_LIMIT_BYTES)


def _shard_shape(shape, ax):
    s = list(shape)
    s[ax] //= N_CHIP
    return tuple(s)


def _ceil_to(n, m):
    return -(-n // m) * m


def _pick(n, target):
    if n <= target:
        return n
    best = None
    for t in range(LANE, target + 1, LANE):
        if n % t == 0:
            best = t
    assert best is not None, (n, target)
    return best


def _mm(a, b, *, tb=False, out_dtype=F32, tm=512, tn=1024, tk=2048, name):
    m, k = a.shape
    n = b.shape[0] if tb else b.shape[1]
    assert (b.shape[1] if tb else b.shape[0]) == k
    tm, tn, tk = _pick(m, tm), _pick(n, tn), _pick(k, tk)
    nk = k // tk
    dn = (((1,), (1,)), ((), ())) if tb else (((1,), (0,)), ((), ()))

    def body(a_ref, b_ref, o_ref, acc_ref):
        part = lax.dot_general(a_ref[...].astype(BF16), b_ref[...].astype(BF16), dn, preferred_element_type=F32)
        if nk == 1:
            o_ref[...] = part.astype(o_ref.dtype)
        else:
            kk = pl.program_id(2)

            @pl.when(kk == 0)
            def _():
                acc_ref[...] = part

            @pl.when(kk > 0)
            def _():
                acc_ref[...] += part

            @pl.when(kk == nk - 1)
            def _():
                o_ref[...] = acc_ref[...].astype(o_ref.dtype)

    b_spec = pl.BlockSpec((tn, tk), lambda i, j, kk: (j, kk)) if tb else pl.BlockSpec((tk, tn), lambda i, j, kk: (kk, j))
    return pl.pallas_call(
        body, name=name, grid=(m // tm, n // tn, nk),
        in_specs=[pl.BlockSpec((tm, tk), lambda i, j, kk: (i, kk)), b_spec],
        out_specs=pl.BlockSpec((tm, tn), lambda i, j, kk: (i, j)),
        out_shape=jax.ShapeDtypeStruct((m, n), out_dtype),
        scratch_shapes=[pltpu.VMEM((tm, tn) if nk > 1 else (8, LANE), F32)],
        compiler_params=_cparams(("parallel", "parallel", "arbitrary")),
    )(a, b)


def _rowwise(fn, rows, *, pars=(), halos=(), outs=(), accs=(), tm, name, ncol=1, t=None):
    nb = rows[0][0].shape[0]
    t = rows[0][0].shape[1] if t is None else t
    tm = min(tm, t)
    assert t % tm == 0 and tm % 8 == 0
    nt = t // tm
    in_specs, args = [], []
    for spec in rows:
        arr, c, off = spec[:3]
        rb = spec[3] if len(spec) > 3 else 0
        in_specs.append(pl.BlockSpec((1, tm, c), lambda b, cc, i, off=off, rb=rb: (b, i + rb, off + cc)))
        args.append(arr)
    for arr, c, off, kind in halos:
        if kind == "prev":
            im = lambda b, cc, i, off=off: (b, jnp.maximum(i * (tm // 8) - 1, 0), off + cc)
        else:
            im = lambda b, cc, i, off=off: (b, jnp.minimum((i + 1) * (tm // 8), t // 8 - 1), off + cc)
        in_specs.append(pl.BlockSpec((1, 8, c), im))
        args.append(arr)
    for arr, c, off in pars:
        bp, r = arr.shape[:2]
        if bp > 1:
            im = lambda b, cc, i, off=off: (b, 0, off + cc)
        else:
            im = lambda b, cc, i, off=off: (0, 0, off + cc)
        in_specs.append(pl.BlockSpec((1, r, c), im))
        args.append(arr)
    out_specs, out_shapes = [], []
    for ctot, c, off, dt in outs:
        out_specs.append(pl.BlockSpec((1, tm, c), lambda b, cc, i, off=off: (b, i, off + cc)))
        out_shapes.append(jax.ShapeDtypeStruct((nb, t, ctot), dt))
    for r, ctot, c, off in accs:
        out_specs.append(pl.BlockSpec((1, r, c), lambda b, cc, i, off=off: (b, 0, off + cc)))
        out_shapes.append(jax.ShapeDtypeStruct((nb, r, ctot), F32))
    n_in, n_out = len(args), len(outs)

    def body(*refs):
        i = pl.program_id(2)
        res = fn(i, nt, *[r[0] for r in refs[:n_in]])
        if not isinstance(res, (tuple, list)):
            res = (res,)
        for o_ref, val in zip(refs[n_in:n_in + n_out], res[:n_out]):
            o_ref[0] = val.astype(o_ref.dtype)
        for a_ref, val in zip(refs[n_in + n_out:], res[n_out:]):
            @pl.when(i == 0)
            def _(a_ref=a_ref, val=val):
                a_ref[0] = val

            @pl.when(i > 0)
            def _(a_ref=a_ref, val=val):
                a_ref[0] += val

    res = pl.pallas_call(
        body, name=name, grid=(nb, ncol, nt), in_specs=in_specs, out_specs=out_specs, out_shape=out_shapes,
        compiler_params=_cparams(("parallel", "parallel", "arbitrary")),
    )(*args)
    return res


def _lanewise(fn, ins, outs, *, tl, name):
    nb, _, t = ins[0].shape
    tl = min(tl, t)
    assert t % tl == 0
    n_in = len(ins)

    def body(*refs):
        res = fn(*[r[0] for r in refs[:n_in]])
        if not isinstance(res, (tuple, list)):
            res = (res,)
        for o_ref, val in zip(refs[n_in:], res):
            o_ref[0] = val.astype(o_ref.dtype)

    return pl.pallas_call(
        body, name=name, grid=(nb, t // tl),
        in_specs=[pl.BlockSpec((1, a.shape[1], tl), lambda b, i: (b, 0, i)) for a in ins],
        out_specs=[pl.BlockSpec((1, r, tl), lambda b, i: (b, 0, i)) for r, _ in outs],
        out_shape=[jax.ShapeDtypeStruct((nb, r, t), dt) for r, dt in outs],
        compiler_params=_cparams(("parallel", "parallel")),
    )(*ins)


def _dot(a, b):
    return lax.dot_general(a, b, (((1,), (0,)), ((), ())), preferred_element_type=F32)


def _dot_nt(a, b):
    return lax.dot_general(a, b, (((1,), (1,)), ((), ())), preferred_element_type=F32)


def _lanes(parts):
    return jnp.concatenate(parts, axis=1)


def _first_row(b, nb):
    return jnp.zeros((1, BLK), F32) + jnp.where(lax.rem(b, nb) == 0, NEG, 0.0).astype(F32)


def _band_consts(lim, b_ref):
    ki = lax.broadcasted_iota(jnp.int32, (BLK, BAND_UNROLL * BLK), 0)
    qi = jnp.bitwise_and(lax.broadcasted_iota(jnp.int32, (BLK, BAND_UNROLL * BLK), 1), BLK - 1)
    tile = lambda a: _lanes([a] * BAND_UNROLL)
    return ki <= qi, (BLK + qi - ki) <= lim, tile(b_ref[0, BLK:2 * BLK, :]), tile(b_ref[0, 0:BLK, :])


def _band_fwd(q, k, vt, bias_t, nbs, lims, sinks, *, name):
    nh, nblk = q.shape[:2]
    scale = HEAD_DIM ** -0.5

    def body(nb_ref, lim_ref, sink_ref, q_ref, k_ref, vt_ref, b_ref, ot_ref, lse_ref):
        h = pl.program_id(0)
        nb, lim, sink = nb_ref[h], lim_ref[h], sink_ref[h]
        cur_ok, prev_ok, b_cur, b_prev = _band_consts(lim, b_ref)

        def group(g, carry):
            bs = [g * BAND_UNROLL + i for i in range(BAND_UNROLL)]
            ps = [jnp.maximum(b - 1, 0) for b in bs]
            qs = [q_ref[0, b] for b in bs]
            first = _lanes([_first_row(b, nb) for b in bs])
            sc = jnp.where(cur_ok, _lanes([_dot_nt(k_ref[0, b], qb) for b, qb in zip(bs, qs)]) * scale + b_cur, NEG)
            sp = jnp.where(prev_ok, _lanes([_dot_nt(k_ref[0, p], qb) for p, qb in zip(ps, qs)]) * scale + b_prev, NEG) + first
            m = jnp.maximum(jnp.maximum(jnp.max(sc, axis=0, keepdims=True), jnp.max(sp, axis=0, keepdims=True)), sink)
            pc, pp = jnp.exp(sc - m), jnp.exp(sp - m)
            l = jnp.sum(pc, axis=0, keepdims=True) + jnp.sum(pp, axis=0, keepdims=True) + jnp.exp(sink - m)
            inv = 1.0 / l
            pc_b, pp_b, lse = (pc * inv).astype(BF16), (pp * inv).astype(BF16), m + jnp.log(l)
            for i, (b, p) in enumerate(zip(bs, ps)):
                sl = slice(i * BLK, (i + 1) * BLK)
                ot_ref[0, b] = _dot(vt_ref[0, b], pc_b[:, sl]) + _dot(vt_ref[0, p], pp_b[:, sl])
                lse_ref[0, b] = lse[:, sl]
            return carry

        lax.fori_loop(0, nblk // BAND_UNROLL, group, 0)

    assert nblk % BAND_UNROLL == 0
    smem = pl.BlockSpec(memory_space=pltpu.SMEM)
    whole = lambda shp: pl.BlockSpec((1,) + shp, lambda h: (h,) + (0,) * len(shp))
    return pl.pallas_call(
        body, name=name, grid=(nh,),
        in_specs=[smem, smem, smem, whole((nblk, BLK, HP)), whole((nblk, BLK, HP)), whole((nblk, HP, BLK)),
                  whole((2 * BLK, BLK))],
        out_specs=[whole((nblk, HP, BLK)), whole((nblk, 1, BLK))],
        out_shape=[jax.ShapeDtypeStruct((nh, nblk, HP, BLK), F32), jax.ShapeDtypeStruct((nh, nblk, 1, BLK), F32)],
        compiler_params=_cparams(("parallel",)),
    )(nbs, lims, sinks, q, k, vt, bias_t)


def _band_bwd(q, k, kt, v, do, lse, dpr, bias_t, nbs, lims, sinks, *, name):
    nh, nblk = q.shape[:2]
    scale = HEAD_DIM ** -0.5

    def body(nb_ref, lim_ref, sink_ref, q_ref, k_ref, kt_ref, v_ref, do_ref, lse_ref, dpr_ref, b_ref,
             dqt_ref, dk_ref, dv_ref, ds_ref, dsink_ref, sink_acc):
        h = pl.program_id(0)
        nb, lim, sink = nb_ref[h], lim_ref[h], sink_ref[h]
        cur_ok, prev_ok, b_cur, b_prev = _band_consts(lim, b_ref)
        ds_ref[...] = jnp.zeros_like(ds_ref)
        sink_acc[...] = jnp.zeros_like(sink_acc)

        def fold(a):
            acc = a[:, 0:BLK]
            for i in range(1, BAND_UNROLL):
                acc = acc + a[:, i * BLK:(i + 1) * BLK]
            return acc

        def group(g, carry):
            bs = [g * BAND_UNROLL + i for i in range(BAND_UNROLL)]
            ps = [jnp.maximum(b - 1, 0) for b in bs]
            qs, dos = [q_ref[0, b] for b in bs], [do_ref[0, b] for b in bs]
            lse_row, dpr_row = _lanes([lse_ref[0, b] for b in bs]), _lanes([dpr_ref[0, b] for b in bs])
            first = _lanes([_first_row(b, nb) for b in bs])
            sc = jnp.where(cur_ok, _lanes([_dot_nt(k_ref[0, b], qb) for b, qb in zip(bs, qs)]) * scale + b_cur, NEG)
            sp = jnp.where(prev_ok, _lanes([_dot_nt(k_ref[0, p], qb) for p, qb in zip(ps, qs)]) * scale + b_prev, NEG) + first
            pc, pp = jnp.exp(sc - lse_row), jnp.exp(sp - lse_row)
            dsc = pc * (_lanes([_dot_nt(v_ref[0, b], dob) for b, dob in zip(bs, dos)]) - dpr_row)
            dsp = pp * (_lanes([_dot_nt(v_ref[0, p], dob) for p, dob in zip(ps, dos)]) - dpr_row)
            ds_ref[0, BLK:2 * BLK, :] += fold(dsc)
            ds_ref[0, 0:BLK, :] += fold(dsp)
            sink_acc[...] += fold(jnp.exp(sink - lse_row) * dpr_row)
            dsc_b, dsp_b = (dsc * scale).astype(BF16), (dsp * scale).astype(BF16)
            pc_b, pp_b = pc.astype(BF16), pp.astype(BF16)
            for i, (b, p) in enumerate(zip(bs, ps)):
                sl = slice(i * BLK, (i + 1) * BLK)
                dqt_ref[0, b] = _dot(kt_ref[0, b], dsc_b[:, sl]) + _dot(kt_ref[0, p], dsp_b[:, sl])
                dk_ref[0, b] = _dot(dsc_b[:, sl], qs[i])
                dv_ref[0, b] = _dot(pc_b[:, sl], dos[i])
                dk_ref[0, p] += _dot(dsp_b[:, sl], qs[i])
                dv_ref[0, p] += _dot(pp_b[:, sl], dos[i])
            return carry

        lax.fori_loop(0, nblk // BAND_UNROLL, group, 0)
        dsink_ref[0] = jnp.zeros((1, BLK), F32) - jnp.sum(sink_acc[...], axis=1, keepdims=True)

    smem = pl.BlockSpec(memory_space=pltpu.SMEM)
    whole = lambda shp: pl.BlockSpec((1,) + shp, lambda h: (h,) + (0,) * len(shp))
    td, dt, st = (nblk, BLK, HP), (nblk, HP, BLK), (nblk, 1, BLK)
    return pl.pallas_call(
        body, name=name, grid=(nh,),
        in_specs=[smem, smem, smem, whole(td), whole(td), whole(dt), whole(td), whole(td), whole(st), whole(st),
                  whole((2 * BLK, BLK))],
        out_specs=[whole(dt), whole(td), whole(td), whole((2 * BLK, BLK)), whole((1, BLK))],
        out_shape=[jax.ShapeDtypeStruct((nh,) + dt, F32), jax.ShapeDtypeStruct((nh,) + td, F32),
                   jax.ShapeDtypeStruct((nh,) + td, F32), jax.ShapeDtypeStruct((nh, 2 * BLK, BLK), F32),
                   jax.ShapeDtypeStruct((nh, 1, BLK), F32)],
        scratch_shapes=[pltpu.VMEM((1, BLK), F32)],
        compiler_params=_cparams(("parallel",)),
    )(nbs, lims, sinks, q, k, kt, v, do, lse, dpr, bias_t)


def _diag_mask():
    return lax.broadcasted_iota(jnp.int32, (TQ, TQ), 0) <= lax.broadcasted_iota(jnp.int32, (TQ, TQ), 1)


def _mla_fwd(q, k, vt, *, name):
    nh, n = q.shape[:2]
    scale = (C_NOPE + C_ROPE) ** -0.5

    def body(q_ref, k_ref, vt_ref, ot_ref, lse_ref, m_ref, l_ref, acc_ref):
        def qloop(qi, carry):
            qb = q_ref[0, qi]
            m_ref[...] = jnp.full_like(m_ref, NEG)
            l_ref[...] = jnp.zeros_like(l_ref)
            acc_ref[...] = jnp.zeros_like(acc_ref)

            def step(kj, diagonal):
                s = _dot_nt(k_ref[0, kj], qb) * scale
                if diagonal:
                    s = jnp.where(_diag_mask(), s, NEG)
                m_prev = m_ref[...]
                m_new = jnp.maximum(m_prev, jnp.max(s, axis=0, keepdims=True))
                a = jnp.exp(m_prev - m_new)
                p = jnp.exp(s - m_new)
                l_ref[...] = a * l_ref[...] + jnp.sum(p, axis=0, keepdims=True)
                acc_ref[...] = a * acc_ref[...] + _dot(vt_ref[0, kj], p.astype(BF16))
                m_ref[...] = m_new

            def kloop(kj, c2):
                step(kj, False)
                return c2

            lax.fori_loop(0, qi, kloop, 0)
            step(qi, True)
            ot_ref[0, qi] = acc_ref[...] * (1.0 / l_ref[...])
            lse_ref[0, qi] = m_ref[...] + jnp.log(l_ref[...])
            return carry

        lax.fori_loop(0, n, qloop, 0)

    whole = lambda shp: pl.BlockSpec((1,) + shp, lambda h: (h,) + (0,) * len(shp))
    return pl.pallas_call(
        body, name=name, grid=(nh,),
        in_specs=[whole((n, TQ, HP)), whole((n, TQ, HP)), whole((n, HP, TQ))],
        out_specs=[whole((n, HP, TQ)), whole((n, 1, TQ))],
        out_shape=[jax.ShapeDtypeStruct((nh, n, HP, TQ), F32), jax.ShapeDtypeStruct((nh, n, 1, TQ), F32)],
        scratch_shapes=[pltpu.VMEM((1, TQ), F32), pltpu.VMEM((1, TQ), F32), pltpu.VMEM((HP, TQ), F32)],
        compiler_params=_cparams(("parallel",)),
    )(q, k, vt)


def _mla_bwd(q, k, kt, v, do, lse, delta, *, name):
    nh, n = q.shape[:2]
    scale = (C_NOPE + C_ROPE) ** -0.5

    def body(q_ref, k_ref, kt_ref, v_ref, do_ref, lse_ref, dl_ref, dqt_ref, dk_ref, dv_ref, dk_acc, dv_acc):
        dqt_ref[...] = jnp.zeros_like(dqt_ref)

        def kloop(kj, carry):
            kb, vb, ktb = k_ref[0, kj], v_ref[0, kj], kt_ref[0, kj]
            dk_acc[...] = jnp.zeros_like(dk_acc)
            dv_acc[...] = jnp.zeros_like(dv_acc)

            def step(qi, diagonal):
                qb, dob = q_ref[0, qi], do_ref[0, qi]
                s = _dot_nt(kb, qb) * scale
                if diagonal:
                    s = jnp.where(_diag_mask(), s, NEG)
                p = jnp.exp(s - lse_ref[0, qi])
                ds = (p * (_dot_nt(vb, dob) - dl_ref[0, qi]) * scale).astype(BF16)
                dv_acc[...] += _dot(p.astype(BF16), dob)
                dk_acc[...] += _dot(ds, qb)
                dqt_ref[0, qi] += _dot(ktb, ds)

            def qloop(qi, c2):
                step(qi, False)
                return c2

            step(kj, True)
            lax.fori_loop(kj + 1, n, qloop, 0)
            dk_ref[0, kj] = dk_acc[...]
            dv_ref[0, kj] = dv_acc[...]
            return carry

        lax.fori_loop(0, n, kloop, 0)

    whole = lambda shp: pl.BlockSpec((1,) + shp, lambda h: (h,) + (0,) * len(shp))
    td, dt, st = (n, TQ, HP), (n, HP, TQ), (n, 1, TQ)
    return pl.pallas_call(
        body, name=name, grid=(nh,),
        in_specs=[whole(td), whole(td), whole(dt), whole(td), whole(td), whole(st), whole(st)],
        out_specs=[whole(dt), whole(td), whole(td)],
        out_shape=[jax.ShapeDtypeStruct((nh,) + dt, F32), jax.ShapeDtypeStruct((nh,) + td, F32),
                   jax.ShapeDtypeStruct((nh,) + td, F32)],
        scratch_shapes=[pltpu.VMEM((TQ, HP), F32), pltpu.VMEM((TQ, HP), F32)],
        compiler_params=_cparams(("parallel",)),
    )(q, k, kt, v, do, lse, delta)


def _bias_lookup(bucket, table_t, *, name):
    nh, npos = bucket.shape
    tp = 4096

    def body(b_ref, t_ref, o_ref):
        bk, tab = b_ref[...], t_ref[...]
        acc = jnp.zeros(bk.shape, F32)
        for i in range(REL_BUCKETS):
            acc = jnp.where(bk == i, tab[:, i:i + 1], acc)
        o_ref[...] = acc

    return pl.pallas_call(
        body, name=name, grid=(npos // tp,),
        in_specs=[pl.BlockSpec((nh, tp), lambda i: (0, i)), pl.BlockSpec((nh, REL_BUCKETS), lambda i: (0, 0))],
        out_specs=pl.BlockSpec((nh, tp), lambda i: (0, i)),
        out_shape=jax.ShapeDtypeStruct((nh, npos), F32),
        compiler_params=_cparams(("parallel",)),
    )(bucket, table_t)


def _bias_grad(bucket, ds0, ds1, *, name):
    nh, npos = bucket.shape
    tp = 4096

    def body(b_ref, a_ref, c_ref, o_ref):
        i = pl.program_id(0)
        bk, ds = b_ref[...], a_ref[...] + c_ref[...]
        lane = lax.broadcasted_iota(jnp.int32, (nh, REL_BUCKETS), 1)
        acc = jnp.zeros((nh, REL_BUCKETS), F32)
        for j in range(REL_BUCKETS):
            col = jnp.sum(jnp.where(bk == j, ds, 0.0), axis=1, keepdims=True)
            acc = acc + jnp.where(lane == j, col, 0.0)

        @pl.when(i == 0)
        def _():
            o_ref[...] = acc

        @pl.when(i > 0)
        def _():
            o_ref[...] += acc

    return pl.pallas_call(
        body, name=name, grid=(npos // tp,),
        in_specs=[pl.BlockSpec((nh, tp), lambda i: (0, i))] * 3,
        out_specs=pl.BlockSpec((nh, REL_BUCKETS), lambda i: (0, 0)),
        out_shape=jax.ShapeDtypeStruct((nh, REL_BUCKETS), F32),
        compiler_params=_cparams(("arbitrary",)),
    )(bucket, ds0, ds1)


def _t5_bucket(dist):
    n = jnp.maximum(dist, 0)
    max_exact = REL_BUCKETS // 2
    scaled = jnp.log(jnp.maximum(n, 1).astype(F32) / max_exact) / math.log(REL_MAX_DIST / max_exact)
    large = max_exact + (scaled * (REL_BUCKETS - max_exact)).astype(jnp.int32)
    return jnp.where(n < max_exact, n, jnp.minimum(large, REL_BUCKETS - 1))


def _bucket_index():
    ci = jnp.arange(2 * BLK)[:, None]
    qi = jnp.arange(BLK)[None, :]
    step = BLK + qi - ci
    per_group = [_t5_bucket(step * d).reshape(1, -1) for d in A_DILS + (1,)]
    return jnp.concatenate([jnp.tile(b, (N_HEADS, 1)) for b in per_group], axis=0).astype(jnp.int32)


def _sigmoid(x):
    return 1.0 / (1.0 + jnp.exp(-x))


def _ln_stats(z):
    mu = jnp.mean(z, axis=-1, keepdims=True)
    zc = z - mu
    var = jnp.mean(zc * zc, axis=-1, keepdims=True)
    return zc * lax.rsqrt(var + LN_EPS)


def _ln_fwd(x, mix, g, b, *, name):
    def fn(i, nt, xv, mv, gv, bv):
        z = ALPHA * xv + mv
        y = _ln_stats(z) * gv + bv
        return y, y, z

    c = x.shape[-1]
    y, yb, z = _rowwise(fn, [(x[None], c, 0), (mix[None], c, 0)], pars=[(g.reshape(1, 1, c), c, 0), (b.reshape(1, 1, c), c, 0)],
                        outs=[(c, c, 0, F32), (c, c, 0, BF16), (c, c, 0, F32)], tm=512, name=name)
    return y[0], yb[0], z[0]


def _ln_bwd(z, g, dys, coefs, *, name):
    n = len(dys)

    def fn(i, nt, zv, *rest):
        gv = rest[n]
        dy = coefs[0] * rest[0]
        for cf, t in zip(coefs[1:], rest[1:n]):
            dy = dy + cf * t
        mu = jnp.mean(zv, axis=-1, keepdims=True)
        zc = zv - mu
        r = lax.rsqrt(jnp.mean(zc * zc, axis=-1, keepdims=True) + LN_EPS)
        xh = zc * r
        dxh = dy * gv
        dz = r * (dxh - jnp.mean(dxh, axis=-1, keepdims=True) - xh * jnp.mean(dxh * xh, axis=-1, keepdims=True))
        return dz, dz, jnp.sum(dy * xh, axis=0, keepdims=True), jnp.sum(dy, axis=0, keepdims=True)

    c = z.shape[-1]
    dz, dzb, dg, db = _rowwise(fn, [(z[None], c, 0)] + [(d[None], c, 0) for d in dys], pars=[(g.reshape(1, 1, c), c, 0)],
                               outs=[(c, c, 0, F32), (c, c, 0, BF16)], accs=[(1, c, c, 0), (1, c, c, 0)], tm=512, name=name)
    return dz[0], dzb[0], dg.reshape(c), db.reshape(c)


def _rms_fwd(src, c, off, g, *, name):
    def fn(i, nt, xv, gv):
        return xv * lax.rsqrt(jnp.mean(xv * xv, axis=-1, keepdims=True) + RMS_EPS) * gv

    return _rowwise(fn, [(src[None], c, off)], pars=[(g.reshape(1, 1, c), c, 0)], outs=[(c, c, 0, BF16)], tm=1024, name=name)[0][0]


def _rms_bwd(src, c, off, g, dy, *, name):
    def fn(i, nt, xv, dyv, gv):
        r = lax.rsqrt(jnp.mean(xv * xv, axis=-1, keepdims=True) + RMS_EPS)
        gd = gv * dyv
        dx = gd * r - xv * (r * r * r) * jnp.mean(gd * xv, axis=-1, keepdims=True)
        return dx, jnp.sum(dyv * xv * r, axis=0, keepdims=True)

    dx, dg = _rowwise(fn, [(src[None], c, off), (dy[None], c, 0)], pars=[(g.reshape(1, 1, c), c, 0)],
                      outs=[(c, c, 0, BF16)], accs=[(1, c, c, 0)], tm=1024, name=name)
    return dx[0], dg.reshape(c)


def _rope(x1, x2, cos, sin, *, name):
    def fn(i, nt, a, b, c, s):
        return a * c - b * s, a * s + b * c

    w = x1.shape[-1]
    y1, y2 = _rowwise(fn, [(x1[None], w, 0), (x2[None], w, 0), (cos[None], w, 0), (sin[None], w, 0)],
                      outs=[(w, w, 0, F32), (w, w, 0, F32)], tm=1024, name=name)
    return y1[0], y2[0]


def _merge_fwd(proj, b_gate, ys, *, name):
    def fn(i, nt, g0, g1, g2, ya, yb, yc, bg):
        return (_sigmoid(g0 + bg[:, 0:1024]) * ya + _sigmoid(g1 + bg[:, 1024:2048]) * yb
                + _sigmoid(g2 + bg[:, 2048:3072]) * yc)

    rows = [(proj[None], 1024, j) for j in range(3)] + [(y[None], 1024, 0) for y in ys]
    return _rowwise(fn, rows, pars=[(b_gate.reshape(1, 1, 3072), 3072, 0)], outs=[(1024, 1024, 0, BF16)], tm=512, name=name)[0][0]


def _merge_bwd(proj, b_gate, ys, dm, *, name):
    def fn(i, nt, g0, g1, g2, ya, yb, yc, dmv, bg):
        outs, dgs = [], []
        for j, (gp, y) in enumerate(((g0, ya), (g1, yb), (g2, yc))):
            s = _sigmoid(gp + bg[:, j * 1024:(j + 1) * 1024])
            outs.append(s * dmv)
            dgs.append(dmv * y * s * (1.0 - s))
        return outs + dgs + [jnp.sum(d, axis=0, keepdims=True) for d in dgs]

    rows = [(proj[None], 1024, j) for j in range(3)] + [(y[None], 1024, 0) for y in ys] + [(dm[None], 1024, 0)]
    res = _rowwise(fn, rows, pars=[(b_gate.reshape(1, 1, 3072), 3072, 0)], outs=[(1024, 1024, 0, BF16)] * 6,
                   accs=[(1, 1024, 1024, 0)] * 3, tm=256, name=name)
    dys = [r[0] for r in res[0:3]]
    dgp = [r[0] for r in res[3:6]]
    dbg = jnp.concatenate([r.reshape(1024) for r in res[6:9]])
    return dys, dgp, dbg


def _shift_down(u, halo, i, k):
    ext = jnp.concatenate([jnp.where(i > 0, halo, 0.0), u], axis=0)
    return pltpu.roll(ext, k, axis=0)[8:]


def _shift_up(u, halo, i, nt, k):
    ext = jnp.concatenate([u, jnp.where(i < nt - 1, halo, 0.0)], axis=0)
    n = ext.shape[0]
    return pltpu.roll(ext, n - k, axis=0)[:n - 8]


GLU_C = D_FF // 2


def _conv(u, halo, i, w, b):
    return w[0:1] * _shift_down(u, halo, i, 2) + w[1:2] * _shift_down(u, halo, i, 1) + w[2:3] * u + b


def _glu_fwd(ug, uv, conv_w, conv_b, *, name):
    def fn(i, nt, g, v, hg, hv, wg, wv, bg, bv):
        cg, cv = _conv(g, hg, i, wg, bg), _conv(v, hv, i, wv, bv)
        return cg * _sigmoid(cg) * cv

    w3, b3 = conv_w[None], conv_b.reshape(1, 1, -1)
    c = GLU_C
    return _rowwise(fn, [(ug[None], c, 0), (uv[None], c, 0)], halos=[(ug[None], c, 0, "prev"), (uv[None], c, 0, "prev")],
                    pars=[(w3, c, 0), (w3, c, 2), (b3, c, 0), (b3, c, 2)], outs=[(D_FF, c, 0, BF16)], tm=256, ncol=2, name=name)[0][0]


def _glu_bwd_a(ug, uv, conv_w, conv_b, dh, *, name):
    def fn(i, nt, g, v, dhv, hg, hv, wg, wv, bg, bv):
        g1, g2 = _shift_down(g, hg, i, 1), _shift_down(g, hg, i, 2)
        v1, v2 = _shift_down(v, hv, i, 1), _shift_down(v, hv, i, 2)
        cg = wg[0:1] * g2 + wg[1:2] * g1 + wg[2:3] * g + bg
        cv = wv[0:1] * v2 + wv[1:2] * v1 + wv[2:3] * v + bv
        s = _sigmoid(cg)
        dcv = dhv * cg * s
        dcg = dhv * cv * (s * (1.0 + cg * (1.0 - s)))
        red = lambda a: jnp.sum(a, axis=0, keepdims=True)
        return (dcg, dcv, red(dcg), red(dcv), red(dcg * g2), red(dcg * g1), red(dcg * g),
                red(dcv * v2), red(dcv * v1), red(dcv * v))

    w3, b3 = conv_w[None], conv_b.reshape(1, 1, -1)
    c = GLU_C
    res = _rowwise(fn, [(ug[None], c, 0), (uv[None], c, 0), (dh[None], c, 0)],
                   halos=[(ug[None], c, 0, "prev"), (uv[None], c, 0, "prev")],
                   pars=[(w3, c, 0), (w3, c, 2), (b3, c, 0), (b3, c, 2)],
                   outs=[(D_FF, c, 0, F32), (D_FF, c, 0, F32)], accs=[(1, D_FF, c, 0)] * 8, tm=256, ncol=2, name=name)
    dcg, dcv = res[0][0], res[1][0]
    dconv_b = jnp.concatenate([res[2].reshape(D_FF), res[3].reshape(D_FF)])
    dconv_w = jnp.concatenate([jnp.concatenate([res[4 + j].reshape(1, D_FF) for j in range(3)], axis=0),
                               jnp.concatenate([res[7 + j].reshape(1, D_FF) for j in range(3)], axis=0)], axis=1)
    return dcg, dcv, dconv_w, dconv_b


def _glu_bwd_b(dc, conv_w, half, *, name):
    def fn(i, nt, d, hd, w):
        return w[2:3] * d + w[1:2] * _shift_up(d, hd, i, nt, 1) + w[0:1] * _shift_up(d, hd, i, nt, 2)

    c = GLU_C
    return _rowwise(fn, [(dc[None], c, 0)], halos=[(dc[None], c, 0, "next")], pars=[(conv_w[None], c, 2 * half)],
                    outs=[(D_FF, c, 0, BF16)], tm=256, ncol=2, name=name)[0][0]


def _loss_and_grad(y, tgt, *, name):
    def fn(i, nt, yv, tv):
        err = yv - tv
        part = jnp.sum(jnp.sum(err * err, axis=0, keepdims=True), axis=1, keepdims=True) * (0.5 / D_MODEL)
        return err * (1.0 / D_MODEL), jnp.zeros((1, LANE), F32) + part

    dy, part = _rowwise(fn, [(y[None], D_MODEL, 0), (tgt[None], D_MODEL, 0)], outs=[(D_MODEL, D_MODEL, 0, F32)],
                        accs=[(1, LANE, LANE, 0)], tm=512, name=name)
    return dy[0], part.reshape(LANE)


def _axpy(a, x, y, *, name):
    def fn(i, nt, xv, yv):
        return a * xv + yv

    c = x.shape[-1]
    return _rowwise(fn, [(x[None], c, 0), (y[None], c, 0)], outs=[(c, c, 0, F32)], tm=512, name=name)[0][0]


def _sum_rows(terms, *, tm, name, dtype=F32):
    def fn(i, nt, *vs):
        acc = vs[0]
        for v in vs[1:]:
            acc = acc + v
        return acc

    c = terms[0].shape[-1]
    return _rowwise(fn, [(t, c, 0) for t in terms], outs=[(c, c, 0, dtype)], tm=tm, name=name)[0]


def _combine_fwd(ots, lses, *, name):
    def fn(o0, o1, o2, l0, l1, l2):
        m = jnp.maximum(jnp.maximum(l0, l1), l2)
        e0, e1, e2 = jnp.exp(l0 - m), jnp.exp(l1 - m), jnp.exp(l2 - m)
        inv = 1.0 / (e0 + e1 + e2)
        return (e0 * inv) * o0 + (e1 * inv) * o1 + (e2 * inv) * o2

    return _lanewise(fn, list(ots) + list(lses), [(HEAD_DIM, F32)], tl=2048, name=name)[0]


def _combine_bwd(ots, lses, dot_a, *, name):
    def fn(o0, o1, o2, l0, l1, l2, da):
        m = jnp.maximum(jnp.maximum(l0, l1), l2)
        es = [jnp.exp(l - m) for l in (l0, l1, l2)]
        inv = 1.0 / (es[0] + es[1] + es[2])
        ws = [e * inv for e in es]
        dws = [jnp.sum(da * o, axis=0, keepdims=True) for o in (o0, o1, o2)]
        mean = ws[0] * dws[0] + ws[1] * dws[1] + ws[2] * dws[2]
        dos = [w * da for w in ws]
        dprs = [w * dw - w * (dw - mean) for w, dw in zip(ws, dws)]
        return dos + dprs

    res = _lanewise(fn, list(ots) + list(lses) + [dot_a], [(HEAD_DIM, BF16)] * 3 + [(1, F32)] * 3, tl=2048, name=name)
    return res[0:3], res[3:6]


def _rowdot(at, bt, *, name):
    def fn(a, b):
        return jnp.sum(a * b, axis=0, keepdims=True)

    return _lanewise(fn, [at, bt], [(1, F32)], tl=2048, name=name)[0]


def _adamw(w, g, m, v, *, name):
    c1 = 1.0 - ADAM_B1 ** ADAM_STEP
    c2 = 1.0 - ADAM_B2 ** ADAM_STEP

    def fn(i, nt, wv, gv, mv, vv):
        mn = ADAM_B1 * mv + (1.0 - ADAM_B1) * gv
        vn = ADAM_B2 * vv + (1.0 - ADAM_B2) * (gv * gv)
        delta = -ADAM_LR * ((mn / c1) / (jnp.sqrt(vn / c2) + ADAM_EPS) + ADAM_WD * wv)
        return delta, mn, vn

    r, c = w.shape
    rp = _ceil_to(r, 8)
    pad = lambda a: jnp.pad(a, ((0, rp - r), (0, 0))) if rp != r else a
    tm = rp
    for cand in (128, 64, 32, 16, 8):
        if rp % cand == 0:
            tm = cand
            break
    res = _rowwise(fn, [(pad(a)[None], c, 0) for a in (w, g, m, v)], outs=[(c, c, 0, F32)] * 3, tm=tm, name=name)
    return [x[0][:r] for x in res]


ANY = pl.BlockSpec(memory_space=pl.ANY)


def _place():
    x, y, c = lax.axis_index("x"), lax.axis_index("y"), lax.axis_index("c")
    chips = [(1 - x, y), (x, 1 - y), (1 - x, 1 - y)]
    return x, y, c, chips


def _allgather_weights(arrs):
    n = len(arrs)

    def body(*refs):
        ins, outs, send_sems, recv_sems = refs[:n], refs[n:2 * n], refs[2 * n], refs[2 * n + 1]
        x, y, c, chips = _place()
        j = 2 * x + y

        def cp(i, k, src, chip_idx, half, to):
            return pltpu.make_async_remote_copy(src_ref=src, dst_ref=outs[i].at[chip_idx, half], send_sem=send_sems.at[k],
                                                recv_sem=recv_sems.at[k], device_id=to, device_id_type=MESH)

        first, passed = [], []
        for i in range(n):
            for r, (cx, cy) in enumerate(chips):
                first.append(cp(i, 3 * i + r, ins[i].at[c], j, c, (cx, cy, c)))
                passed.append(cp(i, 3 * (n + i) + r, outs[i].at[2 * cx + cy, c], 2 * cx + cy, c, (x, y, 1 - c)))
        for d in first:
            d.start()
        for i in range(n):
            for r, (cx, cy) in enumerate(chips):
                cp(i, 3 * i + r, ins[i].at[c], 2 * cx + cy, c, (x, y, c)).wait_recv()
                passed[3 * i + r].start()
        for i in range(n):
            for r, (cx, cy) in enumerate(chips):
                cp(i, 3 * (n + i) + r, ins[i].at[c], 2 * cx + cy, 1 - c, (x, y, c)).wait_recv()
        for d in first + passed:
            d.wait_send()

    return pl.pallas_call(
        body, name="allgather_weights", in_specs=[ANY] * n, out_specs=[ANY] * n,
        out_shape=[jax.ShapeDtypeStruct((N_CHIP,) + a.shape, a.dtype) for a in arrs],
        scratch_shapes=[pltpu.SemaphoreType.DMA((6 * n,)), pltpu.SemaphoreType.DMA((6 * n,))],
    )(*arrs)


def _sibling_swap(gs):
    n = len(gs)

    def body(*refs):
        ins, outs, send_sems, recv_sems = refs[:n], refs[n:2 * n], refs[2 * n], refs[2 * n + 1]
        x, y, c, _ = _place()
        cps = [pltpu.make_async_remote_copy(src_ref=ins[i].at[1 - c], dst_ref=outs[i], send_sem=send_sems.at[i],
                                            recv_sem=recv_sems.at[i], device_id=(x, y, 1 - c), device_id_type=MESH)
               for i in range(n)]
        for d in cps:
            d.start()
        for d in cps:
            d.wait_recv()
        for d in cps:
            d.wait_send()

    return pl.pallas_call(
        body, name="grad_sibling_swap", in_specs=[ANY] * n, out_specs=[ANY] * n,
        out_shape=[jax.ShapeDtypeStruct(g.shape[1:], g.dtype) for g in gs],
        scratch_shapes=[pltpu.SemaphoreType.DMA((n,)), pltpu.SemaphoreType.DMA((n,))],
    )(*gs)


def _chip_scatter(ps):
    n = len(ps)

    def body(*refs):
        ins, outs, send_sems, recv_sems = refs[:n], refs[n:2 * n], refs[2 * n], refs[2 * n + 1]
        x, y, c, chips = _place()
        sends = []
        for i in range(n):
            for r, (cx, cy) in enumerate(chips):
                sends.append(pltpu.make_async_remote_copy(src_ref=ins[i].at[2 * cx + cy], dst_ref=outs[i].at[r], send_sem=send_sems.at[3 * i + r],
                                                          recv_sem=recv_sems.at[3 * i + r], device_id=(cx, cy, c), device_id_type=MESH))
        for d in sends:
            d.start()
        for d in sends:
            d.wait_recv()
        for d in sends:
            d.wait_send()

    return pl.pallas_call(
        body, name="grad_chip_scatter", in_specs=[ANY] * n, out_specs=[ANY] * n,
        out_shape=[jax.ShapeDtypeStruct((3,) + p.shape[1:], p.dtype) for p in ps],
        scratch_shapes=[pltpu.SemaphoreType.DMA((3 * n,)), pltpu.SemaphoreType.DMA((3 * n,))],
    )(*ps)


def _sibling_share(rs):
    n = len(rs)

    def body(*refs):
        ins, outs, send_sems, recv_sems = refs[:n], refs[n:2 * n], refs[2 * n], refs[2 * n + 1]
        x, y, c, _ = _place()
        cps = [pltpu.make_async_remote_copy(src_ref=ins[i], dst_ref=outs[i], send_sem=send_sems.at[i], recv_sem=recv_sems.at[i],
                                            device_id=(x, y, 1 - c), device_id_type=MESH) for i in range(n)]
        for d in cps:
            d.start()
        for d in cps:
            d.wait_recv()
        for d in cps:
            d.wait_send()

    return pl.pallas_call(
        body, name="grad_sibling_share", in_specs=[ANY] * n, out_specs=[ANY] * n,
        out_shape=[jax.ShapeDtypeStruct(r.shape, r.dtype) for r in rs],
        scratch_shapes=[pltpu.SemaphoreType.DMA((n,)), pltpu.SemaphoreType.DMA((n,))],
    )(*rs)


def _allreduce_small(s):
    rows, w = s.shape
    n_dev = 8

    def body(s_ref, out_ref, slots, send_sems, recv_sems):
        x, y, c, _ = _place()
        me = 4 * x + 2 * y + c
        slots[me] = s_ref[...]
        peers = []
        for r in range(1, n_dev):
            px = 1 - x if r & 4 else x
            py = 1 - y if r & 2 else y
            pc = 1 - c if r & 1 else c
            peers.append((px, py, pc))
        sends = [pltpu.make_async_remote_copy(src_ref=s_ref, dst_ref=slots.at[me], send_sem=send_sems.at[r], recv_sem=recv_sems.at[r],
                                              device_id=peer, device_id_type=MESH) for r, peer in enumerate(peers)]
        for d in sends:
            d.start()
        for r, (px, py, pc) in enumerate(peers):
            pltpu.make_async_remote_copy(src_ref=s_ref, dst_ref=slots.at[4 * px + 2 * py + pc], send_sem=send_sems.at[r],
                                         recv_sem=recv_sems.at[r], device_id=(x, y, c), device_id_type=MESH).wait_recv()
        for d in sends:
            d.wait_send()
        acc = slots[0]
        for k in range(1, n_dev):
            acc = acc + slots[k]
        out_ref[...] = acc

    vm = pl.BlockSpec(memory_space=pltpu.VMEM)
    return pl.pallas_call(
        body, name="allreduce_small", in_specs=[vm], out_specs=vm, out_shape=jax.ShapeDtypeStruct((rows, w), F32),
        scratch_shapes=[pltpu.VMEM((n_dev, rows, w), F32), pltpu.SemaphoreType.DMA((n_dev - 1,)), pltpu.SemaphoreType.DMA((n_dev - 1,))],
    )(s)


W_IN_SHARD = D_IN // N_CHIP
W_IN_ROWS_G = 2304
GATE_SPLIT = ORIG_GATE - 2 * W_IN_SHARD
REDUCED = tuple(m for m in MATS if m[0] != "conv_w")
CONV_W_SIZE = 3 * 2 * D_FF


def _weight_send(name, a):
    if name == "w_in":
        return jnp.swapaxes(a, 1, 2).astype(BF16)
    return a if name == "conv_w" else a.astype(BF16)


def _full_weights(gathered, l):
    g = {k: v[:, l] for k, v in gathered.items()}
    s = g["w_in"].astype(F32)
    win_pt = jnp.concatenate([s[2, GATE_SPLIT:], s[3], s[0], s[1], s[2, :GATE_SPLIT],
                              jnp.zeros((D_IN_P - D_IN, D_MODEL), F32)], axis=0).astype(BF16)
    full = {name: jnp.moveaxis(g[name], 0, ax).reshape(shape) for name, shape, ax in MATS if name != "w_in"}
    return {"win_pt": win_pt, "w_uq": full["w_uq"], "w_ukv": full["w_ukv"], "w_branch": full["w_branch"], "w_out": full["w_out"],
            "wup_g": full["w_ffn_up"][:, :D_FF], "wup_v": full["w_ffn_up"][:, D_FF:], "conv_w": full["conv_w"],
            "w_ffn_down": full["w_ffn_down"]}


def _grad_send(name, g, shape, ax):
    if name == "w_in":
        gt = g.T
        rest = W_IN_SHARD - GATE_SPLIT
        slabs = [gt[COL_A:COL_A + W_IN_SHARD], gt[COL_A + W_IN_SHARD:COL_A + 2 * W_IN_SHARD],
                 jnp.concatenate([gt[COL_A + 2 * W_IN_SHARD:D_IN], gt[:rest]], axis=0), gt[rest:COL_A]]
        return jnp.pad(jnp.stack(slabs), ((0, 0), (0, W_IN_ROWS_G - W_IN_SHARD), (0, 0)))
    split = shape[:ax] + (N_CHIP, shape[ax] // N_CHIP) + shape[ax + 1:]
    return jnp.moveaxis(g.reshape(split), ax, 0)


def _grad_recv(name, r):
    return r[:W_IN_SHARD].T if name == "w_in" else r


def _pack_small(rel, small, conv_w, extra):
    parts = [rel.reshape(-1)]
    for l in range(DEPTH):
        for name in SMALL:
            parts.append(small[name][l].reshape(-1))
    parts += [conv_w.reshape(-1), extra]
    flat = jnp.concatenate(parts)
    rows = _ceil_to(-(-flat.shape[0] // LANE), 8)
    return jnp.pad(flat, (0, rows * LANE - flat.shape[0])).reshape(rows, LANE)


def _unpack_small(buf):
    flat = buf.reshape(-1)
    rel = flat[:REL_BUCKETS * 32].reshape(REL_BUCKETS, 32)
    off = REL_BUCKETS * 32
    small = {name: [] for name in SMALL}
    for l in range(DEPTH):
        for name in SMALL:
            n = SMALL_SIZES[name]
            small[name].append(flat[off:off + n])
            off += n
    conv_w = flat[off:off + DEPTH * CONV_W_SIZE].reshape(DEPTH, 3, 2 * D_FF)
    off += DEPTH * CONV_W_SIZE
    return rel, {k: jnp.stack(v) for k, v in small.items()}, conv_w, flat[off:off + LANE]


def _rows2d(a, lead):
    return a.reshape(a.shape[:lead] + (-1, a.shape[-1]))


def _row_tile(rows):
    for cand in (512, 256, 128, 64, 32, 16, 8):
        if rows % cand == 0:
            return cand
    raise ValueError(rows)


def _pair_add(g, got, core, *, name):
    g2, got2 = _rows2d(g, 1), _rows2d(got, 0)
    rows, c = got2.shape
    tm = _row_tile(rows)
    flag = jnp.zeros((1, 1, LANE), F32) + core.astype(F32)

    def fn(i, nt, a0, a1, b, f):
        return jnp.where(f[:, 0:1] == 0.0, a0, a1) + b

    stacked = g2.reshape(1, 2 * rows, c)
    out = _rowwise(fn, [(stacked, c, 0, 0), (stacked, c, 0, rows // tm), (got2[None], c, 0)], pars=[(flag, LANE, 0)],
                   outs=[(c, c, 0, BF16)], tm=tm, t=rows, name=name)[0][0]
    return out.reshape(got.shape)


def _chip_add(own, got, *, name):
    own2, got2 = _rows2d(own, 0), _rows2d(got, 1)
    rows, c = own2.shape
    tm = _row_tile(rows)

    def fn(i, nt, a, b0, b1, b2):
        return ((a.astype(F32) + b0.astype(F32)) + b1.astype(F32)) + b2.astype(F32)

    stacked = got2.reshape(1, 3 * rows, c)
    out = _rowwise(fn, [(own2[None], c, 0)] + [(stacked, c, 0, k * (rows // tm)) for k in range(3)],
                   outs=[(c, c, 0, F32)], tm=tm, t=rows, name=name)[0][0]
    return out.reshape(own.shape)


def _perm(a, d):
    if d == 1:
        return a
    t = a.shape[0]
    return jnp.swapaxes(a.reshape((t // d, d) + a.shape[1:]), 0, 1).reshape(a.shape)


def _unperm(a, d):
    if d == 1:
        return a
    t = a.shape[0]
    return jnp.swapaxes(a.reshape((d, t // d) + a.shape[1:]), 0, 1).reshape(a.shape)


def _pad_lanes(a, w=HP):
    return jnp.pad(a, [(0, 0)] * (a.ndim - 1) + [(0, w - a.shape[-1])])


def _heads_blocks(a, blk):
    t, h, _ = a.shape
    return jnp.transpose(_pad_lanes(a), (1, 0, 2)).astype(BF16).reshape(h, t // blk, blk, HP)


def _heads_blocks_t(a, blk):
    return jnp.swapaxes(_heads_blocks(a, blk), -1, -2)


def _from_blocks_t(a):
    h, n, d, blk = a.shape
    return jnp.transpose(a, (1, 3, 0, 2)).reshape(n * blk, h, d)


def _from_blocks(a):
    h, n, blk, d = a.shape
    return jnp.transpose(a, (1, 2, 0, 3)).reshape(n * blk, h, d)


def _to_hdt(a):
    return jnp.transpose(a, (1, 2, 0))


def _rope_tables(t):
    pos = jnp.arange(t, dtype=F32)
    inv_freq = ROPE_BASE ** (-jnp.arange(0, C_ROPE, 2, dtype=F32) / C_ROPE)
    ang = pos[:, None] * inv_freq[None, :]
    cos, sin = jnp.cos(ang), jnp.sin(ang)
    half = C_ROPE // 2
    wide = lambda a: jnp.concatenate([jnp.tile(a, (1, N_HEADS)), a, jnp.zeros((t, 2 * LANE - (N_HEADS + 1) * half), F32)], axis=1)
    return wide(cos), wide(sin)


def _rope_pack(q_part, k_part):
    t = k_part.shape[0]
    return jnp.concatenate([q_part.reshape(t, -1), k_part, jnp.zeros((t, 2 * LANE - 9 * (C_ROPE // 2)), F32)], axis=1)


def _band_params(t, sinks):
    nbs = jnp.array(sum([[t // (d * BLK)] * N_HEADS for d in A_DILS + (1,)], []), jnp.int32)
    lims = jnp.array([BLK] * (3 * N_HEADS) + [BLK - 1] * N_HEADS, jnp.int32)
    sink_all = jnp.concatenate([jnp.full((3 * N_HEADS,), NEG, F32), sinks.astype(F32)])
    return nbs, lims, sink_all


def _layer_fwd(l, x, xb, w, p, bias_t, rope_cs):
    t = x.shape[0]
    nblk = t // BLK
    n = f"l{l}_"
    s = {"xb": xb}
    proj = _mm(xb, w["win_pt"], tb=True, tn=1280, name=n + "proj")
    s["proj"] = proj

    a = proj[:, COL_A:COL_BQ].reshape(t, 3, 3, N_HEADS, HEAD_DIM)
    qs, ks, vs = [], [], []
    for gi, d in enumerate(A_DILS):
        ag = _perm(a[:, gi], d)
        qs.append(ag[:, 0]); ks.append(ag[:, 1]); vs.append(ag[:, 2])
    bq = proj[:, COL_BQ:COL_BKV].reshape(t, N_HEADS, HEAD_DIM)
    bkv = proj[:, COL_BKV:COL_CQ].reshape(t, 2, 2, HEAD_DIM)
    qs.append(bq); ks.append(jnp.repeat(bkv[:, 0], 4, axis=1)); vs.append(jnp.repeat(bkv[:, 1], 4, axis=1))
    q_all = _heads_blocks(jnp.concatenate(qs, axis=1), BLK)
    k_cat, v_cat = jnp.concatenate(ks, axis=1), jnp.concatenate(vs, axis=1)
    k_all, v_all = _heads_blocks(k_cat, BLK), _heads_blocks(v_cat, BLK)
    kt_all, vt_all = jnp.swapaxes(k_all, -1, -2), jnp.swapaxes(v_all, -1, -2)
    nbs, lims, sink_all = _band_params(t, p["sinks"])
    ot_all, lse_all = _band_fwd(q_all, k_all, vt_all, bias_t, nbs, lims, sink_all, name=n + "band_fwd")
    s.update(q_all=q_all, k_all=k_all, kt_all=kt_all, v_all=v_all, lse_all=lse_all, band=(nbs, lims, sink_all))
    o_perm = _from_blocks_t(ot_all)[:, :, :HEAD_DIM]
    lse_perm = lse_all.reshape(4 * N_HEADS, t).T
    ots = [_to_hdt(_unperm(o_perm[:, gi * 8:(gi + 1) * 8], d)) for gi, d in enumerate(A_DILS)]
    lses = [_unperm(lse_perm[:, gi * 8:(gi + 1) * 8], d).T[:, None, :] for gi, d in enumerate(A_DILS)]
    ot_a = _combine_fwd(ots, lses, name=n + "combine_fwd")
    s.update(ots=ots, lses=lses)
    o_a = jnp.transpose(ot_a, (2, 0, 1)).reshape(t, 512).astype(BF16)
    o_b = o_perm[:, 24:32].reshape(t, 512).astype(BF16)
    s["ot_b"] = _to_hdt(o_perm[:, 24:32])

    rq = _rms_fwd(proj, C_Q_RANK, COL_CQ // C_Q_RANK, p["q_norm_g"], name=n + "rms_q")
    rkv = _rms_fwd(proj, C_KV_RANK, COL_CDKV // C_KV_RANK, p["kv_norm_g"], name=n + "rms_kv")
    q_c = _mm(rq, w["w_uq"], name=n + "uq").reshape(t, N_HEADS, C_NOPE + C_ROPE)
    kv_c = _mm(rkv, w["w_ukv"], name=n + "ukv").reshape(t, N_HEADS, 2 * C_NOPE)
    k_rope = proj[:, COL_CDKV + C_KV_RANK:COL_CDKV + C_KV_RANK + C_ROPE]
    hr = C_ROPE // 2
    x1 = _rope_pack(q_c[:, :, C_NOPE:C_NOPE + hr], k_rope[:, :hr])
    x2 = _rope_pack(q_c[:, :, C_NOPE + hr:], k_rope[:, hr:])
    y1, y2 = _rope(x1, x2, rope_cs[0], rope_cs[1], name=n + "rope")
    qy1, qy2 = y1[:, :LANE].reshape(t, N_HEADS, hr), y2[:, :LANE].reshape(t, N_HEADS, hr)
    ky = jnp.concatenate([y1[:, LANE:LANE + hr], y2[:, LANE:LANE + hr]], axis=1)
    q_full = jnp.concatenate([q_c[:, :, :C_NOPE], qy1, qy2], axis=2)
    k_full = jnp.concatenate([kv_c[:, :, :C_NOPE], jnp.broadcast_to(ky[:, None, :], (t, N_HEADS, C_ROPE))], axis=2)
    qm, km, vm = _heads_blocks(q_full, TQ), _heads_blocks(k_full, TQ), _heads_blocks(kv_c[:, :, C_NOPE:], TQ)
    ot_c, lse_c = _mla_fwd(qm, km, jnp.swapaxes(vm, -1, -2), name=n + "mla_fwd")
    o_c3 = _from_blocks_t(ot_c)[:, :, :HEAD_DIM]
    o_c = o_c3.reshape(t, 512).astype(BF16)
    s.update(rq=rq, rkv=rkv, qm=qm, km=km, vm=vm, lse_c=lse_c, ot_c=_to_hdt(o_c3))

    obs = [o_a, o_b, o_c]
    ys = [_mm(o, w["w_branch"][i], name=n + f"branch{i}") for i, o in enumerate(obs)]
    merged = _merge_fwd(proj, p["b_gate"], ys, name=n + "merge")
    mix = _mm(merged, w["w_out"], name=n + "out")
    x1f, x1b, z1 = _ln_fwd(x, mix, p["ln1_g"], p["ln1_b"], name=n + "ln1")
    s.update(obs=obs, ys=ys, merged=merged, z1=z1, x1b=x1b)

    ug = _mm(x1b, w["wup_g"], tn=1408, name=n + "up_g")
    uv = _mm(x1b, w["wup_v"], tn=1408, name=n + "up_v")
    h = _glu_fwd(ug, uv, w["conv_w"], p["conv_b"], name=n + "glu")
    ff = _mm(h, w["w_ffn_down"], tk=1408, name=n + "down")
    x2f, x2b, z2 = _ln_fwd(x1f, ff, p["ln2_g"], p["ln2_b"], name=n + "ln2")
    s.update(ug=ug, uv=uv, h=h, z2=z2)
    return x2f, x2b, s


def _layer_bwd(l, s, dys, coefs, w, p, bias_t, rope_cs):
    n = f"l{l}b_"
    t = s["z2"].shape[0]
    gw, gs = {}, {}
    tr = lambda a: a.T

    dz2, dz2b, gs["ln2_g"], gs["ln2_b"] = _ln_bwd(s["z2"], p["ln2_g"], dys, coefs, name=n + "ln2")
    dh = _mm(dz2b, w["w_ffn_down"], tb=True, tn=1408, name=n + "d_h")
    gw["w_ffn_down"] = _mm(tr(s["h"]), dz2b, tm=704, name=n + "g_down")
    dcg, dcv, gw["conv_w"], gs["conv_b"] = _glu_bwd_a(s["ug"], s["uv"], w["conv_w"], p["conv_b"], dh, name=n + "glu_a")
    dug = _glu_bwd_b(dcg, w["conv_w"], 0, name=n + "glu_bg")
    duv = _glu_bwd_b(dcv, w["conv_w"], 1, name=n + "glu_bv")
    dx1_g = _mm(dug, w["wup_g"], tb=True, tk=1408, name=n + "d_x1g")
    dx1_v = _mm(duv, w["wup_v"], tb=True, tk=1408, name=n + "d_x1v")
    x1t = tr(s["x1b"])
    gw["w_ffn_up"] = jnp.concatenate([_mm(x1t, dug, tn=1408, name=n + "g_upg"), _mm(x1t, duv, tn=1408, name=n + "g_upv")], axis=1)

    dz1, dz1b, gs["ln1_g"], gs["ln1_b"] = _ln_bwd(s["z1"], p["ln1_g"], [dz2, dx1_g, dx1_v], [ALPHA, 1.0, 1.0], name=n + "ln1")
    dmerged = _mm(dz1b, w["w_out"], tb=True, name=n + "d_merged")
    gw["w_out"] = _mm(tr(s["merged"]), dz1b, name=n + "g_out")
    dys_b, dgp, gs["b_gate"] = _merge_bwd(s["proj"], p["b_gate"], s["ys"], dmerged, name=n + "merge")
    dos = [_mm(dy, w["w_branch"][i], tb=True, name=n + f"d_o{i}") for i, dy in enumerate(dys_b)]
    gw["w_branch"] = jnp.stack([_mm(tr(o), dy, name=n + f"g_branch{i}") for i, (o, dy) in enumerate(zip(s["obs"], dys_b))])

    do_a3 = dos[0].reshape(t, N_HEADS, HEAD_DIM)
    dots, dprs = _combine_bwd(s["ots"], s["lses"], _to_hdt(do_a3), name=n + "combine")
    do_b3 = dos[1].reshape(t, N_HEADS, HEAD_DIM)
    dpr_b = _rowdot(_to_hdt(do_b3), s["ot_b"], name=n + "delta_b")
    do_list = [_perm(jnp.transpose(dt_, (2, 0, 1)), d) for dt_, d in zip(dots, A_DILS)] + [do_b3]
    dpr_list = [_perm(dp_[:, 0, :].T, d) for dp_, d in zip(dprs, A_DILS)] + [dpr_b[:, 0, :].T]
    do_all = _heads_blocks(jnp.concatenate(do_list, axis=1), BLK)
    nblk = t // BLK
    dpr_all = jnp.concatenate(dpr_list, axis=1).T.reshape(4 * N_HEADS, nblk, 1, BLK)
    nbs, lims, sink_all = s["band"]
    dqt, dk, dv, ds_sum, dsink = _band_bwd(s["q_all"], s["k_all"], s["kt_all"], s["v_all"], do_all, s["lse_all"], dpr_all,
                                           bias_t, nbs, lims, sink_all, name=n + "band")
    gs["sinks"] = dsink[3 * N_HEADS:, 0, 0]
    dq_p = _from_blocks_t(dqt)[:, :, :HEAD_DIM]
    dk_p, dv_p = _from_blocks(dk)[:, :, :HEAD_DIM], _from_blocks(dv)[:, :, :HEAD_DIM]
    groups = []
    for gi, d in enumerate(A_DILS):
        sl = slice(gi * 8, (gi + 1) * 8)
        groups.append(jnp.stack([_unperm(dq_p[:, sl], d), _unperm(dk_p[:, sl], d), _unperm(dv_p[:, sl], d)], axis=1))
    da_qkv = jnp.stack(groups, axis=1).reshape(t, 9 * 512).astype(BF16)
    dbq = dq_p[:, 24:32].reshape(t, 512).astype(BF16)
    kv_terms = []
    for part in (dk_p, dv_p):
        hm = jnp.transpose(part[:, 24:32], (1, 0, 2)).reshape(2, 4, t, HEAD_DIM)
        kv_terms.append([hm[:, i] for i in range(4)])
    dkb = _sum_rows(kv_terms[0], tm=1024, name=n + "gqa_dk", dtype=BF16)
    dvb = _sum_rows(kv_terms[1], tm=1024, name=n + "gqa_dv", dtype=BF16)
    dbkv = jnp.transpose(jnp.stack([dkb, dvb]), (2, 0, 1, 3)).reshape(t, 256)

    do_c3 = dos[2].reshape(t, N_HEADS, HEAD_DIM)
    delta_c = _rowdot(_to_hdt(do_c3), s["ot_c"], name=n + "delta_c").reshape(N_HEADS, t // TQ, 1, TQ)
    dqt_c, dk_c, dv_c = _mla_bwd(s["qm"], s["km"], jnp.swapaxes(s["km"], -1, -2), s["vm"], _heads_blocks(do_c3, TQ),
                                 s["lse_c"], delta_c, name=n + "mla")
    dq_full, dk_full, dv_full = _from_blocks_t(dqt_c), _from_blocks(dk_c), _from_blocks(dv_c)
    hr = C_ROPE // 2
    dk_sum = _sum_rows([dk_full[None, :, hh, :] for hh in range(N_HEADS)], tm=1024, name=n + "krope_sum")[0]
    dy1 = _rope_pack(dq_full[:, :, C_NOPE:C_NOPE + hr], dk_sum[:, C_NOPE:C_NOPE + hr])
    dy2 = _rope_pack(dq_full[:, :, C_NOPE + hr:C_NOPE + C_ROPE], dk_sum[:, C_NOPE + hr:C_NOPE + C_ROPE])
    dx1, dx2 = _rope(dy1, dy2, rope_cs[0], -rope_cs[1], name=n + "rope")
    dq_c = jnp.concatenate([dq_full[:, :, :C_NOPE], dx1[:, :LANE].reshape(t, N_HEADS, hr), dx2[:, :LANE].reshape(t, N_HEADS, hr)],
                           axis=2).reshape(t, 768).astype(BF16)
    dkv_c = jnp.concatenate([dk_full[:, :, :C_NOPE], dv_full[:, :, :C_NOPE]], axis=2).reshape(t, 1024).astype(BF16)
    d_rq = _mm(dq_c, w["w_uq"], tb=True, name=n + "d_rq")
    d_rkv = _mm(dkv_c, w["w_ukv"], tb=True, name=n + "d_rkv")
    gw["w_uq"] = _mm(tr(s["rq"]), dq_c, name=n + "g_uq")
    gw["w_ukv"] = _mm(tr(s["rkv"]), dkv_c, name=n + "g_ukv")
    dcq, gs["q_norm_g"] = _rms_bwd(s["proj"], C_Q_RANK, COL_CQ // C_Q_RANK, p["q_norm_g"], d_rq, name=n + "rms_q")
    dckv, gs["kv_norm_g"] = _rms_bwd(s["proj"], C_KV_RANK, COL_CDKV // C_KV_RANK, p["kv_norm_g"], d_rkv, name=n + "rms_kv")
    dcdkv = jnp.concatenate([dckv, dx1[:, LANE:LANE + hr].astype(BF16), dx2[:, LANE:LANE + hr].astype(BF16),
                             jnp.zeros((t, D_IN_P - D_IN), BF16)], axis=1)

    dproj = jnp.concatenate(dgp + [da_qkv, dbq, dbkv, dcq, dcdkv], axis=1)
    dx_proj = _mm(dproj, w["win_pt"], tk=1280, name=n + "d_x")
    gw["w_in"] = _mm(tr(s["xb"]), dproj, tn=1280, name=n + "g_in")
    return [dz1, dx_proj], [ALPHA, 1.0], gw, gs, ds_sum


def _local_step(x, target, ws, rel_table, small):
    t = x.shape[0]
    ps = [{k: small[k][l] for k in SMALL} for l in range(DEPTH)]
    bucket = _bucket_index()
    bias_t = _bias_lookup(bucket, rel_table.T, name="bias_lookup").reshape(4 * N_HEADS, 2 * BLK, BLK)
    rope_cs = _rope_tables(t)

    saved, h, hb = [], x, x.astype(BF16)
    for l in range(DEPTH):
        h, hb, s = _layer_fwd(l, h, hb, ws[l], ps[l], bias_t, rope_cs)
        saved.append(s)
    dy, loss_part = _loss_and_grad(h, target, name="loss")

    dys, coefs = [dy], [1.0]
    gws, gss, dss = [None] * DEPTH, [None] * DEPTH, [None] * DEPTH
    for l in reversed(range(DEPTH)):
        dys, coefs, gws[l], gss[l], dss[l] = _layer_bwd(l, saved[l], dys, coefs, ws[l], ps[l], bias_t, rope_cs)
    grad_x = _axpy(coefs[0], dys[0], dys[1], name="grad_x")
    npos = 2 * BLK * BLK
    g_rel = _bias_grad(bucket, dss[0].reshape(4 * N_HEADS, npos), dss[1].reshape(4 * N_HEADS, npos), name="bias_grad").T
    gsmall = {k: jnp.stack([gss[l][k] for l in range(DEPTH)]) for k in SMALL}
    return loss_part, grad_x, gws, gsmall, g_rel


def kernel(x, rel_table, w_in, b_gate, sinks, q_norm_g, kv_norm_g, w_uq, w_ukv, w_branch, w_out, ln1_g, ln1_b, w_ffn_up, conv_w, conv_b, w_ffn_down, ln2_g, ln2_b, loss_target, m_rel_table, m_w_in, m_b_gate, m_sinks, m_q_norm_g, m_kv_norm_g, m_w_uq, m_w_ukv, m_w_branch, m_w_out, m_ln1_g, m_ln1_b, m_w_ffn_up, m_conv_w, m_conv_b, m_w_ffn_down, m_ln2_g, m_ln2_b, v_rel_table, v_w_in, v_b_gate, v_sinks, v_q_norm_g, v_kv_norm_g, v_w_uq, v_w_ukv, v_w_branch, v_w_out, v_ln1_g, v_ln1_b, v_w_ffn_up, v_conv_w, v_conv_b, v_w_ffn_down, v_ln2_g, v_ln2_b):
    wts = dict(rel_table=rel_table, w_in=w_in, b_gate=b_gate, sinks=sinks, q_norm_g=q_norm_g, kv_norm_g=kv_norm_g, w_uq=w_uq,
               w_ukv=w_ukv, w_branch=w_branch, w_out=w_out, ln1_g=ln1_g, ln1_b=ln1_b, w_ffn_up=w_ffn_up, conv_w=conv_w,
               conv_b=conv_b, w_ffn_down=w_ffn_down, ln2_g=ln2_g, ln2_b=ln2_b)
    ms = dict(rel_table=m_rel_table, w_in=m_w_in, b_gate=m_b_gate, sinks=m_sinks, q_norm_g=m_q_norm_g, kv_norm_g=m_kv_norm_g,
              w_uq=m_w_uq, w_ukv=m_w_ukv, w_branch=m_w_branch, w_out=m_w_out, ln1_g=m_ln1_g, ln1_b=m_ln1_b, w_ffn_up=m_w_ffn_up,
              conv_w=m_conv_w, conv_b=m_conv_b, w_ffn_down=m_w_ffn_down, ln2_g=m_ln2_g, ln2_b=m_ln2_b)
    vs = dict(rel_table=v_rel_table, w_in=v_w_in, b_gate=v_b_gate, sinks=v_sinks, q_norm_g=v_q_norm_g, kv_norm_g=v_kv_norm_g,
              w_uq=v_w_uq, w_ukv=v_w_ukv, w_branch=v_w_branch, w_out=v_w_out, ln1_g=v_ln1_g, ln1_b=v_ln1_b, w_ffn_up=v_w_ffn_up,
              conv_w=v_conv_w, conv_b=v_conv_b, w_ffn_down=v_w_ffn_down, ln2_g=v_ln2_g, ln2_b=v_ln2_b)

    core = lax.axis_index("c")
    chip = 2 * lax.axis_index("x") + lax.axis_index("y")

    names = [name for name, _, _ in MATS]
    sent = [_weight_send(name, wts[name]) for name in names]
    got = _allgather_weights(sent)
    gathered = {name: lax.dynamic_update_slice(g, s[None], (chip,) + (0,) * s.ndim) for name, g, s in zip(names, got, sent)}
    ws = [_full_weights(gathered, l) for l in range(DEPTH)]

    small = {k: wts[k] for k in SMALL}
    loss_part, grad_x, gws, gsmall, g_rel = _local_step(x[0], loss_target[0], ws, rel_table, small)

    rnames = [name for name, _, _ in REDUCED]
    gsend = [jnp.stack([_grad_send(name, gws[l][name], shape, ax) for l in range(DEPTH)]) for name, shape, ax in REDUCED]
    theirs = _sibling_swap(gsend)
    pairs = [_pair_add(g, t_, core, name="grad_pair_" + name) for name, g, t_ in zip(rnames, gsend, theirs)]
    arrived = _chip_scatter(pairs)
    reduced = [_chip_add(lax.dynamic_index_in_dim(p, chip, 0, keepdims=False), a, name="grad_chip_" + name)
               for name, p, a in zip(rnames, pairs, arrived)]
    others = _sibling_share(reduced)
    gshard = {}
    for name, mine, other in zip(rnames, reduced, others):
        layers = [jnp.where(core == l, mine, other) for l in range(DEPTH)]
        gshard[name] = jnp.stack([_grad_recv(name, a) for a in layers])

    conv_w_full = jnp.stack([gws[l]["conv_w"] for l in range(DEPTH)])
    small_red = _allreduce_small(_pack_small(g_rel, gsmall, conv_w_full, loss_part))
    g_rel_r, gsmall_r, conv_w_r, loss_vec = _unpack_small(small_red)
    loss = loss_vec[0]
    shard_w = 2 * D_FF // N_CHIP
    gshard["conv_w"] = lax.dynamic_slice_in_dim(conv_w_r, chip * shard_w, shard_w, axis=2)

    grads = dict(gshard)
    grads.update(gsmall_r)
    grads["rel_table"] = g_rel_r
    deltas, new_m, new_v = {}, {}, {}
    for name, _, _ in MATS:
        shp = wts[name].shape
        v2 = lambda a: a.reshape(-1, shp[-1])
        d_, m_, v_ = _adamw(v2(wts[name]), v2(grads[name]), v2(ms[name]), v2(vs[name]), name="adamw_" + name)
        deltas[name], new_m[name], new_v[name] = d_.reshape(shp), m_.reshape(shp), v_.reshape(shp)
    zero, none = jnp.zeros((LANE,), F32), jnp.zeros((0,), F32)
    sw = _pack_small(wts["rel_table"], {k: wts[k] for k in SMALL}, none, zero)
    sm = _pack_small(ms["rel_table"], {k: ms[k] for k in SMALL}, none, zero)
    sv = _pack_small(vs["rel_table"], {k: vs[k] for k in SMALL}, none, zero)
    sg = _pack_small(g_rel_r, gsmall_r, none, zero)
    sd, smn, svn = _adamw(sw, sg, sm, sv, name="adamw_small")
    for res, buf in ((deltas, sd), (new_m, smn), (new_v, svn)):
        rel_, sm_ = _unpack_small(jnp.pad(buf, ((0, small_red.shape[0] - buf.shape[0]), (0, 0))))[:2]
        res["rel_table"] = rel_
        res.update(sm_)

    return (loss, grad_x[None], *[grads[k] for k in WEIGHT_ORDER], *[deltas[k] for k in WEIGHT_ORDER],
            *[new_m[k] for k in WEIGHT_ORDER], *[new_v[k] for k in WEIGHT_ORDER])
```

```python
import math

import jax
import jax.numpy as jnp
from jax import lax
from jax.experimental import pallas as pl
from jax.experimental.pallas import tpu as pltpu

F32 = jnp.float32
BF16 = jnp.bfloat16
MESH = pl.DeviceIdType.MESH

D_MODEL = 1024
DEPTH = 2
HEAD_DIM = 64
N_HEADS = 8
A_DILS = (1, 4, 16)
C_Q_RANK = 256
C_KV_RANK = 128
C_NOPE = 64
C_ROPE = 32
ROPE_BASE = 10000.0
REL_BUCKETS = 32
REL_MAX_DIST = 2048
D_FF = 2816
ALPHA = (2 * DEPTH) ** 0.25
LN_EPS = 1e-5
RMS_EPS = 1e-6
NEG = -1e30
ADAM_LR, ADAM_B1, ADAM_B2, ADAM_EPS, ADAM_WD, ADAM_STEP = 0.001, 0.9, 0.999, 1e-08, 0.01, 10

VMEM_LIMIT_BYTES = 56 * 1024 * 1024
LANE = 128
BLK = 128
TQ = 512
HP = 128
BAND_UNROLL = 16

D_IN = 8864
A_COLS = 3 * N_HEADS * HEAD_DIM
ORIG = {"a": 0, "bq": 4608, "bk": 5120, "bv": 5248, "cq": 5376, "cdkv": 5632, "gate": 5792}
M_GATE, M_A0, M_BQ, M_BK, M_BV, M_CQ, M_CDKV, M_COLS = 0, 3072, 4608, 5120, 5376, 5632, 5888, 6144

N_CHIP = 4
MATS = (
    ("w_in", (D_MODEL, D_IN), 1),
    ("w_uq", (C_Q_RANK, 768), 1),
    ("w_ukv", (C_KV_RANK, 1024), 1),
    ("w_branch", (3, 512, D_MODEL), 2),
    ("w_out", (D_MODEL, D_MODEL), 0),
    ("w_ffn_up", (D_MODEL, 2 * D_FF), 1),
    ("conv_w", (3, 2 * D_FF), 1),
    ("w_ffn_down", (D_FF, D_MODEL), 0),
)
SMALL = ("b_gate", "sinks", "q_norm_g", "kv_norm_g", "ln1_g", "ln1_b", "conv_b", "ln2_g", "ln2_b")
SMALL_SIZES = {"b_gate": 3072, "sinks": 8, "q_norm_g": 256, "kv_norm_g": 128, "ln1_g": 1024, "ln1_b": 1024,
               "conv_b": 5632, "ln2_g": 1024, "ln2_b": 1024}
WEIGHT_ORDER = ("rel_table", "w_in", "b_gate", "sinks", "q_norm_g", "kv_norm_g", "w_uq", "w_ukv", "w_branch",
                "w_out", "ln1_g", "ln1_b", "w_ffn_up", "conv_w", "conv_b", "w_ffn_down", "ln2_g", "ln2_b")


def _cparams(sem):
    return pltpu.CompilerParams(dimension_semantics=sem, vmem_limit_bytes=VMEM_LIMIT_BYTES)


def _shard_shape(shape, ax):
    s = list(shape)
    s[ax] //= N_CHIP
    return tuple(s)


def _ceil_to(n, m):
    return -(-n // m) * m


def _pick(n, target):
    if n <= target:
        return n
    best = None
    for t in range(LANE, target + 1, LANE):
        if n % t == 0:
            best = t
    assert best is not None, (n, target)
    return best


def _mm(a, b, *, tb=False, out_dtype=F32, tm=512, tn=1024, tk=2048, name):
    m, k = a.shape
    n = b.shape[0] if tb else b.shape[1]
    assert (b.shape[1] if tb else b.shape[0]) == k
    tm, tn, tk = _pick(m, tm), _pick(n, tn), _pick(k, tk)
    nk = k // tk
    dn = (((1,), (1,)), ((), ())) if tb else (((1,), (0,)), ((), ()))

    def body(a_ref, b_ref, o_ref, acc_ref):
        part = lax.dot_general(a_ref[...].astype(BF16), b_ref[...].astype(BF16), dn, preferred_element_type=F32)
        if nk == 1:
            o_ref[...] = part.astype(o_ref.dtype)
        else:
            kk = pl.program_id(2)

            @pl.when(kk == 0)
            def _():
                acc_ref[...] = part

            @pl.when(kk > 0)
            def _():
                acc_ref[...] += part

            @pl.when(kk == nk - 1)
            def _():
                o_ref[...] = acc_ref[...].astype(o_ref.dtype)

    b_spec = pl.BlockSpec((tn, tk), lambda i, j, kk: (j, kk)) if tb else pl.BlockSpec((tk, tn), lambda i, j, kk: (kk, j))
    return pl.pallas_call(
        body, name=name, grid=(m // tm, n // tn, nk),
        in_specs=[pl.BlockSpec((tm, tk), lambda i, j, kk: (i, kk)), b_spec],
        out_specs=pl.BlockSpec((tm, tn), lambda i, j, kk: (i, j)),
        out_shape=jax.ShapeDtypeStruct((m, n), out_dtype),
        scratch_shapes=[pltpu.VMEM((tm, tn) if nk > 1 else (8, LANE), F32)],
        compiler_params=_cparams(("parallel", "parallel", "arbitrary")),
    )(a, b)


def _rowwise(fn, rows, *, pars=(), halos=(), outs=(), accs=(), tm, name, ncol=1, t=None):
    nb = rows[0][0].shape[0]
    t = rows[0][0].shape[1] if t is None else t
    tm = min(tm, t)
    assert t % tm == 0 and tm % 8 == 0
    nt = t // tm
    in_specs, args = [], []
    for spec in rows:
        arr, c, off = spec[:3]
        rb = spec[3] if len(spec) > 3 else 0
        in_specs.append(pl.BlockSpec((1, tm, c), lambda b, cc, i, off=off, rb=rb: (b, i + rb, off + cc)))
        args.append(arr)
    for arr, c, off, kind in halos:
        if kind == "prev":
            im = lambda b, cc, i, off=off: (b, jnp.maximum(i * (tm // 8) - 1, 0), off + cc)
        else:
            im = lambda b, cc, i, off=off: (b, jnp.minimum((i + 1) * (tm // 8), t // 8 - 1), off + cc)
        in_specs.append(pl.BlockSpec((1, 8, c), im))
        args.append(arr)
    for arr, c, off in pars:
        bp, r = arr.shape[:2]
        if bp > 1:
            im = lambda b, cc, i, off=off: (b, 0, off + cc)
        else:
            im = lambda b, cc, i, off=off: (0, 0, off + cc)
        in_specs.append(pl.BlockSpec((1, r, c), im))
        args.append(arr)
    out_specs, out_shapes = [], []
    for ctot, c, off, dt in outs:
        out_specs.append(pl.BlockSpec((1, tm, c), lambda b, cc, i, off=off: (b, i, off + cc)))
        out_shapes.append(jax.ShapeDtypeStruct((nb, t, ctot), dt))
    for r, ctot, c, off in accs:
        out_specs.append(pl.BlockSpec((1, r, c), lambda b, cc, i, off=off: (b, 0, off + cc)))
        out_shapes.append(jax.ShapeDtypeStruct((nb, r, ctot), F32))
    n_in, n_out = len(args), len(outs)

    def body(*refs):
        i = pl.program_id(2)
        res = fn(i, nt, *[r[0].astype(F32) for r in refs[:n_in]])
        if not isinstance(res, (tuple, list)):
            res = (res,)
        for o_ref, val in zip(refs[n_in:n_in + n_out], res[:n_out]):
            o_ref[0] = val.astype(o_ref.dtype)
        for a_ref, val in zip(refs[n_in + n_out:], res[n_out:]):
            @pl.when(i == 0)
            def _(a_ref=a_ref, val=val):
                a_ref[0] = val

            @pl.when(i > 0)
            def _(a_ref=a_ref, val=val):
                a_ref[0] += val

    res = pl.pallas_call(
        body, name=name, grid=(nb, ncol, nt), in_specs=in_specs, out_specs=out_specs, out_shape=out_shapes,
        compiler_params=_cparams(("parallel", "parallel", "arbitrary")),
    )(*args)
    return res


def _lanewise(fn, ins, outs, *, tl, name):
    nb, _, t = ins[0].shape
    tl = min(tl, t)
    assert t % tl == 0
    n_in = len(ins)

    def body(*refs):
        res = fn(*[r[0] for r in refs[:n_in]])
        if not isinstance(res, (tuple, list)):
            res = (res,)
        for o_ref, val in zip(refs[n_in:], res):
            o_ref[0] = val.astype(o_ref.dtype)

    return pl.pallas_call(
        body, name=name, grid=(nb, t // tl),
        in_specs=[pl.BlockSpec((1, a.shape[1], tl), lambda b, i: (b, 0, i)) for a in ins],
        out_specs=[pl.BlockSpec((1, r, tl), lambda b, i: (b, 0, i)) for r, _ in outs],
        out_shape=[jax.ShapeDtypeStruct((nb, r, t), dt) for r, dt in outs],
        compiler_params=_cparams(("parallel", "parallel")),
    )(*ins)


def _dot(a, b):
    return lax.dot_general(a, b, (((1,), (0,)), ((), ())), preferred_element_type=F32)


def _dot_nt(a, b):
    return lax.dot_general(a, b, (((1,), (1,)), ((), ())), preferred_element_type=F32)


def _dot_tn(a, b):
    return lax.dot_general(a, b, (((0,), (0,)), ((), ())), preferred_element_type=F32)


BAND_ROWS = BAND_UNROLL * BLK


def _rows(parts):
    return jnp.concatenate(parts, axis=0)


def _lane_lo():
    return lax.broadcasted_iota(jnp.int32, (1, LANE), 1) < HEAD_DIM


def _blocks(a):
    return [a[i * BLK:(i + 1) * BLK] for i in range(BAND_UNROLL)]


def _band_operands(g, k_ref, v_ref):
    start = pl.multiple_of(g * BAND_ROWS, BAND_ROWS)
    pstart = pl.multiple_of(jnp.maximum(g * BAND_ROWS - BLK, 0), BLK)
    out = []
    for ref in (k_ref, v_ref):
        cur = ref[pl.ds(start, BAND_ROWS), :]
        raw = ref[pl.ds(pstart, BAND_ROWS), :]
        shifted = _rows([jnp.zeros((BLK, LANE), raw.dtype), raw[:BAND_ROWS - BLK]])
        out += [_blocks(cur), _blocks(jnp.where(g == 0, shifted, raw))]
    return out


def _band_scores(g, qa, kc, kp, b_ref, a, nb, lim):
    scale = HEAD_DIM ** -0.5
    qi = jnp.bitwise_and(lax.broadcasted_iota(jnp.int32, (BAND_ROWS, BLK), 0), BLK - 1)
    ki = lax.broadcasted_iota(jnp.int32, (BAND_ROWS, BLK), 1)
    tile = lambda blk: _rows([blk] * BAND_UNROLL)
    firsts = []
    for i in range(BAND_UNROLL):
        if nb >= BAND_UNROLL:
            val = jnp.where(lax.rem(g * BAND_UNROLL, nb) == 0, NEG, 0.0).astype(F32) if i == 0 else 0.0
        else:
            val = NEG if i % nb == 0 else 0.0
        firsts.append(jnp.zeros((BLK, 1), F32) + val)
    sc = _rows([_dot_nt(q, k) for q, k in zip(qa, kc)]) * scale + tile(b_ref[a, :, BLK:2 * BLK])
    sp = _rows([_dot_nt(q, k) for q, k in zip(qa, kp)]) * scale + tile(b_ref[a, :, 0:BLK])
    return jnp.where(ki <= qi, sc, NEG), jnp.where((BLK + qi - ki) <= lim, sp, NEG) + _rows(firsts)


def _band_fwd(src, offs, bias, sinks, *, nb, lim, gqa, name):
    t = src.shape[0]
    assert t % BAND_ROWS == 0 and (nb % BAND_UNROLL == 0 or BAND_UNROLL % nb == 0)
    qo, ko, vo = offs
    share = 2 if gqa else 1

    def body(sink_ref, q_ref, k_ref, v_ref, b_ref, o_ref, lse_ref):
        hp, g = pl.program_id(0), pl.program_id(1)
        lo = _lane_lo()
        q2 = q_ref[...]
        kc, kp, vc, vp = _band_operands(g, k_ref, v_ref)
        outs, lses = [], []
        for a in range(2):
            sink = sink_ref[2 * hp + a]
            qa = _blocks(jnp.where(lo if a == 0 else jnp.logical_not(lo), q2, jnp.zeros_like(q2)))
            sc, sp = _band_scores(g, qa, kc, kp, b_ref, a, nb, lim)
            m = jnp.maximum(jnp.maximum(jnp.max(sc, axis=1, keepdims=True), jnp.max(sp, axis=1, keepdims=True)), sink)
            pc, pp = jnp.exp(sc - m), jnp.exp(sp - m)
            l = jnp.sum(pc, axis=1, keepdims=True) + jnp.sum(pp, axis=1, keepdims=True) + jnp.exp(sink - m)
            inv = 1.0 / l
            pc_b, pp_b = _blocks((pc * inv).astype(BF16)), _blocks((pp * inv).astype(BF16))
            outs.append(_rows([_dot(pc_b[i], vc[i]) + _dot(pp_b[i], vp[i]) for i in range(BAND_UNROLL)]))
            lses.append(m + jnp.log(l))
        o_ref[...] = jnp.where(lo, outs[0], outs[1])
        lse_ref[...] = jnp.where(lo, lses[0], lses[1])

    slab = lambda off: pl.BlockSpec((BAND_ROWS, LANE), lambda hp, g, off=off: (g, off + hp))
    whole = lambda off: pl.BlockSpec((t, LANE), lambda hp, g, off=off: (0, off + hp // share))
    return pl.pallas_call(
        body, name=name, grid=(N_HEADS // 2, t // BAND_ROWS),
        in_specs=[pl.BlockSpec(memory_space=pltpu.SMEM), slab(qo), whole(ko), whole(vo),
                  pl.BlockSpec((2, BLK, 2 * BLK), lambda hp, g: (hp, 0, 0))],
        out_specs=[slab(0), slab(0)],
        out_shape=[jax.ShapeDtypeStruct((t, N_HEADS * HEAD_DIM), F32)] * 2,
        compiler_params=_cparams(("parallel", "parallel")),
    )(sinks, src, src, src, bias)


def _band_bwd(src, offs, do, lse, dpr, bias, sinks, *, nb, lim, gqa, name):
    t = src.shape[0]
    qo, ko, vo = offs
    share = 2 if gqa else 1
    nstep = t // BAND_ROWS
    scale = HEAD_DIM ** -0.5

    def fold(a):
        acc = a[0:BLK]
        for i in range(1, BAND_UNROLL):
            acc = acc + a[i * BLK:(i + 1) * BLK]
        return acc

    def body(sink_ref, q_ref, k_ref, v_ref, do_ref, lse_ref, dpr_ref, b_ref,
             dq_ref, dk_ref, dv_ref, ds_ref, dsink_ref, dk_acc, dv_acc):
        hp, g = pl.program_id(0), pl.program_id(1)
        lo = _lane_lo()
        hi = jnp.logical_not(lo)

        @pl.when(jnp.logical_and(g == 0, lax.rem(hp, share) == 0))
        def _():
            dk_acc[...] = jnp.zeros_like(dk_acc)
            dv_acc[...] = jnp.zeros_like(dv_acc)

        @pl.when(g == 0)
        def _():
            ds_ref[...] = jnp.zeros_like(ds_ref)
            dsink_ref[...] = jnp.zeros_like(dsink_ref)

        q2, do2, lse2, dpr2 = q_ref[...], do_ref[...], lse_ref[...], dpr_ref[...]
        lse_sw, dpr_sw = pltpu.roll(lse2, HEAD_DIM, axis=1), pltpu.roll(dpr2, HEAD_DIM, axis=1)
        kc, kp, vc, vp = _band_operands(g, k_ref, v_ref)
        dqs, dk_cur, dk_prev, dv_cur, dv_prev = [], None, None, None, None
        for a in range(2):
            sink = sink_ref[2 * hp + a]
            mine = lo if a == 0 else hi
            qa = _blocks(jnp.where(mine, q2, jnp.zeros_like(q2)))
            doa = _blocks(jnp.where(mine, do2, jnp.zeros_like(do2)))
            lse_a, dpr_a = jnp.where(mine, lse2, lse_sw), jnp.where(mine, dpr2, dpr_sw)
            sc, sp = _band_scores(g, qa, kc, kp, b_ref, a, nb, lim)
            pc, pp = jnp.exp(sc - lse_a), jnp.exp(sp - lse_a)
            dsc = pc * (_rows([_dot_nt(d, v) for d, v in zip(doa, vc)]) - dpr_a)
            dsp = pp * (_rows([_dot_nt(d, v) for d, v in zip(doa, vp)]) - dpr_a)
            ds_ref[a, :, BLK:2 * BLK] += fold(dsc)
            ds_ref[a, :, 0:BLK] += fold(dsp)
            dsink_ref[a] -= jnp.sum(jnp.exp(sink - lse_a) * dpr_a, axis=0, keepdims=True)
            dsc_b, dsp_b = _blocks((dsc * scale).astype(BF16)), _blocks((dsp * scale).astype(BF16))
            pc_b, pp_b = _blocks(pc.astype(BF16)), _blocks(pp.astype(BF16))
            dqs.append(_rows([_dot(dsc_b[i], kc[i]) + _dot(dsp_b[i], kp[i]) for i in range(BAND_UNROLL)]))
            parts = [_rows([_dot_tn(x[i], y[i]) for i in range(BAND_UNROLL)])
                     for x, y in ((dsc_b, qa), (dsp_b, qa), (pc_b, doa), (pp_b, doa))]
            if a == 0:
                dk_cur, dk_prev, dv_cur, dv_prev = parts
            else:
                dk_cur, dk_prev, dv_cur, dv_prev = dk_cur + parts[0], dk_prev + parts[1], dv_cur + parts[2], dv_prev + parts[3]
        dq_ref[...] = jnp.where(lo, dqs[0], dqs[1]).astype(dq_ref.dtype)
        start = pl.multiple_of(g * BAND_ROWS, BAND_ROWS)
        after = pl.multiple_of(g * BAND_ROWS + BLK, BLK)
        dk_acc[pl.ds(after, BAND_ROWS), :] += dk_cur
        dk_acc[pl.ds(start, BAND_ROWS), :] += dk_prev
        dv_acc[pl.ds(after, BAND_ROWS), :] += dv_cur
        dv_acc[pl.ds(start, BAND_ROWS), :] += dv_prev

        @pl.when(g == nstep - 1)
        def _():
            dk_ref[...] = dk_acc[BLK:, :].astype(dk_ref.dtype)
            dv_ref[...] = dv_acc[BLK:, :].astype(dv_ref.dtype)

    slab = lambda off: pl.BlockSpec((BAND_ROWS, LANE), lambda hp, g, off=off: (g, off + hp))
    whole = lambda off: pl.BlockSpec((t, LANE), lambda hp, g, off=off: (0, off + hp // share))
    per_pair = lambda shp: pl.BlockSpec((2,) + shp, lambda hp, g: (hp,) + (0,) * len(shp))
    kv_cols = N_HEADS * HEAD_DIM // share
    return pl.pallas_call(
        body, name=name, grid=(N_HEADS // 2, nstep),
        in_specs=[pl.BlockSpec(memory_space=pltpu.SMEM), slab(qo), whole(ko), whole(vo), slab(0), slab(0), slab(0),
                  per_pair((BLK, 2 * BLK))],
        out_specs=[slab(0), whole(0), whole(0), per_pair((BLK, 2 * BLK)), per_pair((1, LANE))],
        out_shape=[jax.ShapeDtypeStruct((t, N_HEADS * HEAD_DIM), BF16), jax.ShapeDtypeStruct((t, kv_cols), BF16),
                   jax.ShapeDtypeStruct((t, kv_cols), BF16), jax.ShapeDtypeStruct((N_HEADS, BLK, 2 * BLK), F32),
                   jax.ShapeDtypeStruct((N_HEADS, 1, LANE), F32)],
        scratch_shapes=[pltpu.VMEM((t + BLK, LANE), F32), pltpu.VMEM((t + BLK, LANE), F32)],
        compiler_params=_cparams(("arbitrary", "arbitrary")),
    )(sinks, src, src, src, do, lse, dpr, bias)


def _diag_mask():
    return lax.broadcasted_iota(jnp.int32, (TQ, TQ), 0) <= lax.broadcasted_iota(jnp.int32, (TQ, TQ), 1)


def _mla_fwd(q, k, vt, *, name):
    nh, n = q.shape[:2]
    scale = (C_NOPE + C_ROPE) ** -0.5

    def body(q_ref, k_ref, vt_ref, ot_ref, lse_ref, m_ref, l_ref, acc_ref):
        def qloop(qi, carry):
            qb = q_ref[0, qi]
            m_ref[...] = jnp.full_like(m_ref, NEG)
            l_ref[...] = jnp.zeros_like(l_ref)
            acc_ref[...] = jnp.zeros_like(acc_ref)

            def step(kj, diagonal):
                s = _dot_nt(k_ref[0, kj], qb) * scale
                if diagonal:
                    s = jnp.where(_diag_mask(), s, NEG)
                m_prev = m_ref[...]
                m_new = jnp.maximum(m_prev, jnp.max(s, axis=0, keepdims=True))
                a = jnp.exp(m_prev - m_new)
                p = jnp.exp(s - m_new)
                l_ref[...] = a * l_ref[...] + jnp.sum(p, axis=0, keepdims=True)
                acc_ref[...] = a * acc_ref[...] + _dot(vt_ref[0, kj], p.astype(BF16))
                m_ref[...] = m_new

            def kloop(kj, c2):
                step(kj, False)
                return c2

            lax.fori_loop(0, qi, kloop, 0)
            step(qi, True)
            ot_ref[0, qi] = acc_ref[...] * (1.0 / l_ref[...])
            lse_ref[0, qi] = m_ref[...] + jnp.log(l_ref[...])
            return carry

        lax.fori_loop(0, n, qloop, 0)

    whole = lambda shp: pl.BlockSpec((1,) + shp, lambda h: (h,) + (0,) * len(shp))
    return pl.pallas_call(
        body, name=name, grid=(nh,),
        in_specs=[whole((n, TQ, HP)), whole((n, TQ, HP)), whole((n, HP, TQ))],
        out_specs=[whole((n, HP, TQ)), whole((n, 1, TQ))],
        out_shape=[jax.ShapeDtypeStruct((nh, n, HP, TQ), F32), jax.ShapeDtypeStruct((nh, n, 1, TQ), F32)],
        scratch_shapes=[pltpu.VMEM((1, TQ), F32), pltpu.VMEM((1, TQ), F32), pltpu.VMEM((HP, TQ), F32)],
        compiler_params=_cparams(("parallel",)),
    )(q, k, vt)


def _mla_bwd(q, k, kt, v, do, lse, delta, *, name):
    nh, n = q.shape[:2]
    scale = (C_NOPE + C_ROPE) ** -0.5

    def body(q_ref, k_ref, kt_ref, v_ref, do_ref, lse_ref, dl_ref, dqt_ref, dk_ref, dv_ref, dk_acc, dv_acc):
        dqt_ref[...] = jnp.zeros_like(dqt_ref)

        def kloop(kj, carry):
            kb, vb, ktb = k_ref[0, kj], v_ref[0, kj], kt_ref[0, kj]
            dk_acc[...] = jnp.zeros_like(dk_acc)
            dv_acc[...] = jnp.zeros_like(dv_acc)

            def step(qi, diagonal):
                qb, dob = q_ref[0, qi], do_ref[0, qi]
                s = _dot_nt(kb, qb) * scale
                if diagonal:
                    s = jnp.where(_diag_mask(), s, NEG)
                p = jnp.exp(s - lse_ref[0, qi])
                ds = (p * (_dot_nt(vb, dob) - dl_ref[0, qi]) * scale).astype(BF16)
                dv_acc[...] += _dot(p.astype(BF16), dob)
                dk_acc[...] += _dot(ds, qb)
                dqt_ref[0, qi] += _dot(ktb, ds)

            def qloop(qi, c2):
                step(qi, False)
                return c2

            step(kj, True)
            lax.fori_loop(kj + 1, n, qloop, 0)
            dk_ref[0, kj] = dk_acc[...]
            dv_ref[0, kj] = dv_acc[...]
            return carry

        lax.fori_loop(0, n, kloop, 0)

    whole = lambda shp: pl.BlockSpec((1,) + shp, lambda h: (h,) + (0,) * len(shp))
    td, dt, st = (n, TQ, HP), (n, HP, TQ), (n, 1, TQ)
    return pl.pallas_call(
        body, name=name, grid=(nh,),
        in_specs=[whole(td), whole(td), whole(dt), whole(td), whole(td), whole(st), whole(st)],
        out_specs=[whole(dt), whole(td), whole(td)],
        out_shape=[jax.ShapeDtypeStruct((nh,) + dt, F32), jax.ShapeDtypeStruct((nh,) + td, F32),
                   jax.ShapeDtypeStruct((nh,) + td, F32)],
        scratch_shapes=[pltpu.VMEM((TQ, HP), F32), pltpu.VMEM((TQ, HP), F32)],
        compiler_params=_cparams(("parallel",)),
    )(q, k, kt, v, do, lse, delta)


def _bias_lookup(bucket, table_t, *, name):
    nh, npos = bucket.shape
    tp = 4096

    def body(b_ref, t_ref, o_ref):
        bk, tab = b_ref[...], t_ref[...]
        acc = jnp.zeros(bk.shape, F32)
        for i in range(REL_BUCKETS):
            acc = jnp.where(bk == i, tab[:, i:i + 1], acc)
        o_ref[...] = acc

    return pl.pallas_call(
        body, name=name, grid=(npos // tp,),
        in_specs=[pl.BlockSpec((nh, tp), lambda i: (0, i)), pl.BlockSpec((nh, REL_BUCKETS), lambda i: (0, 0))],
        out_specs=pl.BlockSpec((nh, tp), lambda i: (0, i)),
        out_shape=jax.ShapeDtypeStruct((nh, npos), F32),
        compiler_params=_cparams(("parallel",)),
    )(bucket, table_t)


def _bias_grad(bucket, ds0, ds1, *, name):
    nh, npos = bucket.shape
    tp = 4096

    def body(b_ref, a_ref, c_ref, o_ref):
        i = pl.program_id(0)
        bk, ds = b_ref[...], a_ref[...] + c_ref[...]
        lane = lax.broadcasted_iota(jnp.int32, (nh, REL_BUCKETS), 1)
        acc = jnp.zeros((nh, REL_BUCKETS), F32)
        for j in range(REL_BUCKETS):
            col = jnp.sum(jnp.where(bk == j, ds, 0.0), axis=1, keepdims=True)
            acc = acc + jnp.where(lane == j, col, 0.0)

        @pl.when(i == 0)
        def _():
            o_ref[...] = acc

        @pl.when(i > 0)
        def _():
            o_ref[...] += acc

    return pl.pallas_call(
        body, name=name, grid=(npos // tp,),
        in_specs=[pl.BlockSpec((nh, tp), lambda i: (0, i))] * 3,
        out_specs=pl.BlockSpec((nh, REL_BUCKETS), lambda i: (0, 0)),
        out_shape=jax.ShapeDtypeStruct((nh, REL_BUCKETS), F32),
        compiler_params=_cparams(("arbitrary",)),
    )(bucket, ds0, ds1)


def _t5_bucket(dist):
    n = jnp.maximum(dist, 0)
    max_exact = REL_BUCKETS // 2
    scaled = jnp.log(jnp.maximum(n, 1).astype(F32) / max_exact) / math.log(REL_MAX_DIST / max_exact)
    large = max_exact + (scaled * (REL_BUCKETS - max_exact)).astype(jnp.int32)
    return jnp.where(n < max_exact, n, jnp.minimum(large, REL_BUCKETS - 1))


def _bucket_index():
    qi = jnp.arange(BLK)[:, None]
    ci = jnp.arange(2 * BLK)[None, :]
    step = BLK + qi - ci
    per_group = [_t5_bucket(step * d).reshape(1, -1) for d in A_DILS + (1,)]
    return jnp.concatenate([jnp.tile(b, (N_HEADS, 1)) for b in per_group], axis=0).astype(jnp.int32)


def _sigmoid(x):
    return 1.0 / (1.0 + jnp.exp(-x))


def _ln_stats(z):
    mu = jnp.mean(z, axis=-1, keepdims=True)
    zc = z - mu
    var = jnp.mean(zc * zc, axis=-1, keepdims=True)
    return zc * lax.rsqrt(var + LN_EPS)


def _ln_fwd(x, mix, g, b, *, name):
    def fn(i, nt, xv, mv, gv, bv):
        z = ALPHA * xv + mv
        y = _ln_stats(z) * gv + bv
        return y, y, z

    c = x.shape[-1]
    y, yb, z = _rowwise(fn, [(x[None], c, 0), (mix[None], c, 0)], pars=[(g.reshape(1, 1, c), c, 0), (b.reshape(1, 1, c), c, 0)],
                        outs=[(c, c, 0, F32), (c, c, 0, BF16), (c, c, 0, F32)], tm=512, name=name)
    return y[0], yb[0], z[0]


def _ln_bwd(z, g, dys, coefs, *, name):
    n = len(dys)

    def fn(i, nt, zv, *rest):
        gv = rest[n]
        dy = coefs[0] * rest[0]
        for cf, t in zip(coefs[1:], rest[1:n]):
            dy = dy + cf * t
        mu = jnp.mean(zv, axis=-1, keepdims=True)
        zc = zv - mu
        r = lax.rsqrt(jnp.mean(zc * zc, axis=-1, keepdims=True) + LN_EPS)
        xh = zc * r
        dxh = dy * gv
        dz = r * (dxh - jnp.mean(dxh, axis=-1, keepdims=True) - xh * jnp.mean(dxh * xh, axis=-1, keepdims=True))
        return dz, dz, jnp.sum(dy * xh, axis=0, keepdims=True), jnp.sum(dy, axis=0, keepdims=True)

    c = z.shape[-1]
    dz, dzb, dg, db = _rowwise(fn, [(z[None], c, 0)] + [(d[None], c, 0) for d in dys], pars=[(g.reshape(1, 1, c), c, 0)],
                               outs=[(c, c, 0, F32), (c, c, 0, BF16)], accs=[(1, c, c, 0), (1, c, c, 0)], tm=512, name=name)
    return dz[0], dzb[0], dg.reshape(c), db.reshape(c)


def _rms_fwd(src, c, off, g, *, name):
    def fn(i, nt, xv, gv):
        return xv * lax.rsqrt(jnp.mean(xv * xv, axis=-1, keepdims=True) + RMS_EPS) * gv

    return _rowwise(fn, [(src[None], c, off)], pars=[(g.reshape(1, 1, c), c, 0)], outs=[(c, c, 0, BF16)], tm=1024, name=name)[0][0]


def _rms_bwd(src, c, off, g, dy, *, name):
    def fn(i, nt, xv, dyv, gv):
        r = lax.rsqrt(jnp.mean(xv * xv, axis=-1, keepdims=True) + RMS_EPS)
        gd = gv * dyv
        dx = gd * r - xv * (r * r * r) * jnp.mean(gd * xv, axis=-1, keepdims=True)
        return dx, jnp.sum(dyv * xv * r, axis=0, keepdims=True)

    dx, dg = _rowwise(fn, [(src[None], c, off), (dy[None], c, 0)], pars=[(g.reshape(1, 1, c), c, 0)],
                      outs=[(c, c, 0, BF16)], accs=[(1, c, c, 0)], tm=1024, name=name)
    return dx[0], dg.reshape(c)


def _rope(x1, x2, cos, sin, *, name):
    def fn(i, nt, a, b, c, s):
        return a * c - b * s, a * s + b * c

    w = x1.shape[-1]
    y1, y2 = _rowwise(fn, [(x1[None], w, 0), (x2[None], w, 0), (cos[None], w, 0), (sin[None], w, 0)],
                      outs=[(w, w, 0, F32), (w, w, 0, F32)], tm=1024, name=name)
    return y1[0], y2[0]


def _merge_fwd(proj, b_gate, ys, *, name):
    def fn(i, nt, g0, g1, g2, ya, yb, yc, bg):
        return (_sigmoid(g0 + bg[:, 0:1024]) * ya + _sigmoid(g1 + bg[:, 1024:2048]) * yb
                + _sigmoid(g2 + bg[:, 2048:3072]) * yc)

    rows = [(proj[None], 1024, j) for j in range(3)] + [(y[None], 1024, 0) for y in ys]
    return _rowwise(fn, rows, pars=[(b_gate.reshape(1, 1, 3072), 3072, 0)], outs=[(1024, 1024, 0, BF16)], tm=512, name=name)[0][0]


def _merge_bwd(proj, b_gate, ys, dm, *, name):
    def fn(i, nt, g0, g1, g2, ya, yb, yc, dmv, bg):
        outs, dgs = [], []
        for j, (gp, y) in enumerate(((g0, ya), (g1, yb), (g2, yc))):
            s = _sigmoid(gp + bg[:, j * 1024:(j + 1) * 1024])
            outs.append(s * dmv)
            dgs.append(dmv * y * s * (1.0 - s))
        return outs + dgs + [jnp.sum(d, axis=0, keepdims=True) for d in dgs]

    rows = [(proj[None], 1024, j) for j in range(3)] + [(y[None], 1024, 0) for y in ys] + [(dm[None], 1024, 0)]
    res = _rowwise(fn, rows, pars=[(b_gate.reshape(1, 1, 3072), 3072, 0)], outs=[(1024, 1024, 0, BF16)] * 6,
                   accs=[(1, 1024, 1024, 0)] * 3, tm=256, name=name)
    dys = [r[0] for r in res[0:3]]
    dgp = [r[0] for r in res[3:6]]
    dbg = jnp.concatenate([r.reshape(1024) for r in res[6:9]])
    return dys, dgp, dbg


def _shift_down(u, halo, i, k):
    ext = jnp.concatenate([jnp.where(i > 0, halo, 0.0), u], axis=0)
    return pltpu.roll(ext, k, axis=0)[8:]


def _shift_up(u, halo, i, nt, k):
    ext = jnp.concatenate([u, jnp.where(i < nt - 1, halo, 0.0)], axis=0)
    n = ext.shape[0]
    return pltpu.roll(ext, n - k, axis=0)[:n - 8]


GLU_C = D_FF // 2


def _conv(u, halo, i, w, b):
    return w[0:1] * _shift_down(u, halo, i, 2) + w[1:2] * _shift_down(u, halo, i, 1) + w[2:3] * u + b


def _glu_fwd(ug, uv, conv_w, conv_b, *, name):
    def fn(i, nt, g, v, hg, hv, wg, wv, bg, bv):
        cg, cv = _conv(g, hg, i, wg, bg), _conv(v, hv, i, wv, bv)
        return cg * _sigmoid(cg) * cv

    w3, b3 = conv_w[None], conv_b.reshape(1, 1, -1)
    c = GLU_C
    return _rowwise(fn, [(ug[None], c, 0), (uv[None], c, 0)], halos=[(ug[None], c, 0, "prev"), (uv[None], c, 0, "prev")],
                    pars=[(w3, c, 0), (w3, c, 2), (b3, c, 0), (b3, c, 2)], outs=[(D_FF, c, 0, BF16)], tm=256, ncol=2, name=name)[0][0]


def _glu_bwd_a(ug, uv, conv_w, conv_b, dh, *, name):
    def fn(i, nt, g, v, dhv, hg, hv, wg, wv, bg, bv):
        g1, g2 = _shift_down(g, hg, i, 1), _shift_down(g, hg, i, 2)
        v1, v2 = _shift_down(v, hv, i, 1), _shift_down(v, hv, i, 2)
        cg = wg[0:1] * g2 + wg[1:2] * g1 + wg[2:3] * g + bg
        cv = wv[0:1] * v2 + wv[1:2] * v1 + wv[2:3] * v + bv
        s = _sigmoid(cg)
        dcv = dhv * cg * s
        dcg = dhv * cv * (s * (1.0 + cg * (1.0 - s)))
        red = lambda a: jnp.sum(a, axis=0, keepdims=True)
        return (dcg, dcv, red(dcg), red(dcv), red(dcg * g2), red(dcg * g1), red(dcg * g),
                red(dcv * v2), red(dcv * v1), red(dcv * v))

    w3, b3 = conv_w[None], conv_b.reshape(1, 1, -1)
    c = GLU_C
    res = _rowwise(fn, [(ug[None], c, 0), (uv[None], c, 0), (dh[None], c, 0)],
                   halos=[(ug[None], c, 0, "prev"), (uv[None], c, 0, "prev")],
                   pars=[(w3, c, 0), (w3, c, 2), (b3, c, 0), (b3, c, 2)],
                   outs=[(D_FF, c, 0, F32), (D_FF, c, 0, F32)], accs=[(1, D_FF, c, 0)] * 8, tm=256, ncol=2, name=name)
    dcg, dcv = res[0][0], res[1][0]
    dconv_b = jnp.concatenate([res[2].reshape(D_FF), res[3].reshape(D_FF)])
    dconv_w = jnp.concatenate([jnp.concatenate([res[4 + j].reshape(1, D_FF) for j in range(3)], axis=0),
                               jnp.concatenate([res[7 + j].reshape(1, D_FF) for j in range(3)], axis=0)], axis=1)
    return dcg, dcv, dconv_w, dconv_b


def _glu_bwd_b(dc, conv_w, half, *, name):
    def fn(i, nt, d, hd, w):
        return w[2:3] * d + w[1:2] * _shift_up(d, hd, i, nt, 1) + w[0:1] * _shift_up(d, hd, i, nt, 2)

    c = GLU_C
    return _rowwise(fn, [(dc[None], c, 0)], halos=[(dc[None], c, 0, "next")], pars=[(conv_w[None], c, 2 * half)],
                    outs=[(D_FF, c, 0, BF16)], tm=256, ncol=2, name=name)[0][0]


def _loss_and_grad(y, tgt, *, name):
    def fn(i, nt, yv, tv):
        err = yv - tv
        part = jnp.sum(jnp.sum(err * err, axis=0, keepdims=True), axis=1, keepdims=True) * (0.5 / D_MODEL)
        return err * (1.0 / D_MODEL), jnp.zeros((1, LANE), F32) + part

    dy, part = _rowwise(fn, [(y[None], D_MODEL, 0), (tgt[None], D_MODEL, 0)], outs=[(D_MODEL, D_MODEL, 0, F32)],
                        accs=[(1, LANE, LANE, 0)], tm=512, name=name)
    return dy[0], part.reshape(LANE)


def _lincomb(terms, coefs, *, name):
    def fn(i, nt, *vs):
        acc = coefs[0] * vs[0]
        for cf, v in zip(coefs[1:], vs[1:]):
            acc = acc + cf * v
        return acc

    c = terms[0].shape[-1]
    return _rowwise(fn, [(a[None], c, 0) for a in terms], outs=[(c, c, 0, F32)], tm=512, name=name)[0][0]


def _sum_rows(terms, *, tm, name, dtype=F32):
    def fn(i, nt, *vs):
        acc = vs[0]
        for v in vs[1:]:
            acc = acc + v
        return acc

    c = terms[0].shape[-1]
    return _rowwise(fn, [(t, c, 0) for t in terms], outs=[(c, c, 0, dtype)], tm=tm, name=name)[0]


def _head_sums(x):
    lo = _lane_lo()
    parts = []
    for j in range(x.shape[1] // LANE):
        blk = x[:, j * LANE:(j + 1) * LANE]
        s_lo = jnp.sum(jnp.where(lo, blk, 0.0), axis=1, keepdims=True)
        s_hi = jnp.sum(jnp.where(lo, 0.0, blk), axis=1, keepdims=True)
        parts.append(jnp.where(lo, s_lo, s_hi))
    return jnp.concatenate(parts, axis=1)


def _group_weights(l0, l1, l2):
    m = jnp.maximum(jnp.maximum(l0, l1), l2)
    es = [jnp.exp(l - m) for l in (l0, l1, l2)]
    inv = 1.0 / (es[0] + es[1] + es[2])
    return [e * inv for e in es]


def _combine_fwd(os_, lses, *, name):
    def fn(i, nt, o0, o1, o2, l0, l1, l2):
        w = _group_weights(l0, l1, l2)
        return w[0] * o0 + w[1] * o1 + w[2] * o2

    c = os_[0].shape[-1]
    return _rowwise(fn, [(a[None], c, 0) for a in list(os_) + list(lses)], outs=[(c, c, 0, BF16)], tm=512, name=name)[0][0]


def _combine_bwd(os_, lses, do_a, *, name):
    def fn(i, nt, o0, o1, o2, l0, l1, l2, da):
        ws = _group_weights(l0, l1, l2)
        dws = [_head_sums(da * o) for o in (o0, o1, o2)]
        mean = ws[0] * dws[0] + ws[1] * dws[1] + ws[2] * dws[2]
        return [w * da for w in ws] + [w * mean for w in ws]

    c = do_a.shape[-1]
    res = _rowwise(fn, [(a[None], c, 0) for a in list(os_) + list(lses) + [do_a]], outs=[(c, c, 0, BF16)] * 3 + [(c, c, 0, F32)] * 3,
                   tm=256, name=name)
    return [r[0] for r in res[0:3]], [r[0] for r in res[3:6]]


def _delta(do, o, *, name):
    def fn(i, nt, d, ov):
        return d, _head_sums(d * ov)

    c = do.shape[-1]
    res = _rowwise(fn, [(do[None], c, 0), (o[None], c, 0)], outs=[(c, c, 0, BF16), (c, c, 0, F32)], tm=512, name=name)
    return res[0][0], res[1][0]


def _rowdot(at, bt, *, name):
    def fn(a, b):
        return jnp.sum(a * b, axis=0, keepdims=True)

    return _lanewise(fn, [at, bt], [(1, F32)], tl=2048, name=name)[0]


def _adamw(w, g, m, v, *, name):
    c1 = 1.0 - ADAM_B1 ** ADAM_STEP
    c2 = 1.0 - ADAM_B2 ** ADAM_STEP

    def fn(i, nt, wv, gv, mv, vv):
        mn = ADAM_B1 * mv + (1.0 - ADAM_B1) * gv
        vn = ADAM_B2 * vv + (1.0 - ADAM_B2) * (gv * gv)
        delta = -ADAM_LR * ((mn / c1) / (jnp.sqrt(vn / c2) + ADAM_EPS) + ADAM_WD * wv)
        return delta, mn, vn

    r, c = w.shape
    rp = _ceil_to(r, 8)
    pad = lambda a: jnp.pad(a, ((0, rp - r), (0, 0))) if rp != r else a
    tm = rp
    for cand in (128, 64, 32, 16, 8):
        if rp % cand == 0:
            tm = cand
            break
    res = _rowwise(fn, [(pad(a)[None], c, 0) for a in (w, g, m, v)], outs=[(c, c, 0, F32)] * 3, tm=tm, name=name)
    return [x[0][:r] for x in res]


ANY = pl.BlockSpec(memory_space=pl.ANY)


def _place():
    x, y, c = lax.axis_index("x"), lax.axis_index("y"), lax.axis_index("c")
    chips = [(1 - x, y), (x, 1 - y), (1 - x, 1 - y)]
    return x, y, c, chips


def _allgather_weights(arrs):
    n = len(arrs)

    def body(*refs):
        ins, outs, send_sems, recv_sems = refs[:n], refs[n:2 * n], refs[2 * n], refs[2 * n + 1]
        x, y, c, chips = _place()
        j = 2 * x + y

        def cp(i, k, src, chip_idx, half, to):
            return pltpu.make_async_remote_copy(src_ref=src, dst_ref=outs[i].at[chip_idx, half], send_sem=send_sems.at[k],
                                                recv_sem=recv_sems.at[k], device_id=to, device_id_type=MESH)

        first, passed = [], []
        for i in range(n):
            for r, (cx, cy) in enumerate(chips):
                first.append(cp(i, 3 * i + r, ins[i].at[c], j, c, (cx, cy, c)))
                passed.append(cp(i, 3 * (n + i) + r, outs[i].at[2 * cx + cy, c], 2 * cx + cy, c, (x, y, 1 - c)))
        for d in first:
            d.start()
        for i in range(n):
            for r, (cx, cy) in enumerate(chips):
                cp(i, 3 * i + r, ins[i].at[c], 2 * cx + cy, c, (x, y, c)).wait_recv()
                passed[3 * i + r].start()
        for i in range(n):
            for r, (cx, cy) in enumerate(chips):
                cp(i, 3 * (n + i) + r, ins[i].at[c], 2 * cx + cy, 1 - c, (x, y, c)).wait_recv()
        for d in first + passed:
            d.wait_send()

    return pl.pallas_call(
        body, name="allgather_weights", in_specs=[ANY] * n, out_specs=[ANY] * n,
        out_shape=[jax.ShapeDtypeStruct((N_CHIP,) + a.shape, a.dtype) for a in arrs],
        scratch_shapes=[pltpu.SemaphoreType.DMA((6 * n,)), pltpu.SemaphoreType.DMA((6 * n,))],
    )(*arrs)


def _sibling_swap(gs):
    n = len(gs)

    def body(*refs):
        ins, outs, send_sems, recv_sems = refs[:n], refs[n:2 * n], refs[2 * n], refs[2 * n + 1]
        x, y, c, _ = _place()
        cps = [pltpu.make_async_remote_copy(src_ref=ins[i].at[1 - c], dst_ref=outs[i], send_sem=send_sems.at[i],
                                            recv_sem=recv_sems.at[i], device_id=(x, y, 1 - c), device_id_type=MESH)
               for i in range(n)]
        for d in cps:
            d.start()
        for d in cps:
            d.wait_recv()
        for d in cps:
            d.wait_send()

    return pl.pallas_call(
        body, name="grad_sibling_swap", in_specs=[ANY] * n, out_specs=[ANY] * n,
        out_shape=[jax.ShapeDtypeStruct(g.shape[1:], g.dtype) for g in gs],
        scratch_shapes=[pltpu.SemaphoreType.DMA((n,)), pltpu.SemaphoreType.DMA((n,))],
    )(*gs)


def _chip_scatter(ps):
    n = len(ps)

    def body(*refs):
        ins, outs, send_sems, recv_sems = refs[:n], refs[n:2 * n], refs[2 * n], refs[2 * n + 1]
        x, y, c, chips = _place()
        sends = []
        for i in range(n):
            for r, (cx, cy) in enumerate(chips):
                sends.append(pltpu.make_async_remote_copy(src_ref=ins[i].at[2 * cx + cy], dst_ref=outs[i].at[r], send_sem=send_sems.at[3 * i + r],
                                                          recv_sem=recv_sems.at[3 * i + r], device_id=(cx, cy, c), device_id_type=MESH))
        for d in sends:
            d.start()
        for d in sends:
            d.wait_recv()
        for d in sends:
            d.wait_send()

    return pl.pallas_call(
        body, name="grad_chip_scatter", in_specs=[ANY] * n, out_specs=[ANY] * n,
        out_shape=[jax.ShapeDtypeStruct((3,) + p.shape[1:], p.dtype) for p in ps],
        scratch_shapes=[pltpu.SemaphoreType.DMA((3 * n,)), pltpu.SemaphoreType.DMA((3 * n,))],
    )(*ps)


def _sibling_share(rs):
    n = len(rs)

    def body(*refs):
        ins, outs, send_sems, recv_sems = refs[:n], refs[n:2 * n], refs[2 * n], refs[2 * n + 1]
        x, y, c, _ = _place()
        cps = [pltpu.make_async_remote_copy(src_ref=ins[i], dst_ref=outs[i], send_sem=send_sems.at[i], recv_sem=recv_sems.at[i],
                                            device_id=(x, y, 1 - c), device_id_type=MESH) for i in range(n)]
        for d in cps:
            d.start()
        for d in cps:
            d.wait_recv()
        for d in cps:
            d.wait_send()

    return pl.pallas_call(
        body, name="grad_sibling_share", in_specs=[ANY] * n, out_specs=[ANY] * n,
        out_shape=[jax.ShapeDtypeStruct(r.shape, r.dtype) for r in rs],
        scratch_shapes=[pltpu.SemaphoreType.DMA((n,)), pltpu.SemaphoreType.DMA((n,))],
    )(*rs)


def _allreduce_small(s):
    rows, w = s.shape
    n_dev = 8

    def body(s_ref, out_ref, slots, send_sems, recv_sems):
        x, y, c, _ = _place()
        me = 4 * x + 2 * y + c
        slots[me] = s_ref[...]
        peers = []
        for r in range(1, n_dev):
            px = 1 - x if r & 4 else x
            py = 1 - y if r & 2 else y
            pc = 1 - c if r & 1 else c
            peers.append((px, py, pc))
        sends = [pltpu.make_async_remote_copy(src_ref=s_ref, dst_ref=slots.at[me], send_sem=send_sems.at[r], recv_sem=recv_sems.at[r],
                                              device_id=peer, device_id_type=MESH) for r, peer in enumerate(peers)]
        for d in sends:
            d.start()
        for r, (px, py, pc) in enumerate(peers):
            pltpu.make_async_remote_copy(src_ref=s_ref, dst_ref=slots.at[4 * px + 2 * py + pc], send_sem=send_sems.at[r],
                                         recv_sem=recv_sems.at[r], device_id=(x, y, c), device_id_type=MESH).wait_recv()
        for d in sends:
            d.wait_send()
        acc = slots[0]
        for k in range(1, n_dev):
            acc = acc + slots[k]
        out_ref[...] = acc

    vm = pl.BlockSpec(memory_space=pltpu.VMEM)
    return pl.pallas_call(
        body, name="allreduce_small", in_specs=[vm], out_specs=vm, out_shape=jax.ShapeDtypeStruct((rows, w), F32),
        scratch_shapes=[pltpu.VMEM((n_dev, rows, w), F32), pltpu.SemaphoreType.DMA((n_dev - 1,)), pltpu.SemaphoreType.DMA((n_dev - 1,))],
    )(s)


W_IN_SHARD = D_IN // N_CHIP
W_IN_ROWS_G = 2304
REDUCED = tuple(m for m in MATS if m[0] != "conv_w")
CONV_W_SIZE = 3 * 2 * D_FF


def _weight_send(name, a):
    if name == "w_in":
        return jnp.swapaxes(a, 1, 2).astype(BF16)
    return a if name == "conv_w" else a.astype(BF16)


def _full_weights(gathered, l):
    g = {k: v[:, l] for k, v in gathered.items()}
    s = g["w_in"].astype(F32).reshape(D_IN, D_MODEL)
    dup = lambda a: jnp.concatenate([a[0:64], a[0:64], a[64:128], a[64:128]], axis=0)
    o = ORIG
    wm_t = jnp.concatenate([s[o["gate"]:], s[o["a"]:o["a"] + A_COLS], s[o["bq"]:o["bk"]], dup(s[o["bk"]:o["bv"]]), dup(s[o["bv"]:o["cq"]]),
                            s[o["cq"]:o["gate"]], jnp.zeros((M_COLS - M_CDKV - (o["gate"] - o["cdkv"]), D_MODEL), F32)], axis=0).astype(BF16)
    wg_t = [s[o["a"] + gi * A_COLS:o["a"] + (gi + 1) * A_COLS].astype(BF16) for gi in (1, 2)]
    full = {name: jnp.moveaxis(g[name], 0, ax).reshape(shape) for name, shape, ax in MATS if name != "w_in"}
    return {"wm_t": wm_t, "wg_t": wg_t, "w_uq": full["w_uq"], "w_ukv": full["w_ukv"], "w_branch": full["w_branch"], "w_out": full["w_out"],
            "wup_g": full["w_ffn_up"][:, :D_FF], "wup_v": full["w_ffn_up"][:, D_FF:], "conv_w": full["conv_w"],
            "w_ffn_down": full["w_ffn_down"]}


def _grad_send(name, g, shape, ax):
    if name == "w_in":
        return jnp.pad(g.reshape(N_CHIP, W_IN_SHARD, D_MODEL), ((0, 0), (0, W_IN_ROWS_G - W_IN_SHARD), (0, 0)))
    split = shape[:ax] + (N_CHIP, shape[ax] // N_CHIP) + shape[ax + 1:]
    return jnp.moveaxis(g.reshape(split), ax, 0)


def _grad_recv(name, r):
    return r[:W_IN_SHARD].T if name == "w_in" else r


def _pack_small(rel, small, conv_w, extra):
    parts = [rel.reshape(-1)]
    for l in range(DEPTH):
        for name in SMALL:
            parts.append(small[name][l].reshape(-1))
    parts += [conv_w.reshape(-1), extra]
    flat = jnp.concatenate(parts)
    rows = _ceil_to(-(-flat.shape[0] // LANE), 8)
    return jnp.pad(flat, (0, rows * LANE - flat.shape[0])).reshape(rows, LANE)


def _unpack_small(buf):
    flat = buf.reshape(-1)
    rel = flat[:REL_BUCKETS * 32].reshape(REL_BUCKETS, 32)
    off = REL_BUCKETS * 32
    small = {name: [] for name in SMALL}
    for l in range(DEPTH):
        for name in SMALL:
            n = SMALL_SIZES[name]
            small[name].append(flat[off:off + n])
            off += n
    conv_w = flat[off:off + DEPTH * CONV_W_SIZE].reshape(DEPTH, 3, 2 * D_FF)
    off += DEPTH * CONV_W_SIZE
    return rel, {k: jnp.stack(v) for k, v in small.items()}, conv_w, flat[off:off + LANE]


def _rows2d(a, lead):
    return a.reshape(a.shape[:lead] + (-1, a.shape[-1]))


def _row_tile(rows):
    for cand in (512, 256, 128, 64, 32, 16, 8):
        if rows % cand == 0:
            return cand
    raise ValueError(rows)


def _pair_add(g, got, core, *, name):
    g2, got2 = _rows2d(g, 1), _rows2d(got, 0)
    rows, c = got2.shape
    tm = _row_tile(rows)
    flag = jnp.zeros((1, 1, LANE), F32) + core.astype(F32)

    def fn(i, nt, a0, a1, b, f):
        return jnp.where(f[:, 0:1] == 0.0, a0, a1) + b

    stacked = g2.reshape(1, 2 * rows, c)
    out = _rowwise(fn, [(stacked, c, 0, 0), (stacked, c, 0, rows // tm), (got2[None], c, 0)], pars=[(flag, LANE, 0)],
                   outs=[(c, c, 0, BF16)], tm=tm, t=rows, name=name)[0][0]
    return out.reshape(got.shape)


def _chip_add(own, got, *, name):
    own2, got2 = _rows2d(own, 0), _rows2d(got, 1)
    rows, c = own2.shape
    tm = _row_tile(rows)

    def fn(i, nt, a, b0, b1, b2):
        return ((a.astype(F32) + b0.astype(F32)) + b1.astype(F32)) + b2.astype(F32)

    stacked = got2.reshape(1, 3 * rows, c)
    out = _rowwise(fn, [(own2[None], c, 0)] + [(stacked, c, 0, k * (rows // tm)) for k in range(3)],
                   outs=[(c, c, 0, F32)], tm=tm, t=rows, name=name)[0][0]
    return out.reshape(own.shape)


def _perm(a, d):
    if d == 1:
        return a
    t = a.shape[0]
    return jnp.swapaxes(a.reshape((t // d, d) + a.shape[1:]), 0, 1).reshape(a.shape)


def _unperm(a, d):
    if d == 1:
        return a
    t = a.shape[0]
    return jnp.swapaxes(a.reshape((d, t // d) + a.shape[1:]), 0, 1).reshape(a.shape)


def _pad_lanes(a, w=HP):
    return jnp.pad(a, [(0, 0)] * (a.ndim - 1) + [(0, w - a.shape[-1])])


def _heads_blocks(a, blk):
    t, h, _ = a.shape
    return jnp.transpose(_pad_lanes(a), (1, 0, 2)).astype(BF16).reshape(h, t // blk, blk, HP)


def _heads_blocks_t(a, blk):
    return jnp.swapaxes(_heads_blocks(a, blk), -1, -2)


def _from_blocks_t(a):
    h, n, d, blk = a.shape
    return jnp.transpose(a, (1, 3, 0, 2)).reshape(n * blk, h, d)


def _from_blocks(a):
    h, n, blk, d = a.shape
    return jnp.transpose(a, (1, 2, 0, 3)).reshape(n * blk, h, d)


def _to_hdt(a):
    return jnp.transpose(a, (1, 2, 0))


def _rope_tables(t):
    pos = jnp.arange(t, dtype=F32)
    inv_freq = ROPE_BASE ** (-jnp.arange(0, C_ROPE, 2, dtype=F32) / C_ROPE)
    ang = pos[:, None] * inv_freq[None, :]
    cos, sin = jnp.cos(ang), jnp.sin(ang)
    half = C_ROPE // 2
    wide = lambda a: jnp.concatenate([jnp.tile(a, (1, N_HEADS)), a, jnp.zeros((t, 2 * LANE - (N_HEADS + 1) * half), F32)], axis=1)
    return wide(cos), wide(sin)


def _rope_pack(q_part, k_part):
    t = k_part.shape[0]
    return jnp.concatenate([q_part.reshape(t, -1), k_part, jnp.zeros((t, 2 * LANE - 9 * (C_ROPE // 2)), F32)], axis=1)


def _band_calls(t, proj, projs_g, sinks):
    none = jnp.full((N_HEADS,), NEG, F32)
    a0 = M_A0 // LANE
    calls = [(proj, (a0, a0 + 4, a0 + 8), t // BLK, BLK, False, none)]
    calls += [(pg, (0, 4, 8), t // (d * BLK), BLK, False, none) for pg, d in zip(projs_g, A_DILS[1:])]
    calls.append((proj, (M_BQ // LANE, M_BK // LANE, M_BV // LANE), t // BLK, BLK - 1, True, sinks.astype(F32)))
    return calls


def _layer_fwd(l, x, xb, w, p, biases, rope_cs):
    t = x.shape[0]
    n = f"l{l}_"
    xps = [_perm(xb, d) for d in A_DILS[1:]]
    proj = _mm(xb, w["wm_t"], tb=True, out_dtype=BF16, name=n + "proj")
    projs_g = [_mm(xp, wg, tb=True, out_dtype=BF16, tn=768, name=n + f"proj_g{i + 1}") for i, (xp, wg) in enumerate(zip(xps, w["wg_t"]))]
    s = {"xb": xb, "xps": xps, "proj": proj, "projs_g": projs_g}

    calls = _band_calls(t, proj, projs_g, p["sinks"])
    outs = [_band_fwd(src, offs, biases[i], sk, nb=nb, lim=lim, gqa=gqa, name=n + f"band{i}")
            for i, (src, offs, nb, lim, gqa, sk) in enumerate(calls)]
    os_ = [_unperm(outs[gi][0], d) for gi, d in enumerate(A_DILS)]
    lses = [_unperm(outs[gi][1], d) for gi, d in enumerate(A_DILS)]
    o_a = _combine_fwd(os_, lses, name=n + "combine_fwd")
    o_b_f, lse_b = outs[3]
    o_b = o_b_f.astype(BF16)
    s.update(os=os_, lses=lses, lses_p=[outs[gi][1] for gi in range(3)], o_b=o_b_f, lse_b=lse_b)

    rq = _rms_fwd(proj, C_Q_RANK, M_CQ // C_Q_RANK, p["q_norm_g"], name=n + "rms_q")
    rkv = _rms_fwd(proj, C_KV_RANK, M_CDKV // C_KV_RANK, p["kv_norm_g"], name=n + "rms_kv")
    q_c = _mm(rq, w["w_uq"], name=n + "uq").reshape(t, N_HEADS, C_NOPE + C_ROPE)
    kv_c = _mm(rkv, w["w_ukv"], name=n + "ukv").reshape(t, N_HEADS, 2 * C_NOPE)
    k_rope = proj[:, M_CDKV + C_KV_RANK:M_CDKV + C_KV_RANK + C_ROPE].astype(F32)
    hr = C_ROPE // 2
    x1 = _rope_pack(q_c[:, :, C_NOPE:C_NOPE + hr], k_rope[:, :hr])
    x2 = _rope_pack(q_c[:, :, C_NOPE + hr:], k_rope[:, hr:])
    y1, y2 = _rope(x1, x2, rope_cs[0], rope_cs[1], name=n + "rope")
    qy1, qy2 = y1[:, :LANE].reshape(t, N_HEADS, hr), y2[:, :LANE].reshape(t, N_HEADS, hr)
    ky = jnp.concatenate([y1[:, LANE:LANE + hr], y2[:, LANE:LANE + hr]], axis=1)
    q_full = jnp.concatenate([q_c[:, :, :C_NOPE], qy1, qy2], axis=2)
    k_full = jnp.concatenate([kv_c[:, :, :C_NOPE], jnp.broadcast_to(ky[:, None, :], (t, N_HEADS, C_ROPE))], axis=2)
    qm, km, vm = _heads_blocks(q_full, TQ), _heads_blocks(k_full, TQ), _heads_blocks(kv_c[:, :, C_NOPE:], TQ)
    ot_c, lse_c = _mla_fwd(qm, km, jnp.swapaxes(vm, -1, -2), name=n + "mla_fwd")
    o_c3 = _from_blocks_t(ot_c)[:, :, :HEAD_DIM]
    o_c = o_c3.reshape(t, 512).astype(BF16)
    s.update(rq=rq, rkv=rkv, qm=qm, km=km, vm=vm, lse_c=lse_c, ot_c=_to_hdt(o_c3))

    obs = [o_a, o_b, o_c]
    ys = [_mm(o, w["w_branch"][i], name=n + f"branch{i}") for i, o in enumerate(obs)]
    merged = _merge_fwd(proj, p["b_gate"], ys, name=n + "merge")
    mix = _mm(merged, w["w_out"], name=n + "out")
    x1f, x1b, z1 = _ln_fwd(x, mix, p["ln1_g"], p["ln1_b"], name=n + "ln1")
    s.update(obs=obs, ys=ys, merged=merged, z1=z1, x1b=x1b)

    ug = _mm(x1b, w["wup_g"], tn=1408, name=n + "up_g")
    uv = _mm(x1b, w["wup_v"], tn=1408, name=n + "up_v")
    h = _glu_fwd(ug, uv, w["conv_w"], p["conv_b"], name=n + "glu")
    ff = _mm(h, w["w_ffn_down"], tk=1408, name=n + "down")
    x2f, x2b, z2 = _ln_fwd(x1f, ff, p["ln2_g"], p["ln2_b"], name=n + "ln2")
    s.update(ug=ug, uv=uv, h=h, z2=z2)
    return x2f, x2b, s


def _layer_bwd(l, s, dys, coefs, w, p, biases, rope_cs):
    n = f"l{l}b_"
    t = s["z2"].shape[0]
    gw, gs = {}, {}
    tr = lambda a: a.T

    dz2, dz2b, gs["ln2_g"], gs["ln2_b"] = _ln_bwd(s["z2"], p["ln2_g"], dys, coefs, name=n + "ln2")
    dh = _mm(dz2b, w["w_ffn_down"], tb=True, tn=1408, name=n + "d_h")
    gw["w_ffn_down"] = _mm(tr(s["h"]), dz2b, tm=704, name=n + "g_down")
    dcg, dcv, gw["conv_w"], gs["conv_b"] = _glu_bwd_a(s["ug"], s["uv"], w["conv_w"], p["conv_b"], dh, name=n + "glu_a")
    dug = _glu_bwd_b(dcg, w["conv_w"], 0, name=n + "glu_bg")
    duv = _glu_bwd_b(dcv, w["conv_w"], 1, name=n + "glu_bv")
    dx1_g = _mm(dug, w["wup_g"], tb=True, tk=1408, name=n + "d_x1g")
    dx1_v = _mm(duv, w["wup_v"], tb=True, tk=1408, name=n + "d_x1v")
    x1t = tr(s["x1b"])
    gw["w_ffn_up"] = jnp.concatenate([_mm(x1t, dug, tn=1408, name=n + "g_upg"), _mm(x1t, duv, tn=1408, name=n + "g_upv")], axis=1)

    dz1, dz1b, gs["ln1_g"], gs["ln1_b"] = _ln_bwd(s["z1"], p["ln1_g"], [dz2, dx1_g, dx1_v], [ALPHA, 1.0, 1.0], name=n + "ln1")
    dmerged = _mm(dz1b, w["w_out"], tb=True, name=n + "d_merged")
    gw["w_out"] = _mm(tr(s["merged"]), dz1b, name=n + "g_out")
    dys_b, dgp, gs["b_gate"] = _merge_bwd(s["proj"], p["b_gate"], s["ys"], dmerged, name=n + "merge")
    dos = [_mm(dy, w["w_branch"][i], tb=True, name=n + f"d_o{i}") for i, dy in enumerate(dys_b)]
    gw["w_branch"] = jnp.stack([_mm(tr(o), dy, name=n + f"g_branch{i}") for i, (o, dy) in enumerate(zip(s["obs"], dys_b))])

    do_gs, dpr_gs = _combine_bwd(s["os"], s["lses"], dos[0], name=n + "combine")
    do_b, dpr_b = _delta(dos[1], s["o_b"], name=n + "delta_b")
    do_list = [_perm(a, d) for a, d in zip(do_gs, A_DILS)] + [do_b]
    dpr_list = [_perm(a, d) for a, d in zip(dpr_gs, A_DILS)] + [dpr_b]
    lse_list = s["lses_p"] + [s["lse_b"]]
    calls = _band_calls(t, s["proj"], s["projs_g"], p["sinks"])
    band = [_band_bwd(src, offs, do_list[i], lse_list[i], dpr_list[i], biases[i], sk, nb=nb, lim=lim, gqa=gqa, name=n + f"band{i}")
            for i, (src, offs, nb, lim, gqa, sk) in enumerate(calls)]
    gs["sinks"] = band[3][4][:, 0, 0]
    ds_sum = jnp.concatenate([b_[3] for b_ in band], axis=0)

    do_c3 = dos[2].reshape(t, N_HEADS, HEAD_DIM)
    delta_c = _rowdot(_to_hdt(do_c3), s["ot_c"], name=n + "delta_c").reshape(N_HEADS, t // TQ, 1, TQ)
    dqt_c, dk_c, dv_c = _mla_bwd(s["qm"], s["km"], jnp.swapaxes(s["km"], -1, -2), s["vm"], _heads_blocks(do_c3, TQ),
                                 s["lse_c"], delta_c, name=n + "mla")
    dq_full, dk_full, dv_full = _from_blocks_t(dqt_c), _from_blocks(dk_c), _from_blocks(dv_c)
    hr = C_ROPE // 2
    dk_sum = _sum_rows([dk_full[None, :, hh, :] for hh in range(N_HEADS)], tm=1024, name=n + "krope_sum")[0]
    dy1 = _rope_pack(dq_full[:, :, C_NOPE:C_NOPE + hr], dk_sum[:, C_NOPE:C_NOPE + hr])
    dy2 = _rope_pack(dq_full[:, :, C_NOPE + hr:C_NOPE + C_ROPE], dk_sum[:, C_NOPE + hr:C_NOPE + C_ROPE])
    dx1, dx2 = _rope(dy1, dy2, rope_cs[0], -rope_cs[1], name=n + "rope")
    dq_c = jnp.concatenate([dq_full[:, :, :C_NOPE], dx1[:, :LANE].reshape(t, N_HEADS, hr), dx2[:, :LANE].reshape(t, N_HEADS, hr)],
                           axis=2).reshape(t, 768).astype(BF16)
    dkv_c = jnp.concatenate([dk_full[:, :, :C_NOPE], dv_full[:, :, :C_NOPE]], axis=2).reshape(t, 1024).astype(BF16)
    d_rq = _mm(dq_c, w["w_uq"], tb=True, name=n + "d_rq")
    d_rkv = _mm(dkv_c, w["w_ukv"], tb=True, name=n + "d_rkv")
    gw["w_uq"] = _mm(tr(s["rq"]), dq_c, name=n + "g_uq")
    gw["w_ukv"] = _mm(tr(s["rkv"]), dkv_c, name=n + "g_ukv")
    dcq, gs["q_norm_g"] = _rms_bwd(s["proj"], C_Q_RANK, M_CQ // C_Q_RANK, p["q_norm_g"], d_rq, name=n + "rms_q")
    dckv, gs["kv_norm_g"] = _rms_bwd(s["proj"], C_KV_RANK, M_CDKV // C_KV_RANK, p["kv_norm_g"], d_rkv, name=n + "rms_kv")
    dcdkv = jnp.concatenate([dckv, dx1[:, LANE:LANE + hr].astype(BF16), dx2[:, LANE:LANE + hr].astype(BF16),
                             jnp.zeros((t, M_COLS - M_CDKV - C_KV_RANK - C_ROPE), BF16)], axis=1)

    dproj = jnp.concatenate(dgp + list(band[0][:3]) + list(band[3][:3]) + [dcq, dcdkv], axis=1)
    dprojs_g = [jnp.concatenate(band[gi][:3], axis=1) for gi in (1, 2)]
    dx_terms = [_mm(dproj, w["wm_t"], tk=1024, name=n + "d_x")]
    dx_terms += [_unperm(_mm(dp, wg, tk=768, name=n + f"d_x_g{i + 1}"), d) for i, (dp, wg, d) in enumerate(zip(dprojs_g, w["wg_t"], A_DILS[1:]))]
    g_main = _mm(tr(s["xb"]), dproj, name=n + "g_in").T
    g_groups = [_mm(tr(xp), dp, tn=768, name=n + f"g_in_g{i + 1}").T for i, (xp, dp) in enumerate(zip(s["xps"], dprojs_g))]
    fold = lambda a, tag: _sum_rows([a.reshape(2, 2, HEAD_DIM, D_MODEL)[:, j] for j in range(2)], tm=HEAD_DIM,
                                    name=n + "g_fold_" + tag).reshape(2 * HEAD_DIM, D_MODEL)
    gw["w_in"] = jnp.concatenate([g_main[M_A0:M_BQ], g_groups[0], g_groups[1], g_main[M_BQ:M_BK], fold(g_main[M_BK:M_BV], "k"),
                                  fold(g_main[M_BV:M_CQ], "v"), g_main[M_CQ:M_CDKV + C_KV_RANK + C_ROPE], g_main[M_GATE:M_A0]], axis=0)
    return [dz1] + dx_terms, [ALPHA, 1.0, 1.0, 1.0], gw, gs, ds_sum


def _local_step(x, target, ws, rel_table, small):
    t = x.shape[0]
    ps = [{k: small[k][l] for k in SMALL} for l in range(DEPTH)]
    bucket = _bucket_index()
    bias_all = _bias_lookup(bucket, rel_table.T, name="bias_lookup").reshape(4, N_HEADS, BLK, 2 * BLK)
    biases = [bias_all[i] for i in range(4)]
    rope_cs = _rope_tables(t)

    saved, h, hb = [], x, x.astype(BF16)
    for l in range(DEPTH):
        h, hb, s = _layer_fwd(l, h, hb, ws[l], ps[l], biases, rope_cs)
        saved.append(s)
    dy, loss_part = _loss_and_grad(h, target, name="loss")

    dys, coefs = [dy], [1.0]
    gws, gss, dss = [None] * DEPTH, [None] * DEPTH, [None] * DEPTH
    for l in reversed(range(DEPTH)):
        dys, coefs, gws[l], gss[l], dss[l] = _layer_bwd(l, saved[l], dys, coefs, ws[l], ps[l], biases, rope_cs)
    grad_x = _lincomb(dys, coefs, name="grad_x")
    npos = 2 * BLK * BLK
    g_rel = _bias_grad(bucket, dss[0].reshape(4 * N_HEADS, npos), dss[1].reshape(4 * N_HEADS, npos), name="bias_grad").T
    gsmall = {k: jnp.stack([gss[l][k] for l in range(DEPTH)]) for k in SMALL}
    return loss_part, grad_x, gws, gsmall, g_rel


def kernel(x, rel_table, w_in, b_gate, sinks, q_norm_g, kv_norm_g, w_uq, w_ukv, w_branch, w_out, ln1_g, ln1_b, w_ffn_up, conv_w, conv_b, w_ffn_down, ln2_g, ln2_b, loss_target, m_rel_table, m_w_in, m_b_gate, m_sinks, m_q_norm_g, m_kv_norm_g, m_w_uq, m_w_ukv, m_w_branch, m_w_out, m_ln1_g, m_ln1_b, m_w_ffn_up, m_conv_w, m_conv_b, m_w_ffn_down, m_ln2_g, m_ln2_b, v_rel_table, v_w_in, v_b_gate, v_sinks, v_q_norm_g, v_kv_norm_g, v_w_uq, v_w_ukv, v_w_branch, v_w_out, v_ln1_g, v_ln1_b, v_w_ffn_up, v_conv_w, v_conv_b, v_w_ffn_down, v_ln2_g, v_ln2_b):
    wts = dict(rel_table=rel_table, w_in=w_in, b_gate=b_gate, sinks=sinks, q_norm_g=q_norm_g, kv_norm_g=kv_norm_g, w_uq=w_uq,
               w_ukv=w_ukv, w_branch=w_branch, w_out=w_out, ln1_g=ln1_g, ln1_b=ln1_b, w_ffn_up=w_ffn_up, conv_w=conv_w,
               conv_b=conv_b, w_ffn_down=w_ffn_down, ln2_g=ln2_g, ln2_b=ln2_b)
    ms = dict(rel_table=m_rel_table, w_in=m_w_in, b_gate=m_b_gate, sinks=m_sinks, q_norm_g=m_q_norm_g, kv_norm_g=m_kv_norm_g,
              w_uq=m_w_uq, w_ukv=m_w_ukv, w_branch=m_w_branch, w_out=m_w_out, ln1_g=m_ln1_g, ln1_b=m_ln1_b, w_ffn_up=m_w_ffn_up,
              conv_w=m_conv_w, conv_b=m_conv_b, w_ffn_down=m_w_ffn_down, ln2_g=m_ln2_g, ln2_b=m_ln2_b)
    vs = dict(rel_table=v_rel_table, w_in=v_w_in, b_gate=v_b_gate, sinks=v_sinks, q_norm_g=v_q_norm_g, kv_norm_g=v_kv_norm_g,
              w_uq=v_w_uq, w_ukv=v_w_ukv, w_branch=v_w_branch, w_out=v_w_out, ln1_g=v_ln1_g, ln1_b=v_ln1_b, w_ffn_up=v_w_ffn_up,
              conv_w=v_conv_w, conv_b=v_conv_b, w_ffn_down=v_w_ffn_down, ln2_g=v_ln2_g, ln2_b=v_ln2_b)

    core = lax.axis_index("c")
    chip = 2 * lax.axis_index("x") + lax.axis_index("y")

    names = [name for name, _, _ in MATS]
    sent = [_weight_send(name, wts[name]) for name in names]
    got = _allgather_weights(sent)
    gathered = {name: lax.dynamic_update_slice(g, s[None], (chip,) + (0,) * s.ndim) for name, g, s in zip(names, got, sent)}
    ws = [_full_weights(gathered, l) for l in range(DEPTH)]

    small = {k: wts[k] for k in SMALL}
    loss_part, grad_x, gws, gsmall, g_rel = _local_step(x[0], loss_target[0], ws, rel_table, small)

    rnames = [name for name, _, _ in REDUCED]
    gsend = [jnp.stack([_grad_send(name, gws[l][name], shape, ax) for l in range(DEPTH)]) for name, shape, ax in REDUCED]
    theirs = _sibling_swap(gsend)
    pairs = [_pair_add(g, t_, core, name="grad_pair_" + name) for name, g, t_ in zip(rnames, gsend, theirs)]
    arrived = _chip_scatter(pairs)
    reduced = [_chip_add(lax.dynamic_index_in_dim(p, chip, 0, keepdims=False), a, name="grad_chip_" + name)
               for name, p, a in zip(rnames, pairs, arrived)]
    others = _sibling_share(reduced)
    gshard = {}
    for name, mine, other in zip(rnames, reduced, others):
        layers = [jnp.where(core == l, mine, other) for l in range(DEPTH)]
        gshard[name] = jnp.stack([_grad_recv(name, a) for a in layers])

    conv_w_full = jnp.stack([gws[l]["conv_w"] for l in range(DEPTH)])
    small_red = _allreduce_small(_pack_small(g_rel, gsmall, conv_w_full, loss_part))
    g_rel_r, gsmall_r, conv_w_r, loss_vec = _unpack_small(small_red)
    loss = loss_vec[0]
    shard_w = 2 * D_FF // N_CHIP
    gshard["conv_w"] = lax.dynamic_slice_in_dim(conv_w_r, chip * shard_w, shard_w, axis=2)

    grads = dict(gshard)
    grads.update(gsmall_r)
    grads["rel_table"] = g_rel_r
    deltas, new_m, new_v = {}, {}, {}
    for name, _, _ in MATS:
        shp = wts[name].shape
        v2 = lambda a: a.reshape(-1, shp[-1])
        d_, m_, v_ = _adamw(v2(wts[name]), v2(grads[name]), v2(ms[name]), v2(vs[name]), name="adamw_" + name)
        deltas[name], new_m[name], new_v[name] = d_.reshape(shp), m_.reshape(shp), v_.reshape(shp)
    zero, none = jnp.zeros((LANE,), F32), jnp.zeros((0,), F32)
    sw = _pack_small(wts["rel_table"], {k: wts[k] for k in SMALL}, none, zero)
    sm = _pack_small(ms["rel_table"], {k: ms[k] for k in SMALL}, none, zero)
    sv = _pack_small(vs["rel_table"], {k: vs[k] for k in SMALL}, none, zero)
    sg = _pack_small(g_rel_r, gsmall_r, none, zero)
    sd, smn, svn = _adamw(sw, sg, sm, sv, name="adamw_small")
    for res, buf in ((deltas, sd), (new_m, smn), (new_v, svn)):
        rel_, sm_ = _unpack_small(jnp.pad(buf, ((0, small_red.shape[0] - buf.shape[0]), (0, 0))))[:2]
        res["rel_table"] = rel_
        res.update(sm_)

    return (loss, grad_x[None], *[grads[k] for k in WEIGHT_ORDER], *[deltas[k] for k in WEIGHT_ORDER],
            *[new_m[k] for k in WEIGHT_ORDER], *[new_v[k] for k in WEIGHT_ORDER])
```

```python
import math

import jax
import jax.numpy as jnp
from jax import lax
from jax.experimental import pallas as pl
from jax.experimental.pallas import tpu as pltpu

F32 = jnp.float32
BF16 = jnp.bfloat16
MESH = pl.DeviceIdType.MESH

D_MODEL = 1024
DEPTH = 2
HEAD_DIM = 64
N_HEADS = 8
A_DILS = (1, 4, 16)
C_Q_RANK = 256
C_KV_RANK = 128
C_NOPE = 64
C_ROPE = 32
ROPE_BASE = 10000.0
REL_BUCKETS = 32
REL_MAX_DIST = 2048
D_FF = 2816
ALPHA = (2 * DEPTH) ** 0.25
LN_EPS = 1e-5
RMS_EPS = 1e-6
NEG = -1e30
ADAM_LR, ADAM_B1, ADAM_B2, ADAM_EPS, ADAM_WD, ADAM_STEP = 0.001, 0.9, 0.999, 1e-08, 0.01, 10

VMEM_LIMIT_BYTES = 56 * 1024 * 1024
LANE = 128
BLK = 128
TQ = 512
HP = 128
BAND_UNROLL = 16

D_IN = 8864
A_COLS = 3 * N_HEADS * HEAD_DIM
ORIG = {"a": 0, "bq": 4608, "bk": 5120, "bv": 5248, "cq": 5376, "cdkv": 5632, "gate": 5792}
M_GATE, M_A0, M_BQ, M_BK, M_BV, M_CQ, M_CDKV, M_COLS = 0, 3072, 4608, 5120, 5376, 5632, 5888, 6144

N_CHIP = 4
MATS = (
    ("w_in", (D_MODEL, D_IN), 1),
    ("w_uq", (C_Q_RANK, 768), 1),
    ("w_ukv", (C_KV_RANK, 1024), 1),
    ("w_branch", (3, 512, D_MODEL), 2),
    ("w_out", (D_MODEL, D_MODEL), 0),
    ("w_ffn_up", (D_MODEL, 2 * D_FF), 1),
    ("conv_w", (3, 2 * D_FF), 1),
    ("w_ffn_down", (D_FF, D_MODEL), 0),
)
SMALL = ("b_gate", "sinks", "q_norm_g", "kv_norm_g", "ln1_g", "ln1_b", "conv_b", "ln2_g", "ln2_b")
SMALL_SIZES = {"b_gate": 3072, "sinks": 8, "q_norm_g": 256, "kv_norm_g": 128, "ln1_g": 1024, "ln1_b": 1024,
               "conv_b": 5632, "ln2_g": 1024, "ln2_b": 1024}
WEIGHT_ORDER = ("rel_table", "w_in", "b_gate", "sinks", "q_norm_g", "kv_norm_g", "w_uq", "w_ukv", "w_branch",
                "w_out", "ln1_g", "ln1_b", "w_ffn_up", "conv_w", "conv_b", "w_ffn_down", "ln2_g", "ln2_b")


def _cparams(sem):
    return pltpu.CompilerParams(dimension_semantics=sem, vmem_limit_bytes=VMEM_LIMIT_BYTES)


def _shard_shape(shape, ax):
    s = list(shape)
    s[ax] //= N_CHIP
    return tuple(s)


def _ceil_to(n, m):
    return -(-n // m) * m


def _pick(n, target):
    if n <= target:
        return n
    best = None
    for t in range(LANE, target + 1, LANE):
        if n % t == 0:
            best = t
    assert best is not None, (n, target)
    return best


def _mm(a, b, *, tb=False, out_dtype=F32, tm=512, tn=1024, tk=2048, name):
    m, k = a.shape
    n = b.shape[0] if tb else b.shape[1]
    assert (b.shape[1] if tb else b.shape[0]) == k
    tm, tn, tk = _pick(m, tm), _pick(n, tn), _pick(k, tk)
    nk = k // tk
    dn = (((1,), (1,)), ((), ())) if tb else (((1,), (0,)), ((), ()))

    def body(a_ref, b_ref, o_ref, acc_ref):
        part = lax.dot_general(a_ref[...].astype(BF16), b_ref[...].astype(BF16), dn, preferred_element_type=F32)
        if nk == 1:
            o_ref[...] = part.astype(o_ref.dtype)
        else:
            kk = pl.program_id(2)

            @pl.when(kk == 0)
            def _():
                acc_ref[...] = part

            @pl.when(kk > 0)
            def _():
                acc_ref[...] += part

            @pl.when(kk == nk - 1)
            def _():
                o_ref[...] = acc_ref[...].astype(o_ref.dtype)

    b_spec = pl.BlockSpec((tn, tk), lambda i, j, kk: (j, kk)) if tb else pl.BlockSpec((tk, tn), lambda i, j, kk: (kk, j))
    return pl.pallas_call(
        body, name=name, grid=(m // tm, n // tn, nk),
        in_specs=[pl.BlockSpec((tm, tk), lambda i, j, kk: (i, kk)), b_spec],
        out_specs=pl.BlockSpec((tm, tn), lambda i, j, kk: (i, j)),
        out_shape=jax.ShapeDtypeStruct((m, n), out_dtype),
        scratch_shapes=[pltpu.VMEM((tm, tn) if nk > 1 else (8, LANE), F32)],
        compiler_params=_cparams(("parallel", "parallel", "arbitrary")),
    )(a, b)


def _rowwise(fn, rows, *, pars=(), halos=(), outs=(), accs=(), tm, name, ncol=1, t=None):
    nb = rows[0][0].shape[0]
    t = rows[0][0].shape[1] if t is None else t
    tm = min(tm, t)
    assert t % tm == 0 and tm % 8 == 0
    nt = t // tm
    in_specs, args = [], []
    for spec in rows:
        arr, c, off = spec[:3]
        rb = spec[3] if len(spec) > 3 else 0
        in_specs.append(pl.BlockSpec((1, tm, c), lambda b, cc, i, off=off, rb=rb: (b, i + rb, off + cc)))
        args.append(arr)
    for arr, c, off, kind in halos:
        if kind == "prev":
            im = lambda b, cc, i, off=off: (b, jnp.maximum(i * (tm // 8) - 1, 0), off + cc)
        else:
            im = lambda b, cc, i, off=off: (b, jnp.minimum((i + 1) * (tm // 8), t // 8 - 1), off + cc)
        in_specs.append(pl.BlockSpec((1, 8, c), im))
        args.append(arr)
    for arr, c, off in pars:
        bp, r = arr.shape[:2]
        if bp > 1:
            im = lambda b, cc, i, off=off: (b, 0, off + cc)
        else:
            im = lambda b, cc, i, off=off: (0, 0, off + cc)
        in_specs.append(pl.BlockSpec((1, r, c), im))
        args.append(arr)
    out_specs, out_shapes = [], []
    for ctot, c, off, dt in outs:
        out_specs.append(pl.BlockSpec((1, tm, c), lambda b, cc, i, off=off: (b, i, off + cc)))
        out_shapes.append(jax.ShapeDtypeStruct((nb, t, ctot), dt))
    for r, ctot, c, off in accs:
        out_specs.append(pl.BlockSpec((1, r, c), lambda b, cc, i, off=off: (b, 0, off + cc)))
        out_shapes.append(jax.ShapeDtypeStruct((nb, r, ctot), F32))
    n_in, n_out = len(args), len(outs)

    def body(*refs):
        i = pl.program_id(2)
        res = fn(i, nt, *[r[0].astype(F32) for r in refs[:n_in]])
        if not isinstance(res, (tuple, list)):
            res = (res,)
        for o_ref, val in zip(refs[n_in:n_in + n_out], res[:n_out]):
            o_ref[0] = val.astype(o_ref.dtype)
        for a_ref, val in zip(refs[n_in + n_out:], res[n_out:]):
            @pl.when(i == 0)
            def _(a_ref=a_ref, val=val):
                a_ref[0] = val

            @pl.when(i > 0)
            def _(a_ref=a_ref, val=val):
                a_ref[0] += val

    res = pl.pallas_call(
        body, name=name, grid=(nb, ncol, nt), in_specs=in_specs, out_specs=out_specs, out_shape=out_shapes,
        compiler_params=_cparams(("parallel", "parallel", "arbitrary")),
    )(*args)
    return res


def _lanewise(fn, ins, outs, *, tl, name):
    nb, _, t = ins[0].shape
    tl = min(tl, t)
    assert t % tl == 0
    n_in = len(ins)

    def body(*refs):
        res = fn(*[r[0] for r in refs[:n_in]])
        if not isinstance(res, (tuple, list)):
            res = (res,)
        for o_ref, val in zip(refs[n_in:], res):
            o_ref[0] = val.astype(o_ref.dtype)

    return pl.pallas_call(
        body, name=name, grid=(nb, t // tl),
        in_specs=[pl.BlockSpec((1, a.shape[1], tl), lambda b, i: (b, 0, i)) for a in ins],
        out_specs=[pl.BlockSpec((1, r, tl), lambda b, i: (b, 0, i)) for r, _ in outs],
        out_shape=[jax.ShapeDtypeStruct((nb, r, t), dt) for r, dt in outs],
        compiler_params=_cparams(("parallel", "parallel")),
    )(*ins)


def _dot(a, b):
    return lax.dot_general(a, b, (((1,), (0,)), ((), ())), preferred_element_type=F32)


def _dot_nt(a, b):
    return lax.dot_general(a, b, (((1,), (1,)), ((), ())), preferred_element_type=F32)


def _dot_tn(a, b):
    return lax.dot_general(a, b, (((0,), (0,)), ((), ())), preferred_element_type=F32)


BAND_ROWS = BAND_UNROLL * BLK


def _rows(parts):
    return jnp.concatenate(parts, axis=0)


def _lane_lo():
    return lax.broadcasted_iota(jnp.int32, (1, LANE), 1) < HEAD_DIM


def _blocks(a):
    return [a[i * BLK:(i + 1) * BLK] for i in range(BAND_UNROLL)]


def _band_operands(g, k_ref, v_ref):
    start = pl.multiple_of(g * BAND_ROWS, BAND_ROWS)
    pstart = pl.multiple_of(jnp.maximum(g * BAND_ROWS - BLK, 0), BLK)
    out = []
    for ref in (k_ref, v_ref):
        cur = ref[pl.ds(start, BAND_ROWS), :]
        raw = ref[pl.ds(pstart, BAND_ROWS), :]
        shifted = _rows([jnp.zeros((BLK, LANE), raw.dtype), raw[:BAND_ROWS - BLK]])
        out += [_blocks(cur), _blocks(jnp.where(g == 0, shifted, raw))]
    return out


def _band_scores(g, qa, kc, kp, b_ref, a, nb, lim):
    scale = HEAD_DIM ** -0.5
    qi = jnp.bitwise_and(lax.broadcasted_iota(jnp.int32, (BAND_ROWS, BLK), 0), BLK - 1)
    ki = lax.broadcasted_iota(jnp.int32, (BAND_ROWS, BLK), 1)
    tile = lambda blk: _rows([blk] * BAND_UNROLL)
    firsts = []
    for i in range(BAND_UNROLL):
        if nb >= BAND_UNROLL:
            val = jnp.where(lax.rem(g * BAND_UNROLL, nb) == 0, NEG, 0.0).astype(F32) if i == 0 else 0.0
        else:
            val = NEG if i % nb == 0 else 0.0
        firsts.append(jnp.zeros((BLK, 1), F32) + val)
    sc = _rows([_dot_nt(q, k) for q, k in zip(qa, kc)]) * scale + tile(b_ref[a, :, BLK:2 * BLK])
    sp = _rows([_dot_nt(q, k) for q, k in zip(qa, kp)]) * scale + tile(b_ref[a, :, 0:BLK])
    return jnp.where(ki <= qi, sc, NEG), jnp.where((BLK + qi - ki) <= lim, sp, NEG) + _rows(firsts)


def _band_fwd(src, offs, bias, sinks, *, nb, lim, gqa, name):
    t = src.shape[0]
    assert t % BAND_ROWS == 0 and (nb % BAND_UNROLL == 0 or BAND_UNROLL % nb == 0)
    qo, ko, vo = offs
    share = 2 if gqa else 1

    def body(sink_ref, q_ref, k_ref, v_ref, b_ref, o_ref, lse_ref):
        hp, g = pl.program_id(0), pl.program_id(1)
        lo = _lane_lo()
        q2 = q_ref[...]
        kc, kp, vc, vp = _band_operands(g, k_ref, v_ref)
        outs, lses = [], []
        for a in range(2):
            sink = sink_ref[2 * hp + a]
            qa = _blocks(jnp.where(lo if a == 0 else jnp.logical_not(lo), q2, jnp.zeros_like(q2)))
            sc, sp = _band_scores(g, qa, kc, kp, b_ref, a, nb, lim)
            m = jnp.maximum(jnp.maximum(jnp.max(sc, axis=1, keepdims=True), jnp.max(sp, axis=1, keepdims=True)), sink)
            pc, pp = jnp.exp(sc - m), jnp.exp(sp - m)
            l = jnp.sum(pc, axis=1, keepdims=True) + jnp.sum(pp, axis=1, keepdims=True) + jnp.exp(sink - m)
            inv = 1.0 / l
            pc_b, pp_b = _blocks((pc * inv).astype(BF16)), _blocks((pp * inv).astype(BF16))
            outs.append(_rows([_dot(pc_b[i], vc[i]) + _dot(pp_b[i], vp[i]) for i in range(BAND_UNROLL)]))
            lses.append(m + jnp.log(l))
        o_ref[...] = jnp.where(lo, outs[0], outs[1])
        lse_ref[...] = jnp.where(lo, lses[0], lses[1])

    slab = lambda off: pl.BlockSpec((BAND_ROWS, LANE), lambda hp, g, off=off: (g, off + hp))
    whole = lambda off: pl.BlockSpec((t, LANE), lambda hp, g, off=off: (0, off + hp // share))
    return pl.pallas_call(
        body, name=name, grid=(N_HEADS // 2, t // BAND_ROWS),
        in_specs=[pl.BlockSpec(memory_space=pltpu.SMEM), slab(qo), whole(ko), whole(vo),
                  pl.BlockSpec((2, BLK, 2 * BLK), lambda hp, g: (hp, 0, 0))],
        out_specs=[slab(0), slab(0)],
        out_shape=[jax.ShapeDtypeStruct((t, N_HEADS * HEAD_DIM), F32)] * 2,
        compiler_params=_cparams(("parallel", "parallel")),
    )(sinks, src, src, src, bias)


def _band_bwd(src, offs, do, lse, dpr, bias, sinks, *, nb, lim, gqa, name):
    t = src.shape[0]
    qo, ko, vo = offs
    share = 2 if gqa else 1
    nstep = t // BAND_ROWS
    scale = HEAD_DIM ** -0.5

    def fold(a):
        acc = a[0:BLK]
        for i in range(1, BAND_UNROLL):
            acc = acc + a[i * BLK:(i + 1) * BLK]
        return acc

    def body(sink_ref, q_ref, k_ref, v_ref, do_ref, lse_ref, dpr_ref, b_ref,
             dq_ref, dk_ref, dv_ref, ds_ref, dsink_ref, dk_acc, dv_acc):
        hp, g = pl.program_id(0), pl.program_id(1)
        lo = _lane_lo()
        hi = jnp.logical_not(lo)

        @pl.when(jnp.logical_and(g == 0, lax.rem(hp, share) == 0))
        def _():
            dk_acc[...] = jnp.zeros_like(dk_acc)
            dv_acc[...] = jnp.zeros_like(dv_acc)

        @pl.when(g == 0)
        def _():
            ds_ref[...] = jnp.zeros_like(ds_ref)
            dsink_ref[...] = jnp.zeros_like(dsink_ref)

        q2, do2, lse2, dpr2 = q_ref[...], do_ref[...], lse_ref[...], dpr_ref[...]
        lse_sw, dpr_sw = pltpu.roll(lse2, HEAD_DIM, axis=1), pltpu.roll(dpr2, HEAD_DIM, axis=1)
        kc, kp, vc, vp = _band_operands(g, k_ref, v_ref)
        dqs, dk_cur, dk_prev, dv_cur, dv_prev = [], None, None, None, None
        for a in range(2):
            sink = sink_ref[2 * hp + a]
            mine = lo if a == 0 else hi
            qa = _blocks(jnp.where(mine, q2, jnp.zeros_like(q2)))
            doa = _blocks(jnp.where(mine, do2, jnp.zeros_like(do2)))
            lse_a, dpr_a = jnp.where(mine, lse2, lse_sw), jnp.where(mine, dpr2, dpr_sw)
            sc, sp = _band_scores(g, qa, kc, kp, b_ref, a, nb, lim)
            pc, pp = jnp.exp(sc - lse_a), jnp.exp(sp - lse_a)
            dsc = pc * (_rows([_dot_nt(d, v) for d, v in zip(doa, vc)]) - dpr_a)
            dsp = pp * (_rows([_dot_nt(d, v) for d, v in zip(doa, vp)]) - dpr_a)
            ds_ref[a, :, BLK:2 * BLK] += fold(dsc)
            ds_ref[a, :, 0:BLK] += fold(dsp)
            dsink_ref[a] -= jnp.sum(jnp.exp(sink - lse_a) * dpr_a, axis=0, keepdims=True)
            dsc_b, dsp_b = _blocks((dsc * scale).astype(BF16)), _blocks((dsp * scale).astype(BF16))
            pc_b, pp_b = _blocks(pc.astype(BF16)), _blocks(pp.astype(BF16))
            dqs.append(_rows([_dot(dsc_b[i], kc[i]) + _dot(dsp_b[i], kp[i]) for i in range(BAND_UNROLL)]))
            parts = [_rows([_dot_tn(x[i], y[i]) for i in range(BAND_UNROLL)])
                     for x, y in ((dsc_b, qa), (dsp_b, qa), (pc_b, doa), (pp_b, doa))]
            if a == 0:
                dk_cur, dk_prev, dv_cur, dv_prev = parts
            else:
                dk_cur, dk_prev, dv_cur, dv_prev = dk_cur + parts[0], dk_prev + parts[1], dv_cur + parts[2], dv_prev + parts[3]
        dq_ref[...] = jnp.where(lo, dqs[0], dqs[1]).astype(dq_ref.dtype)
        start = pl.multiple_of(g * BAND_ROWS, BAND_ROWS)
        after = pl.multiple_of(g * BAND_ROWS + BLK, BLK)
        dk_acc[pl.ds(after, BAND_ROWS), :] += dk_cur
        dk_acc[pl.ds(start, BAND_ROWS), :] += dk_prev
        dv_acc[pl.ds(after, BAND_ROWS), :] += dv_cur
        dv_acc[pl.ds(start, BAND_ROWS), :] += dv_prev

        @pl.when(g == nstep - 1)
        def _():
            dk_ref[...] = dk_acc[BLK:, :].astype(dk_ref.dtype)
            dv_ref[...] = dv_acc[BLK:, :].astype(dv_ref.dtype)

    slab = lambda off: pl.BlockSpec((BAND_ROWS, LANE), lambda hp, g, off=off: (g, off + hp))
    whole = lambda off: pl.BlockSpec((t, LANE), lambda hp, g, off=off: (0, off + hp // share))
    per_pair = lambda shp: pl.BlockSpec((2,) + shp, lambda hp, g: (hp,) + (0,) * len(shp))
    kv_cols = N_HEADS * HEAD_DIM // share
    return pl.pallas_call(
        body, name=name, grid=(N_HEADS // 2, nstep),
        in_specs=[pl.BlockSpec(memory_space=pltpu.SMEM), slab(qo), whole(ko), whole(vo), slab(0), slab(0), slab(0),
                  per_pair((BLK, 2 * BLK))],
        out_specs=[slab(0), whole(0), whole(0), per_pair((BLK, 2 * BLK)), per_pair((1, LANE))],
        out_shape=[jax.ShapeDtypeStruct((t, N_HEADS * HEAD_DIM), BF16), jax.ShapeDtypeStruct((t, kv_cols), BF16),
                   jax.ShapeDtypeStruct((t, kv_cols), BF16), jax.ShapeDtypeStruct((N_HEADS, BLK, 2 * BLK), F32),
                   jax.ShapeDtypeStruct((N_HEADS, 1, LANE), F32)],
        scratch_shapes=[pltpu.VMEM((t + BLK, LANE), F32), pltpu.VMEM((t + BLK, LANE), F32)],
        compiler_params=_cparams(("arbitrary", "arbitrary")),
    )(sinks, src, src, src, do, lse, dpr, bias)


MLA_V_OFF = N_HEADS


def _diag_mask():
    return lax.broadcasted_iota(jnp.int32, (TQ, TQ), 1) <= lax.broadcasted_iota(jnp.int32, (TQ, TQ), 0)


def _mla_specs(t):
    blk = lambda f: pl.BlockSpec((TQ, LANE), lambda hp, qi, f=f: (qi, f(hp)))
    whole = lambda f: pl.BlockSpec((t, LANE), lambda hp, qi, f=f: (0, f(hp)))
    return blk, whole


def _mla_fwd(q, k, kv, *, name):
    t = q.shape[0]
    n = t // TQ
    scale = (C_NOPE + C_ROPE) ** -0.5

    def body(q0_ref, q1_ref, k0_ref, k1_ref, v_ref, o_ref, lse_ref, m_ref, l_ref, acc_ref):
        qi = pl.program_id(1)
        lo = _lane_lo()
        qs, ks = (q0_ref[...], q1_ref[...]), (k0_ref, k1_ref)
        m_ref[...] = jnp.full_like(m_ref, NEG)
        l_ref[...] = jnp.zeros_like(l_ref)
        acc_ref[...] = jnp.zeros_like(acc_ref)

        def step(kj, diagonal):
            rows = pl.ds(pl.multiple_of(kj * TQ, TQ), TQ)
            vb = v_ref[rows, :]
            for a in range(2):
                s = _dot_nt(qs[a], ks[a][rows, :]) * scale
                if diagonal:
                    s = jnp.where(_diag_mask(), s, NEG)
                m_prev = m_ref[a]
                m_new = jnp.maximum(m_prev, jnp.max(s, axis=1, keepdims=True))
                alpha = jnp.exp(m_prev - m_new)
                p = jnp.exp(s - m_new)
                l_ref[a] = alpha * l_ref[a] + jnp.sum(p, axis=1, keepdims=True)
                acc_ref[a] = alpha * acc_ref[a] + _dot(p.astype(BF16), vb)
                m_ref[a] = m_new

        def kloop(kj, c2):
            step(kj, False)
            return c2

        lax.fori_loop(0, qi, kloop, 0)
        step(qi, True)
        o_ref[...] = jnp.where(lo, acc_ref[0] * (1.0 / l_ref[0]), acc_ref[1] * (1.0 / l_ref[1]))
        lse_ref[...] = jnp.where(lo, m_ref[0] + jnp.log(l_ref[0]), m_ref[1] + jnp.log(l_ref[1]))

    blk, whole = _mla_specs(t)
    return pl.pallas_call(
        body, name=name, grid=(N_HEADS // 2, n),
        in_specs=[blk(lambda hp: 2 * hp), blk(lambda hp: 2 * hp + 1), whole(lambda hp: 2 * hp), whole(lambda hp: 2 * hp + 1),
                  whole(lambda hp: MLA_V_OFF + hp)],
        out_specs=[blk(lambda hp: hp), blk(lambda hp: hp)],
        out_shape=[jax.ShapeDtypeStruct((t, N_HEADS * HEAD_DIM), F32)] * 2,
        scratch_shapes=[pltpu.VMEM((2, TQ, 1), F32), pltpu.VMEM((2, TQ, 1), F32), pltpu.VMEM((2, TQ, LANE), F32)],
        compiler_params=_cparams(("parallel", "parallel")),
    )(q, q, k, k, kv)


def _mla_bwd(q, k, kv, do, lse, delta, *, name):
    t = q.shape[0]
    n = t // TQ
    scale = (C_NOPE + C_ROPE) ** -0.5

    def body(q0_ref, q1_ref, k0_ref, k1_ref, v_ref, do_ref, lse_ref, dl_ref,
             dq_ref, dk_ref, dv_ref, dq_acc, dk_acc, dv_acc):
        qi = pl.program_id(1)
        lo = _lane_lo()

        @pl.when(qi == 0)
        def _():
            dk_acc[...] = jnp.zeros_like(dk_acc)
            dv_acc[...] = jnp.zeros_like(dv_acc)

        dq_acc[...] = jnp.zeros_like(dq_acc)
        qs, ks = (q0_ref[...], q1_ref[...]), (k0_ref, k1_ref)
        do2, lse2, dl2 = do_ref[...], lse_ref[...], dl_ref[...]
        lse_sw, dl_sw = pltpu.roll(lse2, HEAD_DIM, axis=1), pltpu.roll(dl2, HEAD_DIM, axis=1)
        heads = []
        for a in range(2):
            mine = lo if a == 0 else jnp.logical_not(lo)
            heads.append((jnp.where(mine, do2, jnp.zeros_like(do2)), jnp.where(mine, lse2, lse_sw)[:, 0:1],
                          jnp.where(mine, dl2, dl_sw)[:, 0:1]))

        def step(kj, diagonal):
            rows = pl.ds(pl.multiple_of(kj * TQ, TQ), TQ)
            vb = v_ref[rows, :]
            for a, (doa, lse_a, dl_a) in enumerate(heads):
                kb = ks[a][rows, :]
                s = _dot_nt(qs[a], kb) * scale
                if diagonal:
                    s = jnp.where(_diag_mask(), s, NEG)
                p = jnp.exp(s - lse_a)
                ds = (p * (_dot_nt(doa, vb) - dl_a) * scale).astype(BF16)
                dq_acc[a] += _dot(ds, kb)
                dk_acc[a, rows, :] += _dot_tn(ds, qs[a])
                dv_acc[rows, :] += _dot_tn(p.astype(BF16), doa)

        def kloop(kj, c2):
            step(kj, False)
            return c2

        lax.fori_loop(0, qi, kloop, 0)
        step(qi, True)
        dq_ref[:, 0:LANE] = dq_acc[0].astype(dq_ref.dtype)
        dq_ref[:, LANE:2 * LANE] = dq_acc[1].astype(dq_ref.dtype)

        @pl.when(qi == n - 1)
        def _():
            dk_ref[:, 0:LANE] = dk_acc[0].astype(dk_ref.dtype)
            dk_ref[:, LANE:2 * LANE] = dk_acc[1].astype(dk_ref.dtype)
            dv_ref[...] = dv_acc[...].astype(dv_ref.dtype)

    blk, whole = _mla_specs(t)
    even, odd, pair = (lambda hp: 2 * hp), (lambda hp: 2 * hp + 1), (lambda hp: hp)
    wide = jax.ShapeDtypeStruct((t, N_HEADS * LANE), BF16)
    return pl.pallas_call(
        body, name=name, grid=(N_HEADS // 2, n),
        in_specs=[blk(even), blk(odd), whole(even), whole(odd), whole(lambda hp: MLA_V_OFF + hp), blk(pair), blk(pair), blk(pair)],
        out_specs=[pl.BlockSpec((TQ, 2 * LANE), lambda hp, qi: (qi, hp)), pl.BlockSpec((t, 2 * LANE), lambda hp, qi: (0, hp)), whole(pair)],
        out_shape=[wide, wide, jax.ShapeDtypeStruct((t, N_HEADS * HEAD_DIM), BF16)],
        scratch_shapes=[pltpu.VMEM((2, TQ, LANE), F32), pltpu.VMEM((2, t, LANE), F32), pltpu.VMEM((t, LANE), F32)],
        compiler_params=_cparams(("arbitrary", "arbitrary")),
    )(q, q, k, k, kv, do, lse, delta)


def _bias_lookup(bucket, table_t, *, name):
    nh, npos = bucket.shape
    tp = 4096

    def body(b_ref, t_ref, o_ref):
        bk, tab = b_ref[...], t_ref[...]
        acc = jnp.zeros(bk.shape, F32)
        for i in range(REL_BUCKETS):
            acc = jnp.where(bk == i, tab[:, i:i + 1], acc)
        o_ref[...] = acc

    return pl.pallas_call(
        body, name=name, grid=(npos // tp,),
        in_specs=[pl.BlockSpec((nh, tp), lambda i: (0, i)), pl.BlockSpec((nh, REL_BUCKETS), lambda i: (0, 0))],
        out_specs=pl.BlockSpec((nh, tp), lambda i: (0, i)),
        out_shape=jax.ShapeDtypeStruct((nh, npos), F32),
        compiler_params=_cparams(("parallel",)),
    )(bucket, table_t)


def _bias_grad(bucket, ds0, ds1, *, name):
    nh, npos = bucket.shape
    tp = 4096

    def body(b_ref, a_ref, c_ref, o_ref):
        i = pl.program_id(0)
        bk, ds = b_ref[...], a_ref[...] + c_ref[...]
        lane = lax.broadcasted_iota(jnp.int32, (nh, REL_BUCKETS), 1)
        acc = jnp.zeros((nh, REL_BUCKETS), F32)
        for j in range(REL_BUCKETS):
            col = jnp.sum(jnp.where(bk == j, ds, 0.0), axis=1, keepdims=True)
            acc = acc + jnp.where(lane == j, col, 0.0)

        @pl.when(i == 0)
        def _():
            o_ref[...] = acc

        @pl.when(i > 0)
        def _():
            o_ref[...] += acc

    return pl.pallas_call(
        body, name=name, grid=(npos // tp,),
        in_specs=[pl.BlockSpec((nh, tp), lambda i: (0, i))] * 3,
        out_specs=pl.BlockSpec((nh, REL_BUCKETS), lambda i: (0, 0)),
        out_shape=jax.ShapeDtypeStruct((nh, REL_BUCKETS), F32),
        compiler_params=_cparams(("arbitrary",)),
    )(bucket, ds0, ds1)


def _t5_bucket(dist):
    n = jnp.maximum(dist, 0)
    max_exact = REL_BUCKETS // 2
    scaled = jnp.log(jnp.maximum(n, 1).astype(F32) / max_exact) / math.log(REL_MAX_DIST / max_exact)
    large = max_exact + (scaled * (REL_BUCKETS - max_exact)).astype(jnp.int32)
    return jnp.where(n < max_exact, n, jnp.minimum(large, REL_BUCKETS - 1))


def _bucket_index():
    qi = jnp.arange(BLK)[:, None]
    ci = jnp.arange(2 * BLK)[None, :]
    step = BLK + qi - ci
    per_group = [_t5_bucket(step * d).reshape(1, -1) for d in A_DILS + (1,)]
    return jnp.concatenate([jnp.tile(b, (N_HEADS, 1)) for b in per_group], axis=0).astype(jnp.int32)


def _sigmoid(x):
    return 1.0 / (1.0 + jnp.exp(-x))


def _ln_stats(z):
    mu = jnp.mean(z, axis=-1, keepdims=True)
    zc = z - mu
    var = jnp.mean(zc * zc, axis=-1, keepdims=True)
    return zc * lax.rsqrt(var + LN_EPS)


def _ln_fwd(x, mix, g, b, *, name):
    def fn(i, nt, xv, mv, gv, bv):
        z = ALPHA * xv + mv
        y = _ln_stats(z) * gv + bv
        return y, y, z

    c = x.shape[-1]
    y, yb, z = _rowwise(fn, [(x[None], c, 0), (mix[None], c, 0)], pars=[(g.reshape(1, 1, c), c, 0), (b.reshape(1, 1, c), c, 0)],
                        outs=[(c, c, 0, F32), (c, c, 0, BF16), (c, c, 0, F32)], tm=512, name=name)
    return y[0], yb[0], z[0]


def _ln_bwd(z, g, dys, coefs, *, name):
    n = len(dys)

    def fn(i, nt, zv, *rest):
        gv = rest[n]
        dy = coefs[0] * rest[0]
        for cf, t in zip(coefs[1:], rest[1:n]):
            dy = dy + cf * t
        mu = jnp.mean(zv, axis=-1, keepdims=True)
        zc = zv - mu
        r = lax.rsqrt(jnp.mean(zc * zc, axis=-1, keepdims=True) + LN_EPS)
        xh = zc * r
        dxh = dy * gv
        dz = r * (dxh - jnp.mean(dxh, axis=-1, keepdims=True) - xh * jnp.mean(dxh * xh, axis=-1, keepdims=True))
        return dz, dz, jnp.sum(dy * xh, axis=0, keepdims=True), jnp.sum(dy, axis=0, keepdims=True)

    c = z.shape[-1]
    dz, dzb, dg, db = _rowwise(fn, [(z[None], c, 0)] + [(d[None], c, 0) for d in dys], pars=[(g.reshape(1, 1, c), c, 0)],
                               outs=[(c, c, 0, F32), (c, c, 0, BF16)], accs=[(1, c, c, 0), (1, c, c, 0)], tm=512, name=name)
    return dz[0], dzb[0], dg.reshape(c), db.reshape(c)


def _rms_fwd(src, c, off, g, *, name):
    def fn(i, nt, xv, gv):
        return xv * lax.rsqrt(jnp.mean(xv * xv, axis=-1, keepdims=True) + RMS_EPS) * gv

    return _rowwise(fn, [(src[None], c, off)], pars=[(g.reshape(1, 1, c), c, 0)], outs=[(c, c, 0, BF16)], tm=1024, name=name)[0][0]


def _rms_bwd(src, c, off, g, dy, *, name):
    def fn(i, nt, xv, dyv, gv):
        r = lax.rsqrt(jnp.mean(xv * xv, axis=-1, keepdims=True) + RMS_EPS)
        gd = gv * dyv
        dx = gd * r - xv * (r * r * r) * jnp.mean(gd * xv, axis=-1, keepdims=True)
        return dx, jnp.sum(dyv * xv * r, axis=0, keepdims=True)

    dx, dg = _rowwise(fn, [(src[None], c, off), (dy[None], c, 0)], pars=[(g.reshape(1, 1, c), c, 0)],
                      outs=[(c, c, 0, BF16)], accs=[(1, c, c, 0)], tm=1024, name=name)
    return dx[0], dg.reshape(c)


def _rope_slabs(x, n_slab, c, s, *, add=None, to_front=False, name):
    half = C_ROPE // 2

    def fn(i, nt, xv, cv, sv, *rest):
        lane = lax.broadcasted_iota(jnp.int32, (1, LANE), 1)
        extra = pltpu.roll(rest[0], C_NOPE, axis=1) if rest else None
        outs = []
        for h in range(n_slab):
            xs = xv[:, h * LANE:(h + 1) * LANE]
            if extra is not None:
                xs = xs + extra
            swapped = jnp.where(lane < C_NOPE + half, pltpu.roll(xs, LANE - half, axis=1), pltpu.roll(xs, half, axis=1))
            y = xs * cv + swapped * sv
            if to_front:
                y = jnp.where(lane < C_ROPE, pltpu.roll(y, LANE - C_NOPE, axis=1), 0.0)
            outs.append(y)
        return jnp.concatenate(outs, axis=1) if n_slab > 1 else outs[0]

    w = n_slab * LANE
    rows = [(x[None], w, 0), (c[None], LANE, 0), (s[None], LANE, 0)]
    if add is not None:
        rows.append((add[0][None], LANE, add[1]))
    return _rowwise(fn, rows, outs=[(w, w, 0, BF16)], tm=512, name=name)[0][0]


def _merge_fwd(proj, b_gate, ys, *, name):
    def fn(i, nt, g0, g1, g2, ya, yb, yc, bg):
        return (_sigmoid(g0 + bg[:, 0:1024]) * ya + _sigmoid(g1 + bg[:, 1024:2048]) * yb
                + _sigmoid(g2 + bg[:, 2048:3072]) * yc)

    rows = [(proj[None], 1024, j) for j in range(3)] + [(y[None], 1024, 0) for y in ys]
    return _rowwise(fn, rows, pars=[(b_gate.reshape(1, 1, 3072), 3072, 0)], outs=[(1024, 1024, 0, BF16)], tm=512, name=name)[0][0]


def _merge_bwd(proj, b_gate, ys, dm, *, name):
    def fn(i, nt, g0, g1, g2, ya, yb, yc, dmv, bg):
        outs, dgs = [], []
        for j, (gp, y) in enumerate(((g0, ya), (g1, yb), (g2, yc))):
            s = _sigmoid(gp + bg[:, j * 1024:(j + 1) * 1024])
            outs.append(s * dmv)
            dgs.append(dmv * y * s * (1.0 - s))
        return outs + dgs + [jnp.sum(d, axis=0, keepdims=True) for d in dgs]

    rows = [(proj[None], 1024, j) for j in range(3)] + [(y[None], 1024, 0) for y in ys] + [(dm[None], 1024, 0)]
    res = _rowwise(fn, rows, pars=[(b_gate.reshape(1, 1, 3072), 3072, 0)], outs=[(1024, 1024, 0, BF16)] * 6,
                   accs=[(1, 1024, 1024, 0)] * 3, tm=256, name=name)
    dys = [r[0] for r in res[0:3]]
    dgp = [r[0] for r in res[3:6]]
    dbg = jnp.concatenate([r.reshape(1024) for r in res[6:9]])
    return dys, dgp, dbg


def _shift_down(u, halo, i, k):
    ext = jnp.concatenate([jnp.where(i > 0, halo, 0.0), u], axis=0)
    return pltpu.roll(ext, k, axis=0)[8:]


def _shift_up(u, halo, i, nt, k):
    ext = jnp.concatenate([u, jnp.where(i < nt - 1, halo, 0.0)], axis=0)
    n = ext.shape[0]
    return pltpu.roll(ext, n - k, axis=0)[:n - 8]


GLU_C = D_FF // 2


def _conv(u, halo, i, w, b):
    return w[0:1] * _shift_down(u, halo, i, 2) + w[1:2] * _shift_down(u, halo, i, 1) + w[2:3] * u + b


def _glu_fwd(ug, uv, conv_w, conv_b, *, name):
    def fn(i, nt, g, v, hg, hv, wg, wv, bg, bv):
        cg, cv = _conv(g, hg, i, wg, bg), _conv(v, hv, i, wv, bv)
        return cg * _sigmoid(cg) * cv

    w3, b3 = conv_w[None], conv_b.reshape(1, 1, -1)
    c = GLU_C
    return _rowwise(fn, [(ug[None], c, 0), (uv[None], c, 0)], halos=[(ug[None], c, 0, "prev"), (uv[None], c, 0, "prev")],
                    pars=[(w3, c, 0), (w3, c, 2), (b3, c, 0), (b3, c, 2)], outs=[(D_FF, c, 0, BF16)], tm=256, ncol=2, name=name)[0][0]


def _glu_bwd_a(ug, uv, conv_w, conv_b, dh, *, name):
    def fn(i, nt, g, v, dhv, hg, hv, wg, wv, bg, bv):
        g1, g2 = _shift_down(g, hg, i, 1), _shift_down(g, hg, i, 2)
        v1, v2 = _shift_down(v, hv, i, 1), _shift_down(v, hv, i, 2)
        cg = wg[0:1] * g2 + wg[1:2] * g1 + wg[2:3] * g + bg
        cv = wv[0:1] * v2 + wv[1:2] * v1 + wv[2:3] * v + bv
        s = _sigmoid(cg)
        dcv = dhv * cg * s
        dcg = dhv * cv * (s * (1.0 + cg * (1.0 - s)))
        red = lambda a: jnp.sum(a, axis=0, keepdims=True)
        return (dcg, dcv, red(dcg), red(dcv), red(dcg * g2), red(dcg * g1), red(dcg * g),
                red(dcv * v2), red(dcv * v1), red(dcv * v))

    w3, b3 = conv_w[None], conv_b.reshape(1, 1, -1)
    c = GLU_C
    res = _rowwise(fn, [(ug[None], c, 0), (uv[None], c, 0), (dh[None], c, 0)],
                   halos=[(ug[None], c, 0, "prev"), (uv[None], c, 0, "prev")],
                   pars=[(w3, c, 0), (w3, c, 2), (b3, c, 0), (b3, c, 2)],
                   outs=[(D_FF, c, 0, F32), (D_FF, c, 0, F32)], accs=[(1, D_FF, c, 0)] * 8, tm=256, ncol=2, name=name)
    dcg, dcv = res[0][0], res[1][0]
    dconv_b = jnp.concatenate([res[2].reshape(D_FF), res[3].reshape(D_FF)])
    dconv_w = jnp.concatenate([jnp.concatenate([res[4 + j].reshape(1, D_FF) for j in range(3)], axis=0),
                               jnp.concatenate([res[7 + j].reshape(1, D_FF) for j in range(3)], axis=0)], axis=1)
    return dcg, dcv, dconv_w, dconv_b


def _glu_bwd_b(dc, conv_w, half, *, name):
    def fn(i, nt, d, hd, w):
        return w[2:3] * d + w[1:2] * _shift_up(d, hd, i, nt, 1) + w[0:1] * _shift_up(d, hd, i, nt, 2)

    c = GLU_C
    return _rowwise(fn, [(dc[None], c, 0)], halos=[(dc[None], c, 0, "next")], pars=[(conv_w[None], c, 2 * half)],
                    outs=[(D_FF, c, 0, BF16)], tm=256, ncol=2, name=name)[0][0]


def _loss_and_grad(y, tgt, *, name):
    def fn(i, nt, yv, tv):
        err = yv - tv
        part = jnp.sum(jnp.sum(err * err, axis=0, keepdims=True), axis=1, keepdims=True) * (0.5 / D_MODEL)
        return err * (1.0 / D_MODEL), jnp.zeros((1, LANE), F32) + part

    dy, part = _rowwise(fn, [(y[None], D_MODEL, 0), (tgt[None], D_MODEL, 0)], outs=[(D_MODEL, D_MODEL, 0, F32)],
                        accs=[(1, LANE, LANE, 0)], tm=512, name=name)
    return dy[0], part.reshape(LANE)


def _lincomb(terms, coefs, *, name):
    def fn(i, nt, *vs):
        acc = coefs[0] * vs[0]
        for cf, v in zip(coefs[1:], vs[1:]):
            acc = acc + cf * v
        return acc

    c = terms[0].shape[-1]
    return _rowwise(fn, [(a[None], c, 0) for a in terms], outs=[(c, c, 0, F32)], tm=512, name=name)[0][0]


def _sum_rows(terms, *, tm, name, dtype=F32):
    def fn(i, nt, *vs):
        acc = vs[0]
        for v in vs[1:]:
            acc = acc + v
        return acc

    c = terms[0].shape[-1]
    return _rowwise(fn, [(t, c, 0) for t in terms], outs=[(c, c, 0, dtype)], tm=tm, name=name)[0]


def _head_sums(x):
    lo = _lane_lo()
    parts = []
    for j in range(x.shape[1] // LANE):
        blk = x[:, j * LANE:(j + 1) * LANE]
        s_lo = jnp.sum(jnp.where(lo, blk, 0.0), axis=1, keepdims=True)
        s_hi = jnp.sum(jnp.where(lo, 0.0, blk), axis=1, keepdims=True)
        parts.append(jnp.where(lo, s_lo, s_hi))
    return jnp.concatenate(parts, axis=1)


def _group_weights(l0, l1, l2):
    m = jnp.maximum(jnp.maximum(l0, l1), l2)
    es = [jnp.exp(l - m) for l in (l0, l1, l2)]
    inv = 1.0 / (es[0] + es[1] + es[2])
    return [e * inv for e in es]


def _combine_fwd(os_, lses, *, name):
    def fn(i, nt, o0, o1, o2, l0, l1, l2):
        w = _group_weights(l0, l1, l2)
        return w[0] * o0 + w[1] * o1 + w[2] * o2

    c = os_[0].shape[-1]
    return _rowwise(fn, [(a[None], c, 0) for a in list(os_) + list(lses)], outs=[(c, c, 0, BF16)], tm=512, name=name)[0][0]


def _combine_bwd(os_, lses, do_a, *, name):
    def fn(i, nt, o0, o1, o2, l0, l1, l2, da):
        ws = _group_weights(l0, l1, l2)
        dws = [_head_sums(da * o) for o in (o0, o1, o2)]
        mean = ws[0] * dws[0] + ws[1] * dws[1] + ws[2] * dws[2]
        return [w * da for w in ws] + [w * mean for w in ws]

    c = do_a.shape[-1]
    res = _rowwise(fn, [(a[None], c, 0) for a in list(os_) + list(lses) + [do_a]], outs=[(c, c, 0, BF16)] * 3 + [(c, c, 0, F32)] * 3,
                   tm=256, name=name)
    return [r[0] for r in res[0:3]], [r[0] for r in res[3:6]]


def _delta(do, o, *, name):
    def fn(i, nt, d, ov):
        return d, _head_sums(d * ov)

    c = do.shape[-1]
    res = _rowwise(fn, [(do[None], c, 0), (o[None], c, 0)], outs=[(c, c, 0, BF16), (c, c, 0, F32)], tm=512, name=name)
    return res[0][0], res[1][0]


def _rowdot(at, bt, *, name):
    def fn(a, b):
        return jnp.sum(a * b, axis=0, keepdims=True)

    return _lanewise(fn, [at, bt], [(1, F32)], tl=2048, name=name)[0]


def _adamw(w, g, m, v, *, name):
    c1 = 1.0 - ADAM_B1 ** ADAM_STEP
    c2 = 1.0 - ADAM_B2 ** ADAM_STEP

    def fn(i, nt, wv, gv, mv, vv):
        mn = ADAM_B1 * mv + (1.0 - ADAM_B1) * gv
        vn = ADAM_B2 * vv + (1.0 - ADAM_B2) * (gv * gv)
        delta = -ADAM_LR * ((mn / c1) / (jnp.sqrt(vn / c2) + ADAM_EPS) + ADAM_WD * wv)
        return delta, mn, vn

    r, c = w.shape
    rp = _ceil_to(r, 8)
    pad = lambda a: jnp.pad(a, ((0, rp - r), (0, 0))) if rp != r else a
    tm = rp
    for cand in (128, 64, 32, 16, 8):
        if rp % cand == 0:
            tm = cand
            break
    res = _rowwise(fn, [(pad(a)[None], c, 0) for a in (w, g, m, v)], outs=[(c, c, 0, F32)] * 3, tm=tm, name=name)
    return [x[0][:r] for x in res]


ANY = pl.BlockSpec(memory_space=pl.ANY)


def _place():
    x, y, c = lax.axis_index("x"), lax.axis_index("y"), lax.axis_index("c")
    chips = [(1 - x, y), (x, 1 - y), (1 - x, 1 - y)]
    return x, y, c, chips


def _allgather_weights(arrs):
    n = len(arrs)

    def body(*refs):
        ins, outs, send_sems, recv_sems = refs[:n], refs[n:2 * n], refs[2 * n], refs[2 * n + 1]
        x, y, c, chips = _place()
        j = 2 * x + y

        def cp(i, k, src, chip_idx, half, to):
            return pltpu.make_async_remote_copy(src_ref=src, dst_ref=outs[i].at[chip_idx, half], send_sem=send_sems.at[k],
                                                recv_sem=recv_sems.at[k], device_id=to, device_id_type=MESH)

        first, passed = [], []
        for i in range(n):
            for r, (cx, cy) in enumerate(chips):
                first.append(cp(i, 3 * i + r, ins[i].at[c], j, c, (cx, cy, c)))
                passed.append(cp(i, 3 * (n + i) + r, outs[i].at[2 * cx + cy, c], 2 * cx + cy, c, (x, y, 1 - c)))
        for d in first:
            d.start()
        for i in range(n):
            for r, (cx, cy) in enumerate(chips):
                cp(i, 3 * i + r, ins[i].at[c], 2 * cx + cy, c, (x, y, c)).wait_recv()
                passed[3 * i + r].start()
        for i in range(n):
            for r, (cx, cy) in enumerate(chips):
                cp(i, 3 * (n + i) + r, ins[i].at[c], 2 * cx + cy, 1 - c, (x, y, c)).wait_recv()
        for d in first + passed:
            d.wait_send()

    return pl.pallas_call(
        body, name="allgather_weights", in_specs=[ANY] * n, out_specs=[ANY] * n,
        out_shape=[jax.ShapeDtypeStruct((N_CHIP,) + a.shape, a.dtype) for a in arrs],
        scratch_shapes=[pltpu.SemaphoreType.DMA((6 * n,)), pltpu.SemaphoreType.DMA((6 * n,))],
    )(*arrs)


def _sibling_swap(gs):
    n = len(gs)

    def body(*refs):
        ins, outs, send_sems, recv_sems = refs[:n], refs[n:2 * n], refs[2 * n], refs[2 * n + 1]
        x, y, c, _ = _place()
        cps = [pltpu.make_async_remote_copy(src_ref=ins[i].at[1 - c], dst_ref=outs[i], send_sem=send_sems.at[i],
                                            recv_sem=recv_sems.at[i], device_id=(x, y, 1 - c), device_id_type=MESH)
               for i in range(n)]
        for d in cps:
            d.start()
        for d in cps:
            d.wait_recv()
        for d in cps:
            d.wait_send()

    return pl.pallas_call(
        body, name="grad_sibling_swap", in_specs=[ANY] * n, out_specs=[ANY] * n,
        out_shape=[jax.ShapeDtypeStruct(g.shape[1:], g.dtype) for g in gs],
        scratch_shapes=[pltpu.SemaphoreType.DMA((n,)), pltpu.SemaphoreType.DMA((n,))],
    )(*gs)


def _chip_scatter(ps):
    n = len(ps)

    def body(*refs):
        ins, outs, send_sems, recv_sems = refs[:n], refs[n:2 * n], refs[2 * n], refs[2 * n + 1]
        x, y, c, chips = _place()
        sends = []
        for i in range(n):
            for r, (cx, cy) in enumerate(chips):
                sends.append(pltpu.make_async_remote_copy(src_ref=ins[i].at[2 * cx + cy], dst_ref=outs[i].at[r], send_sem=send_sems.at[3 * i + r],
                                                          recv_sem=recv_sems.at[3 * i + r], device_id=(cx, cy, c), device_id_type=MESH))
        for d in sends:
            d.start()
        for d in sends:
            d.wait_recv()
        for d in sends:
            d.wait_send()

    return pl.pallas_call(
        body, name="grad_chip_scatter", in_specs=[ANY] * n, out_specs=[ANY] * n,
        out_shape=[jax.ShapeDtypeStruct((3,) + p.shape[1:], p.dtype) for p in ps],
        scratch_shapes=[pltpu.SemaphoreType.DMA((3 * n,)), pltpu.SemaphoreType.DMA((3 * n,))],
    )(*ps)


def _sibling_share(rs):
    n = len(rs)

    def body(*refs):
        ins, outs, send_sems, recv_sems = refs[:n], refs[n:2 * n], refs[2 * n], refs[2 * n + 1]
        x, y, c, _ = _place()
        cps = [pltpu.make_async_remote_copy(src_ref=ins[i], dst_ref=outs[i], send_sem=send_sems.at[i], recv_sem=recv_sems.at[i],
                                            device_id=(x, y, 1 - c), device_id_type=MESH) for i in range(n)]
        for d in cps:
            d.start()
        for d in cps:
            d.wait_recv()
        for d in cps:
            d.wait_send()

    return pl.pallas_call(
        body, name="grad_sibling_share", in_specs=[ANY] * n, out_specs=[ANY] * n,
        out_shape=[jax.ShapeDtypeStruct(r.shape, r.dtype) for r in rs],
        scratch_shapes=[pltpu.SemaphoreType.DMA((n,)), pltpu.SemaphoreType.DMA((n,))],
    )(*rs)


def _allreduce_small(s):
    rows, w = s.shape
    n_dev = 8

    def body(s_ref, out_ref, slots, send_sems, recv_sems):
        x, y, c, _ = _place()
        me = 4 * x + 2 * y + c
        slots[me] = s_ref[...]
        peers = []
        for r in range(1, n_dev):
            px = 1 - x if r & 4 else x
            py = 1 - y if r & 2 else y
            pc = 1 - c if r & 1 else c
            peers.append((px, py, pc))
        sends = [pltpu.make_async_remote_copy(src_ref=s_ref, dst_ref=slots.at[me], send_sem=send_sems.at[r], recv_sem=recv_sems.at[r],
                                              device_id=peer, device_id_type=MESH) for r, peer in enumerate(peers)]
        for d in sends:
            d.start()
        for r, (px, py, pc) in enumerate(peers):
            pltpu.make_async_remote_copy(src_ref=s_ref, dst_ref=slots.at[4 * px + 2 * py + pc], send_sem=send_sems.at[r],
                                         recv_sem=recv_sems.at[r], device_id=(x, y, c), device_id_type=MESH).wait_recv()
        for d in sends:
            d.wait_send()
        acc = slots[0]
        for k in range(1, n_dev):
            acc = acc + slots[k]
        out_ref[...] = acc

    vm = pl.BlockSpec(memory_space=pltpu.VMEM)
    return pl.pallas_call(
        body, name="allreduce_small", in_specs=[vm], out_specs=vm, out_shape=jax.ShapeDtypeStruct((rows, w), F32),
        scratch_shapes=[pltpu.VMEM((n_dev, rows, w), F32), pltpu.SemaphoreType.DMA((n_dev - 1,)), pltpu.SemaphoreType.DMA((n_dev - 1,))],
    )(s)


W_IN_SHARD = D_IN // N_CHIP
W_IN_ROWS_G = 2304
REDUCED = tuple(m for m in MATS if m[0] != "conv_w")
CONV_W_SIZE = 3 * 2 * D_FF


def _weight_send(name, a):
    if name == "w_in":
        return jnp.swapaxes(a, 1, 2).astype(BF16)
    return a if name == "conv_w" else a.astype(BF16)


def _full_weights(gathered, l):
    g = {k: v[:, l] for k, v in gathered.items()}
    s = g["w_in"].astype(F32).reshape(D_IN, D_MODEL)
    dup = lambda a: jnp.concatenate([a[0:64], a[0:64], a[64:128], a[64:128]], axis=0)
    o = ORIG
    wm_t = jnp.concatenate([s[o["gate"]:], s[o["a"]:o["a"] + A_COLS], s[o["bq"]:o["bk"]], dup(s[o["bk"]:o["bv"]]), dup(s[o["bv"]:o["cq"]]),
                            s[o["cq"]:o["gate"]], jnp.zeros((M_COLS - M_CDKV - (o["gate"] - o["cdkv"]), D_MODEL), F32)], axis=0).astype(BF16)
    wg_t = [s[o["a"] + gi * A_COLS:o["a"] + (gi + 1) * A_COLS].astype(BF16) for gi in (1, 2)]
    full = {name: jnp.moveaxis(g[name], 0, ax).reshape(shape) for name, shape, ax in MATS if name != "w_in"}
    uq = full["w_uq"].reshape(C_Q_RANK, N_HEADS, C_NOPE + C_ROPE)
    ukv = full["w_ukv"].reshape(C_KV_RANK, N_HEADS, 2 * C_NOPE)
    w_uq_p = _pad_lanes(uq).reshape(C_Q_RANK, N_HEADS * LANE)
    w_ukv_p = jnp.concatenate([_pad_lanes(ukv[:, :, :C_NOPE]).reshape(C_KV_RANK, N_HEADS * LANE),
                               ukv[:, :, C_NOPE:].reshape(C_KV_RANK, N_HEADS * HEAD_DIM)], axis=1)
    return {"wm_t": wm_t, "wg_t": wg_t, "w_uq_p": w_uq_p, "w_ukv_p": w_ukv_p, "w_branch": full["w_branch"], "w_out": full["w_out"],
            "wup_g": full["w_ffn_up"][:, :D_FF], "wup_v": full["w_ffn_up"][:, D_FF:], "conv_w": full["conv_w"],
            "w_ffn_down": full["w_ffn_down"]}


def _grad_send(name, g, shape, ax):
    if name == "w_in":
        return jnp.pad(g.reshape(N_CHIP, W_IN_SHARD, D_MODEL), ((0, 0), (0, W_IN_ROWS_G - W_IN_SHARD), (0, 0)))
    split = shape[:ax] + (N_CHIP, shape[ax] // N_CHIP) + shape[ax + 1:]
    return jnp.moveaxis(g.reshape(split), ax, 0)


def _grad_recv(name, r):
    return r[:W_IN_SHARD].T if name == "w_in" else r


def _pack_small(rel, small, conv_w, extra):
    parts = [rel.reshape(-1)]
    for l in range(DEPTH):
        for name in SMALL:
            parts.append(small[name][l].reshape(-1))
    parts += [conv_w.reshape(-1), extra]
    flat = jnp.concatenate(parts)
    rows = _ceil_to(-(-flat.shape[0] // LANE), 8)
    return jnp.pad(flat, (0, rows * LANE - flat.shape[0])).reshape(rows, LANE)


def _unpack_small(buf):
    flat = buf.reshape(-1)
    rel = flat[:REL_BUCKETS * 32].reshape(REL_BUCKETS, 32)
    off = REL_BUCKETS * 32
    small = {name: [] for name in SMALL}
    for l in range(DEPTH):
        for name in SMALL:
            n = SMALL_SIZES[name]
            small[name].append(flat[off:off + n])
            off += n
    conv_w = flat[off:off + DEPTH * CONV_W_SIZE].reshape(DEPTH, 3, 2 * D_FF)
    off += DEPTH * CONV_W_SIZE
    return rel, {k: jnp.stack(v) for k, v in small.items()}, conv_w, flat[off:off + LANE]


def _rows2d(a, lead):
    return a.reshape(a.shape[:lead] + (-1, a.shape[-1]))


def _row_tile(rows):
    for cand in (512, 256, 128, 64, 32, 16, 8):
        if rows % cand == 0:
            return cand
    raise ValueError(rows)


def _pair_add(g, got, core, *, name):
    g2, got2 = _rows2d(g, 1), _rows2d(got, 0)
    rows, c = got2.shape
    tm = _row_tile(rows)
    flag = jnp.zeros((1, 1, LANE), F32) + core.astype(F32)

    def fn(i, nt, a0, a1, b, f):
        return jnp.where(f[:, 0:1] == 0.0, a0, a1) + b

    stacked = g2.reshape(1, 2 * rows, c)
    out = _rowwise(fn, [(stacked, c, 0, 0), (stacked, c, 0, rows // tm), (got2[None], c, 0)], pars=[(flag, LANE, 0)],
                   outs=[(c, c, 0, BF16)], tm=tm, t=rows, name=name)[0][0]
    return out.reshape(got.shape)


def _chip_add(own, got, *, name):
    own2, got2 = _rows2d(own, 0), _rows2d(got, 1)
    rows, c = own2.shape
    tm = _row_tile(rows)

    def fn(i, nt, a, b0, b1, b2):
        return ((a.astype(F32) + b0.astype(F32)) + b1.astype(F32)) + b2.astype(F32)

    stacked = got2.reshape(1, 3 * rows, c)
    out = _rowwise(fn, [(own2[None], c, 0)] + [(stacked, c, 0, k * (rows // tm)) for k in range(3)],
                   outs=[(c, c, 0, F32)], tm=tm, t=rows, name=name)[0][0]
    return out.reshape(own.shape)


def _perm(a, d):
    if d == 1:
        return a
    t = a.shape[0]
    return jnp.swapaxes(a.reshape((t // d, d) + a.shape[1:]), 0, 1).reshape(a.shape)


def _unperm(a, d):
    if d == 1:
        return a
    t = a.shape[0]
    return jnp.swapaxes(a.reshape((d, t // d) + a.shape[1:]), 0, 1).reshape(a.shape)


def _pad_lanes(a, w=HP):
    return jnp.pad(a, [(0, 0)] * (a.ndim - 1) + [(0, w - a.shape[-1])])


def _heads_blocks(a, blk):
    t, h, _ = a.shape
    return jnp.transpose(_pad_lanes(a), (1, 0, 2)).astype(BF16).reshape(h, t // blk, blk, HP)


def _heads_blocks_t(a, blk):
    return jnp.swapaxes(_heads_blocks(a, blk), -1, -2)


def _from_blocks_t(a):
    h, n, d, blk = a.shape
    return jnp.transpose(a, (1, 3, 0, 2)).reshape(n * blk, h, d)


def _from_blocks(a):
    h, n, blk, d = a.shape
    return jnp.transpose(a, (1, 2, 0, 3)).reshape(n * blk, h, d)


def _to_hdt(a):
    return jnp.transpose(a, (1, 2, 0))


def _rope_tables(t):
    pos = jnp.arange(t, dtype=F32)
    inv_freq = ROPE_BASE ** (-jnp.arange(0, C_ROPE, 2, dtype=F32) / C_ROPE)
    ang = pos[:, None] * inv_freq[None, :]
    cos, sin = jnp.cos(ang), jnp.sin(ang)
    ones, zeros = jnp.ones((t, C_NOPE), F32), jnp.zeros((t, C_NOPE), F32)
    tail = LANE - C_NOPE - C_ROPE
    c = jnp.concatenate([ones, cos, cos, ones[:, :tail]], axis=1)
    s = jnp.concatenate([zeros, -sin, sin, zeros[:, :tail]], axis=1)
    return c, s


def _band_calls(t, proj, projs_g, sinks):
    none = jnp.full((N_HEADS,), NEG, F32)
    a0 = M_A0 // LANE
    calls = [(proj, (a0, a0 + 4, a0 + 8), t // BLK, BLK, False, none)]
    calls += [(pg, (0, 4, 8), t // (d * BLK), BLK, False, none) for pg, d in zip(projs_g, A_DILS[1:])]
    calls.append((proj, (M_BQ // LANE, M_BK // LANE, M_BV // LANE), t // BLK, BLK - 1, True, sinks.astype(F32)))
    return calls


def _layer_fwd(l, x, xb, w, p, biases, rope_cs):
    t = x.shape[0]
    n = f"l{l}_"
    xps = [_perm(xb, d) for d in A_DILS[1:]]
    proj = _mm(xb, w["wm_t"], tb=True, out_dtype=BF16, name=n + "proj")
    projs_g = [_mm(xp, wg, tb=True, out_dtype=BF16, tn=768, name=n + f"proj_g{i + 1}") for i, (xp, wg) in enumerate(zip(xps, w["wg_t"]))]
    s = {"xb": xb, "xps": xps, "proj": proj, "projs_g": projs_g}

    calls = _band_calls(t, proj, projs_g, p["sinks"])
    outs = [_band_fwd(src, offs, biases[i], sk, nb=nb, lim=lim, gqa=gqa, name=n + f"band{i}")
            for i, (src, offs, nb, lim, gqa, sk) in enumerate(calls)]
    os_ = [_unperm(outs[gi][0], d) for gi, d in enumerate(A_DILS)]
    lses = [_unperm(outs[gi][1], d) for gi, d in enumerate(A_DILS)]
    o_a = _combine_fwd(os_, lses, name=n + "combine_fwd")
    o_b_f, lse_b = outs[3]
    o_b = o_b_f.astype(BF16)
    s.update(os=os_, lses=lses, lses_p=[outs[gi][1] for gi in range(3)], o_b=o_b_f, lse_b=lse_b)

    rq = _rms_fwd(proj, C_Q_RANK, M_CQ // C_Q_RANK, p["q_norm_g"], name=n + "rms_q")
    rkv = _rms_fwd(proj, C_KV_RANK, M_CDKV // C_KV_RANK, p["kv_norm_g"], name=n + "rms_kv")
    q_cp = _mm(rq, w["w_uq_p"], out_dtype=BF16, name=n + "uq")
    kv_cp = _mm(rkv, w["w_ukv_p"], out_dtype=BF16, name=n + "ukv")
    q_full = _rope_slabs(q_cp, N_HEADS, rope_cs[0], rope_cs[1], name=n + "rope_q")
    k_full = _rope_slabs(kv_cp, N_HEADS, rope_cs[0], rope_cs[1], add=(proj, (M_CDKV + C_KV_RANK) // LANE), name=n + "rope_k")
    o_c_f, lse_c = _mla_fwd(q_full, k_full, kv_cp, name=n + "mla_fwd")
    o_c = o_c_f.astype(BF16)
    s.update(rq=rq, rkv=rkv, q_full=q_full, k_full=k_full, kv_cp=kv_cp, lse_c=lse_c, o_c=o_c_f)

    obs = [o_a, o_b, o_c]
    ys = [_mm(o, w["w_branch"][i], name=n + f"branch{i}") for i, o in enumerate(obs)]
    merged = _merge_fwd(proj, p["b_gate"], ys, name=n + "merge")
    mix = _mm(merged, w["w_out"], name=n + "out")
    x1f, x1b, z1 = _ln_fwd(x, mix, p["ln1_g"], p["ln1_b"], name=n + "ln1")
    s.update(obs=obs, ys=ys, merged=merged, z1=z1, x1b=x1b)

    ug = _mm(x1b, w["wup_g"], tn=1408, name=n + "up_g")
    uv = _mm(x1b, w["wup_v"], tn=1408, name=n + "up_v")
    h = _glu_fwd(ug, uv, w["conv_w"], p["conv_b"], name=n + "glu")
    ff = _mm(h, w["w_ffn_down"], tk=1408, name=n + "down")
    x2f, x2b, z2 = _ln_fwd(x1f, ff, p["ln2_g"], p["ln2_b"], name=n + "ln2")
    s.update(ug=ug, uv=uv, h=h, z2=z2)
    return x2f, x2b, s


def _layer_bwd(l, s, dys, coefs, w, p, biases, rope_cs):
    n = f"l{l}b_"
    t = s["z2"].shape[0]
    gw, gs = {}, {}
    tr = lambda a: a.T

    dz2, dz2b, gs["ln2_g"], gs["ln2_b"] = _ln_bwd(s["z2"], p["ln2_g"], dys, coefs, name=n + "ln2")
    dh = _mm(dz2b, w["w_ffn_down"], tb=True, tn=1408, name=n + "d_h")
    gw["w_ffn_down"] = _mm(tr(s["h"]), dz2b, tm=704, name=n + "g_down")
    dcg, dcv, gw["conv_w"], gs["conv_b"] = _glu_bwd_a(s["ug"], s["uv"], w["conv_w"], p["conv_b"], dh, name=n + "glu_a")
    dug = _glu_bwd_b(dcg, w["conv_w"], 0, name=n + "glu_bg")
    duv = _glu_bwd_b(dcv, w["conv_w"], 1, name=n + "glu_bv")
    dx1_g = _mm(dug, w["wup_g"], tb=True, tk=1408, name=n + "d_x1g")
    dx1_v = _mm(duv, w["wup_v"], tb=True, tk=1408, name=n + "d_x1v")
    x1t = tr(s["x1b"])
    gw["w_ffn_up"] = jnp.concatenate([_mm(x1t, dug, tn=1408, name=n + "g_upg"), _mm(x1t, duv, tn=1408, name=n + "g_upv")], axis=1)

    dz1, dz1b, gs["ln1_g"], gs["ln1_b"] = _ln_bwd(s["z1"], p["ln1_g"], [dz2, dx1_g, dx1_v], [ALPHA, 1.0, 1.0], name=n + "ln1")
    dmerged = _mm(dz1b, w["w_out"], tb=True, name=n + "d_merged")
    gw["w_out"] = _mm(tr(s["merged"]), dz1b, name=n + "g_out")
    dys_b, dgp, gs["b_gate"] = _merge_bwd(s["proj"], p["b_gate"], s["ys"], dmerged, name=n + "merge")
    dos = [_mm(dy, w["w_branch"][i], tb=True, name=n + f"d_o{i}") for i, dy in enumerate(dys_b)]
    gw["w_branch"] = jnp.stack([_mm(tr(o), dy, name=n + f"g_branch{i}") for i, (o, dy) in enumerate(zip(s["obs"], dys_b))])

    do_gs, dpr_gs = _combine_bwd(s["os"], s["lses"], dos[0], name=n + "combine")
    do_b, dpr_b = _delta(dos[1], s["o_b"], name=n + "delta_b")
    do_list = [_perm(a, d) for a, d in zip(do_gs, A_DILS)] + [do_b]
    dpr_list = [_perm(a, d) for a, d in zip(dpr_gs, A_DILS)] + [dpr_b]
    lse_list = s["lses_p"] + [s["lse_b"]]
    calls = _band_calls(t, s["proj"], s["projs_g"], p["sinks"])
    band = [_band_bwd(src, offs, do_list[i], lse_list[i], dpr_list[i], biases[i], sk, nb=nb, lim=lim, gqa=gqa, name=n + f"band{i}")
            for i, (src, offs, nb, lim, gqa, sk) in enumerate(calls)]
    gs["sinks"] = band[3][4][:, 0, 0]
    ds_sum = jnp.concatenate([b_[3] for b_ in band], axis=0)

    do_c, delta_c = _delta(dos[2], s["o_c"], name=n + "delta_c")
    dq, dk, dv = _mla_bwd(s["q_full"], s["k_full"], s["kv_cp"], do_c, s["lse_c"], delta_c, name=n + "mla")
    dq_cp = _rope_slabs(dq, N_HEADS, rope_cs[0], -rope_cs[1], name=n + "rope_q")
    dk_sum = _rowwise(lambda i, nt, *vs: sum(vs[1:], vs[0]), [(dk[None], LANE, hh) for hh in range(N_HEADS)],
                      outs=[(LANE, LANE, 0, F32)], tm=1024, name=n + "krope_sum")[0][0]
    dkr = _rope_slabs(dk_sum, 1, rope_cs[0], -rope_cs[1], to_front=True, name=n + "rope_k")
    dkv_cp = jnp.concatenate([dk, dv], axis=1)
    d_rq = _mm(dq_cp, w["w_uq_p"], tb=True, name=n + "d_rq")
    d_rkv = _mm(dkv_cp, w["w_ukv_p"], tb=True, name=n + "d_rkv")
    g_uq = _mm(tr(s["rq"]), dq_cp, name=n + "g_uq")
    g_ukv = _mm(tr(s["rkv"]), dkv_cp, name=n + "g_ukv")
    gw["w_uq"] = g_uq.reshape(C_Q_RANK, N_HEADS, LANE)[:, :, :C_NOPE + C_ROPE].reshape(C_Q_RANK, -1)
    kw = N_HEADS * LANE
    gw["w_ukv"] = jnp.concatenate([g_ukv[:, :kw].reshape(C_KV_RANK, N_HEADS, LANE)[:, :, :C_NOPE],
                                   g_ukv[:, kw:].reshape(C_KV_RANK, N_HEADS, HEAD_DIM)], axis=2).reshape(C_KV_RANK, -1)
    dcq, gs["q_norm_g"] = _rms_bwd(s["proj"], C_Q_RANK, M_CQ // C_Q_RANK, p["q_norm_g"], d_rq, name=n + "rms_q")
    dckv, gs["kv_norm_g"] = _rms_bwd(s["proj"], C_KV_RANK, M_CDKV // C_KV_RANK, p["kv_norm_g"], d_rkv, name=n + "rms_kv")
    dcdkv = jnp.concatenate([dckv, dkr], axis=1)

    dproj = jnp.concatenate(dgp + list(band[0][:3]) + list(band[3][:3]) + [dcq, dcdkv], axis=1)
    dprojs_g = [jnp.concatenate(band[gi][:3], axis=1) for gi in (1, 2)]
    dx_terms = [_mm(dproj, w["wm_t"], tk=1024, name=n + "d_x")]
    dx_terms += [_unperm(_mm(dp, wg, tk=768, name=n + f"d_x_g{i + 1}"), d) for i, (dp, wg, d) in enumerate(zip(dprojs_g, w["wg_t"], A_DILS[1:]))]
    g_main = _mm(tr(s["xb"]), dproj, name=n + "g_in").T
    g_groups = [_mm(tr(xp), dp, tn=768, name=n + f"g_in_g{i + 1}").T for i, (xp, dp) in enumerate(zip(s["xps"], dprojs_g))]
    fold = lambda a, tag: _sum_rows([a.reshape(2, 2, HEAD_DIM, D_MODEL)[:, j] for j in range(2)], tm=HEAD_DIM,
                                    name=n + "g_fold_" + tag).reshape(2 * HEAD_DIM, D_MODEL)
    gw["w_in"] = jnp.concatenate([g_main[M_A0:M_BQ], g_groups[0], g_groups[1], g_main[M_BQ:M_BK], fold(g_main[M_BK:M_BV], "k"),
                                  fold(g_main[M_BV:M_CQ], "v"), g_main[M_CQ:M_CDKV + C_KV_RANK + C_ROPE], g_main[M_GATE:M_A0]], axis=0)
    return [dz1] + dx_terms, [ALPHA, 1.0, 1.0, 1.0], gw, gs, ds_sum


def _local_step(x, target, ws, rel_table, small):
    t = x.shape[0]
    ps = [{k: small[k][l] for k in SMALL} for l in range(DEPTH)]
    bucket = _bucket_index()
    bias_all = _bias_lookup(bucket, rel_table.T, name="bias_lookup").reshape(4, N_HEADS, BLK, 2 * BLK)
    biases = [bias_all[i] for i in range(4)]
    rope_cs = _rope_tables(t)

    saved, h, hb = [], x, x.astype(BF16)
    for l in range(DEPTH):
        h, hb, s = _layer_fwd(l, h, hb, ws[l], ps[l], biases, rope_cs)
        saved.append(s)
    dy, loss_part = _loss_and_grad(h, target, name="loss")

    dys, coefs = [dy], [1.0]
    gws, gss, dss = [None] * DEPTH, [None] * DEPTH, [None] * DEPTH
    for l in reversed(range(DEPTH)):
        dys, coefs, gws[l], gss[l], dss[l] = _layer_bwd(l, saved[l], dys, coefs, ws[l], ps[l], biases, rope_cs)
    grad_x = _lincomb(dys, coefs, name="grad_x")
    npos = 2 * BLK * BLK
    g_rel = _bias_grad(bucket, dss[0].reshape(4 * N_HEADS, npos), dss[1].reshape(4 * N_HEADS, npos), name="bias_grad").T
    gsmall = {k: jnp.stack([gss[l][k] for l in range(DEPTH)]) for k in SMALL}
    return loss_part, grad_x, gws, gsmall, g_rel


def kernel(x, rel_table, w_in, b_gate, sinks, q_norm_g, kv_norm_g, w_uq, w_ukv, w_branch, w_out, ln1_g, ln1_b, w_ffn_up, conv_w, conv_b, w_ffn_down, ln2_g, ln2_b, loss_target, m_rel_table, m_w_in, m_b_gate, m_sinks, m_q_norm_g, m_kv_norm_g, m_w_uq, m_w_ukv, m_w_branch, m_w_out, m_ln1_g, m_ln1_b, m_w_ffn_up, m_conv_w, m_conv_b, m_w_ffn_down, m_ln2_g, m_ln2_b, v_rel_table, v_w_in, v_b_gate, v_sinks, v_q_norm_g, v_kv_norm_g, v_w_uq, v_w_ukv, v_w_branch, v_w_out, v_ln1_g, v_ln1_b, v_w_ffn_up, v_conv_w, v_conv_b, v_w_ffn_down, v_ln2_g, v_ln2_b):
    wts = dict(rel_table=rel_table, w_in=w_in, b_gate=b_gate, sinks=sinks, q_norm_g=q_norm_g, kv_norm_g=kv_norm_g, w_uq=w_uq,
               w_ukv=w_ukv, w_branch=w_branch, w_out=w_out, ln1_g=ln1_g, ln1_b=ln1_b, w_ffn_up=w_ffn_up, conv_w=conv_w,
               conv_b=conv_b, w_ffn_down=w_ffn_down, ln2_g=ln2_g, ln2_b=ln2_b)
    ms = dict(rel_table=m_rel_table, w_in=m_w_in, b_gate=m_b_gate, sinks=m_sinks, q_norm_g=m_q_norm_g, kv_norm_g=m_kv_norm_g,
              w_uq=m_w_uq, w_ukv=m_w_ukv, w_branch=m_w_branch, w_out=m_w_out, ln1_g=m_ln1_g, ln1_b=m_ln1_b, w_ffn_up=m_w_ffn_up,
              conv_w=m_conv_w, conv_b=m_conv_b, w_ffn_down=m_w_ffn_down, ln2_g=m_ln2_g, ln2_b=m_ln2_b)
    vs = dict(rel_table=v_rel_table, w_in=v_w_in, b_gate=v_b_gate, sinks=v_sinks, q_norm_g=v_q_norm_g, kv_norm_g=v_kv_norm_g,
              w_uq=v_w_uq, w_ukv=v_w_ukv, w_branch=v_w_branch, w_out=v_w_out, ln1_g=v_ln1_g, ln1_b=v_ln1_b, w_ffn_up=v_w_ffn_up,
              conv_w=v_conv_w, conv_b=v_conv_b, w_ffn_down=v_w_ffn_down, ln2_g=v_ln2_g, ln2_b=v_ln2_b)

    core = lax.axis_index("c")
    chip = 2 * lax.axis_index("x") + lax.axis_index("y")

    names = [name for name, _, _ in MATS]
    sent = [_weight_send(name, wts[name]) for name in names]
    got = _allgather_weights(sent)
    gathered = {name: lax.dynamic_update_slice(g, s[None], (chip,) + (0,) * s.ndim) for name, g, s in zip(names, got, sent)}
    ws = [_full_weights(gathered, l) for l in range(DEPTH)]

    small = {k: wts[k] for k in SMALL}
    loss_part, grad_x, gws, gsmall, g_rel = _local_step(x[0], loss_target[0], ws, rel_table, small)

    rnames = [name for name, _, _ in REDUCED]
    gsend = [jnp.stack([_grad_send(name, gws[l][name], shape, ax) for l in range(DEPTH)]) for name, shape, ax in REDUCED]
    theirs = _sibling_swap(gsend)
    pairs = [_pair_add(g, t_, core, name="grad_pair_" + name) for name, g, t_ in zip(rnames, gsend, theirs)]
    arrived = _chip_scatter(pairs)
    reduced = [_chip_add(lax.dynamic_index_in_dim(p, chip, 0, keepdims=False), a, name="grad_chip_" + name)
               for name, p, a in zip(rnames, pairs, arrived)]
    others = _sibling_share(reduced)
    gshard = {}
    for name, mine, other in zip(rnames, reduced, others):
        layers = [jnp.where(core == l, mine, other) for l in range(DEPTH)]
        gshard[name] = jnp.stack([_grad_recv(name, a) for a in layers])

    conv_w_full = jnp.stack([gws[l]["conv_w"] for l in range(DEPTH)])
    small_red = _allreduce_small(_pack_small(g_rel, gsmall, conv_w_full, loss_part))
    g_rel_r, gsmall_r, conv_w_r, loss_vec = _unpack_small(small_red)
    loss = loss_vec[0]
    shard_w = 2 * D_FF // N_CHIP
    gshard["conv_w"] = lax.dynamic_slice_in_dim(conv_w_r, chip * shard_w, shard_w, axis=2)

    grads = dict(gshard)
    grads.update(gsmall_r)
    grads["rel_table"] = g_rel_r
    deltas, new_m, new_v = {}, {}, {}
    for name, _, _ in MATS:
        shp = wts[name].shape
        v2 = lambda a: a.reshape(-1, shp[-1])
        d_, m_, v_ = _adamw(v2(wts[name]), v2(grads[name]), v2(ms[name]), v2(vs[name]), name="adamw_" + name)
        deltas[name], new_m[name], new_v[name] = d_.reshape(shp), m_.reshape(shp), v_.reshape(shp)
    zero, none = jnp.zeros((LANE,), F32), jnp.zeros((0,), F32)
    sw = _pack_small(wts["rel_table"], {k: wts[k] for k in SMALL}, none, zero)
    sm = _pack_small(ms["rel_table"], {k: ms[k] for k in SMALL}, none, zero)
    sv = _pack_small(vs["rel_table"], {k: vs[k] for k in SMALL}, none, zero)
    sg = _pack_small(g_rel_r, gsmall_r, none, zero)
    sd, smn, svn = _adamw(sw, sg, sm, sv, name="adamw_small")
    for res, buf in ((deltas, sd), (new_m, smn), (new_v, svn)):
        rel_, sm_ = _unpack_small(jnp.pad(buf, ((0, small_red.shape[0] - buf.shape[0]), (0, 0))))[:2]
        res["rel_table"] = rel_
        res.update(sm_)

    return (loss, grad_x[None], *[grads[k] for k in WEIGHT_ORDER], *[deltas[k] for k in WEIGHT_ORDER],
            *[new_m[k] for k in WEIGHT_ORDER], *[new_v[k] for k in WEIGHT_ORDER])
```

```python
import math

import jax
import jax.numpy as jnp
from jax import lax
from jax.experimental import pallas as pl
from jax.experimental.pallas import tpu as pltpu

F32 = jnp.float32
BF16 = jnp.bfloat16
MESH = pl.DeviceIdType.MESH

D_MODEL = 1024
DEPTH = 2
HEAD_DIM = 64
N_HEADS = 8
A_DILS = (1, 4, 16)
C_Q_RANK = 256
C_KV_RANK = 128
C_NOPE = 64
C_ROPE = 32
ROPE_BASE = 10000.0
REL_BUCKETS = 32
REL_MAX_DIST = 2048
D_FF = 2816
ALPHA = (2 * DEPTH) ** 0.25
LN_EPS = 1e-5
RMS_EPS = 1e-6
NEG = -1e30
ADAM_LR, ADAM_B1, ADAM_B2, ADAM_EPS, ADAM_WD, ADAM_STEP = 0.001, 0.9, 0.999, 1e-08, 0.01, 10

VMEM_LIMIT_BYTES = 56 * 1024 * 1024
LANE = 128
BLK = 128
TQ = 512
HP = 128
BAND_UNROLL = 16

D_IN = 8864
A_COLS = 3 * N_HEADS * HEAD_DIM
ORIG = {"a": 0, "bq": 4608, "bk": 5120, "bv": 5248, "cq": 5376, "cdkv": 5632, "gate": 5792}
M_GATE, M_A0, M_BQ, M_BK, M_BV, M_CQ, M_CDKV, M_COLS = 0, 3072, 4608, 5120, 5376, 5632, 5888, 6144

N_CHIP = 4
MATS = (
    ("w_in", (D_MODEL, D_IN), 1),
    ("w_uq", (C_Q_RANK, 768), 1),
    ("w_ukv", (C_KV_RANK, 1024), 1),
    ("w_branch", (3, 512, D_MODEL), 2),
    ("w_out", (D_MODEL, D_MODEL), 0),
    ("w_ffn_up", (D_MODEL, 2 * D_FF), 1),
    ("conv_w", (3, 2 * D_FF), 1),
    ("w_ffn_down", (D_FF, D_MODEL), 0),
)
SMALL = ("b_gate", "sinks", "q_norm_g", "kv_norm_g", "ln1_g", "ln1_b", "conv_b", "ln2_g", "ln2_b")
SMALL_SIZES = {"b_gate": 3072, "sinks": 8, "q_norm_g": 256, "kv_norm_g": 128, "ln1_g": 1024, "ln1_b": 1024,
               "conv_b": 5632, "ln2_g": 1024, "ln2_b": 1024}
WEIGHT_ORDER = ("rel_table", "w_in", "b_gate", "sinks", "q_norm_g", "kv_norm_g", "w_uq", "w_ukv", "w_branch",
                "w_out", "ln1_g", "ln1_b", "w_ffn_up", "conv_w", "conv_b", "w_ffn_down", "ln2_g", "ln2_b")


def _cparams(sem):
    return pltpu.CompilerParams(dimension_semantics=sem, vmem_limit_bytes=VMEM_LIMIT_BYTES)


def _shard_shape(shape, ax):
    s = list(shape)
    s[ax] //= N_CHIP
    return tuple(s)


def _ceil_to(n, m):
    return -(-n // m) * m


def _pick(n, target):
    if n <= target:
        return n
    best = None
    for t in range(LANE, target + 1, LANE):
        if n % t == 0:
            best = t
    assert best is not None, (n, target)
    return best


def _mm(a, b, *, tb=False, out_dtype=F32, tm=512, tn=1024, tk=2048, name):
    m, k = a.shape
    n = b.shape[0] if tb else b.shape[1]
    assert (b.shape[1] if tb else b.shape[0]) == k
    tm, tn, tk = _pick(m, tm), _pick(n, tn), _pick(k, tk)
    nk = k // tk
    dn = (((1,), (1,)), ((), ())) if tb else (((1,), (0,)), ((), ()))

    def body(a_ref, b_ref, o_ref, acc_ref):
        part = lax.dot_general(a_ref[...].astype(BF16), b_ref[...].astype(BF16), dn, preferred_element_type=F32)
        if nk == 1:
            o_ref[...] = part.astype(o_ref.dtype)
        else:
            kk = pl.program_id(2)

            @pl.when(kk == 0)
            def _():
                acc_ref[...] = part

            @pl.when(kk > 0)
            def _():
                acc_ref[...] += part

            @pl.when(kk == nk - 1)
            def _():
                o_ref[...] = acc_ref[...].astype(o_ref.dtype)

    b_spec = pl.BlockSpec((tn, tk), lambda i, j, kk: (j, kk)) if tb else pl.BlockSpec((tk, tn), lambda i, j, kk: (kk, j))
    return pl.pallas_call(
        body, name=name, grid=(m // tm, n // tn, nk),
        in_specs=[pl.BlockSpec((tm, tk), lambda i, j, kk: (i, kk)), b_spec],
        out_specs=pl.BlockSpec((tm, tn), lambda i, j, kk: (i, j)),
        out_shape=jax.ShapeDtypeStruct((m, n), out_dtype),
        scratch_shapes=[pltpu.VMEM((tm, tn) if nk > 1 else (8, LANE), F32)],
        compiler_params=_cparams(("parallel", "parallel", "arbitrary")),
    )(a, b)


def _rowwise(fn, rows, *, pars=(), halos=(), outs=(), accs=(), tm, name, ncol=1, t=None):
    nb = rows[0][0].shape[0]
    t = rows[0][0].shape[1] if t is None else t
    tm = min(tm, t)
    assert t % tm == 0 and tm % 8 == 0
    nt = t // tm
    in_specs, args = [], []
    for spec in rows:
        arr, c, off = spec[:3]
        rb = spec[3] if len(spec) > 3 else 0
        in_specs.append(pl.BlockSpec((1, tm, c), lambda b, cc, i, off=off, rb=rb: (b, i + rb, off + cc)))
        args.append(arr)
    for arr, c, off, kind in halos:
        if kind == "prev":
            im = lambda b, cc, i, off=off: (b, jnp.maximum(i * (tm // 8) - 1, 0), off + cc)
        else:
            im = lambda b, cc, i, off=off: (b, jnp.minimum((i + 1) * (tm // 8), t // 8 - 1), off + cc)
        in_specs.append(pl.BlockSpec((1, 8, c), im))
        args.append(arr)
    for arr, c, off in pars:
        bp, r = arr.shape[:2]
        if bp > 1:
            im = lambda b, cc, i, off=off: (b, 0, off + cc)
        else:
            im = lambda b, cc, i, off=off: (0, 0, off + cc)
        in_specs.append(pl.BlockSpec((1, r, c), im))
        args.append(arr)
    out_specs, out_shapes = [], []
    for ctot, c, off, dt in outs:
        out_specs.append(pl.BlockSpec((1, tm, c), lambda b, cc, i, off=off: (b, i, off + cc)))
        out_shapes.append(jax.ShapeDtypeStruct((nb, t, ctot), dt))
    for r, ctot, c, off in accs:
        out_specs.append(pl.BlockSpec((1, r, c), lambda b, cc, i, off=off: (b, 0, off + cc)))
        out_shapes.append(jax.ShapeDtypeStruct((nb, r, ctot), F32))
    n_in, n_out = len(args), len(outs)

    def body(*refs):
        i = pl.program_id(2)
        res = fn(i, nt, *[r[0].astype(F32) for r in refs[:n_in]])
        if not isinstance(res, (tuple, list)):
            res = (res,)
        for o_ref, val in zip(refs[n_in:n_in + n_out], res[:n_out]):
            o_ref[0] = val.astype(o_ref.dtype)
        for a_ref, val in zip(refs[n_in + n_out:], res[n_out:]):
            @pl.when(i == 0)
            def _(a_ref=a_ref, val=val):
                a_ref[0] = val

            @pl.when(i > 0)
            def _(a_ref=a_ref, val=val):
                a_ref[0] += val

    res = pl.pallas_call(
        body, name=name, grid=(nb, ncol, nt), in_specs=in_specs, out_specs=out_specs, out_shape=out_shapes,
        compiler_params=_cparams(("parallel", "parallel", "arbitrary")),
    )(*args)
    return res


def _lanewise(fn, ins, outs, *, tl, name):
    nb, _, t = ins[0].shape
    tl = min(tl, t)
    assert t % tl == 0
    n_in = len(ins)

    def body(*refs):
        res = fn(*[r[0] for r in refs[:n_in]])
        if not isinstance(res, (tuple, list)):
            res = (res,)
        for o_ref, val in zip(refs[n_in:], res):
            o_ref[0] = val.astype(o_ref.dtype)

    return pl.pallas_call(
        body, name=name, grid=(nb, t // tl),
        in_specs=[pl.BlockSpec((1, a.shape[1], tl), lambda b, i: (b, 0, i)) for a in ins],
        out_specs=[pl.BlockSpec((1, r, tl), lambda b, i: (b, 0, i)) for r, _ in outs],
        out_shape=[jax.ShapeDtypeStruct((nb, r, t), dt) for r, dt in outs],
        compiler_params=_cparams(("parallel", "parallel")),
    )(*ins)


def _dot(a, b):
    return lax.dot_general(a, b, (((1,), (0,)), ((), ())), preferred_element_type=F32)


def _dot_nt(a, b):
    return lax.dot_general(a, b, (((1,), (1,)), ((), ())), preferred_element_type=F32)


def _dot_tn(a, b):
    return lax.dot_general(a, b, (((0,), (0,)), ((), ())), preferred_element_type=F32)


BAND_ROWS = BAND_UNROLL * BLK


def _rows(parts):
    return jnp.concatenate(parts, axis=0)


def _lane_lo():
    return lax.broadcasted_iota(jnp.int32, (1, LANE), 1) < HEAD_DIM


def _blocks(a):
    return [a[i * BLK:(i + 1) * BLK] for i in range(BAND_UNROLL)]


def _band_operands(g, k_ref, v_ref):
    start = pl.multiple_of(g * BAND_ROWS, BAND_ROWS)
    pstart = pl.multiple_of(jnp.maximum(g * BAND_ROWS - BLK, 0), BLK)
    out = []
    for ref in (k_ref, v_ref):
        cur = ref[pl.ds(start, BAND_ROWS), :]
        raw = ref[pl.ds(pstart, BAND_ROWS), :]
        shifted = _rows([jnp.zeros((BLK, LANE), raw.dtype), raw[:BAND_ROWS - BLK]])
        out += [_blocks(cur), _blocks(jnp.where(g == 0, shifted, raw))]
    return out


def _band_scores(g, qa, kc, kp, b_ref, a, nb, lim):
    scale = HEAD_DIM ** -0.5
    qi = jnp.bitwise_and(lax.broadcasted_iota(jnp.int32, (BAND_ROWS, BLK), 0), BLK - 1)
    ki = lax.broadcasted_iota(jnp.int32, (BAND_ROWS, BLK), 1)
    tile = lambda blk: _rows([blk] * BAND_UNROLL)
    firsts = []
    for i in range(BAND_UNROLL):
        if nb >= BAND_UNROLL:
            val = jnp.where(lax.rem(g * BAND_UNROLL, nb) == 0, NEG, 0.0).astype(F32) if i == 0 else 0.0
        else:
            val = NEG if i % nb == 0 else 0.0
        firsts.append(jnp.zeros((BLK, 1), F32) + val)
    sc = _rows([_dot_nt(q, k) for q, k in zip(qa, kc)]) * scale + tile(b_ref[a, :, BLK:2 * BLK])
    sp = _rows([_dot_nt(q, k) for q, k in zip(qa, kp)]) * scale + tile(b_ref[a, :, 0:BLK])
    return jnp.where(ki <= qi, sc, NEG), jnp.where((BLK + qi - ki) <= lim, sp, NEG) + _rows(firsts)


def _band_fwd(src, offs, bias, sinks, *, nb, lim, gqa, name):
    t = src.shape[0]
    assert t % BAND_ROWS == 0 and (nb % BAND_UNROLL == 0 or BAND_UNROLL % nb == 0)
    qo, ko, vo = offs
    share = 2 if gqa else 1

    def body(sink_ref, q_ref, k_ref, v_ref, b_ref, o_ref, lse_ref):
        hp, g = pl.program_id(0), pl.program_id(1)
        lo = _lane_lo()
        q2 = q_ref[...]
        kc, kp, vc, vp = _band_operands(g, k_ref, v_ref)
        outs, lses = [], []
        for a in range(2):
            sink = sink_ref[2 * hp + a]
            qa = _blocks(jnp.where(lo if a == 0 else jnp.logical_not(lo), q2, jnp.zeros_like(q2)))
            sc, sp = _band_scores(g, qa, kc, kp, b_ref, a, nb, lim)
            m = jnp.maximum(jnp.maximum(jnp.max(sc, axis=1, keepdims=True), jnp.max(sp, axis=1, keepdims=True)), sink)
            pc, pp = jnp.exp(sc - m), jnp.exp(sp - m)
            l = jnp.sum(pc, axis=1, keepdims=True) + jnp.sum(pp, axis=1, keepdims=True) + jnp.exp(sink - m)
            inv = 1.0 / l
            pc_b, pp_b = _blocks((pc * inv).astype(BF16)), _blocks((pp * inv).astype(BF16))
            outs.append(_rows([_dot(pc_b[i], vc[i]) + _dot(pp_b[i], vp[i]) for i in range(BAND_UNROLL)]))
            lses.append(m + jnp.log(l))
        o_ref[...] = jnp.where(lo, outs[0], outs[1])
        lse_ref[...] = jnp.where(lo, lses[0], lses[1])

    slab = lambda off: pl.BlockSpec((BAND_ROWS, LANE), lambda hp, g, off=off: (g, off + hp))
    whole = lambda off: pl.BlockSpec((t, LANE), lambda hp, g, off=off: (0, off + hp // share))
    return pl.pallas_call(
        body, name=name, grid=(N_HEADS // 2, t // BAND_ROWS),
        in_specs=[pl.BlockSpec(memory_space=pltpu.SMEM), slab(qo), whole(ko), whole(vo),
                  pl.BlockSpec((2, BLK, 2 * BLK), lambda hp, g: (hp, 0, 0))],
        out_specs=[slab(0), slab(0)],
        out_shape=[jax.ShapeDtypeStruct((t, N_HEADS * HEAD_DIM), F32)] * 2,
        compiler_params=_cparams(("parallel", "parallel")),
    )(sinks, src, src, src, bias)


def _band_bwd(src, offs, do, lse, dpr, bias, sinks, *, nb, lim, gqa, name):
    t = src.shape[0]
    qo, ko, vo = offs
    share = 2 if gqa else 1
    nstep = t // BAND_ROWS
    scale = HEAD_DIM ** -0.5

    def fold(a):
        acc = a[0:BLK]
        for i in range(1, BAND_UNROLL):
            acc = acc + a[i * BLK:(i + 1) * BLK]
        return acc

    def body(sink_ref, q_ref, k_ref, v_ref, do_ref, lse_ref, dpr_ref, b_ref,
             dq_ref, dk_ref, dv_ref, ds_ref, dsink_ref, dk_acc, dv_acc):
        hp, g = pl.program_id(0), pl.program_id(1)
        lo = _lane_lo()
        hi = jnp.logical_not(lo)

        @pl.when(jnp.logical_and(g == 0, lax.rem(hp, share) == 0))
        def _():
            dk_acc[...] = jnp.zeros_like(dk_acc)
            dv_acc[...] = jnp.zeros_like(dv_acc)

        @pl.when(g == 0)
        def _():
            ds_ref[...] = jnp.zeros_like(ds_ref)
            dsink_ref[...] = jnp.zeros_like(dsink_ref)

        q2, do2, lse2, dpr2 = q_ref[...], do_ref[...], lse_ref[...], dpr_ref[...]
        lse_sw, dpr_sw = pltpu.roll(lse2, HEAD_DIM, axis=1), pltpu.roll(dpr2, HEAD_DIM, axis=1)
        kc, kp, vc, vp = _band_operands(g, k_ref, v_ref)
        dqs, dk_cur, dk_prev, dv_cur, dv_prev = [], None, None, None, None
        for a in range(2):
            sink = sink_ref[2 * hp + a]
            mine = lo if a == 0 else hi
            qa = _blocks(jnp.where(mine, q2, jnp.zeros_like(q2)))
            doa = _blocks(jnp.where(mine, do2, jnp.zeros_like(do2)))
            lse_a, dpr_a = jnp.where(mine, lse2, lse_sw), jnp.where(mine, dpr2, dpr_sw)
            sc, sp = _band_scores(g, qa, kc, kp, b_ref, a, nb, lim)
            pc, pp = jnp.exp(sc - lse_a), jnp.exp(sp - lse_a)
            dsc = pc * (_rows([_dot_nt(d, v) for d, v in zip(doa, vc)]) - dpr_a)
            dsp = pp * (_rows([_dot_nt(d, v) for d, v in zip(doa, vp)]) - dpr_a)
            ds_ref[a, :, BLK:2 * BLK] += fold(dsc)
            ds_ref[a, :, 0:BLK] += fold(dsp)
            dsink_ref[a] -= jnp.sum(jnp.exp(sink - lse_a) * dpr_a, axis=0, keepdims=True)
            dsc_b, dsp_b = _blocks((dsc * scale).astype(BF16)), _blocks((dsp * scale).astype(BF16))
            pc_b, pp_b = _blocks(pc.astype(BF16)), _blocks(pp.astype(BF16))
            dqs.append(_rows([_dot(dsc_b[i], kc[i]) + _dot(dsp_b[i], kp[i]) for i in range(BAND_UNROLL)]))
            parts = [_rows([_dot_tn(x[i], y[i]) for i in range(BAND_UNROLL)])
                     for x, y in ((dsc_b, qa), (dsp_b, qa), (pc_b, doa), (pp_b, doa))]
            if a == 0:
                dk_cur, dk_prev, dv_cur, dv_prev = parts
            else:
                dk_cur, dk_prev, dv_cur, dv_prev = dk_cur + parts[0], dk_prev + parts[1], dv_cur + parts[2], dv_prev + parts[3]
        dq_ref[...] = jnp.where(lo, dqs[0], dqs[1]).astype(dq_ref.dtype)
        start = pl.multiple_of(g * BAND_ROWS, BAND_ROWS)
        after = pl.multiple_of(g * BAND_ROWS + BLK, BLK)
        dk_acc[pl.ds(after, BAND_ROWS), :] += dk_cur
        dk_acc[pl.ds(start, BAND_ROWS), :] += dk_prev
        dv_acc[pl.ds(after, BAND_ROWS), :] += dv_cur
        dv_acc[pl.ds(start, BAND_ROWS), :] += dv_prev

        @pl.when(g == nstep - 1)
        def _():
            dk_ref[...] = dk_acc[BLK:, :].astype(dk_ref.dtype)
            dv_ref[...] = dv_acc[BLK:, :].astype(dv_ref.dtype)

    slab = lambda off: pl.BlockSpec((BAND_ROWS, LANE), lambda hp, g, off=off: (g, off + hp))
    whole = lambda off: pl.BlockSpec((t, LANE), lambda hp, g, off=off: (0, off + hp // share))
    per_pair = lambda shp: pl.BlockSpec((2,) + shp, lambda hp, g: (hp,) + (0,) * len(shp))
    kv_cols = N_HEADS * HEAD_DIM // share
    return pl.pallas_call(
        body, name=name, grid=(N_HEADS // 2, nstep),
        in_specs=[pl.BlockSpec(memory_space=pltpu.SMEM), slab(qo), whole(ko), whole(vo), slab(0), slab(0), slab(0),
                  per_pair((BLK, 2 * BLK))],
        out_specs=[slab(0), whole(0), whole(0), per_pair((BLK, 2 * BLK)), per_pair((1, LANE))],
        out_shape=[jax.ShapeDtypeStruct((t, N_HEADS * HEAD_DIM), BF16), jax.ShapeDtypeStruct((t, kv_cols), BF16),
                   jax.ShapeDtypeStruct((t, kv_cols), BF16), jax.ShapeDtypeStruct((N_HEADS, BLK, 2 * BLK), F32),
                   jax.ShapeDtypeStruct((N_HEADS, 1, LANE), F32)],
        scratch_shapes=[pltpu.VMEM((t + BLK, LANE), F32), pltpu.VMEM((t + BLK, LANE), F32)],
        compiler_params=_cparams(("arbitrary", "arbitrary")),
    )(sinks, src, src, src, do, lse, dpr, bias)


MLA_V_OFF = N_HEADS


def _diag_mask(keys_on_rows=False):
    rows, cols = lax.broadcasted_iota(jnp.int32, (TQ, TQ), 0), lax.broadcasted_iota(jnp.int32, (TQ, TQ), 1)
    return rows <= cols if keys_on_rows else cols <= rows


def _mla_specs(t):
    blk = lambda f: pl.BlockSpec((TQ, LANE), lambda hp, qi, f=f: (qi, f(hp)))
    whole = lambda f: pl.BlockSpec((t, LANE), lambda hp, qi, f=f: (0, f(hp)))
    return blk, whole


def _mla_fwd(q, k, vt, *, name):
    t = q.shape[0]
    n = t // TQ
    scale = (C_NOPE + C_ROPE) ** -0.5

    def body(q0_ref, q1_ref, k0_ref, k1_ref, vt_ref, o_ref, lse_ref, m_ref, l_ref, acc_ref):
        qi = pl.program_id(1)
        qs, ks = (q0_ref[...], q1_ref[...]), (k0_ref, k1_ref)
        m_ref[...] = jnp.full_like(m_ref, NEG)
        l_ref[...] = jnp.zeros_like(l_ref)
        acc_ref[...] = jnp.zeros_like(acc_ref)

        def step(kj, diagonal):
            rows = pl.ds(pl.multiple_of(kj * TQ, TQ), TQ)
            vtb = vt_ref[0, kj]
            for a in range(2):
                s = _dot_nt(ks[a][rows, :], qs[a]) * scale
                if diagonal:
                    s = jnp.where(_diag_mask(keys_on_rows=True), s, NEG)
                m_prev = m_ref[a]
                m_new = jnp.maximum(m_prev, jnp.max(s, axis=0, keepdims=True))
                alpha = jnp.exp(m_prev - m_new)
                p = jnp.exp(s - m_new)
                l_ref[a] = alpha * l_ref[a] + jnp.sum(p, axis=0, keepdims=True)
                acc_ref[a] = alpha * acc_ref[a] + _dot(vtb, p.astype(BF16))
                m_ref[a] = m_new

        def kloop(kj, c2):
            step(kj, False)
            return c2

        lax.fori_loop(0, qi, kloop, 0)
        step(qi, True)
        first = lax.broadcasted_iota(jnp.int32, (LANE, 1), 0) < HEAD_DIM
        ot = jnp.where(first, acc_ref[0] * (1.0 / l_ref[0]), acc_ref[1] * (1.0 / l_ref[1]))
        lset = jnp.where(first, m_ref[0] + jnp.log(l_ref[0]), m_ref[1] + jnp.log(l_ref[1]))
        o_ref[...] = ot.T
        lse_ref[...] = lset.T

    blk, whole = _mla_specs(t)
    return pl.pallas_call(
        body, name=name, grid=(N_HEADS // 2, n),
        in_specs=[blk(lambda hp: 2 * hp), blk(lambda hp: 2 * hp + 1), whole(lambda hp: 2 * hp), whole(lambda hp: 2 * hp + 1),
                  pl.BlockSpec((1, n, LANE, TQ), lambda hp, qi: (hp, 0, 0, 0))],
        out_specs=[blk(lambda hp: hp), blk(lambda hp: hp)],
        out_shape=[jax.ShapeDtypeStruct((t, N_HEADS * HEAD_DIM), F32)] * 2,
        scratch_shapes=[pltpu.VMEM((2, 1, TQ), F32), pltpu.VMEM((2, 1, TQ), F32), pltpu.VMEM((2, LANE, TQ), F32)],
        compiler_params=_cparams(("parallel", "parallel")),
    )(q, q, k, k, vt)


def _mla_bwd(q, k, kv, do, lse, delta, *, name):
    t = q.shape[0]
    n = t // TQ
    scale = (C_NOPE + C_ROPE) ** -0.5

    def body(q0_ref, q1_ref, k0_ref, k1_ref, v_ref, do_ref, lse_ref, dl_ref,
             dq_ref, dk_ref, dv_ref, dq_acc, dk_acc, dv_acc):
        qi = pl.program_id(1)
        lo = _lane_lo()

        @pl.when(qi == 0)
        def _():
            dk_acc[...] = jnp.zeros_like(dk_acc)
            dv_acc[...] = jnp.zeros_like(dv_acc)

        dq_acc[...] = jnp.zeros_like(dq_acc)
        qs, ks = (q0_ref[...], q1_ref[...]), (k0_ref, k1_ref)
        do2, lse2, dl2 = do_ref[...], lse_ref[...], dl_ref[...]
        lse_sw, dl_sw = pltpu.roll(lse2, HEAD_DIM, axis=1), pltpu.roll(dl2, HEAD_DIM, axis=1)
        heads = []
        for a in range(2):
            mine = lo if a == 0 else jnp.logical_not(lo)
            heads.append((jnp.where(mine, do2, jnp.zeros_like(do2)), jnp.where(mine, lse2, lse_sw)[:, 0:1],
                          jnp.where(mine, dl2, dl_sw)[:, 0:1]))

        def step(kj, diagonal):
            rows = pl.ds(pl.multiple_of(kj * TQ, TQ), TQ)
            vb = v_ref[rows, :]
            for a, (doa, lse_a, dl_a) in enumerate(heads):
                kb = ks[a][rows, :]
                s = _dot_nt(qs[a], kb) * scale
                if diagonal:
                    s = jnp.where(_diag_mask(), s, NEG)
                p = jnp.exp(s - lse_a)
                ds = (p * (_dot_nt(doa, vb) - dl_a) * scale).astype(BF16)
                dq_acc[a] += _dot(ds, kb)
                dk_acc[a, rows, :] += _dot_tn(ds, qs[a])
                dv_acc[rows, :] += _dot_tn(p.astype(BF16), doa)

        def kloop(kj, c2):
            step(kj, False)
            return c2

        lax.fori_loop(0, qi, kloop, 0)
        step(qi, True)
        dq_ref[:, 0:LANE] = dq_acc[0].astype(dq_ref.dtype)
        dq_ref[:, LANE:2 * LANE] = dq_acc[1].astype(dq_ref.dtype)

        @pl.when(qi == n - 1)
        def _():
            dk_ref[:, 0:LANE] = dk_acc[0].astype(dk_ref.dtype)
            dk_ref[:, LANE:2 * LANE] = dk_acc[1].astype(dk_ref.dtype)
            dv_ref[...] = dv_acc[...].astype(dv_ref.dtype)

    blk, whole = _mla_specs(t)
    even, odd, pair = (lambda hp: 2 * hp), (lambda hp: 2 * hp + 1), (lambda hp: hp)
    wide = jax.ShapeDtypeStruct((t, N_HEADS * LANE), BF16)
    return pl.pallas_call(
        body, name=name, grid=(N_HEADS // 2, n),
        in_specs=[blk(even), blk(odd), whole(even), whole(odd), whole(lambda hp: MLA_V_OFF + hp), blk(pair), blk(pair), blk(pair)],
        out_specs=[pl.BlockSpec((TQ, 2 * LANE), lambda hp, qi: (qi, hp)), pl.BlockSpec((t, 2 * LANE), lambda hp, qi: (0, hp)), whole(pair)],
        out_shape=[wide, wide, jax.ShapeDtypeStruct((t, N_HEADS * HEAD_DIM), BF16)],
        scratch_shapes=[pltpu.VMEM((2, TQ, LANE), F32), pltpu.VMEM((2, t, LANE), F32), pltpu.VMEM((t, LANE), F32)],
        compiler_params=_cparams(("arbitrary", "arbitrary")),
    )(q, q, k, k, kv, do, lse, delta)


def _bias_lookup(bucket, table_t, *, name):
    nh, npos = bucket.shape
    tp = 4096

    def body(b_ref, t_ref, o_ref):
        bk, tab = b_ref[...], t_ref[...]
        acc = jnp.zeros(bk.shape, F32)
        for i in range(REL_BUCKETS):
            acc = jnp.where(bk == i, tab[:, i:i + 1], acc)
        o_ref[...] = acc

    return pl.pallas_call(
        body, name=name, grid=(npos // tp,),
        in_specs=[pl.BlockSpec((nh, tp), lambda i: (0, i)), pl.BlockSpec((nh, REL_BUCKETS), lambda i: (0, 0))],
        out_specs=pl.BlockSpec((nh, tp), lambda i: (0, i)),
        out_shape=jax.ShapeDtypeStruct((nh, npos), F32),
        compiler_params=_cparams(("parallel",)),
    )(bucket, table_t)


def _bias_grad(bucket, ds0, ds1, *, name):
    nh, npos = bucket.shape
    tp = 4096

    def body(b_ref, a_ref, c_ref, o_ref):
        i = pl.program_id(0)
        bk, ds = b_ref[...], a_ref[...] + c_ref[...]
        lane = lax.broadcasted_iota(jnp.int32, (nh, REL_BUCKETS), 1)
        acc = jnp.zeros((nh, REL_BUCKETS), F32)
        for j in range(REL_BUCKETS):
            col = jnp.sum(jnp.where(bk == j, ds, 0.0), axis=1, keepdims=True)
            acc = acc + jnp.where(lane == j, col, 0.0)

        @pl.when(i == 0)
        def _():
            o_ref[...] = acc

        @pl.when(i > 0)
        def _():
            o_ref[...] += acc

    return pl.pallas_call(
        body, name=name, grid=(npos // tp,),
        in_specs=[pl.BlockSpec((nh, tp), lambda i: (0, i))] * 3,
        out_specs=pl.BlockSpec((nh, REL_BUCKETS), lambda i: (0, 0)),
        out_shape=jax.ShapeDtypeStruct((nh, REL_BUCKETS), F32),
        compiler_params=_cparams(("arbitrary",)),
    )(bucket, ds0, ds1)


def _t5_bucket(dist):
    n = jnp.maximum(dist, 0)
    max_exact = REL_BUCKETS // 2
    scaled = jnp.log(jnp.maximum(n, 1).astype(F32) / max_exact) / math.log(REL_MAX_DIST / max_exact)
    large = max_exact + (scaled * (REL_BUCKETS - max_exact)).astype(jnp.int32)
    return jnp.where(n < max_exact, n, jnp.minimum(large, REL_BUCKETS - 1))


def _bucket_index():
    qi = jnp.arange(BLK)[:, None]
    ci = jnp.arange(2 * BLK)[None, :]
    step = BLK + qi - ci
    per_group = [_t5_bucket(step * d).reshape(1, -1) for d in A_DILS + (1,)]
    return jnp.concatenate([jnp.tile(b, (N_HEADS, 1)) for b in per_group], axis=0).astype(jnp.int32)


def _sigmoid(x):
    return 1.0 / (1.0 + jnp.exp(-x))


def _ln_stats(z):
    mu = jnp.mean(z, axis=-1, keepdims=True)
    zc = z - mu
    var = jnp.mean(zc * zc, axis=-1, keepdims=True)
    return zc * lax.rsqrt(var + LN_EPS)


def _ln_fwd(x, mix, g, b, *, name):
    def fn(i, nt, xv, mv, gv, bv):
        z = ALPHA * xv + mv
        y = _ln_stats(z) * gv + bv
        return y, y, z

    c = x.shape[-1]
    y, yb, z = _rowwise(fn, [(x[None], c, 0), (mix[None], c, 0)], pars=[(g.reshape(1, 1, c), c, 0), (b.reshape(1, 1, c), c, 0)],
                        outs=[(c, c, 0, F32), (c, c, 0, BF16), (c, c, 0, F32)], tm=512, name=name)
    return y[0], yb[0], z[0]


def _ln_bwd(z, g, dys, coefs, *, name):
    n = len(dys)

    def fn(i, nt, zv, *rest):
        gv = rest[n]
        dy = coefs[0] * rest[0]
        for cf, t in zip(coefs[1:], rest[1:n]):
            dy = dy + cf * t
        mu = jnp.mean(zv, axis=-1, keepdims=True)
        zc = zv - mu
        r = lax.rsqrt(jnp.mean(zc * zc, axis=-1, keepdims=True) + LN_EPS)
        xh = zc * r
        dxh = dy * gv
        dz = r * (dxh - jnp.mean(dxh, axis=-1, keepdims=True) - xh * jnp.mean(dxh * xh, axis=-1, keepdims=True))
        return dz, dz, jnp.sum(dy * xh, axis=0, keepdims=True), jnp.sum(dy, axis=0, keepdims=True)

    c = z.shape[-1]
    dz, dzb, dg, db = _rowwise(fn, [(z[None], c, 0)] + [(d[None], c, 0) for d in dys], pars=[(g.reshape(1, 1, c), c, 0)],
                               outs=[(c, c, 0, F32), (c, c, 0, BF16)], accs=[(1, c, c, 0), (1, c, c, 0)], tm=512, name=name)
    return dz[0], dzb[0], dg.reshape(c), db.reshape(c)


def _rms_fwd(src, c, off, g, *, name):
    def fn(i, nt, xv, gv):
        return xv * lax.rsqrt(jnp.mean(xv * xv, axis=-1, keepdims=True) + RMS_EPS) * gv

    return _rowwise(fn, [(src[None], c, off)], pars=[(g.reshape(1, 1, c), c, 0)], outs=[(c, c, 0, BF16)], tm=1024, name=name)[0][0]


def _rms_bwd(src, c, off, g, dy, *, name):
    def fn(i, nt, xv, dyv, gv):
        r = lax.rsqrt(jnp.mean(xv * xv, axis=-1, keepdims=True) + RMS_EPS)
        gd = gv * dyv
        dx = gd * r - xv * (r * r * r) * jnp.mean(gd * xv, axis=-1, keepdims=True)
        return dx, jnp.sum(dyv * xv * r, axis=0, keepdims=True)

    dx, dg = _rowwise(fn, [(src[None], c, off), (dy[None], c, 0)], pars=[(g.reshape(1, 1, c), c, 0)],
                      outs=[(c, c, 0, BF16)], accs=[(1, c, c, 0)], tm=1024, name=name)
    return dx[0], dg.reshape(c)


def _rope_slabs(x, n_slab, c, s, *, add=None, to_front=False, name):
    half = C_ROPE // 2

    def fn(i, nt, xv, cv, sv, *rest):
        lane = lax.broadcasted_iota(jnp.int32, (1, LANE), 1)
        extra = pltpu.roll(rest[0], C_NOPE, axis=1) if rest else None
        outs = []
        for h in range(n_slab):
            xs = xv[:, h * LANE:(h + 1) * LANE]
            if extra is not None:
                xs = xs + extra
            swapped = jnp.where(lane < C_NOPE + half, pltpu.roll(xs, LANE - half, axis=1), pltpu.roll(xs, half, axis=1))
            y = xs * cv + swapped * sv
            if to_front:
                y = jnp.where(lane < C_ROPE, pltpu.roll(y, LANE - C_NOPE, axis=1), 0.0)
            outs.append(y)
        return jnp.concatenate(outs, axis=1) if n_slab > 1 else outs[0]

    w = n_slab * LANE
    rows = [(x[None], w, 0), (c[None], LANE, 0), (s[None], LANE, 0)]
    if add is not None:
        rows.append((add[0][None], LANE, add[1]))
    return _rowwise(fn, rows, outs=[(w, w, 0, BF16)], tm=512, name=name)[0][0]


def _merge_fwd(proj, b_gate, ys, *, name):
    def fn(i, nt, g0, g1, g2, ya, yb, yc, bg):
        return (_sigmoid(g0 + bg[:, 0:1024]) * ya + _sigmoid(g1 + bg[:, 1024:2048]) * yb
                + _sigmoid(g2 + bg[:, 2048:3072]) * yc)

    rows = [(proj[None], 1024, j) for j in range(3)] + [(y[None], 1024, 0) for y in ys]
    return _rowwise(fn, rows, pars=[(b_gate.reshape(1, 1, 3072), 3072, 0)], outs=[(1024, 1024, 0, BF16)], tm=512, name=name)[0][0]


def _merge_bwd(proj, b_gate, ys, dm, *, name):
    def fn(i, nt, g0, g1, g2, ya, yb, yc, dmv, bg):
        outs, dgs = [], []
        for j, (gp, y) in enumerate(((g0, ya), (g1, yb), (g2, yc))):
            s = _sigmoid(gp + bg[:, j * 1024:(j + 1) * 1024])
            outs.append(s * dmv)
            dgs.append(dmv * y * s * (1.0 - s))
        return outs + dgs + [jnp.sum(d, axis=0, keepdims=True) for d in dgs]

    rows = [(proj[None], 1024, j) for j in range(3)] + [(y[None], 1024, 0) for y in ys] + [(dm[None], 1024, 0)]
    res = _rowwise(fn, rows, pars=[(b_gate.reshape(1, 1, 3072), 3072, 0)], outs=[(1024, 1024, 0, BF16)] * 6,
                   accs=[(1, 1024, 1024, 0)] * 3, tm=256, name=name)
    dys = [r[0] for r in res[0:3]]
    dgp = [r[0] for r in res[3:6]]
    dbg = jnp.concatenate([r.reshape(1024) for r in res[6:9]])
    return dys, dgp, dbg


def _shift_down(u, halo, i, k):
    ext = jnp.concatenate([jnp.where(i > 0, halo, 0.0), u], axis=0)
    return pltpu.roll(ext, k, axis=0)[8:]


def _shift_up(u, halo, i, nt, k):
    ext = jnp.concatenate([u, jnp.where(i < nt - 1, halo, 0.0)], axis=0)
    n = ext.shape[0]
    return pltpu.roll(ext, n - k, axis=0)[:n - 8]


GLU_C = D_FF // 2


def _conv(u, halo, i, w, b):
    return w[0:1] * _shift_down(u, halo, i, 2) + w[1:2] * _shift_down(u, halo, i, 1) + w[2:3] * u + b


def _glu_fwd(ug, uv, conv_w, conv_b, *, name):
    def fn(i, nt, g, v, hg, hv, wg, wv, bg, bv):
        cg, cv = _conv(g, hg, i, wg, bg), _conv(v, hv, i, wv, bv)
        return cg * _sigmoid(cg) * cv

    w3, b3 = conv_w[None], conv_b.reshape(1, 1, -1)
    c = GLU_C
    return _rowwise(fn, [(ug[None], c, 0), (uv[None], c, 0)], halos=[(ug[None], c, 0, "prev"), (uv[None], c, 0, "prev")],
                    pars=[(w3, c, 0), (w3, c, 2), (b3, c, 0), (b3, c, 2)], outs=[(D_FF, c, 0, BF16)], tm=256, ncol=2, name=name)[0][0]


def _glu_bwd_a(ug, uv, conv_w, conv_b, dh, *, name):
    def fn(i, nt, g, v, dhv, hg, hv, wg, wv, bg, bv):
        g1, g2 = _shift_down(g, hg, i, 1), _shift_down(g, hg, i, 2)
        v1, v2 = _shift_down(v, hv, i, 1), _shift_down(v, hv, i, 2)
        cg = wg[0:1] * g2 + wg[1:2] * g1 + wg[2:3] * g + bg
        cv = wv[0:1] * v2 + wv[1:2] * v1 + wv[2:3] * v + bv
        s = _sigmoid(cg)
        dcv = dhv * cg * s
        dcg = dhv * cv * (s * (1.0 + cg * (1.0 - s)))
        red = lambda a: jnp.sum(a, axis=0, keepdims=True)
        return (dcg, dcv, red(dcg), red(dcv), red(dcg * g2), red(dcg * g1), red(dcg * g),
                red(dcv * v2), red(dcv * v1), red(dcv * v))

    w3, b3 = conv_w[None], conv_b.reshape(1, 1, -1)
    c = GLU_C
    res = _rowwise(fn, [(ug[None], c, 0), (uv[None], c, 0), (dh[None], c, 0)],
                   halos=[(ug[None], c, 0, "prev"), (uv[None], c, 0, "prev")],
                   pars=[(w3, c, 0), (w3, c, 2), (b3, c, 0), (b3, c, 2)],
                   outs=[(D_FF, c, 0, F32), (D_FF, c, 0, F32)], accs=[(1, D_FF, c, 0)] * 8, tm=256, ncol=2, name=name)
    dcg, dcv = res[0][0], res[1][0]
    dconv_b = jnp.concatenate([res[2].reshape(D_FF), res[3].reshape(D_FF)])
    dconv_w = jnp.concatenate([jnp.concatenate([res[4 + j].reshape(1, D_FF) for j in range(3)], axis=0),
                               jnp.concatenate([res[7 + j].reshape(1, D_FF) for j in range(3)], axis=0)], axis=1)
    return dcg, dcv, dconv_w, dconv_b


def _glu_bwd_b(dc, conv_w, half, *, name):
    def fn(i, nt, d, hd, w):
        return w[2:3] * d + w[1:2] * _shift_up(d, hd, i, nt, 1) + w[0:1] * _shift_up(d, hd, i, nt, 2)

    c = GLU_C
    return _rowwise(fn, [(dc[None], c, 0)], halos=[(dc[None], c, 0, "next")], pars=[(conv_w[None], c, 2 * half)],
                    outs=[(D_FF, c, 0, BF16)], tm=256, ncol=2, name=name)[0][0]


def _loss_and_grad(y, tgt, *, name):
    def fn(i, nt, yv, tv):
        err = yv - tv
        part = jnp.sum(jnp.sum(err * err, axis=0, keepdims=True), axis=1, keepdims=True) * (0.5 / D_MODEL)
        return err * (1.0 / D_MODEL), jnp.zeros((1, LANE), F32) + part

    dy, part = _rowwise(fn, [(y[None], D_MODEL, 0), (tgt[None], D_MODEL, 0)], outs=[(D_MODEL, D_MODEL, 0, F32)],
                        accs=[(1, LANE, LANE, 0)], tm=512, name=name)
    return dy[0], part.reshape(LANE)


def _lincomb(terms, coefs, *, name):
    def fn(i, nt, *vs):
        acc = coefs[0] * vs[0]
        for cf, v in zip(coefs[1:], vs[1:]):
            acc = acc + cf * v
        return acc

    c = terms[0].shape[-1]
    return _rowwise(fn, [(a[None], c, 0) for a in terms], outs=[(c, c, 0, F32)], tm=512, name=name)[0][0]


def _sum_rows(terms, *, tm, name, dtype=F32):
    def fn(i, nt, *vs):
        acc = vs[0]
        for v in vs[1:]:
            acc = acc + v
        return acc

    c = terms[0].shape[-1]
    return _rowwise(fn, [(t, c, 0) for t in terms], outs=[(c, c, 0, dtype)], tm=tm, name=name)[0]


def _head_sums(x):
    lo = _lane_lo()
    parts = []
    for j in range(x.shape[1] // LANE):
        blk = x[:, j * LANE:(j + 1) * LANE]
        s_lo = jnp.sum(jnp.where(lo, blk, 0.0), axis=1, keepdims=True)
        s_hi = jnp.sum(jnp.where(lo, 0.0, blk), axis=1, keepdims=True)
        parts.append(jnp.where(lo, s_lo, s_hi))
    return jnp.concatenate(parts, axis=1)


def _group_weights(l0, l1, l2):
    m = jnp.maximum(jnp.maximum(l0, l1), l2)
    es = [jnp.exp(l - m) for l in (l0, l1, l2)]
    inv = 1.0 / (es[0] + es[1] + es[2])
    return [e * inv for e in es]


def _combine_fwd(os_, lses, *, name):
    def fn(i, nt, o0, o1, o2, l0, l1, l2):
        w = _group_weights(l0, l1, l2)
        return w[0] * o0 + w[1] * o1 + w[2] * o2

    c = os_[0].shape[-1]
    return _rowwise(fn, [(a[None], c, 0) for a in list(os_) + list(lses)], outs=[(c, c, 0, BF16)], tm=512, name=name)[0][0]


def _combine_bwd(os_, lses, do_a, *, name):
    def fn(i, nt, o0, o1, o2, l0, l1, l2, da):
        ws = _group_weights(l0, l1, l2)
        dws = [_head_sums(da * o) for o in (o0, o1, o2)]
        mean = ws[0] * dws[0] + ws[1] * dws[1] + ws[2] * dws[2]
        return [w * da for w in ws] + [w * mean for w in ws]

    c = do_a.shape[-1]
    res = _rowwise(fn, [(a[None], c, 0) for a in list(os_) + list(lses) + [do_a]], outs=[(c, c, 0, BF16)] * 3 + [(c, c, 0, F32)] * 3,
                   tm=256, name=name)
    return [r[0] for r in res[0:3]], [r[0] for r in res[3:6]]


def _delta(do, o, *, name):
    def fn(i, nt, d, ov):
        return d, _head_sums(d * ov)

    c = do.shape[-1]
    res = _rowwise(fn, [(do[None], c, 0), (o[None], c, 0)], outs=[(c, c, 0, BF16), (c, c, 0, F32)], tm=512, name=name)
    return res[0][0], res[1][0]


def _rowdot(at, bt, *, name):
    def fn(a, b):
        return jnp.sum(a * b, axis=0, keepdims=True)

    return _lanewise(fn, [at, bt], [(1, F32)], tl=2048, name=name)[0]


def _adamw(w, g, m, v, *, name):
    c1 = 1.0 - ADAM_B1 ** ADAM_STEP
    c2 = 1.0 - ADAM_B2 ** ADAM_STEP

    def fn(i, nt, wv, gv, mv, vv):
        mn = ADAM_B1 * mv + (1.0 - ADAM_B1) * gv
        vn = ADAM_B2 * vv + (1.0 - ADAM_B2) * (gv * gv)
        delta = -ADAM_LR * ((mn / c1) / (jnp.sqrt(vn / c2) + ADAM_EPS) + ADAM_WD * wv)
        return delta, mn, vn

    r, c = w.shape
    rp = _ceil_to(r, 8)
    pad = lambda a: jnp.pad(a, ((0, rp - r), (0, 0))) if rp != r else a
    tm = rp
    for cand in (128, 64, 32, 16, 8):
        if rp % cand == 0:
            tm = cand
            break
    res = _rowwise(fn, [(pad(a)[None], c, 0) for a in (w, g, m, v)], outs=[(c, c, 0, F32)] * 3, tm=tm, name=name)
    return [x[0][:r] for x in res]


ANY = pl.BlockSpec(memory_space=pl.ANY)


def _place():
    x, y, c = lax.axis_index("x"), lax.axis_index("y"), lax.axis_index("c")
    chips = [(1 - x, y), (x, 1 - y), (1 - x, 1 - y)]
    return x, y, c, chips


def _allgather_weights(arrs):
    n = len(arrs)

    def body(*refs):
        ins, outs, send_sems, recv_sems = refs[:n], refs[n:2 * n], refs[2 * n], refs[2 * n + 1]
        x, y, c, chips = _place()
        j = 2 * x + y

        def cp(i, k, src, chip_idx, half, to):
            return pltpu.make_async_remote_copy(src_ref=src, dst_ref=outs[i].at[chip_idx, half], send_sem=send_sems.at[k],
                                                recv_sem=recv_sems.at[k], device_id=to, device_id_type=MESH)

        first, passed = [], []
        for i in range(n):
            for r, (cx, cy) in enumerate(chips):
                first.append(cp(i, 3 * i + r, ins[i].at[c], j, c, (cx, cy, c)))
                passed.append(cp(i, 3 * (n + i) + r, outs[i].at[2 * cx + cy, c], 2 * cx + cy, c, (x, y, 1 - c)))
        for d in first:
            d.start()
        for i in range(n):
            for r, (cx, cy) in enumerate(chips):
                cp(i, 3 * i + r, ins[i].at[c], 2 * cx + cy, c, (x, y, c)).wait_recv()
                passed[3 * i + r].start()
        for i in range(n):
            for r, (cx, cy) in enumerate(chips):
                cp(i, 3 * (n + i) + r, ins[i].at[c], 2 * cx + cy, 1 - c, (x, y, c)).wait_recv()
        for d in first + passed:
            d.wait_send()

    return pl.pallas_call(
        body, name="allgather_weights", in_specs=[ANY] * n, out_specs=[ANY] * n,
        out_shape=[jax.ShapeDtypeStruct((N_CHIP,) + a.shape, a.dtype) for a in arrs],
        scratch_shapes=[pltpu.SemaphoreType.DMA((6 * n,)), pltpu.SemaphoreType.DMA((6 * n,))],
    )(*arrs)


def _sibling_swap(gs):
    n = len(gs)

    def body(*refs):
        ins, outs, send_sems, recv_sems = refs[:n], refs[n:2 * n], refs[2 * n], refs[2 * n + 1]
        x, y, c, _ = _place()
        cps = [pltpu.make_async_remote_copy(src_ref=ins[i].at[1 - c], dst_ref=outs[i], send_sem=send_sems.at[i],
                                            recv_sem=recv_sems.at[i], device_id=(x, y, 1 - c), device_id_type=MESH)
               for i in range(n)]
        for d in cps:
            d.start()
        for d in cps:
            d.wait_recv()
        for d in cps:
            d.wait_send()

    return pl.pallas_call(
        body, name="grad_sibling_swap", in_specs=[ANY] * n, out_specs=[ANY] * n,
        out_shape=[jax.ShapeDtypeStruct(g.shape[1:], g.dtype) for g in gs],
        scratch_shapes=[pltpu.SemaphoreType.DMA((n,)), pltpu.SemaphoreType.DMA((n,))],
    )(*gs)


def _chip_scatter(ps):
    n = len(ps)

    def body(*refs):
        ins, outs, send_sems, recv_sems = refs[:n], refs[n:2 * n], refs[2 * n], refs[2 * n + 1]
        x, y, c, chips = _place()
        sends = []
        for i in range(n):
            for r, (cx, cy) in enumerate(chips):
                sends.append(pltpu.make_async_remote_copy(src_ref=ins[i].at[2 * cx + cy], dst_ref=outs[i].at[r], send_sem=send_sems.at[3 * i + r],
                                                          recv_sem=recv_sems.at[3 * i + r], device_id=(cx, cy, c), device_id_type=MESH))
        for d in sends:
            d.start()
        for d in sends:
            d.wait_recv()
        for d in sends:
            d.wait_send()

    return pl.pallas_call(
        body, name="grad_chip_scatter", in_specs=[ANY] * n, out_specs=[ANY] * n,
        out_shape=[jax.ShapeDtypeStruct((3,) + p.shape[1:], p.dtype) for p in ps],
        scratch_shapes=[pltpu.SemaphoreType.DMA((3 * n,)), pltpu.SemaphoreType.DMA((3 * n,))],
    )(*ps)


def _sibling_share(rs):
    n = len(rs)

    def body(*refs):
        ins, outs, send_sems, recv_sems = refs[:n], refs[n:2 * n], refs[2 * n], refs[2 * n + 1]
        x, y, c, _ = _place()
        cps = [pltpu.make_async_remote_copy(src_ref=ins[i], dst_ref=outs[i], send_sem=send_sems.at[i], recv_sem=recv_sems.at[i],
                                            device_id=(x, y, 1 - c), device_id_type=MESH) for i in range(n)]
        for d in cps:
            d.start()
        for d in cps:
            d.wait_recv()
        for d in cps:
            d.wait_send()

    return pl.pallas_call(
        body, name="grad_sibling_share", in_specs=[ANY] * n, out_specs=[ANY] * n,
        out_shape=[jax.ShapeDtypeStruct(r.shape, r.dtype) for r in rs],
        scratch_shapes=[pltpu.SemaphoreType.DMA((n,)), pltpu.SemaphoreType.DMA((n,))],
    )(*rs)


def _allreduce_small(s):
    rows, w = s.shape
    n_dev = 8

    def body(s_ref, out_ref, slots, send_sems, recv_sems):
        x, y, c, _ = _place()
        me = 4 * x + 2 * y + c
        slots[me] = s_ref[...]
        peers = []
        for r in range(1, n_dev):
            px = 1 - x if r & 4 else x
            py = 1 - y if r & 2 else y
            pc = 1 - c if r & 1 else c
            peers.append((px, py, pc))
        sends = [pltpu.make_async_remote_copy(src_ref=s_ref, dst_ref=slots.at[me], send_sem=send_sems.at[r], recv_sem=recv_sems.at[r],
                                              device_id=peer, device_id_type=MESH) for r, peer in enumerate(peers)]
        for d in sends:
            d.start()
        for r, (px, py, pc) in enumerate(peers):
            pltpu.make_async_remote_copy(src_ref=s_ref, dst_ref=slots.at[4 * px + 2 * py + pc], send_sem=send_sems.at[r],
                                         recv_sem=recv_sems.at[r], device_id=(x, y, c), device_id_type=MESH).wait_recv()
        for d in sends:
            d.wait_send()
        acc = slots[0]
        for k in range(1, n_dev):
            acc = acc + slots[k]
        out_ref[...] = acc

    vm = pl.BlockSpec(memory_space=pltpu.VMEM)
    return pl.pallas_call(
        body, name="allreduce_small", in_specs=[vm], out_specs=vm, out_shape=jax.ShapeDtypeStruct((rows, w), F32),
        scratch_shapes=[pltpu.VMEM((n_dev, rows, w), F32), pltpu.SemaphoreType.DMA((n_dev - 1,)), pltpu.SemaphoreType.DMA((n_dev - 1,))],
    )(s)


W_IN_SHARD = D_IN // N_CHIP
W_IN_ROWS_G = 2304
REDUCED = tuple(m for m in MATS if m[0] != "conv_w")
CONV_W_SIZE = 3 * 2 * D_FF


def _weight_send(name, a):
    if name == "w_in":
        return jnp.swapaxes(a, 1, 2).astype(BF16)
    return a if name == "conv_w" else a.astype(BF16)


def _full_weights(gathered, l):
    g = {k: v[:, l] for k, v in gathered.items()}
    s = g["w_in"].astype(F32).reshape(D_IN, D_MODEL)
    dup = lambda a: jnp.concatenate([a[0:64], a[0:64], a[64:128], a[64:128]], axis=0)
    o = ORIG
    wm_t = jnp.concatenate([s[o["gate"]:], s[o["a"]:o["a"] + A_COLS], s[o["bq"]:o["bk"]], dup(s[o["bk"]:o["bv"]]), dup(s[o["bv"]:o["cq"]]),
                            s[o["cq"]:o["gate"]], jnp.zeros((M_COLS - M_CDKV - (o["gate"] - o["cdkv"]), D_MODEL), F32)], axis=0).astype(BF16)
    wg_t = [s[o["a"] + gi * A_COLS:o["a"] + (gi + 1) * A_COLS].astype(BF16) for gi in (1, 2)]
    full = {name: jnp.moveaxis(g[name], 0, ax).reshape(shape) for name, shape, ax in MATS if name != "w_in"}
    uq = full["w_uq"].reshape(C_Q_RANK, N_HEADS, C_NOPE + C_ROPE)
    ukv = full["w_ukv"].reshape(C_KV_RANK, N_HEADS, 2 * C_NOPE)
    w_uq_p = _pad_lanes(uq).reshape(C_Q_RANK, N_HEADS * LANE)
    w_ukv_p = jnp.concatenate([_pad_lanes(ukv[:, :, :C_NOPE]).reshape(C_KV_RANK, N_HEADS * LANE),
                               ukv[:, :, C_NOPE:].reshape(C_KV_RANK, N_HEADS * HEAD_DIM)], axis=1)
    return {"wm_t": wm_t, "wg_t": wg_t, "w_uq_p": w_uq_p, "w_ukv_p": w_ukv_p, "w_branch": full["w_branch"], "w_out": full["w_out"],
            "wup_g": full["w_ffn_up"][:, :D_FF], "wup_v": full["w_ffn_up"][:, D_FF:], "conv_w": full["conv_w"],
            "w_ffn_down": full["w_ffn_down"]}


def _grad_send(name, g, shape, ax):
    if name == "w_in":
        return jnp.pad(g.reshape(N_CHIP, W_IN_SHARD, D_MODEL), ((0, 0), (0, W_IN_ROWS_G - W_IN_SHARD), (0, 0)))
    split = shape[:ax] + (N_CHIP, shape[ax] // N_CHIP) + shape[ax + 1:]
    return jnp.moveaxis(g.reshape(split), ax, 0)


def _grad_recv(name, r):
    return r[:W_IN_SHARD].T if name == "w_in" else r


def _pack_small(rel, small, conv_w, extra):
    parts = [rel.reshape(-1)]
    for l in range(DEPTH):
        for name in SMALL:
            parts.append(small[name][l].reshape(-1))
    parts += [conv_w.reshape(-1), extra]
    flat = jnp.concatenate(parts)
    rows = _ceil_to(-(-flat.shape[0] // LANE), 8)
    return jnp.pad(flat, (0, rows * LANE - flat.shape[0])).reshape(rows, LANE)


def _unpack_small(buf):
    flat = buf.reshape(-1)
    rel = flat[:REL_BUCKETS * 32].reshape(REL_BUCKETS, 32)
    off = REL_BUCKETS * 32
    small = {name: [] for name in SMALL}
    for l in range(DEPTH):
        for name in SMALL:
            n = SMALL_SIZES[name]
            small[name].append(flat[off:off + n])
            off += n
    conv_w = flat[off:off + DEPTH * CONV_W_SIZE].reshape(DEPTH, 3, 2 * D_FF)
    off += DEPTH * CONV_W_SIZE
    return rel, {k: jnp.stack(v) for k, v in small.items()}, conv_w, flat[off:off + LANE]


def _rows2d(a, lead):
    return a.reshape(a.shape[:lead] + (-1, a.shape[-1]))


def _row_tile(rows):
    for cand in (512, 256, 128, 64, 32, 16, 8):
        if rows % cand == 0:
            return cand
    raise ValueError(rows)


def _pair_add(g, got, core, *, name):
    g2, got2 = _rows2d(g, 1), _rows2d(got, 0)
    rows, c = got2.shape
    tm = _row_tile(rows)
    flag = jnp.zeros((1, 1, LANE), F32) + core.astype(F32)

    def fn(i, nt, a0, a1, b, f):
        return jnp.where(f[:, 0:1] == 0.0, a0, a1) + b

    stacked = g2.reshape(1, 2 * rows, c)
    out = _rowwise(fn, [(stacked, c, 0, 0), (stacked, c, 0, rows // tm), (got2[None], c, 0)], pars=[(flag, LANE, 0)],
                   outs=[(c, c, 0, BF16)], tm=tm, t=rows, name=name)[0][0]
    return out.reshape(got.shape)


def _chip_add(own, got, *, name):
    own2, got2 = _rows2d(own, 0), _rows2d(got, 1)
    rows, c = own2.shape
    tm = _row_tile(rows)

    def fn(i, nt, a, b0, b1, b2):
        return ((a.astype(F32) + b0.astype(F32)) + b1.astype(F32)) + b2.astype(F32)

    stacked = got2.reshape(1, 3 * rows, c)
    out = _rowwise(fn, [(own2[None], c, 0)] + [(stacked, c, 0, k * (rows // tm)) for k in range(3)],
                   outs=[(c, c, 0, F32)], tm=tm, t=rows, name=name)[0][0]
    return out.reshape(own.shape)


def _perm(a, d):
    if d == 1:
        return a
    t = a.shape[0]
    return jnp.swapaxes(a.reshape((t // d, d) + a.shape[1:]), 0, 1).reshape(a.shape)


def _unperm(a, d):
    if d == 1:
        return a
    t = a.shape[0]
    return jnp.swapaxes(a.reshape((d, t // d) + a.shape[1:]), 0, 1).reshape(a.shape)


def _pad_lanes(a, w=HP):
    return jnp.pad(a, [(0, 0)] * (a.ndim - 1) + [(0, w - a.shape[-1])])


def _heads_blocks(a, blk):
    t, h, _ = a.shape
    return jnp.transpose(_pad_lanes(a), (1, 0, 2)).astype(BF16).reshape(h, t // blk, blk, HP)


def _heads_blocks_t(a, blk):
    return jnp.swapaxes(_heads_blocks(a, blk), -1, -2)


def _from_blocks_t(a):
    h, n, d, blk = a.shape
    return jnp.transpose(a, (1, 3, 0, 2)).reshape(n * blk, h, d)


def _from_blocks(a):
    h, n, blk, d = a.shape
    return jnp.transpose(a, (1, 2, 0, 3)).reshape(n * blk, h, d)


def _to_hdt(a):
    return jnp.transpose(a, (1, 2, 0))


def _rope_tables(t):
    pos = jnp.arange(t, dtype=F32)
    inv_freq = ROPE_BASE ** (-jnp.arange(0, C_ROPE, 2, dtype=F32) / C_ROPE)
    ang = pos[:, None] * inv_freq[None, :]
    cos, sin = jnp.cos(ang), jnp.sin(ang)
    ones, zeros = jnp.ones((t, C_NOPE), F32), jnp.zeros((t, C_NOPE), F32)
    tail = LANE - C_NOPE - C_ROPE
    c = jnp.concatenate([ones, cos, cos, ones[:, :tail]], axis=1)
    s = jnp.concatenate([zeros, -sin, sin, zeros[:, :tail]], axis=1)
    return c, s


def _band_calls(t, proj, projs_g, sinks):
    none = jnp.full((N_HEADS,), NEG, F32)
    a0 = M_A0 // LANE
    calls = [(proj, (a0, a0 + 4, a0 + 8), t // BLK, BLK, False, none)]
    calls += [(pg, (0, 4, 8), t // (d * BLK), BLK, False, none) for pg, d in zip(projs_g, A_DILS[1:])]
    calls.append((proj, (M_BQ // LANE, M_BK // LANE, M_BV // LANE), t // BLK, BLK - 1, True, sinks.astype(F32)))
    return calls


def _layer_fwd(l, x, xb, w, p, biases, rope_cs):
    t = x.shape[0]
    n = f"l{l}_"
    xps = [_perm(xb, d) for d in A_DILS[1:]]
    proj = _mm(xb, w["wm_t"], tb=True, out_dtype=BF16, name=n + "proj")
    projs_g = [_mm(xp, wg, tb=True, out_dtype=BF16, tn=768, name=n + f"proj_g{i + 1}") for i, (xp, wg) in enumerate(zip(xps, w["wg_t"]))]
    s = {"xb": xb, "xps": xps, "proj": proj, "projs_g": projs_g}

    calls = _band_calls(t, proj, projs_g, p["sinks"])
    outs = [_band_fwd(src, offs, biases[i], sk, nb=nb, lim=lim, gqa=gqa, name=n + f"band{i}")
            for i, (src, offs, nb, lim, gqa, sk) in enumerate(calls)]
    os_ = [_unperm(outs[gi][0], d) for gi, d in enumerate(A_DILS)]
    lses = [_unperm(outs[gi][1], d) for gi, d in enumerate(A_DILS)]
    o_a = _combine_fwd(os_, lses, name=n + "combine_fwd")
    o_b_f, lse_b = outs[3]
    o_b = o_b_f.astype(BF16)
    s.update(os=os_, lses=lses, lses_p=[outs[gi][1] for gi in range(3)], o_b=o_b_f, lse_b=lse_b)

    rq = _rms_fwd(proj, C_Q_RANK, M_CQ // C_Q_RANK, p["q_norm_g"], name=n + "rms_q")
    rkv = _rms_fwd(proj, C_KV_RANK, M_CDKV // C_KV_RANK, p["kv_norm_g"], name=n + "rms_kv")
    q_cp = _mm(rq, w["w_uq_p"], out_dtype=BF16, name=n + "uq")
    kv_cp = _mm(rkv, w["w_ukv_p"], out_dtype=BF16, name=n + "ukv")
    q_full = _rope_slabs(q_cp, N_HEADS, rope_cs[0], rope_cs[1], name=n + "rope_q")
    k_full = _rope_slabs(kv_cp, N_HEADS, rope_cs[0], rope_cs[1], add=(proj, (M_CDKV + C_KV_RANK) // LANE), name=n + "rope_k")
    vt = jnp.transpose(kv_cp[:, N_HEADS * LANE:].T.reshape(N_HEADS // 2, LANE, t // TQ, TQ), (0, 2, 1, 3))
    o_c_f, lse_c = _mla_fwd(q_full, k_full, vt, name=n + "mla_fwd")
    o_c = o_c_f.astype(BF16)
    s.update(rq=rq, rkv=rkv, q_full=q_full, k_full=k_full, kv_cp=kv_cp, lse_c=lse_c, o_c=o_c_f)

    obs = [o_a, o_b, o_c]
    ys = [_mm(o, w["w_branch"][i], name=n + f"branch{i}") for i, o in enumerate(obs)]
    merged = _merge_fwd(proj, p["b_gate"], ys, name=n + "merge")
    mix = _mm(merged, w["w_out"], name=n + "out")
    x1f, x1b, z1 = _ln_fwd(x, mix, p["ln1_g"], p["ln1_b"], name=n + "ln1")
    s.update(obs=obs, ys=ys, merged=merged, z1=z1, x1b=x1b)

    ug = _mm(x1b, w["wup_g"], tn=1408, name=n + "up_g")
    uv = _mm(x1b, w["wup_v"], tn=1408, name=n + "up_v")
    h = _glu_fwd(ug, uv, w["conv_w"], p["conv_b"], name=n + "glu")
    ff = _mm(h, w["w_ffn_down"], tk=1408, name=n + "down")
    x2f, x2b, z2 = _ln_fwd(x1f, ff, p["ln2_g"], p["ln2_b"], name=n + "ln2")
    s.update(ug=ug, uv=uv, h=h, z2=z2)
    return x2f, x2b, s


def _layer_bwd(l, s, dys, coefs, w, p, biases, rope_cs):
    n = f"l{l}b_"
    t = s["z2"].shape[0]
    gw, gs = {}, {}
    tr = lambda a: a.T

    dz2, dz2b, gs["ln2_g"], gs["ln2_b"] = _ln_bwd(s["z2"], p["ln2_g"], dys, coefs, name=n + "ln2")
    dh = _mm(dz2b, w["w_ffn_down"], tb=True, tn=1408, name=n + "d_h")
    gw["w_ffn_down"] = _mm(tr(s["h"]), dz2b, tm=704, name=n + "g_down")
    dcg, dcv, gw["conv_w"], gs["conv_b"] = _glu_bwd_a(s["ug"], s["uv"], w["conv_w"], p["conv_b"], dh, name=n + "glu_a")
    dug = _glu_bwd_b(dcg, w["conv_w"], 0, name=n + "glu_bg")
    duv = _glu_bwd_b(dcv, w["conv_w"], 1, name=n + "glu_bv")
    dx1_g = _mm(dug, w["wup_g"], tb=True, tk=1408, name=n + "d_x1g")
    dx1_v = _mm(duv, w["wup_v"], tb=True, tk=1408, name=n + "d_x1v")
    x1t = tr(s["x1b"])
    gw["w_ffn_up"] = jnp.concatenate([_mm(x1t, dug, tn=1408, name=n + "g_upg"), _mm(x1t, duv, tn=1408, name=n + "g_upv")], axis=1)

    dz1, dz1b, gs["ln1_g"], gs["ln1_b"] = _ln_bwd(s["z1"], p["ln1_g"], [dz2, dx1_g, dx1_v], [ALPHA, 1.0, 1.0], name=n + "ln1")
    dmerged = _mm(dz1b, w["w_out"], tb=True, name=n + "d_merged")
    gw["w_out"] = _mm(tr(s["merged"]), dz1b, name=n + "g_out")
    dys_b, dgp, gs["b_gate"] = _merge_bwd(s["proj"], p["b_gate"], s["ys"], dmerged, name=n + "merge")
    dos = [_mm(dy, w["w_branch"][i], tb=True, name=n + f"d_o{i}") for i, dy in enumerate(dys_b)]
    gw["w_branch"] = jnp.stack([_mm(tr(o), dy, name=n + f"g_branch{i}") for i, (o, dy) in enumerate(zip(s["obs"], dys_b))])

    do_gs, dpr_gs = _combine_bwd(s["os"], s["lses"], dos[0], name=n + "combine")
    do_b, dpr_b = _delta(dos[1], s["o_b"], name=n + "delta_b")
    do_list = [_perm(a, d) for a, d in zip(do_gs, A_DILS)] + [do_b]
    dpr_list = [_perm(a, d) for a, d in zip(dpr_gs, A_DILS)] + [dpr_b]
    lse_list = s["lses_p"] + [s["lse_b"]]
    calls = _band_calls(t, s["proj"], s["projs_g"], p["sinks"])
    band = [_band_bwd(src, offs, do_list[i], lse_list[i], dpr_list[i], biases[i], sk, nb=nb, lim=lim, gqa=gqa, name=n + f"band{i}")
            for i, (src, offs, nb, lim, gqa, sk) in enumerate(calls)]
    gs["sinks"] = band[3][4][:, 0, 0]
    ds_sum = jnp.concatenate([b_[3] for b_ in band], axis=0)

    do_c, delta_c = _delta(dos[2], s["o_c"], name=n + "delta_c")
    dq, dk, dv = _mla_bwd(s["q_full"], s["k_full"], s["kv_cp"], do_c, s["lse_c"], delta_c, name=n + "mla")
    dq_cp = _rope_slabs(dq, N_HEADS, rope_cs[0], -rope_cs[1], name=n + "rope_q")
    dk_sum = _rowwise(lambda i, nt, *vs: sum(vs[1:], vs[0]), [(dk[None], LANE, hh) for hh in range(N_HEADS)],
                      outs=[(LANE, LANE, 0, F32)], tm=1024, name=n + "krope_sum")[0][0]
    dkr = _rope_slabs(dk_sum, 1, rope_cs[0], -rope_cs[1], to_front=True, name=n + "rope_k")
    dkv_cp = jnp.concatenate([dk, dv], axis=1)
    d_rq = _mm(dq_cp, w["w_uq_p"], tb=True, name=n + "d_rq")
    d_rkv = _mm(dkv_cp, w["w_ukv_p"], tb=True, name=n + "d_rkv")
    g_uq = _mm(tr(s["rq"]), dq_cp, name=n + "g_uq")
    g_ukv = _mm(tr(s["rkv"]), dkv_cp, name=n + "g_ukv")
    gw["w_uq"] = g_uq.reshape(C_Q_RANK, N_HEADS, LANE)[:, :, :C_NOPE + C_ROPE].reshape(C_Q_RANK, -1)
    kw = N_HEADS * LANE
    gw["w_ukv"] = jnp.concatenate([g_ukv[:, :kw].reshape(C_KV_RANK, N_HEADS, LANE)[:, :, :C_NOPE],
                                   g_ukv[:, kw:].reshape(C_KV_RANK, N_HEADS, HEAD_DIM)], axis=2).reshape(C_KV_RANK, -1)
    dcq, gs["q_norm_g"] = _rms_bwd(s["proj"], C_Q_RANK, M_CQ // C_Q_RANK, p["q_norm_g"], d_rq, name=n + "rms_q")
    dckv, gs["kv_norm_g"] = _rms_bwd(s["proj"], C_KV_RANK, M_CDKV // C_KV_RANK, p["kv_norm_g"], d_rkv, name=n + "rms_kv")
    dcdkv = jnp.concatenate([dckv, dkr], axis=1)

    dproj = jnp.concatenate(dgp + list(band[0][:3]) + list(band[3][:3]) + [dcq, dcdkv], axis=1)
    dprojs_g = [jnp.concatenate(band[gi][:3], axis=1) for gi in (1, 2)]
    dx_terms = [_mm(dproj, w["wm_t"], tk=1024, name=n + "d_x")]
    dx_terms += [_unperm(_mm(dp, wg, tk=768, name=n + f"d_x_g{i + 1}"), d) for i, (dp, wg, d) in enumerate(zip(dprojs_g, w["wg_t"], A_DILS[1:]))]
    g_main = _mm(tr(s["xb"]), dproj, name=n + "g_in").T
    g_groups = [_mm(tr(xp), dp, tn=768, name=n + f"g_in_g{i + 1}").T for i, (xp, dp) in enumerate(zip(s["xps"], dprojs_g))]
    fold = lambda a, tag: _sum_rows([a.reshape(2, 2, HEAD_DIM, D_MODEL)[:, j] for j in range(2)], tm=HEAD_DIM,
                                    name=n + "g_fold_" + tag).reshape(2 * HEAD_DIM, D_MODEL)
    gw["w_in"] = jnp.concatenate([g_main[M_A0:M_BQ], g_groups[0], g_groups[1], g_main[M_BQ:M_BK], fold(g_main[M_BK:M_BV], "k"),
                                  fold(g_main[M_BV:M_CQ], "v"), g_main[M_CQ:M_CDKV + C_KV_RANK + C_ROPE], g_main[M_GATE:M_A0]], axis=0)
    return [dz1] + dx_terms, [ALPHA, 1.0, 1.0, 1.0], gw, gs, ds_sum


def _local_step(x, target, ws, rel_table, small):
    t = x.shape[0]
    ps = [{k: small[k][l] for k in SMALL} for l in range(DEPTH)]
    bucket = _bucket_index()
    bias_all = _bias_lookup(bucket, rel_table.T, name="bias_lookup").reshape(4, N_HEADS, BLK, 2 * BLK)
    biases = [bias_all[i] for i in range(4)]
    rope_cs = _rope_tables(t)

    saved, h, hb = [], x, x.astype(BF16)
    for l in range(DEPTH):
        h, hb, s = _layer_fwd(l, h, hb, ws[l], ps[l], biases, rope_cs)
        saved.append(s)
    dy, loss_part = _loss_and_grad(h, target, name="loss")

    dys, coefs = [dy], [1.0]
    gws, gss, dss = [None] * DEPTH, [None] * DEPTH, [None] * DEPTH
    for l in reversed(range(DEPTH)):
        dys, coefs, gws[l], gss[l], dss[l] = _layer_bwd(l, saved[l], dys, coefs, ws[l], ps[l], biases, rope_cs)
    grad_x = _lincomb(dys, coefs, name="grad_x")
    npos = 2 * BLK * BLK
    g_rel = _bias_grad(bucket, dss[0].reshape(4 * N_HEADS, npos), dss[1].reshape(4 * N_HEADS, npos), name="bias_grad").T
    gsmall = {k: jnp.stack([gss[l][k] for l in range(DEPTH)]) for k in SMALL}
    return loss_part, grad_x, gws, gsmall, g_rel


def kernel(x, rel_table, w_in, b_gate, sinks, q_norm_g, kv_norm_g, w_uq, w_ukv, w_branch, w_out, ln1_g, ln1_b, w_ffn_up, conv_w, conv_b, w_ffn_down, ln2_g, ln2_b, loss_target, m_rel_table, m_w_in, m_b_gate, m_sinks, m_q_norm_g, m_kv_norm_g, m_w_uq, m_w_ukv, m_w_branch, m_w_out, m_ln1_g, m_ln1_b, m_w_ffn_up, m_conv_w, m_conv_b, m_w_ffn_down, m_ln2_g, m_ln2_b, v_rel_table, v_w_in, v_b_gate, v_sinks, v_q_norm_g, v_kv_norm_g, v_w_uq, v_w_ukv, v_w_branch, v_w_out, v_ln1_g, v_ln1_b, v_w_ffn_up, v_conv_w, v_conv_b, v_w_ffn_down, v_ln2_g, v_ln2_b):
    wts = dict(rel_table=rel_table, w_in=w_in, b_gate=b_gate, sinks=sinks, q_norm_g=q_norm_g, kv_norm_g=kv_norm_g, w_uq=w_uq,
               w_ukv=w_ukv, w_branch=w_branch, w_out=w_out, ln1_g=ln1_g, ln1_b=ln1_b, w_ffn_up=w_ffn_up, conv_w=conv_w,
               conv_b=conv_b, w_ffn_down=w_ffn_down, ln2_g=ln2_g, ln2_b=ln2_b)
    ms = dict(rel_table=m_rel_table, w_in=m_w_in, b_gate=m_b_gate, sinks=m_sinks, q_norm_g=m_q_norm_g, kv_norm_g=m_kv_norm_g,
              w_uq=m_w_uq, w_ukv=m_w_ukv, w_branch=m_w_branch, w_out=m_w_out, ln1_g=m_ln1_g, ln1_b=m_ln1_b, w_ffn_up=m_w_ffn_up,
              conv_w=m_conv_w, conv_b=m_conv_b, w_ffn_down=m_w_ffn_down, ln2_g=m_ln2_g, ln2_b=m_ln2_b)
    vs = dict(rel_table=v_rel_table, w_in=v_w_in, b_gate=v_b_gate, sinks=v_sinks, q_norm_g=v_q_norm_g, kv_norm_g=v_kv_norm_g,
              w_uq=v_w_uq, w_ukv=v_w_ukv, w_branch=v_w_branch, w_out=v_w_out, ln1_g=v_ln1_g, ln1_b=v_ln1_b, w_ffn_up=v_w_ffn_up,
              conv_w=v_conv_w, conv_b=v_conv_b, w_ffn_down=v_w_ffn_down, ln2_g=v_ln2_g, ln2_b=v_ln2_b)

    core = lax.axis_index("c")
    chip = 2 * lax.axis_index("x") + lax.axis_index("y")

    names = [name for name, _, _ in MATS]
    sent = [_weight_send(name, wts[name]) for name in names]
    got = _allgather_weights(sent)
    gathered = {name: lax.dynamic_update_slice(g, s[None], (chip,) + (0,) * s.ndim) for name, g, s in zip(names, got, sent)}
    ws = [_full_weights(gathered, l) for l in range(DEPTH)]

    small = {k: wts[k] for k in SMALL}
    loss_part, grad_x, gws, gsmall, g_rel = _local_step(x[0], loss_target[0], ws, rel_table, small)

    rnames = [name for name, _, _ in REDUCED]
    gsend = [jnp.stack([_grad_send(name, gws[l][name], shape, ax) for l in range(DEPTH)]) for name, shape, ax in REDUCED]
    theirs = _sibling_swap(gsend)
    pairs = [_pair_add(g, t_, core, name="grad_pair_" + name) for name, g, t_ in zip(rnames, gsend, theirs)]
    arrived = _chip_scatter(pairs)
    reduced = [_chip_add(lax.dynamic_index_in_dim(p, chip, 0, keepdims=False), a, name="grad_chip_" + name)
               for name, p, a in zip(rnames, pairs, arrived)]
    others = _sibling_share(reduced)
    gshard = {}
    for name, mine, other in zip(rnames, reduced, others):
        layers = [jnp.where(core == l, mine, other) for l in range(DEPTH)]
        gshard[name] = jnp.stack([_grad_recv(name, a) for a in layers])

    conv_w_full = jnp.stack([gws[l]["conv_w"] for l in range(DEPTH)])
    small_red = _allreduce_small(_pack_small(g_rel, gsmall, conv_w_full, loss_part))
    g_rel_r, gsmall_r, conv_w_r, loss_vec = _unpack_small(small_red)
    loss = loss_vec[0]
    shard_w = 2 * D_FF // N_CHIP
    gshard["conv_w"] = lax.dynamic_slice_in_dim(conv_w_r, chip * shard_w, shard_w, axis=2)

    grads = dict(gshard)
    grads.update(gsmall_r)
    grads["rel_table"] = g_rel_r
    deltas, new_m, new_v = {}, {}, {}
    for name, _, _ in MATS:
        shp = wts[name].shape
        v2 = lambda a: a.reshape(-1, shp[-1])
        d_, m_, v_ = _adamw(v2(wts[name]), v2(grads[name]), v2(ms[name]), v2(vs[name]), name="adamw_" + name)
        deltas[name], new_m[name], new_v[name] = d_.reshape(shp), m_.reshape(shp), v_.reshape(shp)
    zero, none = jnp.zeros((LANE,), F32), jnp.zeros((0,), F32)
    sw = _pack_small(wts["rel_table"], {k: wts[k] for k in SMALL}, none, zero)
    sm = _pack_small(ms["rel_table"], {k: ms[k] for k in SMALL}, none, zero)
    sv = _pack_small(vs["rel_table"], {k: vs[k] for k in SMALL}, none, zero)
    sg = _pack_small(g_rel_r, gsmall_r, none, zero)
    sd, smn, svn = _adamw(sw, sg, sm, sv, name="adamw_small")
    for res, buf in ((deltas, sd), (new_m, smn), (new_v, svn)):
        rel_, sm_ = _unpack_small(jnp.pad(buf, ((0, small_red.shape[0] - buf.shape[0]), (0, 0))))[:2]
        res["rel_table"] = rel_
        res.update(sm_)

    return (loss, grad_x[None], *[grads[k] for k in WEIGHT_ORDER], *[deltas[k] for k in WEIGHT_ORDER],
            *[new_m[k] for k in WEIGHT_ORDER], *[new_v[k] for k in WEIGHT_ORDER])
```

```python
import math

import jax
import jax.numpy as jnp
from jax import lax
from jax.experimental import pallas as pl
from jax.experimental.pallas import tpu as pltpu

F32 = jnp.float32
BF16 = jnp.bfloat16
MESH = pl.DeviceIdType.MESH

D_MODEL = 1024
DEPTH = 2
HEAD_DIM = 64
N_HEADS = 8
A_DILS = (1, 4, 16)
C_Q_RANK = 256
C_KV_RANK = 128
C_NOPE = 64
C_ROPE = 32
ROPE_BASE = 10000.0
REL_BUCKETS = 32
REL_MAX_DIST = 2048
D_FF = 2816
ALPHA = (2 * DEPTH) ** 0.25
LN_EPS = 1e-5
RMS_EPS = 1e-6
NEG = -1e30
ADAM_LR, ADAM_B1, ADAM_B2, ADAM_EPS, ADAM_WD, ADAM_STEP = 0.001, 0.9, 0.999, 1e-08, 0.01, 10

VMEM_LIMIT_BYTES = 56 * 1024 * 1024
LANE = 128
BLK = 128
TQ = 512
HP = 128
BAND_UNROLL = 16

D_IN = 8864
A_COLS = 3 * N_HEADS * HEAD_DIM
ORIG = {"a": 0, "bq": 4608, "bk": 5120, "bv": 5248, "cq": 5376, "cdkv": 5632, "gate": 5792}
M_GATE, M_A0, M_BQ, M_BK, M_BV, M_CQ, M_CDKV, M_COLS = 0, 3072, 4608, 5120, 5376, 5632, 5888, 6144

N_CHIP = 4
MATS = (
    ("w_in", (D_MODEL, D_IN), 1),
    ("w_uq", (C_Q_RANK, 768), 1),
    ("w_ukv", (C_KV_RANK, 1024), 1),
    ("w_branch", (3, 512, D_MODEL), 2),
    ("w_out", (D_MODEL, D_MODEL), 0),
    ("w_ffn_up", (D_MODEL, 2 * D_FF), 1),
    ("conv_w", (3, 2 * D_FF), 1),
    ("w_ffn_down", (D_FF, D_MODEL), 0),
)
SMALL = ("b_gate", "sinks", "q_norm_g", "kv_norm_g", "ln1_g", "ln1_b", "conv_b", "ln2_g", "ln2_b")
SMALL_SIZES = {"b_gate": 3072, "sinks": 8, "q_norm_g": 256, "kv_norm_g": 128, "ln1_g": 1024, "ln1_b": 1024,
               "conv_b": 5632, "ln2_g": 1024, "ln2_b": 1024}
WEIGHT_ORDER = ("rel_table", "w_in", "b_gate", "sinks", "q_norm_g", "kv_norm_g", "w_uq", "w_ukv", "w_branch",
                "w_out", "ln1_g", "ln1_b", "w_ffn_up", "conv_w", "conv_b", "w_ffn_down", "ln2_g", "ln2_b")


def _cparams(sem):
    return pltpu.CompilerParams(dimension_semantics=sem, vmem_limit_bytes=VMEM_LIMIT_BYTES)


def _shard_shape(shape, ax):
    s = list(shape)
    s[ax] //= N_CHIP
    return tuple(s)


def _ceil_to(n, m):
    return -(-n // m) * m


def _pick(n, target):
    if n <= target:
        return n
    best = None
    for t in range(LANE, target + 1, LANE):
        if n % t == 0:
            best = t
    assert best is not None, (n, target)
    return best


def _mm(a, b, *, ta=False, tb=False, out_dtype=F32, tm=1024, tn=1024, tk=2048, name):
    assert not (ta and tb)
    k, m = a.shape[::-1] if not ta else a.shape
    n = b.shape[0] if tb else b.shape[1]
    assert (b.shape[1] if tb else b.shape[0]) == k
    tm, tn, tk = _pick(m, tm), _pick(n, tn), _pick(k, tk)
    nk = k // tk
    dn = (((0 if ta else 1,), (1 if tb else 0,)), ((), ()))

    def body(a_ref, b_ref, o_ref, acc_ref):
        part = lax.dot_general(a_ref[...].astype(BF16), b_ref[...].astype(BF16), dn, preferred_element_type=F32)
        if nk == 1:
            o_ref[...] = part.astype(o_ref.dtype)
        else:
            kk = pl.program_id(2)

            @pl.when(kk == 0)
            def _():
                acc_ref[...] = part

            @pl.when(kk > 0)
            def _():
                acc_ref[...] += part

            @pl.when(kk == nk - 1)
            def _():
                o_ref[...] = acc_ref[...].astype(o_ref.dtype)

    a_spec = pl.BlockSpec((tk, tm), lambda i, j, kk: (kk, i)) if ta else pl.BlockSpec((tm, tk), lambda i, j, kk: (i, kk))
    b_spec = pl.BlockSpec((tn, tk), lambda i, j, kk: (j, kk)) if tb else pl.BlockSpec((tk, tn), lambda i, j, kk: (kk, j))
    return pl.pallas_call(
        body, name=name, grid=(m // tm, n // tn, nk),
        in_specs=[a_spec, b_spec],
        out_specs=pl.BlockSpec((tm, tn), lambda i, j, kk: (i, j)),
        out_shape=jax.ShapeDtypeStruct((m, n), out_dtype),
        scratch_shapes=[pltpu.VMEM((tm, tn) if nk > 1 else (8, LANE), F32)],
        compiler_params=_cparams(("parallel", "parallel", "arbitrary")),
    )(a, b)


def _rowwise(fn, rows, *, pars=(), halos=(), outs=(), accs=(), tm, name, ncol=1, t=None):
    nb = rows[0][0].shape[0]
    t = rows[0][0].shape[1] if t is None else t
    tm = min(tm, t)
    assert t % tm == 0 and tm % 8 == 0
    nt = t // tm
    in_specs, args = [], []
    for spec in rows:
        arr, c, off = spec[:3]
        rb = spec[3] if len(spec) > 3 else 0
        in_specs.append(pl.BlockSpec((1, tm, c), lambda b, cc, i, off=off, rb=rb: (b, i + rb, off + cc)))
        args.append(arr)
    for arr, c, off, kind in halos:
        if kind == "prev":
            im = lambda b, cc, i, off=off: (b, jnp.maximum(i * (tm // 8) - 1, 0), off + cc)
        else:
            im = lambda b, cc, i, off=off: (b, jnp.minimum((i + 1) * (tm // 8), t // 8 - 1), off + cc)
        in_specs.append(pl.BlockSpec((1, 8, c), im))
        args.append(arr)
    for arr, c, off in pars:
        bp, r = arr.shape[:2]
        if bp > 1:
            im = lambda b, cc, i, off=off: (b, 0, off + cc)
        else:
            im = lambda b, cc, i, off=off: (0, 0, off + cc)
        in_specs.append(pl.BlockSpec((1, r, c), im))
        args.append(arr)
    out_specs, out_shapes = [], []
    for ctot, c, off, dt in outs:
        out_specs.append(pl.BlockSpec((1, tm, c), lambda b, cc, i, off=off: (b, i, off + cc)))
        out_shapes.append(jax.ShapeDtypeStruct((nb, t, ctot), dt))
    for r, ctot, c, off in accs:
        out_specs.append(pl.BlockSpec((1, r, c), lambda b, cc, i, off=off: (b, 0, off + cc)))
        out_shapes.append(jax.ShapeDtypeStruct((nb, r, ctot), F32))
    n_in, n_out = len(args), len(outs)

    def body(*refs):
        i = pl.program_id(2)
        res = fn(i, nt, *[r[0].astype(F32) for r in refs[:n_in]])
        if not isinstance(res, (tuple, list)):
            res = (res,)
        for o_ref, val in zip(refs[n_in:n_in + n_out], res[:n_out]):
            o_ref[0] = val.astype(o_ref.dtype)
        for a_ref, val in zip(refs[n_in + n_out:], res[n_out:]):
            @pl.when(i == 0)
            def _(a_ref=a_ref, val=val):
                a_ref[0] = val

            @pl.when(i > 0)
            def _(a_ref=a_ref, val=val):
                a_ref[0] += val

    res = pl.pallas_call(
        body, name=name, grid=(nb, ncol, nt), in_specs=in_specs, out_specs=out_specs, out_shape=out_shapes,
        compiler_params=_cparams(("parallel", "parallel", "arbitrary")),
    )(*args)
    return res


def _lanewise(fn, ins, outs, *, tl, name):
    nb, _, t = ins[0].shape
    tl = min(tl, t)
    assert t % tl == 0
    n_in = len(ins)

    def body(*refs):
        res = fn(*[r[0] for r in refs[:n_in]])
        if not isinstance(res, (tuple, list)):
            res = (res,)
        for o_ref, val in zip(refs[n_in:], res):
            o_ref[0] = val.astype(o_ref.dtype)

    return pl.pallas_call(
        body, name=name, grid=(nb, t // tl),
        in_specs=[pl.BlockSpec((1, a.shape[1], tl), lambda b, i: (b, 0, i)) for a in ins],
        out_specs=[pl.BlockSpec((1, r, tl), lambda b, i: (b, 0, i)) for r, _ in outs],
        out_shape=[jax.ShapeDtypeStruct((nb, r, t), dt) for r, dt in outs],
        compiler_params=_cparams(("parallel", "parallel")),
    )(*ins)


def _dot(a, b):
    return lax.dot_general(a, b, (((1,), (0,)), ((), ())), preferred_element_type=F32)


def _dot_nt(a, b):
    return lax.dot_general(a, b, (((1,), (1,)), ((), ())), preferred_element_type=F32)


def _dot_tn(a, b):
    return lax.dot_general(a, b, (((0,), (0,)), ((), ())), preferred_element_type=F32)


BAND_ROWS = BAND_UNROLL * BLK


def _rows(parts):
    return jnp.concatenate(parts, axis=0)


def _lane_lo():
    return lax.broadcasted_iota(jnp.int32, (1, LANE), 1) < HEAD_DIM


def _blocks(a):
    return [a[i * BLK:(i + 1) * BLK] for i in range(BAND_UNROLL)]


def _band_operands(g, k_ref, v_ref):
    start = pl.multiple_of(g * BAND_ROWS, BAND_ROWS)
    pstart = pl.multiple_of(jnp.maximum(g * BAND_ROWS - BLK, 0), BLK)
    out = []
    for ref in (k_ref, v_ref):
        cur = ref[pl.ds(start, BAND_ROWS), :]
        raw = ref[pl.ds(pstart, BAND_ROWS), :]
        shifted = _rows([jnp.zeros((BLK, LANE), raw.dtype), raw[:BAND_ROWS - BLK]])
        out += [_blocks(cur), _blocks(jnp.where(g == 0, shifted, raw))]
    return out


def _band_scores(g, qa, kc, kp, b_ref, a, nb, lim):
    scale = HEAD_DIM ** -0.5
    qi = jnp.bitwise_and(lax.broadcasted_iota(jnp.int32, (BAND_ROWS, BLK), 0), BLK - 1)
    ki = lax.broadcasted_iota(jnp.int32, (BAND_ROWS, BLK), 1)
    tile = lambda blk: _rows([blk] * BAND_UNROLL)
    firsts = []
    for i in range(BAND_UNROLL):
        if nb >= BAND_UNROLL:
            val = jnp.where(lax.rem(g * BAND_UNROLL, nb) == 0, NEG, 0.0).astype(F32) if i == 0 else 0.0
        else:
            val = NEG if i % nb == 0 else 0.0
        firsts.append(jnp.zeros((BLK, 1), F32) + val)
    sc = _rows([_dot_nt(q, k) for q, k in zip(qa, kc)]) * scale + tile(b_ref[a, :, BLK:2 * BLK])
    sp = _rows([_dot_nt(q, k) for q, k in zip(qa, kp)]) * scale + tile(b_ref[a, :, 0:BLK])
    return jnp.where(ki <= qi, sc, NEG), jnp.where((BLK + qi - ki) <= lim, sp, NEG) + _rows(firsts)


def _band_fwd(src, offs, bias, sinks, *, nb, lim, gqa, name):
    t = src.shape[0]
    assert t % BAND_ROWS == 0 and (nb % BAND_UNROLL == 0 or BAND_UNROLL % nb == 0)
    qo, ko, vo = offs
    share = 2 if gqa else 1

    def body(sink_ref, q_ref, k_ref, v_ref, b_ref, o_ref, lse_ref):
        hp, g = pl.program_id(0), pl.program_id(1)
        lo = _lane_lo()
        q2 = q_ref[...]
        kc, kp, vc, vp = _band_operands(g, k_ref, v_ref)
        outs, lses = [], []
        for a in range(2):
            sink = sink_ref[2 * hp + a]
            qa = _blocks(jnp.where(lo if a == 0 else jnp.logical_not(lo), q2, jnp.zeros_like(q2)))
            sc, sp = _band_scores(g, qa, kc, kp, b_ref, a, nb, lim)
            m = jnp.maximum(jnp.maximum(jnp.max(sc, axis=1, keepdims=True), jnp.max(sp, axis=1, keepdims=True)), sink)
            pc, pp = jnp.exp(sc - m), jnp.exp(sp - m)
            l = jnp.sum(pc, axis=1, keepdims=True) + jnp.sum(pp, axis=1, keepdims=True) + jnp.exp(sink - m)
            inv = 1.0 / l
            pc_b, pp_b = _blocks((pc * inv).astype(BF16)), _blocks((pp * inv).astype(BF16))
            outs.append(_rows([_dot(pc_b[i], vc[i]) + _dot(pp_b[i], vp[i]) for i in range(BAND_UNROLL)]))
            lses.append(m + jnp.log(l))
        o_ref[...] = jnp.where(lo, outs[0], outs[1])
        lse_ref[...] = jnp.where(lo, lses[0], lses[1])

    slab = lambda off: pl.BlockSpec((BAND_ROWS, LANE), lambda hp, g, off=off: (g, off + hp))
    whole = lambda off: pl.BlockSpec((t, LANE), lambda hp, g, off=off: (0, off + hp // share))
    return pl.pallas_call(
        body, name=name, grid=(N_HEADS // 2, t // BAND_ROWS),
        in_specs=[pl.BlockSpec(memory_space=pltpu.SMEM), slab(qo), whole(ko), whole(vo),
                  pl.BlockSpec((2, BLK, 2 * BLK), lambda hp, g: (hp, 0, 0))],
        out_specs=[slab(0), slab(0)],
        out_shape=[jax.ShapeDtypeStruct((t, N_HEADS * HEAD_DIM), F32)] * 2,
        compiler_params=_cparams(("parallel", "parallel")),
    )(sinks, src, src, src, bias)


def _band_bwd(src, offs, do, lse, dpr, bias, sinks, *, nb, lim, gqa, name):
    t = src.shape[0]
    qo, ko, vo = offs
    share = 2 if gqa else 1
    nstep = t // BAND_ROWS
    scale = HEAD_DIM ** -0.5

    def fold(a):
        acc = a[0:BLK]
        for i in range(1, BAND_UNROLL):
            acc = acc + a[i * BLK:(i + 1) * BLK]
        return acc

    def body(sink_ref, q_ref, k_ref, v_ref, do_ref, lse_ref, dpr_ref, b_ref,
             dq_ref, dk_ref, dv_ref, ds_ref, dsink_ref, dk_acc, dv_acc):
        hp, g = pl.program_id(0), pl.program_id(1)
        lo = _lane_lo()
        hi = jnp.logical_not(lo)

        @pl.when(jnp.logical_and(g == 0, lax.rem(hp, share) == 0))
        def _():
            dk_acc[...] = jnp.zeros_like(dk_acc)
            dv_acc[...] = jnp.zeros_like(dv_acc)

        @pl.when(g == 0)
        def _():
            ds_ref[...] = jnp.zeros_like(ds_ref)
            dsink_ref[...] = jnp.zeros_like(dsink_ref)

        q2, do2, lse2, dpr2 = q_ref[...], do_ref[...], lse_ref[...], dpr_ref[...]
        lse_sw, dpr_sw = pltpu.roll(lse2, HEAD_DIM, axis=1), pltpu.roll(dpr2, HEAD_DIM, axis=1)
        kc, kp, vc, vp = _band_operands(g, k_ref, v_ref)
        dqs, dk_cur, dk_prev, dv_cur, dv_prev = [], None, None, None, None
        for a in range(2):
            sink = sink_ref[2 * hp + a]
            mine = lo if a == 0 else hi
            qa = _blocks(jnp.where(mine, q2, jnp.zeros_like(q2)))
            doa = _blocks(jnp.where(mine, do2, jnp.zeros_like(do2)))
            lse_a, dpr_a = jnp.where(mine, lse2, lse_sw), jnp.where(mine, dpr2, dpr_sw)
            sc, sp = _band_scores(g, qa, kc, kp, b_ref, a, nb, lim)
            pc, pp = jnp.exp(sc - lse_a), jnp.exp(sp - lse_a)
            dsc = pc * (_rows([_dot_nt(d, v) for d, v in zip(doa, vc)]) - dpr_a)
            dsp = pp * (_rows([_dot_nt(d, v) for d, v in zip(doa, vp)]) - dpr_a)
            ds_ref[a, :, BLK:2 * BLK] += fold(dsc)
            ds_ref[a, :, 0:BLK] += fold(dsp)
            dsink_ref[a] -= jnp.sum(jnp.exp(sink - lse_a) * dpr_a, axis=0, keepdims=True)
            dsc_b, dsp_b = _blocks((dsc * scale).astype(BF16)), _blocks((dsp * scale).astype(BF16))
            pc_b, pp_b = _blocks(pc.astype(BF16)), _blocks(pp.astype(BF16))
            dqs.append(_rows([_dot(dsc_b[i], kc[i]) + _dot(dsp_b[i], kp[i]) for i in range(BAND_UNROLL)]))
            parts = [_rows([_dot_tn(x[i], y[i]) for i in range(BAND_UNROLL)])
                     for x, y in ((dsc_b, qa), (dsp_b, qa), (pc_b, doa), (pp_b, doa))]
            if a == 0:
                dk_cur, dk_prev, dv_cur, dv_prev = parts
            else:
                dk_cur, dk_prev, dv_cur, dv_prev = dk_cur + parts[0], dk_prev + parts[1], dv_cur + parts[2], dv_prev + parts[3]
        dq_ref[...] = jnp.where(lo, dqs[0], dqs[1]).astype(dq_ref.dtype)
        start = pl.multiple_of(g * BAND_ROWS, BAND_ROWS)
        after = pl.multiple_of(g * BAND_ROWS + BLK, BLK)
        dk_acc[pl.ds(after, BAND_ROWS), :] += dk_cur
        dk_acc[pl.ds(start, BAND_ROWS), :] += dk_prev
        dv_acc[pl.ds(after, BAND_ROWS), :] += dv_cur
        dv_acc[pl.ds(start, BAND_ROWS), :] += dv_prev

        @pl.when(g == nstep - 1)
        def _():
            dk_ref[...] = dk_acc[BLK:, :].astype(dk_ref.dtype)
            dv_ref[...] = dv_acc[BLK:, :].astype(dv_ref.dtype)

    slab = lambda off: pl.BlockSpec((BAND_ROWS, LANE), lambda hp, g, off=off: (g, off + hp))
    whole = lambda off: pl.BlockSpec((t, LANE), lambda hp, g, off=off: (0, off + hp // share))
    per_pair = lambda shp: pl.BlockSpec((2,) + shp, lambda hp, g: (hp,) + (0,) * len(shp))
    kv_cols = N_HEADS * HEAD_DIM // share
    return pl.pallas_call(
        body, name=name, grid=(N_HEADS // 2, nstep),
        in_specs=[pl.BlockSpec(memory_space=pltpu.SMEM), slab(qo), whole(ko), whole(vo), slab(0), slab(0), slab(0),
                  per_pair((BLK, 2 * BLK))],
        out_specs=[slab(0), whole(0), whole(0), per_pair((BLK, 2 * BLK)), per_pair((1, LANE))],
        out_shape=[jax.ShapeDtypeStruct((t, N_HEADS * HEAD_DIM), BF16), jax.ShapeDtypeStruct((t, kv_cols), BF16),
                   jax.ShapeDtypeStruct((t, kv_cols), BF16), jax.ShapeDtypeStruct((N_HEADS, BLK, 2 * BLK), F32),
                   jax.ShapeDtypeStruct((N_HEADS, 1, LANE), F32)],
        scratch_shapes=[pltpu.VMEM((t + BLK, LANE), F32), pltpu.VMEM((t + BLK, LANE), F32)],
        compiler_params=_cparams(("arbitrary", "arbitrary")),
    )(sinks, src, src, src, do, lse, dpr, bias)


MLA_V_OFF = N_HEADS


def _diag_mask(keys_on_rows=False):
    rows, cols = lax.broadcasted_iota(jnp.int32, (TQ, TQ), 0), lax.broadcasted_iota(jnp.int32, (TQ, TQ), 1)
    return rows <= cols if keys_on_rows else cols <= rows


def _mla_specs(t):
    blk = lambda f: pl.BlockSpec((TQ, LANE), lambda hp, qi, f=f: (qi, f(hp)))
    whole = lambda f: pl.BlockSpec((t, LANE), lambda hp, qi, f=f: (0, f(hp)))
    return blk, whole


def _mla_fwd(q, k, vt, *, name):
    t = q.shape[0]
    n = t // TQ
    scale = (C_NOPE + C_ROPE) ** -0.5

    def body(q0_ref, q1_ref, k0_ref, k1_ref, vt_ref, o_ref, lse_ref, m_ref, l_ref, acc_ref):
        qi = pl.program_id(1)
        qs, ks = (q0_ref[...], q1_ref[...]), (k0_ref, k1_ref)
        m_ref[...] = jnp.full_like(m_ref, NEG)
        l_ref[...] = jnp.zeros_like(l_ref)
        acc_ref[...] = jnp.zeros_like(acc_ref)

        def step(kj, diagonal):
            rows = pl.ds(pl.multiple_of(kj * TQ, TQ), TQ)
            vtb = vt_ref[0, kj]
            for a in range(2):
                s = _dot_nt(ks[a][rows, :], qs[a]) * scale
                if diagonal:
                    s = jnp.where(_diag_mask(keys_on_rows=True), s, NEG)
                m_prev = m_ref[a]
                m_new = jnp.maximum(m_prev, jnp.max(s, axis=0, keepdims=True))
                alpha = jnp.exp(m_prev - m_new)
                p = jnp.exp(s - m_new)
                l_ref[a] = alpha * l_ref[a] + jnp.sum(p, axis=0, keepdims=True)
                acc_ref[a] = alpha * acc_ref[a] + _dot(vtb, p.astype(BF16))
                m_ref[a] = m_new

        def kloop(kj, c2):
            step(kj, False)
            return c2

        lax.fori_loop(0, qi, kloop, 0)
        step(qi, True)
        first = lax.broadcasted_iota(jnp.int32, (LANE, 1), 0) < HEAD_DIM
        ot = jnp.where(first, acc_ref[0] * (1.0 / l_ref[0]), acc_ref[1] * (1.0 / l_ref[1]))
        lset = jnp.where(first, m_ref[0] + jnp.log(l_ref[0]), m_ref[1] + jnp.log(l_ref[1]))
        o_ref[...] = ot.T
        lse_ref[...] = lset.T

    blk, whole = _mla_specs(t)
    return pl.pallas_call(
        body, name=name, grid=(N_HEADS // 2, n),
        in_specs=[blk(lambda hp: 2 * hp), blk(lambda hp: 2 * hp + 1), whole(lambda hp: 2 * hp), whole(lambda hp: 2 * hp + 1),
                  pl.BlockSpec((1, n, LANE, TQ), lambda hp, qi: (hp, 0, 0, 0))],
        out_specs=[blk(lambda hp: hp), blk(lambda hp: hp)],
        out_shape=[jax.ShapeDtypeStruct((t, N_HEADS * HEAD_DIM), F32)] * 2,
        scratch_shapes=[pltpu.VMEM((2, 1, TQ), F32), pltpu.VMEM((2, 1, TQ), F32), pltpu.VMEM((2, LANE, TQ), F32)],
        compiler_params=_cparams(("parallel", "parallel")),
    )(q, q, k, k, vt)


def _mla_bwd(q, k, kv, do, lse, delta, *, name):
    t = q.shape[0]
    n = t // TQ
    scale = (C_NOPE + C_ROPE) ** -0.5

    def body(q0_ref, q1_ref, k0_ref, k1_ref, v_ref, do_ref, lse_ref, dl_ref,
             dq_ref, dk_ref, dv_ref, dq_acc, dk_acc, dv_acc):
        qi = pl.program_id(1)
        lo = _lane_lo()

        @pl.when(qi == 0)
        def _():
            dk_acc[...] = jnp.zeros_like(dk_acc)
            dv_acc[...] = jnp.zeros_like(dv_acc)

        dq_acc[...] = jnp.zeros_like(dq_acc)
        qs, ks = (q0_ref[...], q1_ref[...]), (k0_ref, k1_ref)
        do2, lse2, dl2 = do_ref[...], lse_ref[...], dl_ref[...]
        lse_sw, dl_sw = pltpu.roll(lse2, HEAD_DIM, axis=1), pltpu.roll(dl2, HEAD_DIM, axis=1)
        heads = []
        for a in range(2):
            mine = lo if a == 0 else jnp.logical_not(lo)
            heads.append((jnp.where(mine, do2, jnp.zeros_like(do2)), jnp.where(mine, lse2, lse_sw)[:, 0:1],
                          jnp.where(mine, dl2, dl_sw)[:, 0:1]))

        def step(kj, diagonal):
            rows = pl.ds(pl.multiple_of(kj * TQ, TQ), TQ)
            vb = v_ref[rows, :]
            for a, (doa, lse_a, dl_a) in enumerate(heads):
                kb = ks[a][rows, :]
                s = _dot_nt(qs[a], kb) * scale
                if diagonal:
                    s = jnp.where(_diag_mask(), s, NEG)
                p = jnp.exp(s - lse_a)
                ds = (p * (_dot_nt(doa, vb) - dl_a) * scale).astype(BF16)
                dq_acc[a] += _dot(ds, kb)
                dk_acc[a, rows, :] += _dot_tn(ds, qs[a])
                dv_acc[rows, :] += _dot_tn(p.astype(BF16), doa)

        def kloop(kj, c2):
            step(kj, False)
            return c2

        lax.fori_loop(0, qi, kloop, 0)
        step(qi, True)
        dq_ref[:, 0:LANE] = dq_acc[0].astype(dq_ref.dtype)
        dq_ref[:, LANE:2 * LANE] = dq_acc[1].astype(dq_ref.dtype)

        @pl.when(qi == n - 1)
        def _():
            dk_ref[:, 0:LANE] = dk_acc[0].astype(dk_ref.dtype)
            dk_ref[:, LANE:2 * LANE] = dk_acc[1].astype(dk_ref.dtype)
            dv_ref[...] = dv_acc[...].astype(dv_ref.dtype)

    blk, whole = _mla_specs(t)
    even, odd, pair = (lambda hp: 2 * hp), (lambda hp: 2 * hp + 1), (lambda hp: hp)
    wide = jax.ShapeDtypeStruct((t, N_HEADS * LANE), BF16)
    return pl.pallas_call(
        body, name=name, grid=(N_HEADS // 2, n),
        in_specs=[blk(even), blk(odd), whole(even), whole(odd), whole(lambda hp: MLA_V_OFF + hp), blk(pair), blk(pair), blk(pair)],
        out_specs=[pl.BlockSpec((TQ, 2 * LANE), lambda hp, qi: (qi, hp)), pl.BlockSpec((t, 2 * LANE), lambda hp, qi: (0, hp)), whole(pair)],
        out_shape=[wide, wide, jax.ShapeDtypeStruct((t, N_HEADS * HEAD_DIM), BF16)],
        scratch_shapes=[pltpu.VMEM((2, TQ, LANE), F32), pltpu.VMEM((2, t, LANE), F32), pltpu.VMEM((t, LANE), F32)],
        compiler_params=_cparams(("arbitrary", "arbitrary")),
    )(q, q, k, k, kv, do, lse, delta)


def _bias_lookup(bucket, table_t, *, name):
    nh, npos = bucket.shape
    tp = 4096

    def body(b_ref, t_ref, o_ref):
        bk, tab = b_ref[...], t_ref[...]
        acc = jnp.zeros(bk.shape, F32)
        for i in range(REL_BUCKETS):
            acc = jnp.where(bk == i, tab[:, i:i + 1], acc)
        o_ref[...] = acc

    return pl.pallas_call(
        body, name=name, grid=(npos // tp,),
        in_specs=[pl.BlockSpec((nh, tp), lambda i: (0, i)), pl.BlockSpec((nh, REL_BUCKETS), lambda i: (0, 0))],
        out_specs=pl.BlockSpec((nh, tp), lambda i: (0, i)),
        out_shape=jax.ShapeDtypeStruct((nh, npos), F32),
        compiler_params=_cparams(("parallel",)),
    )(bucket, table_t)


def _bias_grad(bucket, ds0, ds1, *, name):
    nh, npos = bucket.shape
    tp = 4096

    def body(b_ref, a_ref, c_ref, o_ref):
        i = pl.program_id(0)
        bk, ds = b_ref[...], a_ref[...] + c_ref[...]
        lane = lax.broadcasted_iota(jnp.int32, (nh, REL_BUCKETS), 1)
        acc = jnp.zeros((nh, REL_BUCKETS), F32)
        for j in range(REL_BUCKETS):
            col = jnp.sum(jnp.where(bk == j, ds, 0.0), axis=1, keepdims=True)
            acc = acc + jnp.where(lane == j, col, 0.0)

        @pl.when(i == 0)
        def _():
            o_ref[...] = acc

        @pl.when(i > 0)
        def _():
            o_ref[...] += acc

    return pl.pallas_call(
        body, name=name, grid=(npos // tp,),
        in_specs=[pl.BlockSpec((nh, tp), lambda i: (0, i))] * 3,
        out_specs=pl.BlockSpec((nh, REL_BUCKETS), lambda i: (0, 0)),
        out_shape=jax.ShapeDtypeStruct((nh, REL_BUCKETS), F32),
        compiler_params=_cparams(("arbitrary",)),
    )(bucket, ds0, ds1)


def _t5_bucket(dist):
    n = jnp.maximum(dist, 0)
    max_exact = REL_BUCKETS // 2
    scaled = jnp.log(jnp.maximum(n, 1).astype(F32) / max_exact) / math.log(REL_MAX_DIST / max_exact)
    large = max_exact + (scaled * (REL_BUCKETS - max_exact)).astype(jnp.int32)
    return jnp.where(n < max_exact, n, jnp.minimum(large, REL_BUCKETS - 1))


def _bucket_index():
    qi = jnp.arange(BLK)[:, None]
    ci = jnp.arange(2 * BLK)[None, :]
    step = BLK + qi - ci
    per_group = [_t5_bucket(step * d).reshape(1, -1) for d in A_DILS + (1,)]
    return jnp.concatenate([jnp.tile(b, (N_HEADS, 1)) for b in per_group], axis=0).astype(jnp.int32)


def _sigmoid(x):
    return 1.0 / (1.0 + jnp.exp(-x))


def _ln_stats(z):
    mu = jnp.mean(z, axis=-1, keepdims=True)
    zc = z - mu
    var = jnp.mean(zc * zc, axis=-1, keepdims=True)
    return zc * lax.rsqrt(var + LN_EPS)


def _ln_fwd(x, mix, g, b, *, name):
    def fn(i, nt, xv, mv, gv, bv):
        z = ALPHA * xv + mv
        y = _ln_stats(z) * gv + bv
        return y, y, z

    c = x.shape[-1]
    y, yb, z = _rowwise(fn, [(x[None], c, 0), (mix[None], c, 0)], pars=[(g.reshape(1, 1, c), c, 0), (b.reshape(1, 1, c), c, 0)],
                        outs=[(c, c, 0, F32), (c, c, 0, BF16), (c, c, 0, F32)], tm=512, name=name)
    return y[0], yb[0], z[0]


def _ln_bwd(z, g, dys, coefs, *, name):
    n = len(dys)

    def fn(i, nt, zv, *rest):
        gv = rest[n]
        dy = coefs[0] * rest[0]
        for cf, t in zip(coefs[1:], rest[1:n]):
            dy = dy + cf * t
        mu = jnp.mean(zv, axis=-1, keepdims=True)
        zc = zv - mu
        r = lax.rsqrt(jnp.mean(zc * zc, axis=-1, keepdims=True) + LN_EPS)
        xh = zc * r
        dxh = dy * gv
        dz = r * (dxh - jnp.mean(dxh, axis=-1, keepdims=True) - xh * jnp.mean(dxh * xh, axis=-1, keepdims=True))
        return dz, dz, jnp.sum(dy * xh, axis=0, keepdims=True), jnp.sum(dy, axis=0, keepdims=True)

    c = z.shape[-1]
    dz, dzb, dg, db = _rowwise(fn, [(z[None], c, 0)] + [(d[None], c, 0) for d in dys], pars=[(g.reshape(1, 1, c), c, 0)],
                               outs=[(c, c, 0, F32), (c, c, 0, BF16)], accs=[(1, c, c, 0), (1, c, c, 0)], tm=512, name=name)
    return dz[0], dzb[0], dg.reshape(c), db.reshape(c)


def _rms_fwd(src, c, off, g, *, name):
    def fn(i, nt, xv, gv):
        return xv * lax.rsqrt(jnp.mean(xv * xv, axis=-1, keepdims=True) + RMS_EPS) * gv

    return _rowwise(fn, [(src[None], c, off)], pars=[(g.reshape(1, 1, c), c, 0)], outs=[(c, c, 0, BF16)], tm=1024, name=name)[0][0]


def _rms_bwd(src, c, off, g, dy, *, name):
    def fn(i, nt, xv, dyv, gv):
        r = lax.rsqrt(jnp.mean(xv * xv, axis=-1, keepdims=True) + RMS_EPS)
        gd = gv * dyv
        dx = gd * r - xv * (r * r * r) * jnp.mean(gd * xv, axis=-1, keepdims=True)
        return dx, jnp.sum(dyv * xv * r, axis=0, keepdims=True)

    dx, dg = _rowwise(fn, [(src[None], c, off), (dy[None], c, 0)], pars=[(g.reshape(1, 1, c), c, 0)],
                      outs=[(c, c, 0, BF16)], accs=[(1, c, c, 0)], tm=1024, name=name)
    return dx[0], dg.reshape(c)


def _rope_slabs(x, n_slab, c, s, *, add=None, to_front=False, name):
    half = C_ROPE // 2

    def fn(i, nt, xv, cv, sv, *rest):
        lane = lax.broadcasted_iota(jnp.int32, (1, LANE), 1)
        extra = pltpu.roll(rest[0], C_NOPE, axis=1) if rest else None
        outs = []
        for h in range(n_slab):
            xs = xv[:, h * LANE:(h + 1) * LANE]
            if extra is not None:
                xs = xs + extra
            swapped = jnp.where(lane < C_NOPE + half, pltpu.roll(xs, LANE - half, axis=1), pltpu.roll(xs, half, axis=1))
            y = xs * cv + swapped * sv
            if to_front:
                y = jnp.where(lane < C_ROPE, pltpu.roll(y, LANE - C_NOPE, axis=1), 0.0)
            outs.append(y)
        return jnp.concatenate(outs, axis=1) if n_slab > 1 else outs[0]

    w = n_slab * LANE
    rows = [(x[None], w, 0), (c[None], LANE, 0), (s[None], LANE, 0)]
    if add is not None:
        rows.append((add[0][None], LANE, add[1]))
    return _rowwise(fn, rows, outs=[(w, w, 0, BF16)], tm=512, name=name)[0][0]


def _merge_fwd(proj, b_gate, ys, *, name):
    def fn(i, nt, g0, g1, g2, ya, yb, yc, bg):
        return (_sigmoid(g0 + bg[:, 0:1024]) * ya + _sigmoid(g1 + bg[:, 1024:2048]) * yb
                + _sigmoid(g2 + bg[:, 2048:3072]) * yc)

    rows = [(proj[None], 1024, j) for j in range(3)] + [(y[None], 1024, 0) for y in ys]
    return _rowwise(fn, rows, pars=[(b_gate.reshape(1, 1, 3072), 3072, 0)], outs=[(1024, 1024, 0, BF16)], tm=512, name=name)[0][0]


def _merge_bwd(proj, b_gate, ys, dm, *, name):
    def fn(i, nt, g0, g1, g2, ya, yb, yc, dmv, bg):
        outs, dgs = [], []
        for j, (gp, y) in enumerate(((g0, ya), (g1, yb), (g2, yc))):
            s = _sigmoid(gp + bg[:, j * 1024:(j + 1) * 1024])
            outs.append(s * dmv)
            dgs.append(dmv * y * s * (1.0 - s))
        return outs + dgs + [jnp.sum(d, axis=0, keepdims=True) for d in dgs]

    rows = [(proj[None], 1024, j) for j in range(3)] + [(y[None], 1024, 0) for y in ys] + [(dm[None], 1024, 0)]
    res = _rowwise(fn, rows, pars=[(b_gate.reshape(1, 1, 3072), 3072, 0)], outs=[(1024, 1024, 0, BF16)] * 6,
                   accs=[(1, 1024, 1024, 0)] * 3, tm=256, name=name)
    dys = [r[0] for r in res[0:3]]
    dgp = [r[0] for r in res[3:6]]
    dbg = jnp.concatenate([r.reshape(1024) for r in res[6:9]])
    return dys, dgp, dbg


def _shift_down(u, halo, i, k):
    ext = jnp.concatenate([jnp.where(i > 0, halo, 0.0), u], axis=0)
    return pltpu.roll(ext, k, axis=0)[8:]


def _shift_up(u, halo, i, nt, k):
    ext = jnp.concatenate([u, jnp.where(i < nt - 1, halo, 0.0)], axis=0)
    n = ext.shape[0]
    return pltpu.roll(ext, n - k, axis=0)[:n - 8]


GLU_C = D_FF // 2


def _conv(u, halo, i, w, b):
    return w[0:1] * _shift_down(u, halo, i, 2) + w[1:2] * _shift_down(u, halo, i, 1) + w[2:3] * u + b


def _glu_fwd(ug, uv, conv_w, conv_b, *, name):
    def fn(i, nt, g, v, hg, hv, wg, wv, bg, bv):
        cg, cv = _conv(g, hg, i, wg, bg), _conv(v, hv, i, wv, bv)
        return cg * _sigmoid(cg) * cv

    w3, b3 = conv_w[None], conv_b.reshape(1, 1, -1)
    c = GLU_C
    return _rowwise(fn, [(ug[None], c, 0), (uv[None], c, 0)], halos=[(ug[None], c, 0, "prev"), (uv[None], c, 0, "prev")],
                    pars=[(w3, c, 0), (w3, c, 2), (b3, c, 0), (b3, c, 2)], outs=[(D_FF, c, 0, BF16)], tm=256, ncol=2, name=name)[0][0]


def _glu_bwd_a(ug, uv, conv_w, conv_b, dh, *, name):
    def fn(i, nt, g, v, dhv, hg, hv, wg, wv, bg, bv):
        g1, g2 = _shift_down(g, hg, i, 1), _shift_down(g, hg, i, 2)
        v1, v2 = _shift_down(v, hv, i, 1), _shift_down(v, hv, i, 2)
        cg = wg[0:1] * g2 + wg[1:2] * g1 + wg[2:3] * g + bg
        cv = wv[0:1] * v2 + wv[1:2] * v1 + wv[2:3] * v + bv
        s = _sigmoid(cg)
        dcv = dhv * cg * s
        dcg = dhv * cv * (s * (1.0 + cg * (1.0 - s)))
        red = lambda a: jnp.sum(a, axis=0, keepdims=True)
        return (dcg, dcv, red(dcg), red(dcv), red(dcg * g2), red(dcg * g1), red(dcg * g),
                red(dcv * v2), red(dcv * v1), red(dcv * v))

    w3, b3 = conv_w[None], conv_b.reshape(1, 1, -1)
    c = GLU_C
    res = _rowwise(fn, [(ug[None], c, 0), (uv[None], c, 0), (dh[None], c, 0)],
                   halos=[(ug[None], c, 0, "prev"), (uv[None], c, 0, "prev")],
                   pars=[(w3, c, 0), (w3, c, 2), (b3, c, 0), (b3, c, 2)],
                   outs=[(D_FF, c, 0, F32), (D_FF, c, 0, F32)], accs=[(1, D_FF, c, 0)] * 8, tm=256, ncol=2, name=name)
    dcg, dcv = res[0][0], res[1][0]
    dconv_b = jnp.concatenate([res[2].reshape(D_FF), res[3].reshape(D_FF)])
    dconv_w = jnp.concatenate([jnp.concatenate([res[4 + j].reshape(1, D_FF) for j in range(3)], axis=0),
                               jnp.concatenate([res[7 + j].reshape(1, D_FF) for j in range(3)], axis=0)], axis=1)
    return dcg, dcv, dconv_w, dconv_b


def _glu_bwd_b(dc, conv_w, half, *, name):
    def fn(i, nt, d, hd, w):
        return w[2:3] * d + w[1:2] * _shift_up(d, hd, i, nt, 1) + w[0:1] * _shift_up(d, hd, i, nt, 2)

    c = GLU_C
    return _rowwise(fn, [(dc[None], c, 0)], halos=[(dc[None], c, 0, "next")], pars=[(conv_w[None], c, 2 * half)],
                    outs=[(D_FF, c, 0, BF16)], tm=256, ncol=2, name=name)[0][0]


def _loss_and_grad(y, tgt, *, name):
    def fn(i, nt, yv, tv):
        err = yv - tv
        part = jnp.sum(jnp.sum(err * err, axis=0, keepdims=True), axis=1, keepdims=True) * (0.5 / D_MODEL)
        return err * (1.0 / D_MODEL), jnp.zeros((1, LANE), F32) + part

    dy, part = _rowwise(fn, [(y[None], D_MODEL, 0), (tgt[None], D_MODEL, 0)], outs=[(D_MODEL, D_MODEL, 0, F32)],
                        accs=[(1, LANE, LANE, 0)], tm=512, name=name)
    return dy[0], part.reshape(LANE)


def _lincomb(terms, coefs, *, name):
    def fn(i, nt, *vs):
        acc = coefs[0] * vs[0]
        for cf, v in zip(coefs[1:], vs[1:]):
            acc = acc + cf * v
        return acc

    c = terms[0].shape[-1]
    return _rowwise(fn, [(a[None], c, 0) for a in terms], outs=[(c, c, 0, F32)], tm=512, name=name)[0][0]


def _sum_rows(terms, *, tm, name, dtype=F32):
    def fn(i, nt, *vs):
        acc = vs[0]
        for v in vs[1:]:
            acc = acc + v
        return acc

    c = terms[0].shape[-1]
    return _rowwise(fn, [(t, c, 0) for t in terms], outs=[(c, c, 0, dtype)], tm=tm, name=name)[0]


def _head_sums(x):
    lo = _lane_lo()
    parts = []
    for j in range(x.shape[1] // LANE):
        blk = x[:, j * LANE:(j + 1) * LANE]
        s_lo = jnp.sum(jnp.where(lo, blk, 0.0), axis=1, keepdims=True)
        s_hi = jnp.sum(jnp.where(lo, 0.0, blk), axis=1, keepdims=True)
        parts.append(jnp.where(lo, s_lo, s_hi))
    return jnp.concatenate(parts, axis=1)


def _group_weights(l0, l1, l2):
    m = jnp.maximum(jnp.maximum(l0, l1), l2)
    es = [jnp.exp(l - m) for l in (l0, l1, l2)]
    inv = 1.0 / (es[0] + es[1] + es[2])
    return [e * inv for e in es]


def _combine_fwd(os_, lses, *, name):
    def fn(i, nt, o0, o1, o2, l0, l1, l2):
        w = _group_weights(l0, l1, l2)
        return w[0] * o0 + w[1] * o1 + w[2] * o2

    c = os_[0].shape[-1]
    return _rowwise(fn, [(a[None], c, 0) for a in list(os_) + list(lses)], outs=[(c, c, 0, BF16)], tm=512, name=name)[0][0]


def _combine_bwd(os_, lses, do_a, *, name):
    def fn(i, nt, o0, o1, o2, l0, l1, l2, da):
        ws = _group_weights(l0, l1, l2)
        dws = [_head_sums(da * o) for o in (o0, o1, o2)]
        mean = ws[0] * dws[0] + ws[1] * dws[1] + ws[2] * dws[2]
        return [w * da for w in ws] + [w * mean for w in ws]

    c = do_a.shape[-1]
    res = _rowwise(fn, [(a[None], c, 0) for a in list(os_) + list(lses) + [do_a]], outs=[(c, c, 0, BF16)] * 3 + [(c, c, 0, F32)] * 3,
                   tm=256, name=name)
    return [r[0] for r in res[0:3]], [r[0] for r in res[3:6]]


def _delta(do, o, *, name):
    def fn(i, nt, d, ov):
        return d, _head_sums(d * ov)

    c = do.shape[-1]
    res = _rowwise(fn, [(do[None], c, 0), (o[None], c, 0)], outs=[(c, c, 0, BF16), (c, c, 0, F32)], tm=512, name=name)
    return res[0][0], res[1][0]


def _rowdot(at, bt, *, name):
    def fn(a, b):
        return jnp.sum(a * b, axis=0, keepdims=True)

    return _lanewise(fn, [at, bt], [(1, F32)], tl=2048, name=name)[0]


def _adamw(w, g, m, v, *, name):
    c1 = 1.0 - ADAM_B1 ** ADAM_STEP
    c2 = 1.0 - ADAM_B2 ** ADAM_STEP

    def fn(i, nt, wv, gv, mv, vv):
        mn = ADAM_B1 * mv + (1.0 - ADAM_B1) * gv
        vn = ADAM_B2 * vv + (1.0 - ADAM_B2) * (gv * gv)
        delta = -ADAM_LR * ((mn / c1) / (jnp.sqrt(vn / c2) + ADAM_EPS) + ADAM_WD * wv)
        return delta, mn, vn

    r, c = w.shape
    rp = _ceil_to(r, 8)
    pad = lambda a: jnp.pad(a, ((0, rp - r), (0, 0))) if rp != r else a
    tm = rp
    for cand in (128, 64, 32, 16, 8):
        if rp % cand == 0:
            tm = cand
            break
    res = _rowwise(fn, [(pad(a)[None], c, 0) for a in (w, g, m, v)], outs=[(c, c, 0, F32)] * 3, tm=tm, name=name)
    return [x[0][:r] for x in res]


ANY = pl.BlockSpec(memory_space=pl.ANY)


def _place():
    x, y, c = lax.axis_index("x"), lax.axis_index("y"), lax.axis_index("c")
    chips = [(1 - x, y), (x, 1 - y), (1 - x, 1 - y)]
    return x, y, c, chips


def _allgather_weights(arrs):
    n = len(arrs)

    def body(*refs):
        ins, outs, send_sems, recv_sems = refs[:n], refs[n:2 * n], refs[2 * n], refs[2 * n + 1]
        x, y, c, chips = _place()
        j = 2 * x + y

        def cp(i, k, src, chip_idx, half, to):
            return pltpu.make_async_remote_copy(src_ref=src, dst_ref=outs[i].at[chip_idx, half], send_sem=send_sems.at[k],
                                                recv_sem=recv_sems.at[k], device_id=to, device_id_type=MESH)

        first, passed = [], []
        for i in range(n):
            for r, (cx, cy) in enumerate(chips):
                first.append(cp(i, 3 * i + r, ins[i].at[c], j, c, (cx, cy, c)))
                passed.append(cp(i, 3 * (n + i) + r, outs[i].at[2 * cx + cy, c], 2 * cx + cy, c, (x, y, 1 - c)))
        for d in first:
            d.start()
        for i in range(n):
            for r, (cx, cy) in enumerate(chips):
                cp(i, 3 * i + r, ins[i].at[c], 2 * cx + cy, c, (x, y, c)).wait_recv()
                passed[3 * i + r].start()
        for i in range(n):
            for r, (cx, cy) in enumerate(chips):
                cp(i, 3 * (n + i) + r, ins[i].at[c], 2 * cx + cy, 1 - c, (x, y, c)).wait_recv()
        for d in first + passed:
            d.wait_send()

    return pl.pallas_call(
        body, name="allgather_weights", in_specs=[ANY] * n, out_specs=[ANY] * n,
        out_shape=[jax.ShapeDtypeStruct((N_CHIP,) + a.shape, a.dtype) for a in arrs],
        scratch_shapes=[pltpu.SemaphoreType.DMA((6 * n,)), pltpu.SemaphoreType.DMA((6 * n,))],
    )(*arrs)


def _sibling_swap(gs):
    n = len(gs)

    def body(*refs):
        ins, outs, send_sems, recv_sems = refs[:n], refs[n:2 * n], refs[2 * n], refs[2 * n + 1]
        x, y, c, _ = _place()
        cps = [pltpu.make_async_remote_copy(src_ref=ins[i].at[1 - c], dst_ref=outs[i], send_sem=send_sems.at[i],
                                            recv_sem=recv_sems.at[i], device_id=(x, y, 1 - c), device_id_type=MESH)
               for i in range(n)]
        for d in cps:
            d.start()
        for d in cps:
            d.wait_recv()
        for d in cps:
            d.wait_send()

    return pl.pallas_call(
        body, name="grad_sibling_swap", in_specs=[ANY] * n, out_specs=[ANY] * n,
        out_shape=[jax.ShapeDtypeStruct(g.shape[1:], g.dtype) for g in gs],
        scratch_shapes=[pltpu.SemaphoreType.DMA((n,)), pltpu.SemaphoreType.DMA((n,))],
    )(*gs)


def _chip_scatter(ps):
    n = len(ps)

    def body(*refs):
        ins, outs, send_sems, recv_sems = refs[:n], refs[n:2 * n], refs[2 * n], refs[2 * n + 1]
        x, y, c, chips = _place()
        sends = []
        for i in range(n):
            for r, (cx, cy) in enumerate(chips):
                sends.append(pltpu.make_async_remote_copy(src_ref=ins[i].at[2 * cx + cy], dst_ref=outs[i].at[r], send_sem=send_sems.at[3 * i + r],
                                                          recv_sem=recv_sems.at[3 * i + r], device_id=(cx, cy, c), device_id_type=MESH))
        for d in sends:
            d.start()
        for d in sends:
            d.wait_recv()
        for d in sends:
            d.wait_send()

    return pl.pallas_call(
        body, name="grad_chip_scatter", in_specs=[ANY] * n, out_specs=[ANY] * n,
        out_shape=[jax.ShapeDtypeStruct((3,) + p.shape[1:], p.dtype) for p in ps],
        scratch_shapes=[pltpu.SemaphoreType.DMA((3 * n,)), pltpu.SemaphoreType.DMA((3 * n,))],
    )(*ps)


def _sibling_share(rs):
    n = len(rs)

    def body(*refs):
        ins, outs, send_sems, recv_sems = refs[:n], refs[n:2 * n], refs[2 * n], refs[2 * n + 1]
        x, y, c, _ = _place()
        cps = [pltpu.make_async_remote_copy(src_ref=ins[i], dst_ref=outs[i], send_sem=send_sems.at[i], recv_sem=recv_sems.at[i],
                                            device_id=(x, y, 1 - c), device_id_type=MESH) for i in range(n)]
        for d in cps:
            d.start()
        for d in cps:
            d.wait_recv()
        for d in cps:
            d.wait_send()

    return pl.pallas_call(
        body, name="grad_sibling_share", in_specs=[ANY] * n, out_specs=[ANY] * n,
        out_shape=[jax.ShapeDtypeStruct(r.shape, r.dtype) for r in rs],
        scratch_shapes=[pltpu.SemaphoreType.DMA((n,)), pltpu.SemaphoreType.DMA((n,))],
    )(*rs)


def _allreduce_small(s):
    rows, w = s.shape
    n_dev = 8

    def body(s_ref, out_ref, slots, send_sems, recv_sems):
        x, y, c, _ = _place()
        me = 4 * x + 2 * y + c
        slots[me] = s_ref[...]
        peers = []
        for r in range(1, n_dev):
            px = 1 - x if r & 4 else x
            py = 1 - y if r & 2 else y
            pc = 1 - c if r & 1 else c
            peers.append((px, py, pc))
        sends = [pltpu.make_async_remote_copy(src_ref=s_ref, dst_ref=slots.at[me], send_sem=send_sems.at[r], recv_sem=recv_sems.at[r],
                                              device_id=peer, device_id_type=MESH) for r, peer in enumerate(peers)]
        for d in sends:
            d.start()
        for r, (px, py, pc) in enumerate(peers):
            pltpu.make_async_remote_copy(src_ref=s_ref, dst_ref=slots.at[4 * px + 2 * py + pc], send_sem=send_sems.at[r],
                                         recv_sem=recv_sems.at[r], device_id=(x, y, c), device_id_type=MESH).wait_recv()
        for d in sends:
            d.wait_send()
        acc = slots[0]
        for k in range(1, n_dev):
            acc = acc + slots[k]
        out_ref[...] = acc

    vm = pl.BlockSpec(memory_space=pltpu.VMEM)
    return pl.pallas_call(
        body, name="allreduce_small", in_specs=[vm], out_specs=vm, out_shape=jax.ShapeDtypeStruct((rows, w), F32),
        scratch_shapes=[pltpu.VMEM((n_dev, rows, w), F32), pltpu.SemaphoreType.DMA((n_dev - 1,)), pltpu.SemaphoreType.DMA((n_dev - 1,))],
    )(s)


W_IN_SHARD = D_IN // N_CHIP
W_IN_ROWS_G = 2304
REDUCED = tuple(m for m in MATS if m[0] != "conv_w")
CONV_W_SIZE = 3 * 2 * D_FF


def _weight_send(name, a):
    if name == "w_in":
        return jnp.swapaxes(a, 1, 2).astype(BF16)
    return a if name == "conv_w" else a.astype(BF16)


def _full_weights(gathered, l):
    g = {k: v[:, l] for k, v in gathered.items()}
    s = g["w_in"].astype(F32).reshape(D_IN, D_MODEL)
    dup = lambda a: jnp.concatenate([a[0:64], a[0:64], a[64:128], a[64:128]], axis=0)
    o = ORIG
    wm_t = jnp.concatenate([s[o["gate"]:], s[o["a"]:o["a"] + A_COLS], s[o["bq"]:o["bk"]], dup(s[o["bk"]:o["bv"]]), dup(s[o["bv"]:o["cq"]]),
                            s[o["cq"]:o["gate"]], jnp.zeros((M_COLS - M_CDKV - (o["gate"] - o["cdkv"]), D_MODEL), F32)], axis=0).astype(BF16)
    wg_t = [s[o["a"] + gi * A_COLS:o["a"] + (gi + 1) * A_COLS].astype(BF16) for gi in (1, 2)]
    full = {name: jnp.moveaxis(g[name], 0, ax).reshape(shape) for name, shape, ax in MATS if name != "w_in"}
    uq = full["w_uq"].reshape(C_Q_RANK, N_HEADS, C_NOPE + C_ROPE)
    ukv = full["w_ukv"].reshape(C_KV_RANK, N_HEADS, 2 * C_NOPE)
    w_uq_p = _pad_lanes(uq).reshape(C_Q_RANK, N_HEADS * LANE)
    w_ukv_p = jnp.concatenate([_pad_lanes(ukv[:, :, :C_NOPE]).reshape(C_KV_RANK, N_HEADS * LANE),
                               ukv[:, :, C_NOPE:].reshape(C_KV_RANK, N_HEADS * HEAD_DIM)], axis=1)
    return {"wm_t": wm_t, "wg_t": wg_t, "w_uq_p": w_uq_p, "w_ukv_p": w_ukv_p, "w_branch": full["w_branch"], "w_out": full["w_out"],
            "wup_g": full["w_ffn_up"][:, :D_FF], "wup_v": full["w_ffn_up"][:, D_FF:], "conv_w": full["conv_w"],
            "w_ffn_down": full["w_ffn_down"]}


def _grad_send(name, g, shape, ax):
    if name == "w_in":
        return jnp.pad(g.reshape(N_CHIP, W_IN_SHARD, D_MODEL), ((0, 0), (0, W_IN_ROWS_G - W_IN_SHARD), (0, 0)))
    split = shape[:ax] + (N_CHIP, shape[ax] // N_CHIP) + shape[ax + 1:]
    return jnp.moveaxis(g.reshape(split), ax, 0)


def _grad_recv(name, r):
    return r[:W_IN_SHARD].T if name == "w_in" else r


def _pack_small(rel, small, conv_w, extra):
    parts = [rel.reshape(-1)]
    for l in range(DEPTH):
        for name in SMALL:
            parts.append(small[name][l].reshape(-1))
    parts += [conv_w.reshape(-1), extra]
    flat = jnp.concatenate(parts)
    rows = _ceil_to(-(-flat.shape[0] // LANE), 8)
    return jnp.pad(flat, (0, rows * LANE - flat.shape[0])).reshape(rows, LANE)


def _unpack_small(buf):
    flat = buf.reshape(-1)
    rel = flat[:REL_BUCKETS * 32].reshape(REL_BUCKETS, 32)
    off = REL_BUCKETS * 32
    small = {name: [] for name in SMALL}
    for l in range(DEPTH):
        for name in SMALL:
            n = SMALL_SIZES[name]
            small[name].append(flat[off:off + n])
            off += n
    conv_w = flat[off:off + DEPTH * CONV_W_SIZE].reshape(DEPTH, 3, 2 * D_FF)
    off += DEPTH * CONV_W_SIZE
    return rel, {k: jnp.stack(v) for k, v in small.items()}, conv_w, flat[off:off + LANE]


def _rows2d(a, lead):
    return a.reshape(a.shape[:lead] + (-1, a.shape[-1]))


def _row_tile(rows):
    for cand in (512, 256, 128, 64, 32, 16, 8):
        if rows % cand == 0:
            return cand
    raise ValueError(rows)


def _pair_add(g, got, core, *, name):
    g2, got2 = _rows2d(g, 1), _rows2d(got, 0)
    rows, c = got2.shape
    tm = _row_tile(rows)
    flag = jnp.zeros((1, 1, LANE), F32) + core.astype(F32)

    def fn(i, nt, a0, a1, b, f):
        return jnp.where(f[:, 0:1] == 0.0, a0, a1) + b

    stacked = g2.reshape(1, 2 * rows, c)
    out = _rowwise(fn, [(stacked, c, 0, 0), (stacked, c, 0, rows // tm), (got2[None], c, 0)], pars=[(flag, LANE, 0)],
                   outs=[(c, c, 0, BF16)], tm=tm, t=rows, name=name)[0][0]
    return out.reshape(got.shape)


def _chip_add(own, got, *, name):
    own2, got2 = _rows2d(own, 0), _rows2d(got, 1)
    rows, c = own2.shape
    tm = _row_tile(rows)

    def fn(i, nt, a, b0, b1, b2):
        return ((a.astype(F32) + b0.astype(F32)) + b1.astype(F32)) + b2.astype(F32)

    stacked = got2.reshape(1, 3 * rows, c)
    out = _rowwise(fn, [(own2[None], c, 0)] + [(stacked, c, 0, k * (rows // tm)) for k in range(3)],
                   outs=[(c, c, 0, F32)], tm=tm, t=rows, name=name)[0][0]
    return out.reshape(own.shape)


def _perm(a, d):
    if d == 1:
        return a
    t = a.shape[0]
    return jnp.swapaxes(a.reshape((t // d, d) + a.shape[1:]), 0, 1).reshape(a.shape)


def _unperm(a, d):
    if d == 1:
        return a
    t = a.shape[0]
    return jnp.swapaxes(a.reshape((d, t // d) + a.shape[1:]), 0, 1).reshape(a.shape)


def _pad_lanes(a, w=HP):
    return jnp.pad(a, [(0, 0)] * (a.ndim - 1) + [(0, w - a.shape[-1])])


def _heads_blocks(a, blk):
    t, h, _ = a.shape
    return jnp.transpose(_pad_lanes(a), (1, 0, 2)).astype(BF16).reshape(h, t // blk, blk, HP)


def _heads_blocks_t(a, blk):
    return jnp.swapaxes(_heads_blocks(a, blk), -1, -2)


def _from_blocks_t(a):
    h, n, d, blk = a.shape
    return jnp.transpose(a, (1, 3, 0, 2)).reshape(n * blk, h, d)


def _from_blocks(a):
    h, n, blk, d = a.shape
    return jnp.transpose(a, (1, 2, 0, 3)).reshape(n * blk, h, d)


def _to_hdt(a):
    return jnp.transpose(a, (1, 2, 0))


def _rope_tables(t):
    pos = jnp.arange(t, dtype=F32)
    inv_freq = ROPE_BASE ** (-jnp.arange(0, C_ROPE, 2, dtype=F32) / C_ROPE)
    ang = pos[:, None] * inv_freq[None, :]
    cos, sin = jnp.cos(ang), jnp.sin(ang)
    ones, zeros = jnp.ones((t, C_NOPE), F32), jnp.zeros((t, C_NOPE), F32)
    tail = LANE - C_NOPE - C_ROPE
    c = jnp.concatenate([ones, cos, cos, ones[:, :tail]], axis=1)
    s = jnp.concatenate([zeros, -sin, sin, zeros[:, :tail]], axis=1)
    return c, s


def _band_calls(t, proj, projs_g, sinks):
    none = jnp.full((N_HEADS,), NEG, F32)
    a0 = M_A0 // LANE
    calls = [(proj, (a0, a0 + 4, a0 + 8), t // BLK, BLK, False, none)]
    calls += [(pg, (0, 4, 8), t // (d * BLK), BLK, False, none) for pg, d in zip(projs_g, A_DILS[1:])]
    calls.append((proj, (M_BQ // LANE, M_BK // LANE, M_BV // LANE), t // BLK, BLK - 1, True, sinks.astype(F32)))
    return calls


def _layer_fwd(l, x, xb, w, p, biases, rope_cs):
    t = x.shape[0]
    n = f"l{l}_"
    xps = [_perm(xb, d) for d in A_DILS[1:]]
    proj = _mm(xb, w["wm_t"], tb=True, out_dtype=BF16, name=n + "proj")
    projs_g = [_mm(xp, wg, tb=True, out_dtype=BF16, tn=768, name=n + f"proj_g{i + 1}") for i, (xp, wg) in enumerate(zip(xps, w["wg_t"]))]
    s = {"xb": xb, "xps": xps, "proj": proj, "projs_g": projs_g}

    calls = _band_calls(t, proj, projs_g, p["sinks"])
    outs = [_band_fwd(src, offs, biases[i], sk, nb=nb, lim=lim, gqa=gqa, name=n + f"band{i}")
            for i, (src, offs, nb, lim, gqa, sk) in enumerate(calls)]
    os_ = [_unperm(outs[gi][0], d) for gi, d in enumerate(A_DILS)]
    lses = [_unperm(outs[gi][1], d) for gi, d in enumerate(A_DILS)]
    o_a = _combine_fwd(os_, lses, name=n + "combine_fwd")
    o_b_f, lse_b = outs[3]
    o_b = o_b_f.astype(BF16)
    s.update(os=os_, lses=lses, lses_p=[outs[gi][1] for gi in range(3)], o_b=o_b_f, lse_b=lse_b)

    rq = _rms_fwd(proj, C_Q_RANK, M_CQ // C_Q_RANK, p["q_norm_g"], name=n + "rms_q")
    rkv = _rms_fwd(proj, C_KV_RANK, M_CDKV // C_KV_RANK, p["kv_norm_g"], name=n + "rms_kv")
    q_cp = _mm(rq, w["w_uq_p"], out_dtype=BF16, name=n + "uq")
    kv_cp = _mm(rkv, w["w_ukv_p"], out_dtype=BF16, name=n + "ukv")
    q_full = _rope_slabs(q_cp, N_HEADS, rope_cs[0], rope_cs[1], name=n + "rope_q")
    k_full = _rope_slabs(kv_cp, N_HEADS, rope_cs[0], rope_cs[1], add=(proj, (M_CDKV + C_KV_RANK) // LANE), name=n + "rope_k")
    vt = jnp.transpose(kv_cp[:, N_HEADS * LANE:].T.reshape(N_HEADS // 2, LANE, t // TQ, TQ), (0, 2, 1, 3))
    o_c_f, lse_c = _mla_fwd(q_full, k_full, vt, name=n + "mla_fwd")
    o_c = o_c_f.astype(BF16)
    s.update(rq=rq, rkv=rkv, q_full=q_full, k_full=k_full, kv_cp=kv_cp, lse_c=lse_c, o_c=o_c_f)

    obs = [o_a, o_b, o_c]
    ys = [_mm(o, w["w_branch"][i], name=n + f"branch{i}") for i, o in enumerate(obs)]
    merged = _merge_fwd(proj, p["b_gate"], ys, name=n + "merge")
    mix = _mm(merged, w["w_out"], name=n + "out")
    x1f, x1b, z1 = _ln_fwd(x, mix, p["ln1_g"], p["ln1_b"], name=n + "ln1")
    s.update(obs=obs, ys=ys, merged=merged, z1=z1, x1b=x1b)

    ug = _mm(x1b, w["wup_g"], tn=1408, name=n + "up_g")
    uv = _mm(x1b, w["wup_v"], tn=1408, name=n + "up_v")
    h = _glu_fwd(ug, uv, w["conv_w"], p["conv_b"], name=n + "glu")
    ff = _mm(h, w["w_ffn_down"], tk=1408, name=n + "down")
    x2f, x2b, z2 = _ln_fwd(x1f, ff, p["ln2_g"], p["ln2_b"], name=n + "ln2")
    s.update(ug=ug, uv=uv, h=h, z2=z2)
    return x2f, x2b, s


def _layer_bwd(l, s, dys, coefs, w, p, biases, rope_cs):
    n = f"l{l}b_"
    t = s["z2"].shape[0]
    gw, gs = {}, {}

    dz2, dz2b, gs["ln2_g"], gs["ln2_b"] = _ln_bwd(s["z2"], p["ln2_g"], dys, coefs, name=n + "ln2")
    dh = _mm(dz2b, w["w_ffn_down"], tb=True, tn=1408, name=n + "d_h")
    gw["w_ffn_down"] = _mm(s["h"], dz2b, ta=True, tm=1408, tk=1024, name=n + "g_down")
    dcg, dcv, gw["conv_w"], gs["conv_b"] = _glu_bwd_a(s["ug"], s["uv"], w["conv_w"], p["conv_b"], dh, name=n + "glu_a")
    dug = _glu_bwd_b(dcg, w["conv_w"], 0, name=n + "glu_bg")
    duv = _glu_bwd_b(dcv, w["conv_w"], 1, name=n + "glu_bv")
    dx1_g = _mm(dug, w["wup_g"], tb=True, tk=1408, name=n + "d_x1g")
    dx1_v = _mm(duv, w["wup_v"], tb=True, tk=1408, name=n + "d_x1v")
    gw["w_ffn_up"] = jnp.concatenate([_mm(s["x1b"], dug, ta=True, tn=1408, tk=1024, name=n + "g_upg"),
                                      _mm(s["x1b"], duv, ta=True, tn=1408, tk=1024, name=n + "g_upv")], axis=1)

    dz1, dz1b, gs["ln1_g"], gs["ln1_b"] = _ln_bwd(s["z1"], p["ln1_g"], [dz2, dx1_g, dx1_v], [ALPHA, 1.0, 1.0], name=n + "ln1")
    dmerged = _mm(dz1b, w["w_out"], tb=True, name=n + "d_merged")
    gw["w_out"] = _mm(s["merged"], dz1b, ta=True, name=n + "g_out")
    dys_b, dgp, gs["b_gate"] = _merge_bwd(s["proj"], p["b_gate"], s["ys"], dmerged, name=n + "merge")
    dos = [_mm(dy, w["w_branch"][i], tb=True, name=n + f"d_o{i}") for i, dy in enumerate(dys_b)]
    gw["w_branch"] = jnp.stack([_mm(o, dy, ta=True, name=n + f"g_branch{i}") for i, (o, dy) in enumerate(zip(s["obs"], dys_b))])

    do_gs, dpr_gs = _combine_bwd(s["os"], s["lses"], dos[0], name=n + "combine")
    do_b, dpr_b = _delta(dos[1], s["o_b"], name=n + "delta_b")
    do_list = [_perm(a, d) for a, d in zip(do_gs, A_DILS)] + [do_b]
    dpr_list = [_perm(a, d) for a, d in zip(dpr_gs, A_DILS)] + [dpr_b]
    lse_list = s["lses_p"] + [s["lse_b"]]
    calls = _band_calls(t, s["proj"], s["projs_g"], p["sinks"])
    band = [_band_bwd(src, offs, do_list[i], lse_list[i], dpr_list[i], biases[i], sk, nb=nb, lim=lim, gqa=gqa, name=n + f"band{i}")
            for i, (src, offs, nb, lim, gqa, sk) in enumerate(calls)]
    gs["sinks"] = band[3][4][:, 0, 0]
    ds_sum = jnp.concatenate([b_[3] for b_ in band], axis=0)

    do_c, delta_c = _delta(dos[2], s["o_c"], name=n + "delta_c")
    dq, dk, dv = _mla_bwd(s["q_full"], s["k_full"], s["kv_cp"], do_c, s["lse_c"], delta_c, name=n + "mla")
    dq_cp = _rope_slabs(dq, N_HEADS, rope_cs[0], -rope_cs[1], name=n + "rope_q")
    dk_sum = _rowwise(lambda i, nt, *vs: sum(vs[1:], vs[0]), [(dk[None], LANE, hh) for hh in range(N_HEADS)],
                      outs=[(LANE, LANE, 0, F32)], tm=1024, name=n + "krope_sum")[0][0]
    dkr = _rope_slabs(dk_sum, 1, rope_cs[0], -rope_cs[1], to_front=True, name=n + "rope_k")
    dkv_cp = jnp.concatenate([dk, dv], axis=1)
    d_rq = _mm(dq_cp, w["w_uq_p"], tb=True, name=n + "d_rq")
    d_rkv = _mm(dkv_cp, w["w_ukv_p"], tb=True, name=n + "d_rkv")
    g_uq = _mm(s["rq"], dq_cp, ta=True, name=n + "g_uq")
    g_ukv = _mm(s["rkv"], dkv_cp, ta=True, name=n + "g_ukv")
    gw["w_uq"] = g_uq.reshape(C_Q_RANK, N_HEADS, LANE)[:, :, :C_NOPE + C_ROPE].reshape(C_Q_RANK, -1)
    kw = N_HEADS * LANE
    gw["w_ukv"] = jnp.concatenate([g_ukv[:, :kw].reshape(C_KV_RANK, N_HEADS, LANE)[:, :, :C_NOPE],
                                   g_ukv[:, kw:].reshape(C_KV_RANK, N_HEADS, HEAD_DIM)], axis=2).reshape(C_KV_RANK, -1)
    dcq, gs["q_norm_g"] = _rms_bwd(s["proj"], C_Q_RANK, M_CQ // C_Q_RANK, p["q_norm_g"], d_rq, name=n + "rms_q")
    dckv, gs["kv_norm_g"] = _rms_bwd(s["proj"], C_KV_RANK, M_CDKV // C_KV_RANK, p["kv_norm_g"], d_rkv, name=n + "rms_kv")
    dcdkv = jnp.concatenate([dckv, dkr], axis=1)

    dproj = jnp.concatenate(dgp + list(band[0][:3]) + list(band[3][:3]) + [dcq, dcdkv], axis=1)
    dprojs_g = [jnp.concatenate(band[gi][:3], axis=1) for gi in (1, 2)]
    dx_terms = [_mm(dproj, w["wm_t"], tk=1024, name=n + "d_x")]
    dx_terms += [_unperm(_mm(dp, wg, tk=768, name=n + f"d_x_g{i + 1}"), d) for i, (dp, wg, d) in enumerate(zip(dprojs_g, w["wg_t"], A_DILS[1:]))]
    g_main = _mm(dproj, s["xb"], ta=True, name=n + "g_in")
    g_groups = [_mm(dp, xp, ta=True, tm=768, name=n + f"g_in_g{i + 1}") for i, (xp, dp) in enumerate(zip(s["xps"], dprojs_g))]
    fold = lambda a, tag: _sum_rows([a.reshape(2, 2, HEAD_DIM, D_MODEL)[:, j] for j in range(2)], tm=HEAD_DIM,
                                    name=n + "g_fold_" + tag).reshape(2 * HEAD_DIM, D_MODEL)
    gw["w_in"] = jnp.concatenate([g_main[M_A0:M_BQ], g_groups[0], g_groups[1], g_main[M_BQ:M_BK], fold(g_main[M_BK:M_BV], "k"),
                                  fold(g_main[M_BV:M_CQ], "v"), g_main[M_CQ:M_CDKV + C_KV_RANK + C_ROPE], g_main[M_GATE:M_A0]], axis=0)
    return [dz1] + dx_terms, [ALPHA, 1.0, 1.0, 1.0], gw, gs, ds_sum


def _local_step(x, target, ws, rel_table, small):
    t = x.shape[0]
    ps = [{k: small[k][l] for k in SMALL} for l in range(DEPTH)]
    bucket = _bucket_index()
    bias_all = _bias_lookup(bucket, rel_table.T, name="bias_lookup").reshape(4, N_HEADS, BLK, 2 * BLK)
    biases = [bias_all[i] for i in range(4)]
    rope_cs = _rope_tables(t)

    saved, h, hb = [], x, x.astype(BF16)
    for l in range(DEPTH):
        h, hb, s = _layer_fwd(l, h, hb, ws[l], ps[l], biases, rope_cs)
        saved.append(s)
    dy, loss_part = _loss_and_grad(h, target, name="loss")

    dys, coefs = [dy], [1.0]
    gws, gss, dss = [None] * DEPTH, [None] * DEPTH, [None] * DEPTH
    for l in reversed(range(DEPTH)):
        dys, coefs, gws[l], gss[l], dss[l] = _layer_bwd(l, saved[l], dys, coefs, ws[l], ps[l], biases, rope_cs)
    grad_x = _lincomb(dys, coefs, name="grad_x")
    npos = 2 * BLK * BLK
    g_rel = _bias_grad(bucket, dss[0].reshape(4 * N_HEADS, npos), dss[1].reshape(4 * N_HEADS, npos), name="bias_grad").T
    gsmall = {k: jnp.stack([gss[l][k] for l in range(DEPTH)]) for k in SMALL}
    return loss_part, grad_x, gws, gsmall, g_rel


def kernel(x, rel_table, w_in, b_gate, sinks, q_norm_g, kv_norm_g, w_uq, w_ukv, w_branch, w_out, ln1_g, ln1_b, w_ffn_up, conv_w, conv_b, w_ffn_down, ln2_g, ln2_b, loss_target, m_rel_table, m_w_in, m_b_gate, m_sinks, m_q_norm_g, m_kv_norm_g, m_w_uq, m_w_ukv, m_w_branch, m_w_out, m_ln1_g, m_ln1_b, m_w_ffn_up, m_conv_w, m_conv_b, m_w_ffn_down, m_ln2_g, m_ln2_b, v_rel_table, v_w_in, v_b_gate, v_sinks, v_q_norm_g, v_kv_norm_g, v_w_uq, v_w_ukv, v_w_branch, v_w_out, v_ln1_g, v_ln1_b, v_w_ffn_up, v_conv_w, v_conv_b, v_w_ffn_down, v_ln2_g, v_ln2_b):
    wts = dict(rel_table=rel_table, w_in=w_in, b_gate=b_gate, sinks=sinks, q_norm_g=q_norm_g, kv_norm_g=kv_norm_g, w_uq=w_uq,
               w_ukv=w_ukv, w_branch=w_branch, w_out=w_out, ln1_g=ln1_g, ln1_b=ln1_b, w_ffn_up=w_ffn_up, conv_w=conv_w,
               conv_b=conv_b, w_ffn_down=w_ffn_down, ln2_g=ln2_g, ln2_b=ln2_b)
    ms = dict(rel_table=m_rel_table, w_in=m_w_in, b_gate=m_b_gate, sinks=m_sinks, q_norm_g=m_q_norm_g, kv_norm_g=m_kv_norm_g,
              w_uq=m_w_uq, w_ukv=m_w_ukv, w_branch=m_w_branch, w_out=m_w_out, ln1_g=m_ln1_g, ln1_b=m_ln1_b, w_ffn_up=m_w_ffn_up,
              conv_w=m_conv_w, conv_b=m_conv_b, w_ffn_down=m_w_ffn_down, ln2_g=m_ln2_g, ln2_b=m_ln2_b)
    vs = dict(rel_table=v_rel_table, w_in=v_w_in, b_gate=v_b_gate, sinks=v_sinks, q_norm_g=v_q_norm_g, kv_norm_g=v_kv_norm_g,
              w_uq=v_w_uq, w_ukv=v_w_ukv, w_branch=v_w_branch, w_out=v_w_out, ln1_g=v_ln1_g, ln1_b=v_ln1_b, w_ffn_up=v_w_ffn_up,
              conv_w=v_conv_w, conv_b=v_conv_b, w_ffn_down=v_w_ffn_down, ln2_g=v_ln2_g, ln2_b=v_ln2_b)

    core = lax.axis_index("c")
    chip = 2 * lax.axis_index("x") + lax.axis_index("y")

    names = [name for name, _, _ in MATS]
    sent = [_weight_send(name, wts[name]) for name in names]
    got = _allgather_weights(sent)
    gathered = {name: lax.dynamic_update_slice(g, s[None], (chip,) + (0,) * s.ndim) for name, g, s in zip(names, got, sent)}
    ws = [_full_weights(gathered, l) for l in range(DEPTH)]

    small = {k: wts[k] for k in SMALL}
    loss_part, grad_x, gws, gsmall, g_rel = _local_step(x[0], loss_target[0], ws, rel_table, small)

    rnames = [name for name, _, _ in REDUCED]
    gsend = [jnp.stack([_grad_send(name, gws[l][name], shape, ax) for l in range(DEPTH)]) for name, shape, ax in REDUCED]
    theirs = _sibling_swap(gsend)
    pairs = [_pair_add(g, t_, core, name="grad_pair_" + name) for name, g, t_ in zip(rnames, gsend, theirs)]
    arrived = _chip_scatter(pairs)
    reduced = [_chip_add(lax.dynamic_index_in_dim(p, chip, 0, keepdims=False), a, name="grad_chip_" + name)
               for name, p, a in zip(rnames, pairs, arrived)]
    others = _sibling_share(reduced)
    gshard = {}
    for name, mine, other in zip(rnames, reduced, others):
        layers = [jnp.where(core == l, mine, other) for l in range(DEPTH)]
        gshard[name] = jnp.stack([_grad_recv(name, a) for a in layers])

    conv_w_full = jnp.stack([gws[l]["conv_w"] for l in range(DEPTH)])
    small_red = _allreduce_small(_pack_small(g_rel, gsmall, conv_w_full, loss_part))
    g_rel_r, gsmall_r, conv_w_r, loss_vec = _unpack_small(small_red)
    loss = loss_vec[0]
    shard_w = 2 * D_FF // N_CHIP
    gshard["conv_w"] = lax.dynamic_slice_in_dim(conv_w_r, chip * shard_w, shard_w, axis=2)

    grads = dict(gshard)
    grads.update(gsmall_r)
    grads["rel_table"] = g_rel_r
    deltas, new_m, new_v = {}, {}, {}
    for name, _, _ in MATS:
        shp = wts[name].shape
        v2 = lambda a: a.reshape(-1, shp[-1])
        d_, m_, v_ = _adamw(v2(wts[name]), v2(grads[name]), v2(ms[name]), v2(vs[name]), name="adamw_" + name)
        deltas[name], new_m[name], new_v[name] = d_.reshape(shp), m_.reshape(shp), v_.reshape(shp)
    zero, none = jnp.zeros((LANE,), F32), jnp.zeros((0,), F32)
    sw = _pack_small(wts["rel_table"], {k: wts[k] for k in SMALL}, none, zero)
    sm = _pack_small(ms["rel_table"], {k: ms[k] for k in SMALL}, none, zero)
    sv = _pack_small(vs["rel_table"], {k: vs[k] for k in SMALL}, none, zero)
    sg = _pack_small(g_rel_r, gsmall_r, none, zero)
    sd, smn, svn = _adamw(sw, sg, sm, sv, name="adamw_small")
    for res, buf in ((deltas, sd), (new_m, smn), (new_v, svn)):
        rel_, sm_ = _unpack_small(jnp.pad(buf, ((0, small_red.shape[0] - buf.shape[0]), (0, 0))))[:2]
        res["rel_table"] = rel_
        res.update(sm_)

    return (loss, grad_x[None], *[grads[k] for k in WEIGHT_ORDER], *[deltas[k] for k in WEIGHT_ORDER],
            *[new_m[k] for k in WEIGHT_ORDER], *[new_v[k] for k in WEIGHT_ORDER])
```

```python
import math

import jax
import jax.numpy as jnp
from jax import lax
from jax.experimental import pallas as pl
from jax.experimental.pallas import tpu as pltpu

F32 = jnp.float32
BF16 = jnp.bfloat16
MESH = pl.DeviceIdType.MESH

D_MODEL = 1024
DEPTH = 2
HEAD_DIM = 64
N_HEADS = 8
A_DILS = (1, 4, 16)
C_Q_RANK = 256
C_KV_RANK = 128
C_NOPE = 64
C_ROPE = 32
ROPE_BASE = 10000.0
REL_BUCKETS = 32
REL_MAX_DIST = 2048
D_FF = 2816
ALPHA = (2 * DEPTH) ** 0.25
LN_EPS = 1e-5
RMS_EPS = 1e-6
NEG = -1e30
ADAM_LR, ADAM_B1, ADAM_B2, ADAM_EPS, ADAM_WD, ADAM_STEP = 0.001, 0.9, 0.999, 1e-08, 0.01, 10

VMEM_LIMIT_BYTES = 56 * 1024 * 1024
LANE = 128
BLK = 128
TQ = 512
HP = 128
TM_TOKENS = 2048
HALO = 16
BAND_UNROLL = 16

D_IN = 8864
A_COLS = 3 * N_HEADS * HEAD_DIM
ORIG = {"a": 0, "bq": 4608, "bk": 5120, "bv": 5248, "cq": 5376, "cdkv": 5632, "gate": 5792}
M_GATE, M_A0, M_BQ, M_BK, M_BV, M_CQ, M_CDKV, M_COLS = 0, 3072, 4608, 5120, 5376, 5632, 5888, 6144

N_CHIP = 4
MATS = (
    ("w_in", (D_MODEL, D_IN), 1),
    ("w_uq", (C_Q_RANK, 768), 1),
    ("w_ukv", (C_KV_RANK, 1024), 1),
    ("w_branch", (3, 512, D_MODEL), 2),
    ("w_out", (D_MODEL, D_MODEL), 0),
    ("w_ffn_up", (D_MODEL, 2 * D_FF), 1),
    ("conv_w", (3, 2 * D_FF), 1),
    ("w_ffn_down", (D_FF, D_MODEL), 0),
)
SMALL = ("b_gate", "sinks", "q_norm_g", "kv_norm_g", "ln1_g", "ln1_b", "conv_b", "ln2_g", "ln2_b")
SMALL_SIZES = {"b_gate": 3072, "sinks": 8, "q_norm_g": 256, "kv_norm_g": 128, "ln1_g": 1024, "ln1_b": 1024,
               "conv_b": 5632, "ln2_g": 1024, "ln2_b": 1024}
WEIGHT_ORDER = ("rel_table", "w_in", "b_gate", "sinks", "q_norm_g", "kv_norm_g", "w_uq", "w_ukv", "w_branch",
                "w_out", "ln1_g", "ln1_b", "w_ffn_up", "conv_w", "conv_b", "w_ffn_down", "ln2_g", "ln2_b")


def _cparams(sem):
    return pltpu.CompilerParams(dimension_semantics=sem, vmem_limit_bytes=VMEM_LIMIT_BYTES)


def _ceil_to(n, m):
    return -(-n // m) * m


def _pick(n, target):
    if n <= target:
        return n
    best = None
    for t in range(LANE, target + 1, LANE):
        if n % t == 0:
            best = t
    assert best is not None, (n, target)
    return best


def _mm(a, b, *, ta=False, tb=False, out_dtype=F32, tm=1024, tn=1024, tk=2048, name):
    assert not (ta and tb)
    k, m = a.shape[::-1] if not ta else a.shape
    n = b.shape[0] if tb else b.shape[1]
    assert (b.shape[1] if tb else b.shape[0]) == k
    tm, tn, tk = _pick(m, tm), _pick(n, tn), _pick(k, tk)
    nk = k // tk
    dn = (((0 if ta else 1,), (1 if tb else 0,)), ((), ()))

    def body(a_ref, b_ref, o_ref, acc_ref):
        part = lax.dot_general(a_ref[...].astype(BF16), b_ref[...].astype(BF16), dn, preferred_element_type=F32)
        if nk == 1:
            o_ref[...] = part.astype(o_ref.dtype)
        else:
            kk = pl.program_id(2)

            @pl.when(kk == 0)
            def _():
                acc_ref[...] = part

            @pl.when(kk > 0)
            def _():
                acc_ref[...] += part

            @pl.when(kk == nk - 1)
            def _():
                o_ref[...] = acc_ref[...].astype(o_ref.dtype)

    a_spec = pl.BlockSpec((tk, tm), lambda i, j, kk: (kk, i)) if ta else pl.BlockSpec((tm, tk), lambda i, j, kk: (i, kk))
    b_spec = pl.BlockSpec((tn, tk), lambda i, j, kk: (j, kk)) if tb else pl.BlockSpec((tk, tn), lambda i, j, kk: (kk, j))
    return pl.pallas_call(
        body, name=name, grid=(m // tm, n // tn, nk),
        in_specs=[a_spec, b_spec],
        out_specs=pl.BlockSpec((tm, tn), lambda i, j, kk: (i, j)),
        out_shape=jax.ShapeDtypeStruct((m, n), out_dtype),
        scratch_shapes=[pltpu.VMEM((tm, tn) if nk > 1 else (8, LANE), F32)],
        compiler_params=_cparams(("parallel", "parallel", "arbitrary")),
    )(a, b)


def _rowwise(fn, rows, *, pars=(), halos=(), outs=(), accs=(), tm, name, ncol=1, t=None):
    nb = rows[0][0].shape[0]
    t = rows[0][0].shape[1] if t is None else t
    tm = min(tm, t)
    assert t % tm == 0 and tm % 8 == 0
    nt = t // tm
    in_specs, args = [], []
    for spec in rows:
        arr, c, off = spec[:3]
        rb = spec[3] if len(spec) > 3 else 0
        in_specs.append(pl.BlockSpec((1, tm, c), lambda b, cc, i, off=off, rb=rb: (b, i + rb, off + cc)))
        args.append(arr)
    for arr, c, off, kind in halos:
        if kind == "prev":
            im = lambda b, cc, i, off=off: (b, jnp.maximum(i * (tm // HALO) - 1, 0), off + cc)
        else:
            im = lambda b, cc, i, off=off: (b, jnp.minimum((i + 1) * (tm // HALO), t // HALO - 1), off + cc)
        in_specs.append(pl.BlockSpec((1, HALO, c), im))
        args.append(arr)
    for arr, c, off in pars:
        bp, r = arr.shape[:2]
        if bp > 1:
            im = lambda b, cc, i, off=off: (b, 0, off + cc)
        else:
            im = lambda b, cc, i, off=off: (0, 0, off + cc)
        in_specs.append(pl.BlockSpec((1, r, c), im))
        args.append(arr)
    out_specs, out_shapes = [], []
    for ctot, c, off, dt in outs:
        out_specs.append(pl.BlockSpec((1, tm, c), lambda b, cc, i, off=off: (b, i, off + cc)))
        out_shapes.append(jax.ShapeDtypeStruct((nb, t, ctot), dt))
    for r, ctot, c, off in accs:
        out_specs.append(pl.BlockSpec((1, r, c), lambda b, cc, i, off=off: (b, 0, off + cc)))
        out_shapes.append(jax.ShapeDtypeStruct((nb, r, ctot), F32))
    n_in, n_out = len(args), len(outs)

    def body(*refs):
        i = pl.program_id(2)
        res = fn(i, nt, *[r[0].astype(F32) for r in refs[:n_in]])
        if not isinstance(res, (tuple, list)):
            res = (res,)
        for o_ref, val in zip(refs[n_in:n_in + n_out], res[:n_out]):
            o_ref[0] = val.astype(o_ref.dtype)
        for a_ref, val in zip(refs[n_in + n_out:], res[n_out:]):
            @pl.when(i == 0)
            def _(a_ref=a_ref, val=val):
                a_ref[0] = val

            @pl.when(i > 0)
            def _(a_ref=a_ref, val=val):
                a_ref[0] += val

    res = pl.pallas_call(
        body, name=name, grid=(nb, ncol, nt), in_specs=in_specs, out_specs=out_specs, out_shape=out_shapes,
        compiler_params=_cparams(("parallel", "parallel", "arbitrary")),
    )(*args)
    return res


def _dot(a, b):
    return lax.dot_general(a, b, (((1,), (0,)), ((), ())), preferred_element_type=F32)


def _dot_nt(a, b):
    return lax.dot_general(a, b, (((1,), (1,)), ((), ())), preferred_element_type=F32)


def _dot_tn(a, b):
    return lax.dot_general(a, b, (((0,), (0,)), ((), ())), preferred_element_type=F32)


BAND_ROWS = BAND_UNROLL * BLK


def _rows(parts):
    return jnp.concatenate(parts, axis=0)


def _lane_lo():
    return lax.broadcasted_iota(jnp.int32, (1, LANE), 1) < HEAD_DIM


def _blocks(a):
    return [a[i * BLK:(i + 1) * BLK] for i in range(BAND_UNROLL)]


def _band_operands(g, k_ref, v_ref):
    start = pl.multiple_of(g * BAND_ROWS, BAND_ROWS)
    pstart = pl.multiple_of(jnp.maximum(g * BAND_ROWS - BLK, 0), BLK)
    out = []
    for ref in (k_ref, v_ref):
        cur = ref[pl.ds(start, BAND_ROWS), :]
        raw = ref[pl.ds(pstart, BAND_ROWS), :]
        shifted = _rows([jnp.zeros((BLK, LANE), raw.dtype), raw[:BAND_ROWS - BLK]])
        out += [_blocks(cur), _blocks(jnp.where(g == 0, shifted, raw))]
    return out


def _band_scores(g, qa, kc, kp, b_ref, a, nb, lim):
    scale = HEAD_DIM ** -0.5
    qi = jnp.bitwise_and(lax.broadcasted_iota(jnp.int32, (BAND_ROWS, BLK), 0), BLK - 1)
    ki = lax.broadcasted_iota(jnp.int32, (BAND_ROWS, BLK), 1)
    tile = lambda blk: _rows([blk] * BAND_UNROLL)
    firsts = []
    for i in range(BAND_UNROLL):
        if nb >= BAND_UNROLL:
            val = jnp.where(lax.rem(g * BAND_UNROLL, nb) == 0, NEG, 0.0).astype(F32) if i == 0 else 0.0
        else:
            val = NEG if i % nb == 0 else 0.0
        firsts.append(jnp.zeros((BLK, 1), F32) + val)
    sc = _rows([_dot_nt(q, k) for q, k in zip(qa, kc)]) * scale + tile(b_ref[a, :, BLK:2 * BLK])
    sp = _rows([_dot_nt(q, k) for q, k in zip(qa, kp)]) * scale + tile(b_ref[a, :, 0:BLK])
    return jnp.where(ki <= qi, sc, NEG), jnp.where((BLK + qi - ki) <= lim, sp, NEG) + _rows(firsts)


def _band_fwd(src, offs, bias, sinks, *, nb, lim, gqa, name):
    t = src.shape[0]
    assert t % BAND_ROWS == 0 and (nb % BAND_UNROLL == 0 or BAND_UNROLL % nb == 0)
    qo, ko, vo = offs
    share = 2 if gqa else 1

    def body(sink_ref, q_ref, k_ref, v_ref, b_ref, o_ref, lse_ref):
        hp, g = pl.program_id(0), pl.program_id(1)
        lo = _lane_lo()
        q2 = q_ref[...]
        kc, kp, vc, vp = _band_operands(g, k_ref, v_ref)
        outs, lses = [], []
        for a in range(2):
            sink = sink_ref[2 * hp + a]
            qa = _blocks(jnp.where(lo if a == 0 else jnp.logical_not(lo), q2, jnp.zeros_like(q2)))
            sc, sp = _band_scores(g, qa, kc, kp, b_ref, a, nb, lim)
            m = jnp.maximum(jnp.maximum(jnp.max(sc, axis=1, keepdims=True), jnp.max(sp, axis=1, keepdims=True)), sink)
            pc, pp = jnp.exp(sc - m), jnp.exp(sp - m)
            l = jnp.sum(pc, axis=1, keepdims=True) + jnp.sum(pp, axis=1, keepdims=True) + jnp.exp(sink - m)
            inv = 1.0 / l
            pc_b, pp_b = _blocks((pc * inv).astype(BF16)), _blocks((pp * inv).astype(BF16))
            outs.append(_rows([_dot(pc_b[i], vc[i]) + _dot(pp_b[i], vp[i]) for i in range(BAND_UNROLL)]))
            lses.append(m + jnp.log(l))
        o_ref[...] = jnp.where(lo, outs[0], outs[1])
        lse_ref[...] = jnp.where(lo, lses[0], lses[1])

    slab = lambda off: pl.BlockSpec((BAND_ROWS, LANE), lambda hp, g, off=off: (g, off + hp))
    whole = lambda off: pl.BlockSpec((t, LANE), lambda hp, g, off=off: (0, off + hp // share))
    return pl.pallas_call(
        body, name=name, grid=(N_HEADS // 2, t // BAND_ROWS),
        in_specs=[pl.BlockSpec(memory_space=pltpu.SMEM), slab(qo), whole(ko), whole(vo),
                  pl.BlockSpec((2, BLK, 2 * BLK), lambda hp, g: (hp, 0, 0))],
        out_specs=[slab(0), slab(0)],
        out_shape=[jax.ShapeDtypeStruct((t, N_HEADS * HEAD_DIM), F32)] * 2,
        compiler_params=_cparams(("parallel", "parallel")),
    )(sinks, src, src, src, bias)


def _band_bwd(src, offs, do, lse, dpr, bias, sinks, *, nb, lim, gqa, name):
    t = src.shape[0]
    qo, ko, vo = offs
    share = 2 if gqa else 1
    nstep = t // BAND_ROWS
    scale = HEAD_DIM ** -0.5

    def fold(a):
        acc = a[0:BLK]
        for i in range(1, BAND_UNROLL):
            acc = acc + a[i * BLK:(i + 1) * BLK]
        return acc

    def body(sink_ref, q_ref, k_ref, v_ref, do_ref, lse_ref, dpr_ref, b_ref,
             dq_ref, dk_ref, dv_ref, ds_ref, dsink_ref, dk_acc, dv_acc):
        hp, g = pl.program_id(0), pl.program_id(1)
        lo = _lane_lo()
        hi = jnp.logical_not(lo)

        @pl.when(jnp.logical_and(g == 0, lax.rem(hp, share) == 0))
        def _():
            dk_acc[...] = jnp.zeros_like(dk_acc)
            dv_acc[...] = jnp.zeros_like(dv_acc)

        @pl.when(g == 0)
        def _():
            ds_ref[...] = jnp.zeros_like(ds_ref)
            dsink_ref[...] = jnp.zeros_like(dsink_ref)

        q2, do2, lse2, dpr2 = q_ref[...], do_ref[...], lse_ref[...], dpr_ref[...]
        lse_sw, dpr_sw = pltpu.roll(lse2, HEAD_DIM, axis=1), pltpu.roll(dpr2, HEAD_DIM, axis=1)
        kc, kp, vc, vp = _band_operands(g, k_ref, v_ref)
        dqs, dk_cur, dk_prev, dv_cur, dv_prev = [], None, None, None, None
        for a in range(2):
            sink = sink_ref[2 * hp + a]
            mine = lo if a == 0 else hi
            qa = _blocks(jnp.where(mine, q2, jnp.zeros_like(q2)))
            doa = _blocks(jnp.where(mine, do2, jnp.zeros_like(do2)))
            lse_a, dpr_a = jnp.where(mine, lse2, lse_sw), jnp.where(mine, dpr2, dpr_sw)
            sc, sp = _band_scores(g, qa, kc, kp, b_ref, a, nb, lim)
            pc, pp = jnp.exp(sc - lse_a), jnp.exp(sp - lse_a)
            dsc = pc * (_rows([_dot_nt(d, v) for d, v in zip(doa, vc)]) - dpr_a)
            dsp = pp * (_rows([_dot_nt(d, v) for d, v in zip(doa, vp)]) - dpr_a)
            ds_ref[a, :, BLK:2 * BLK] += fold(dsc)
            ds_ref[a, :, 0:BLK] += fold(dsp)
            dsink_ref[a] -= jnp.sum(jnp.exp(sink - lse_a) * dpr_a, axis=0, keepdims=True)
            dsc_b, dsp_b = _blocks((dsc * scale).astype(BF16)), _blocks((dsp * scale).astype(BF16))
            pc_b, pp_b = _blocks(pc.astype(BF16)), _blocks(pp.astype(BF16))
            dqs.append(_rows([_dot(dsc_b[i], kc[i]) + _dot(dsp_b[i], kp[i]) for i in range(BAND_UNROLL)]))
            parts = [_rows([_dot_tn(x[i], y[i]) for i in range(BAND_UNROLL)])
                     for x, y in ((dsc_b, qa), (dsp_b, qa), (pc_b, doa), (pp_b, doa))]
            if a == 0:
                dk_cur, dk_prev, dv_cur, dv_prev = parts
            else:
                dk_cur, dk_prev, dv_cur, dv_prev = dk_cur + parts[0], dk_prev + parts[1], dv_cur + parts[2], dv_prev + parts[3]
        dq_ref[...] = jnp.where(lo, dqs[0], dqs[1]).astype(dq_ref.dtype)
        start = pl.multiple_of(g * BAND_ROWS, BAND_ROWS)
        after = pl.multiple_of(g * BAND_ROWS + BLK, BLK)
        dk_acc[pl.ds(after, BAND_ROWS), :] += dk_cur
        dk_acc[pl.ds(start, BAND_ROWS), :] += dk_prev
        dv_acc[pl.ds(after, BAND_ROWS), :] += dv_cur
        dv_acc[pl.ds(start, BAND_ROWS), :] += dv_prev

        @pl.when(g == nstep - 1)
        def _():
            dk_ref[...] = dk_acc[BLK:, :].astype(dk_ref.dtype)
            dv_ref[...] = dv_acc[BLK:, :].astype(dv_ref.dtype)

    slab = lambda off: pl.BlockSpec((BAND_ROWS, LANE), lambda hp, g, off=off: (g, off + hp))
    whole = lambda off: pl.BlockSpec((t, LANE), lambda hp, g, off=off: (0, off + hp // share))
    per_pair = lambda shp: pl.BlockSpec((2,) + shp, lambda hp, g: (hp,) + (0,) * len(shp))
    kv_cols = N_HEADS * HEAD_DIM // share
    return pl.pallas_call(
        body, name=name, grid=(N_HEADS // 2, nstep),
        in_specs=[pl.BlockSpec(memory_space=pltpu.SMEM), slab(qo), whole(ko), whole(vo), slab(0), slab(0), slab(0),
                  per_pair((BLK, 2 * BLK))],
        out_specs=[slab(0), whole(0), whole(0), per_pair((BLK, 2 * BLK)), per_pair((1, LANE))],
        out_shape=[jax.ShapeDtypeStruct((t, N_HEADS * HEAD_DIM), BF16), jax.ShapeDtypeStruct((t, kv_cols), BF16),
                   jax.ShapeDtypeStruct((t, kv_cols), BF16), jax.ShapeDtypeStruct((N_HEADS, BLK, 2 * BLK), F32),
                   jax.ShapeDtypeStruct((N_HEADS, 1, LANE), F32)],
        scratch_shapes=[pltpu.VMEM((t + BLK, LANE), F32), pltpu.VMEM((t + BLK, LANE), F32)],
        compiler_params=_cparams(("arbitrary", "arbitrary")),
    )(sinks, src, src, src, do, lse, dpr, bias)


MLA_V_OFF = N_HEADS


def _diag_mask(keys_on_rows=False):
    rows, cols = lax.broadcasted_iota(jnp.int32, (TQ, TQ), 0), lax.broadcasted_iota(jnp.int32, (TQ, TQ), 1)
    return rows <= cols if keys_on_rows else cols <= rows


def _mla_specs(t):
    blk = lambda f: pl.BlockSpec((TQ, LANE), lambda hp, qi, f=f: (qi, f(hp)))
    whole = lambda f: pl.BlockSpec((t, LANE), lambda hp, qi, f=f: (0, f(hp)))
    return blk, whole


def _mla_fwd(q, k, vt, *, name):
    t = q.shape[0]
    n = t // TQ
    scale = (C_NOPE + C_ROPE) ** -0.5

    def body(q0_ref, q1_ref, k0_ref, k1_ref, vt_ref, o_ref, lse_ref, m_ref, l_ref, acc_ref):
        qi = pl.program_id(1)
        qs, ks = (q0_ref[...], q1_ref[...]), (k0_ref, k1_ref)
        m_ref[...] = jnp.full_like(m_ref, NEG)
        l_ref[...] = jnp.zeros_like(l_ref)
        acc_ref[...] = jnp.zeros_like(acc_ref)

        def step(kj, diagonal):
            rows = pl.ds(pl.multiple_of(kj * TQ, TQ), TQ)
            vtb = vt_ref[0, kj]
            for a in range(2):
                s = _dot_nt(ks[a][rows, :], qs[a]) * scale
                if diagonal:
                    s = jnp.where(_diag_mask(keys_on_rows=True), s, NEG)
                m_prev = m_ref[a]
                m_new = jnp.maximum(m_prev, jnp.max(s, axis=0, keepdims=True))
                alpha = jnp.exp(m_prev - m_new)
                p = jnp.exp(s - m_new)
                l_ref[a] = alpha * l_ref[a] + jnp.sum(p, axis=0, keepdims=True)
                acc_ref[a] = alpha * acc_ref[a] + _dot(vtb, p.astype(BF16))
                m_ref[a] = m_new

        def kloop(kj, c2):
            step(kj, False)
            return c2

        lax.fori_loop(0, qi, kloop, 0)
        step(qi, True)
        first = lax.broadcasted_iota(jnp.int32, (LANE, 1), 0) < HEAD_DIM
        ot = jnp.where(first, acc_ref[0] * (1.0 / l_ref[0]), acc_ref[1] * (1.0 / l_ref[1]))
        lset = jnp.where(first, m_ref[0] + jnp.log(l_ref[0]), m_ref[1] + jnp.log(l_ref[1]))
        o_ref[...] = ot.T
        lse_ref[...] = lset.T

    blk, whole = _mla_specs(t)
    return pl.pallas_call(
        body, name=name, grid=(N_HEADS // 2, n),
        in_specs=[blk(lambda hp: 2 * hp), blk(lambda hp: 2 * hp + 1), whole(lambda hp: 2 * hp), whole(lambda hp: 2 * hp + 1),
                  pl.BlockSpec((1, n, LANE, TQ), lambda hp, qi: (hp, 0, 0, 0))],
        out_specs=[blk(lambda hp: hp), blk(lambda hp: hp)],
        out_shape=[jax.ShapeDtypeStruct((t, N_HEADS * HEAD_DIM), F32)] * 2,
        scratch_shapes=[pltpu.VMEM((2, 1, TQ), F32), pltpu.VMEM((2, 1, TQ), F32), pltpu.VMEM((2, LANE, TQ), F32)],
        compiler_params=_cparams(("parallel", "parallel")),
    )(q, q, k, k, vt)


def _mla_bwd(q, k, kv, do, lse, delta, *, name):
    t = q.shape[0]
    n = t // TQ
    scale = (C_NOPE + C_ROPE) ** -0.5

    def body(q0_ref, q1_ref, k0_ref, k1_ref, v_ref, do_ref, lse_ref, dl_ref,
             dq_ref, dk_ref, dv_ref, dq_acc, dk_acc, dv_acc):
        qi = pl.program_id(1)
        lo = _lane_lo()

        @pl.when(qi == 0)
        def _():
            dk_acc[...] = jnp.zeros_like(dk_acc)
            dv_acc[...] = jnp.zeros_like(dv_acc)

        dq_acc[...] = jnp.zeros_like(dq_acc)
        qs, ks = (q0_ref[...], q1_ref[...]), (k0_ref, k1_ref)
        do2, lse2, dl2 = do_ref[...], lse_ref[...], dl_ref[...]
        lse_sw, dl_sw = pltpu.roll(lse2, HEAD_DIM, axis=1), pltpu.roll(dl2, HEAD_DIM, axis=1)
        heads = []
        for a in range(2):
            mine = lo if a == 0 else jnp.logical_not(lo)
            heads.append((jnp.where(mine, do2, jnp.zeros_like(do2)), jnp.where(mine, lse2, lse_sw)[:, 0:1],
                          jnp.where(mine, dl2, dl_sw)[:, 0:1]))

        def step(kj, diagonal):
            rows = pl.ds(pl.multiple_of(kj * TQ, TQ), TQ)
            vb = v_ref[rows, :]
            for a, (doa, lse_a, dl_a) in enumerate(heads):
                kb = ks[a][rows, :]
                s = _dot_nt(qs[a], kb) * scale
                if diagonal:
                    s = jnp.where(_diag_mask(), s, NEG)
                p = jnp.exp(s - lse_a)
                ds = (p * (_dot_nt(doa, vb) - dl_a) * scale).astype(BF16)
                dq_acc[a] += _dot(ds, kb)
                dk_acc[a, rows, :] += _dot_tn(ds, qs[a])
                dv_acc[rows, :] += _dot_tn(p.astype(BF16), doa)

        def kloop(kj, c2):
            step(kj, False)
            return c2

        lax.fori_loop(0, qi, kloop, 0)
        step(qi, True)
        dq_ref[:, 0:LANE] = dq_acc[0].astype(dq_ref.dtype)
        dq_ref[:, LANE:2 * LANE] = dq_acc[1].astype(dq_ref.dtype)

        @pl.when(qi == n - 1)
        def _():
            dk_ref[:, 0:LANE] = dk_acc[0].astype(dk_ref.dtype)
            dk_ref[:, LANE:2 * LANE] = dk_acc[1].astype(dk_ref.dtype)
            dv_ref[...] = dv_acc[...].astype(dv_ref.dtype)

    blk, whole = _mla_specs(t)
    even, odd, pair = (lambda hp: 2 * hp), (lambda hp: 2 * hp + 1), (lambda hp: hp)
    wide = jax.ShapeDtypeStruct((t, N_HEADS * LANE), BF16)
    return pl.pallas_call(
        body, name=name, grid=(N_HEADS // 2, n),
        in_specs=[blk(even), blk(odd), whole(even), whole(odd), whole(lambda hp: MLA_V_OFF + hp), blk(pair), blk(pair), blk(pair)],
        out_specs=[pl.BlockSpec((TQ, 2 * LANE), lambda hp, qi: (qi, hp)), pl.BlockSpec((t, 2 * LANE), lambda hp, qi: (0, hp)), whole(pair)],
        out_shape=[wide, wide, jax.ShapeDtypeStruct((t, N_HEADS * HEAD_DIM), BF16)],
        scratch_shapes=[pltpu.VMEM((2, TQ, LANE), F32), pltpu.VMEM((2, t, LANE), F32), pltpu.VMEM((t, LANE), F32)],
        compiler_params=_cparams(("arbitrary", "arbitrary")),
    )(q, q, k, k, kv, do, lse, delta)


def _bias_lookup(bucket, table_t, *, name):
    nh, npos = bucket.shape
    tp = 4096

    def body(b_ref, t_ref, o_ref):
        bk, tab = b_ref[...], t_ref[...]
        acc = jnp.zeros(bk.shape, F32)
        for i in range(REL_BUCKETS):
            acc = jnp.where(bk == i, tab[:, i:i + 1], acc)
        o_ref[...] = acc

    return pl.pallas_call(
        body, name=name, grid=(npos // tp,),
        in_specs=[pl.BlockSpec((nh, tp), lambda i: (0, i)), pl.BlockSpec((nh, REL_BUCKETS), lambda i: (0, 0))],
        out_specs=pl.BlockSpec((nh, tp), lambda i: (0, i)),
        out_shape=jax.ShapeDtypeStruct((nh, npos), F32),
        compiler_params=_cparams(("parallel",)),
    )(bucket, table_t)


def _bias_grad(bucket, ds0, ds1, *, name):
    nh, npos = bucket.shape
    tp = 4096

    def body(b_ref, a_ref, c_ref, o_ref):
        i = pl.program_id(0)
        bk, ds = b_ref[...], a_ref[...] + c_ref[...]
        lane = lax.broadcasted_iota(jnp.int32, (nh, REL_BUCKETS), 1)
        acc = jnp.zeros((nh, REL_BUCKETS), F32)
        for j in range(REL_BUCKETS):
            col = jnp.sum(jnp.where(bk == j, ds, 0.0), axis=1, keepdims=True)
            acc = acc + jnp.where(lane == j, col, 0.0)

        @pl.when(i == 0)
        def _():
            o_ref[...] = acc

        @pl.when(i > 0)
        def _():
            o_ref[...] += acc

    return pl.pallas_call(
        body, name=name, grid=(npos // tp,),
        in_specs=[pl.BlockSpec((nh, tp), lambda i: (0, i))] * 3,
        out_specs=pl.BlockSpec((nh, REL_BUCKETS), lambda i: (0, 0)),
        out_shape=jax.ShapeDtypeStruct((nh, REL_BUCKETS), F32),
        compiler_params=_cparams(("arbitrary",)),
    )(bucket, ds0, ds1)


def _t5_bucket(dist):
    n = jnp.maximum(dist, 0)
    max_exact = REL_BUCKETS // 2
    scaled = jnp.log(jnp.maximum(n, 1).astype(F32) / max_exact) / math.log(REL_MAX_DIST / max_exact)
    large = max_exact + (scaled * (REL_BUCKETS - max_exact)).astype(jnp.int32)
    return jnp.where(n < max_exact, n, jnp.minimum(large, REL_BUCKETS - 1))


def _bucket_index():
    qi = jnp.arange(BLK)[:, None]
    ci = jnp.arange(2 * BLK)[None, :]
    step = BLK + qi - ci
    per_group = [_t5_bucket(step * d).reshape(1, -1) for d in A_DILS + (1,)]
    return jnp.concatenate([jnp.tile(b, (N_HEADS, 1)) for b in per_group], axis=0).astype(jnp.int32)


def _sigmoid(x):
    return 1.0 / (1.0 + jnp.exp(-x))


def _ln_stats(z):
    mu = jnp.mean(z, axis=-1, keepdims=True)
    zc = z - mu
    var = jnp.mean(zc * zc, axis=-1, keepdims=True)
    return zc * lax.rsqrt(var + LN_EPS)


def _ln_fwd(x, mix, g, b, *, name):
    def fn(i, nt, xv, mv, gv, bv):
        z = ALPHA * xv + mv
        y = _ln_stats(z) * gv + bv
        return y, y, z

    c = x.shape[-1]
    y, yb, z = _rowwise(fn, [(x[None], c, 0), (mix[None], c, 0)], pars=[(g.reshape(1, 1, c), c, 0), (b.reshape(1, 1, c), c, 0)],
                        outs=[(c, c, 0, F32), (c, c, 0, BF16), (c, c, 0, F32)], tm=512, name=name)
    return y[0], yb[0], z[0]


def _ln_bwd(z, g, dys, coefs, *, name):
    n = len(dys)

    def fn(i, nt, zv, *rest):
        gv = rest[n]
        dy = coefs[0] * rest[0]
        for cf, t in zip(coefs[1:], rest[1:n]):
            dy = dy + cf * t
        mu = jnp.mean(zv, axis=-1, keepdims=True)
        zc = zv - mu
        r = lax.rsqrt(jnp.mean(zc * zc, axis=-1, keepdims=True) + LN_EPS)
        xh = zc * r
        dxh = dy * gv
        dz = r * (dxh - jnp.mean(dxh, axis=-1, keepdims=True) - xh * jnp.mean(dxh * xh, axis=-1, keepdims=True))
        return dz, dz, jnp.sum(dy * xh, axis=0, keepdims=True), jnp.sum(dy, axis=0, keepdims=True)

    c = z.shape[-1]
    dz, dzb, dg, db = _rowwise(fn, [(z[None], c, 0)] + [(d[None], c, 0) for d in dys], pars=[(g.reshape(1, 1, c), c, 0)],
                               outs=[(c, c, 0, F32), (c, c, 0, BF16)], accs=[(1, c, c, 0), (1, c, c, 0)], tm=512, name=name)
    return dz[0], dzb[0], dg.reshape(c), db.reshape(c)


def _rms_fwd(src, c, off, g, *, name):
    def fn(i, nt, xv, gv):
        return xv * lax.rsqrt(jnp.mean(xv * xv, axis=-1, keepdims=True) + RMS_EPS) * gv

    return _rowwise(fn, [(src[None], c, off)], pars=[(g.reshape(1, 1, c), c, 0)], outs=[(c, c, 0, BF16)], tm=1024, name=name)[0][0]


def _rms_bwd(src, c, off, g, dy, *, name):
    def fn(i, nt, xv, dyv, gv):
        r = lax.rsqrt(jnp.mean(xv * xv, axis=-1, keepdims=True) + RMS_EPS)
        gd = gv * dyv
        dx = gd * r - xv * (r * r * r) * jnp.mean(gd * xv, axis=-1, keepdims=True)
        return dx, jnp.sum(dyv * xv * r, axis=0, keepdims=True)

    dx, dg = _rowwise(fn, [(src[None], c, off), (dy[None], c, 0)], pars=[(g.reshape(1, 1, c), c, 0)],
                      outs=[(c, c, 0, BF16)], accs=[(1, c, c, 0)], tm=1024, name=name)
    return dx[0], dg.reshape(c)


def _rope_slabs(x, n_slab, c, s, *, add=None, to_front=False, name):
    half = C_ROPE // 2

    def fn(i, nt, xv, cv, sv, *rest):
        lane = lax.broadcasted_iota(jnp.int32, (1, LANE), 1)
        extra = pltpu.roll(rest[0], C_NOPE, axis=1) if rest else None
        outs = []
        for h in range(n_slab):
            xs = xv[:, h * LANE:(h + 1) * LANE]
            if extra is not None:
                xs = xs + extra
            swapped = jnp.where(lane < C_NOPE + half, pltpu.roll(xs, LANE - half, axis=1), pltpu.roll(xs, half, axis=1))
            y = xs * cv + swapped * sv
            if to_front:
                y = jnp.where(lane < C_ROPE, pltpu.roll(y, LANE - C_NOPE, axis=1), 0.0)
            outs.append(y)
        return jnp.concatenate(outs, axis=1) if n_slab > 1 else outs[0]

    w = n_slab * LANE
    rows = [(x[None], w, 0), (c[None], LANE, 0), (s[None], LANE, 0)]
    if add is not None:
        rows.append((add[0][None], LANE, add[1]))
    return _rowwise(fn, rows, outs=[(w, w, 0, BF16)], tm=512, name=name)[0][0]


def _merge_fwd(proj, b_gate, ys, *, name):
    def fn(i, nt, g0, g1, g2, ya, yb, yc, bg):
        return (_sigmoid(g0 + bg[:, 0:1024]) * ya + _sigmoid(g1 + bg[:, 1024:2048]) * yb
                + _sigmoid(g2 + bg[:, 2048:3072]) * yc)

    rows = [(proj[None], 1024, j) for j in range(3)] + [(y[None], 1024, 0) for y in ys]
    return _rowwise(fn, rows, pars=[(b_gate.reshape(1, 1, 3072), 3072, 0)], outs=[(1024, 1024, 0, BF16)], tm=512, name=name)[0][0]


def _merge_bwd(proj, b_gate, ys, dm, *, name):
    def fn(i, nt, g0, g1, g2, ya, yb, yc, dmv, bg):
        outs, dgs = [], []
        for j, (gp, y) in enumerate(((g0, ya), (g1, yb), (g2, yc))):
            s = _sigmoid(gp + bg[:, j * 1024:(j + 1) * 1024])
            outs.append(s * dmv)
            dgs.append(dmv * y * s * (1.0 - s))
        return outs + dgs + [jnp.sum(d, axis=0, keepdims=True) for d in dgs]

    rows = [(proj[None], 1024, j) for j in range(3)] + [(y[None], 1024, 0) for y in ys] + [(dm[None], 1024, 0)]
    res = _rowwise(fn, rows, pars=[(b_gate.reshape(1, 1, 3072), 3072, 0)], outs=[(1024, 1024, 0, BF16)] * 6,
                   accs=[(1, 1024, 1024, 0)] * 3, tm=256, name=name)
    dys = [r[0] for r in res[0:3]]
    dgp = [r[0] for r in res[3:6]]
    dbg = jnp.concatenate([r.reshape(1024) for r in res[6:9]])
    return dys, dgp, dbg


def _shift_down(u, halo, i, k):
    ext = jnp.concatenate([jnp.where(i > 0, halo, 0.0), u], axis=0)
    return pltpu.roll(ext, k, axis=0)[HALO:]


def _shift_up(u, halo, i, nt, k):
    ext = jnp.concatenate([u, jnp.where(i < nt - 1, halo, 0.0)], axis=0)
    n = ext.shape[0]
    return pltpu.roll(ext, n - k, axis=0)[:n - HALO]


GLU_C = D_FF // 2


def _conv(u, halo, i, w, b):
    return w[0:1] * _shift_down(u, halo, i, 2) + w[1:2] * _shift_down(u, halo, i, 1) + w[2:3] * u + b


def _glu_fwd(ug, uv, conv_w, conv_b, *, name):
    def fn(i, nt, g, v, hg, hv, wg, wv, bg, bv):
        cg, cv = _conv(g, hg, i, wg, bg), _conv(v, hv, i, wv, bv)
        return cg * _sigmoid(cg) * cv

    w3, b3 = conv_w[None], conv_b.reshape(1, 1, -1)
    c = GLU_C
    return _rowwise(fn, [(ug[None], c, 0), (uv[None], c, 0)], halos=[(ug[None], c, 0, "prev"), (uv[None], c, 0, "prev")],
                    pars=[(w3, c, 0), (w3, c, 2), (b3, c, 0), (b3, c, 2)], outs=[(D_FF, c, 0, BF16)], tm=256, ncol=2, name=name)[0][0]


def _glu_bwd_a(ug, uv, conv_w, conv_b, dh, *, name):
    def fn(i, nt, g, v, dhv, hg, hv, wg, wv, bg, bv):
        g1, g2 = _shift_down(g, hg, i, 1), _shift_down(g, hg, i, 2)
        v1, v2 = _shift_down(v, hv, i, 1), _shift_down(v, hv, i, 2)
        cg = wg[0:1] * g2 + wg[1:2] * g1 + wg[2:3] * g + bg
        cv = wv[0:1] * v2 + wv[1:2] * v1 + wv[2:3] * v + bv
        s = _sigmoid(cg)
        dcv = dhv * cg * s
        dcg = dhv * cv * (s * (1.0 + cg * (1.0 - s)))
        red = lambda a: jnp.sum(a, axis=0, keepdims=True)
        return (dcg, dcv, red(dcg), red(dcv), red(dcg * g2), red(dcg * g1), red(dcg * g),
                red(dcv * v2), red(dcv * v1), red(dcv * v))

    w3, b3 = conv_w[None], conv_b.reshape(1, 1, -1)
    c = GLU_C
    res = _rowwise(fn, [(ug[None], c, 0), (uv[None], c, 0), (dh[None], c, 0)],
                   halos=[(ug[None], c, 0, "prev"), (uv[None], c, 0, "prev")],
                   pars=[(w3, c, 0), (w3, c, 2), (b3, c, 0), (b3, c, 2)],
                   outs=[(D_FF, c, 0, BF16), (D_FF, c, 0, BF16)], accs=[(1, D_FF, c, 0)] * 8, tm=256, ncol=2, name=name)
    dcg, dcv = res[0][0], res[1][0]
    dconv_b = jnp.concatenate([res[2].reshape(D_FF), res[3].reshape(D_FF)])
    dconv_w = jnp.concatenate([jnp.concatenate([res[4 + j].reshape(1, D_FF) for j in range(3)], axis=0),
                               jnp.concatenate([res[7 + j].reshape(1, D_FF) for j in range(3)], axis=0)], axis=1)
    return dcg, dcv, dconv_w, dconv_b


def _glu_bwd_b(dc, conv_w, half, *, name):
    def fn(i, nt, d, hd, w):
        return w[2:3] * d + w[1:2] * _shift_up(d, hd, i, nt, 1) + w[0:1] * _shift_up(d, hd, i, nt, 2)

    c = GLU_C
    return _rowwise(fn, [(dc[None], c, 0)], halos=[(dc[None], c, 0, "next")], pars=[(conv_w[None], c, 2 * half)],
                    outs=[(D_FF, c, 0, BF16)], tm=256, ncol=2, name=name)[0][0]


def _loss_and_grad(y, tgt, *, name):
    def fn(i, nt, yv, tv):
        err = yv - tv
        part = jnp.sum(jnp.sum(err * err, axis=0, keepdims=True), axis=1, keepdims=True) * (0.5 / D_MODEL)
        return err * (1.0 / D_MODEL), jnp.zeros((1, LANE), F32) + part

    dy, part = _rowwise(fn, [(y[None], D_MODEL, 0), (tgt[None], D_MODEL, 0)], outs=[(D_MODEL, D_MODEL, 0, F32)],
                        accs=[(1, LANE, LANE, 0)], tm=512, name=name)
    return dy[0], part.reshape(LANE)


def _lincomb(terms, coefs, *, name):
    def fn(i, nt, *vs):
        acc = coefs[0] * vs[0]
        for cf, v in zip(coefs[1:], vs[1:]):
            acc = acc + cf * v
        return acc

    c = terms[0].shape[-1]
    return _rowwise(fn, [(a[None], c, 0) for a in terms], outs=[(c, c, 0, F32)], tm=512, name=name)[0][0]


def _sum_rows(terms, *, tm, name, dtype=F32):
    def fn(i, nt, *vs):
        acc = vs[0]
        for v in vs[1:]:
            acc = acc + v
        return acc

    c = terms[0].shape[-1]
    return _rowwise(fn, [(t, c, 0) for t in terms], outs=[(c, c, 0, dtype)], tm=tm, name=name)[0]


def _head_sums(x):
    lo = _lane_lo()
    parts = []
    for j in range(x.shape[1] // LANE):
        blk = x[:, j * LANE:(j + 1) * LANE]
        s_lo = jnp.sum(jnp.where(lo, blk, 0.0), axis=1, keepdims=True)
        s_hi = jnp.sum(jnp.where(lo, 0.0, blk), axis=1, keepdims=True)
        parts.append(jnp.where(lo, s_lo, s_hi))
    return jnp.concatenate(parts, axis=1)


def _group_weights(l0, l1, l2):
    m = jnp.maximum(jnp.maximum(l0, l1), l2)
    es = [jnp.exp(l - m) for l in (l0, l1, l2)]
    inv = 1.0 / (es[0] + es[1] + es[2])
    return [e * inv for e in es]


def _combine_fwd(os_, lses, *, name):
    def fn(i, nt, o0, o1, o2, l0, l1, l2):
        w = _group_weights(l0, l1, l2)
        return w[0] * o0 + w[1] * o1 + w[2] * o2

    c = os_[0].shape[-1]
    return _rowwise(fn, [(a[None], c, 0) for a in list(os_) + list(lses)], outs=[(c, c, 0, BF16)], tm=512, name=name)[0][0]


def _combine_bwd(os_, lses, do_a, *, name):
    def fn(i, nt, o0, o1, o2, l0, l1, l2, da):
        ws = _group_weights(l0, l1, l2)
        dws = [_head_sums(da * o) for o in (o0, o1, o2)]
        mean = ws[0] * dws[0] + ws[1] * dws[1] + ws[2] * dws[2]
        return [w * da for w in ws] + [w * mean for w in ws]

    c = do_a.shape[-1]
    res = _rowwise(fn, [(a[None], c, 0) for a in list(os_) + list(lses) + [do_a]], outs=[(c, c, 0, BF16)] * 3 + [(c, c, 0, F32)] * 3,
                   tm=256, name=name)
    return [r[0] for r in res[0:3]], [r[0] for r in res[3:6]]


def _delta(do, o, *, name):
    def fn(i, nt, d, ov):
        return d, _head_sums(d * ov)

    c = do.shape[-1]
    res = _rowwise(fn, [(do[None], c, 0), (o[None], c, 0)], outs=[(c, c, 0, BF16), (c, c, 0, F32)], tm=512, name=name)
    return res[0][0], res[1][0]


def _adamw(w, g, m, v, *, name):
    c1 = 1.0 - ADAM_B1 ** ADAM_STEP
    c2 = 1.0 - ADAM_B2 ** ADAM_STEP

    def fn(i, nt, wv, gv, mv, vv):
        mn = ADAM_B1 * mv + (1.0 - ADAM_B1) * gv
        vn = ADAM_B2 * vv + (1.0 - ADAM_B2) * (gv * gv)
        delta = -ADAM_LR * ((mn / c1) / (jnp.sqrt(vn / c2) + ADAM_EPS) + ADAM_WD * wv)
        return delta, mn, vn

    r, c = w.shape
    rp = _ceil_to(r, 8)
    pad = lambda a: jnp.pad(a, ((0, rp - r), (0, 0))) if rp != r else a
    tm = rp
    for cand in (128, 64, 32, 16, 8):
        if rp % cand == 0:
            tm = cand
            break
    res = _rowwise(fn, [(pad(a)[None], c, 0) for a in (w, g, m, v)], outs=[(c, c, 0, F32)] * 3, tm=tm, name=name)
    return [x[0][:r] for x in res]


ANY = pl.BlockSpec(memory_space=pl.ANY)


def _place():
    x, y, c = lax.axis_index("x"), lax.axis_index("y"), lax.axis_index("c")
    chips = [(1 - x, y), (x, 1 - y), (1 - x, 1 - y)]
    return x, y, c, chips


def _allgather_weights(arrs):
    n = len(arrs)

    def body(*refs):
        ins, outs, send_sems, recv_sems = refs[:n], refs[n:2 * n], refs[2 * n], refs[2 * n + 1]
        x, y, c, chips = _place()
        j = 2 * x + y

        def cp(i, k, src, chip_idx, half, to):
            return pltpu.make_async_remote_copy(src_ref=src, dst_ref=outs[i].at[chip_idx, half], send_sem=send_sems.at[k],
                                                recv_sem=recv_sems.at[k], device_id=to, device_id_type=MESH)

        first, passed = [], []
        for i in range(n):
            for r, (cx, cy) in enumerate(chips):
                first.append(cp(i, 3 * i + r, ins[i].at[c], j, c, (cx, cy, c)))
                passed.append(cp(i, 3 * (n + i) + r, outs[i].at[2 * cx + cy, c], 2 * cx + cy, c, (x, y, 1 - c)))
        for d in first:
            d.start()
        for i in range(n):
            for r, (cx, cy) in enumerate(chips):
                cp(i, 3 * i + r, ins[i].at[c], 2 * cx + cy, c, (x, y, c)).wait_recv()
                passed[3 * i + r].start()
        for i in range(n):
            for r, (cx, cy) in enumerate(chips):
                cp(i, 3 * (n + i) + r, ins[i].at[c], 2 * cx + cy, 1 - c, (x, y, c)).wait_recv()
        for d in first + passed:
            d.wait_send()

    return pl.pallas_call(
        body, name="allgather_weights", in_specs=[ANY] * n, out_specs=[ANY] * n,
        out_shape=[jax.ShapeDtypeStruct((N_CHIP,) + a.shape, a.dtype) for a in arrs],
        scratch_shapes=[pltpu.SemaphoreType.DMA((6 * n,)), pltpu.SemaphoreType.DMA((6 * n,))],
    )(*arrs)


def _sibling_swap(gs):
    n = len(gs)

    def body(*refs):
        ins, outs, send_sems, recv_sems = refs[:n], refs[n:2 * n], refs[2 * n], refs[2 * n + 1]
        x, y, c, _ = _place()
        cps = [pltpu.make_async_remote_copy(src_ref=ins[i].at[1 - c], dst_ref=outs[i], send_sem=send_sems.at[i],
                                            recv_sem=recv_sems.at[i], device_id=(x, y, 1 - c), device_id_type=MESH)
               for i in range(n)]
        for d in cps:
            d.start()
        for d in cps:
            d.wait_recv()
        for d in cps:
            d.wait_send()

    return pl.pallas_call(
        body, name="grad_sibling_swap", in_specs=[ANY] * n, out_specs=[ANY] * n,
        out_shape=[jax.ShapeDtypeStruct(g.shape[1:], g.dtype) for g in gs],
        scratch_shapes=[pltpu.SemaphoreType.DMA((n,)), pltpu.SemaphoreType.DMA((n,))],
    )(*gs)


def _chip_scatter(ps):
    n = len(ps)

    def body(*refs):
        ins, outs, send_sems, recv_sems = refs[:n], refs[n:2 * n], refs[2 * n], refs[2 * n + 1]
        x, y, c, chips = _place()
        sends = []
        for i in range(n):
            for r, (cx, cy) in enumerate(chips):
                sends.append(pltpu.make_async_remote_copy(src_ref=ins[i].at[2 * cx + cy], dst_ref=outs[i].at[r], send_sem=send_sems.at[3 * i + r],
                                                          recv_sem=recv_sems.at[3 * i + r], device_id=(cx, cy, c), device_id_type=MESH))
        for d in sends:
            d.start()
        for d in sends:
            d.wait_recv()
        for d in sends:
            d.wait_send()

    return pl.pallas_call(
        body, name="grad_chip_scatter", in_specs=[ANY] * n, out_specs=[ANY] * n,
        out_shape=[jax.ShapeDtypeStruct((3,) + p.shape[1:], p.dtype) for p in ps],
        scratch_shapes=[pltpu.SemaphoreType.DMA((3 * n,)), pltpu.SemaphoreType.DMA((3 * n,))],
    )(*ps)


def _sibling_share(rs):
    n = len(rs)

    def body(*refs):
        ins, outs, send_sems, recv_sems = refs[:n], refs[n:2 * n], refs[2 * n], refs[2 * n + 1]
        x, y, c, _ = _place()
        cps = [pltpu.make_async_remote_copy(src_ref=ins[i], dst_ref=outs[i], send_sem=send_sems.at[i], recv_sem=recv_sems.at[i],
                                            device_id=(x, y, 1 - c), device_id_type=MESH) for i in range(n)]
        for d in cps:
            d.start()
        for d in cps:
            d.wait_recv()
        for d in cps:
            d.wait_send()

    return pl.pallas_call(
        body, name="grad_sibling_share", in_specs=[ANY] * n, out_specs=[ANY] * n,
        out_shape=[jax.ShapeDtypeStruct(r.shape, r.dtype) for r in rs],
        scratch_shapes=[pltpu.SemaphoreType.DMA((n,)), pltpu.SemaphoreType.DMA((n,))],
    )(*rs)


def _allreduce_small(s):
    rows, w = s.shape
    n_dev = 8

    def body(s_ref, out_ref, slots, send_sems, recv_sems):
        x, y, c, _ = _place()
        me = 4 * x + 2 * y + c
        slots[me] = s_ref[...]
        peers = []
        for r in range(1, n_dev):
            px = 1 - x if r & 4 else x
            py = 1 - y if r & 2 else y
            pc = 1 - c if r & 1 else c
            peers.append((px, py, pc))
        sends = [pltpu.make_async_remote_copy(src_ref=s_ref, dst_ref=slots.at[me], send_sem=send_sems.at[r], recv_sem=recv_sems.at[r],
                                              device_id=peer, device_id_type=MESH) for r, peer in enumerate(peers)]
        for d in sends:
            d.start()
        for r, (px, py, pc) in enumerate(peers):
            pltpu.make_async_remote_copy(src_ref=s_ref, dst_ref=slots.at[4 * px + 2 * py + pc], send_sem=send_sems.at[r],
                                         recv_sem=recv_sems.at[r], device_id=(x, y, c), device_id_type=MESH).wait_recv()
        for d in sends:
            d.wait_send()
        acc = slots[0]
        for k in range(1, n_dev):
            acc = acc + slots[k]
        out_ref[...] = acc

    vm = pl.BlockSpec(memory_space=pltpu.VMEM)
    return pl.pallas_call(
        body, name="allreduce_small", in_specs=[vm], out_specs=vm, out_shape=jax.ShapeDtypeStruct((rows, w), F32),
        scratch_shapes=[pltpu.VMEM((n_dev, rows, w), F32), pltpu.SemaphoreType.DMA((n_dev - 1,)), pltpu.SemaphoreType.DMA((n_dev - 1,))],
    )(s)


W_IN_SHARD = D_IN // N_CHIP
W_IN_ROWS_G = 2304
REDUCED = tuple(m for m in MATS if m[0] != "conv_w")
CONV_W_SIZE = 3 * 2 * D_FF


def _weight_send(name, a):
    if name == "w_in":
        return jnp.swapaxes(a, 1, 2).astype(BF16)
    return a if name == "conv_w" else a.astype(BF16)


def _full_weights(gathered, l):
    g = {k: v[:, l] for k, v in gathered.items()}
    s = g["w_in"].astype(F32).reshape(D_IN, D_MODEL)
    dup = lambda a: jnp.concatenate([a[0:64], a[0:64], a[64:128], a[64:128]], axis=0)
    o = ORIG
    wm_t = jnp.concatenate([s[o["gate"]:], s[o["a"]:o["a"] + A_COLS], s[o["bq"]:o["bk"]], dup(s[o["bk"]:o["bv"]]), dup(s[o["bv"]:o["cq"]]),
                            s[o["cq"]:o["gate"]], jnp.zeros((M_COLS - M_CDKV - (o["gate"] - o["cdkv"]), D_MODEL), F32)], axis=0).astype(BF16)
    wg_t = [s[o["a"] + gi * A_COLS:o["a"] + (gi + 1) * A_COLS].astype(BF16) for gi in (1, 2)]
    full = {name: jnp.moveaxis(g[name], 0, ax).reshape(shape) for name, shape, ax in MATS if name != "w_in"}
    uq = full["w_uq"].reshape(C_Q_RANK, N_HEADS, C_NOPE + C_ROPE)
    ukv = full["w_ukv"].reshape(C_KV_RANK, N_HEADS, 2 * C_NOPE)
    w_uq_p = _pad_lanes(uq).reshape(C_Q_RANK, N_HEADS * LANE)
    w_ukv_p = jnp.concatenate([_pad_lanes(ukv[:, :, :C_NOPE]).reshape(C_KV_RANK, N_HEADS * LANE),
                               ukv[:, :, C_NOPE:].reshape(C_KV_RANK, N_HEADS * HEAD_DIM)], axis=1)
    return {"wm_t": wm_t, "wg_t": wg_t, "w_uq_p": w_uq_p, "w_ukv_p": w_ukv_p, "w_branch": full["w_branch"], "w_out": full["w_out"],
            "wup_g": full["w_ffn_up"][:, :D_FF], "wup_v": full["w_ffn_up"][:, D_FF:], "conv_w": full["conv_w"],
            "w_ffn_down": full["w_ffn_down"]}


def _grad_send(name, g, shape, ax):
    if name == "w_in":
        return jnp.pad(g.reshape(N_CHIP, W_IN_SHARD, D_MODEL), ((0, 0), (0, W_IN_ROWS_G - W_IN_SHARD), (0, 0)))
    split = shape[:ax] + (N_CHIP, shape[ax] // N_CHIP) + shape[ax + 1:]
    return jnp.moveaxis(g.reshape(split), ax, 0)


def _grad_recv(name, r):
    return r[:W_IN_SHARD].T if name == "w_in" else r


def _pack_small(rel, small, conv_w, extra):
    parts = [rel.reshape(-1)]
    for l in range(DEPTH):
        for name in SMALL:
            parts.append(small[name][l].reshape(-1))
    parts += [conv_w.reshape(-1), extra]
    flat = jnp.concatenate(parts)
    rows = _ceil_to(-(-flat.shape[0] // LANE), 8)
    return jnp.pad(flat, (0, rows * LANE - flat.shape[0])).reshape(rows, LANE)


def _unpack_small(buf):
    flat = buf.reshape(-1)
    rel = flat[:REL_BUCKETS * 32].reshape(REL_BUCKETS, 32)
    off = REL_BUCKETS * 32
    small = {name: [] for name in SMALL}
    for l in range(DEPTH):
        for name in SMALL:
            n = SMALL_SIZES[name]
            small[name].append(flat[off:off + n])
            off += n
    conv_w = flat[off:off + DEPTH * CONV_W_SIZE].reshape(DEPTH, 3, 2 * D_FF)
    off += DEPTH * CONV_W_SIZE
    return rel, {k: jnp.stack(v) for k, v in small.items()}, conv_w, flat[off:off + LANE]


def _rows2d(a, lead):
    return a.reshape(a.shape[:lead] + (-1, a.shape[-1]))


def _row_tile(rows):
    for cand in (512, 256, 128, 64, 32, 16, 8):
        if rows % cand == 0:
            return cand
    raise ValueError(rows)


def _pair_add(g, got, core, *, name):
    g2, got2 = _rows2d(g, 1), _rows2d(got, 0)
    rows, c = got2.shape
    tm = _row_tile(rows)
    flag = jnp.zeros((1, 1, LANE), F32) + core.astype(F32)

    def fn(i, nt, a0, a1, b, f):
        return jnp.where(f[:, 0:1] == 0.0, a0, a1) + b

    stacked = g2.reshape(1, 2 * rows, c)
    out = _rowwise(fn, [(stacked, c, 0, 0), (stacked, c, 0, rows // tm), (got2[None], c, 0)], pars=[(flag, LANE, 0)],
                   outs=[(c, c, 0, BF16)], tm=tm, t=rows, name=name)[0][0]
    return out.reshape(got.shape)


def _chip_add(own, got, *, name):
    own2, got2 = _rows2d(own, 0), _rows2d(got, 1)
    rows, c = own2.shape
    tm = _row_tile(rows)

    def fn(i, nt, a, b0, b1, b2):
        return ((a.astype(F32) + b0.astype(F32)) + b1.astype(F32)) + b2.astype(F32)

    stacked = got2.reshape(1, 3 * rows, c)
    out = _rowwise(fn, [(own2[None], c, 0)] + [(stacked, c, 0, k * (rows // tm)) for k in range(3)],
                   outs=[(c, c, 0, F32)], tm=tm, t=rows, name=name)[0][0]
    return out.reshape(own.shape)


def _perm(a, d):
    if d == 1:
        return a
    t = a.shape[0]
    return jnp.swapaxes(a.reshape((t // d, d) + a.shape[1:]), 0, 1).reshape(a.shape)


def _unperm(a, d):
    if d == 1:
        return a
    t = a.shape[0]
    return jnp.swapaxes(a.reshape((d, t // d) + a.shape[1:]), 0, 1).reshape(a.shape)


def _pad_lanes(a, w=HP):
    return jnp.pad(a, [(0, 0)] * (a.ndim - 1) + [(0, w - a.shape[-1])])


def _rope_tables(t):
    pos = jnp.arange(t, dtype=F32)
    inv_freq = ROPE_BASE ** (-jnp.arange(0, C_ROPE, 2, dtype=F32) / C_ROPE)
    ang = pos[:, None] * inv_freq[None, :]
    cos, sin = jnp.cos(ang), jnp.sin(ang)
    ones, zeros = jnp.ones((t, C_NOPE), F32), jnp.zeros((t, C_NOPE), F32)
    tail = LANE - C_NOPE - C_ROPE
    c = jnp.concatenate([ones, cos, cos, ones[:, :tail]], axis=1)
    s = jnp.concatenate([zeros, -sin, sin, zeros[:, :tail]], axis=1)
    return c, s


def _band_calls(t, proj, projs_g, sinks):
    none = jnp.full((N_HEADS,), NEG, F32)
    a0 = M_A0 // LANE
    calls = [(proj, (a0, a0 + 4, a0 + 8), t // BLK, BLK, False, none)]
    calls += [(pg, (0, 4, 8), t // (d * BLK), BLK, False, none) for pg, d in zip(projs_g, A_DILS[1:])]
    calls.append((proj, (M_BQ // LANE, M_BK // LANE, M_BV // LANE), t // BLK, BLK - 1, True, sinks.astype(F32)))
    return calls


def _layer_fwd(l, x, xb, w, p, biases, rope_cs):
    t = x.shape[0]
    n = f"l{l}_"
    xps = [_perm(xb, d) for d in A_DILS[1:]]
    proj = _mm(xb, w["wm_t"], tb=True, out_dtype=BF16, tm=TM_TOKENS, name=n + "proj")
    projs_g = [_mm(xp, wg, tb=True, out_dtype=BF16, tm=TM_TOKENS, tn=768, name=n + f"proj_g{i + 1}")
               for i, (xp, wg) in enumerate(zip(xps, w["wg_t"]))]
    s = {"xb": xb, "xps": xps, "proj": proj, "projs_g": projs_g}

    calls = _band_calls(t, proj, projs_g, p["sinks"])
    outs = [_band_fwd(src, offs, biases[i], sk, nb=nb, lim=lim, gqa=gqa, name=n + f"band{i}")
            for i, (src, offs, nb, lim, gqa, sk) in enumerate(calls)]
    os_ = [_unperm(outs[gi][0], d) for gi, d in enumerate(A_DILS)]
    lses = [_unperm(outs[gi][1], d) for gi, d in enumerate(A_DILS)]
    o_a = _combine_fwd(os_, lses, name=n + "combine_fwd")
    o_b_f, lse_b = outs[3]
    o_b = o_b_f.astype(BF16)
    s.update(os=os_, lses=lses, lses_p=[outs[gi][1] for gi in range(3)], o_b=o_b_f, lse_b=lse_b)

    rq = _rms_fwd(proj, C_Q_RANK, M_CQ // C_Q_RANK, p["q_norm_g"], name=n + "rms_q")
    rkv = _rms_fwd(proj, C_KV_RANK, M_CDKV // C_KV_RANK, p["kv_norm_g"], name=n + "rms_kv")
    q_cp = _mm(rq, w["w_uq_p"], out_dtype=BF16, name=n + "uq")
    kv_cp = _mm(rkv, w["w_ukv_p"], out_dtype=BF16, name=n + "ukv")
    q_full = _rope_slabs(q_cp, N_HEADS, rope_cs[0], rope_cs[1], name=n + "rope_q")
    k_full = _rope_slabs(kv_cp, N_HEADS, rope_cs[0], rope_cs[1], add=(proj, (M_CDKV + C_KV_RANK) // LANE), name=n + "rope_k")
    vt = jnp.transpose(kv_cp[:, N_HEADS * LANE:].T.reshape(N_HEADS // 2, LANE, t // TQ, TQ), (0, 2, 1, 3))
    o_c_f, lse_c = _mla_fwd(q_full, k_full, vt, name=n + "mla_fwd")
    o_c = o_c_f.astype(BF16)
    s.update(rq=rq, rkv=rkv, q_full=q_full, k_full=k_full, kv_cp=kv_cp, lse_c=lse_c, o_c=o_c_f)

    obs = [o_a, o_b, o_c]
    ys = [_mm(o, w["w_branch"][i], out_dtype=BF16, name=n + f"branch{i}") for i, o in enumerate(obs)]
    merged = _merge_fwd(proj, p["b_gate"], ys, name=n + "merge")
    mix = _mm(merged, w["w_out"], name=n + "out")
    x1f, x1b, z1 = _ln_fwd(x, mix, p["ln1_g"], p["ln1_b"], name=n + "ln1")
    s.update(obs=obs, ys=ys, merged=merged, z1=z1, x1b=x1b)

    ug = _mm(x1b, w["wup_g"], tm=TM_TOKENS, tn=1408, out_dtype=BF16, name=n + "up_g")
    uv = _mm(x1b, w["wup_v"], tm=TM_TOKENS, tn=1408, out_dtype=BF16, name=n + "up_v")
    h = _glu_fwd(ug, uv, w["conv_w"], p["conv_b"], name=n + "glu")
    ff = _mm(h, w["w_ffn_down"], tm=TM_TOKENS, tk=1408, name=n + "down")
    x2f, x2b, z2 = _ln_fwd(x1f, ff, p["ln2_g"], p["ln2_b"], name=n + "ln2")
    s.update(ug=ug, uv=uv, h=h, z2=z2)
    return x2f, x2b, s


def _layer_bwd(l, s, dys, coefs, w, p, biases, rope_cs):
    n = f"l{l}b_"
    t = s["z2"].shape[0]
    gw, gs = {}, {}

    dz2, dz2b, gs["ln2_g"], gs["ln2_b"] = _ln_bwd(s["z2"], p["ln2_g"], dys, coefs, name=n + "ln2")
    dh = _mm(dz2b, w["w_ffn_down"], tb=True, tm=TM_TOKENS, tn=1408, out_dtype=BF16, name=n + "d_h")
    gw["w_ffn_down"] = _mm(s["h"], dz2b, ta=True, tm=1408, tk=1024, name=n + "g_down")
    dcg, dcv, gw["conv_w"], gs["conv_b"] = _glu_bwd_a(s["ug"], s["uv"], w["conv_w"], p["conv_b"], dh, name=n + "glu_a")
    dug = _glu_bwd_b(dcg, w["conv_w"], 0, name=n + "glu_bg")
    duv = _glu_bwd_b(dcv, w["conv_w"], 1, name=n + "glu_bv")
    dx1_g = _mm(dug, w["wup_g"], tb=True, tm=TM_TOKENS, tk=1408, name=n + "d_x1g")
    dx1_v = _mm(duv, w["wup_v"], tb=True, tm=TM_TOKENS, tk=1408, name=n + "d_x1v")
    gw["w_ffn_up"] = jnp.concatenate([_mm(s["x1b"], dug, ta=True, tn=1408, tk=1024, name=n + "g_upg"),
                                      _mm(s["x1b"], duv, ta=True, tn=1408, tk=1024, name=n + "g_upv")], axis=1)

    dz1, dz1b, gs["ln1_g"], gs["ln1_b"] = _ln_bwd(s["z1"], p["ln1_g"], [dz2, dx1_g, dx1_v], [ALPHA, 1.0, 1.0], name=n + "ln1")
    dmerged = _mm(dz1b, w["w_out"], tb=True, out_dtype=BF16, name=n + "d_merged")
    gw["w_out"] = _mm(s["merged"], dz1b, ta=True, name=n + "g_out")
    dys_b, dgp, gs["b_gate"] = _merge_bwd(s["proj"], p["b_gate"], s["ys"], dmerged, name=n + "merge")
    dos = [_mm(dy, w["w_branch"][i], tb=True, out_dtype=BF16, name=n + f"d_o{i}") for i, dy in enumerate(dys_b)]
    gw["w_branch"] = jnp.stack([_mm(o, dy, ta=True, name=n + f"g_branch{i}") for i, (o, dy) in enumerate(zip(s["obs"], dys_b))])

    do_gs, dpr_gs = _combine_bwd(s["os"], s["lses"], dos[0], name=n + "combine")
    do_b, dpr_b = _delta(dos[1], s["o_b"], name=n + "delta_b")
    do_list = [_perm(a, d) for a, d in zip(do_gs, A_DILS)] + [do_b]
    dpr_list = [_perm(a, d) for a, d in zip(dpr_gs, A_DILS)] + [dpr_b]
    lse_list = s["lses_p"] + [s["lse_b"]]
    calls = _band_calls(t, s["proj"], s["projs_g"], p["sinks"])
    band = [_band_bwd(src, offs, do_list[i], lse_list[i], dpr_list[i], biases[i], sk, nb=nb, lim=lim, gqa=gqa, name=n + f"band{i}")
            for i, (src, offs, nb, lim, gqa, sk) in enumerate(calls)]
    gs["sinks"] = band[3][4][:, 0, 0]
    ds_sum = jnp.concatenate([b_[3] for b_ in band], axis=0)

    do_c, delta_c = _delta(dos[2], s["o_c"], name=n + "delta_c")
    dq, dk, dv = _mla_bwd(s["q_full"], s["k_full"], s["kv_cp"], do_c, s["lse_c"], delta_c, name=n + "mla")
    dq_cp = _rope_slabs(dq, N_HEADS, rope_cs[0], -rope_cs[1], name=n + "rope_q")
    dk_sum = _rowwise(lambda i, nt, *vs: sum(vs[1:], vs[0]), [(dk[None], LANE, hh) for hh in range(N_HEADS)],
                      outs=[(LANE, LANE, 0, F32)], tm=1024, name=n + "krope_sum")[0][0]
    dkr = _rope_slabs(dk_sum, 1, rope_cs[0], -rope_cs[1], to_front=True, name=n + "rope_k")
    dkv_cp = jnp.concatenate([dk, dv], axis=1)
    d_rq = _mm(dq_cp, w["w_uq_p"], tb=True, name=n + "d_rq")
    d_rkv = _mm(dkv_cp, w["w_ukv_p"], tb=True, name=n + "d_rkv")
    g_uq = _mm(s["rq"], dq_cp, ta=True, name=n + "g_uq")
    g_ukv = _mm(s["rkv"], dkv_cp, ta=True, name=n + "g_ukv")
    gw["w_uq"] = g_uq.reshape(C_Q_RANK, N_HEADS, LANE)[:, :, :C_NOPE + C_ROPE].reshape(C_Q_RANK, -1)
    kw = N_HEADS * LANE
    gw["w_ukv"] = jnp.concatenate([g_ukv[:, :kw].reshape(C_KV_RANK, N_HEADS, LANE)[:, :, :C_NOPE],
                                   g_ukv[:, kw:].reshape(C_KV_RANK, N_HEADS, HEAD_DIM)], axis=2).reshape(C_KV_RANK, -1)
    dcq, gs["q_norm_g"] = _rms_bwd(s["proj"], C_Q_RANK, M_CQ // C_Q_RANK, p["q_norm_g"], d_rq, name=n + "rms_q")
    dckv, gs["kv_norm_g"] = _rms_bwd(s["proj"], C_KV_RANK, M_CDKV // C_KV_RANK, p["kv_norm_g"], d_rkv, name=n + "rms_kv")
    dcdkv = jnp.concatenate([dckv, dkr], axis=1)

    dproj = jnp.concatenate(dgp + list(band[0][:3]) + list(band[3][:3]) + [dcq, dcdkv], axis=1)
    dprojs_g = [jnp.concatenate(band[gi][:3], axis=1) for gi in (1, 2)]
    dx_terms = [_mm(dproj, w["wm_t"], tm=TM_TOKENS, tk=1024, name=n + "d_x")]
    dx_terms += [_unperm(_mm(dp, wg, tm=TM_TOKENS, tk=768, name=n + f"d_x_g{i + 1}"), d)
                 for i, (dp, wg, d) in enumerate(zip(dprojs_g, w["wg_t"], A_DILS[1:]))]
    g_main = _mm(dproj, s["xb"], ta=True, name=n + "g_in")
    g_groups = [_mm(dp, xp, ta=True, tm=768, name=n + f"g_in_g{i + 1}") for i, (xp, dp) in enumerate(zip(s["xps"], dprojs_g))]
    fold = lambda a, tag: _sum_rows([a.reshape(2, 2, HEAD_DIM, D_MODEL)[:, j] for j in range(2)], tm=HEAD_DIM,
                                    name=n + "g_fold_" + tag).reshape(2 * HEAD_DIM, D_MODEL)
    gw["w_in"] = jnp.concatenate([g_main[M_A0:M_BQ], g_groups[0], g_groups[1], g_main[M_BQ:M_BK], fold(g_main[M_BK:M_BV], "k"),
                                  fold(g_main[M_BV:M_CQ], "v"), g_main[M_CQ:M_CDKV + C_KV_RANK + C_ROPE], g_main[M_GATE:M_A0]], axis=0)
    return [dz1] + dx_terms, [ALPHA, 1.0, 1.0, 1.0], gw, gs, ds_sum


def _local_step(x, target, ws, rel_table, small):
    t = x.shape[0]
    ps = [{k: small[k][l] for k in SMALL} for l in range(DEPTH)]
    bucket = _bucket_index()
    bias_all = _bias_lookup(bucket, rel_table.T, name="bias_lookup").reshape(4, N_HEADS, BLK, 2 * BLK)
    biases = [bias_all[i] for i in range(4)]
    rope_cs = _rope_tables(t)

    saved, h, hb = [], x, x.astype(BF16)
    for l in range(DEPTH):
        h, hb, s = _layer_fwd(l, h, hb, ws[l], ps[l], biases, rope_cs)
        saved.append(s)
    dy, loss_part = _loss_and_grad(h, target, name="loss")

    dys, coefs = [dy], [1.0]
    gws, gss, dss = [None] * DEPTH, [None] * DEPTH, [None] * DEPTH
    for l in reversed(range(DEPTH)):
        dys, coefs, gws[l], gss[l], dss[l] = _layer_bwd(l, saved[l], dys, coefs, ws[l], ps[l], biases, rope_cs)
    grad_x = _lincomb(dys, coefs, name="grad_x")
    npos = 2 * BLK * BLK
    g_rel = _bias_grad(bucket, dss[0].reshape(4 * N_HEADS, npos), dss[1].reshape(4 * N_HEADS, npos), name="bias_grad").T
    gsmall = {k: jnp.stack([gss[l][k] for l in range(DEPTH)]) for k in SMALL}
    return loss_part, grad_x, gws, gsmall, g_rel


def kernel(x, rel_table, w_in, b_gate, sinks, q_norm_g, kv_norm_g, w_uq, w_ukv, w_branch, w_out, ln1_g, ln1_b, w_ffn_up, conv_w, conv_b, w_ffn_down, ln2_g, ln2_b, loss_target, m_rel_table, m_w_in, m_b_gate, m_sinks, m_q_norm_g, m_kv_norm_g, m_w_uq, m_w_ukv, m_w_branch, m_w_out, m_ln1_g, m_ln1_b, m_w_ffn_up, m_conv_w, m_conv_b, m_w_ffn_down, m_ln2_g, m_ln2_b, v_rel_table, v_w_in, v_b_gate, v_sinks, v_q_norm_g, v_kv_norm_g, v_w_uq, v_w_ukv, v_w_branch, v_w_out, v_ln1_g, v_ln1_b, v_w_ffn_up, v_conv_w, v_conv_b, v_w_ffn_down, v_ln2_g, v_ln2_b):
    wts = dict(rel_table=rel_table, w_in=w_in, b_gate=b_gate, sinks=sinks, q_norm_g=q_norm_g, kv_norm_g=kv_norm_g, w_uq=w_uq,
               w_ukv=w_ukv, w_branch=w_branch, w_out=w_out, ln1_g=ln1_g, ln1_b=ln1_b, w_ffn_up=w_ffn_up, conv_w=conv_w,
               conv_b=conv_b, w_ffn_down=w_ffn_down, ln2_g=ln2_g, ln2_b=ln2_b)
    ms = dict(rel_table=m_rel_table, w_in=m_w_in, b_gate=m_b_gate, sinks=m_sinks, q_norm_g=m_q_norm_g, kv_norm_g=m_kv_norm_g,
              w_uq=m_w_uq, w_ukv=m_w_ukv, w_branch=m_w_branch, w_out=m_w_out, ln1_g=m_ln1_g, ln1_b=m_ln1_b, w_ffn_up=m_w_ffn_up,
              conv_w=m_conv_w, conv_b=m_conv_b, w_ffn_down=m_w_ffn_down, ln2_g=m_ln2_g, ln2_b=m_ln2_b)
    vs = dict(rel_table=v_rel_table, w_in=v_w_in, b_gate=v_b_gate, sinks=v_sinks, q_norm_g=v_q_norm_g, kv_norm_g=v_kv_norm_g,
              w_uq=v_w_uq, w_ukv=v_w_ukv, w_branch=v_w_branch, w_out=v_w_out, ln1_g=v_ln1_g, ln1_b=v_ln1_b, w_ffn_up=v_w_ffn_up,
              conv_w=v_conv_w, conv_b=v_conv_b, w_ffn_down=v_w_ffn_down, ln2_g=v_ln2_g, ln2_b=v_ln2_b)

    core = lax.axis_index("c")
    chip = 2 * lax.axis_index("x") + lax.axis_index("y")

    names = [name for name, _, _ in MATS]
    sent = [_weight_send(name, wts[name]) for name in names]
    got = _allgather_weights(sent)
    gathered = {name: lax.dynamic_update_slice(g, s[None], (chip,) + (0,) * s.ndim) for name, g, s in zip(names, got, sent)}
    ws = [_full_weights(gathered, l) for l in range(DEPTH)]

    small = {k: wts[k] for k in SMALL}
    loss_part, grad_x, gws, gsmall, g_rel = _local_step(x[0], loss_target[0], ws, rel_table, small)

    rnames = [name for name, _, _ in REDUCED]
    gsend = [jnp.stack([_grad_send(name, gws[l][name], shape, ax) for l in range(DEPTH)]) for name, shape, ax in REDUCED]
    theirs = _sibling_swap(gsend)
    pairs = [_pair_add(g, t_, core, name="grad_pair_" + name) for name, g, t_ in zip(rnames, gsend, theirs)]
    arrived = _chip_scatter(pairs)
    reduced = [_chip_add(lax.dynamic_index_in_dim(p, chip, 0, keepdims=False), a, name="grad_chip_" + name)
               for name, p, a in zip(rnames, pairs, arrived)]
    others = _sibling_share(reduced)
    gshard = {}
    for name, mine, other in zip(rnames, reduced, others):
        layers = [jnp.where(core == l, mine, other) for l in range(DEPTH)]
        gshard[name] = jnp.stack([_grad_recv(name, a) for a in layers])

    conv_w_full = jnp.stack([gws[l]["conv_w"] for l in range(DEPTH)])
    small_red = _allreduce_small(_pack_small(g_rel, gsmall, conv_w_full, loss_part))
    g_rel_r, gsmall_r, conv_w_r, loss_vec = _unpack_small(small_red)
    loss = loss_vec[0]
    shard_w = 2 * D_FF // N_CHIP
    gshard["conv_w"] = lax.dynamic_slice_in_dim(conv_w_r, chip * shard_w, shard_w, axis=2)

    grads = dict(gshard)
    grads.update(gsmall_r)
    grads["rel_table"] = g_rel_r
    deltas, new_m, new_v = {}, {}, {}
    for name, _, _ in MATS:
        shp = wts[name].shape
        v2 = lambda a: a.reshape(-1, shp[-1])
        d_, m_, v_ = _adamw(v2(wts[name]), v2(grads[name]), v2(ms[name]), v2(vs[name]), name="adamw_" + name)
        deltas[name], new_m[name], new_v[name] = d_.reshape(shp), m_.reshape(shp), v_.reshape(shp)
    zero, none = jnp.zeros((LANE,), F32), jnp.zeros((0,), F32)
    sw = _pack_small(wts["rel_table"], {k: wts[k] for k in SMALL}, none, zero)
    sm = _pack_small(ms["rel_table"], {k: ms[k] for k in SMALL}, none, zero)
    sv = _pack_small(vs["rel_table"], {k: vs[k] for k in SMALL}, none, zero)
    sg = _pack_small(g_rel_r, gsmall_r, none, zero)
    sd, smn, svn = _adamw(sw, sg, sm, sv, name="adamw_small")
    for res, buf in ((deltas, sd), (new_m, smn), (new_v, svn)):
        rel_, sm_ = _unpack_small(jnp.pad(buf, ((0, small_red.shape[0] - buf.shape[0]), (0, 0))))[:2]
        res["rel_table"] = rel_
        res.update(sm_)

    return (loss, grad_x[None], *[grads[k] for k in WEIGHT_ORDER], *[deltas[k] for k in WEIGHT_ORDER],
            *[new_m[k] for k in WEIGHT_ORDER], *[new_v[k] for k in WEIGHT_ORDER])
```

```python
import math

import jax
import jax.numpy as jnp
from jax import lax
from jax.experimental import pallas as pl
from jax.experimental.pallas import tpu as pltpu

F32 = jnp.float32
BF16 = jnp.bfloat16
MESH = pl.DeviceIdType.MESH

D_MODEL = 1024
DEPTH = 2
HEAD_DIM = 64
N_HEADS = 8
A_DILS = (1, 4, 16)
C_Q_RANK = 256
C_KV_RANK = 128
C_NOPE = 64
C_ROPE = 32
ROPE_BASE = 10000.0
REL_BUCKETS = 32
REL_MAX_DIST = 2048
D_FF = 2816
ALPHA = (2 * DEPTH) ** 0.25
LN_EPS = 1e-5
RMS_EPS = 1e-6
NEG = -1e30
LOG2E, LN2 = math.log2(math.e), math.log(2.0)
ADAM_LR, ADAM_B1, ADAM_B2, ADAM_EPS, ADAM_WD, ADAM_STEP = 0.001, 0.9, 0.999, 1e-08, 0.01, 10

VMEM_LIMIT_BYTES = 56 * 1024 * 1024
LANE = 128
BLK = 128
TQ = 512
HP = 128
TM_TOKENS = 2048
HALO = 16
BAND_UNROLL = 16

D_IN = 8864
A_COLS = 3 * N_HEADS * HEAD_DIM
ORIG = {"a": 0, "bq": 4608, "bk": 5120, "bv": 5248, "cq": 5376, "cdkv": 5632, "gate": 5792}
M_GATE, M_A0, M_BQ, M_BK, M_BV, M_CQ, M_CDKV, M_COLS = 0, 3072, 7680, 8192, 8448, 8704, 8960, 9216

N_CHIP = 4
MATS = (
    ("w_in", (D_MODEL, D_IN), 1),
    ("w_uq", (C_Q_RANK, 768), 1),
    ("w_ukv", (C_KV_RANK, 1024), 1),
    ("w_branch", (3, 512, D_MODEL), 2),
    ("w_out", (D_MODEL, D_MODEL), 0),
    ("w_ffn_up", (D_MODEL, 2 * D_FF), 1),
    ("conv_w", (3, 2 * D_FF), 1),
    ("w_ffn_down", (D_FF, D_MODEL), 0),
)
SMALL = ("b_gate", "sinks", "q_norm_g", "kv_norm_g", "ln1_g", "ln1_b", "conv_b", "ln2_g", "ln2_b")
SMALL_SIZES = {"b_gate": 3072, "sinks": 8, "q_norm_g": 256, "kv_norm_g": 128, "ln1_g": 1024, "ln1_b": 1024,
               "conv_b": 5632, "ln2_g": 1024, "ln2_b": 1024}
WEIGHT_ORDER = ("rel_table", "w_in", "b_gate", "sinks", "q_norm_g", "kv_norm_g", "w_uq", "w_ukv", "w_branch",
                "w_out", "ln1_g", "ln1_b", "w_ffn_up", "conv_w", "conv_b", "w_ffn_down", "ln2_g", "ln2_b")


def _cparams(sem):
    return pltpu.CompilerParams(dimension_semantics=sem, vmem_limit_bytes=VMEM_LIMIT_BYTES)


def _ceil_to(n, m):
    return -(-n // m) * m


def _pick(n, target):
    if n <= target:
        return n
    best = None
    for t in range(LANE, target + 1, LANE):
        if n % t == 0:
            best = t
    assert best is not None, (n, target)
    return best


def _mm(a, b, *, ta=False, tb=False, out_dtype=F32, tm=1024, tn=1024, tk=2048, name):
    assert not (ta and tb)
    k, m = a.shape[::-1] if not ta else a.shape
    n = b.shape[0] if tb else b.shape[1]
    assert (b.shape[1] if tb else b.shape[0]) == k
    tm, tn, tk = _pick(m, tm), _pick(n, tn), _pick(k, tk)
    nk = k // tk
    dn = (((0 if ta else 1,), (1 if tb else 0,)), ((), ()))

    def body(a_ref, b_ref, o_ref, acc_ref):
        part = lax.dot_general(a_ref[...].astype(BF16), b_ref[...].astype(BF16), dn, preferred_element_type=F32)
        if nk == 1:
            o_ref[...] = part.astype(o_ref.dtype)
        else:
            kk = pl.program_id(2)

            @pl.when(kk == 0)
            def _():
                acc_ref[...] = part

            @pl.when(kk > 0)
            def _():
                acc_ref[...] += part

            @pl.when(kk == nk - 1)
            def _():
                o_ref[...] = acc_ref[...].astype(o_ref.dtype)

    a_spec = pl.BlockSpec((tk, tm), lambda i, j, kk: (kk, i)) if ta else pl.BlockSpec((tm, tk), lambda i, j, kk: (i, kk))
    b_spec = pl.BlockSpec((tn, tk), lambda i, j, kk: (j, kk)) if tb else pl.BlockSpec((tk, tn), lambda i, j, kk: (kk, j))
    return pl.pallas_call(
        body, name=name, grid=(m // tm, n // tn, nk),
        in_specs=[a_spec, b_spec],
        out_specs=pl.BlockSpec((tm, tn), lambda i, j, kk: (i, j)),
        out_shape=jax.ShapeDtypeStruct((m, n), out_dtype),
        scratch_shapes=[pltpu.VMEM((tm, tn) if nk > 1 else (8, LANE), F32)],
        compiler_params=_cparams(("parallel", "parallel", "arbitrary")),
    )(a, b)


def _rowwise(fn, rows, *, pars=(), halos=(), outs=(), accs=(), tm, name, ncol=1, t=None):
    nb = rows[0][0].shape[0]
    t = rows[0][0].shape[1] if t is None else t
    tm = min(tm, t)
    assert t % tm == 0 and tm % 8 == 0
    nt = t // tm
    in_specs, args = [], []
    for spec in rows:
        arr, c, off = spec[:3]
        rb = spec[3] if len(spec) > 3 else 0
        in_specs.append(pl.BlockSpec((1, tm, c), lambda b, cc, i, off=off, rb=rb: (b, i + rb, off + cc)))
        args.append(arr)
    for arr, c, off, kind in halos:
        if kind == "prev":
            im = lambda b, cc, i, off=off: (b, jnp.maximum(i * (tm // HALO) - 1, 0), off + cc)
        else:
            im = lambda b, cc, i, off=off: (b, jnp.minimum((i + 1) * (tm // HALO), t // HALO - 1), off + cc)
        in_specs.append(pl.BlockSpec((1, HALO, c), im))
        args.append(arr)
    for arr, c, off in pars:
        bp, r = arr.shape[:2]
        if bp > 1:
            im = lambda b, cc, i, off=off: (b, 0, off + cc)
        else:
            im = lambda b, cc, i, off=off: (0, 0, off + cc)
        in_specs.append(pl.BlockSpec((1, r, c), im))
        args.append(arr)
    out_specs, out_shapes = [], []
    for ctot, c, off, dt in outs:
        out_specs.append(pl.BlockSpec((1, tm, c), lambda b, cc, i, off=off: (b, i, off + cc)))
        out_shapes.append(jax.ShapeDtypeStruct((nb, t, ctot), dt))
    for r, ctot, c, off in accs:
        out_specs.append(pl.BlockSpec((1, r, c), lambda b, cc, i, off=off: (b, 0, off + cc)))
        out_shapes.append(jax.ShapeDtypeStruct((nb, r, ctot), F32))
    n_in, n_out = len(args), len(outs)

    def body(*refs):
        i = pl.program_id(2)
        res = fn(i, nt, *[r[0].astype(F32) for r in refs[:n_in]])
        if not isinstance(res, (tuple, list)):
            res = (res,)
        for o_ref, val in zip(refs[n_in:n_in + n_out], res[:n_out]):
            o_ref[0] = val.astype(o_ref.dtype)
        for a_ref, val in zip(refs[n_in + n_out:], res[n_out:]):
            @pl.when(i == 0)
            def _(a_ref=a_ref, val=val):
                a_ref[0] = val

            @pl.when(i > 0)
            def _(a_ref=a_ref, val=val):
                a_ref[0] += val

    res = pl.pallas_call(
        body, name=name, grid=(nb, ncol, nt), in_specs=in_specs, out_specs=out_specs, out_shape=out_shapes,
        compiler_params=_cparams(("parallel", "parallel", "arbitrary")),
    )(*args)
    return res


def _dot(a, b):
    return lax.dot_general(a, b, (((1,), (0,)), ((), ())), preferred_element_type=F32)


def _dot_nt(a, b):
    return lax.dot_general(a, b, (((1,), (1,)), ((), ())), preferred_element_type=F32)


def _dot_tn(a, b):
    return lax.dot_general(a, b, (((0,), (0,)), ((), ())), preferred_element_type=F32)


def _rows(parts):
    return jnp.concatenate(parts, axis=0)


def _lane_lo():
    return lax.broadcasted_iota(jnp.int32, (1, LANE), 1) < HEAD_DIM


def _blocks(a):
    return [a[i * BLK:(i + 1) * BLK] for i in range(a.shape[0] // BLK)]


def _band_geometry(t, dil):
    seq = t // dil
    rows = min(BAND_UNROLL, seq // BLK) * BLK
    assert seq % rows == 0
    return seq, rows, seq // rows


def _band_operands(g, k_ref, v_ref, rows):
    start = pl.multiple_of(g * rows, rows)
    pstart = pl.multiple_of(jnp.maximum(g * rows - BLK, 0), BLK)
    out = []
    for ref in (k_ref, v_ref):
        cur = ref[pl.ds(start, rows), :]
        raw = ref[pl.ds(pstart, rows), :]
        shifted = _rows([jnp.zeros((BLK, LANE), raw.dtype), raw[:rows - BLK]])
        out += [_blocks(cur), _blocks(jnp.where(g == 0, shifted, raw))]
    return out


def _band_scores(g, qa, kc, kp, b_ref, a, lim):
    scale = HEAD_DIM ** -0.5
    u = len(qa)
    qi = jnp.bitwise_and(lax.broadcasted_iota(jnp.int32, (u * BLK, BLK), 0), BLK - 1)
    ki = lax.broadcasted_iota(jnp.int32, (u * BLK, BLK), 1)
    tile = lambda blk: _rows([blk] * u)
    first = _rows([jnp.zeros((BLK, 1), F32) + jnp.where(g == 0, NEG, 0.0).astype(F32), jnp.zeros(((u - 1) * BLK, 1), F32)]) \
        if u > 1 else jnp.zeros((BLK, 1), F32) + jnp.where(g == 0, NEG, 0.0).astype(F32)
    sc = _rows([_dot_nt(q, k) for q, k in zip(qa, kc)]) * scale + tile(b_ref[a, :, BLK:2 * BLK])
    sp = _rows([_dot_nt(q, k) for q, k in zip(qa, kp)]) * scale + tile(b_ref[a, :, 0:BLK])
    return jnp.where(ki <= qi, sc, NEG), jnp.where((BLK + qi - ki) <= lim, sp, NEG) + first


def _band_fwd(src, offs, bias, sinks, *, dil, lim, gqa, name):
    t, c = src.shape
    seq, rows, nstep = _band_geometry(t, dil)
    qo, ko, vo = offs
    share = 2 if gqa else 1
    slabs, out_slabs = c // LANE, N_HEADS // 2

    def body(sink_ref, q_ref, k_ref, v_ref, b_ref, o_ref, lse_ref):
        hp, g = pl.program_id(0), pl.program_id(2)
        lo = _lane_lo()
        q2 = q_ref[...]
        kc, kp, vc, vp = _band_operands(g, k_ref, v_ref, rows)
        outs, lses = [], []
        for a in range(2):
            sink = sink_ref[2 * hp + a]
            qa = _blocks(jnp.where(lo if a == 0 else jnp.logical_not(lo), q2, jnp.zeros_like(q2)))
            sc, sp = _band_scores(g, qa, kc, kp, b_ref, a, lim)
            m = jnp.maximum(jnp.maximum(jnp.max(sc, axis=1, keepdims=True), jnp.max(sp, axis=1, keepdims=True)), sink)
            pc, pp = jnp.exp(sc - m), jnp.exp(sp - m)
            l = jnp.sum(pc, axis=1, keepdims=True) + jnp.sum(pp, axis=1, keepdims=True) + jnp.exp(sink - m)
            inv = 1.0 / l
            pc_b, pp_b = _blocks((pc * inv).astype(BF16)), _blocks((pp * inv).astype(BF16))
            outs.append(_rows([_dot(pc_b[i], vc[i]) + _dot(pp_b[i], vp[i]) for i in range(len(qa))]))
            lses.append(m + jnp.log(l))
        o_ref[...] = jnp.where(lo, outs[0], outs[1])
        lse_ref[...] = jnp.where(lo, lses[0], lses[1])

    slab = lambda off, width: pl.BlockSpec((rows, LANE), lambda hp, r, g: (g, r * width + off + hp))
    whole = lambda off: pl.BlockSpec((seq, LANE), lambda hp, r, g: (0, r * slabs + off + hp // share))
    src2 = src.reshape(seq, dil * c)
    o, lse = pl.pallas_call(
        body, name=name, grid=(N_HEADS // 2, dil, nstep),
        in_specs=[pl.BlockSpec(memory_space=pltpu.SMEM), slab(qo, slabs), whole(ko), whole(vo),
                  pl.BlockSpec((2, BLK, 2 * BLK), lambda hp, r, g: (hp, 0, 0))],
        out_specs=[slab(0, out_slabs), slab(0, out_slabs)],
        out_shape=[jax.ShapeDtypeStruct((seq, dil * N_HEADS * HEAD_DIM), F32)] * 2,
        compiler_params=_cparams(("parallel", "parallel", "parallel")),
    )(sinks, src2, src2, src2, bias)
    return o.reshape(t, -1), lse.reshape(t, -1)


def _band_bwd(src, offs, do, lse, dpr, bias, sinks, *, dil, lim, gqa, name):
    t, c = src.shape
    seq, rows, nstep = _band_geometry(t, dil)
    assert not (gqa and dil > 1)
    qo, ko, vo = offs
    share = 2 if gqa else 1
    slabs, out_slabs = c // LANE, N_HEADS // 2
    scale = HEAD_DIM ** -0.5

    def fold(a):
        acc = a[0:BLK]
        for i in range(1, rows // BLK):
            acc = acc + a[i * BLK:(i + 1) * BLK]
        return acc

    def body(sink_ref, q_ref, k_ref, v_ref, do_ref, lse_ref, dpr_ref, b_ref,
             dq_ref, dk_ref, dv_ref, ds_ref, dsink_ref, dk_acc, dv_acc):
        hp, r, g = pl.program_id(0), pl.program_id(1), pl.program_id(2)
        lo = _lane_lo()
        hi = jnp.logical_not(lo)

        @pl.when(jnp.logical_and(g == 0, lax.rem(hp, share) == 0))
        def _():
            dk_acc[...] = jnp.zeros_like(dk_acc)
            dv_acc[...] = jnp.zeros_like(dv_acc)

        @pl.when(jnp.logical_and(g == 0, r == 0))
        def _():
            ds_ref[...] = jnp.zeros_like(ds_ref)
            dsink_ref[...] = jnp.zeros_like(dsink_ref)

        q2, do2, lse2, dpr2 = q_ref[...], do_ref[...], lse_ref[...], dpr_ref[...]
        lse_sw, dpr_sw = pltpu.roll(lse2, HEAD_DIM, axis=1), pltpu.roll(dpr2, HEAD_DIM, axis=1)
        kc, kp, vc, vp = _band_operands(g, k_ref, v_ref, rows)
        dqs, dk_cur, dk_prev, dv_cur, dv_prev = [], None, None, None, None
        for a in range(2):
            sink = sink_ref[2 * hp + a]
            mine = lo if a == 0 else hi
            qa = _blocks(jnp.where(mine, q2, jnp.zeros_like(q2)))
            doa = _blocks(jnp.where(mine, do2, jnp.zeros_like(do2)))
            lse_a, dpr_a = jnp.where(mine, lse2, lse_sw), jnp.where(mine, dpr2, dpr_sw)
            sc, sp = _band_scores(g, qa, kc, kp, b_ref, a, lim)
            pc, pp = jnp.exp(sc - lse_a), jnp.exp(sp - lse_a)
            dsc = pc * (_rows([_dot_nt(d, v) for d, v in zip(doa, vc)]) - dpr_a)
            dsp = pp * (_rows([_dot_nt(d, v) for d, v in zip(doa, vp)]) - dpr_a)
            ds_ref[a, :, BLK:2 * BLK] += fold(dsc)
            ds_ref[a, :, 0:BLK] += fold(dsp)
            dsink_ref[a] -= jnp.sum(jnp.exp(sink - lse_a) * dpr_a, axis=0, keepdims=True)
            dsc_b, dsp_b = _blocks((dsc * scale).astype(BF16)), _blocks((dsp * scale).astype(BF16))
            pc_b, pp_b = _blocks(pc.astype(BF16)), _blocks(pp.astype(BF16))
            dqs.append(_rows([_dot(dsc_b[i], kc[i]) + _dot(dsp_b[i], kp[i]) for i in range(len(qa))]))
            parts = [_rows([_dot_tn(x[i], y[i]) for i in range(len(qa))])
                     for x, y in ((dsc_b, qa), (dsp_b, qa), (pc_b, doa), (pp_b, doa))]
            if a == 0:
                dk_cur, dk_prev, dv_cur, dv_prev = parts
            else:
                dk_cur, dk_prev, dv_cur, dv_prev = dk_cur + parts[0], dk_prev + parts[1], dv_cur + parts[2], dv_prev + parts[3]
        dq_ref[...] = jnp.where(lo, dqs[0], dqs[1]).astype(dq_ref.dtype)
        start = pl.multiple_of(g * rows, rows)
        after = pl.multiple_of(g * rows + BLK, BLK)
        dk_acc[pl.ds(after, rows), :] += dk_cur
        dk_acc[pl.ds(start, rows), :] += dk_prev
        dv_acc[pl.ds(after, rows), :] += dv_cur
        dv_acc[pl.ds(start, rows), :] += dv_prev

        @pl.when(g == nstep - 1)
        def _():
            dk_ref[...] = dk_acc[BLK:, :].astype(dk_ref.dtype)
            dv_ref[...] = dv_acc[BLK:, :].astype(dv_ref.dtype)

    kv_slabs = out_slabs // share
    slab = lambda off, width: pl.BlockSpec((rows, LANE), lambda hp, r, g: (g, r * width + off + hp))
    whole = lambda off, width: pl.BlockSpec((seq, LANE), lambda hp, r, g: (0, r * width + off + hp // share))
    per_pair = lambda shp: pl.BlockSpec((2,) + shp, lambda hp, r, g: (hp,) + (0,) * len(shp))
    view = lambda a: a.reshape(seq, dil * a.shape[1])
    src2 = view(src)
    dq, dk, dv, ds, dsink = pl.pallas_call(
        body, name=name, grid=(N_HEADS // 2, dil, nstep),
        in_specs=[pl.BlockSpec(memory_space=pltpu.SMEM), slab(qo, slabs), whole(ko, slabs), whole(vo, slabs),
                  slab(0, out_slabs), slab(0, out_slabs), slab(0, out_slabs), per_pair((BLK, 2 * BLK))],
        out_specs=[slab(0, out_slabs), whole(0, kv_slabs), whole(0, kv_slabs), per_pair((BLK, 2 * BLK)), per_pair((1, LANE))],
        out_shape=[jax.ShapeDtypeStruct((seq, dil * out_slabs * LANE), BF16), jax.ShapeDtypeStruct((seq, dil * kv_slabs * LANE), BF16),
                   jax.ShapeDtypeStruct((seq, dil * kv_slabs * LANE), BF16), jax.ShapeDtypeStruct((N_HEADS, BLK, 2 * BLK), F32),
                   jax.ShapeDtypeStruct((N_HEADS, 1, LANE), F32)],
        scratch_shapes=[pltpu.VMEM((seq + BLK, LANE), F32), pltpu.VMEM((seq + BLK, LANE), F32)],
        compiler_params=_cparams(("arbitrary", "arbitrary", "arbitrary")),
    )(sinks, src2, src2, src2, view(do), view(lse), view(dpr), bias)
    return dq.reshape(t, -1), dk.reshape(t, -1), dv.reshape(t, -1), ds, dsink


MLA_V_OFF = N_HEADS


def _diag_mask(keys_on_rows=False):
    rows, cols = lax.broadcasted_iota(jnp.int32, (TQ, TQ), 0), lax.broadcasted_iota(jnp.int32, (TQ, TQ), 1)
    return rows <= cols if keys_on_rows else cols <= rows


def _mla_specs(t):
    blk = lambda f: pl.BlockSpec((TQ, LANE), lambda hp, qi, f=f: (qi, f(hp)))
    whole = lambda f: pl.BlockSpec((t, LANE), lambda hp, qi, f=f: (0, f(hp)))
    return blk, whole


def _mla_fwd(q, k, vt, *, name):
    t = q.shape[0]
    n = t // TQ
    scale = (C_NOPE + C_ROPE) ** -0.5

    def body(q0_ref, q1_ref, k0_ref, k1_ref, vt_ref, o_ref, lse_ref, m_ref, l_ref, acc_ref):
        qi = pl.program_id(1)
        qs, ks = (q0_ref[...], q1_ref[...]), (k0_ref, k1_ref)
        m_ref[...] = jnp.full_like(m_ref, NEG)
        l_ref[...] = jnp.zeros_like(l_ref)
        acc_ref[...] = jnp.zeros_like(acc_ref)

        def step(kj, diagonal):
            rows = pl.ds(pl.multiple_of(kj * TQ, TQ), TQ)
            vtb = vt_ref[0, kj]
            for a in range(2):
                s = _dot_nt(ks[a][rows, :], qs[a]) * (scale * LOG2E)
                if diagonal:
                    s = jnp.where(_diag_mask(keys_on_rows=True), s, NEG)
                m_prev = m_ref[a]
                m_new = jnp.maximum(m_prev, jnp.max(s, axis=0, keepdims=True))
                alpha = jnp.exp2(m_prev - m_new)
                p = jnp.exp2(s - m_new)
                l_ref[a] = alpha * l_ref[a] + jnp.sum(p, axis=0, keepdims=True)
                acc_ref[a] = alpha * acc_ref[a] + _dot(vtb, p.astype(BF16))
                m_ref[a] = m_new

        def kloop(kj, c2):
            step(kj, False)
            return c2

        lax.fori_loop(0, qi, kloop, 0)
        step(qi, True)
        first = lax.broadcasted_iota(jnp.int32, (LANE, 1), 0) < HEAD_DIM
        ot = jnp.where(first, acc_ref[0] * (1.0 / l_ref[0]), acc_ref[1] * (1.0 / l_ref[1]))
        lset = jnp.where(first, m_ref[0] * LN2 + jnp.log(l_ref[0]), m_ref[1] * LN2 + jnp.log(l_ref[1]))
        o_ref[...] = ot.T
        lse_ref[...] = lset.T

    blk, whole = _mla_specs(t)
    return pl.pallas_call(
        body, name=name, grid=(N_HEADS // 2, n),
        in_specs=[blk(lambda hp: 2 * hp), blk(lambda hp: 2 * hp + 1), whole(lambda hp: 2 * hp), whole(lambda hp: 2 * hp + 1),
                  pl.BlockSpec((1, n, LANE, TQ), lambda hp, qi: (hp, 0, 0, 0))],
        out_specs=[blk(lambda hp: hp), blk(lambda hp: hp)],
        out_shape=[jax.ShapeDtypeStruct((t, N_HEADS * HEAD_DIM), F32)] * 2,
        scratch_shapes=[pltpu.VMEM((2, 1, TQ), F32), pltpu.VMEM((2, 1, TQ), F32), pltpu.VMEM((2, LANE, TQ), F32)],
        compiler_params=_cparams(("parallel", "parallel")),
    )(q, q, k, k, vt)


def _mla_bwd(q, k, kv, do, lse, delta, *, name):
    t = q.shape[0]
    n = t // TQ
    scale = (C_NOPE + C_ROPE) ** -0.5

    def body(q0_ref, q1_ref, k0_ref, k1_ref, v_ref, do_ref, lse_ref, dl_ref,
             dq_ref, dk_ref, dv_ref, dq_acc, dk_acc, dv_acc):
        qi = pl.program_id(1)
        lo = _lane_lo()

        @pl.when(qi == 0)
        def _():
            dk_acc[...] = jnp.zeros_like(dk_acc)
            dv_acc[...] = jnp.zeros_like(dv_acc)

        dq_acc[...] = jnp.zeros_like(dq_acc)
        qs, ks = (q0_ref[...], q1_ref[...]), (k0_ref, k1_ref)
        do2, lse2, dl2 = do_ref[...], lse_ref[...], dl_ref[...]
        lse_sw, dl_sw = pltpu.roll(lse2, HEAD_DIM, axis=1), pltpu.roll(dl2, HEAD_DIM, axis=1)
        heads = []
        for a in range(2):
            mine = lo if a == 0 else jnp.logical_not(lo)
            heads.append((jnp.where(mine, do2, jnp.zeros_like(do2)), jnp.where(mine, lse2, lse_sw)[:, 0:1] * LOG2E,
                          jnp.where(mine, dl2, dl_sw)[:, 0:1]))

        def step(kj, diagonal):
            rows = pl.ds(pl.multiple_of(kj * TQ, TQ), TQ)
            vb = v_ref[rows, :]
            for a, (doa, lse_a, dl_a) in enumerate(heads):
                kb = ks[a][rows, :]
                s = _dot_nt(qs[a], kb) * (scale * LOG2E)
                if diagonal:
                    s = jnp.where(_diag_mask(), s, NEG)
                p = jnp.exp2(s - lse_a)
                ds = (p * (_dot_nt(doa, vb) - dl_a)).astype(BF16)
                dq_acc[a] += _dot(ds, kb)
                dk_acc[a, rows, :] += _dot_tn(ds, qs[a])
                dv_acc[rows, :] += _dot_tn(p.astype(BF16), doa)

        def kloop(kj, c2):
            step(kj, False)
            return c2

        lax.fori_loop(0, qi, kloop, 0)
        step(qi, True)
        dq_ref[:, 0:LANE] = (dq_acc[0] * scale).astype(dq_ref.dtype)
        dq_ref[:, LANE:2 * LANE] = (dq_acc[1] * scale).astype(dq_ref.dtype)

        @pl.when(qi == n - 1)
        def _():
            dk_ref[:, 0:LANE] = (dk_acc[0] * scale).astype(dk_ref.dtype)
            dk_ref[:, LANE:2 * LANE] = (dk_acc[1] * scale).astype(dk_ref.dtype)
            dv_ref[...] = dv_acc[...].astype(dv_ref.dtype)

    blk, whole = _mla_specs(t)
    even, odd, pair = (lambda hp: 2 * hp), (lambda hp: 2 * hp + 1), (lambda hp: hp)
    wide = jax.ShapeDtypeStruct((t, N_HEADS * LANE), BF16)
    return pl.pallas_call(
        body, name=name, grid=(N_HEADS // 2, n),
        in_specs=[blk(even), blk(odd), whole(even), whole(odd), whole(lambda hp: MLA_V_OFF + hp), blk(pair), blk(pair), blk(pair)],
        out_specs=[pl.BlockSpec((TQ, 2 * LANE), lambda hp, qi: (qi, hp)), pl.BlockSpec((t, 2 * LANE), lambda hp, qi: (0, hp)), whole(pair)],
        out_shape=[wide, wide, jax.ShapeDtypeStruct((t, N_HEADS * HEAD_DIM), BF16)],
        scratch_shapes=[pltpu.VMEM((2, TQ, LANE), F32), pltpu.VMEM((2, t, LANE), F32), pltpu.VMEM((t, LANE), F32)],
        compiler_params=_cparams(("arbitrary", "arbitrary")),
    )(q, q, k, k, kv, do, lse, delta)


def _bias_lookup(bucket, table_t, *, name):
    nh, npos = bucket.shape
    tp = 4096

    def body(b_ref, t_ref, o_ref):
        bk, tab = b_ref[...], t_ref[...]
        acc = jnp.zeros(bk.shape, F32)
        for i in range(REL_BUCKETS):
            acc = jnp.where(bk == i, tab[:, i:i + 1], acc)
        o_ref[...] = acc

    return pl.pallas_call(
        body, name=name, grid=(npos // tp,),
        in_specs=[pl.BlockSpec((nh, tp), lambda i: (0, i)), pl.BlockSpec((nh, REL_BUCKETS), lambda i: (0, 0))],
        out_specs=pl.BlockSpec((nh, tp), lambda i: (0, i)),
        out_shape=jax.ShapeDtypeStruct((nh, npos), F32),
        compiler_params=_cparams(("parallel",)),
    )(bucket, table_t)


def _bias_grad(bucket, ds0, ds1, *, name):
    nh, npos = bucket.shape
    tp = 4096

    def body(b_ref, a_ref, c_ref, o_ref):
        i = pl.program_id(0)
        bk, ds = b_ref[...], a_ref[...] + c_ref[...]
        lane = lax.broadcasted_iota(jnp.int32, (nh, REL_BUCKETS), 1)
        acc = jnp.zeros((nh, REL_BUCKETS), F32)
        for j in range(REL_BUCKETS):
            col = jnp.sum(jnp.where(bk == j, ds, 0.0), axis=1, keepdims=True)
            acc = acc + jnp.where(lane == j, col, 0.0)

        @pl.when(i == 0)
        def _():
            o_ref[...] = acc

        @pl.when(i > 0)
        def _():
            o_ref[...] += acc

    return pl.pallas_call(
        body, name=name, grid=(npos // tp,),
        in_specs=[pl.BlockSpec((nh, tp), lambda i: (0, i))] * 3,
        out_specs=pl.BlockSpec((nh, REL_BUCKETS), lambda i: (0, 0)),
        out_shape=jax.ShapeDtypeStruct((nh, REL_BUCKETS), F32),
        compiler_params=_cparams(("arbitrary",)),
    )(bucket, ds0, ds1)


def _t5_bucket(dist):
    n = jnp.maximum(dist, 0)
    max_exact = REL_BUCKETS // 2
    scaled = jnp.log(jnp.maximum(n, 1).astype(F32) / max_exact) / math.log(REL_MAX_DIST / max_exact)
    large = max_exact + (scaled * (REL_BUCKETS - max_exact)).astype(jnp.int32)
    return jnp.where(n < max_exact, n, jnp.minimum(large, REL_BUCKETS - 1))


def _bucket_index():
    qi = jnp.arange(BLK)[:, None]
    ci = jnp.arange(2 * BLK)[None, :]
    step = BLK + qi - ci
    per_group = [_t5_bucket(step * d).reshape(1, -1) for d in A_DILS + (1,)]
    return jnp.concatenate([jnp.tile(b, (N_HEADS, 1)) for b in per_group], axis=0).astype(jnp.int32)


def _sigmoid(x):
    return 1.0 / (1.0 + jnp.exp(-x))


def _ln_stats(z):
    mu = jnp.mean(z, axis=-1, keepdims=True)
    zc = z - mu
    var = jnp.mean(zc * zc, axis=-1, keepdims=True)
    return zc * lax.rsqrt(var + LN_EPS)


def _ln_fwd(x, mix, g, b, *, name):
    def fn(i, nt, xv, mv, gv, bv):
        z = ALPHA * xv + mv
        y = _ln_stats(z) * gv + bv
        return y, y, z

    c = x.shape[-1]
    y, yb, z = _rowwise(fn, [(x[None], c, 0), (mix[None], c, 0)], pars=[(g.reshape(1, 1, c), c, 0), (b.reshape(1, 1, c), c, 0)],
                        outs=[(c, c, 0, F32), (c, c, 0, BF16), (c, c, 0, F32)], tm=512, name=name)
    return y[0], yb[0], z[0]


def _ln_bwd(z, g, dys, coefs, *, name):
    n = len(dys)

    def fn(i, nt, zv, *rest):
        gv = rest[n]
        dy = coefs[0] * rest[0]
        for cf, t in zip(coefs[1:], rest[1:n]):
            dy = dy + cf * t
        mu = jnp.mean(zv, axis=-1, keepdims=True)
        zc = zv - mu
        r = lax.rsqrt(jnp.mean(zc * zc, axis=-1, keepdims=True) + LN_EPS)
        xh = zc * r
        dxh = dy * gv
        dz = r * (dxh - jnp.mean(dxh, axis=-1, keepdims=True) - xh * jnp.mean(dxh * xh, axis=-1, keepdims=True))
        return dz, dz, jnp.sum(dy * xh, axis=0, keepdims=True), jnp.sum(dy, axis=0, keepdims=True)

    c = z.shape[-1]
    dz, dzb, dg, db = _rowwise(fn, [(z[None], c, 0)] + [(d[None], c, 0) for d in dys], pars=[(g.reshape(1, 1, c), c, 0)],
                               outs=[(c, c, 0, F32), (c, c, 0, BF16)], accs=[(1, c, c, 0), (1, c, c, 0)], tm=512, name=name)
    return dz[0], dzb[0], dg.reshape(c), db.reshape(c)


def _rms_fwd(src, c, off, g, *, name):
    def fn(i, nt, xv, gv):
        return xv * lax.rsqrt(jnp.mean(xv * xv, axis=-1, keepdims=True) + RMS_EPS) * gv

    return _rowwise(fn, [(src[None], c, off)], pars=[(g.reshape(1, 1, c), c, 0)], outs=[(c, c, 0, BF16)], tm=1024, name=name)[0][0]


def _rms_bwd(src, c, off, g, dy, *, name):
    def fn(i, nt, xv, dyv, gv):
        r = lax.rsqrt(jnp.mean(xv * xv, axis=-1, keepdims=True) + RMS_EPS)
        gd = gv * dyv
        dx = gd * r - xv * (r * r * r) * jnp.mean(gd * xv, axis=-1, keepdims=True)
        return dx, jnp.sum(dyv * xv * r, axis=0, keepdims=True)

    dx, dg = _rowwise(fn, [(src[None], c, off), (dy[None], c, 0)], pars=[(g.reshape(1, 1, c), c, 0)],
                      outs=[(c, c, 0, BF16)], accs=[(1, c, c, 0)], tm=1024, name=name)
    return dx[0], dg.reshape(c)


def _rope_slabs(x, n_slab, c, s, *, add=None, to_front=False, name):
    half = C_ROPE // 2

    def fn(i, nt, xv, cv, sv, *rest):
        lane = lax.broadcasted_iota(jnp.int32, (1, LANE), 1)
        extra = pltpu.roll(rest[0], C_NOPE, axis=1) if rest else None
        outs = []
        for h in range(n_slab):
            xs = xv[:, h * LANE:(h + 1) * LANE]
            if extra is not None:
                xs = xs + extra
            swapped = jnp.where(lane < C_NOPE + half, pltpu.roll(xs, LANE - half, axis=1), pltpu.roll(xs, half, axis=1))
            y = xs * cv + swapped * sv
            if to_front:
                y = jnp.where(lane < C_ROPE, pltpu.roll(y, LANE - C_NOPE, axis=1), 0.0)
            outs.append(y)
        return jnp.concatenate(outs, axis=1) if n_slab > 1 else outs[0]

    w = n_slab * LANE
    rows = [(x[None], w, 0), (c[None], LANE, 0), (s[None], LANE, 0)]
    if add is not None:
        rows.append((add[0][None], LANE, add[1]))
    return _rowwise(fn, rows, outs=[(w, w, 0, BF16)], tm=512, name=name)[0][0]


def _merge_fwd(proj, b_gate, ys, *, name):
    def fn(i, nt, g0, g1, g2, ya, yb, yc, bg):
        return (_sigmoid(g0 + bg[:, 0:1024]) * ya + _sigmoid(g1 + bg[:, 1024:2048]) * yb
                + _sigmoid(g2 + bg[:, 2048:3072]) * yc)

    rows = [(proj[None], 1024, j) for j in range(3)] + [(y[None], 1024, 0) for y in ys]
    return _rowwise(fn, rows, pars=[(b_gate.reshape(1, 1, 3072), 3072, 0)], outs=[(1024, 1024, 0, BF16)], tm=512, name=name)[0][0]


def _merge_bwd(proj, b_gate, ys, dm, *, name):
    def fn(i, nt, g0, g1, g2, ya, yb, yc, dmv, bg):
        outs, dgs = [], []
        for j, (gp, y) in enumerate(((g0, ya), (g1, yb), (g2, yc))):
            s = _sigmoid(gp + bg[:, j * 1024:(j + 1) * 1024])
            outs.append(s * dmv)
            dgs.append(dmv * y * s * (1.0 - s))
        return outs + dgs + [jnp.sum(d, axis=0, keepdims=True) for d in dgs]

    rows = [(proj[None], 1024, j) for j in range(3)] + [(y[None], 1024, 0) for y in ys] + [(dm[None], 1024, 0)]
    res = _rowwise(fn, rows, pars=[(b_gate.reshape(1, 1, 3072), 3072, 0)], outs=[(1024, 1024, 0, BF16)] * 6,
                   accs=[(1, 1024, 1024, 0)] * 3, tm=256, name=name)
    dys = [r[0] for r in res[0:3]]
    dgp = [r[0] for r in res[3:6]]
    dbg = jnp.concatenate([r.reshape(1024) for r in res[6:9]])
    return dys, dgp, dbg


def _shift_down(u, halo, i, k):
    ext = jnp.concatenate([jnp.where(i > 0, halo, 0.0), u], axis=0)
    return pltpu.roll(ext, k, axis=0)[HALO:]


def _shift_up(u, halo, i, nt, k):
    ext = jnp.concatenate([u, jnp.where(i < nt - 1, halo, 0.0)], axis=0)
    n = ext.shape[0]
    return pltpu.roll(ext, n - k, axis=0)[:n - HALO]


GLU_C = D_FF // 2


def _conv(u, halo, i, w, b):
    return w[0:1] * _shift_down(u, halo, i, 2) + w[1:2] * _shift_down(u, halo, i, 1) + w[2:3] * u + b


def _glu_fwd(ug, uv, conv_w, conv_b, *, name):
    def fn(i, nt, g, v, hg, hv, wg, wv, bg, bv):
        cg, cv = _conv(g, hg, i, wg, bg), _conv(v, hv, i, wv, bv)
        return cg * _sigmoid(cg) * cv

    w3, b3 = conv_w[None], conv_b.reshape(1, 1, -1)
    c = GLU_C
    return _rowwise(fn, [(ug[None], c, 0), (uv[None], c, 0)], halos=[(ug[None], c, 0, "prev"), (uv[None], c, 0, "prev")],
                    pars=[(w3, c, 0), (w3, c, 2), (b3, c, 0), (b3, c, 2)], outs=[(D_FF, c, 0, BF16)], tm=256, ncol=2, name=name)[0][0]


def _glu_bwd_a(ug, uv, conv_w, conv_b, dh, *, name):
    def fn(i, nt, g, v, dhv, hg, hv, wg, wv, bg, bv):
        g1, g2 = _shift_down(g, hg, i, 1), _shift_down(g, hg, i, 2)
        v1, v2 = _shift_down(v, hv, i, 1), _shift_down(v, hv, i, 2)
        cg = wg[0:1] * g2 + wg[1:2] * g1 + wg[2:3] * g + bg
        cv = wv[0:1] * v2 + wv[1:2] * v1 + wv[2:3] * v + bv
        s = _sigmoid(cg)
        dcv = dhv * cg * s
        dcg = dhv * cv * (s * (1.0 + cg * (1.0 - s)))
        red = lambda a: jnp.sum(a, axis=0, keepdims=True)
        return (dcg, dcv, red(dcg), red(dcv), red(dcg * g2), red(dcg * g1), red(dcg * g),
                red(dcv * v2), red(dcv * v1), red(dcv * v))

    w3, b3 = conv_w[None], conv_b.reshape(1, 1, -1)
    c = GLU_C
    res = _rowwise(fn, [(ug[None], c, 0), (uv[None], c, 0), (dh[None], c, 0)],
                   halos=[(ug[None], c, 0, "prev"), (uv[None], c, 0, "prev")],
                   pars=[(w3, c, 0), (w3, c, 2), (b3, c, 0), (b3, c, 2)],
                   outs=[(D_FF, c, 0, BF16), (D_FF, c, 0, BF16)], accs=[(1, D_FF, c, 0)] * 8, tm=256, ncol=2, name=name)
    dcg, dcv = res[0][0], res[1][0]
    dconv_b = jnp.concatenate([res[2].reshape(D_FF), res[3].reshape(D_FF)])
    dconv_w = jnp.concatenate([jnp.concatenate([res[4 + j].reshape(1, D_FF) for j in range(3)], axis=0),
                               jnp.concatenate([res[7 + j].reshape(1, D_FF) for j in range(3)], axis=0)], axis=1)
    return dcg, dcv, dconv_w, dconv_b


def _glu_bwd_b(dc, conv_w, half, *, name):
    def fn(i, nt, d, hd, w):
        return w[2:3] * d + w[1:2] * _shift_up(d, hd, i, nt, 1) + w[0:1] * _shift_up(d, hd, i, nt, 2)

    c = GLU_C
    return _rowwise(fn, [(dc[None], c, 0)], halos=[(dc[None], c, 0, "next")], pars=[(conv_w[None], c, 2 * half)],
                    outs=[(D_FF, c, 0, BF16)], tm=256, ncol=2, name=name)[0][0]


def _loss_and_grad(y, tgt, *, name):
    def fn(i, nt, yv, tv):
        err = yv - tv
        part = jnp.sum(jnp.sum(err * err, axis=0, keepdims=True), axis=1, keepdims=True) * (0.5 / D_MODEL)
        return err * (1.0 / D_MODEL), jnp.zeros((1, LANE), F32) + part

    dy, part = _rowwise(fn, [(y[None], D_MODEL, 0), (tgt[None], D_MODEL, 0)], outs=[(D_MODEL, D_MODEL, 0, F32)],
                        accs=[(1, LANE, LANE, 0)], tm=512, name=name)
    return dy[0], part.reshape(LANE)


def _lincomb(terms, coefs, *, name):
    def fn(i, nt, *vs):
        acc = coefs[0] * vs[0]
        for cf, v in zip(coefs[1:], vs[1:]):
            acc = acc + cf * v
        return acc

    c = terms[0].shape[-1]
    return _rowwise(fn, [(a[None], c, 0) for a in terms], outs=[(c, c, 0, F32)], tm=512, name=name)[0][0]


def _sum_rows(terms, *, tm, name, dtype=F32):
    def fn(i, nt, *vs):
        acc = vs[0]
        for v in vs[1:]:
            acc = acc + v
        return acc

    c = terms[0].shape[-1]
    return _rowwise(fn, [(t, c, 0) for t in terms], outs=[(c, c, 0, dtype)], tm=tm, name=name)[0]


def _head_sums(x):
    lo = _lane_lo()
    parts = []
    for j in range(x.shape[1] // LANE):
        blk = x[:, j * LANE:(j + 1) * LANE]
        s_lo = jnp.sum(jnp.where(lo, blk, 0.0), axis=1, keepdims=True)
        s_hi = jnp.sum(jnp.where(lo, 0.0, blk), axis=1, keepdims=True)
        parts.append(jnp.where(lo, s_lo, s_hi))
    return jnp.concatenate(parts, axis=1)


def _group_weights(l0, l1, l2):
    m = jnp.maximum(jnp.maximum(l0, l1), l2)
    es = [jnp.exp(l - m) for l in (l0, l1, l2)]
    inv = 1.0 / (es[0] + es[1] + es[2])
    return [e * inv for e in es]


def _combine_fwd(os_, lses, *, name):
    def fn(i, nt, o0, o1, o2, l0, l1, l2):
        w = _group_weights(l0, l1, l2)
        return w[0] * o0 + w[1] * o1 + w[2] * o2

    c = os_[0].shape[-1]
    return _rowwise(fn, [(a[None], c, 0) for a in list(os_) + list(lses)], outs=[(c, c, 0, BF16)], tm=512, name=name)[0][0]


def _combine_bwd(os_, lses, do_a, *, name):
    def fn(i, nt, o0, o1, o2, l0, l1, l2, da):
        ws = _group_weights(l0, l1, l2)
        dws = [_head_sums(da * o) for o in (o0, o1, o2)]
        mean = ws[0] * dws[0] + ws[1] * dws[1] + ws[2] * dws[2]
        return [w * da for w in ws] + [w * mean for w in ws]

    c = do_a.shape[-1]
    res = _rowwise(fn, [(a[None], c, 0) for a in list(os_) + list(lses) + [do_a]], outs=[(c, c, 0, BF16)] * 3 + [(c, c, 0, F32)] * 3,
                   tm=256, name=name)
    return [r[0] for r in res[0:3]], [r[0] for r in res[3:6]]


def _delta(do, o, *, name):
    def fn(i, nt, d, ov):
        return d, _head_sums(d * ov)

    c = do.shape[-1]
    res = _rowwise(fn, [(do[None], c, 0), (o[None], c, 0)], outs=[(c, c, 0, BF16), (c, c, 0, F32)], tm=512, name=name)
    return res[0][0], res[1][0]


def _adamw(w, g, m, v, *, name):
    c1 = 1.0 - ADAM_B1 ** ADAM_STEP
    c2 = 1.0 - ADAM_B2 ** ADAM_STEP

    def fn(i, nt, wv, gv, mv, vv):
        mn = ADAM_B1 * mv + (1.0 - ADAM_B1) * gv
        vn = ADAM_B2 * vv + (1.0 - ADAM_B2) * (gv * gv)
        delta = -ADAM_LR * ((mn / c1) / (jnp.sqrt(vn / c2) + ADAM_EPS) + ADAM_WD * wv)
        return delta, mn, vn

    r, c = w.shape
    rp = _ceil_to(r, 8)
    pad = lambda a: jnp.pad(a, ((0, rp - r), (0, 0))) if rp != r else a
    tm = rp
    for cand in (128, 64, 32, 16, 8):
        if rp % cand == 0:
            tm = cand
            break
    res = _rowwise(fn, [(pad(a)[None], c, 0) for a in (w, g, m, v)], outs=[(c, c, 0, F32)] * 3, tm=tm, name=name)
    return [x[0][:r] for x in res]


ANY = pl.BlockSpec(memory_space=pl.ANY)


def _place():
    x, y, c = lax.axis_index("x"), lax.axis_index("y"), lax.axis_index("c")
    chips = [(1 - x, y), (x, 1 - y), (1 - x, 1 - y)]
    return x, y, c, chips


def _allgather_weights(arrs):
    n = len(arrs)

    def body(*refs):
        ins, outs, send_sems, recv_sems = refs[:n], refs[n:2 * n], refs[2 * n], refs[2 * n + 1]
        x, y, c, chips = _place()
        j = 2 * x + y

        def cp(i, k, src, chip_idx, half, to):
            return pltpu.make_async_remote_copy(src_ref=src, dst_ref=outs[i].at[chip_idx, half], send_sem=send_sems.at[k],
                                                recv_sem=recv_sems.at[k], device_id=to, device_id_type=MESH)

        first, passed = [], []
        for i in range(n):
            for r, (cx, cy) in enumerate(chips):
                first.append(cp(i, 3 * i + r, ins[i].at[c], j, c, (cx, cy, c)))
                passed.append(cp(i, 3 * (n + i) + r, outs[i].at[2 * cx + cy, c], 2 * cx + cy, c, (x, y, 1 - c)))
        for d in first:
            d.start()
        for i in range(n):
            for r, (cx, cy) in enumerate(chips):
                cp(i, 3 * i + r, ins[i].at[c], 2 * cx + cy, c, (x, y, c)).wait_recv()
                passed[3 * i + r].start()
        for i in range(n):
            for r, (cx, cy) in enumerate(chips):
                cp(i, 3 * (n + i) + r, ins[i].at[c], 2 * cx + cy, 1 - c, (x, y, c)).wait_recv()
        for d in first + passed:
            d.wait_send()

    return pl.pallas_call(
        body, name="allgather_weights", in_specs=[ANY] * n, out_specs=[ANY] * n,
        out_shape=[jax.ShapeDtypeStruct((N_CHIP,) + a.shape, a.dtype) for a in arrs],
        scratch_shapes=[pltpu.SemaphoreType.DMA((6 * n,)), pltpu.SemaphoreType.DMA((6 * n,))],
    )(*arrs)


def _sibling_swap(gs):
    n = len(gs)

    def body(*refs):
        ins, outs, send_sems, recv_sems = refs[:n], refs[n:2 * n], refs[2 * n], refs[2 * n + 1]
        x, y, c, _ = _place()
        cps = [pltpu.make_async_remote_copy(src_ref=ins[i].at[1 - c], dst_ref=outs[i], send_sem=send_sems.at[i],
                                            recv_sem=recv_sems.at[i], device_id=(x, y, 1 - c), device_id_type=MESH)
               for i in range(n)]
        for d in cps:
            d.start()
        for d in cps:
            d.wait_recv()
        for d in cps:
            d.wait_send()

    return pl.pallas_call(
        body, name="grad_sibling_swap", in_specs=[ANY] * n, out_specs=[ANY] * n,
        out_shape=[jax.ShapeDtypeStruct(g.shape[1:], g.dtype) for g in gs],
        scratch_shapes=[pltpu.SemaphoreType.DMA((n,)), pltpu.SemaphoreType.DMA((n,))],
    )(*gs)


def _chip_scatter(ps):
    n = len(ps)

    def body(*refs):
        ins, outs, send_sems, recv_sems = refs[:n], refs[n:2 * n], refs[2 * n], refs[2 * n + 1]
        x, y, c, chips = _place()
        sends = []
        for i in range(n):
            for r, (cx, cy) in enumerate(chips):
                sends.append(pltpu.make_async_remote_copy(src_ref=ins[i].at[2 * cx + cy], dst_ref=outs[i].at[r], send_sem=send_sems.at[3 * i + r],
                                                          recv_sem=recv_sems.at[3 * i + r], device_id=(cx, cy, c), device_id_type=MESH))
        for d in sends:
            d.start()
        for d in sends:
            d.wait_recv()
        for d in sends:
            d.wait_send()

    return pl.pallas_call(
        body, name="grad_chip_scatter", in_specs=[ANY] * n, out_specs=[ANY] * n,
        out_shape=[jax.ShapeDtypeStruct((3,) + p.shape[1:], p.dtype) for p in ps],
        scratch_shapes=[pltpu.SemaphoreType.DMA((3 * n,)), pltpu.SemaphoreType.DMA((3 * n,))],
    )(*ps)


def _sibling_share(rs):
    n = len(rs)

    def body(*refs):
        ins, outs, send_sems, recv_sems = refs[:n], refs[n:2 * n], refs[2 * n], refs[2 * n + 1]
        x, y, c, _ = _place()
        cps = [pltpu.make_async_remote_copy(src_ref=ins[i], dst_ref=outs[i], send_sem=send_sems.at[i], recv_sem=recv_sems.at[i],
                                            device_id=(x, y, 1 - c), device_id_type=MESH) for i in range(n)]
        for d in cps:
            d.start()
        for d in cps:
            d.wait_recv()
        for d in cps:
            d.wait_send()

    return pl.pallas_call(
        body, name="grad_sibling_share", in_specs=[ANY] * n, out_specs=[ANY] * n,
        out_shape=[jax.ShapeDtypeStruct(r.shape, r.dtype) for r in rs],
        scratch_shapes=[pltpu.SemaphoreType.DMA((n,)), pltpu.SemaphoreType.DMA((n,))],
    )(*rs)


def _allreduce_small(s):
    rows, w = s.shape
    n_dev = 8

    def body(s_ref, out_ref, slots, send_sems, recv_sems):
        x, y, c, _ = _place()
        me = 4 * x + 2 * y + c
        slots[me] = s_ref[...]
        peers = []
        for r in range(1, n_dev):
            px = 1 - x if r & 4 else x
            py = 1 - y if r & 2 else y
            pc = 1 - c if r & 1 else c
            peers.append((px, py, pc))
        sends = [pltpu.make_async_remote_copy(src_ref=s_ref, dst_ref=slots.at[me], send_sem=send_sems.at[r], recv_sem=recv_sems.at[r],
                                              device_id=peer, device_id_type=MESH) for r, peer in enumerate(peers)]
        for d in sends:
            d.start()
        for r, (px, py, pc) in enumerate(peers):
            pltpu.make_async_remote_copy(src_ref=s_ref, dst_ref=slots.at[4 * px + 2 * py + pc], send_sem=send_sems.at[r],
                                         recv_sem=recv_sems.at[r], device_id=(x, y, c), device_id_type=MESH).wait_recv()
        for d in sends:
            d.wait_send()
        acc = slots[0]
        for k in range(1, n_dev):
            acc = acc + slots[k]
        out_ref[...] = acc

    vm = pl.BlockSpec(memory_space=pltpu.VMEM)
    return pl.pallas_call(
        body, name="allreduce_small", in_specs=[vm], out_specs=vm, out_shape=jax.ShapeDtypeStruct((rows, w), F32),
        scratch_shapes=[pltpu.VMEM((n_dev, rows, w), F32), pltpu.SemaphoreType.DMA((n_dev - 1,)), pltpu.SemaphoreType.DMA((n_dev - 1,))],
    )(s)


W_IN_SHARD = D_IN // N_CHIP
W_IN_ROWS_G = 2304
REDUCED = tuple(m for m in MATS if m[0] != "conv_w")
CONV_W_SIZE = 3 * 2 * D_FF


def _weight_send(name, a):
    if name == "w_in":
        return jnp.swapaxes(a, 1, 2).astype(BF16)
    return a if name == "conv_w" else a.astype(BF16)


def _full_weights(gathered, l):
    g = {k: v[:, l] for k, v in gathered.items()}
    s = g["w_in"].astype(F32).reshape(D_IN, D_MODEL)
    dup = lambda a: jnp.concatenate([a[0:64], a[0:64], a[64:128], a[64:128]], axis=0)
    o = ORIG
    wm_t = jnp.concatenate([s[o["gate"]:], s[o["a"]:o["bq"]], s[o["bq"]:o["bk"]], dup(s[o["bk"]:o["bv"]]), dup(s[o["bv"]:o["cq"]]),
                            s[o["cq"]:o["gate"]], jnp.zeros((M_COLS - M_CDKV - (o["gate"] - o["cdkv"]), D_MODEL), F32)], axis=0).astype(BF16)
    full = {name: jnp.moveaxis(g[name], 0, ax).reshape(shape) for name, shape, ax in MATS if name != "w_in"}
    uq = full["w_uq"].reshape(C_Q_RANK, N_HEADS, C_NOPE + C_ROPE)
    ukv = full["w_ukv"].reshape(C_KV_RANK, N_HEADS, 2 * C_NOPE)
    w_uq_p = _pad_lanes(uq).reshape(C_Q_RANK, N_HEADS * LANE)
    w_ukv_p = jnp.concatenate([_pad_lanes(ukv[:, :, :C_NOPE]).reshape(C_KV_RANK, N_HEADS * LANE),
                               ukv[:, :, C_NOPE:].reshape(C_KV_RANK, N_HEADS * HEAD_DIM)], axis=1)
    return {"wm_t": wm_t, "w_uq_p": w_uq_p, "w_ukv_p": w_ukv_p, "w_branch": full["w_branch"], "w_out": full["w_out"],
            "wup_g": full["w_ffn_up"][:, :D_FF], "wup_v": full["w_ffn_up"][:, D_FF:], "conv_w": full["conv_w"],
            "w_ffn_down": full["w_ffn_down"]}


def _grad_send(name, g, shape, ax):
    if name == "w_in":
        return jnp.pad(g.reshape(N_CHIP, W_IN_SHARD, D_MODEL), ((0, 0), (0, W_IN_ROWS_G - W_IN_SHARD), (0, 0)))
    split = shape[:ax] + (N_CHIP, shape[ax] // N_CHIP) + shape[ax + 1:]
    return jnp.moveaxis(g.reshape(split), ax, 0)


def _grad_recv(name, r):
    return r[:W_IN_SHARD].T if name == "w_in" else r


def _pack_small(rel, small, conv_w, extra):
    parts = [rel.reshape(-1)]
    for l in range(DEPTH):
        for name in SMALL:
            parts.append(small[name][l].reshape(-1))
    parts += [conv_w.reshape(-1), extra]
    flat = jnp.concatenate(parts)
    rows = _ceil_to(-(-flat.shape[0] // LANE), 8)
    return jnp.pad(flat, (0, rows * LANE - flat.shape[0])).reshape(rows, LANE)


def _unpack_small(buf):
    flat = buf.reshape(-1)
    rel = flat[:REL_BUCKETS * 32].reshape(REL_BUCKETS, 32)
    off = REL_BUCKETS * 32
    small = {name: [] for name in SMALL}
    for l in range(DEPTH):
        for name in SMALL:
            n = SMALL_SIZES[name]
            small[name].append(flat[off:off + n])
            off += n
    conv_w = flat[off:off + DEPTH * CONV_W_SIZE].reshape(DEPTH, 3, 2 * D_FF)
    off += DEPTH * CONV_W_SIZE
    return rel, {k: jnp.stack(v) for k, v in small.items()}, conv_w, flat[off:off + LANE]


def _rows2d(a, lead):
    return a.reshape(a.shape[:lead] + (-1, a.shape[-1]))


def _row_tile(rows):
    for cand in (512, 256, 128, 64, 32, 16, 8):
        if rows % cand == 0:
            return cand
    raise ValueError(rows)


def _pair_add(g, got, core, *, name):
    g2, got2 = _rows2d(g, 1), _rows2d(got, 0)
    rows, c = got2.shape
    tm = _row_tile(rows)
    flag = jnp.zeros((1, 1, LANE), F32) + core.astype(F32)

    def fn(i, nt, a0, a1, b, f):
        return jnp.where(f[:, 0:1] == 0.0, a0, a1) + b

    stacked = g2.reshape(1, 2 * rows, c)
    out = _rowwise(fn, [(stacked, c, 0, 0), (stacked, c, 0, rows // tm), (got2[None], c, 0)], pars=[(flag, LANE, 0)],
                   outs=[(c, c, 0, BF16)], tm=tm, t=rows, name=name)[0][0]
    return out.reshape(got.shape)


def _chip_add(own, got, *, name):
    own2, got2 = _rows2d(own, 0), _rows2d(got, 1)
    rows, c = own2.shape
    tm = _row_tile(rows)

    def fn(i, nt, a, b0, b1, b2):
        return ((a.astype(F32) + b0.astype(F32)) + b1.astype(F32)) + b2.astype(F32)

    stacked = got2.reshape(1, 3 * rows, c)
    out = _rowwise(fn, [(own2[None], c, 0)] + [(stacked, c, 0, k * (rows // tm)) for k in range(3)],
                   outs=[(c, c, 0, F32)], tm=tm, t=rows, name=name)[0][0]
    return out.reshape(own.shape)


def _perm(a, d):
    if d == 1:
        return a
    t = a.shape[0]
    return jnp.swapaxes(a.reshape((t // d, d) + a.shape[1:]), 0, 1).reshape(a.shape)


def _unperm(a, d):
    if d == 1:
        return a
    t = a.shape[0]
    return jnp.swapaxes(a.reshape((d, t // d) + a.shape[1:]), 0, 1).reshape(a.shape)


def _pad_lanes(a, w=HP):
    return jnp.pad(a, [(0, 0)] * (a.ndim - 1) + [(0, w - a.shape[-1])])


def _rope_tables(t):
    pos = jnp.arange(t, dtype=F32)
    inv_freq = ROPE_BASE ** (-jnp.arange(0, C_ROPE, 2, dtype=F32) / C_ROPE)
    ang = pos[:, None] * inv_freq[None, :]
    cos, sin = jnp.cos(ang), jnp.sin(ang)
    ones, zeros = jnp.ones((t, C_NOPE), F32), jnp.zeros((t, C_NOPE), F32)
    tail = LANE - C_NOPE - C_ROPE
    c = jnp.concatenate([ones, cos, cos, ones[:, :tail]], axis=1)
    s = jnp.concatenate([zeros, -sin, sin, zeros[:, :tail]], axis=1)
    return c, s


def _band_calls(sinks):
    none = jnp.full((N_HEADS,), NEG, F32)
    a0, per_group = M_A0 // LANE, A_COLS // LANE
    calls = [((a0 + gi * per_group, a0 + gi * per_group + 4, a0 + gi * per_group + 8), d, BLK, False, none) for gi, d in enumerate(A_DILS)]
    calls.append(((M_BQ // LANE, M_BK // LANE, M_BV // LANE), 1, BLK - 1, True, sinks.astype(F32)))
    return calls


def _layer_fwd(l, x, xb, w, p, biases, rope_cs):
    t = x.shape[0]
    n = f"l{l}_"
    proj = _mm(xb, w["wm_t"], tb=True, out_dtype=BF16, tm=TM_TOKENS, name=n + "proj")
    s = {"xb": xb, "proj": proj}

    outs = [_band_fwd(proj, offs, biases[i], sk, dil=d, lim=lim, gqa=gqa, name=n + f"band{i}")
            for i, (offs, d, lim, gqa, sk) in enumerate(_band_calls(p["sinks"]))]
    os_, lses = [outs[gi][0] for gi in range(3)], [outs[gi][1] for gi in range(3)]
    o_a = _combine_fwd(os_, lses, name=n + "combine_fwd")
    o_b_f, lse_b = outs[3]
    o_b = o_b_f.astype(BF16)
    s.update(os=os_, lses=lses, o_b=o_b_f, lse_b=lse_b)

    rq = _rms_fwd(proj, C_Q_RANK, M_CQ // C_Q_RANK, p["q_norm_g"], name=n + "rms_q")
    rkv = _rms_fwd(proj, C_KV_RANK, M_CDKV // C_KV_RANK, p["kv_norm_g"], name=n + "rms_kv")
    q_cp = _mm(rq, w["w_uq_p"], out_dtype=BF16, name=n + "uq")
    kv_cp = _mm(rkv, w["w_ukv_p"], out_dtype=BF16, name=n + "ukv")
    q_full = _rope_slabs(q_cp, N_HEADS, rope_cs[0], rope_cs[1], name=n + "rope_q")
    k_full = _rope_slabs(kv_cp, N_HEADS, rope_cs[0], rope_cs[1], add=(proj, (M_CDKV + C_KV_RANK) // LANE), name=n + "rope_k")
    vt = jnp.transpose(kv_cp[:, N_HEADS * LANE:].T.reshape(N_HEADS // 2, LANE, t // TQ, TQ), (0, 2, 1, 3))
    o_c_f, lse_c = _mla_fwd(q_full, k_full, vt, name=n + "mla_fwd")
    o_c = o_c_f.astype(BF16)
    s.update(rq=rq, rkv=rkv, q_full=q_full, k_full=k_full, kv_cp=kv_cp, lse_c=lse_c, o_c=o_c_f)

    obs = [o_a, o_b, o_c]
    ys = [_mm(o, w["w_branch"][i], out_dtype=BF16, name=n + f"branch{i}") for i, o in enumerate(obs)]
    merged = _merge_fwd(proj, p["b_gate"], ys, name=n + "merge")
    mix = _mm(merged, w["w_out"], name=n + "out")
    x1f, x1b, z1 = _ln_fwd(x, mix, p["ln1_g"], p["ln1_b"], name=n + "ln1")
    s.update(obs=obs, ys=ys, merged=merged, z1=z1, x1b=x1b)

    ug = _mm(x1b, w["wup_g"], tm=TM_TOKENS, tn=1408, out_dtype=BF16, name=n + "up_g")
    uv = _mm(x1b, w["wup_v"], tm=TM_TOKENS, tn=1408, out_dtype=BF16, name=n + "up_v")
    h = _glu_fwd(ug, uv, w["conv_w"], p["conv_b"], name=n + "glu")
    ff = _mm(h, w["w_ffn_down"], tm=TM_TOKENS, tk=1408, name=n + "down")
    x2f, x2b, z2 = _ln_fwd(x1f, ff, p["ln2_g"], p["ln2_b"], name=n + "ln2")
    s.update(ug=ug, uv=uv, h=h, z2=z2)
    return x2f, x2b, s


def _layer_bwd(l, s, dys, coefs, w, p, biases, rope_cs):
    n = f"l{l}b_"
    t = s["z2"].shape[0]
    gw, gs = {}, {}

    dz2, dz2b, gs["ln2_g"], gs["ln2_b"] = _ln_bwd(s["z2"], p["ln2_g"], dys, coefs, name=n + "ln2")
    dh = _mm(dz2b, w["w_ffn_down"], tb=True, tm=TM_TOKENS, tn=1408, out_dtype=BF16, name=n + "d_h")
    gw["w_ffn_down"] = _mm(s["h"], dz2b, ta=True, tm=1408, tk=1024, name=n + "g_down")
    dcg, dcv, gw["conv_w"], gs["conv_b"] = _glu_bwd_a(s["ug"], s["uv"], w["conv_w"], p["conv_b"], dh, name=n + "glu_a")
    dug = _glu_bwd_b(dcg, w["conv_w"], 0, name=n + "glu_bg")
    duv = _glu_bwd_b(dcv, w["conv_w"], 1, name=n + "glu_bv")
    dx1_g = _mm(dug, w["wup_g"], tb=True, tm=TM_TOKENS, tk=1408, name=n + "d_x1g")
    dx1_v = _mm(duv, w["wup_v"], tb=True, tm=TM_TOKENS, tk=1408, name=n + "d_x1v")
    gw["w_ffn_up"] = jnp.concatenate([_mm(s["x1b"], dug, ta=True, tn=1408, tk=1024, name=n + "g_upg"),
                                      _mm(s["x1b"], duv, ta=True, tn=1408, tk=1024, name=n + "g_upv")], axis=1)

    dz1, dz1b, gs["ln1_g"], gs["ln1_b"] = _ln_bwd(s["z1"], p["ln1_g"], [dz2, dx1_g, dx1_v], [ALPHA, 1.0, 1.0], name=n + "ln1")
    dmerged = _mm(dz1b, w["w_out"], tb=True, out_dtype=BF16, name=n + "d_merged")
    gw["w_out"] = _mm(s["merged"], dz1b, ta=True, name=n + "g_out")
    dys_b, dgp, gs["b_gate"] = _merge_bwd(s["proj"], p["b_gate"], s["ys"], dmerged, name=n + "merge")
    dos = [_mm(dy, w["w_branch"][i], tb=True, out_dtype=BF16, name=n + f"d_o{i}") for i, dy in enumerate(dys_b)]
    gw["w_branch"] = jnp.stack([_mm(o, dy, ta=True, name=n + f"g_branch{i}") for i, (o, dy) in enumerate(zip(s["obs"], dys_b))])

    do_gs, dpr_gs = _combine_bwd(s["os"], s["lses"], dos[0], name=n + "combine")
    do_b, dpr_b = _delta(dos[1], s["o_b"], name=n + "delta_b")
    do_list, dpr_list, lse_list = do_gs + [do_b], dpr_gs + [dpr_b], s["lses"] + [s["lse_b"]]
    band = [_band_bwd(s["proj"], offs, do_list[i], lse_list[i], dpr_list[i], biases[i], sk, dil=d, lim=lim, gqa=gqa, name=n + f"band{i}")
            for i, (offs, d, lim, gqa, sk) in enumerate(_band_calls(p["sinks"]))]
    gs["sinks"] = band[3][4][:, 0, 0]
    ds_sum = jnp.concatenate([b_[3] for b_ in band], axis=0)

    do_c, delta_c = _delta(dos[2], s["o_c"], name=n + "delta_c")
    dq, dk, dv = _mla_bwd(s["q_full"], s["k_full"], s["kv_cp"], do_c, s["lse_c"], delta_c, name=n + "mla")
    dq_cp = _rope_slabs(dq, N_HEADS, rope_cs[0], -rope_cs[1], name=n + "rope_q")
    dk_sum = _rowwise(lambda i, nt, *vs: sum(vs[1:], vs[0]), [(dk[None], LANE, hh) for hh in range(N_HEADS)],
                      outs=[(LANE, LANE, 0, F32)], tm=1024, name=n + "krope_sum")[0][0]
    dkr = _rope_slabs(dk_sum, 1, rope_cs[0], -rope_cs[1], to_front=True, name=n + "rope_k")
    dkv_cp = jnp.concatenate([dk, dv], axis=1)
    d_rq = _mm(dq_cp, w["w_uq_p"], tb=True, name=n + "d_rq")
    d_rkv = _mm(dkv_cp, w["w_ukv_p"], tb=True, name=n + "d_rkv")
    g_uq = _mm(s["rq"], dq_cp, ta=True, name=n + "g_uq")
    g_ukv = _mm(s["rkv"], dkv_cp, ta=True, name=n + "g_ukv")
    gw["w_uq"] = g_uq.reshape(C_Q_RANK, N_HEADS, LANE)[:, :, :C_NOPE + C_ROPE].reshape(C_Q_RANK, -1)
    kw = N_HEADS * LANE
    gw["w_ukv"] = jnp.concatenate([g_ukv[:, :kw].reshape(C_KV_RANK, N_HEADS, LANE)[:, :, :C_NOPE],
                                   g_ukv[:, kw:].reshape(C_KV_RANK, N_HEADS, HEAD_DIM)], axis=2).reshape(C_KV_RANK, -1)
    dcq, gs["q_norm_g"] = _rms_bwd(s["proj"], C_Q_RANK, M_CQ // C_Q_RANK, p["q_norm_g"], d_rq, name=n + "rms_q")
    dckv, gs["kv_norm_g"] = _rms_bwd(s["proj"], C_KV_RANK, M_CDKV // C_KV_RANK, p["kv_norm_g"], d_rkv, name=n + "rms_kv")
    dcdkv = jnp.concatenate([dckv, dkr], axis=1)

    pieces = dgp + [a for b_ in band for a in b_[:3]] + [dcq, dcdkv]
    dproj = jnp.concatenate(pieces, axis=1)
    dx_proj = _mm(dproj, w["wm_t"], tm=TM_TOKENS, tk=1024, name=n + "d_x")
    g_main = _mm(dproj, s["xb"], ta=True, name=n + "g_in")
    fold = lambda a, tag: _sum_rows([a.reshape(2, 2, HEAD_DIM, D_MODEL)[:, j] for j in range(2)], tm=HEAD_DIM,
                                    name=n + "g_fold_" + tag).reshape(2 * HEAD_DIM, D_MODEL)
    gw["w_in"] = jnp.concatenate([g_main[M_A0:M_BK], fold(g_main[M_BK:M_BV], "k"), fold(g_main[M_BV:M_CQ], "v"),
                                  g_main[M_CQ:M_CDKV + C_KV_RANK + C_ROPE], g_main[M_GATE:M_A0]], axis=0)
    return [dz1, dx_proj], [ALPHA, 1.0], gw, gs, ds_sum


def _local_step(x, target, ws, rel_table, small):
    t = x.shape[0]
    ps = [{k: small[k][l] for k in SMALL} for l in range(DEPTH)]
    bucket = _bucket_index()
    bias_all = _bias_lookup(bucket, rel_table.T, name="bias_lookup").reshape(4, N_HEADS, BLK, 2 * BLK)
    biases = [bias_all[i] for i in range(4)]
    rope_cs = _rope_tables(t)

    saved, h, hb = [], x, x.astype(BF16)
    for l in range(DEPTH):
        h, hb, s = _layer_fwd(l, h, hb, ws[l], ps[l], biases, rope_cs)
        saved.append(s)
    dy, loss_part = _loss_and_grad(h, target, name="loss")

    dys, coefs = [dy], [1.0]
    gws, gss, dss = [None] * DEPTH, [None] * DEPTH, [None] * DEPTH
    for l in reversed(range(DEPTH)):
        dys, coefs, gws[l], gss[l], dss[l] = _layer_bwd(l, saved[l], dys, coefs, ws[l], ps[l], biases, rope_cs)
    grad_x = _lincomb(dys, coefs, name="grad_x")
    npos = 2 * BLK * BLK
    g_rel = _bias_grad(bucket, dss[0].reshape(4 * N_HEADS, npos), dss[1].reshape(4 * N_HEADS, npos), name="bias_grad").T
    gsmall = {k: jnp.stack([gss[l][k] for l in range(DEPTH)]) for k in SMALL}
    return loss_part, grad_x, gws, gsmall, g_rel


def kernel(x, rel_table, w_in, b_gate, sinks, q_norm_g, kv_norm_g, w_uq, w_ukv, w_branch, w_out, ln1_g, ln1_b, w_ffn_up, conv_w, conv_b, w_ffn_down, ln2_g, ln2_b, loss_target, m_rel_table, m_w_in, m_b_gate, m_sinks, m_q_norm_g, m_kv_norm_g, m_w_uq, m_w_ukv, m_w_branch, m_w_out, m_ln1_g, m_ln1_b, m_w_ffn_up, m_conv_w, m_conv_b, m_w_ffn_down, m_ln2_g, m_ln2_b, v_rel_table, v_w_in, v_b_gate, v_sinks, v_q_norm_g, v_kv_norm_g, v_w_uq, v_w_ukv, v_w_branch, v_w_out, v_ln1_g, v_ln1_b, v_w_ffn_up, v_conv_w, v_conv_b, v_w_ffn_down, v_ln2_g, v_ln2_b):
    wts = dict(rel_table=rel_table, w_in=w_in, b_gate=b_gate, sinks=sinks, q_norm_g=q_norm_g, kv_norm_g=kv_norm_g, w_uq=w_uq,
               w_ukv=w_ukv, w_branch=w_branch, w_out=w_out, ln1_g=ln1_g, ln1_b=ln1_b, w_ffn_up=w_ffn_up, conv_w=conv_w,
               conv_b=conv_b, w_ffn_down=w_ffn_down, ln2_g=ln2_g, ln2_b=ln2_b)
    ms = dict(rel_table=m_rel_table, w_in=m_w_in, b_gate=m_b_gate, sinks=m_sinks, q_norm_g=m_q_norm_g, kv_norm_g=m_kv_norm_g,
              w_uq=m_w_uq, w_ukv=m_w_ukv, w_branch=m_w_branch, w_out=m_w_out, ln1_g=m_ln1_g, ln1_b=m_ln1_b, w_ffn_up=m_w_ffn_up,
              conv_w=m_conv_w, conv_b=m_conv_b, w_ffn_down=m_w_ffn_down, ln2_g=m_ln2_g, ln2_b=m_ln2_b)
    vs = dict(rel_table=v_rel_table, w_in=v_w_in, b_gate=v_b_gate, sinks=v_sinks, q_norm_g=v_q_norm_g, kv_norm_g=v_kv_norm_g,
              w_uq=v_w_uq, w_ukv=v_w_ukv, w_branch=v_w_branch, w_out=v_w_out, ln1_g=v_ln1_g, ln1_b=v_ln1_b, w_ffn_up=v_w_ffn_up,
              conv_w=v_conv_w, conv_b=v_conv_b, w_ffn_down=v_w_ffn_down, ln2_g=v_ln2_g, ln2_b=v_ln2_b)

    core = lax.axis_index("c")
    chip = 2 * lax.axis_index("x") + lax.axis_index("y")

    names = [name for name, _, _ in MATS]
    sent = [_weight_send(name, wts[name]) for name in names]
    got = _allgather_weights(sent)
    gathered = {name: lax.dynamic_update_slice(g, s[None], (chip,) + (0,) * s.ndim) for name, g, s in zip(names, got, sent)}
    ws = [_full_weights(gathered, l) for l in range(DEPTH)]

    small = {k: wts[k] for k in SMALL}
    loss_part, grad_x, gws, gsmall, g_rel = _local_step(x[0], loss_target[0], ws, rel_table, small)

    rnames = [name for name, _, _ in REDUCED]
    gsend = [jnp.stack([_grad_send(name, gws[l][name], shape, ax) for l in range(DEPTH)]) for name, shape, ax in REDUCED]
    theirs = _sibling_swap(gsend)
    pairs = [_pair_add(g, t_, core, name="grad_pair_" + name) for name, g, t_ in zip(rnames, gsend, theirs)]
    arrived = _chip_scatter(pairs)
    reduced = [_chip_add(lax.dynamic_index_in_dim(p, chip, 0, keepdims=False), a, name="grad_chip_" + name)
               for name, p, a in zip(rnames, pairs, arrived)]
    others = _sibling_share(reduced)
    gshard = {}
    for name, mine, other in zip(rnames, reduced, others):
        layers = [jnp.where(core == l, mine, other) for l in range(DEPTH)]
        gshard[name] = jnp.stack([_grad_recv(name, a) for a in layers])

    conv_w_full = jnp.stack([gws[l]["conv_w"] for l in range(DEPTH)])
    small_red = _allreduce_small(_pack_small(g_rel, gsmall, conv_w_full, loss_part))
    g_rel_r, gsmall_r, conv_w_r, loss_vec = _unpack_small(small_red)
    loss = loss_vec[0]
    shard_w = 2 * D_FF // N_CHIP
    gshard["conv_w"] = lax.dynamic_slice_in_dim(conv_w_r, chip * shard_w, shard_w, axis=2)

    grads = dict(gshard)
    grads.update(gsmall_r)
    grads["rel_table"] = g_rel_r
    deltas, new_m, new_v = {}, {}, {}
    for name, _, _ in MATS:
        shp = wts[name].shape
        v2 = lambda a: a.reshape(-1, shp[-1])
        d_, m_, v_ = _adamw(v2(wts[name]), v2(grads[name]), v2(ms[name]), v2(vs[name]), name="adamw_" + name)
        deltas[name], new_m[name], new_v[name] = d_.reshape(shp), m_.reshape(shp), v_.reshape(shp)
    zero, none = jnp.zeros((LANE,), F32), jnp.zeros((0,), F32)
    sw = _pack_small(wts["rel_table"], {k: wts[k] for k in SMALL}, none, zero)
    sm = _pack_small(ms["rel_table"], {k: ms[k] for k in SMALL}, none, zero)
    sv = _pack_small(vs["rel_table"], {k: vs[k] for k in SMALL}, none, zero)
    sg = _pack_small(g_rel_r, gsmall_r, none, zero)
    sd, smn, svn = _adamw(sw, sg, sm, sv, name="adamw_small")
    for res, buf in ((deltas, sd), (new_m, smn), (new_v, svn)):
        rel_, sm_ = _unpack_small(jnp.pad(buf, ((0, small_red.shape[0] - buf.shape[0]), (0, 0))))[:2]
        res["rel_table"] = rel_
        res.update(sm_)

    return (loss, grad_x[None], *[grads[k] for k in WEIGHT_ORDER], *[deltas[k] for k in WEIGHT_ORDER],
            *[new_m[k] for k in WEIGHT_ORDER], *[new_v[k] for k in WEIGHT_ORDER])
```

```python
import math

import jax
import jax.numpy as jnp
from jax import lax
from jax.experimental import pallas as pl
from jax.experimental.pallas import tpu as pltpu

F32 = jnp.float32
BF16 = jnp.bfloat16
MESH = pl.DeviceIdType.MESH

D_MODEL = 1024
DEPTH = 2
HEAD_DIM = 64
N_HEADS = 8
A_DILS = (1, 4, 16)
C_Q_RANK = 256
C_KV_RANK = 128
C_NOPE = 64
C_ROPE = 32
ROPE_BASE = 10000.0
REL_BUCKETS = 32
REL_MAX_DIST = 2048
D_FF = 2816
ALPHA = (2 * DEPTH) ** 0.25
LN_EPS = 1e-5
RMS_EPS = 1e-6
NEG = -1e30
LOG2E, LN2 = math.log2(math.e), math.log(2.0)
ADAM_LR, ADAM_B1, ADAM_B2, ADAM_EPS, ADAM_WD, ADAM_STEP = 0.001, 0.9, 0.999, 1e-08, 0.01, 10

VMEM_LIMIT_BYTES = 56 * 1024 * 1024
LANE = 128
BLK = 128
TQ = 512
HP = 128
TM_TOKENS = 2048
HALO = 16
BAND_SCALE = HEAD_DIM ** -0.5
BAND_UNROLL = 16

D_IN = 8864
A_COLS = 3 * N_HEADS * HEAD_DIM
ORIG = {"a": 0, "bq": 4608, "bk": 5120, "bv": 5248, "cq": 5376, "cdkv": 5632, "gate": 5792}
M_GATE, M_A0, M_BQ, M_BK, M_BV, M_CQ, M_CDKV, M_COLS = 0, 3072, 4608, 5120, 5376, 5632, 5888, 6144

N_CHIP = 4
MATS = (
    ("w_in", (D_MODEL, D_IN), 1),
    ("w_uq", (C_Q_RANK, 768), 1),
    ("w_ukv", (C_KV_RANK, 1024), 1),
    ("w_branch", (3, 512, D_MODEL), 2),
    ("w_out", (D_MODEL, D_MODEL), 0),
    ("w_ffn_up", (D_MODEL, 2 * D_FF), 1),
    ("conv_w", (3, 2 * D_FF), 1),
    ("w_ffn_down", (D_FF, D_MODEL), 0),
)
SMALL = ("b_gate", "sinks", "q_norm_g", "kv_norm_g", "ln1_g", "ln1_b", "conv_b", "ln2_g", "ln2_b")
SMALL_SIZES = {"b_gate": 3072, "sinks": 8, "q_norm_g": 256, "kv_norm_g": 128, "ln1_g": 1024, "ln1_b": 1024,
               "conv_b": 5632, "ln2_g": 1024, "ln2_b": 1024}
WEIGHT_ORDER = ("rel_table", "w_in", "b_gate", "sinks", "q_norm_g", "kv_norm_g", "w_uq", "w_ukv", "w_branch",
                "w_out", "ln1_g", "ln1_b", "w_ffn_up", "conv_w", "conv_b", "w_ffn_down", "ln2_g", "ln2_b")


def _cparams(sem):
    return pltpu.CompilerParams(dimension_semantics=sem, vmem_limit_bytes=VMEM_LIMIT_BYTES)


def _ceil_to(n, m):
    return -(-n // m) * m


def _pick(n, target):
    if n <= target:
        return n
    best = None
    for t in range(LANE, target + 1, LANE):
        if n % t == 0:
            best = t
    assert best is not None, (n, target)
    return best


def _mm(a, b, *, ta=False, tb=False, out_dtype=F32, tm=1024, tn=1024, tk=2048, name):
    assert not (ta and tb)
    k, m = a.shape[::-1] if not ta else a.shape
    n = b.shape[0] if tb else b.shape[1]
    assert (b.shape[1] if tb else b.shape[0]) == k
    tm, tn, tk = _pick(m, tm), _pick(n, tn), _pick(k, tk)
    nk = k // tk
    dn = (((0 if ta else 1,), (1 if tb else 0,)), ((), ()))

    def body(a_ref, b_ref, o_ref, acc_ref):
        part = lax.dot_general(a_ref[...].astype(BF16), b_ref[...].astype(BF16), dn, preferred_element_type=F32)
        if nk == 1:
            o_ref[...] = part.astype(o_ref.dtype)
        else:
            kk = pl.program_id(2)

            @pl.when(kk == 0)
            def _():
                acc_ref[...] = part

            @pl.when(kk > 0)
            def _():
                acc_ref[...] += part

            @pl.when(kk == nk - 1)
            def _():
                o_ref[...] = acc_ref[...].astype(o_ref.dtype)

    a_spec = pl.BlockSpec((tk, tm), lambda i, j, kk: (kk, i)) if ta else pl.BlockSpec((tm, tk), lambda i, j, kk: (i, kk))
    b_spec = pl.BlockSpec((tn, tk), lambda i, j, kk: (j, kk)) if tb else pl.BlockSpec((tk, tn), lambda i, j, kk: (kk, j))
    return pl.pallas_call(
        body, name=name, grid=(m // tm, n // tn, nk),
        in_specs=[a_spec, b_spec],
        out_specs=pl.BlockSpec((tm, tn), lambda i, j, kk: (i, j)),
        out_shape=jax.ShapeDtypeStruct((m, n), out_dtype),
        scratch_shapes=[pltpu.VMEM((tm, tn) if nk > 1 else (8, LANE), F32)],
        compiler_params=_cparams(("parallel", "parallel", "arbitrary")),
    )(a, b)


def _rowwise(fn, rows, *, pars=(), halos=(), outs=(), accs=(), tm, name, ncol=1, t=None):
    nb = rows[0][0].shape[0]
    t = rows[0][0].shape[1] if t is None else t
    tm = min(tm, t)
    assert t % tm == 0 and tm % 8 == 0
    nt = t // tm
    in_specs, args = [], []
    for spec in rows:
        arr, c, off = spec[:3]
        rb = spec[3] if len(spec) > 3 else 0
        in_specs.append(pl.BlockSpec((1, tm, c), lambda b, cc, i, off=off, rb=rb: (b, i + rb, off + cc)))
        args.append(arr)
    for arr, c, off, kind in halos:
        if kind == "prev":
            im = lambda b, cc, i, off=off: (b, jnp.maximum(i * (tm // HALO) - 1, 0), off + cc)
        else:
            im = lambda b, cc, i, off=off: (b, jnp.minimum((i + 1) * (tm // HALO), t // HALO - 1), off + cc)
        in_specs.append(pl.BlockSpec((1, HALO, c), im))
        args.append(arr)
    for arr, c, off in pars:
        bp, r = arr.shape[:2]
        if bp > 1:
            im = lambda b, cc, i, off=off: (b, 0, off + cc)
        else:
            im = lambda b, cc, i, off=off: (0, 0, off + cc)
        in_specs.append(pl.BlockSpec((1, r, c), im))
        args.append(arr)
    out_specs, out_shapes = [], []
    for ctot, c, off, dt in outs:
        out_specs.append(pl.BlockSpec((1, tm, c), lambda b, cc, i, off=off: (b, i, off + cc)))
        out_shapes.append(jax.ShapeDtypeStruct((nb, t, ctot), dt))
    for r, ctot, c, off in accs:
        out_specs.append(pl.BlockSpec((1, r, c), lambda b, cc, i, off=off: (b, 0, off + cc)))
        out_shapes.append(jax.ShapeDtypeStruct((nb, r, ctot), F32))
    n_in, n_out = len(args), len(outs)

    def body(*refs):
        i = pl.program_id(2)
        res = fn(i, nt, *[r[0].astype(F32) for r in refs[:n_in]])
        if not isinstance(res, (tuple, list)):
            res = (res,)
        for o_ref, val in zip(refs[n_in:n_in + n_out], res[:n_out]):
            o_ref[0] = val.astype(o_ref.dtype)
        for a_ref, val in zip(refs[n_in + n_out:], res[n_out:]):
            @pl.when(i == 0)
            def _(a_ref=a_ref, val=val):
                a_ref[0] = val

            @pl.when(i > 0)
            def _(a_ref=a_ref, val=val):
                a_ref[0] += val

    res = pl.pallas_call(
        body, name=name, grid=(nb, ncol, nt), in_specs=in_specs, out_specs=out_specs, out_shape=out_shapes,
        compiler_params=_cparams(("parallel", "parallel", "arbitrary")),
    )(*args)
    return res


def _dot(a, b):
    return lax.dot_general(a, b, (((1,), (0,)), ((), ())), preferred_element_type=F32)


def _dot_nt(a, b):
    return lax.dot_general(a, b, (((1,), (1,)), ((), ())), preferred_element_type=F32)


def _dot_tn(a, b):
    return lax.dot_general(a, b, (((0,), (0,)), ((), ())), preferred_element_type=F32)


def _rows(parts):
    return jnp.concatenate(parts, axis=0)


def _lane_lo():
    return lax.broadcasted_iota(jnp.int32, (1, LANE), 1) < HEAD_DIM


def _blocks(a):
    return [a[i * BLK:(i + 1) * BLK] for i in range(a.shape[0] // BLK)]


def _band_geometry(t):
    rows = min(BAND_UNROLL, t // BLK) * BLK
    assert t % rows == 0
    return rows, t // rows


def _band_operands(g, k_ref, v_ref, rows):
    start = pl.multiple_of(g * rows, rows)
    pstart = pl.multiple_of(jnp.maximum(g * rows - BLK, 0), BLK)
    out = []
    for ref in (k_ref, v_ref):
        cur = ref[pl.ds(start, rows), :]
        raw = ref[pl.ds(pstart, rows), :]
        shifted = _rows([jnp.zeros((BLK, LANE), raw.dtype), raw[:rows - BLK]])
        out += [_blocks(cur), _blocks(jnp.where(g == 0, shifted, raw))]
    return out


def _band_scores(g, qa, kc, kp, b_ref, a, nb):
    u = len(qa)
    assert nb % u == 0 or u % nb == 0
    tile = lambda blk: _rows([blk] * u)
    firsts = []
    for i in range(u):
        if nb >= u:
            val = jnp.where(lax.rem(g * u, nb) == 0, NEG, 0.0).astype(F32) if i == 0 else 0.0
        else:
            val = NEG if i % nb == 0 else 0.0
        firsts.append(jnp.zeros((BLK, 1), F32) + val)
    sc = _rows([_dot_nt(q, k) for q, k in zip(qa, kc)]) + tile(b_ref[a, :, BLK:2 * BLK])
    sp = _rows([_dot_nt(q, k) for q, k in zip(qa, kp)]) + tile(b_ref[a, :, 0:BLK]) + _rows(firsts)
    return sc, sp


def _band_fwd(src, offs, bias, sinks, *, nb, gqa, name):
    t = src.shape[0]
    rows, nstep = _band_geometry(t)
    qo, ko, vo = offs
    share = 2 if gqa else 1

    def body(sink_ref, q_ref, k_ref, v_ref, b_ref, o_ref, lse_ref):
        hp, g = pl.program_id(0), pl.program_id(1)
        lo = _lane_lo()
        q2 = q_ref[...]
        kc, kp, vc, vp = _band_operands(g, k_ref, v_ref, rows)
        outs, lses = [], []
        for a in range(2):
            sink = sink_ref[2 * hp + a]
            qa = _blocks(jnp.where(lo if a == 0 else jnp.logical_not(lo), q2, jnp.zeros_like(q2)) * BAND_SCALE)
            sc, sp = _band_scores(g, qa, kc, kp, b_ref, a, nb)
            m = jnp.maximum(jnp.maximum(jnp.max(sc, axis=1, keepdims=True), jnp.max(sp, axis=1, keepdims=True)), sink)
            pc, pp = jnp.exp(sc - m), jnp.exp(sp - m)
            l = jnp.sum(pc, axis=1, keepdims=True) + jnp.sum(pp, axis=1, keepdims=True) + jnp.exp(sink - m)
            inv = 1.0 / l
            pc_b, pp_b = _blocks((pc * inv).astype(BF16)), _blocks((pp * inv).astype(BF16))
            outs.append(_rows([_dot(pc_b[i], vc[i]) + _dot(pp_b[i], vp[i]) for i in range(len(qa))]))
            lses.append(m + jnp.log(l))
        o_ref[...] = jnp.where(lo, outs[0], outs[1])
        lse_ref[...] = jnp.where(lo, lses[0], lses[1])

    slab = lambda off: pl.BlockSpec((rows, LANE), lambda hp, g: (g, off + hp))
    whole = lambda off: pl.BlockSpec((t, LANE), lambda hp, g: (0, off + hp // share))
    return pl.pallas_call(
        body, name=name, grid=(N_HEADS // 2, nstep),
        in_specs=[pl.BlockSpec(memory_space=pltpu.SMEM), slab(qo), whole(ko), whole(vo),
                  pl.BlockSpec((2, BLK, 2 * BLK), lambda hp, g: (hp, 0, 0))],
        out_specs=[slab(0), slab(0)],
        out_shape=[jax.ShapeDtypeStruct((t, N_HEADS * HEAD_DIM), F32)] * 2,
        compiler_params=_cparams(("parallel", "parallel")),
    )(sinks, src, src, src, bias)


def _band_bwd(src, offs, do, lse, dpr, bias, sinks, *, nb, gqa, name):
    t = src.shape[0]
    rows, nstep = _band_geometry(t)
    qo, ko, vo = offs
    share = 2 if gqa else 1

    def fold(a):
        acc = a[0:BLK]
        for i in range(1, rows // BLK):
            acc = acc + a[i * BLK:(i + 1) * BLK]
        return acc

    def body(sink_ref, q_ref, k_ref, v_ref, do_ref, lse_ref, dpr_ref, b_ref,
             dq_ref, dk_ref, dv_ref, ds_ref, dsink_ref, dk_acc, dv_acc):
        hp, g = pl.program_id(0), pl.program_id(1)
        lo = _lane_lo()
        hi = jnp.logical_not(lo)

        @pl.when(jnp.logical_and(g == 0, lax.rem(hp, share) == 0))
        def _():
            dk_acc[...] = jnp.zeros_like(dk_acc)
            dv_acc[...] = jnp.zeros_like(dv_acc)

        @pl.when(g == 0)
        def _():
            ds_ref[...] = jnp.zeros_like(ds_ref)
            dsink_ref[...] = jnp.zeros_like(dsink_ref)

        q2, do2, lse2, dpr2 = q_ref[...], do_ref[...], lse_ref[...], dpr_ref[...]
        lse_sw, dpr_sw = pltpu.roll(lse2, HEAD_DIM, axis=1), pltpu.roll(dpr2, HEAD_DIM, axis=1)
        kc, kp, vc, vp = _band_operands(g, k_ref, v_ref, rows)
        dqs, dk_cur, dk_prev, dv_cur, dv_prev = [], None, None, None, None
        for a in range(2):
            sink = sink_ref[2 * hp + a]
            mine = lo if a == 0 else hi
            qa = _blocks(jnp.where(mine, q2, jnp.zeros_like(q2)) * BAND_SCALE)
            doa = _blocks(jnp.where(mine, do2, jnp.zeros_like(do2)))
            lse_a, dpr_a = jnp.where(mine, lse2, lse_sw), jnp.where(mine, dpr2, dpr_sw)
            sc, sp = _band_scores(g, qa, kc, kp, b_ref, a, nb)
            pc, pp = jnp.exp(sc - lse_a), jnp.exp(sp - lse_a)
            dsc = pc * (_rows([_dot_nt(d, v) for d, v in zip(doa, vc)]) - dpr_a)
            dsp = pp * (_rows([_dot_nt(d, v) for d, v in zip(doa, vp)]) - dpr_a)
            ds_ref[a, :, BLK:2 * BLK] += fold(dsc)
            ds_ref[a, :, 0:BLK] += fold(dsp)
            dsink_ref[a] -= jnp.sum(jnp.exp(sink - lse_a) * dpr_a, axis=0, keepdims=True)
            dsc_b, dsp_b = _blocks(dsc.astype(BF16)), _blocks(dsp.astype(BF16))
            pc_b, pp_b = _blocks(pc.astype(BF16)), _blocks(pp.astype(BF16))
            dqs.append(_rows([_dot(dsc_b[i], kc[i]) + _dot(dsp_b[i], kp[i]) for i in range(len(qa))]))
            parts = [_rows([_dot_tn(x[i], y[i]) for i in range(len(qa))])
                     for x, y in ((dsc_b, qa), (dsp_b, qa), (pc_b, doa), (pp_b, doa))]
            if a == 0:
                dk_cur, dk_prev, dv_cur, dv_prev = parts
            else:
                dk_cur, dk_prev, dv_cur, dv_prev = dk_cur + parts[0], dk_prev + parts[1], dv_cur + parts[2], dv_prev + parts[3]
        dq_ref[...] = (jnp.where(lo, dqs[0], dqs[1]) * BAND_SCALE).astype(dq_ref.dtype)
        start = pl.multiple_of(g * rows, rows)
        after = pl.multiple_of(g * rows + BLK, BLK)
        dk_acc[pl.ds(after, rows), :] += dk_cur
        dk_acc[pl.ds(start, rows), :] += dk_prev
        dv_acc[pl.ds(after, rows), :] += dv_cur
        dv_acc[pl.ds(start, rows), :] += dv_prev

        @pl.when(g == nstep - 1)
        def _():
            dk_ref[...] = dk_acc[BLK:, :].astype(dk_ref.dtype)
            dv_ref[...] = dv_acc[BLK:, :].astype(dv_ref.dtype)

    slab = lambda off: pl.BlockSpec((rows, LANE), lambda hp, g: (g, off + hp))
    whole = lambda off: pl.BlockSpec((t, LANE), lambda hp, g: (0, off + hp // share))
    per_pair = lambda shp: pl.BlockSpec((2,) + shp, lambda hp, g: (hp,) + (0,) * len(shp))
    kv_cols = N_HEADS * HEAD_DIM // share
    return pl.pallas_call(
        body, name=name, grid=(N_HEADS // 2, nstep),
        in_specs=[pl.BlockSpec(memory_space=pltpu.SMEM), slab(qo), whole(ko), whole(vo), slab(0), slab(0), slab(0),
                  per_pair((BLK, 2 * BLK))],
        out_specs=[slab(0), whole(0), whole(0), per_pair((BLK, 2 * BLK)), per_pair((1, LANE))],
        out_shape=[jax.ShapeDtypeStruct((t, N_HEADS * HEAD_DIM), BF16), jax.ShapeDtypeStruct((t, kv_cols), BF16),
                   jax.ShapeDtypeStruct((t, kv_cols), BF16), jax.ShapeDtypeStruct((N_HEADS, BLK, 2 * BLK), F32),
                   jax.ShapeDtypeStruct((N_HEADS, 1, LANE), F32)],
        scratch_shapes=[pltpu.VMEM((t + BLK, LANE), F32), pltpu.VMEM((t + BLK, LANE), F32)],
        compiler_params=_cparams(("arbitrary", "arbitrary")),
    )(sinks, src, src, src, do, lse, dpr, bias)


MLA_V_OFF = N_HEADS


def _diag_mask(keys_on_rows=False):
    rows, cols = lax.broadcasted_iota(jnp.int32, (TQ, TQ), 0), lax.broadcasted_iota(jnp.int32, (TQ, TQ), 1)
    return rows <= cols if keys_on_rows else cols <= rows


def _mla_specs(t):
    blk = lambda f: pl.BlockSpec((TQ, LANE), lambda hp, qi, f=f: (qi, f(hp)))
    whole = lambda f: pl.BlockSpec((t, LANE), lambda hp, qi, f=f: (0, f(hp)))
    return blk, whole


def _mla_fwd(q, k, vt, *, name):
    t = q.shape[0]
    n = t // TQ
    scale = (C_NOPE + C_ROPE) ** -0.5

    def body(q0_ref, q1_ref, k0_ref, k1_ref, vt_ref, o_ref, lse_ref, m_ref, l_ref, acc_ref):
        qi = pl.program_id(1)
        qs, ks = (q0_ref[...], q1_ref[...]), (k0_ref, k1_ref)
        m_ref[...] = jnp.full_like(m_ref, NEG)
        l_ref[...] = jnp.zeros_like(l_ref)
        acc_ref[...] = jnp.zeros_like(acc_ref)

        def step(kj, diagonal):
            rows = pl.ds(pl.multiple_of(kj * TQ, TQ), TQ)
            vtb = vt_ref[0, kj]
            for a in range(2):
                s = _dot_nt(ks[a][rows, :], qs[a]) * (scale * LOG2E)
                if diagonal:
                    s = jnp.where(_diag_mask(keys_on_rows=True), s, NEG)
                m_prev = m_ref[a]
                m_new = jnp.maximum(m_prev, jnp.max(s, axis=0, keepdims=True))
                alpha = jnp.exp2(m_prev - m_new)
                p = jnp.exp2(s - m_new)
                l_ref[a] = alpha * l_ref[a] + jnp.sum(p, axis=0, keepdims=True)
                acc_ref[a] = alpha * acc_ref[a] + _dot(vtb, p.astype(BF16))
                m_ref[a] = m_new

        def kloop(kj, c2):
            step(kj, False)
            return c2

        lax.fori_loop(0, qi, kloop, 0)
        step(qi, True)
        first = lax.broadcasted_iota(jnp.int32, (LANE, 1), 0) < HEAD_DIM
        ot = jnp.where(first, acc_ref[0] * (1.0 / l_ref[0]), acc_ref[1] * (1.0 / l_ref[1]))
        lset = jnp.where(first, m_ref[0] * LN2 + jnp.log(l_ref[0]), m_ref[1] * LN2 + jnp.log(l_ref[1]))
        o_ref[...] = ot.T
        lse_ref[...] = lset.T

    blk, whole = _mla_specs(t)
    return pl.pallas_call(
        body, name=name, grid=(N_HEADS // 2, n),
        in_specs=[blk(lambda hp: 2 * hp), blk(lambda hp: 2 * hp + 1), whole(lambda hp: 2 * hp), whole(lambda hp: 2 * hp + 1),
                  pl.BlockSpec((1, n, LANE, TQ), lambda hp, qi: (hp, 0, 0, 0))],
        out_specs=[blk(lambda hp: hp), blk(lambda hp: hp)],
        out_shape=[jax.ShapeDtypeStruct((t, N_HEADS * HEAD_DIM), F32)] * 2,
        scratch_shapes=[pltpu.VMEM((2, 1, TQ), F32), pltpu.VMEM((2, 1, TQ), F32), pltpu.VMEM((2, LANE, TQ), F32)],
        compiler_params=_cparams(("parallel", "parallel")),
    )(q, q, k, k, vt)


def _mla_bwd(q, k, kv, do, lse, delta, *, name):
    t = q.shape[0]
    n = t // TQ
    scale = (C_NOPE + C_ROPE) ** -0.5

    def body(q0_ref, q1_ref, k0_ref, k1_ref, v_ref, do_ref, lse_ref, dl_ref,
             dq_ref, dk_ref, dv_ref, dq_acc, dk_acc, dv_acc):
        qi = pl.program_id(1)
        lo = _lane_lo()

        @pl.when(qi == 0)
        def _():
            dk_acc[...] = jnp.zeros_like(dk_acc)
            dv_acc[...] = jnp.zeros_like(dv_acc)

        dq_acc[...] = jnp.zeros_like(dq_acc)
        qs, ks = (q0_ref[...], q1_ref[...]), (k0_ref, k1_ref)
        do2, lse2, dl2 = do_ref[...], lse_ref[...], dl_ref[...]
        lse_sw, dl_sw = pltpu.roll(lse2, HEAD_DIM, axis=1), pltpu.roll(dl2, HEAD_DIM, axis=1)
        heads = []
        for a in range(2):
            mine = lo if a == 0 else jnp.logical_not(lo)
            heads.append((jnp.where(mine, do2, jnp.zeros_like(do2)), jnp.where(mine, lse2, lse_sw)[:, 0:1] * LOG2E,
                          jnp.where(mine, dl2, dl_sw)[:, 0:1]))

        def step(kj, diagonal):
            rows = pl.ds(pl.multiple_of(kj * TQ, TQ), TQ)
            vb = v_ref[rows, :]
            for a, (doa, lse_a, dl_a) in enumerate(heads):
                kb = ks[a][rows, :]
                s = _dot_nt(qs[a], kb) * (scale * LOG2E)
                if diagonal:
                    s = jnp.where(_diag_mask(), s, NEG)
                p = jnp.exp2(s - lse_a)
                ds = (p * (_dot_nt(doa, vb) - dl_a)).astype(BF16)
                dq_acc[a] += _dot(ds, kb)
                dk_acc[a, rows, :] += _dot_tn(ds, qs[a])
                dv_acc[rows, :] += _dot_tn(p.astype(BF16), doa)

        def kloop(kj, c2):
            step(kj, False)
            return c2

        lax.fori_loop(0, qi, kloop, 0)
        step(qi, True)
        dq_ref[:, 0:LANE] = (dq_acc[0] * scale).astype(dq_ref.dtype)
        dq_ref[:, LANE:2 * LANE] = (dq_acc[1] * scale).astype(dq_ref.dtype)

        @pl.when(qi == n - 1)
        def _():
            dk_ref[:, 0:LANE] = (dk_acc[0] * scale).astype(dk_ref.dtype)
            dk_ref[:, LANE:2 * LANE] = (dk_acc[1] * scale).astype(dk_ref.dtype)
            dv_ref[...] = dv_acc[...].astype(dv_ref.dtype)

    blk, whole = _mla_specs(t)
    even, odd, pair = (lambda hp: 2 * hp), (lambda hp: 2 * hp + 1), (lambda hp: hp)
    wide = jax.ShapeDtypeStruct((t, N_HEADS * LANE), BF16)
    return pl.pallas_call(
        body, name=name, grid=(N_HEADS // 2, n),
        in_specs=[blk(even), blk(odd), whole(even), whole(odd), whole(lambda hp: MLA_V_OFF + hp), blk(pair), blk(pair), blk(pair)],
        out_specs=[pl.BlockSpec((TQ, 2 * LANE), lambda hp, qi: (qi, hp)), pl.BlockSpec((t, 2 * LANE), lambda hp, qi: (0, hp)), whole(pair)],
        out_shape=[wide, wide, jax.ShapeDtypeStruct((t, N_HEADS * HEAD_DIM), BF16)],
        scratch_shapes=[pltpu.VMEM((2, TQ, LANE), F32), pltpu.VMEM((2, t, LANE), F32), pltpu.VMEM((t, LANE), F32)],
        compiler_params=_cparams(("arbitrary", "arbitrary")),
    )(q, q, k, k, kv, do, lse, delta)


def _bias_lookup(bucket, table_t, *, name):
    nh, npos = bucket.shape
    tp = 4096

    def body(b_ref, t_ref, o_ref):
        bk, tab = b_ref[...], t_ref[...]
        acc = jnp.zeros(bk.shape, F32)
        for i in range(REL_BUCKETS):
            acc = jnp.where(bk == i, tab[:, i:i + 1], acc)
        o_ref[...] = acc

    return pl.pallas_call(
        body, name=name, grid=(npos // tp,),
        in_specs=[pl.BlockSpec((nh, tp), lambda i: (0, i)), pl.BlockSpec((nh, REL_BUCKETS), lambda i: (0, 0))],
        out_specs=pl.BlockSpec((nh, tp), lambda i: (0, i)),
        out_shape=jax.ShapeDtypeStruct((nh, npos), F32),
        compiler_params=_cparams(("parallel",)),
    )(bucket, table_t)


def _bias_grad(bucket, ds0, ds1, *, name):
    nh, npos = bucket.shape
    tp = 4096

    def body(b_ref, a_ref, c_ref, o_ref):
        i = pl.program_id(0)
        bk, ds = b_ref[...], a_ref[...] + c_ref[...]
        lane = lax.broadcasted_iota(jnp.int32, (nh, REL_BUCKETS), 1)
        acc = jnp.zeros((nh, REL_BUCKETS), F32)
        for j in range(REL_BUCKETS):
            col = jnp.sum(jnp.where(bk == j, ds, 0.0), axis=1, keepdims=True)
            acc = acc + jnp.where(lane == j, col, 0.0)

        @pl.when(i == 0)
        def _():
            o_ref[...] = acc

        @pl.when(i > 0)
        def _():
            o_ref[...] += acc

    return pl.pallas_call(
        body, name=name, grid=(npos // tp,),
        in_specs=[pl.BlockSpec((nh, tp), lambda i: (0, i))] * 3,
        out_specs=pl.BlockSpec((nh, REL_BUCKETS), lambda i: (0, 0)),
        out_shape=jax.ShapeDtypeStruct((nh, REL_BUCKETS), F32),
        compiler_params=_cparams(("arbitrary",)),
    )(bucket, ds0, ds1)


def _t5_bucket(dist):
    n = jnp.maximum(dist, 0)
    max_exact = REL_BUCKETS // 2
    scaled = jnp.log(jnp.maximum(n, 1).astype(F32) / max_exact) / math.log(REL_MAX_DIST / max_exact)
    large = max_exact + (scaled * (REL_BUCKETS - max_exact)).astype(jnp.int32)
    return jnp.where(n < max_exact, n, jnp.minimum(large, REL_BUCKETS - 1))


def _bucket_index():
    qi = jnp.arange(BLK)[:, None]
    ci = jnp.arange(2 * BLK)[None, :]
    step = BLK + qi - ci
    per_group = [_t5_bucket(step * d).reshape(1, -1) for d in A_DILS + (1,)]
    return jnp.concatenate([jnp.tile(b, (N_HEADS, 1)) for b in per_group], axis=0).astype(jnp.int32)


def _sigmoid(x):
    return 1.0 / (1.0 + jnp.exp(-x))


def _ln_stats(z):
    mu = jnp.mean(z, axis=-1, keepdims=True)
    zc = z - mu
    var = jnp.mean(zc * zc, axis=-1, keepdims=True)
    return zc * lax.rsqrt(var + LN_EPS)


def _ln_fwd(x, mix, g, b, *, name):
    def fn(i, nt, xv, mv, gv, bv):
        z = ALPHA * xv + mv
        y = _ln_stats(z) * gv + bv
        return y, y, z

    c = x.shape[-1]
    y, yb, z = _rowwise(fn, [(x[None], c, 0), (mix[None], c, 0)], pars=[(g.reshape(1, 1, c), c, 0), (b.reshape(1, 1, c), c, 0)],
                        outs=[(c, c, 0, F32), (c, c, 0, BF16), (c, c, 0, F32)], tm=512, name=name)
    return y[0], yb[0], z[0]


def _ln_bwd(z, g, dys, coefs, *, name):
    n = len(dys)

    def fn(i, nt, zv, *rest):
        gv = rest[n]
        dy = coefs[0] * rest[0]
        for cf, t in zip(coefs[1:], rest[1:n]):
            dy = dy + cf * t
        mu = jnp.mean(zv, axis=-1, keepdims=True)
        zc = zv - mu
        r = lax.rsqrt(jnp.mean(zc * zc, axis=-1, keepdims=True) + LN_EPS)
        xh = zc * r
        dxh = dy * gv
        dz = r * (dxh - jnp.mean(dxh, axis=-1, keepdims=True) - xh * jnp.mean(dxh * xh, axis=-1, keepdims=True))
        return dz, dz, jnp.sum(dy * xh, axis=0, keepdims=True), jnp.sum(dy, axis=0, keepdims=True)

    c = z.shape[-1]
    dz, dzb, dg, db = _rowwise(fn, [(z[None], c, 0)] + [(d[None], c, 0) for d in dys], pars=[(g.reshape(1, 1, c), c, 0)],
                               outs=[(c, c, 0, F32), (c, c, 0, BF16)], accs=[(1, c, c, 0), (1, c, c, 0)], tm=512, name=name)
    return dz[0], dzb[0], dg.reshape(c), db.reshape(c)


def _rms_fwd(src, c, off, g, *, name):
    def fn(i, nt, xv, gv):
        return xv * lax.rsqrt(jnp.mean(xv * xv, axis=-1, keepdims=True) + RMS_EPS) * gv

    return _rowwise(fn, [(src[None], c, off)], pars=[(g.reshape(1, 1, c), c, 0)], outs=[(c, c, 0, BF16)], tm=1024, name=name)[0][0]


def _rms_bwd(src, c, off, g, dy, *, name):
    def fn(i, nt, xv, dyv, gv):
        r = lax.rsqrt(jnp.mean(xv * xv, axis=-1, keepdims=True) + RMS_EPS)
        gd = gv * dyv
        dx = gd * r - xv * (r * r * r) * jnp.mean(gd * xv, axis=-1, keepdims=True)
        return dx, jnp.sum(dyv * xv * r, axis=0, keepdims=True)

    dx, dg = _rowwise(fn, [(src[None], c, off), (dy[None], c, 0)], pars=[(g.reshape(1, 1, c), c, 0)],
                      outs=[(c, c, 0, BF16)], accs=[(1, c, c, 0)], tm=1024, name=name)
    return dx[0], dg.reshape(c)


def _rope_slabs(x, n_slab, c, s, *, add=None, to_front=False, name):
    half = C_ROPE // 2

    def fn(i, nt, xv, cv, sv, *rest):
        lane = lax.broadcasted_iota(jnp.int32, (1, LANE), 1)
        extra = pltpu.roll(rest[0], C_NOPE, axis=1) if rest else None
        outs = []
        for h in range(n_slab):
            xs = xv[:, h * LANE:(h + 1) * LANE]
            if extra is not None:
                xs = xs + extra
            swapped = jnp.where(lane < C_NOPE + half, pltpu.roll(xs, LANE - half, axis=1), pltpu.roll(xs, half, axis=1))
            y = xs * cv + swapped * sv
            if to_front:
                y = jnp.where(lane < C_ROPE, pltpu.roll(y, LANE - C_NOPE, axis=1), 0.0)
            outs.append(y)
        return jnp.concatenate(outs, axis=1) if n_slab > 1 else outs[0]

    w = n_slab * LANE
    rows = [(x[None], w, 0), (c[None], LANE, 0), (s[None], LANE, 0)]
    if add is not None:
        rows.append((add[0][None], LANE, add[1]))
    return _rowwise(fn, rows, outs=[(w, w, 0, BF16)], tm=512, name=name)[0][0]


def _merge_fwd(proj, b_gate, ys, *, name):
    def fn(i, nt, g0, g1, g2, ya, yb, yc, bg):
        return (_sigmoid(g0 + bg[:, 0:1024]) * ya + _sigmoid(g1 + bg[:, 1024:2048]) * yb
                + _sigmoid(g2 + bg[:, 2048:3072]) * yc)

    rows = [(proj[None], 1024, j) for j in range(3)] + [(y[None], 1024, 0) for y in ys]
    return _rowwise(fn, rows, pars=[(b_gate.reshape(1, 1, 3072), 3072, 0)], outs=[(1024, 1024, 0, BF16)], tm=512, name=name)[0][0]


def _merge_bwd(proj, b_gate, ys, dm, *, name):
    def fn(i, nt, g0, g1, g2, ya, yb, yc, dmv, bg):
        outs, dgs = [], []
        for j, (gp, y) in enumerate(((g0, ya), (g1, yb), (g2, yc))):
            s = _sigmoid(gp + bg[:, j * 1024:(j + 1) * 1024])
            outs.append(s * dmv)
            dgs.append(dmv * y * s * (1.0 - s))
        return outs + dgs + [jnp.sum(d, axis=0, keepdims=True) for d in dgs]

    rows = [(proj[None], 1024, j) for j in range(3)] + [(y[None], 1024, 0) for y in ys] + [(dm[None], 1024, 0)]
    res = _rowwise(fn, rows, pars=[(b_gate.reshape(1, 1, 3072), 3072, 0)], outs=[(1024, 1024, 0, BF16)] * 6,
                   accs=[(1, 1024, 1024, 0)] * 3, tm=256, name=name)
    dys = [r[0] for r in res[0:3]]
    dgp = [r[0] for r in res[3:6]]
    dbg = jnp.concatenate([r.reshape(1024) for r in res[6:9]])
    return dys, dgp, dbg


def _shift_down(u, halo, i, k):
    ext = jnp.concatenate([jnp.where(i > 0, halo, 0.0), u], axis=0)
    return pltpu.roll(ext, k, axis=0)[HALO:]


def _shift_up(u, halo, i, nt, k):
    ext = jnp.concatenate([u, jnp.where(i < nt - 1, halo, 0.0)], axis=0)
    n = ext.shape[0]
    return pltpu.roll(ext, n - k, axis=0)[:n - HALO]


GLU_C = D_FF // 2


def _conv(u, halo, i, w, b):
    return w[0:1] * _shift_down(u, halo, i, 2) + w[1:2] * _shift_down(u, halo, i, 1) + w[2:3] * u + b


def _glu_fwd(ug, uv, conv_w, conv_b, *, name):
    def fn(i, nt, g, v, hg, hv, wg, wv, bg, bv):
        cg, cv = _conv(g, hg, i, wg, bg), _conv(v, hv, i, wv, bv)
        return cg * _sigmoid(cg) * cv

    w3, b3 = conv_w[None], conv_b.reshape(1, 1, -1)
    c = GLU_C
    return _rowwise(fn, [(ug[None], c, 0), (uv[None], c, 0)], halos=[(ug[None], c, 0, "prev"), (uv[None], c, 0, "prev")],
                    pars=[(w3, c, 0), (w3, c, 2), (b3, c, 0), (b3, c, 2)], outs=[(D_FF, c, 0, BF16)], tm=256, ncol=2, name=name)[0][0]


def _glu_bwd_a(ug, uv, conv_w, conv_b, dh, *, name):
    def fn(i, nt, g, v, dhv, hg, hv, wg, wv, bg, bv):
        g1, g2 = _shift_down(g, hg, i, 1), _shift_down(g, hg, i, 2)
        v1, v2 = _shift_down(v, hv, i, 1), _shift_down(v, hv, i, 2)
        cg = wg[0:1] * g2 + wg[1:2] * g1 + wg[2:3] * g + bg
        cv = wv[0:1] * v2 + wv[1:2] * v1 + wv[2:3] * v + bv
        s = _sigmoid(cg)
        dcv = dhv * cg * s
        dcg = dhv * cv * (s * (1.0 + cg * (1.0 - s)))
        red = lambda a: jnp.sum(a, axis=0, keepdims=True)
        return (dcg, dcv, red(dcg), red(dcv), red(dcg * g2), red(dcg * g1), red(dcg * g),
                red(dcv * v2), red(dcv * v1), red(dcv * v))

    w3, b3 = conv_w[None], conv_b.reshape(1, 1, -1)
    c = GLU_C
    res = _rowwise(fn, [(ug[None], c, 0), (uv[None], c, 0), (dh[None], c, 0)],
                   halos=[(ug[None], c, 0, "prev"), (uv[None], c, 0, "prev")],
                   pars=[(w3, c, 0), (w3, c, 2), (b3, c, 0), (b3, c, 2)],
                   outs=[(D_FF, c, 0, BF16), (D_FF, c, 0, BF16)], accs=[(1, D_FF, c, 0)] * 8, tm=256, ncol=2, name=name)
    dcg, dcv = res[0][0], res[1][0]
    dconv_b = jnp.concatenate([res[2].reshape(D_FF), res[3].reshape(D_FF)])
    dconv_w = jnp.concatenate([jnp.concatenate([res[4 + j].reshape(1, D_FF) for j in range(3)], axis=0),
                               jnp.concatenate([res[7 + j].reshape(1, D_FF) for j in range(3)], axis=0)], axis=1)
    return dcg, dcv, dconv_w, dconv_b


def _glu_bwd_b(dc, conv_w, half, *, name):
    def fn(i, nt, d, hd, w):
        return w[2:3] * d + w[1:2] * _shift_up(d, hd, i, nt, 1) + w[0:1] * _shift_up(d, hd, i, nt, 2)

    c = GLU_C
    return _rowwise(fn, [(dc[None], c, 0)], halos=[(dc[None], c, 0, "next")], pars=[(conv_w[None], c, 2 * half)],
                    outs=[(D_FF, c, 0, BF16)], tm=256, ncol=2, name=name)[0][0]


def _loss_and_grad(y, tgt, *, name):
    def fn(i, nt, yv, tv):
        err = yv - tv
        part = jnp.sum(jnp.sum(err * err, axis=0, keepdims=True), axis=1, keepdims=True) * (0.5 / D_MODEL)
        return err * (1.0 / D_MODEL), jnp.zeros((1, LANE), F32) + part

    dy, part = _rowwise(fn, [(y[None], D_MODEL, 0), (tgt[None], D_MODEL, 0)], outs=[(D_MODEL, D_MODEL, 0, F32)],
                        accs=[(1, LANE, LANE, 0)], tm=512, name=name)
    return dy[0], part.reshape(LANE)


def _lincomb(terms, coefs, *, name):
    def fn(i, nt, *vs):
        acc = coefs[0] * vs[0]
        for cf, v in zip(coefs[1:], vs[1:]):
            acc = acc + cf * v
        return acc

    c = terms[0].shape[-1]
    return _rowwise(fn, [(a[None], c, 0) for a in terms], outs=[(c, c, 0, F32)], tm=512, name=name)[0][0]


def _sum_rows(terms, *, tm, name, dtype=F32):
    def fn(i, nt, *vs):
        acc = vs[0]
        for v in vs[1:]:
            acc = acc + v
        return acc

    c = terms[0].shape[-1]
    return _rowwise(fn, [(t, c, 0) for t in terms], outs=[(c, c, 0, dtype)], tm=tm, name=name)[0]


def _head_sums(x):
    lo = _lane_lo()
    parts = []
    for j in range(x.shape[1] // LANE):
        blk = x[:, j * LANE:(j + 1) * LANE]
        s_lo = jnp.sum(jnp.where(lo, blk, 0.0), axis=1, keepdims=True)
        s_hi = jnp.sum(jnp.where(lo, 0.0, blk), axis=1, keepdims=True)
        parts.append(jnp.where(lo, s_lo, s_hi))
    return jnp.concatenate(parts, axis=1)


def _group_weights(l0, l1, l2):
    m = jnp.maximum(jnp.maximum(l0, l1), l2)
    es = [jnp.exp(l - m) for l in (l0, l1, l2)]
    inv = 1.0 / (es[0] + es[1] + es[2])
    return [e * inv for e in es]


def _combine_fwd(os_, lses, *, name):
    def fn(i, nt, o0, o1, o2, l0, l1, l2):
        w = _group_weights(l0, l1, l2)
        return w[0] * o0 + w[1] * o1 + w[2] * o2

    c = os_[0].shape[-1]
    return _rowwise(fn, [(a[None], c, 0) for a in list(os_) + list(lses)], outs=[(c, c, 0, BF16)], tm=512, name=name)[0][0]


def _combine_bwd(os_, lses, do_a, *, name):
    def fn(i, nt, o0, o1, o2, l0, l1, l2, da):
        ws = _group_weights(l0, l1, l2)
        dws = [_head_sums(da * o) for o in (o0, o1, o2)]
        mean = ws[0] * dws[0] + ws[1] * dws[1] + ws[2] * dws[2]
        return [w * da for w in ws] + [w * mean for w in ws]

    c = do_a.shape[-1]
    res = _rowwise(fn, [(a[None], c, 0) for a in list(os_) + list(lses) + [do_a]], outs=[(c, c, 0, BF16)] * 3 + [(c, c, 0, F32)] * 3,
                   tm=256, name=name)
    return [r[0] for r in res[0:3]], [r[0] for r in res[3:6]]


def _delta(do, o, *, name):
    def fn(i, nt, d, ov):
        return d, _head_sums(d * ov)

    c = do.shape[-1]
    res = _rowwise(fn, [(do[None], c, 0), (o[None], c, 0)], outs=[(c, c, 0, BF16), (c, c, 0, F32)], tm=512, name=name)
    return res[0][0], res[1][0]


def _adamw(w, g, m, v, *, name):
    c1 = 1.0 - ADAM_B1 ** ADAM_STEP
    c2 = 1.0 - ADAM_B2 ** ADAM_STEP

    def fn(i, nt, wv, gv, mv, vv):
        mn = ADAM_B1 * mv + (1.0 - ADAM_B1) * gv
        vn = ADAM_B2 * vv + (1.0 - ADAM_B2) * (gv * gv)
        delta = -ADAM_LR * ((mn / c1) / (jnp.sqrt(vn / c2) + ADAM_EPS) + ADAM_WD * wv)
        return delta, mn, vn

    r, c = w.shape
    rp = _ceil_to(r, 8)
    pad = lambda a: jnp.pad(a, ((0, rp - r), (0, 0))) if rp != r else a
    tm = rp
    for cand in (128, 64, 32, 16, 8):
        if rp % cand == 0:
            tm = cand
            break
    res = _rowwise(fn, [(pad(a)[None], c, 0) for a in (w, g, m, v)], outs=[(c, c, 0, F32)] * 3, tm=tm, name=name)
    return [x[0][:r] for x in res]


ANY = pl.BlockSpec(memory_space=pl.ANY)


def _place():
    x, y, c = lax.axis_index("x"), lax.axis_index("y"), lax.axis_index("c")
    chips = [(1 - x, y), (x, 1 - y), (1 - x, 1 - y)]
    return x, y, c, chips


def _allgather_weights(arrs):
    n = len(arrs)

    def body(*refs):
        ins, outs, send_sems, recv_sems = refs[:n], refs[n:2 * n], refs[2 * n], refs[2 * n + 1]
        x, y, c, chips = _place()
        j = 2 * x + y

        def cp(i, k, src, chip_idx, half, to):
            return pltpu.make_async_remote_copy(src_ref=src, dst_ref=outs[i].at[chip_idx, half], send_sem=send_sems.at[k],
                                                recv_sem=recv_sems.at[k], device_id=to, device_id_type=MESH)

        first, passed = [], []
        for i in range(n):
            for r, (cx, cy) in enumerate(chips):
                first.append(cp(i, 3 * i + r, ins[i].at[c], j, c, (cx, cy, c)))
                passed.append(cp(i, 3 * (n + i) + r, outs[i].at[2 * cx + cy, c], 2 * cx + cy, c, (x, y, 1 - c)))
        for d in first:
            d.start()
        for i in range(n):
            for r, (cx, cy) in enumerate(chips):
                cp(i, 3 * i + r, ins[i].at[c], 2 * cx + cy, c, (x, y, c)).wait_recv()
                passed[3 * i + r].start()
        for i in range(n):
            for r, (cx, cy) in enumerate(chips):
                cp(i, 3 * (n + i) + r, ins[i].at[c], 2 * cx + cy, 1 - c, (x, y, c)).wait_recv()
        for d in first + passed:
            d.wait_send()

    return pl.pallas_call(
        body, name="allgather_weights", in_specs=[ANY] * n, out_specs=[ANY] * n,
        out_shape=[jax.ShapeDtypeStruct((N_CHIP,) + a.shape, a.dtype) for a in arrs],
        scratch_shapes=[pltpu.SemaphoreType.DMA((6 * n,)), pltpu.SemaphoreType.DMA((6 * n,))],
    )(*arrs)


def _sibling_swap(gs):
    n = len(gs)

    def body(*refs):
        ins, outs, send_sems, recv_sems = refs[:n], refs[n:2 * n], refs[2 * n], refs[2 * n + 1]
        x, y, c, _ = _place()
        cps = [pltpu.make_async_remote_copy(src_ref=ins[i].at[1 - c], dst_ref=outs[i], send_sem=send_sems.at[i],
                                            recv_sem=recv_sems.at[i], device_id=(x, y, 1 - c), device_id_type=MESH)
               for i in range(n)]
        for d in cps:
            d.start()
        for d in cps:
            d.wait_recv()
        for d in cps:
            d.wait_send()

    return pl.pallas_call(
        body, name="grad_sibling_swap", in_specs=[ANY] * n, out_specs=[ANY] * n,
        out_shape=[jax.ShapeDtypeStruct(g.shape[1:], g.dtype) for g in gs],
        scratch_shapes=[pltpu.SemaphoreType.DMA((n,)), pltpu.SemaphoreType.DMA((n,))],
    )(*gs)


def _chip_scatter(ps):
    n = len(ps)

    def body(*refs):
        ins, outs, send_sems, recv_sems = refs[:n], refs[n:2 * n], refs[2 * n], refs[2 * n + 1]
        x, y, c, chips = _place()
        sends = []
        for i in range(n):
            for r, (cx, cy) in enumerate(chips):
                sends.append(pltpu.make_async_remote_copy(src_ref=ins[i].at[2 * cx + cy], dst_ref=outs[i].at[r], send_sem=send_sems.at[3 * i + r],
                                                          recv_sem=recv_sems.at[3 * i + r], device_id=(cx, cy, c), device_id_type=MESH))
        for d in sends:
            d.start()
        for d in sends:
            d.wait_recv()
        for d in sends:
            d.wait_send()

    return pl.pallas_call(
        body, name="grad_chip_scatter", in_specs=[ANY] * n, out_specs=[ANY] * n,
        out_shape=[jax.ShapeDtypeStruct((3,) + p.shape[1:], p.dtype) for p in ps],
        scratch_shapes=[pltpu.SemaphoreType.DMA((3 * n,)), pltpu.SemaphoreType.DMA((3 * n,))],
    )(*ps)


def _sibling_share(rs):
    n = len(rs)

    def body(*refs):
        ins, outs, send_sems, recv_sems = refs[:n], refs[n:2 * n], refs[2 * n], refs[2 * n + 1]
        x, y, c, _ = _place()
        cps = [pltpu.make_async_remote_copy(src_ref=ins[i], dst_ref=outs[i], send_sem=send_sems.at[i], recv_sem=recv_sems.at[i],
                                            device_id=(x, y, 1 - c), device_id_type=MESH) for i in range(n)]
        for d in cps:
            d.start()
        for d in cps:
            d.wait_recv()
        for d in cps:
            d.wait_send()

    return pl.pallas_call(
        body, name="grad_sibling_share", in_specs=[ANY] * n, out_specs=[ANY] * n,
        out_shape=[jax.ShapeDtypeStruct(r.shape, r.dtype) for r in rs],
        scratch_shapes=[pltpu.SemaphoreType.DMA((n,)), pltpu.SemaphoreType.DMA((n,))],
    )(*rs)


def _allreduce_small(s):
    rows, w = s.shape
    n_dev = 8

    def body(s_ref, out_ref, slots, send_sems, recv_sems):
        x, y, c, _ = _place()
        me = 4 * x + 2 * y + c
        slots[me] = s_ref[...]
        peers = []
        for r in range(1, n_dev):
            px = 1 - x if r & 4 else x
            py = 1 - y if r & 2 else y
            pc = 1 - c if r & 1 else c
            peers.append((px, py, pc))
        sends = [pltpu.make_async_remote_copy(src_ref=s_ref, dst_ref=slots.at[me], send_sem=send_sems.at[r], recv_sem=recv_sems.at[r],
                                              device_id=peer, device_id_type=MESH) for r, peer in enumerate(peers)]
        for d in sends:
            d.start()
        for r, (px, py, pc) in enumerate(peers):
            pltpu.make_async_remote_copy(src_ref=s_ref, dst_ref=slots.at[4 * px + 2 * py + pc], send_sem=send_sems.at[r],
                                         recv_sem=recv_sems.at[r], device_id=(x, y, c), device_id_type=MESH).wait_recv()
        for d in sends:
            d.wait_send()
        acc = slots[0]
        for k in range(1, n_dev):
            acc = acc + slots[k]
        out_ref[...] = acc

    vm = pl.BlockSpec(memory_space=pltpu.VMEM)
    return pl.pallas_call(
        body, name="allreduce_small", in_specs=[vm], out_specs=vm, out_shape=jax.ShapeDtypeStruct((rows, w), F32),
        scratch_shapes=[pltpu.VMEM((n_dev, rows, w), F32), pltpu.SemaphoreType.DMA((n_dev - 1,)), pltpu.SemaphoreType.DMA((n_dev - 1,))],
    )(s)


W_IN_SHARD = D_IN // N_CHIP
W_IN_ROWS_G = 2304
REDUCED = tuple(m for m in MATS if m[0] != "conv_w")
CONV_W_SIZE = 3 * 2 * D_FF


def _weight_send(name, a):
    if name == "w_in":
        return jnp.swapaxes(a, 1, 2).astype(BF16)
    return a if name == "conv_w" else a.astype(BF16)


def _full_weights(gathered, l):
    g = {k: v[:, l] for k, v in gathered.items()}
    s = g["w_in"].astype(F32).reshape(D_IN, D_MODEL)
    dup = lambda a: jnp.concatenate([a[0:64], a[0:64], a[64:128], a[64:128]], axis=0)
    o = ORIG
    wm_t = jnp.concatenate([s[o["gate"]:], s[o["a"]:o["a"] + A_COLS], s[o["bq"]:o["bk"]], dup(s[o["bk"]:o["bv"]]), dup(s[o["bv"]:o["cq"]]),
                            s[o["cq"]:o["gate"]], jnp.zeros((M_COLS - M_CDKV - (o["gate"] - o["cdkv"]), D_MODEL), F32)], axis=0).astype(BF16)
    wg_t = [s[o["a"] + gi * A_COLS:o["a"] + (gi + 1) * A_COLS].astype(BF16) for gi in (1, 2)]
    full = {name: jnp.moveaxis(g[name], 0, ax).reshape(shape) for name, shape, ax in MATS if name != "w_in"}
    uq = full["w_uq"].reshape(C_Q_RANK, N_HEADS, C_NOPE + C_ROPE)
    ukv = full["w_ukv"].reshape(C_KV_RANK, N_HEADS, 2 * C_NOPE)
    w_uq_p = _pad_lanes(uq).reshape(C_Q_RANK, N_HEADS * LANE)
    w_ukv_p = jnp.concatenate([_pad_lanes(ukv[:, :, :C_NOPE]).reshape(C_KV_RANK, N_HEADS * LANE),
                               ukv[:, :, C_NOPE:].reshape(C_KV_RANK, N_HEADS * HEAD_DIM)], axis=1)
    return {"wm_t": wm_t, "wg_t": wg_t, "w_uq_p": w_uq_p, "w_ukv_p": w_ukv_p, "w_branch": full["w_branch"], "w_out": full["w_out"],
            "wup_g": full["w_ffn_up"][:, :D_FF], "wup_v": full["w_ffn_up"][:, D_FF:], "conv_w": full["conv_w"],
            "w_ffn_down": full["w_ffn_down"]}


def _grad_send(name, g, shape, ax):
    if name == "w_in":
        return jnp.pad(g.reshape(N_CHIP, W_IN_SHARD, D_MODEL), ((0, 0), (0, W_IN_ROWS_G - W_IN_SHARD), (0, 0)))
    split = shape[:ax] + (N_CHIP, shape[ax] // N_CHIP) + shape[ax + 1:]
    return jnp.moveaxis(g.reshape(split), ax, 0)


def _grad_recv(name, r):
    return r[:W_IN_SHARD].T if name == "w_in" else r


def _pack_small(rel, small, conv_w, extra):
    parts = [rel.reshape(-1)]
    for l in range(DEPTH):
        for name in SMALL:
            parts.append(small[name][l].reshape(-1))
    parts += [conv_w.reshape(-1), extra]
    flat = jnp.concatenate(parts)
    rows = _ceil_to(-(-flat.shape[0] // LANE), 8)
    return jnp.pad(flat, (0, rows * LANE - flat.shape[0])).reshape(rows, LANE)


def _unpack_small(buf):
    flat = buf.reshape(-1)
    rel = flat[:REL_BUCKETS * 32].reshape(REL_BUCKETS, 32)
    off = REL_BUCKETS * 32
    small = {name: [] for name in SMALL}
    for l in range(DEPTH):
        for name in SMALL:
            n = SMALL_SIZES[name]
            small[name].append(flat[off:off + n])
            off += n
    conv_w = flat[off:off + DEPTH * CONV_W_SIZE].reshape(DEPTH, 3, 2 * D_FF)
    off += DEPTH * CONV_W_SIZE
    return rel, {k: jnp.stack(v) for k, v in small.items()}, conv_w, flat[off:off + LANE]


def _rows2d(a, lead):
    return a.reshape(a.shape[:lead] + (-1, a.shape[-1]))


def _row_tile(rows):
    for cand in (512, 256, 128, 64, 32, 16, 8):
        if rows % cand == 0:
            return cand
    raise ValueError(rows)


def _pair_add(g, got, core, *, name):
    g2, got2 = _rows2d(g, 1), _rows2d(got, 0)
    rows, c = got2.shape
    tm = _row_tile(rows)
    flag = jnp.zeros((1, 1, LANE), F32) + core.astype(F32)

    def fn(i, nt, a0, a1, b, f):
        return jnp.where(f[:, 0:1] == 0.0, a0, a1) + b

    stacked = g2.reshape(1, 2 * rows, c)
    out = _rowwise(fn, [(stacked, c, 0, 0), (stacked, c, 0, rows // tm), (got2[None], c, 0)], pars=[(flag, LANE, 0)],
                   outs=[(c, c, 0, BF16)], tm=tm, t=rows, name=name)[0][0]
    return out.reshape(got.shape)


def _chip_add(own, got, *, name):
    own2, got2 = _rows2d(own, 0), _rows2d(got, 1)
    rows, c = own2.shape
    tm = _row_tile(rows)

    def fn(i, nt, a, b0, b1, b2):
        return ((a.astype(F32) + b0.astype(F32)) + b1.astype(F32)) + b2.astype(F32)

    stacked = got2.reshape(1, 3 * rows, c)
    out = _rowwise(fn, [(own2[None], c, 0)] + [(stacked, c, 0, k * (rows // tm)) for k in range(3)],
                   outs=[(c, c, 0, F32)], tm=tm, t=rows, name=name)[0][0]
    return out.reshape(own.shape)


def _perm(a, d):
    if d == 1:
        return a
    t = a.shape[0]
    return jnp.swapaxes(a.reshape((t // d, d) + a.shape[1:]), 0, 1).reshape(a.shape)


def _unperm(a, d):
    if d == 1:
        return a
    t = a.shape[0]
    return jnp.swapaxes(a.reshape((d, t // d) + a.shape[1:]), 0, 1).reshape(a.shape)


def _pad_lanes(a, w=HP):
    return jnp.pad(a, [(0, 0)] * (a.ndim - 1) + [(0, w - a.shape[-1])])


def _rope_tables(t):
    pos = jnp.arange(t, dtype=F32)
    inv_freq = ROPE_BASE ** (-jnp.arange(0, C_ROPE, 2, dtype=F32) / C_ROPE)
    ang = pos[:, None] * inv_freq[None, :]
    cos, sin = jnp.cos(ang), jnp.sin(ang)
    ones, zeros = jnp.ones((t, C_NOPE), F32), jnp.zeros((t, C_NOPE), F32)
    tail = LANE - C_NOPE - C_ROPE
    c = jnp.concatenate([ones, cos, cos, ones[:, :tail]], axis=1)
    s = jnp.concatenate([zeros, -sin, sin, zeros[:, :tail]], axis=1)
    return c, s


def _band_calls(t, proj, projs_g, sinks):
    none = jnp.full((N_HEADS,), NEG, F32)
    a0 = M_A0 // LANE
    calls = [(proj, (a0, a0 + 4, a0 + 8), t // BLK, BLK, False, none)]
    calls += [(pg, (0, 4, 8), t // (d * BLK), BLK, False, none) for pg, d in zip(projs_g, A_DILS[1:])]
    calls.append((proj, (M_BQ // LANE, M_BK // LANE, M_BV // LANE), t // BLK, BLK - 1, True, sinks.astype(F32)))
    return calls


def _layer_fwd(l, x, xb, w, p, biases, rope_cs):
    t = x.shape[0]
    n = f"l{l}_"
    xps = [_perm(xb, d) for d in A_DILS[1:]]
    proj = _mm(xb, w["wm_t"], tb=True, out_dtype=BF16, tm=TM_TOKENS, name=n + "proj")
    projs_g = [_mm(xp, wg, tb=True, out_dtype=BF16, tm=TM_TOKENS, tn=768, name=n + f"proj_g{i + 1}")
               for i, (xp, wg) in enumerate(zip(xps, w["wg_t"]))]
    s = {"xb": xb, "xps": xps, "proj": proj, "projs_g": projs_g}

    calls = _band_calls(t, proj, projs_g, p["sinks"])
    outs = [_band_fwd(src, offs, biases[i], sk, nb=nb, gqa=gqa, name=n + f"band{i}")
            for i, (src, offs, nb, lim, gqa, sk) in enumerate(calls)]
    os_ = [_unperm(outs[gi][0], d) for gi, d in enumerate(A_DILS)]
    lses = [_unperm(outs[gi][1], d) for gi, d in enumerate(A_DILS)]
    o_a = _combine_fwd(os_, lses, name=n + "combine_fwd")
    o_b_f, lse_b = outs[3]
    o_b = o_b_f.astype(BF16)
    s.update(os=os_, lses=lses, lses_p=[outs[gi][1] for gi in range(3)], o_b=o_b_f, lse_b=lse_b)

    rq = _rms_fwd(proj, C_Q_RANK, M_CQ // C_Q_RANK, p["q_norm_g"], name=n + "rms_q")
    rkv = _rms_fwd(proj, C_KV_RANK, M_CDKV // C_KV_RANK, p["kv_norm_g"], name=n + "rms_kv")
    q_cp = _mm(rq, w["w_uq_p"], out_dtype=BF16, name=n + "uq")
    kv_cp = _mm(rkv, w["w_ukv_p"], out_dtype=BF16, name=n + "ukv")
    q_full = _rope_slabs(q_cp, N_HEADS, rope_cs[0], rope_cs[1], name=n + "rope_q")
    k_full = _rope_slabs(kv_cp, N_HEADS, rope_cs[0], rope_cs[1], add=(proj, (M_CDKV + C_KV_RANK) // LANE), name=n + "rope_k")
    vt = jnp.transpose(kv_cp[:, N_HEADS * LANE:].T.reshape(N_HEADS // 2, LANE, t // TQ, TQ), (0, 2, 1, 3))
    o_c_f, lse_c = _mla_fwd(q_full, k_full, vt, name=n + "mla_fwd")
    o_c = o_c_f.astype(BF16)
    s.update(rq=rq, rkv=rkv, q_full=q_full, k_full=k_full, kv_cp=kv_cp, lse_c=lse_c, o_c=o_c_f)

    obs = [o_a, o_b, o_c]
    ys = [_mm(o, w["w_branch"][i], out_dtype=BF16, name=n + f"branch{i}") for i, o in enumerate(obs)]
    merged = _merge_fwd(proj, p["b_gate"], ys, name=n + "merge")
    mix = _mm(merged, w["w_out"], name=n + "out")
    x1f, x1b, z1 = _ln_fwd(x, mix, p["ln1_g"], p["ln1_b"], name=n + "ln1")
    s.update(obs=obs, ys=ys, merged=merged, z1=z1, x1b=x1b)

    ug = _mm(x1b, w["wup_g"], tm=TM_TOKENS, tn=1408, out_dtype=BF16, name=n + "up_g")
    uv = _mm(x1b, w["wup_v"], tm=TM_TOKENS, tn=1408, out_dtype=BF16, name=n + "up_v")
    h = _glu_fwd(ug, uv, w["conv_w"], p["conv_b"], name=n + "glu")
    ff = _mm(h, w["w_ffn_down"], tm=TM_TOKENS, tk=1408, name=n + "down")
    x2f, x2b, z2 = _ln_fwd(x1f, ff, p["ln2_g"], p["ln2_b"], name=n + "ln2")
    s.update(ug=ug, uv=uv, h=h, z2=z2)
    return x2f, x2b, s


def _layer_bwd(l, s, dys, coefs, w, p, biases, rope_cs):
    n = f"l{l}b_"
    t = s["z2"].shape[0]
    gw, gs = {}, {}

    dz2, dz2b, gs["ln2_g"], gs["ln2_b"] = _ln_bwd(s["z2"], p["ln2_g"], dys, coefs, name=n + "ln2")
    dh = _mm(dz2b, w["w_ffn_down"], tb=True, tm=TM_TOKENS, tn=1408, out_dtype=BF16, name=n + "d_h")
    gw["w_ffn_down"] = _mm(s["h"], dz2b, ta=True, tm=1408, tk=1024, name=n + "g_down")
    dcg, dcv, gw["conv_w"], gs["conv_b"] = _glu_bwd_a(s["ug"], s["uv"], w["conv_w"], p["conv_b"], dh, name=n + "glu_a")
    dug = _glu_bwd_b(dcg, w["conv_w"], 0, name=n + "glu_bg")
    duv = _glu_bwd_b(dcv, w["conv_w"], 1, name=n + "glu_bv")
    dx1_g = _mm(dug, w["wup_g"], tb=True, tm=TM_TOKENS, tk=1408, name=n + "d_x1g")
    dx1_v = _mm(duv, w["wup_v"], tb=True, tm=TM_TOKENS, tk=1408, name=n + "d_x1v")
    gw["w_ffn_up"] = jnp.concatenate([_mm(s["x1b"], dug, ta=True, tn=1408, tk=1024, name=n + "g_upg"),
                                      _mm(s["x1b"], duv, ta=True, tn=1408, tk=1024, name=n + "g_upv")], axis=1)

    dz1, dz1b, gs["ln1_g"], gs["ln1_b"] = _ln_bwd(s["z1"], p["ln1_g"], [dz2, dx1_g, dx1_v], [ALPHA, 1.0, 1.0], name=n + "ln1")
    dmerged = _mm(dz1b, w["w_out"], tb=True, out_dtype=BF16, name=n + "d_merged")
    gw["w_out"] = _mm(s["merged"], dz1b, ta=True, name=n + "g_out")
    dys_b, dgp, gs["b_gate"] = _merge_bwd(s["proj"], p["b_gate"], s["ys"], dmerged, name=n + "merge")
    dos = [_mm(dy, w["w_branch"][i], tb=True, out_dtype=BF16, name=n + f"d_o{i}") for i, dy in enumerate(dys_b)]
    gw["w_branch"] = jnp.stack([_mm(o, dy, ta=True, name=n + f"g_branch{i}") for i, (o, dy) in enumerate(zip(s["obs"], dys_b))])

    do_gs, dpr_gs = _combine_bwd(s["os"], s["lses"], dos[0], name=n + "combine")
    do_b, dpr_b = _delta(dos[1], s["o_b"], name=n + "delta_b")
    do_list = [_perm(a, d) for a, d in zip(do_gs, A_DILS)] + [do_b]
    dpr_list = [_perm(a, d) for a, d in zip(dpr_gs, A_DILS)] + [dpr_b]
    lse_list = s["lses_p"] + [s["lse_b"]]
    calls = _band_calls(t, s["proj"], s["projs_g"], p["sinks"])
    band = [_band_bwd(src, offs, do_list[i], lse_list[i], dpr_list[i], biases[i], sk, nb=nb, gqa=gqa, name=n + f"band{i}")
            for i, (src, offs, nb, lim, gqa, sk) in enumerate(calls)]
    gs["sinks"] = band[3][4][:, 0, 0]
    ds_sum = jnp.concatenate([b_[3] for b_ in band], axis=0)

    do_c, delta_c = _delta(dos[2], s["o_c"], name=n + "delta_c")
    dq, dk, dv = _mla_bwd(s["q_full"], s["k_full"], s["kv_cp"], do_c, s["lse_c"], delta_c, name=n + "mla")
    dq_cp = _rope_slabs(dq, N_HEADS, rope_cs[0], -rope_cs[1], name=n + "rope_q")
    dk_sum = _rowwise(lambda i, nt, *vs: sum(vs[1:], vs[0]), [(dk[None], LANE, hh) for hh in range(N_HEADS)],
                      outs=[(LANE, LANE, 0, F32)], tm=1024, name=n + "krope_sum")[0][0]
    dkr = _rope_slabs(dk_sum, 1, rope_cs[0], -rope_cs[1], to_front=True, name=n + "rope_k")
    dkv_cp = jnp.concatenate([dk, dv], axis=1)
    d_rq = _mm(dq_cp, w["w_uq_p"], tb=True, name=n + "d_rq")
    d_rkv = _mm(dkv_cp, w["w_ukv_p"], tb=True, name=n + "d_rkv")
    g_uq = _mm(s["rq"], dq_cp, ta=True, name=n + "g_uq")
    g_ukv = _mm(s["rkv"], dkv_cp, ta=True, name=n + "g_ukv")
    gw["w_uq"] = g_uq.reshape(C_Q_RANK, N_HEADS, LANE)[:, :, :C_NOPE + C_ROPE].reshape(C_Q_RANK, -1)
    kw = N_HEADS * LANE
    gw["w_ukv"] = jnp.concatenate([g_ukv[:, :kw].reshape(C_KV_RANK, N_HEADS, LANE)[:, :, :C_NOPE],
                                   g_ukv[:, kw:].reshape(C_KV_RANK, N_HEADS, HEAD_DIM)], axis=2).reshape(C_KV_RANK, -1)
    dcq, gs["q_norm_g"] = _rms_bwd(s["proj"], C_Q_RANK, M_CQ // C_Q_RANK, p["q_norm_g"], d_rq, name=n + "rms_q")
    dckv, gs["kv_norm_g"] = _rms_bwd(s["proj"], C_KV_RANK, M_CDKV // C_KV_RANK, p["kv_norm_g"], d_rkv, name=n + "rms_kv")
    dcdkv = jnp.concatenate([dckv, dkr], axis=1)

    dproj = jnp.concatenate(dgp + list(band[0][:3]) + list(band[3][:3]) + [dcq, dcdkv], axis=1)
    dprojs_g = [jnp.concatenate(band[gi][:3], axis=1) for gi in (1, 2)]
    dx_terms = [_mm(dproj, w["wm_t"], tm=TM_TOKENS, tk=1024, name=n + "d_x")]
    dx_terms += [_unperm(_mm(dp, wg, tm=TM_TOKENS, tk=768, name=n + f"d_x_g{i + 1}"), d)
                 for i, (dp, wg, d) in enumerate(zip(dprojs_g, w["wg_t"], A_DILS[1:]))]
    g_main = _mm(dproj, s["xb"], ta=True, name=n + "g_in")
    g_groups = [_mm(dp, xp, ta=True, tm=768, name=n + f"g_in_g{i + 1}") for i, (xp, dp) in enumerate(zip(s["xps"], dprojs_g))]
    fold = lambda a, tag: _sum_rows([a.reshape(2, 2, HEAD_DIM, D_MODEL)[:, j] for j in range(2)], tm=HEAD_DIM,
                                    name=n + "g_fold_" + tag).reshape(2 * HEAD_DIM, D_MODEL)
    gw["w_in"] = jnp.concatenate([g_main[M_A0:M_BQ], g_groups[0], g_groups[1], g_main[M_BQ:M_BK], fold(g_main[M_BK:M_BV], "k"),
                                  fold(g_main[M_BV:M_CQ], "v"), g_main[M_CQ:M_CDKV + C_KV_RANK + C_ROPE], g_main[M_GATE:M_A0]], axis=0)
    return [dz1] + dx_terms, [ALPHA, 1.0, 1.0, 1.0], gw, gs, ds_sum


def _local_step(x, target, ws, rel_table, small):
    t = x.shape[0]
    ps = [{k: small[k][l] for k in SMALL} for l in range(DEPTH)]
    bucket = _bucket_index()
    bias_all = _bias_lookup(bucket, rel_table.T, name="bias_lookup").reshape(4, N_HEADS, BLK, 2 * BLK)
    step = BLK + jnp.arange(BLK)[:, None] - jnp.arange(2 * BLK)[None, :]
    biases = [jnp.where((step >= 0) & (step <= lim), bias_all[i], NEG) for i, lim in enumerate((BLK, BLK, BLK, BLK - 1))]
    rope_cs = _rope_tables(t)

    saved, h, hb = [], x, x.astype(BF16)
    for l in range(DEPTH):
        h, hb, s = _layer_fwd(l, h, hb, ws[l], ps[l], biases, rope_cs)
        saved.append(s)
    dy, loss_part = _loss_and_grad(h, target, name="loss")

    dys, coefs = [dy], [1.0]
    gws, gss, dss = [None] * DEPTH, [None] * DEPTH, [None] * DEPTH
    for l in reversed(range(DEPTH)):
        dys, coefs, gws[l], gss[l], dss[l] = _layer_bwd(l, saved[l], dys, coefs, ws[l], ps[l], biases, rope_cs)
    grad_x = _lincomb(dys, coefs, name="grad_x")
    npos = 2 * BLK * BLK
    g_rel = _bias_grad(bucket, dss[0].reshape(4 * N_HEADS, npos), dss[1].reshape(4 * N_HEADS, npos), name="bias_grad").T
    gsmall = {k: jnp.stack([gss[l][k] for l in range(DEPTH)]) for k in SMALL}
    return loss_part, grad_x, gws, gsmall, g_rel


def kernel(x, rel_table, w_in, b_gate, sinks, q_norm_g, kv_norm_g, w_uq, w_ukv, w_branch, w_out, ln1_g, ln1_b, w_ffn_up, conv_w, conv_b, w_ffn_down, ln2_g, ln2_b, loss_target, m_rel_table, m_w_in, m_b_gate, m_sinks, m_q_norm_g, m_kv_norm_g, m_w_uq, m_w_ukv, m_w_branch, m_w_out, m_ln1_g, m_ln1_b, m_w_ffn_up, m_conv_w, m_conv_b, m_w_ffn_down, m_ln2_g, m_ln2_b, v_rel_table, v_w_in, v_b_gate, v_sinks, v_q_norm_g, v_kv_norm_g, v_w_uq, v_w_ukv, v_w_branch, v_w_out, v_ln1_g, v_ln1_b, v_w_ffn_up, v_conv_w, v_conv_b, v_w_ffn_down, v_ln2_g, v_ln2_b):
    wts = dict(rel_table=rel_table, w_in=w_in, b_gate=b_gate, sinks=sinks, q_norm_g=q_norm_g, kv_norm_g=kv_norm_g, w_uq=w_uq,
               w_ukv=w_ukv, w_branch=w_branch, w_out=w_out, ln1_g=ln1_g, ln1_b=ln1_b, w_ffn_up=w_ffn_up, conv_w=conv_w,
               conv_b=conv_b, w_ffn_down=w_ffn_down, ln2_g=ln2_g, ln2_b=ln2_b)
    ms = dict(rel_table=m_rel_table, w_in=m_w_in, b_gate=m_b_gate, sinks=m_sinks, q_norm_g=m_q_norm_g, kv_norm_g=m_kv_norm_g,
              w_uq=m_w_uq, w_ukv=m_w_ukv, w_branch=m_w_branch, w_out=m_w_out, ln1_g=m_ln1_g, ln1_b=m_ln1_b, w_ffn_up=m_w_ffn_up,
              conv_w=m_conv_w, conv_b=m_conv_b, w_ffn_down=m_w_ffn_down, ln2_g=m_ln2_g, ln2_b=m_ln2_b)
    vs = dict(rel_table=v_rel_table, w_in=v_w_in, b_gate=v_b_gate, sinks=v_sinks, q_norm_g=v_q_norm_g, kv_norm_g=v_kv_norm_g,
              w_uq=v_w_uq, w_ukv=v_w_ukv, w_branch=v_w_branch, w_out=v_w_out, ln1_g=v_ln1_g, ln1_b=v_ln1_b, w_ffn_up=v_w_ffn_up,
              conv_w=v_conv_w, conv_b=v_conv_b, w_ffn_down=v_w_ffn_down, ln2_g=v_ln2_g, ln2_b=v_ln2_b)

    core = lax.axis_index("c")
    chip = 2 * lax.axis_index("x") + lax.axis_index("y")

    names = [name for name, _, _ in MATS]
    sent = [_weight_send(name, wts[name]) for name in names]
    got = _allgather_weights(sent)
    gathered = {name: lax.dynamic_update_slice(g, s[None], (chip,) + (0,) * s.ndim) for name, g, s in zip(names, got, sent)}
    ws = [_full_weights(gathered, l) for l in range(DEPTH)]

    small = {k: wts[k] for k in SMALL}
    loss_part, grad_x, gws, gsmall, g_rel = _local_step(x[0], loss_target[0], ws, rel_table, small)

    rnames = [name for name, _, _ in REDUCED]
    gsend = [jnp.stack([_grad_send(name, gws[l][name], shape, ax) for l in range(DEPTH)]) for name, shape, ax in REDUCED]
    theirs = _sibling_swap(gsend)
    pairs = [_pair_add(g, t_, core, name="grad_pair_" + name) for name, g, t_ in zip(rnames, gsend, theirs)]
    arrived = _chip_scatter(pairs)
    reduced = [_chip_add(lax.dynamic_index_in_dim(p, chip, 0, keepdims=False), a, name="grad_chip_" + name)
               for name, p, a in zip(rnames, pairs, arrived)]
    others = _sibling_share(reduced)
    gshard = {}
    for name, mine, other in zip(rnames, reduced, others):
        layers = [jnp.where(core == l, mine, other) for l in range(DEPTH)]
        gshard[name] = jnp.stack([_grad_recv(name, a) for a in layers])

    conv_w_full = jnp.stack([gws[l]["conv_w"] for l in range(DEPTH)])
    small_red = _allreduce_small(_pack_small(g_rel, gsmall, conv_w_full, loss_part))
    g_rel_r, gsmall_r, conv_w_r, loss_vec = _unpack_small(small_red)
    loss = loss_vec[0]
    shard_w = 2 * D_FF // N_CHIP
    gshard["conv_w"] = lax.dynamic_slice_in_dim(conv_w_r, chip * shard_w, shard_w, axis=2)

    grads = dict(gshard)
    grads.update(gsmall_r)
    grads["rel_table"] = g_rel_r
    deltas, new_m, new_v = {}, {}, {}
    for name, _, _ in MATS:
        shp = wts[name].shape
        v2 = lambda a: a.reshape(-1, shp[-1])
        d_, m_, v_ = _adamw(v2(wts[name]), v2(grads[name]), v2(ms[name]), v2(vs[name]), name="adamw_" + name)
        deltas[name], new_m[name], new_v[name] = d_.reshape(shp), m_.reshape(shp), v_.reshape(shp)
    zero, none = jnp.zeros((LANE,), F32), jnp.zeros((0,), F32)
    sw = _pack_small(wts["rel_table"], {k: wts[k] for k in SMALL}, none, zero)
    sm = _pack_small(ms["rel_table"], {k: ms[k] for k in SMALL}, none, zero)
    sv = _pack_small(vs["rel_table"], {k: vs[k] for k in SMALL}, none, zero)
    sg = _pack_small(g_rel_r, gsmall_r, none, zero)
    sd, smn, svn = _adamw(sw, sg, sm, sv, name="adamw_small")
    for res, buf in ((deltas, sd), (new_m, smn), (new_v, svn)):
        rel_, sm_ = _unpack_small(jnp.pad(buf, ((0, small_red.shape[0] - buf.shape[0]), (0, 0))))[:2]
        res["rel_table"] = rel_
        res.update(sm_)

    return (loss, grad_x[None], *[grads[k] for k in WEIGHT_ORDER], *[deltas[k] for k in WEIGHT_ORDER],
            *[new_m[k] for k in WEIGHT_ORDER], *[new_v[k] for k in WEIGHT_ORDER])
```

```python
import math

import jax
import jax.numpy as jnp
from jax import lax
from jax.experimental import pallas as pl
from jax.experimental.pallas import tpu as pltpu

F32 = jnp.float32
BF16 = jnp.bfloat16
MESH = pl.DeviceIdType.MESH

D_MODEL = 1024
DEPTH = 2
HEAD_DIM = 64
N_HEADS = 8
A_DILS = (1, 4, 16)
C_Q_RANK = 256
C_KV_RANK = 128
C_NOPE = 64
C_ROPE = 32
ROPE_BASE = 10000.0
REL_BUCKETS = 32
REL_MAX_DIST = 2048
D_FF = 2816
ALPHA = (2 * DEPTH) ** 0.25
LN_EPS = 1e-5
RMS_EPS = 1e-6
NEG = -1e30
LOG2E, LN2 = math.log2(math.e), math.log(2.0)
ADAM_LR, ADAM_B1, ADAM_B2, ADAM_EPS, ADAM_WD, ADAM_STEP = 0.001, 0.9, 0.999, 1e-08, 0.01, 10

VMEM_LIMIT_BYTES = 56 * 1024 * 1024
LANE = 128
BLK = 128
TQ = 512
HP = 128
TM_TOKENS = 2048
HALO = 16
BAND_SCALE = HEAD_DIM ** -0.5
BAND_UNROLL = 16

D_IN = 8864
A_COLS = 3 * N_HEADS * HEAD_DIM
ORIG = {"a": 0, "bq": 4608, "bk": 5120, "bv": 5248, "cq": 5376, "cdkv": 5632, "gate": 5792}
M_GATE, M_A0, M_BQ, M_BK, M_BV, M_CQ, M_CDKV, M_COLS = 0, 3072, 4608, 5120, 5376, 5632, 5888, 6144

N_CHIP = 4
MATS = (
    ("w_in", (D_MODEL, D_IN), 1),
    ("w_uq", (C_Q_RANK, 768), 1),
    ("w_ukv", (C_KV_RANK, 1024), 1),
    ("w_branch", (3, 512, D_MODEL), 2),
    ("w_out", (D_MODEL, D_MODEL), 0),
    ("w_ffn_up", (D_MODEL, 2 * D_FF), 1),
    ("conv_w", (3, 2 * D_FF), 1),
    ("w_ffn_down", (D_FF, D_MODEL), 0),
)
SMALL = ("b_gate", "sinks", "q_norm_g", "kv_norm_g", "ln1_g", "ln1_b", "conv_b", "ln2_g", "ln2_b")
SMALL_SIZES = {"b_gate": 3072, "sinks": 8, "q_norm_g": 256, "kv_norm_g": 128, "ln1_g": 1024, "ln1_b": 1024,
               "conv_b": 5632, "ln2_g": 1024, "ln2_b": 1024}
WEIGHT_ORDER = ("rel_table", "w_in", "b_gate", "sinks", "q_norm_g", "kv_norm_g", "w_uq", "w_ukv", "w_branch",
                "w_out", "ln1_g", "ln1_b", "w_ffn_up", "conv_w", "conv_b", "w_ffn_down", "ln2_g", "ln2_b")


def _cparams(sem):
    return pltpu.CompilerParams(dimension_semantics=sem, vmem_limit_bytes=VMEM_LIMIT_BYTES)


def _ceil_to(n, m):
    return -(-n // m) * m


def _pick(n, target):
    if n <= target:
        return n
    best = None
    for t in range(LANE, target + 1, LANE):
        if n % t == 0:
            best = t
    assert best is not None, (n, target)
    return best


def _mm(a, b, *, ta=False, tb=False, out_dtype=F32, tm=1024, tn=1024, tk=2048, name):
    assert not (ta and tb)
    k, m = a.shape[::-1] if not ta else a.shape
    n = b.shape[0] if tb else b.shape[1]
    assert (b.shape[1] if tb else b.shape[0]) == k
    tm, tn, tk = _pick(m, tm), _pick(n, tn), _pick(k, tk)
    nk = k // tk
    dn = (((0 if ta else 1,), (1 if tb else 0,)), ((), ()))

    def body(a_ref, b_ref, o_ref, acc_ref):
        part = lax.dot_general(a_ref[...].astype(BF16), b_ref[...].astype(BF16), dn, preferred_element_type=F32)
        if nk == 1:
            o_ref[...] = part.astype(o_ref.dtype)
        else:
            kk = pl.program_id(2)

            @pl.when(kk == 0)
            def _():
                acc_ref[...] = part

            @pl.when(kk > 0)
            def _():
                acc_ref[...] += part

            @pl.when(kk == nk - 1)
            def _():
                o_ref[...] = acc_ref[...].astype(o_ref.dtype)

    a_spec = pl.BlockSpec((tk, tm), lambda i, j, kk: (kk, i)) if ta else pl.BlockSpec((tm, tk), lambda i, j, kk: (i, kk))
    b_spec = pl.BlockSpec((tn, tk), lambda i, j, kk: (j, kk)) if tb else pl.BlockSpec((tk, tn), lambda i, j, kk: (kk, j))
    return pl.pallas_call(
        body, name=name, grid=(m // tm, n // tn, nk),
        in_specs=[a_spec, b_spec],
        out_specs=pl.BlockSpec((tm, tn), lambda i, j, kk: (i, j)),
        out_shape=jax.ShapeDtypeStruct((m, n), out_dtype),
        scratch_shapes=[pltpu.VMEM((tm, tn) if nk > 1 else (8, LANE), F32)],
        compiler_params=_cparams(("parallel", "parallel", "arbitrary")),
    )(a, b)


def _rowwise(fn, rows, *, pars=(), halos=(), outs=(), accs=(), tm, name, ncol=1, t=None):
    nb = rows[0][0].shape[0]
    t = rows[0][0].shape[1] if t is None else t
    tm = min(tm, t)
    assert t % tm == 0 and tm % 8 == 0
    nt = t // tm
    in_specs, args = [], []
    for spec in rows:
        arr, c, off = spec[:3]
        rb = spec[3] if len(spec) > 3 else 0
        in_specs.append(pl.BlockSpec((1, tm, c), lambda b, cc, i, off=off, rb=rb: (b, i + rb, off + cc)))
        args.append(arr)
    for arr, c, off, kind in halos:
        if kind == "prev":
            im = lambda b, cc, i, off=off: (b, jnp.maximum(i * (tm // HALO) - 1, 0), off + cc)
        else:
            im = lambda b, cc, i, off=off: (b, jnp.minimum((i + 1) * (tm // HALO), t // HALO - 1), off + cc)
        in_specs.append(pl.BlockSpec((1, HALO, c), im))
        args.append(arr)
    for arr, c, off in pars:
        bp, r = arr.shape[:2]
        if bp > 1:
            im = lambda b, cc, i, off=off: (b, 0, off + cc)
        else:
            im = lambda b, cc, i, off=off: (0, 0, off + cc)
        in_specs.append(pl.BlockSpec((1, r, c), im))
        args.append(arr)
    out_specs, out_shapes = [], []
    for ctot, c, off, dt in outs:
        out_specs.append(pl.BlockSpec((1, tm, c), lambda b, cc, i, off=off: (b, i, off + cc)))
        out_shapes.append(jax.ShapeDtypeStruct((nb, t, ctot), dt))
    for r, ctot, c, off in accs:
        out_specs.append(pl.BlockSpec((1, r, c), lambda b, cc, i, off=off: (b, 0, off + cc)))
        out_shapes.append(jax.ShapeDtypeStruct((nb, r, ctot), F32))
    n_in, n_out = len(args), len(outs)

    def body(*refs):
        i = pl.program_id(2)
        res = fn(i, nt, *[r[0].astype(F32) for r in refs[:n_in]])
        if not isinstance(res, (tuple, list)):
            res = (res,)
        for o_ref, val in zip(refs[n_in:n_in + n_out], res[:n_out]):
            o_ref[0] = val.astype(o_ref.dtype)
        for a_ref, val in zip(refs[n_in + n_out:], res[n_out:]):
            @pl.when(i == 0)
            def _(a_ref=a_ref, val=val):
                a_ref[0] = val

            @pl.when(i > 0)
            def _(a_ref=a_ref, val=val):
                a_ref[0] += val

    res = pl.pallas_call(
        body, name=name, grid=(nb, ncol, nt), in_specs=in_specs, out_specs=out_specs, out_shape=out_shapes,
        compiler_params=_cparams(("parallel", "parallel", "arbitrary")),
    )(*args)
    return res


def _dot(a, b):
    return lax.dot_general(a, b, (((1,), (0,)), ((), ())), preferred_element_type=F32)


def _dot_nt(a, b):
    return lax.dot_general(a, b, (((1,), (1,)), ((), ())), preferred_element_type=F32)


def _dot_tn(a, b):
    return lax.dot_general(a, b, (((0,), (0,)), ((), ())), preferred_element_type=F32)


def _rows(parts):
    return jnp.concatenate(parts, axis=0)


def _lane_lo():
    return lax.broadcasted_iota(jnp.int32, (1, LANE), 1) < HEAD_DIM


def _blocks(a):
    return [a[i * BLK:(i + 1) * BLK] for i in range(a.shape[0] // BLK)]


def _band_geometry(t):
    rows = min(BAND_UNROLL, t // BLK) * BLK
    assert t % rows == 0
    return rows, t // rows


def _band_operands(g, k_ref, v_ref, rows):
    start = pl.multiple_of(g * rows, rows)
    pstart = pl.multiple_of(jnp.maximum(g * rows - BLK, 0), BLK)
    out = []
    for ref in (k_ref, v_ref):
        cur = ref[pl.ds(start, rows), :]
        raw = ref[pl.ds(pstart, rows), :]
        shifted = _rows([jnp.zeros((BLK, LANE), raw.dtype), raw[:rows - BLK]])
        out += [_blocks(cur), _blocks(jnp.where(g == 0, shifted, raw))]
    return out


def _band_scores(g, qa, kc, kp, b_ref, a, nb):
    u = len(qa)
    assert nb % u == 0 or u % nb == 0
    tile = lambda blk: _rows([blk] * u)
    firsts = []
    for i in range(u):
        if nb >= u:
            val = jnp.where(lax.rem(g * u, nb) == 0, NEG, 0.0).astype(F32) if i == 0 else 0.0
        else:
            val = NEG if i % nb == 0 else 0.0
        firsts.append(jnp.zeros((BLK, 1), F32) + val)
    sc = _rows([_dot_nt(q, k) for q, k in zip(qa, kc)]) + tile(b_ref[a, :, BLK:2 * BLK])
    sp = _rows([_dot_nt(q, k) for q, k in zip(qa, kp)]) + tile(b_ref[a, :, 0:BLK]) + _rows(firsts)
    return sc, sp


def _band_fwd(src, offs, bias, sinks, *, nb, gqa, name):
    t = src.shape[0]
    rows, nstep = _band_geometry(t)
    qo, ko, vo = offs
    share = 2 if gqa else 1

    def body(sink_ref, q_ref, k_ref, v_ref, b_ref, o_ref, lse_ref):
        hp, g = pl.program_id(0), pl.program_id(1)
        lo = _lane_lo()
        q2 = q_ref[...]
        kc, kp, vc, vp = _band_operands(g, k_ref, v_ref, rows)
        outs, lses = [], []
        for a in range(2):
            sink = sink_ref[2 * hp + a]
            qa = _blocks(jnp.where(lo if a == 0 else jnp.logical_not(lo), q2, jnp.zeros_like(q2)) * BAND_SCALE)
            sc, sp = _band_scores(g, qa, kc, kp, b_ref, a, nb)
            m = jnp.maximum(jnp.maximum(jnp.max(sc, axis=1, keepdims=True), jnp.max(sp, axis=1, keepdims=True)), sink)
            pc, pp = jnp.exp(sc - m), jnp.exp(sp - m)
            l = jnp.sum(pc, axis=1, keepdims=True) + jnp.sum(pp, axis=1, keepdims=True) + jnp.exp(sink - m)
            inv = 1.0 / l
            pc_b, pp_b = _blocks((pc * inv).astype(BF16)), _blocks((pp * inv).astype(BF16))
            outs.append(_rows([_dot(pc_b[i], vc[i]) + _dot(pp_b[i], vp[i]) for i in range(len(qa))]))
            lses.append(m + jnp.log(l))
        o_ref[...] = jnp.where(lo, outs[0], outs[1])
        lse_ref[...] = jnp.where(lo, lses[0], lses[1])

    slab = lambda off: pl.BlockSpec((rows, LANE), lambda hp, g: (g, off + hp))
    whole = lambda off: pl.BlockSpec((t, LANE), lambda hp, g: (0, off + hp // share))
    return pl.pallas_call(
        body, name=name, grid=(N_HEADS // 2, nstep),
        in_specs=[pl.BlockSpec(memory_space=pltpu.SMEM), slab(qo), whole(ko), whole(vo),
                  pl.BlockSpec((2, BLK, 2 * BLK), lambda hp, g: (hp, 0, 0))],
        out_specs=[slab(0), slab(0)],
        out_shape=[jax.ShapeDtypeStruct((t, N_HEADS * HEAD_DIM), F32)] * 2,
        compiler_params=_cparams(("parallel", "parallel")),
    )(sinks, src, src, src, bias)


def _band_bwd(src, offs, do, lse, dpr, bias, sinks, *, nb, gqa, name):
    t = src.shape[0]
    rows, nstep = _band_geometry(t)
    qo, ko, vo = offs
    share = 2 if gqa else 1

    def fold(a):
        acc = a[0:BLK]
        for i in range(1, rows // BLK):
            acc = acc + a[i * BLK:(i + 1) * BLK]
        return acc

    def body(sink_ref, q_ref, k_ref, v_ref, do_ref, lse_ref, dpr_ref, b_ref,
             dq_ref, dk_ref, dv_ref, ds_ref, dsink_ref, dk_acc, dv_acc):
        hp, g = pl.program_id(0), pl.program_id(1)
        lo = _lane_lo()
        hi = jnp.logical_not(lo)

        @pl.when(jnp.logical_and(g == 0, lax.rem(hp, share) == 0))
        def _():
            dk_acc[...] = jnp.zeros_like(dk_acc)
            dv_acc[...] = jnp.zeros_like(dv_acc)

        @pl.when(g == 0)
        def _():
            ds_ref[...] = jnp.zeros_like(ds_ref)
            dsink_ref[...] = jnp.zeros_like(dsink_ref)

        q2, do2, lse2, dpr2 = q_ref[...], do_ref[...], lse_ref[...], dpr_ref[...]
        lse_sw, dpr_sw = pltpu.roll(lse2, HEAD_DIM, axis=1), pltpu.roll(dpr2, HEAD_DIM, axis=1)
        kc, kp, vc, vp = _band_operands(g, k_ref, v_ref, rows)
        dqs, dk_cur, dk_prev, dv_cur, dv_prev = [], None, None, None, None
        for a in range(2):
            sink = sink_ref[2 * hp + a]
            mine = lo if a == 0 else hi
            qa = _blocks(jnp.where(mine, q2, jnp.zeros_like(q2)) * BAND_SCALE)
            doa = _blocks(jnp.where(mine, do2, jnp.zeros_like(do2)))
            lse_a, dpr_a = jnp.where(mine, lse2, lse_sw), jnp.where(mine, dpr2, dpr_sw)
            sc, sp = _band_scores(g, qa, kc, kp, b_ref, a, nb)
            pc, pp = jnp.exp(sc - lse_a), jnp.exp(sp - lse_a)
            dsc = pc * (_rows([_dot_nt(d, v) for d, v in zip(doa, vc)]) - dpr_a)
            dsp = pp * (_rows([_dot_nt(d, v) for d, v in zip(doa, vp)]) - dpr_a)
            ds_ref[a, :, BLK:2 * BLK] += fold(dsc)
            ds_ref[a, :, 0:BLK] += fold(dsp)
            dsink_ref[a] -= jnp.sum(jnp.exp(sink - lse_a) * dpr_a, axis=0, keepdims=True)
            dsc_b, dsp_b = _blocks(dsc.astype(BF16)), _blocks(dsp.astype(BF16))
            pc_b, pp_b = _blocks(pc.astype(BF16)), _blocks(pp.astype(BF16))
            dqs.append(_rows([_dot(dsc_b[i], kc[i]) + _dot(dsp_b[i], kp[i]) for i in range(len(qa))]))
            parts = [_rows([_dot_tn(x[i], y[i]) for i in range(len(qa))])
                     for x, y in ((dsc_b, qa), (dsp_b, qa), (pc_b, doa), (pp_b, doa))]
            if a == 0:
                dk_cur, dk_prev, dv_cur, dv_prev = parts
            else:
                dk_cur, dk_prev, dv_cur, dv_prev = dk_cur + parts[0], dk_prev + parts[1], dv_cur + parts[2], dv_prev + parts[3]
        dq_ref[...] = (jnp.where(lo, dqs[0], dqs[1]) * BAND_SCALE).astype(dq_ref.dtype)
        start = pl.multiple_of(g * rows, rows)
        after = pl.multiple_of(g * rows + BLK, BLK)
        dk_acc[pl.ds(after, rows), :] += dk_cur
        dk_acc[pl.ds(start, rows), :] += dk_prev
        dv_acc[pl.ds(after, rows), :] += dv_cur
        dv_acc[pl.ds(start, rows), :] += dv_prev

        @pl.when(g == nstep - 1)
        def _():
            dk_ref[...] = dk_acc[BLK:, :].astype(dk_ref.dtype)
            dv_ref[...] = dv_acc[BLK:, :].astype(dv_ref.dtype)

    slab = lambda off: pl.BlockSpec((rows, LANE), lambda hp, g: (g, off + hp))
    whole = lambda off: pl.BlockSpec((t, LANE), lambda hp, g: (0, off + hp // share))
    per_pair = lambda shp: pl.BlockSpec((2,) + shp, lambda hp, g: (hp,) + (0,) * len(shp))
    kv_cols = N_HEADS * HEAD_DIM // share
    return pl.pallas_call(
        body, name=name, grid=(N_HEADS // 2, nstep),
        in_specs=[pl.BlockSpec(memory_space=pltpu.SMEM), slab(qo), whole(ko), whole(vo), slab(0), slab(0), slab(0),
                  per_pair((BLK, 2 * BLK))],
        out_specs=[slab(0), whole(0), whole(0), per_pair((BLK, 2 * BLK)), per_pair((1, LANE))],
        out_shape=[jax.ShapeDtypeStruct((t, N_HEADS * HEAD_DIM), BF16), jax.ShapeDtypeStruct((t, kv_cols), BF16),
                   jax.ShapeDtypeStruct((t, kv_cols), BF16), jax.ShapeDtypeStruct((N_HEADS, BLK, 2 * BLK), F32),
                   jax.ShapeDtypeStruct((N_HEADS, 1, LANE), F32)],
        scratch_shapes=[pltpu.VMEM((t + BLK, LANE), F32), pltpu.VMEM((t + BLK, LANE), F32)],
        compiler_params=_cparams(("arbitrary", "arbitrary")),
    )(sinks, src, src, src, do, lse, dpr, bias)


MLA_V_OFF = N_HEADS


def _diag_mask(keys_on_rows=False):
    rows, cols = lax.broadcasted_iota(jnp.int32, (TQ, TQ), 0), lax.broadcasted_iota(jnp.int32, (TQ, TQ), 1)
    return rows <= cols if keys_on_rows else cols <= rows


def _mla_specs(t):
    blk = lambda f: pl.BlockSpec((TQ, LANE), lambda hp, qi, f=f: (qi, f(hp)))
    whole = lambda f: pl.BlockSpec((t, LANE), lambda hp, qi, f=f: (0, f(hp)))
    return blk, whole


def _mla_fwd(q, k, vt, *, name):
    t = q.shape[0]
    n = t // TQ
    scale = (C_NOPE + C_ROPE) ** -0.5

    def body(q0_ref, q1_ref, k0_ref, k1_ref, vt_ref, o_ref, lse_ref, m_ref, l_ref, acc_ref):
        qi = pl.program_id(1)
        qs, ks = (q0_ref[...], q1_ref[...]), (k0_ref, k1_ref)
        m_ref[...] = jnp.full_like(m_ref, NEG)
        l_ref[...] = jnp.zeros_like(l_ref)
        acc_ref[...] = jnp.zeros_like(acc_ref)

        def step(kj, diagonal):
            rows = pl.ds(pl.multiple_of(kj * TQ, TQ), TQ)
            vtb = vt_ref[0, kj]
            for a in range(2):
                s = _dot_nt(ks[a][rows, :], qs[a]) * (scale * LOG2E)
                if diagonal:
                    s = jnp.where(_diag_mask(keys_on_rows=True), s, NEG)
                m_prev = m_ref[a]
                m_new = jnp.maximum(m_prev, jnp.max(s, axis=0, keepdims=True))
                alpha = jnp.exp2(m_prev - m_new)
                p = jnp.exp2(s - m_new)
                l_ref[a] = alpha * l_ref[a] + jnp.sum(p, axis=0, keepdims=True)
                acc_ref[a] = alpha * acc_ref[a] + _dot(vtb, p.astype(BF16))
                m_ref[a] = m_new

        def kloop(kj, c2):
            step(kj, False)
            return c2

        lax.fori_loop(0, qi, kloop, 0)
        step(qi, True)
        first = lax.broadcasted_iota(jnp.int32, (LANE, 1), 0) < HEAD_DIM
        ot = jnp.where(first, acc_ref[0] * (1.0 / l_ref[0]), acc_ref[1] * (1.0 / l_ref[1]))
        lset = jnp.where(first, m_ref[0] * LN2 + jnp.log(l_ref[0]), m_ref[1] * LN2 + jnp.log(l_ref[1]))
        o_ref[...] = ot.T
        lse_ref[...] = lset.T

    blk, whole = _mla_specs(t)
    return pl.pallas_call(
        body, name=name, grid=(N_HEADS // 2, n),
        in_specs=[blk(lambda hp: 2 * hp), blk(lambda hp: 2 * hp + 1), whole(lambda hp: 2 * hp), whole(lambda hp: 2 * hp + 1),
                  pl.BlockSpec((1, n, LANE, TQ), lambda hp, qi: (hp, 0, 0, 0))],
        out_specs=[blk(lambda hp: hp), blk(lambda hp: hp)],
        out_shape=[jax.ShapeDtypeStruct((t, N_HEADS * HEAD_DIM), F32)] * 2,
        scratch_shapes=[pltpu.VMEM((2, 1, TQ), F32), pltpu.VMEM((2, 1, TQ), F32), pltpu.VMEM((2, LANE, TQ), F32)],
        compiler_params=_cparams(("parallel", "parallel")),
    )(q, q, k, k, vt)


def _mla_bwd(q, k, kv, do, lse, delta, *, name):
    t = q.shape[0]
    n = t // TQ
    scale = (C_NOPE + C_ROPE) ** -0.5

    def body(q0_ref, q1_ref, k0_ref, k1_ref, v_ref, do_ref, lse_ref, dl_ref,
             dq_ref, dk_ref, dv_ref, dq_acc, dk_acc, dv_acc):
        qi = pl.program_id(1)
        lo = _lane_lo()

        @pl.when(qi == 0)
        def _():
            dk_acc[...] = jnp.zeros_like(dk_acc)
            dv_acc[...] = jnp.zeros_like(dv_acc)

        dq_acc[...] = jnp.zeros_like(dq_acc)
        qs, ks = (q0_ref[...], q1_ref[...]), (k0_ref, k1_ref)
        do2, lse2, dl2 = do_ref[...], lse_ref[...], dl_ref[...]
        lse_sw, dl_sw = pltpu.roll(lse2, HEAD_DIM, axis=1), pltpu.roll(dl2, HEAD_DIM, axis=1)
        heads = []
        for a in range(2):
            mine = lo if a == 0 else jnp.logical_not(lo)
            heads.append((jnp.where(mine, do2, jnp.zeros_like(do2)), jnp.where(mine, lse2, lse_sw)[:, 0:1] * LOG2E,
                          jnp.where(mine, dl2, dl_sw)[:, 0:1]))

        def step(kj, diagonal):
            rows = pl.ds(pl.multiple_of(kj * TQ, TQ), TQ)
            vb = v_ref[rows, :]
            for a, (doa, lse_a, dl_a) in enumerate(heads):
                kb = ks[a][rows, :]
                s = _dot_nt(qs[a], kb) * (scale * LOG2E)
                if diagonal:
                    s = jnp.where(_diag_mask(), s, NEG)
                p = jnp.exp2(s - lse_a)
                ds = (p * (_dot_nt(doa, vb) - dl_a)).astype(BF16)
                dq_acc[a] += _dot(ds, kb)
                dk_acc[a, rows, :] += _dot_tn(ds, qs[a])
                dv_acc[rows, :] += _dot_tn(p.astype(BF16), doa)

        def kloop(kj, c2):
            step(kj, False)
            return c2

        lax.fori_loop(0, qi, kloop, 0)
        step(qi, True)
        dq_ref[:, 0:LANE] = (dq_acc[0] * scale).astype(dq_ref.dtype)
        dq_ref[:, LANE:2 * LANE] = (dq_acc[1] * scale).astype(dq_ref.dtype)

        @pl.when(qi == n - 1)
        def _():
            dk_ref[:, 0:LANE] = (dk_acc[0] * scale).astype(dk_ref.dtype)
            dk_ref[:, LANE:2 * LANE] = (dk_acc[1] * scale).astype(dk_ref.dtype)
            dv_ref[...] = dv_acc[...].astype(dv_ref.dtype)

    blk, whole = _mla_specs(t)
    even, odd, pair = (lambda hp: 2 * hp), (lambda hp: 2 * hp + 1), (lambda hp: hp)
    wide = jax.ShapeDtypeStruct((t, N_HEADS * LANE), BF16)
    return pl.pallas_call(
        body, name=name, grid=(N_HEADS // 2, n),
        in_specs=[blk(even), blk(odd), whole(even), whole(odd), whole(lambda hp: MLA_V_OFF + hp), blk(pair), blk(pair), blk(pair)],
        out_specs=[pl.BlockSpec((TQ, 2 * LANE), lambda hp, qi: (qi, hp)), pl.BlockSpec((t, 2 * LANE), lambda hp, qi: (0, hp)), whole(pair)],
        out_shape=[wide, wide, jax.ShapeDtypeStruct((t, N_HEADS * HEAD_DIM), BF16)],
        scratch_shapes=[pltpu.VMEM((2, TQ, LANE), F32), pltpu.VMEM((2, t, LANE), F32), pltpu.VMEM((t, LANE), F32)],
        compiler_params=_cparams(("arbitrary", "arbitrary")),
    )(q, q, k, k, kv, do, lse, delta)


def _bias_lookup(bucket, table_t, *, name):
    nh, npos = bucket.shape
    tp = 4096

    def body(b_ref, t_ref, o_ref):
        bk, tab = b_ref[...], t_ref[...]
        acc = jnp.zeros(bk.shape, F32)
        for i in range(REL_BUCKETS):
            acc = jnp.where(bk == i, tab[:, i:i + 1], acc)
        o_ref[...] = acc

    return pl.pallas_call(
        body, name=name, grid=(npos // tp,),
        in_specs=[pl.BlockSpec((nh, tp), lambda i: (0, i)), pl.BlockSpec((nh, REL_BUCKETS), lambda i: (0, 0))],
        out_specs=pl.BlockSpec((nh, tp), lambda i: (0, i)),
        out_shape=jax.ShapeDtypeStruct((nh, npos), F32),
        compiler_params=_cparams(("parallel",)),
    )(bucket, table_t)


def _bias_grad(bucket, ds0, ds1, *, name):
    nh, npos = bucket.shape
    tp = 4096

    def body(b_ref, a_ref, c_ref, o_ref):
        i = pl.program_id(0)
        bk, ds = b_ref[...], a_ref[...] + c_ref[...]
        lane = lax.broadcasted_iota(jnp.int32, (nh, REL_BUCKETS), 1)
        acc = jnp.zeros((nh, REL_BUCKETS), F32)
        for j in range(REL_BUCKETS):
            col = jnp.sum(jnp.where(bk == j, ds, 0.0), axis=1, keepdims=True)
            acc = acc + jnp.where(lane == j, col, 0.0)

        @pl.when(i == 0)
        def _():
            o_ref[...] = acc

        @pl.when(i > 0)
        def _():
            o_ref[...] += acc

    return pl.pallas_call(
        body, name=name, grid=(npos // tp,),
        in_specs=[pl.BlockSpec((nh, tp), lambda i: (0, i))] * 3,
        out_specs=pl.BlockSpec((nh, REL_BUCKETS), lambda i: (0, 0)),
        out_shape=jax.ShapeDtypeStruct((nh, REL_BUCKETS), F32),
        compiler_params=_cparams(("arbitrary",)),
    )(bucket, ds0, ds1)


def _t5_bucket(dist):
    n = jnp.maximum(dist, 0)
    max_exact = REL_BUCKETS // 2
    scaled = jnp.log(jnp.maximum(n, 1).astype(F32) / max_exact) / math.log(REL_MAX_DIST / max_exact)
    large = max_exact + (scaled * (REL_BUCKETS - max_exact)).astype(jnp.int32)
    return jnp.where(n < max_exact, n, jnp.minimum(large, REL_BUCKETS - 1))


def _bucket_index():
    qi = jnp.arange(BLK)[:, None]
    ci = jnp.arange(2 * BLK)[None, :]
    step = BLK + qi - ci
    per_group = [_t5_bucket(step * d).reshape(1, -1) for d in A_DILS + (1,)]
    return jnp.concatenate([jnp.tile(b, (N_HEADS, 1)) for b in per_group], axis=0).astype(jnp.int32)


def _sigmoid(x):
    return 1.0 / (1.0 + jnp.exp(-x))


def _ln_stats(z):
    mu = jnp.mean(z, axis=-1, keepdims=True)
    zc = z - mu
    var = jnp.mean(zc * zc, axis=-1, keepdims=True)
    return zc * lax.rsqrt(var + LN_EPS)


def _ln_fwd(x, mix, g, b, *, name):
    def fn(i, nt, xv, mv, gv, bv):
        z = ALPHA * xv + mv
        y = _ln_stats(z) * gv + bv
        return y, y, z

    c = x.shape[-1]
    y, yb, z = _rowwise(fn, [(x[None], c, 0), (mix[None], c, 0)], pars=[(g.reshape(1, 1, c), c, 0), (b.reshape(1, 1, c), c, 0)],
                        outs=[(c, c, 0, F32), (c, c, 0, BF16), (c, c, 0, F32)], tm=512, name=name)
    return y[0], yb[0], z[0]


def _ln_bwd(z, g, dys, coefs, *, name):
    n = len(dys)

    def fn(i, nt, zv, *rest):
        gv = rest[n]
        dy = coefs[0] * rest[0]
        for cf, t in zip(coefs[1:], rest[1:n]):
            dy = dy + cf * t
        mu = jnp.mean(zv, axis=-1, keepdims=True)
        zc = zv - mu
        r = lax.rsqrt(jnp.mean(zc * zc, axis=-1, keepdims=True) + LN_EPS)
        xh = zc * r
        dxh = dy * gv
        dz = r * (dxh - jnp.mean(dxh, axis=-1, keepdims=True) - xh * jnp.mean(dxh * xh, axis=-1, keepdims=True))
        return dz, dz, jnp.sum(dy * xh, axis=0, keepdims=True), jnp.sum(dy, axis=0, keepdims=True)

    c = z.shape[-1]
    dz, dzb, dg, db = _rowwise(fn, [(z[None], c, 0)] + [(d[None], c, 0) for d in dys], pars=[(g.reshape(1, 1, c), c, 0)],
                               outs=[(c, c, 0, F32), (c, c, 0, BF16)], accs=[(1, c, c, 0), (1, c, c, 0)], tm=512, name=name)
    return dz[0], dzb[0], dg.reshape(c), db.reshape(c)


def _rms_fwd(src, c, off, g, *, name):
    def fn(i, nt, xv, gv):
        return xv * lax.rsqrt(jnp.mean(xv * xv, axis=-1, keepdims=True) + RMS_EPS) * gv

    return _rowwise(fn, [(src[None], c, off)], pars=[(g.reshape(1, 1, c), c, 0)], outs=[(c, c, 0, BF16)], tm=1024, name=name)[0][0]


def _rms_bwd(src, c, off, g, dy, *, name):
    def fn(i, nt, xv, dyv, gv):
        r = lax.rsqrt(jnp.mean(xv * xv, axis=-1, keepdims=True) + RMS_EPS)
        gd = gv * dyv
        dx = gd * r - xv * (r * r * r) * jnp.mean(gd * xv, axis=-1, keepdims=True)
        return dx, jnp.sum(dyv * xv * r, axis=0, keepdims=True)

    dx, dg = _rowwise(fn, [(src[None], c, off), (dy[None], c, 0)], pars=[(g.reshape(1, 1, c), c, 0)],
                      outs=[(c, c, 0, BF16)], accs=[(1, c, c, 0)], tm=1024, name=name)
    return dx[0], dg.reshape(c)


def _rope_slabs(x, n_slab, c, s, *, add=None, to_front=False, name):
    half = C_ROPE // 2

    def fn(i, nt, xv, cv, sv, *rest):
        lane = lax.broadcasted_iota(jnp.int32, (1, LANE), 1)
        extra = pltpu.roll(rest[0], C_NOPE, axis=1) if rest else None
        outs = []
        for h in range(n_slab):
            xs = xv[:, h * LANE:(h + 1) * LANE]
            if extra is not None:
                xs = xs + extra
            swapped = jnp.where(lane < C_NOPE + half, pltpu.roll(xs, LANE - half, axis=1), pltpu.roll(xs, half, axis=1))
            y = xs * cv + swapped * sv
            if to_front:
                y = jnp.where(lane < C_ROPE, pltpu.roll(y, LANE - C_NOPE, axis=1), 0.0)
            outs.append(y)
        return jnp.concatenate(outs, axis=1) if n_slab > 1 else outs[0]

    w = n_slab * LANE
    rows = [(x[None], w, 0), (c[None], LANE, 0), (s[None], LANE, 0)]
    if add is not None:
        rows.append((add[0][None], LANE, add[1]))
    return _rowwise(fn, rows, outs=[(w, w, 0, BF16)], tm=512, name=name)[0][0]


def _merge_fwd(proj, b_gate, ys, *, name):
    def fn(i, nt, g0, g1, g2, ya, yb, yc, bg):
        return (_sigmoid(g0 + bg[:, 0:1024]) * ya + _sigmoid(g1 + bg[:, 1024:2048]) * yb
                + _sigmoid(g2 + bg[:, 2048:3072]) * yc)

    rows = [(proj[None], 1024, j) for j in range(3)] + [(y[None], 1024, 0) for y in ys]
    return _rowwise(fn, rows, pars=[(b_gate.reshape(1, 1, 3072), 3072, 0)], outs=[(1024, 1024, 0, BF16)], tm=512, name=name)[0][0]


def _merge_bwd(proj, b_gate, ys, dm, *, name):
    def fn(i, nt, g0, g1, g2, ya, yb, yc, dmv, bg):
        outs, dgs = [], []
        for j, (gp, y) in enumerate(((g0, ya), (g1, yb), (g2, yc))):
            s = _sigmoid(gp + bg[:, j * 1024:(j + 1) * 1024])
            outs.append(s * dmv)
            dgs.append(dmv * y * s * (1.0 - s))
        return outs + dgs + [jnp.sum(d, axis=0, keepdims=True) for d in dgs]

    rows = [(proj[None], 1024, j) for j in range(3)] + [(y[None], 1024, 0) for y in ys] + [(dm[None], 1024, 0)]
    res = _rowwise(fn, rows, pars=[(b_gate.reshape(1, 1, 3072), 3072, 0)], outs=[(1024, 1024, 0, BF16)] * 6,
                   accs=[(1, 1024, 1024, 0)] * 3, tm=256, name=name)
    dys = [r[0] for r in res[0:3]]
    dgp = [r[0] for r in res[3:6]]
    dbg = jnp.concatenate([r.reshape(1024) for r in res[6:9]])
    return dys, dgp, dbg


def _shift_down(u, halo, i, k):
    ext = jnp.concatenate([jnp.where(i > 0, halo, 0.0), u], axis=0)
    return pltpu.roll(ext, k, axis=0)[HALO:]


def _shift_up(u, halo, i, nt, k):
    ext = jnp.concatenate([u, jnp.where(i < nt - 1, halo, 0.0)], axis=0)
    n = ext.shape[0]
    return pltpu.roll(ext, n - k, axis=0)[:n - HALO]


GLU_C = D_FF // 2


def _conv(u, halo, i, w, b):
    return w[0:1] * _shift_down(u, halo, i, 2) + w[1:2] * _shift_down(u, halo, i, 1) + w[2:3] * u + b


def _glu_fwd(ug, uv, conv_w, conv_b, *, name):
    def fn(i, nt, g, v, hg, hv, wg, wv, bg, bv):
        cg, cv = _conv(g, hg, i, wg, bg), _conv(v, hv, i, wv, bv)
        return cg * _sigmoid(cg) * cv

    w3, b3 = conv_w[None], conv_b.reshape(1, 1, -1)
    c = GLU_C
    return _rowwise(fn, [(ug[None], c, 0), (uv[None], c, 0)], halos=[(ug[None], c, 0, "prev"), (uv[None], c, 0, "prev")],
                    pars=[(w3, c, 0), (w3, c, 2), (b3, c, 0), (b3, c, 2)], outs=[(D_FF, c, 0, BF16)], tm=256, ncol=2, name=name)[0][0]


def _glu_bwd_a(ug, uv, conv_w, conv_b, dh, *, name):
    def fn(i, nt, g, v, dhv, hg, hv, wg, wv, bg, bv):
        g1, g2 = _shift_down(g, hg, i, 1), _shift_down(g, hg, i, 2)
        v1, v2 = _shift_down(v, hv, i, 1), _shift_down(v, hv, i, 2)
        cg = wg[0:1] * g2 + wg[1:2] * g1 + wg[2:3] * g + bg
        cv = wv[0:1] * v2 + wv[1:2] * v1 + wv[2:3] * v + bv
        s = _sigmoid(cg)
        dcv = dhv * cg * s
        dcg = dhv * cv * (s * (1.0 + cg * (1.0 - s)))
        red = lambda a: jnp.sum(a, axis=0, keepdims=True)
        return (dcg, dcv, red(dcg), red(dcv), red(dcg * g2), red(dcg * g1), red(dcg * g),
                red(dcv * v2), red(dcv * v1), red(dcv * v))

    w3, b3 = conv_w[None], conv_b.reshape(1, 1, -1)
    c = GLU_C
    res = _rowwise(fn, [(ug[None], c, 0), (uv[None], c, 0), (dh[None], c, 0)],
                   halos=[(ug[None], c, 0, "prev"), (uv[None], c, 0, "prev")],
                   pars=[(w3, c, 0), (w3, c, 2), (b3, c, 0), (b3, c, 2)],
                   outs=[(D_FF, c, 0, BF16), (D_FF, c, 0, BF16)], accs=[(1, D_FF, c, 0)] * 8, tm=256, ncol=2, name=name)
    dcg, dcv = res[0][0], res[1][0]
    dconv_b = jnp.concatenate([res[2].reshape(D_FF), res[3].reshape(D_FF)])
    dconv_w = jnp.concatenate([jnp.concatenate([res[4 + j].reshape(1, D_FF) for j in range(3)], axis=0),
                               jnp.concatenate([res[7 + j].reshape(1, D_FF) for j in range(3)], axis=0)], axis=1)
    return dcg, dcv, dconv_w, dconv_b


def _glu_bwd_b(dc, conv_w, half, *, name):
    def fn(i, nt, d, hd, w):
        return w[2:3] * d + w[1:2] * _shift_up(d, hd, i, nt, 1) + w[0:1] * _shift_up(d, hd, i, nt, 2)

    c = GLU_C
    return _rowwise(fn, [(dc[None], c, 0)], halos=[(dc[None], c, 0, "next")], pars=[(conv_w[None], c, 2 * half)],
                    outs=[(D_FF, c, 0, BF16)], tm=256, ncol=2, name=name)[0][0]


def _loss_and_grad(y, tgt, *, name):
    def fn(i, nt, yv, tv):
        err = yv - tv
        part = jnp.sum(jnp.sum(err * err, axis=0, keepdims=True), axis=1, keepdims=True) * (0.5 / D_MODEL)
        return err * (1.0 / D_MODEL), jnp.zeros((1, LANE), F32) + part

    dy, part = _rowwise(fn, [(y[None], D_MODEL, 0), (tgt[None], D_MODEL, 0)], outs=[(D_MODEL, D_MODEL, 0, F32)],
                        accs=[(1, LANE, LANE, 0)], tm=512, name=name)
    return dy[0], part.reshape(LANE)


def _lincomb(terms, coefs, *, name):
    def fn(i, nt, *vs):
        acc = coefs[0] * vs[0]
        for cf, v in zip(coefs[1:], vs[1:]):
            acc = acc + cf * v
        return acc

    c = terms[0].shape[-1]
    return _rowwise(fn, [(a[None], c, 0) for a in terms], outs=[(c, c, 0, F32)], tm=512, name=name)[0][0]


def _sum_rows(terms, *, tm, name, dtype=F32):
    def fn(i, nt, *vs):
        acc = vs[0]
        for v in vs[1:]:
            acc = acc + v
        return acc

    c = terms[0].shape[-1]
    return _rowwise(fn, [(t, c, 0) for t in terms], outs=[(c, c, 0, dtype)], tm=tm, name=name)[0]


def _head_sums(x):
    lo = _lane_lo()
    parts = []
    for j in range(x.shape[1] // LANE):
        blk = x[:, j * LANE:(j + 1) * LANE]
        s_lo = jnp.sum(jnp.where(lo, blk, 0.0), axis=1, keepdims=True)
        s_hi = jnp.sum(jnp.where(lo, 0.0, blk), axis=1, keepdims=True)
        parts.append(jnp.where(lo, s_lo, s_hi))
    return jnp.concatenate(parts, axis=1)


def _group_weights(l0, l1, l2):
    m = jnp.maximum(jnp.maximum(l0, l1), l2)
    es = [jnp.exp(l - m) for l in (l0, l1, l2)]
    inv = 1.0 / (es[0] + es[1] + es[2])
    return [e * inv for e in es]


def _combine_fwd(os_, lses, *, name):
    def fn(i, nt, o0, o1, o2, l0, l1, l2):
        w = _group_weights(l0, l1, l2)
        return w[0] * o0 + w[1] * o1 + w[2] * o2

    c = os_[0].shape[-1]
    return _rowwise(fn, [(a[None], c, 0) for a in list(os_) + list(lses)], outs=[(c, c, 0, BF16)], tm=512, name=name)[0][0]


def _combine_bwd(os_, lses, do_a, *, name):
    def fn(i, nt, o0, o1, o2, l0, l1, l2, da):
        ws = _group_weights(l0, l1, l2)
        dws = [_head_sums(da * o) for o in (o0, o1, o2)]
        mean = ws[0] * dws[0] + ws[1] * dws[1] + ws[2] * dws[2]
        return [w * da for w in ws] + [w * mean for w in ws]

    c = do_a.shape[-1]
    res = _rowwise(fn, [(a[None], c, 0) for a in list(os_) + list(lses) + [do_a]], outs=[(c, c, 0, BF16)] * 3 + [(c, c, 0, F32)] * 3,
                   tm=256, name=name)
    return [r[0] for r in res[0:3]], [r[0] for r in res[3:6]]


def _delta(do, o, *, name):
    def fn(i, nt, d, ov):
        return d, _head_sums(d * ov)

    c = do.shape[-1]
    res = _rowwise(fn, [(do[None], c, 0), (o[None], c, 0)], outs=[(c, c, 0, BF16), (c, c, 0, F32)], tm=512, name=name)
    return res[0][0], res[1][0]


def _adamw(w, g, m, v, *, name):
    c1 = 1.0 - ADAM_B1 ** ADAM_STEP
    c2 = 1.0 - ADAM_B2 ** ADAM_STEP

    def fn(i, nt, wv, gv, mv, vv):
        mn = ADAM_B1 * mv + (1.0 - ADAM_B1) * gv
        vn = ADAM_B2 * vv + (1.0 - ADAM_B2) * (gv * gv)
        delta = -ADAM_LR * ((mn / c1) / (jnp.sqrt(vn / c2) + ADAM_EPS) + ADAM_WD * wv)
        return delta, mn, vn

    r, c = w.shape
    rp = _ceil_to(r, 8)
    pad = lambda a: jnp.pad(a, ((0, rp - r), (0, 0))) if rp != r else a
    tm = rp
    for cand in (128, 64, 32, 16, 8):
        if rp % cand == 0:
            tm = cand
            break
    res = _rowwise(fn, [(pad(a)[None], c, 0) for a in (w, g, m, v)], outs=[(c, c, 0, F32)] * 3, tm=tm, name=name)
    return [x[0][:r] for x in res]


ANY = pl.BlockSpec(memory_space=pl.ANY)


def _place():
    x, y, c = lax.axis_index("x"), lax.axis_index("y"), lax.axis_index("c")
    chips = [(1 - x, y), (x, 1 - y), (1 - x, 1 - y)]
    return x, y, c, chips


def _allgather_weights(arrs):
    n = len(arrs)

    def body(*refs):
        ins, outs, send_sems, recv_sems = refs[:n], refs[n:2 * n], refs[2 * n], refs[2 * n + 1]
        x, y, c, chips = _place()
        j = 2 * x + y
        me, sibling = (x, y, c), (x, y, 1 - c)

        def cp(i, k, src, chip_idx, half, to):
            return pltpu.make_async_remote_copy(src_ref=src, dst_ref=outs[i].at[chip_idx, half], send_sem=send_sems.at[k],
                                                recv_sem=recv_sems.at[k], device_id=to, device_id_type=MESH)

        first, passed, own = [], [], []
        for i in range(n):
            for r, (cx, cy) in enumerate(chips):
                first.append(cp(i, 3 * i + r, ins[i].at[c], j, c, (cx, cy, c)))
                passed.append(cp(i, 3 * (n + i) + r, outs[i].at[2 * cx + cy, c], 2 * cx + cy, c, sibling))
            own += [cp(i, 6 * n + 2 * i + half, ins[i].at[half], j, half, sibling) for half in range(2)]
        for d in first + own:
            d.start()
        for i in range(n):
            for r, (cx, cy) in enumerate(chips):
                cp(i, 3 * i + r, ins[i].at[c], 2 * cx + cy, c, me).wait_recv()
                passed[3 * i + r].start()
        for i in range(n):
            for r, (cx, cy) in enumerate(chips):
                cp(i, 3 * (n + i) + r, ins[i].at[c], 2 * cx + cy, 1 - c, me).wait_recv()
        for d in own:
            d.wait_recv()
        for d in first + passed + own:
            d.wait_send()

    return pl.pallas_call(
        body, name="allgather_weights", in_specs=[ANY] * n, out_specs=[ANY] * n,
        out_shape=[jax.ShapeDtypeStruct((N_CHIP,) + a.shape, a.dtype) for a in arrs],
        scratch_shapes=[pltpu.SemaphoreType.DMA((8 * n,)), pltpu.SemaphoreType.DMA((8 * n,))],
    )(*arrs)


def _sibling_swap(gs):
    n = len(gs)

    def body(*refs):
        ins, outs, send_sems, recv_sems = refs[:n], refs[n:2 * n], refs[2 * n], refs[2 * n + 1]
        x, y, c, _ = _place()
        cps = [pltpu.make_async_remote_copy(src_ref=ins[i].at[1 - c], dst_ref=outs[i], send_sem=send_sems.at[i],
                                            recv_sem=recv_sems.at[i], device_id=(x, y, 1 - c), device_id_type=MESH)
               for i in range(n)]
        for d in cps:
            d.start()
        for d in cps:
            d.wait_recv()
        for d in cps:
            d.wait_send()

    return pl.pallas_call(
        body, name="grad_sibling_swap", in_specs=[ANY] * n, out_specs=[ANY] * n,
        out_shape=[jax.ShapeDtypeStruct(g.shape[1:], g.dtype) for g in gs],
        scratch_shapes=[pltpu.SemaphoreType.DMA((n,)), pltpu.SemaphoreType.DMA((n,))],
    )(*gs)


def _chip_scatter(ps):
    n = len(ps)

    def body(*refs):
        ins, outs, send_sems, recv_sems = refs[:n], refs[n:2 * n], refs[2 * n], refs[2 * n + 1]
        x, y, c, chips = _place()
        sends = []
        for i in range(n):
            for r, (cx, cy) in enumerate(chips):
                sends.append(pltpu.make_async_remote_copy(src_ref=ins[i].at[2 * cx + cy], dst_ref=outs[i].at[r], send_sem=send_sems.at[3 * i + r],
                                                          recv_sem=recv_sems.at[3 * i + r], device_id=(cx, cy, c), device_id_type=MESH))
        for d in sends:
            d.start()
        for d in sends:
            d.wait_recv()
        for d in sends:
            d.wait_send()

    return pl.pallas_call(
        body, name="grad_chip_scatter", in_specs=[ANY] * n, out_specs=[ANY] * n,
        out_shape=[jax.ShapeDtypeStruct((3,) + p.shape[1:], p.dtype) for p in ps],
        scratch_shapes=[pltpu.SemaphoreType.DMA((3 * n,)), pltpu.SemaphoreType.DMA((3 * n,))],
    )(*ps)


def _sibling_share(rs):
    n = len(rs)

    def body(*refs):
        ins, outs, send_sems, recv_sems = refs[:n], refs[n:2 * n], refs[2 * n], refs[2 * n + 1]
        x, y, c, _ = _place()
        cps = [pltpu.make_async_remote_copy(src_ref=ins[i], dst_ref=outs[i], send_sem=send_sems.at[i], recv_sem=recv_sems.at[i],
                                            device_id=(x, y, 1 - c), device_id_type=MESH) for i in range(n)]
        for d in cps:
            d.start()
        for d in cps:
            d.wait_recv()
        for d in cps:
            d.wait_send()

    return pl.pallas_call(
        body, name="grad_sibling_share", in_specs=[ANY] * n, out_specs=[ANY] * n,
        out_shape=[jax.ShapeDtypeStruct(r.shape, r.dtype) for r in rs],
        scratch_shapes=[pltpu.SemaphoreType.DMA((n,)), pltpu.SemaphoreType.DMA((n,))],
    )(*rs)


def _allreduce_small(s):
    rows, w = s.shape
    n_dev = 8

    def body(s_ref, out_ref, slots, send_sems, recv_sems):
        x, y, c, _ = _place()
        me = 4 * x + 2 * y + c
        slots[me] = s_ref[...]
        peers = []
        for r in range(1, n_dev):
            px = 1 - x if r & 4 else x
            py = 1 - y if r & 2 else y
            pc = 1 - c if r & 1 else c
            peers.append((px, py, pc))
        sends = [pltpu.make_async_remote_copy(src_ref=s_ref, dst_ref=slots.at[me], send_sem=send_sems.at[r], recv_sem=recv_sems.at[r],
                                              device_id=peer, device_id_type=MESH) for r, peer in enumerate(peers)]
        for d in sends:
            d.start()
        for r, (px, py, pc) in enumerate(peers):
            pltpu.make_async_remote_copy(src_ref=s_ref, dst_ref=slots.at[4 * px + 2 * py + pc], send_sem=send_sems.at[r],
                                         recv_sem=recv_sems.at[r], device_id=(x, y, c), device_id_type=MESH).wait_recv()
        for d in sends:
            d.wait_send()
        acc = slots[0]
        for k in range(1, n_dev):
            acc = acc + slots[k]
        out_ref[...] = acc

    vm = pl.BlockSpec(memory_space=pltpu.VMEM)
    return pl.pallas_call(
        body, name="allreduce_small", in_specs=[vm], out_specs=vm, out_shape=jax.ShapeDtypeStruct((rows, w), F32),
        scratch_shapes=[pltpu.VMEM((n_dev, rows, w), F32), pltpu.SemaphoreType.DMA((n_dev - 1,)), pltpu.SemaphoreType.DMA((n_dev - 1,))],
    )(s)


W_IN_SHARD = D_IN // N_CHIP
W_IN_ROWS_G = 2304
REDUCED = tuple(m for m in MATS if m[0] != "conv_w")
CONV_W_SIZE = 3 * 2 * D_FF


def _weight_send(name, a):
    if name == "w_in":
        return jnp.swapaxes(a, 1, 2).astype(BF16)
    return a if name == "conv_w" else a.astype(BF16)


def _full_weights(gathered, l):
    g = {k: v[:, l] for k, v in gathered.items()}
    s = g["w_in"].astype(F32).reshape(D_IN, D_MODEL)
    dup = lambda a: jnp.concatenate([a[0:64], a[0:64], a[64:128], a[64:128]], axis=0)
    o = ORIG
    wm_t = jnp.concatenate([s[o["gate"]:], s[o["a"]:o["a"] + A_COLS], s[o["bq"]:o["bk"]], dup(s[o["bk"]:o["bv"]]), dup(s[o["bv"]:o["cq"]]),
                            s[o["cq"]:o["gate"]], jnp.zeros((M_COLS - M_CDKV - (o["gate"] - o["cdkv"]), D_MODEL), F32)], axis=0).astype(BF16)
    wg_t = [s[o["a"] + gi * A_COLS:o["a"] + (gi + 1) * A_COLS].astype(BF16) for gi in (1, 2)]
    full = {name: jnp.moveaxis(g[name], 0, ax).reshape(shape) for name, shape, ax in MATS if name != "w_in"}
    uq = full["w_uq"].reshape(C_Q_RANK, N_HEADS, C_NOPE + C_ROPE)
    ukv = full["w_ukv"].reshape(C_KV_RANK, N_HEADS, 2 * C_NOPE)
    w_uq_p = _pad_lanes(uq).reshape(C_Q_RANK, N_HEADS * LANE)
    w_ukv_p = jnp.concatenate([_pad_lanes(ukv[:, :, :C_NOPE]).reshape(C_KV_RANK, N_HEADS * LANE),
                               ukv[:, :, C_NOPE:].reshape(C_KV_RANK, N_HEADS * HEAD_DIM)], axis=1)
    return {"wm_t": wm_t, "wg_t": wg_t, "w_uq_p": w_uq_p, "w_ukv_p": w_ukv_p, "w_branch": full["w_branch"], "w_out": full["w_out"],
            "wup_g": full["w_ffn_up"][:, :D_FF], "wup_v": full["w_ffn_up"][:, D_FF:], "conv_w": full["conv_w"],
            "w_ffn_down": full["w_ffn_down"]}


def _grad_send(name, g, shape, ax):
    if name == "w_in":
        return jnp.pad(g.reshape(N_CHIP, W_IN_SHARD, D_MODEL), ((0, 0), (0, W_IN_ROWS_G - W_IN_SHARD), (0, 0)))
    split = shape[:ax] + (N_CHIP, shape[ax] // N_CHIP) + shape[ax + 1:]
    return jnp.moveaxis(g.reshape(split), ax, 0)


def _grad_recv(name, r):
    return r[:W_IN_SHARD].T if name == "w_in" else r


def _pack_small(rel, small, conv_w, extra):
    parts = [rel.reshape(-1)]
    for l in range(DEPTH):
        for name in SMALL:
            parts.append(small[name][l].reshape(-1))
    parts += [conv_w.reshape(-1), extra]
    flat = jnp.concatenate(parts)
    rows = _ceil_to(-(-flat.shape[0] // LANE), 8)
    return jnp.pad(flat, (0, rows * LANE - flat.shape[0])).reshape(rows, LANE)


def _unpack_small(buf):
    flat = buf.reshape(-1)
    rel = flat[:REL_BUCKETS * 32].reshape(REL_BUCKETS, 32)
    off = REL_BUCKETS * 32
    small = {name: [] for name in SMALL}
    for l in range(DEPTH):
        for name in SMALL:
            n = SMALL_SIZES[name]
            small[name].append(flat[off:off + n])
            off += n
    conv_w = flat[off:off + DEPTH * CONV_W_SIZE].reshape(DEPTH, 3, 2 * D_FF)
    off += DEPTH * CONV_W_SIZE
    return rel, {k: jnp.stack(v) for k, v in small.items()}, conv_w, flat[off:off + LANE]


def _rows2d(a, lead):
    return a.reshape(a.shape[:lead] + (-1, a.shape[-1]))


def _row_tile(rows):
    for cand in (512, 256, 128, 64, 32, 16, 8):
        if rows % cand == 0:
            return cand
    raise ValueError(rows)


def _pair_add(g, got, core, *, name):
    g2, got2 = _rows2d(g, 1), _rows2d(got, 0)
    rows, c = got2.shape
    tm = _row_tile(rows)
    flag = jnp.zeros((1, 1, LANE), F32) + core.astype(F32)

    def fn(i, nt, a0, a1, b, f):
        return jnp.where(f[:, 0:1] == 0.0, a0, a1) + b

    stacked = g2.reshape(1, 2 * rows, c)
    out = _rowwise(fn, [(stacked, c, 0, 0), (stacked, c, 0, rows // tm), (got2[None], c, 0)], pars=[(flag, LANE, 0)],
                   outs=[(c, c, 0, BF16)], tm=tm, t=rows, name=name)[0][0]
    return out.reshape(got.shape)


def _chip_add(own, got, *, name):
    own2, got2 = _rows2d(own, 0), _rows2d(got, 1)
    rows, c = own2.shape
    tm = _row_tile(rows)

    def fn(i, nt, a, b0, b1, b2):
        return ((a.astype(F32) + b0.astype(F32)) + b1.astype(F32)) + b2.astype(F32)

    stacked = got2.reshape(1, 3 * rows, c)
    out = _rowwise(fn, [(own2[None], c, 0)] + [(stacked, c, 0, k * (rows // tm)) for k in range(3)],
                   outs=[(c, c, 0, F32)], tm=tm, t=rows, name=name)[0][0]
    return out.reshape(own.shape)


def _perm(a, d):
    if d == 1:
        return a
    t = a.shape[0]
    return jnp.swapaxes(a.reshape((t // d, d) + a.shape[1:]), 0, 1).reshape(a.shape)


def _unperm(a, d):
    if d == 1:
        return a
    t = a.shape[0]
    return jnp.swapaxes(a.reshape((d, t // d) + a.shape[1:]), 0, 1).reshape(a.shape)


def _pad_lanes(a, w=HP):
    return jnp.pad(a, [(0, 0)] * (a.ndim - 1) + [(0, w - a.shape[-1])])


def _rope_tables(t):
    pos = jnp.arange(t, dtype=F32)
    inv_freq = ROPE_BASE ** (-jnp.arange(0, C_ROPE, 2, dtype=F32) / C_ROPE)
    ang = pos[:, None] * inv_freq[None, :]
    cos, sin = jnp.cos(ang), jnp.sin(ang)
    ones, zeros = jnp.ones((t, C_NOPE), F32), jnp.zeros((t, C_NOPE), F32)
    tail = LANE - C_NOPE - C_ROPE
    c = jnp.concatenate([ones, cos, cos, ones[:, :tail]], axis=1)
    s = jnp.concatenate([zeros, -sin, sin, zeros[:, :tail]], axis=1)
    return c, s


def _band_calls(t, proj, projs_g, sinks):
    none = jnp.full((N_HEADS,), NEG, F32)
    a0 = M_A0 // LANE
    calls = [(proj, (a0, a0 + 4, a0 + 8), t // BLK, BLK, False, none)]
    calls += [(pg, (0, 4, 8), t // (d * BLK), BLK, False, none) for pg, d in zip(projs_g, A_DILS[1:])]
    calls.append((proj, (M_BQ // LANE, M_BK // LANE, M_BV // LANE), t // BLK, BLK - 1, True, sinks.astype(F32)))
    return calls


def _layer_fwd(l, x, xb, w, p, biases, rope_cs):
    t = x.shape[0]
    n = f"l{l}_"
    xps = [_perm(xb, d) for d in A_DILS[1:]]
    proj = _mm(xb, w["wm_t"], tb=True, out_dtype=BF16, tm=TM_TOKENS, name=n + "proj")
    projs_g = [_mm(xp, wg, tb=True, out_dtype=BF16, tm=TM_TOKENS, tn=768, name=n + f"proj_g{i + 1}")
               for i, (xp, wg) in enumerate(zip(xps, w["wg_t"]))]
    s = {"xb": xb, "xps": xps, "proj": proj, "projs_g": projs_g}

    calls = _band_calls(t, proj, projs_g, p["sinks"])
    outs = [_band_fwd(src, offs, biases[i], sk, nb=nb, gqa=gqa, name=n + f"band{i}")
            for i, (src, offs, nb, lim, gqa, sk) in enumerate(calls)]
    os_ = [_unperm(outs[gi][0], d) for gi, d in enumerate(A_DILS)]
    lses = [_unperm(outs[gi][1], d) for gi, d in enumerate(A_DILS)]
    o_a = _combine_fwd(os_, lses, name=n + "combine_fwd")
    o_b_f, lse_b = outs[3]
    o_b = o_b_f.astype(BF16)
    s.update(os=os_, lses=lses, lses_p=[outs[gi][1] for gi in range(3)], o_b=o_b_f, lse_b=lse_b)

    rq = _rms_fwd(proj, C_Q_RANK, M_CQ // C_Q_RANK, p["q_norm_g"], name=n + "rms_q")
    rkv = _rms_fwd(proj, C_KV_RANK, M_CDKV // C_KV_RANK, p["kv_norm_g"], name=n + "rms_kv")
    q_cp = _mm(rq, w["w_uq_p"], out_dtype=BF16, name=n + "uq")
    kv_cp = _mm(rkv, w["w_ukv_p"], out_dtype=BF16, name=n + "ukv")
    q_full = _rope_slabs(q_cp, N_HEADS, rope_cs[0], rope_cs[1], name=n + "rope_q")
    k_full = _rope_slabs(kv_cp, N_HEADS, rope_cs[0], rope_cs[1], add=(proj, (M_CDKV + C_KV_RANK) // LANE), name=n + "rope_k")
    vt = jnp.transpose(kv_cp[:, N_HEADS * LANE:].T.reshape(N_HEADS // 2, LANE, t // TQ, TQ), (0, 2, 1, 3))
    o_c_f, lse_c = _mla_fwd(q_full, k_full, vt, name=n + "mla_fwd")
    o_c = o_c_f.astype(BF16)
    s.update(rq=rq, rkv=rkv, q_full=q_full, k_full=k_full, kv_cp=kv_cp, lse_c=lse_c, o_c=o_c_f)

    obs = [o_a, o_b, o_c]
    ys = [_mm(o, w["w_branch"][i], out_dtype=BF16, name=n + f"branch{i}") for i, o in enumerate(obs)]
    merged = _merge_fwd(proj, p["b_gate"], ys, name=n + "merge")
    mix = _mm(merged, w["w_out"], name=n + "out")
    x1f, x1b, z1 = _ln_fwd(x, mix, p["ln1_g"], p["ln1_b"], name=n + "ln1")
    s.update(obs=obs, ys=ys, merged=merged, z1=z1, x1b=x1b)

    ug = _mm(x1b, w["wup_g"], tm=TM_TOKENS, tn=1408, out_dtype=BF16, name=n + "up_g")
    uv = _mm(x1b, w["wup_v"], tm=TM_TOKENS, tn=1408, out_dtype=BF16, name=n + "up_v")
    h = _glu_fwd(ug, uv, w["conv_w"], p["conv_b"], name=n + "glu")
    ff = _mm(h, w["w_ffn_down"], tm=TM_TOKENS, tk=1408, name=n + "down")
    x2f, x2b, z2 = _ln_fwd(x1f, ff, p["ln2_g"], p["ln2_b"], name=n + "ln2")
    s.update(ug=ug, uv=uv, h=h, z2=z2)
    return x2f, x2b, s


def _layer_bwd(l, s, dys, coefs, w, p, biases, rope_cs):
    n = f"l{l}b_"
    t = s["z2"].shape[0]
    gw, gs = {}, {}

    dz2, dz2b, gs["ln2_g"], gs["ln2_b"] = _ln_bwd(s["z2"], p["ln2_g"], dys, coefs, name=n + "ln2")
    dh = _mm(dz2b, w["w_ffn_down"], tb=True, tm=TM_TOKENS, tn=1408, out_dtype=BF16, name=n + "d_h")
    gw["w_ffn_down"] = _mm(s["h"], dz2b, ta=True, tm=1408, tk=1024, name=n + "g_down")
    dcg, dcv, gw["conv_w"], gs["conv_b"] = _glu_bwd_a(s["ug"], s["uv"], w["conv_w"], p["conv_b"], dh, name=n + "glu_a")
    dug = _glu_bwd_b(dcg, w["conv_w"], 0, name=n + "glu_bg")
    duv = _glu_bwd_b(dcv, w["conv_w"], 1, name=n + "glu_bv")
    dx1_g = _mm(dug, w["wup_g"], tb=True, tm=TM_TOKENS, tk=1408, name=n + "d_x1g")
    dx1_v = _mm(duv, w["wup_v"], tb=True, tm=TM_TOKENS, tk=1408, name=n + "d_x1v")
    gw["w_ffn_up"] = jnp.concatenate([_mm(s["x1b"], dug, ta=True, tn=1408, tk=1024, name=n + "g_upg"),
                                      _mm(s["x1b"], duv, ta=True, tn=1408, tk=1024, name=n + "g_upv")], axis=1)

    dz1, dz1b, gs["ln1_g"], gs["ln1_b"] = _ln_bwd(s["z1"], p["ln1_g"], [dz2, dx1_g, dx1_v], [ALPHA, 1.0, 1.0], name=n + "ln1")
    dmerged = _mm(dz1b, w["w_out"], tb=True, out_dtype=BF16, name=n + "d_merged")
    gw["w_out"] = _mm(s["merged"], dz1b, ta=True, name=n + "g_out")
    dys_b, dgp, gs["b_gate"] = _merge_bwd(s["proj"], p["b_gate"], s["ys"], dmerged, name=n + "merge")
    dos = [_mm(dy, w["w_branch"][i], tb=True, out_dtype=BF16, name=n + f"d_o{i}") for i, dy in enumerate(dys_b)]
    gw["w_branch"] = jnp.stack([_mm(o, dy, ta=True, name=n + f"g_branch{i}") for i, (o, dy) in enumerate(zip(s["obs"], dys_b))])

    do_gs, dpr_gs = _combine_bwd(s["os"], s["lses"], dos[0], name=n + "combine")
    do_b, dpr_b = _delta(dos[1], s["o_b"], name=n + "delta_b")
    do_list = [_perm(a, d) for a, d in zip(do_gs, A_DILS)] + [do_b]
    dpr_list = [_perm(a, d) for a, d in zip(dpr_gs, A_DILS)] + [dpr_b]
    lse_list = s["lses_p"] + [s["lse_b"]]
    calls = _band_calls(t, s["proj"], s["projs_g"], p["sinks"])
    band = [_band_bwd(src, offs, do_list[i], lse_list[i], dpr_list[i], biases[i], sk, nb=nb, gqa=gqa, name=n + f"band{i}")
            for i, (src, offs, nb, lim, gqa, sk) in enumerate(calls)]
    gs["sinks"] = band[3][4][:, 0, 0]
    ds_sum = jnp.concatenate([b_[3] for b_ in band], axis=0)

    do_c, delta_c = _delta(dos[2], s["o_c"], name=n + "delta_c")
    dq, dk, dv = _mla_bwd(s["q_full"], s["k_full"], s["kv_cp"], do_c, s["lse_c"], delta_c, name=n + "mla")
    dq_cp = _rope_slabs(dq, N_HEADS, rope_cs[0], -rope_cs[1], name=n + "rope_q")
    dk_sum = _rowwise(lambda i, nt, *vs: sum(vs[1:], vs[0]), [(dk[None], LANE, hh) for hh in range(N_HEADS)],
                      outs=[(LANE, LANE, 0, F32)], tm=1024, name=n + "krope_sum")[0][0]
    dkr = _rope_slabs(dk_sum, 1, rope_cs[0], -rope_cs[1], to_front=True, name=n + "rope_k")
    dkv_cp = jnp.concatenate([dk, dv], axis=1)
    d_rq = _mm(dq_cp, w["w_uq_p"], tb=True, name=n + "d_rq")
    d_rkv = _mm(dkv_cp, w["w_ukv_p"], tb=True, name=n + "d_rkv")
    g_uq = _mm(s["rq"], dq_cp, ta=True, name=n + "g_uq")
    g_ukv = _mm(s["rkv"], dkv_cp, ta=True, name=n + "g_ukv")
    gw["w_uq"] = g_uq.reshape(C_Q_RANK, N_HEADS, LANE)[:, :, :C_NOPE + C_ROPE].reshape(C_Q_RANK, -1)
    kw = N_HEADS * LANE
    gw["w_ukv"] = jnp.concatenate([g_ukv[:, :kw].reshape(C_KV_RANK, N_HEADS, LANE)[:, :, :C_NOPE],
                                   g_ukv[:, kw:].reshape(C_KV_RANK, N_HEADS, HEAD_DIM)], axis=2).reshape(C_KV_RANK, -1)
    dcq, gs["q_norm_g"] = _rms_bwd(s["proj"], C_Q_RANK, M_CQ // C_Q_RANK, p["q_norm_g"], d_rq, name=n + "rms_q")
    dckv, gs["kv_norm_g"] = _rms_bwd(s["proj"], C_KV_RANK, M_CDKV // C_KV_RANK, p["kv_norm_g"], d_rkv, name=n + "rms_kv")
    dcdkv = jnp.concatenate([dckv, dkr], axis=1)

    dproj = jnp.concatenate(dgp + list(band[0][:3]) + list(band[3][:3]) + [dcq, dcdkv], axis=1)
    dprojs_g = [jnp.concatenate(band[gi][:3], axis=1) for gi in (1, 2)]
    dx_terms = [_mm(dproj, w["wm_t"], tm=TM_TOKENS, tk=1024, name=n + "d_x")]
    dx_terms += [_unperm(_mm(dp, wg, tm=TM_TOKENS, tk=768, name=n + f"d_x_g{i + 1}"), d)
                 for i, (dp, wg, d) in enumerate(zip(dprojs_g, w["wg_t"], A_DILS[1:]))]
    g_main = _mm(dproj, s["xb"], ta=True, name=n + "g_in")
    g_groups = [_mm(dp, xp, ta=True, tm=768, name=n + f"g_in_g{i + 1}") for i, (xp, dp) in enumerate(zip(s["xps"], dprojs_g))]
    fold = lambda a, tag: _sum_rows([a.reshape(2, 2, HEAD_DIM, D_MODEL)[:, j] for j in range(2)], tm=HEAD_DIM,
                                    name=n + "g_fold_" + tag).reshape(2 * HEAD_DIM, D_MODEL)
    gw["w_in"] = jnp.concatenate([g_main[M_A0:M_BQ], g_groups[0], g_groups[1], g_main[M_BQ:M_BK], fold(g_main[M_BK:M_BV], "k"),
                                  fold(g_main[M_BV:M_CQ], "v"), g_main[M_CQ:M_CDKV + C_KV_RANK + C_ROPE], g_main[M_GATE:M_A0]], axis=0)
    return [dz1] + dx_terms, [ALPHA, 1.0, 1.0, 1.0], gw, gs, ds_sum


def _local_step(x, target, ws, rel_table, small):
    t = x.shape[0]
    ps = [{k: small[k][l] for k in SMALL} for l in range(DEPTH)]
    bucket = _bucket_index()
    bias_all = _bias_lookup(bucket, rel_table.T, name="bias_lookup").reshape(4, N_HEADS, BLK, 2 * BLK)
    step = BLK + jnp.arange(BLK)[:, None] - jnp.arange(2 * BLK)[None, :]
    biases = [jnp.where((step >= 0) & (step <= lim), bias_all[i], NEG) for i, lim in enumerate((BLK, BLK, BLK, BLK - 1))]
    rope_cs = _rope_tables(t)

    saved, h, hb = [], x, x.astype(BF16)
    for l in range(DEPTH):
        h, hb, s = _layer_fwd(l, h, hb, ws[l], ps[l], biases, rope_cs)
        saved.append(s)
    dy, loss_part = _loss_and_grad(h, target, name="loss")

    dys, coefs = [dy], [1.0]
    gws, gss, dss = [None] * DEPTH, [None] * DEPTH, [None] * DEPTH
    for l in reversed(range(DEPTH)):
        dys, coefs, gws[l], gss[l], dss[l] = _layer_bwd(l, saved[l], dys, coefs, ws[l], ps[l], biases, rope_cs)
    grad_x = _lincomb(dys, coefs, name="grad_x")
    npos = 2 * BLK * BLK
    g_rel = _bias_grad(bucket, dss[0].reshape(4 * N_HEADS, npos), dss[1].reshape(4 * N_HEADS, npos), name="bias_grad").T
    gsmall = {k: jnp.stack([gss[l][k] for l in range(DEPTH)]) for k in SMALL}
    return loss_part, grad_x, gws, gsmall, g_rel


def kernel(x, rel_table, w_in, b_gate, sinks, q_norm_g, kv_norm_g, w_uq, w_ukv, w_branch, w_out, ln1_g, ln1_b, w_ffn_up, conv_w, conv_b, w_ffn_down, ln2_g, ln2_b, loss_target, m_rel_table, m_w_in, m_b_gate, m_sinks, m_q_norm_g, m_kv_norm_g, m_w_uq, m_w_ukv, m_w_branch, m_w_out, m_ln1_g, m_ln1_b, m_w_ffn_up, m_conv_w, m_conv_b, m_w_ffn_down, m_ln2_g, m_ln2_b, v_rel_table, v_w_in, v_b_gate, v_sinks, v_q_norm_g, v_kv_norm_g, v_w_uq, v_w_ukv, v_w_branch, v_w_out, v_ln1_g, v_ln1_b, v_w_ffn_up, v_conv_w, v_conv_b, v_w_ffn_down, v_ln2_g, v_ln2_b):
    wts = dict(rel_table=rel_table, w_in=w_in, b_gate=b_gate, sinks=sinks, q_norm_g=q_norm_g, kv_norm_g=kv_norm_g, w_uq=w_uq,
               w_ukv=w_ukv, w_branch=w_branch, w_out=w_out, ln1_g=ln1_g, ln1_b=ln1_b, w_ffn_up=w_ffn_up, conv_w=conv_w,
               conv_b=conv_b, w_ffn_down=w_ffn_down, ln2_g=ln2_g, ln2_b=ln2_b)
    ms = dict(rel_table=m_rel_table, w_in=m_w_in, b_gate=m_b_gate, sinks=m_sinks, q_norm_g=m_q_norm_g, kv_norm_g=m_kv_norm_g,
              w_uq=m_w_uq, w_ukv=m_w_ukv, w_branch=m_w_branch, w_out=m_w_out, ln1_g=m_ln1_g, ln1_b=m_ln1_b, w_ffn_up=m_w_ffn_up,
              conv_w=m_conv_w, conv_b=m_conv_b, w_ffn_down=m_w_ffn_down, ln2_g=m_ln2_g, ln2_b=m_ln2_b)
    vs = dict(rel_table=v_rel_table, w_in=v_w_in, b_gate=v_b_gate, sinks=v_sinks, q_norm_g=v_q_norm_g, kv_norm_g=v_kv_norm_g,
              w_uq=v_w_uq, w_ukv=v_w_ukv, w_branch=v_w_branch, w_out=v_w_out, ln1_g=v_ln1_g, ln1_b=v_ln1_b, w_ffn_up=v_w_ffn_up,
              conv_w=v_conv_w, conv_b=v_conv_b, w_ffn_down=v_w_ffn_down, ln2_g=v_ln2_g, ln2_b=v_ln2_b)

    core = lax.axis_index("c")
    chip = 2 * lax.axis_index("x") + lax.axis_index("y")

    names = [name for name, _, _ in MATS]
    gathered = dict(zip(names, _allgather_weights([_weight_send(name, wts[name]) for name in names])))
    ws = [_full_weights(gathered, l) for l in range(DEPTH)]

    small = {k: wts[k] for k in SMALL}
    loss_part, grad_x, gws, gsmall, g_rel = _local_step(x[0], loss_target[0], ws, rel_table, small)

    rnames = [name for name, _, _ in REDUCED]
    gsend = [jnp.stack([_grad_send(name, gws[l][name], shape, ax) for l in range(DEPTH)]) for name, shape, ax in REDUCED]
    theirs = _sibling_swap(gsend)
    pairs = [_pair_add(g, t_, core, name="grad_pair_" + name) for name, g, t_ in zip(rnames, gsend, theirs)]
    arrived = _chip_scatter(pairs)
    reduced = [_chip_add(lax.dynamic_index_in_dim(p, chip, 0, keepdims=False), a, name="grad_chip_" + name)
               for name, p, a in zip(rnames, pairs, arrived)]
    others = _sibling_share(reduced)
    gshard = {}
    for name, mine, other in zip(rnames, reduced, others):
        layers = [jnp.where(core == l, mine, other) for l in range(DEPTH)]
        gshard[name] = jnp.stack([_grad_recv(name, a) for a in layers])

    conv_w_full = jnp.stack([gws[l]["conv_w"] for l in range(DEPTH)])
    small_red = _allreduce_small(_pack_small(g_rel, gsmall, conv_w_full, loss_part))
    g_rel_r, gsmall_r, conv_w_r, loss_vec = _unpack_small(small_red)
    loss = loss_vec[0]
    shard_w = 2 * D_FF // N_CHIP
    gshard["conv_w"] = lax.dynamic_slice_in_dim(conv_w_r, chip * shard_w, shard_w, axis=2)

    grads = dict(gshard)
    grads.update(gsmall_r)
    grads["rel_table"] = g_rel_r
    deltas, new_m, new_v = {}, {}, {}
    for name, _, _ in MATS:
        shp = wts[name].shape
        v2 = lambda a: a.reshape(-1, shp[-1])
        d_, m_, v_ = _adamw(v2(wts[name]), v2(grads[name]), v2(ms[name]), v2(vs[name]), name="adamw_" + name)
        deltas[name], new_m[name], new_v[name] = d_.reshape(shp), m_.reshape(shp), v_.reshape(shp)
    zero, none = jnp.zeros((LANE,), F32), jnp.zeros((0,), F32)
    sw = _pack_small(wts["rel_table"], {k: wts[k] for k in SMALL}, none, zero)
    sm = _pack_small(ms["rel_table"], {k: ms[k] for k in SMALL}, none, zero)
    sv = _pack_small(vs["rel_table"], {k: vs[k] for k in SMALL}, none, zero)
    sg = _pack_small(g_rel_r, gsmall_r, none, zero)
    sd, smn, svn = _adamw(sw, sg, sm, sv, name="adamw_small")
    for res, buf in ((deltas, sd), (new_m, smn), (new_v, svn)):
        rel_, sm_ = _unpack_small(jnp.pad(buf, ((0, small_red.shape[0] - buf.shape[0]), (0, 0))))[:2]
        res["rel_table"] = rel_
        res.update(sm_)

    return (loss, grad_x[None], *[grads[k] for k in WEIGHT_ORDER], *[deltas[k] for k in WEIGHT_ORDER],
            *[new_m[k] for k in WEIGHT_ORDER], *[new_v[k] for k in WEIGHT_ORDER])
```

```python
import math

import jax
import jax.numpy as jnp
from jax import lax
from jax.experimental import pallas as pl
from jax.experimental.pallas import tpu as pltpu

F32 = jnp.float32
BF16 = jnp.bfloat16
MESH = pl.DeviceIdType.MESH

D_MODEL = 1024
DEPTH = 2
HEAD_DIM = 64
N_HEADS = 8
A_DILS = (1, 4, 16)
C_Q_RANK = 256
C_KV_RANK = 128
C_NOPE = 64
C_ROPE = 32
ROPE_BASE = 10000.0
REL_BUCKETS = 32
REL_MAX_DIST = 2048
D_FF = 2816
ALPHA = (2 * DEPTH) ** 0.25
LN_EPS = 1e-5
RMS_EPS = 1e-6
NEG = -1e30
LOG2E, LN2 = math.log2(math.e), math.log(2.0)
ADAM_LR, ADAM_B1, ADAM_B2, ADAM_EPS, ADAM_WD, ADAM_STEP = 0.001, 0.9, 0.999, 1e-08, 0.01, 10

VMEM_LIMIT_BYTES = 56 * 1024 * 1024
LANE = 128
BLK = 128
TQ = 512
TM_TOKENS = 2048
HALO = 16
BAND_SCALE = HEAD_DIM ** -0.5
BAND_UNROLL = 16

D_IN = 8864
A_COLS = 3 * N_HEADS * HEAD_DIM
ORIG = {"a": 0, "bq": 4608, "bk": 5120, "bv": 5248, "cq": 5376, "cdkv": 5632, "gate": 5792}
M_GATE, M_A0, M_BQ, M_BK, M_BV, M_CQ, M_CDKV, M_COLS = 0, 3072, 4608, 5120, 5376, 5632, 5888, 6144

N_CHIP = 4
MATS = (
    ("w_in", (D_MODEL, D_IN), 1),
    ("w_uq", (C_Q_RANK, 768), 1),
    ("w_ukv", (C_KV_RANK, 1024), 1),
    ("w_branch", (3, 512, D_MODEL), 2),
    ("w_out", (D_MODEL, D_MODEL), 0),
    ("w_ffn_up", (D_MODEL, 2 * D_FF), 1),
    ("conv_w", (3, 2 * D_FF), 1),
    ("w_ffn_down", (D_FF, D_MODEL), 0),
)
SMALL = ("b_gate", "sinks", "q_norm_g", "kv_norm_g", "ln1_g", "ln1_b", "conv_b", "ln2_g", "ln2_b")
SMALL_SIZES = {"b_gate": 3072, "sinks": 8, "q_norm_g": 256, "kv_norm_g": 128, "ln1_g": 1024, "ln1_b": 1024,
               "conv_b": 5632, "ln2_g": 1024, "ln2_b": 1024}
WEIGHT_ORDER = ("rel_table", "w_in", "b_gate", "sinks", "q_norm_g", "kv_norm_g", "w_uq", "w_ukv", "w_branch",
                "w_out", "ln1_g", "ln1_b", "w_ffn_up", "conv_w", "conv_b", "w_ffn_down", "ln2_g", "ln2_b")


def _cparams(sem):
    return pltpu.CompilerParams(dimension_semantics=sem, vmem_limit_bytes=VMEM_LIMIT_BYTES)


def _ceil_to(n, m):
    return -(-n // m) * m


def _pick(n, target):
    if n <= target:
        return n
    best = None
    for t in range(LANE, target + 1, LANE):
        if n % t == 0:
            best = t
    assert best is not None, (n, target)
    return best


def _mm(a, b, *, ta=False, tb=False, out_dtype=F32, tm=1024, tn=1024, tk=2048, name):
    assert not (ta and tb)
    k, m = a.shape[::-1] if not ta else a.shape
    n = b.shape[0] if tb else b.shape[1]
    assert (b.shape[1] if tb else b.shape[0]) == k
    tm, tn, tk = _pick(m, tm), _pick(n, tn), _pick(k, tk)
    nk = k // tk
    dn = (((0 if ta else 1,), (1 if tb else 0,)), ((), ()))

    def body(a_ref, b_ref, o_ref, acc_ref):
        part = lax.dot_general(a_ref[...].astype(BF16), b_ref[...].astype(BF16), dn, preferred_element_type=F32)
        if nk == 1:
            o_ref[...] = part.astype(o_ref.dtype)
        else:
            kk = pl.program_id(2)

            @pl.when(kk == 0)
            def _():
                acc_ref[...] = part

            @pl.when(kk > 0)
            def _():
                acc_ref[...] += part

            @pl.when(kk == nk - 1)
            def _():
                o_ref[...] = acc_ref[...].astype(o_ref.dtype)

    a_spec = pl.BlockSpec((tk, tm), lambda i, j, kk: (kk, i)) if ta else pl.BlockSpec((tm, tk), lambda i, j, kk: (i, kk))
    b_spec = pl.BlockSpec((tn, tk), lambda i, j, kk: (j, kk)) if tb else pl.BlockSpec((tk, tn), lambda i, j, kk: (kk, j))
    return pl.pallas_call(
        body, name=name, grid=(m // tm, n // tn, nk),
        in_specs=[a_spec, b_spec],
        out_specs=pl.BlockSpec((tm, tn), lambda i, j, kk: (i, j)),
        out_shape=jax.ShapeDtypeStruct((m, n), out_dtype),
        scratch_shapes=[pltpu.VMEM((tm, tn) if nk > 1 else (8, LANE), F32)],
        compiler_params=_cparams(("parallel", "parallel", "arbitrary")),
    )(a, b)


def _rowwise(fn, rows, *, pars=(), halos=(), outs=(), accs=(), tm, name, ncol=1, t=None):
    nb = rows[0][0].shape[0]
    t = rows[0][0].shape[1] if t is None else t
    tm = min(tm, t)
    assert t % tm == 0 and tm % 8 == 0
    nt = t // tm
    in_specs, args = [], []
    for spec in rows:
        arr, c, off = spec[:3]
        rb = spec[3] if len(spec) > 3 else 0
        in_specs.append(pl.BlockSpec((1, tm, c), lambda b, cc, i, off=off, rb=rb: (b, i + rb, off + cc)))
        args.append(arr)
    for arr, c, off, kind in halos:
        if kind == "prev":
            im = lambda b, cc, i, off=off: (b, jnp.maximum(i * (tm // HALO) - 1, 0), off + cc)
        else:
            im = lambda b, cc, i, off=off: (b, jnp.minimum((i + 1) * (tm // HALO), t // HALO - 1), off + cc)
        in_specs.append(pl.BlockSpec((1, HALO, c), im))
        args.append(arr)
    for arr, c, off in pars:
        bp, r = arr.shape[:2]
        if bp > 1:
            im = lambda b, cc, i, off=off: (b, 0, off + cc)
        else:
            im = lambda b, cc, i, off=off: (0, 0, off + cc)
        in_specs.append(pl.BlockSpec((1, r, c), im))
        args.append(arr)
    out_specs, out_shapes = [], []
    for ctot, c, off, dt in outs:
        out_specs.append(pl.BlockSpec((1, tm, c), lambda b, cc, i, off=off: (b, i, off + cc)))
        out_shapes.append(jax.ShapeDtypeStruct((nb, t, ctot), dt))
    for r, ctot, c, off in accs:
        out_specs.append(pl.BlockSpec((1, r, c), lambda b, cc, i, off=off: (b, 0, off + cc)))
        out_shapes.append(jax.ShapeDtypeStruct((nb, r, ctot), F32))
    n_in, n_out = len(args), len(outs)

    def body(*refs):
        i = pl.program_id(2)
        res = fn(i, nt, *[r[0].astype(F32) for r in refs[:n_in]])
        if not isinstance(res, (tuple, list)):
            res = (res,)
        for o_ref, val in zip(refs[n_in:n_in + n_out], res[:n_out]):
            o_ref[0] = val.astype(o_ref.dtype)
        for a_ref, val in zip(refs[n_in + n_out:], res[n_out:]):
            @pl.when(i == 0)
            def _(a_ref=a_ref, val=val):
                a_ref[0] = val

            @pl.when(i > 0)
            def _(a_ref=a_ref, val=val):
                a_ref[0] += val

    res = pl.pallas_call(
        body, name=name, grid=(nb, ncol, nt), in_specs=in_specs, out_specs=out_specs, out_shape=out_shapes,
        compiler_params=_cparams(("parallel", "parallel", "arbitrary")),
    )(*args)
    return res


def _dot(a, b):
    return lax.dot_general(a, b, (((1,), (0,)), ((), ())), preferred_element_type=F32)


def _dot_nt(a, b):
    return lax.dot_general(a, b, (((1,), (1,)), ((), ())), preferred_element_type=F32)


def _dot_tn(a, b):
    return lax.dot_general(a, b, (((0,), (0,)), ((), ())), preferred_element_type=F32)


def _rows(parts):
    return jnp.concatenate(parts, axis=0)


def _lane_lo():
    return lax.broadcasted_iota(jnp.int32, (1, LANE), 1) < HEAD_DIM


def _blocks(a):
    return [a[i * BLK:(i + 1) * BLK] for i in range(a.shape[0] // BLK)]


def _band_geometry(t):
    rows = min(BAND_UNROLL, t // BLK) * BLK
    assert t % rows == 0
    return rows, t // rows


def _band_operands(g, k_ref, v_ref, rows):
    start = pl.multiple_of(g * rows, rows)
    pstart = pl.multiple_of(jnp.maximum(g * rows - BLK, 0), BLK)
    out = []
    for ref in (k_ref, v_ref):
        cur = ref[pl.ds(start, rows), :]
        raw = ref[pl.ds(pstart, rows), :]
        shifted = _rows([jnp.zeros((BLK, LANE), raw.dtype), raw[:rows - BLK]])
        out += [_blocks(cur), _blocks(jnp.where(g == 0, shifted, raw))]
    return out


def _band_scores(g, qa, kc, kp, b_ref, a, nb):
    u = len(qa)
    assert nb % u == 0 or u % nb == 0
    tile = lambda blk: _rows([blk] * u)
    firsts = []
    for i in range(u):
        if nb >= u:
            val = jnp.where(lax.rem(g * u, nb) == 0, NEG, 0.0).astype(F32) if i == 0 else 0.0
        else:
            val = NEG if i % nb == 0 else 0.0
        firsts.append(jnp.zeros((BLK, 1), F32) + val)
    sc = _rows([_dot_nt(q, k) for q, k in zip(qa, kc)]) + tile(b_ref[a, :, BLK:2 * BLK])
    sp = _rows([_dot_nt(q, k) for q, k in zip(qa, kp)]) + tile(b_ref[a, :, 0:BLK]) + _rows(firsts)
    return sc, sp


def _band_fwd(src, offs, bias, sinks, *, nb, gqa, name):
    t = src.shape[0]
    rows, nstep = _band_geometry(t)
    qo, ko, vo = offs
    share = 2 if gqa else 1

    def body(sink_ref, q_ref, k_ref, v_ref, b_ref, o_ref, lse_ref):
        hp, g = pl.program_id(0), pl.program_id(1)
        lo = _lane_lo()
        q2 = q_ref[...]
        kc, kp, vc, vp = _band_operands(g, k_ref, v_ref, rows)
        outs, lses = [], []
        for a in range(2):
            sink = sink_ref[2 * hp + a]
            qa = _blocks(jnp.where(lo if a == 0 else jnp.logical_not(lo), q2, jnp.zeros_like(q2)) * BAND_SCALE)
            sc, sp = _band_scores(g, qa, kc, kp, b_ref, a, nb)
            m = jnp.maximum(jnp.maximum(jnp.max(sc, axis=1, keepdims=True), jnp.max(sp, axis=1, keepdims=True)), sink)
            pc, pp = jnp.exp(sc - m), jnp.exp(sp - m)
            l = jnp.sum(pc, axis=1, keepdims=True) + jnp.sum(pp, axis=1, keepdims=True) + jnp.exp(sink - m)
            inv = 1.0 / l
            pc_b, pp_b = _blocks((pc * inv).astype(BF16)), _blocks((pp * inv).astype(BF16))
            outs.append(_rows([_dot(pc_b[i], vc[i]) + _dot(pp_b[i], vp[i]) for i in range(len(qa))]))
            lses.append(m + jnp.log(l))
        o_ref[...] = jnp.where(lo, outs[0], outs[1])
        lse_ref[...] = jnp.where(lo, lses[0], lses[1])

    slab = lambda off: pl.BlockSpec((rows, LANE), lambda hp, g: (g, off + hp))
    whole = lambda off: pl.BlockSpec((t, LANE), lambda hp, g: (0, off + hp // share))
    return pl.pallas_call(
        body, name=name, grid=(N_HEADS // 2, nstep),
        in_specs=[pl.BlockSpec(memory_space=pltpu.SMEM), slab(qo), whole(ko), whole(vo),
                  pl.BlockSpec((2, BLK, 2 * BLK), lambda hp, g: (hp, 0, 0))],
        out_specs=[slab(0), slab(0)],
        out_shape=[jax.ShapeDtypeStruct((t, N_HEADS * HEAD_DIM), F32)] * 2,
        compiler_params=_cparams(("parallel", "parallel")),
    )(sinks, src, src, src, bias)


def _band_bwd(src, offs, do, lse, dpr, bias, sinks, *, nb, gqa, name):
    t = src.shape[0]
    rows, nstep = _band_geometry(t)
    qo, ko, vo = offs
    share = 2 if gqa else 1

    def fold(a):
        acc = a[0:BLK]
        for i in range(1, rows // BLK):
            acc = acc + a[i * BLK:(i + 1) * BLK]
        return acc

    def body(sink_ref, q_ref, k_ref, v_ref, do_ref, lse_ref, dpr_ref, b_ref,
             dq_ref, dk_ref, dv_ref, ds_ref, dsink_ref, dk_acc, dv_acc):
        hp, g = pl.program_id(0), pl.program_id(1)
        lo = _lane_lo()
        hi = jnp.logical_not(lo)

        @pl.when(jnp.logical_and(g == 0, lax.rem(hp, share) == 0))
        def _():
            dk_acc[...] = jnp.zeros_like(dk_acc)
            dv_acc[...] = jnp.zeros_like(dv_acc)

        @pl.when(g == 0)
        def _():
            ds_ref[...] = jnp.zeros_like(ds_ref)
            dsink_ref[...] = jnp.zeros_like(dsink_ref)

        q2, do2, lse2, dpr2 = q_ref[...], do_ref[...], lse_ref[...], dpr_ref[...]
        lse_sw, dpr_sw = pltpu.roll(lse2, HEAD_DIM, axis=1), pltpu.roll(dpr2, HEAD_DIM, axis=1)
        kc, kp, vc, vp = _band_operands(g, k_ref, v_ref, rows)
        dqs, dk_cur, dk_prev, dv_cur, dv_prev = [], None, None, None, None
        for a in range(2):
            sink = sink_ref[2 * hp + a]
            mine = lo if a == 0 else hi
            qa = _blocks(jnp.where(mine, q2, jnp.zeros_like(q2)) * BAND_SCALE)
            doa = _blocks(jnp.where(mine, do2, jnp.zeros_like(do2)))
            lse_a, dpr_a = jnp.where(mine, lse2, lse_sw), jnp.where(mine, dpr2, dpr_sw)
            sc, sp = _band_scores(g, qa, kc, kp, b_ref, a, nb)
            pc, pp = jnp.exp(sc - lse_a), jnp.exp(sp - lse_a)
            dsc = pc * (_rows([_dot_nt(d, v) for d, v in zip(doa, vc)]) - dpr_a)
            dsp = pp * (_rows([_dot_nt(d, v) for d, v in zip(doa, vp)]) - dpr_a)
            ds_ref[a, :, BLK:2 * BLK] += fold(dsc)
            ds_ref[a, :, 0:BLK] += fold(dsp)
            dsink_ref[a] -= jnp.sum(jnp.exp(sink - lse_a) * dpr_a, axis=0, keepdims=True)
            dsc_b, dsp_b = _blocks(dsc.astype(BF16)), _blocks(dsp.astype(BF16))
            pc_b, pp_b = _blocks(pc.astype(BF16)), _blocks(pp.astype(BF16))
            dqs.append(_rows([_dot(dsc_b[i], kc[i]) + _dot(dsp_b[i], kp[i]) for i in range(len(qa))]))
            parts = [_rows([_dot_tn(x[i], y[i]) for i in range(len(qa))])
                     for x, y in ((dsc_b, qa), (dsp_b, qa), (pc_b, doa), (pp_b, doa))]
            if a == 0:
                dk_cur, dk_prev, dv_cur, dv_prev = parts
            else:
                dk_cur, dk_prev, dv_cur, dv_prev = dk_cur + parts[0], dk_prev + parts[1], dv_cur + parts[2], dv_prev + parts[3]
        dq_ref[...] = (jnp.where(lo, dqs[0], dqs[1]) * BAND_SCALE).astype(dq_ref.dtype)
        start = pl.multiple_of(g * rows, rows)
        after = pl.multiple_of(g * rows + BLK, BLK)
        dk_acc[pl.ds(after, rows), :] += dk_cur
        dk_acc[pl.ds(start, rows), :] += dk_prev
        dv_acc[pl.ds(after, rows), :] += dv_cur
        dv_acc[pl.ds(start, rows), :] += dv_prev

        @pl.when(g == nstep - 1)
        def _():
            dk_ref[...] = dk_acc[BLK:, :].astype(dk_ref.dtype)
            dv_ref[...] = dv_acc[BLK:, :].astype(dv_ref.dtype)

    slab = lambda off: pl.BlockSpec((rows, LANE), lambda hp, g: (g, off + hp))
    whole = lambda off: pl.BlockSpec((t, LANE), lambda hp, g: (0, off + hp // share))
    per_pair = lambda shp: pl.BlockSpec((2,) + shp, lambda hp, g: (hp,) + (0,) * len(shp))
    kv_cols = N_HEADS * HEAD_DIM // share
    return pl.pallas_call(
        body, name=name, grid=(N_HEADS // 2, nstep),
        in_specs=[pl.BlockSpec(memory_space=pltpu.SMEM), slab(qo), whole(ko), whole(vo), slab(0), slab(0), slab(0),
                  per_pair((BLK, 2 * BLK))],
        out_specs=[slab(0), whole(0), whole(0), per_pair((BLK, 2 * BLK)), per_pair((1, LANE))],
        out_shape=[jax.ShapeDtypeStruct((t, N_HEADS * HEAD_DIM), BF16), jax.ShapeDtypeStruct((t, kv_cols), BF16),
                   jax.ShapeDtypeStruct((t, kv_cols), BF16), jax.ShapeDtypeStruct((N_HEADS, BLK, 2 * BLK), F32),
                   jax.ShapeDtypeStruct((N_HEADS, 1, LANE), F32)],
        scratch_shapes=[pltpu.VMEM((t + BLK, LANE), F32), pltpu.VMEM((t + BLK, LANE), F32)],
        compiler_params=_cparams(("arbitrary", "arbitrary")),
    )(sinks, src, src, src, do, lse, dpr, bias)


MLA_V_OFF = N_HEADS


def _diag_mask(keys_on_rows=False):
    rows, cols = lax.broadcasted_iota(jnp.int32, (TQ, TQ), 0), lax.broadcasted_iota(jnp.int32, (TQ, TQ), 1)
    return rows <= cols if keys_on_rows else cols <= rows


def _mla_specs(t):
    blk = lambda f: pl.BlockSpec((TQ, LANE), lambda hp, qi, f=f: (qi, f(hp)))
    whole = lambda f: pl.BlockSpec((t, LANE), lambda hp, qi, f=f: (0, f(hp)))
    return blk, whole


def _mla_fwd(q, k, vt, *, name):
    t = q.shape[0]
    n = t // TQ
    scale = (C_NOPE + C_ROPE) ** -0.5

    def body(q0_ref, q1_ref, k0_ref, k1_ref, vt_ref, o_ref, lse_ref, m_ref, acc_ref):
        qi = pl.program_id(1)
        qs, ks = (q0_ref[...], q1_ref[...]), (k0_ref, k1_ref)
        m_ref[...] = jnp.full_like(m_ref, NEG)
        acc_ref[...] = jnp.zeros_like(acc_ref)
        first = lax.broadcasted_iota(jnp.int32, (LANE, 1), 0) < HEAD_DIM

        def step(kj, diagonal):
            rows = pl.ds(pl.multiple_of(kj * TQ, TQ), TQ)
            vtb = vt_ref[0, kj]
            one = jnp.ones_like(vtb)
            vts = (jnp.where(first, vtb, one), jnp.where(first, one, vtb))
            for a in range(2):
                s = _dot_nt(ks[a][rows, :], qs[a]) * (scale * LOG2E)
                if diagonal:
                    s = jnp.where(_diag_mask(keys_on_rows=True), s, NEG)
                m_prev = m_ref[a]
                m_new = jnp.maximum(m_prev, jnp.max(s, axis=0, keepdims=True))
                acc_ref[a] = jnp.exp2(m_prev - m_new) * acc_ref[a] + _dot(vts[a], jnp.exp2(s - m_new).astype(BF16))
                m_ref[a] = m_new

        def kloop(kj, c2):
            step(kj, False)
            return c2

        lax.fori_loop(0, qi, kloop, 0)
        step(qi, True)
        l0, l1 = acc_ref[0, HEAD_DIM:HEAD_DIM + 1, :], acc_ref[1, 0:1, :]
        ot = jnp.where(first, acc_ref[0] * (1.0 / l0), acc_ref[1] * (1.0 / l1))
        lset = jnp.where(first, m_ref[0] * LN2 + jnp.log(l0), m_ref[1] * LN2 + jnp.log(l1))
        o_ref[...] = ot.T
        lse_ref[...] = lset.T

    blk, whole = _mla_specs(t)
    return pl.pallas_call(
        body, name=name, grid=(N_HEADS // 2, n),
        in_specs=[blk(lambda hp: 2 * hp), blk(lambda hp: 2 * hp + 1), whole(lambda hp: 2 * hp), whole(lambda hp: 2 * hp + 1),
                  pl.BlockSpec((1, n, LANE, TQ), lambda hp, qi: (hp, 0, 0, 0))],
        out_specs=[blk(lambda hp: hp), blk(lambda hp: hp)],
        out_shape=[jax.ShapeDtypeStruct((t, N_HEADS * HEAD_DIM), F32)] * 2,
        scratch_shapes=[pltpu.VMEM((2, 1, TQ), F32), pltpu.VMEM((2, LANE, TQ), F32)],
        compiler_params=_cparams(("parallel", "parallel")),
    )(q, q, k, k, vt)


def _mla_bwd(q, k, kv, do, lse, delta, *, name):
    t = q.shape[0]
    n = t // TQ
    scale = (C_NOPE + C_ROPE) ** -0.5

    def body(q0_ref, q1_ref, k0_ref, k1_ref, v_ref, do_ref, lse_ref, dl_ref,
             dq_ref, dk_ref, dv_ref, dq_acc, dk_acc, dv_acc):
        qi = pl.program_id(1)
        lo = _lane_lo()

        @pl.when(qi == 0)
        def _():
            dk_acc[...] = jnp.zeros_like(dk_acc)
            dv_acc[...] = jnp.zeros_like(dv_acc)

        dq_acc[...] = jnp.zeros_like(dq_acc)
        qs, ks = (q0_ref[...], q1_ref[...]), (k0_ref, k1_ref)
        do2, lse2, dl2 = do_ref[...], lse_ref[...], dl_ref[...]
        lse_sw, dl_sw = pltpu.roll(lse2, HEAD_DIM, axis=1), pltpu.roll(dl2, HEAD_DIM, axis=1)
        heads = []
        for a in range(2):
            mine = lo if a == 0 else jnp.logical_not(lo)
            heads.append((jnp.where(mine, do2, jnp.zeros_like(do2)), jnp.where(mine, lse2, lse_sw)[:, 0:1] * LOG2E,
                          jnp.where(mine, dl2, dl_sw)[:, 0:1]))

        def step(kj, diagonal):
            rows = pl.ds(pl.multiple_of(kj * TQ, TQ), TQ)
            vb = v_ref[rows, :]
            for a, (doa, lse_a, dl_a) in enumerate(heads):
                kb = ks[a][rows, :]
                s = _dot_nt(qs[a], kb) * (scale * LOG2E)
                if diagonal:
                    s = jnp.where(_diag_mask(), s, NEG)
                p = jnp.exp2(s - lse_a)
                ds = (p * (_dot_nt(doa, vb) - dl_a)).astype(BF16)
                dq_acc[a] += _dot(ds, kb)
                dk_acc[a, rows, :] += _dot_tn(ds, qs[a])
                dv_acc[rows, :] += _dot_tn(p.astype(BF16), doa)

        def kloop(kj, c2):
            step(kj, False)
            return c2

        lax.fori_loop(0, qi, kloop, 0)
        step(qi, True)
        dq_ref[:, 0:LANE] = (dq_acc[0] * scale).astype(dq_ref.dtype)
        dq_ref[:, LANE:2 * LANE] = (dq_acc[1] * scale).astype(dq_ref.dtype)

        @pl.when(qi == n - 1)
        def _():
            dk_ref[:, 0:LANE] = (dk_acc[0] * scale).astype(dk_ref.dtype)
            dk_ref[:, LANE:2 * LANE] = (dk_acc[1] * scale).astype(dk_ref.dtype)
            dv_ref[...] = dv_acc[...].astype(dv_ref.dtype)

    blk, whole = _mla_specs(t)
    even, odd, pair = (lambda hp: 2 * hp), (lambda hp: 2 * hp + 1), (lambda hp: hp)
    wide = jax.ShapeDtypeStruct((t, N_HEADS * LANE), BF16)
    return pl.pallas_call(
        body, name=name, grid=(N_HEADS // 2, n),
        in_specs=[blk(even), blk(odd), whole(even), whole(odd), whole(lambda hp: MLA_V_OFF + hp), blk(pair), blk(pair), blk(pair)],
        out_specs=[pl.BlockSpec((TQ, 2 * LANE), lambda hp, qi: (qi, hp)), pl.BlockSpec((t, 2 * LANE), lambda hp, qi: (0, hp)), whole(pair)],
        out_shape=[wide, wide, jax.ShapeDtypeStruct((t, N_HEADS * HEAD_DIM), BF16)],
        scratch_shapes=[pltpu.VMEM((2, TQ, LANE), F32), pltpu.VMEM((2, t, LANE), F32), pltpu.VMEM((t, LANE), F32)],
        compiler_params=_cparams(("arbitrary", "arbitrary")),
    )(q, q, k, k, kv, do, lse, delta)


def _bias_lookup(bucket, table_t, *, name):
    nh, npos = bucket.shape
    tp = 4096

    def body(b_ref, t_ref, o_ref):
        bk, tab = b_ref[...], t_ref[...]
        acc = jnp.zeros(bk.shape, F32)
        for i in range(REL_BUCKETS):
            acc = jnp.where(bk == i, tab[:, i:i + 1], acc)
        o_ref[...] = acc

    return pl.pallas_call(
        body, name=name, grid=(npos // tp,),
        in_specs=[pl.BlockSpec((nh, tp), lambda i: (0, i)), pl.BlockSpec((nh, REL_BUCKETS), lambda i: (0, 0))],
        out_specs=pl.BlockSpec((nh, tp), lambda i: (0, i)),
        out_shape=jax.ShapeDtypeStruct((nh, npos), F32),
        compiler_params=_cparams(("parallel",)),
    )(bucket, table_t)


def _bias_grad(bucket, ds0, ds1, *, name):
    nh, npos = bucket.shape
    tp = 4096

    def body(b_ref, a_ref, c_ref, o_ref):
        i = pl.program_id(0)
        bk, ds = b_ref[...], a_ref[...] + c_ref[...]
        lane = lax.broadcasted_iota(jnp.int32, (nh, REL_BUCKETS), 1)
        acc = jnp.zeros((nh, REL_BUCKETS), F32)
        for j in range(REL_BUCKETS):
            col = jnp.sum(jnp.where(bk == j, ds, 0.0), axis=1, keepdims=True)
            acc = acc + jnp.where(lane == j, col, 0.0)

        @pl.when(i == 0)
        def _():
            o_ref[...] = acc

        @pl.when(i > 0)
        def _():
            o_ref[...] += acc

    return pl.pallas_call(
        body, name=name, grid=(npos // tp,),
        in_specs=[pl.BlockSpec((nh, tp), lambda i: (0, i))] * 3,
        out_specs=pl.BlockSpec((nh, REL_BUCKETS), lambda i: (0, 0)),
        out_shape=jax.ShapeDtypeStruct((nh, REL_BUCKETS), F32),
        compiler_params=_cparams(("arbitrary",)),
    )(bucket, ds0, ds1)


def _t5_bucket(dist):
    n = jnp.maximum(dist, 0)
    max_exact = REL_BUCKETS // 2
    scaled = jnp.log(jnp.maximum(n, 1).astype(F32) / max_exact) / math.log(REL_MAX_DIST / max_exact)
    large = max_exact + (scaled * (REL_BUCKETS - max_exact)).astype(jnp.int32)
    return jnp.where(n < max_exact, n, jnp.minimum(large, REL_BUCKETS - 1))


def _bucket_index():
    qi = jnp.arange(BLK)[:, None]
    ci = jnp.arange(2 * BLK)[None, :]
    step = BLK + qi - ci
    per_group = [_t5_bucket(step * d).reshape(1, -1) for d in A_DILS + (1,)]
    return jnp.concatenate([jnp.tile(b, (N_HEADS, 1)) for b in per_group], axis=0).astype(jnp.int32)


def _sigmoid(x):
    return 1.0 / (1.0 + jnp.exp(-x))


def _ln_stats(z):
    mu = jnp.mean(z, axis=-1, keepdims=True)
    zc = z - mu
    var = jnp.mean(zc * zc, axis=-1, keepdims=True)
    return zc * lax.rsqrt(var + LN_EPS)


def _ln_fwd(x, mix, g, b, *, name):
    def fn(i, nt, xv, mv, gv, bv):
        z = ALPHA * xv + mv
        y = _ln_stats(z) * gv + bv
        return y, y, z

    c = x.shape[-1]
    y, yb, z = _rowwise(fn, [(x[None], c, 0), (mix[None], c, 0)], pars=[(g.reshape(1, 1, c), c, 0), (b.reshape(1, 1, c), c, 0)],
                        outs=[(c, c, 0, F32), (c, c, 0, BF16), (c, c, 0, F32)], tm=512, name=name)
    return y[0], yb[0], z[0]


def _ln_bwd(z, g, dys, coefs, *, name):
    n = len(dys)

    def fn(i, nt, zv, *rest):
        gv = rest[n]
        dy = coefs[0] * rest[0]
        for cf, t in zip(coefs[1:], rest[1:n]):
            dy = dy + cf * t
        mu = jnp.mean(zv, axis=-1, keepdims=True)
        zc = zv - mu
        r = lax.rsqrt(jnp.mean(zc * zc, axis=-1, keepdims=True) + LN_EPS)
        xh = zc * r
        dxh = dy * gv
        dz = r * (dxh - jnp.mean(dxh, axis=-1, keepdims=True) - xh * jnp.mean(dxh * xh, axis=-1, keepdims=True))
        return dz, dz, jnp.sum(dy * xh, axis=0, keepdims=True), jnp.sum(dy, axis=0, keepdims=True)

    c = z.shape[-1]
    dz, dzb, dg, db = _rowwise(fn, [(z[None], c, 0)] + [(d[None], c, 0) for d in dys], pars=[(g.reshape(1, 1, c), c, 0)],
                               outs=[(c, c, 0, F32), (c, c, 0, BF16)], accs=[(1, c, c, 0), (1, c, c, 0)], tm=512, name=name)
    return dz[0], dzb[0], dg.reshape(c), db.reshape(c)


def _rms_fwd(src, c, off, g, *, name):
    def fn(i, nt, xv, gv):
        return xv * lax.rsqrt(jnp.mean(xv * xv, axis=-1, keepdims=True) + RMS_EPS) * gv

    return _rowwise(fn, [(src[None], c, off)], pars=[(g.reshape(1, 1, c), c, 0)], outs=[(c, c, 0, BF16)], tm=1024, name=name)[0][0]


def _rms_bwd(src, c, off, g, dy, *, name):
    def fn(i, nt, xv, dyv, gv):
        r = lax.rsqrt(jnp.mean(xv * xv, axis=-1, keepdims=True) + RMS_EPS)
        gd = gv * dyv
        dx = gd * r - xv * (r * r * r) * jnp.mean(gd * xv, axis=-1, keepdims=True)
        return dx, jnp.sum(dyv * xv * r, axis=0, keepdims=True)

    dx, dg = _rowwise(fn, [(src[None], c, off), (dy[None], c, 0)], pars=[(g.reshape(1, 1, c), c, 0)],
                      outs=[(c, c, 0, BF16)], accs=[(1, c, c, 0)], tm=1024, name=name)
    return dx[0], dg.reshape(c)


def _rope_slabs(x, n_slab, c, s, *, add=None, to_front=False, name):
    half = C_ROPE // 2

    def fn(i, nt, xv, cv, sv, *rest):
        lane = lax.broadcasted_iota(jnp.int32, (1, LANE), 1)
        extra = pltpu.roll(rest[0], C_NOPE, axis=1) if rest else None
        outs = []
        for h in range(n_slab):
            xs = xv[:, h * LANE:(h + 1) * LANE]
            if extra is not None:
                xs = xs + extra
            swapped = jnp.where(lane < C_NOPE + half, pltpu.roll(xs, LANE - half, axis=1), pltpu.roll(xs, half, axis=1))
            y = xs * cv + swapped * sv
            if to_front:
                y = jnp.where(lane < C_ROPE, pltpu.roll(y, LANE - C_NOPE, axis=1), 0.0)
            outs.append(y)
        return jnp.concatenate(outs, axis=1) if n_slab > 1 else outs[0]

    w = n_slab * LANE
    rows = [(x[None], w, 0), (c[None], LANE, 0), (s[None], LANE, 0)]
    if add is not None:
        rows.append((add[0][None], LANE, add[1]))
    return _rowwise(fn, rows, outs=[(w, w, 0, BF16)], tm=512, name=name)[0][0]


def _merge_fwd(proj, b_gate, ys, *, name):
    def fn(i, nt, g0, g1, g2, ya, yb, yc, bg):
        return (_sigmoid(g0 + bg[:, 0:1024]) * ya + _sigmoid(g1 + bg[:, 1024:2048]) * yb
                + _sigmoid(g2 + bg[:, 2048:3072]) * yc)

    rows = [(proj[None], 1024, j) for j in range(3)] + [(y[None], 1024, 0) for y in ys]
    return _rowwise(fn, rows, pars=[(b_gate.reshape(1, 1, 3072), 3072, 0)], outs=[(1024, 1024, 0, BF16)], tm=512, name=name)[0][0]


def _merge_bwd(proj, b_gate, ys, dm, *, name):
    def fn(i, nt, g0, g1, g2, ya, yb, yc, dmv, bg):
        outs, dgs = [], []
        for j, (gp, y) in enumerate(((g0, ya), (g1, yb), (g2, yc))):
            s = _sigmoid(gp + bg[:, j * 1024:(j + 1) * 1024])
            outs.append(s * dmv)
            dgs.append(dmv * y * s * (1.0 - s))
        return outs + dgs + [jnp.sum(d, axis=0, keepdims=True) for d in dgs]

    rows = [(proj[None], 1024, j) for j in range(3)] + [(y[None], 1024, 0) for y in ys] + [(dm[None], 1024, 0)]
    res = _rowwise(fn, rows, pars=[(b_gate.reshape(1, 1, 3072), 3072, 0)], outs=[(1024, 1024, 0, BF16)] * 6,
                   accs=[(1, 1024, 1024, 0)] * 3, tm=256, name=name)
    dys = [r[0] for r in res[0:3]]
    dgp = [r[0] for r in res[3:6]]
    dbg = jnp.concatenate([r.reshape(1024) for r in res[6:9]])
    return dys, dgp, dbg


def _shift_down(u, halo, i, k):
    ext = jnp.concatenate([jnp.where(i > 0, halo, 0.0), u], axis=0)
    return pltpu.roll(ext, k, axis=0)[HALO:]


def _shift_up(u, halo, i, nt, k):
    ext = jnp.concatenate([u, jnp.where(i < nt - 1, halo, 0.0)], axis=0)
    n = ext.shape[0]
    return pltpu.roll(ext, n - k, axis=0)[:n - HALO]


GLU_C = D_FF // 2


def _conv(u, halo, i, w, b):
    return w[0:1] * _shift_down(u, halo, i, 2) + w[1:2] * _shift_down(u, halo, i, 1) + w[2:3] * u + b


def _glu_fwd(ug, uv, conv_w, conv_b, *, name):
    def fn(i, nt, g, v, hg, hv, wg, wv, bg, bv):
        cg, cv = _conv(g, hg, i, wg, bg), _conv(v, hv, i, wv, bv)
        return cg * _sigmoid(cg) * cv

    w3, b3 = conv_w[None], conv_b.reshape(1, 1, -1)
    c = GLU_C
    return _rowwise(fn, [(ug[None], c, 0), (uv[None], c, 0)], halos=[(ug[None], c, 0, "prev"), (uv[None], c, 0, "prev")],
                    pars=[(w3, c, 0), (w3, c, 2), (b3, c, 0), (b3, c, 2)], outs=[(D_FF, c, 0, BF16)], tm=256, ncol=2, name=name)[0][0]


def _glu_bwd_a(ug, uv, conv_w, conv_b, dh, *, name):
    def fn(i, nt, g, v, dhv, hg, hv, wg, wv, bg, bv):
        g1, g2 = _shift_down(g, hg, i, 1), _shift_down(g, hg, i, 2)
        v1, v2 = _shift_down(v, hv, i, 1), _shift_down(v, hv, i, 2)
        cg = wg[0:1] * g2 + wg[1:2] * g1 + wg[2:3] * g + bg
        cv = wv[0:1] * v2 + wv[1:2] * v1 + wv[2:3] * v + bv
        s = _sigmoid(cg)
        dcv = dhv * cg * s
        dcg = dhv * cv * (s * (1.0 + cg * (1.0 - s)))
        red = lambda a: jnp.sum(a, axis=0, keepdims=True)
        return (dcg, dcv, red(dcg), red(dcv), red(dcg * g2), red(dcg * g1), red(dcg * g),
                red(dcv * v2), red(dcv * v1), red(dcv * v))

    w3, b3 = conv_w[None], conv_b.reshape(1, 1, -1)
    c = GLU_C
    res = _rowwise(fn, [(ug[None], c, 0), (uv[None], c, 0), (dh[None], c, 0)],
                   halos=[(ug[None], c, 0, "prev"), (uv[None], c, 0, "prev")],
                   pars=[(w3, c, 0), (w3, c, 2), (b3, c, 0), (b3, c, 2)],
                   outs=[(D_FF, c, 0, BF16), (D_FF, c, 0, BF16)], accs=[(1, D_FF, c, 0)] * 8, tm=256, ncol=2, name=name)
    dcg, dcv = res[0][0], res[1][0]
    dconv_b = jnp.concatenate([res[2].reshape(D_FF), res[3].reshape(D_FF)])
    dconv_w = jnp.concatenate([jnp.concatenate([res[4 + j].reshape(1, D_FF) for j in range(3)], axis=0),
                               jnp.concatenate([res[7 + j].reshape(1, D_FF) for j in range(3)], axis=0)], axis=1)
    return dcg, dcv, dconv_w, dconv_b


def _glu_bwd_b(dc, conv_w, half, *, name):
    def fn(i, nt, d, hd, w):
        return w[2:3] * d + w[1:2] * _shift_up(d, hd, i, nt, 1) + w[0:1] * _shift_up(d, hd, i, nt, 2)

    c = GLU_C
    return _rowwise(fn, [(dc[None], c, 0)], halos=[(dc[None], c, 0, "next")], pars=[(conv_w[None], c, 2 * half)],
                    outs=[(D_FF, c, 0, BF16)], tm=256, ncol=2, name=name)[0][0]


def _loss_and_grad(y, tgt, *, name):
    def fn(i, nt, yv, tv):
        err = yv - tv
        part = jnp.sum(jnp.sum(err * err, axis=0, keepdims=True), axis=1, keepdims=True) * (0.5 / D_MODEL)
        return err * (1.0 / D_MODEL), jnp.zeros((1, LANE), F32) + part

    dy, part = _rowwise(fn, [(y[None], D_MODEL, 0), (tgt[None], D_MODEL, 0)], outs=[(D_MODEL, D_MODEL, 0, F32)],
                        accs=[(1, LANE, LANE, 0)], tm=512, name=name)
    return dy[0], part.reshape(LANE)


def _lincomb(terms, coefs, *, name):
    def fn(i, nt, *vs):
        acc = coefs[0] * vs[0]
        for cf, v in zip(coefs[1:], vs[1:]):
            acc = acc + cf * v
        return acc

    c = terms[0].shape[-1]
    return _rowwise(fn, [(a[None], c, 0) for a in terms], outs=[(c, c, 0, F32)], tm=512, name=name)[0][0]


def _sum_rows(terms, *, tm, name, dtype=F32):
    def fn(i, nt, *vs):
        acc = vs[0]
        for v in vs[1:]:
            acc = acc + v
        return acc

    c = terms[0].shape[-1]
    return _rowwise(fn, [(t, c, 0) for t in terms], outs=[(c, c, 0, dtype)], tm=tm, name=name)[0]


def _head_sums(x):
    lo = _lane_lo()
    parts = []
    for j in range(x.shape[1] // LANE):
        blk = x[:, j * LANE:(j + 1) * LANE]
        s_lo = jnp.sum(jnp.where(lo, blk, 0.0), axis=1, keepdims=True)
        s_hi = jnp.sum(jnp.where(lo, 0.0, blk), axis=1, keepdims=True)
        parts.append(jnp.where(lo, s_lo, s_hi))
    return jnp.concatenate(parts, axis=1)


def _group_weights(l0, l1, l2):
    m = jnp.maximum(jnp.maximum(l0, l1), l2)
    es = [jnp.exp(l - m) for l in (l0, l1, l2)]
    inv = 1.0 / (es[0] + es[1] + es[2])
    return [e * inv for e in es]


def _combine_fwd(os_, lses, *, name):
    def fn(i, nt, o0, o1, o2, l0, l1, l2):
        w = _group_weights(l0, l1, l2)
        return w[0] * o0 + w[1] * o1 + w[2] * o2

    c = os_[0].shape[-1]
    return _rowwise(fn, [(a[None], c, 0) for a in list(os_) + list(lses)], outs=[(c, c, 0, BF16)], tm=512, name=name)[0][0]


def _combine_bwd(os_, lses, do_a, *, name):
    def fn(i, nt, o0, o1, o2, l0, l1, l2, da):
        ws = _group_weights(l0, l1, l2)
        dws = [_head_sums(da * o) for o in (o0, o1, o2)]
        mean = ws[0] * dws[0] + ws[1] * dws[1] + ws[2] * dws[2]
        return [w * da for w in ws] + [w * mean for w in ws]

    c = do_a.shape[-1]
    res = _rowwise(fn, [(a[None], c, 0) for a in list(os_) + list(lses) + [do_a]], outs=[(c, c, 0, BF16)] * 3 + [(c, c, 0, F32)] * 3,
                   tm=256, name=name)
    return [r[0] for r in res[0:3]], [r[0] for r in res[3:6]]


def _delta(do, o, *, name):
    def fn(i, nt, d, ov):
        return d, _head_sums(d * ov)

    c = do.shape[-1]
    res = _rowwise(fn, [(do[None], c, 0), (o[None], c, 0)], outs=[(c, c, 0, BF16), (c, c, 0, F32)], tm=512, name=name)
    return res[0][0], res[1][0]


def _adamw(w, g, m, v, *, name):
    c1 = 1.0 - ADAM_B1 ** ADAM_STEP
    c2 = 1.0 - ADAM_B2 ** ADAM_STEP

    def fn(i, nt, wv, gv, mv, vv):
        mn = ADAM_B1 * mv + (1.0 - ADAM_B1) * gv
        vn = ADAM_B2 * vv + (1.0 - ADAM_B2) * (gv * gv)
        delta = -ADAM_LR * ((mn / c1) / (jnp.sqrt(vn / c2) + ADAM_EPS) + ADAM_WD * wv)
        return delta, mn, vn

    r, c = w.shape
    rp = _ceil_to(r, 8)
    pad = lambda a: jnp.pad(a, ((0, rp - r), (0, 0))) if rp != r else a
    tm = rp
    for cand in (128, 64, 32, 16, 8):
        if rp % cand == 0:
            tm = cand
            break
    res = _rowwise(fn, [(pad(a)[None], c, 0) for a in (w, g, m, v)], outs=[(c, c, 0, F32)] * 3, tm=tm, name=name)
    return [x[0][:r] for x in res]


ANY = pl.BlockSpec(memory_space=pl.ANY)


def _place():
    x, y, c = lax.axis_index("x"), lax.axis_index("y"), lax.axis_index("c")
    chips = [(1 - x, y), (x, 1 - y), (1 - x, 1 - y)]
    return x, y, c, chips


def _allgather_weights(arrs):
    n = len(arrs)

    def body(*refs):
        ins, outs, send_sems, recv_sems = refs[:n], refs[n:2 * n], refs[2 * n], refs[2 * n + 1]
        x, y, c, chips = _place()
        j = 2 * x + y
        me, sibling = (x, y, c), (x, y, 1 - c)

        def cp(i, k, src, chip_idx, half, to):
            return pltpu.make_async_remote_copy(src_ref=src, dst_ref=outs[i].at[chip_idx, half], send_sem=send_sems.at[k],
                                                recv_sem=recv_sems.at[k], device_id=to, device_id_type=MESH)

        first, passed, own = [], [], []
        for i in range(n):
            for r, (cx, cy) in enumerate(chips):
                first.append(cp(i, 3 * i + r, ins[i].at[c], j, c, (cx, cy, c)))
                passed.append(cp(i, 3 * (n + i) + r, outs[i].at[2 * cx + cy, c], 2 * cx + cy, c, sibling))
            own += [cp(i, 6 * n + 2 * i + half, ins[i].at[half], j, half, sibling) for half in range(2)]
        for d in first + own:
            d.start()
        for i in range(n):
            for r, (cx, cy) in enumerate(chips):
                cp(i, 3 * i + r, ins[i].at[c], 2 * cx + cy, c, me).wait_recv()
                passed[3 * i + r].start()
        for i in range(n):
            for r, (cx, cy) in enumerate(chips):
                cp(i, 3 * (n + i) + r, ins[i].at[c], 2 * cx + cy, 1 - c, me).wait_recv()
        for d in own:
            d.wait_recv()
        for d in first + passed + own:
            d.wait_send()

    return pl.pallas_call(
        body, name="allgather_weights", in_specs=[ANY] * n, out_specs=[ANY] * n,
        out_shape=[jax.ShapeDtypeStruct((N_CHIP,) + a.shape, a.dtype) for a in arrs],
        scratch_shapes=[pltpu.SemaphoreType.DMA((8 * n,)), pltpu.SemaphoreType.DMA((8 * n,))],
    )(*arrs)


def _sibling_swap(gs):
    n = len(gs)

    def body(*refs):
        layers, outs, send_sems, recv_sems = (refs[:n], refs[n:2 * n]), refs[2 * n:3 * n], refs[3 * n], refs[3 * n + 1]
        x, y, c, _ = _place()

        def copies(srcs):
            return [pltpu.make_async_remote_copy(src_ref=srcs[i], dst_ref=outs[i], send_sem=send_sems.at[i], recv_sem=recv_sems.at[i],
                                                 device_id=(x, y, 1 - c), device_id_type=MESH) for i in range(n)]

        for layer in range(DEPTH):
            @pl.when(c == 1 - layer)
            def _(layer=layer):
                for d in copies(layers[layer]):
                    d.start()

        waits = copies(layers[0])
        for d in waits:
            d.wait_recv()
        for d in waits:
            d.wait_send()

    return pl.pallas_call(
        body, name="grad_sibling_swap", in_specs=[ANY] * (2 * n), out_specs=[ANY] * n,
        out_shape=[jax.ShapeDtypeStruct(g0.shape, g0.dtype) for g0, _ in gs],
        scratch_shapes=[pltpu.SemaphoreType.DMA((n,)), pltpu.SemaphoreType.DMA((n,))],
    )(*[g0 for g0, _ in gs], *[g1 for _, g1 in gs])


def _chip_scatter(ps):
    n = len(ps)

    def body(*refs):
        ins, outs, send_sems, recv_sems = refs[:n], refs[n:2 * n], refs[2 * n], refs[2 * n + 1]
        x, y, c, chips = _place()
        sends = []
        for i in range(n):
            for r, (cx, cy) in enumerate(chips):
                sends.append(pltpu.make_async_remote_copy(src_ref=ins[i].at[2 * cx + cy], dst_ref=outs[i].at[r], send_sem=send_sems.at[3 * i + r],
                                                          recv_sem=recv_sems.at[3 * i + r], device_id=(cx, cy, c), device_id_type=MESH))
        for d in sends:
            d.start()
        for d in sends:
            d.wait_recv()
        for d in sends:
            d.wait_send()

    return pl.pallas_call(
        body, name="grad_chip_scatter", in_specs=[ANY] * n, out_specs=[ANY] * n,
        out_shape=[jax.ShapeDtypeStruct((3,) + p.shape[1:], p.dtype) for p in ps],
        scratch_shapes=[pltpu.SemaphoreType.DMA((3 * n,)), pltpu.SemaphoreType.DMA((3 * n,))],
    )(*ps)


def _sibling_share(rs):
    n = len(rs)

    def body(*refs):
        ins, outs, send_sems, recv_sems = refs[:n], refs[n:2 * n], refs[2 * n], refs[2 * n + 1]
        x, y, c, _ = _place()
        cps = [pltpu.make_async_remote_copy(src_ref=ins[i], dst_ref=outs[i], send_sem=send_sems.at[i], recv_sem=recv_sems.at[i],
                                            device_id=(x, y, 1 - c), device_id_type=MESH) for i in range(n)]
        for d in cps:
            d.start()
        for d in cps:
            d.wait_recv()
        for d in cps:
            d.wait_send()

    return pl.pallas_call(
        body, name="grad_sibling_share", in_specs=[ANY] * n, out_specs=[ANY] * n,
        out_shape=[jax.ShapeDtypeStruct(r.shape, r.dtype) for r in rs],
        scratch_shapes=[pltpu.SemaphoreType.DMA((n,)), pltpu.SemaphoreType.DMA((n,))],
    )(*rs)


def _allreduce_small(s):
    rows, w = s.shape
    n_dev = 8

    def body(s_ref, out_ref, slots, send_sems, recv_sems):
        x, y, c, _ = _place()
        me = 4 * x + 2 * y + c
        slots[me] = s_ref[...]
        peers = []
        for r in range(1, n_dev):
            px = 1 - x if r & 4 else x
            py = 1 - y if r & 2 else y
            pc = 1 - c if r & 1 else c
            peers.append((px, py, pc))
        sends = [pltpu.make_async_remote_copy(src_ref=s_ref, dst_ref=slots.at[me], send_sem=send_sems.at[r], recv_sem=recv_sems.at[r],
                                              device_id=peer, device_id_type=MESH) for r, peer in enumerate(peers)]
        for d in sends:
            d.start()
        for r, (px, py, pc) in enumerate(peers):
            pltpu.make_async_remote_copy(src_ref=s_ref, dst_ref=slots.at[4 * px + 2 * py + pc], send_sem=send_sems.at[r],
                                         recv_sem=recv_sems.at[r], device_id=(x, y, c), device_id_type=MESH).wait_recv()
        for d in sends:
            d.wait_send()
        acc = slots[0]
        for k in range(1, n_dev):
            acc = acc + slots[k]
        out_ref[...] = acc

    vm = pl.BlockSpec(memory_space=pltpu.VMEM)
    return pl.pallas_call(
        body, name="allreduce_small", in_specs=[vm], out_specs=vm, out_shape=jax.ShapeDtypeStruct((rows, w), F32),
        scratch_shapes=[pltpu.VMEM((n_dev, rows, w), F32), pltpu.SemaphoreType.DMA((n_dev - 1,)), pltpu.SemaphoreType.DMA((n_dev - 1,))],
    )(s)


W_IN_SHARD = D_IN // N_CHIP
W_IN_ROWS_G = 2304
REDUCED = tuple(m for m in MATS if m[0] != "conv_w")
CONV_W_SIZE = 3 * 2 * D_FF


def _weight_send(name, a):
    if name == "w_in":
        return jnp.swapaxes(a, 1, 2).astype(BF16)
    return a if name == "conv_w" else a.astype(BF16)


def _full_weights(gathered, l):
    g = {k: v[:, l] for k, v in gathered.items()}
    s = g["w_in"].astype(F32).reshape(D_IN, D_MODEL)
    dup = lambda a: jnp.concatenate([a[0:64], a[0:64], a[64:128], a[64:128]], axis=0)
    o = ORIG
    wm_t = jnp.concatenate([s[o["gate"]:], s[o["a"]:o["a"] + A_COLS], s[o["bq"]:o["bk"]], dup(s[o["bk"]:o["bv"]]), dup(s[o["bv"]:o["cq"]]),
                            s[o["cq"]:o["gate"]], jnp.zeros((M_COLS - M_CDKV - (o["gate"] - o["cdkv"]), D_MODEL), F32)], axis=0).astype(BF16)
    wg_t = [s[o["a"] + gi * A_COLS:o["a"] + (gi + 1) * A_COLS].astype(BF16) for gi in (1, 2)]
    full = {name: jnp.moveaxis(g[name], 0, ax).reshape(shape) for name, shape, ax in MATS if name != "w_in"}
    uq = full["w_uq"].reshape(C_Q_RANK, N_HEADS, C_NOPE + C_ROPE)
    ukv = full["w_ukv"].reshape(C_KV_RANK, N_HEADS, 2 * C_NOPE)
    w_uq_p = _pad_lanes(uq).reshape(C_Q_RANK, N_HEADS * LANE)
    w_ukv_p = jnp.concatenate([_pad_lanes(ukv[:, :, :C_NOPE]).reshape(C_KV_RANK, N_HEADS * LANE),
                               ukv[:, :, C_NOPE:].reshape(C_KV_RANK, N_HEADS * HEAD_DIM)], axis=1)
    return {"wm_t": wm_t, "wg_t": wg_t, "w_uq_p": w_uq_p, "w_ukv_p": w_ukv_p, "w_branch": full["w_branch"], "w_out": full["w_out"],
            "wup_g": full["w_ffn_up"][:, :D_FF], "wup_v": full["w_ffn_up"][:, D_FF:], "conv_w": full["conv_w"],
            "w_ffn_down": full["w_ffn_down"]}


def _grad_send(name, g, shape, ax):
    if name == "w_in":
        return jnp.pad(g.reshape(N_CHIP, W_IN_SHARD, D_MODEL), ((0, 0), (0, W_IN_ROWS_G - W_IN_SHARD), (0, 0)))
    split = shape[:ax] + (N_CHIP, shape[ax] // N_CHIP) + shape[ax + 1:]
    return jnp.moveaxis(g.reshape(split), ax, 0)


def _grad_recv(name, r):
    return r[:W_IN_SHARD].T if name == "w_in" else r


def _pack_small(rel, small, conv_w, extra):
    parts = [rel.reshape(-1)]
    for l in range(DEPTH):
        for name in SMALL:
            parts.append(small[name][l].reshape(-1))
    parts += [conv_w.reshape(-1), extra]
    flat = jnp.concatenate(parts)
    rows = _ceil_to(-(-flat.shape[0] // LANE), 8)
    return jnp.pad(flat, (0, rows * LANE - flat.shape[0])).reshape(rows, LANE)


def _unpack_small(buf):
    flat = buf.reshape(-1)
    rel = flat[:REL_BUCKETS * 32].reshape(REL_BUCKETS, 32)
    off = REL_BUCKETS * 32
    small = {name: [] for name in SMALL}
    for l in range(DEPTH):
        for name in SMALL:
            n = SMALL_SIZES[name]
            small[name].append(flat[off:off + n])
            off += n
    conv_w = flat[off:off + DEPTH * CONV_W_SIZE].reshape(DEPTH, 3, 2 * D_FF)
    off += DEPTH * CONV_W_SIZE
    return rel, {k: jnp.stack(v) for k, v in small.items()}, conv_w, flat[off:off + LANE]


def _rows2d(a, lead):
    return a.reshape(a.shape[:lead] + (-1, a.shape[-1]))


def _row_tile(rows):
    for cand in (512, 256, 128, 64, 32, 16, 8):
        if rows % cand == 0:
            return cand
    raise ValueError(rows)


def _pair_add(g, got, core, *, name):
    g0, g1, got2 = _rows2d(g[0], 0), _rows2d(g[1], 0), _rows2d(got, 0)
    rows, c = got2.shape
    flag = jnp.zeros((1, 1, LANE), F32) + core.astype(F32)

    def fn(i, nt, a0, a1, b, f):
        return jnp.where(f[:, 0:1] == 0.0, a0, a1) + b

    out = _rowwise(fn, [(g0[None], c, 0), (g1[None], c, 0), (got2[None], c, 0)], pars=[(flag, LANE, 0)],
                   outs=[(c, c, 0, BF16)], tm=_row_tile(rows), name=name)[0][0]
    return out.reshape(got.shape)


def _chip_add(own, got, *, name):
    own2, got2 = _rows2d(own, 0), _rows2d(got, 1)
    rows, c = own2.shape
    tm = _row_tile(rows)

    def fn(i, nt, a, b0, b1, b2):
        return ((a.astype(F32) + b0.astype(F32)) + b1.astype(F32)) + b2.astype(F32)

    stacked = got2.reshape(1, 3 * rows, c)
    out = _rowwise(fn, [(own2[None], c, 0)] + [(stacked, c, 0, k * (rows // tm)) for k in range(3)],
                   outs=[(c, c, 0, F32)], tm=tm, t=rows, name=name)[0][0]
    return out.reshape(own.shape)


def _perm(a, d):
    if d == 1:
        return a
    t = a.shape[0]
    return jnp.swapaxes(a.reshape((t // d, d) + a.shape[1:]), 0, 1).reshape(a.shape)


def _unperm(a, d):
    if d == 1:
        return a
    t = a.shape[0]
    return jnp.swapaxes(a.reshape((d, t // d) + a.shape[1:]), 0, 1).reshape(a.shape)


def _pad_lanes(a, w=LANE):
    return jnp.pad(a, [(0, 0)] * (a.ndim - 1) + [(0, w - a.shape[-1])])


def _rope_tables(t):
    pos = jnp.arange(t, dtype=F32)
    inv_freq = ROPE_BASE ** (-jnp.arange(0, C_ROPE, 2, dtype=F32) / C_ROPE)
    ang = pos[:, None] * inv_freq[None, :]
    cos, sin = jnp.cos(ang), jnp.sin(ang)
    ones, zeros = jnp.ones((t, C_NOPE), F32), jnp.zeros((t, C_NOPE), F32)
    tail = LANE - C_NOPE - C_ROPE
    c = jnp.concatenate([ones, cos, cos, ones[:, :tail]], axis=1)
    s = jnp.concatenate([zeros, -sin, sin, zeros[:, :tail]], axis=1)
    return c, s


def _band_calls(t, proj, projs_g, sinks):
    none = jnp.full((N_HEADS,), NEG, F32)
    a0 = M_A0 // LANE
    calls = [(proj, (a0, a0 + 4, a0 + 8), t // BLK, False, none)]
    calls += [(pg, (0, 4, 8), t // (d * BLK), False, none) for pg, d in zip(projs_g, A_DILS[1:])]
    calls.append((proj, (M_BQ // LANE, M_BK // LANE, M_BV // LANE), t // BLK, True, sinks.astype(F32)))
    return calls


def _layer_fwd(l, x, xb, w, p, biases, rope_cs):
    t = x.shape[0]
    n = f"l{l}_"
    xps = [_perm(xb, d) for d in A_DILS[1:]]
    proj = _mm(xb, w["wm_t"], tb=True, out_dtype=BF16, tm=TM_TOKENS, name=n + "proj")
    projs_g = [_mm(xp, wg, tb=True, out_dtype=BF16, tm=TM_TOKENS, tn=768, name=n + f"proj_g{i + 1}")
               for i, (xp, wg) in enumerate(zip(xps, w["wg_t"]))]
    s = {"xb": xb, "xps": xps, "proj": proj, "projs_g": projs_g}

    calls = _band_calls(t, proj, projs_g, p["sinks"])
    outs = [_band_fwd(src, offs, biases[i], sk, nb=nb, gqa=gqa, name=n + f"band{i}")
            for i, (src, offs, nb, gqa, sk) in enumerate(calls)]
    os_ = [_unperm(outs[gi][0], d) for gi, d in enumerate(A_DILS)]
    lses = [_unperm(outs[gi][1], d) for gi, d in enumerate(A_DILS)]
    o_a = _combine_fwd(os_, lses, name=n + "combine_fwd")
    o_b_f, lse_b = outs[3]
    o_b = o_b_f.astype(BF16)
    s.update(os=os_, lses=lses, lses_p=[outs[gi][1] for gi in range(3)], o_b=o_b_f, lse_b=lse_b)

    rq = _rms_fwd(proj, C_Q_RANK, M_CQ // C_Q_RANK, p["q_norm_g"], name=n + "rms_q")
    rkv = _rms_fwd(proj, C_KV_RANK, M_CDKV // C_KV_RANK, p["kv_norm_g"], name=n + "rms_kv")
    q_cp = _mm(rq, w["w_uq_p"], out_dtype=BF16, name=n + "uq")
    kv_cp = _mm(rkv, w["w_ukv_p"], out_dtype=BF16, name=n + "ukv")
    q_full = _rope_slabs(q_cp, N_HEADS, rope_cs[0], rope_cs[1], name=n + "rope_q")
    k_full = _rope_slabs(kv_cp, N_HEADS, rope_cs[0], rope_cs[1], add=(proj, (M_CDKV + C_KV_RANK) // LANE), name=n + "rope_k")
    vt = jnp.transpose(kv_cp[:, N_HEADS * LANE:].T.reshape(N_HEADS // 2, LANE, t // TQ, TQ), (0, 2, 1, 3))
    o_c_f, lse_c = _mla_fwd(q_full, k_full, vt, name=n + "mla_fwd")
    o_c = o_c_f.astype(BF16)
    s.update(rq=rq, rkv=rkv, q_full=q_full, k_full=k_full, kv_cp=kv_cp, lse_c=lse_c, o_c=o_c_f)

    obs = [o_a, o_b, o_c]
    ys = [_mm(o, w["w_branch"][i], out_dtype=BF16, name=n + f"branch{i}") for i, o in enumerate(obs)]
    merged = _merge_fwd(proj, p["b_gate"], ys, name=n + "merge")
    mix = _mm(merged, w["w_out"], name=n + "out")
    x1f, x1b, z1 = _ln_fwd(x, mix, p["ln1_g"], p["ln1_b"], name=n + "ln1")
    s.update(obs=obs, ys=ys, merged=merged, z1=z1, x1b=x1b)

    ug = _mm(x1b, w["wup_g"], tm=TM_TOKENS, tn=1408, out_dtype=BF16, name=n + "up_g")
    uv = _mm(x1b, w["wup_v"], tm=TM_TOKENS, tn=1408, out_dtype=BF16, name=n + "up_v")
    h = _glu_fwd(ug, uv, w["conv_w"], p["conv_b"], name=n + "glu")
    ff = _mm(h, w["w_ffn_down"], tm=TM_TOKENS, tk=1408, name=n + "down")
    x2f, x2b, z2 = _ln_fwd(x1f, ff, p["ln2_g"], p["ln2_b"], name=n + "ln2")
    s.update(ug=ug, uv=uv, h=h, z2=z2)
    return x2f, x2b, s


def _layer_bwd(l, s, dys, coefs, w, p, biases, rope_cs):
    n = f"l{l}b_"
    t = s["z2"].shape[0]
    gw, gs = {}, {}

    dz2, dz2b, gs["ln2_g"], gs["ln2_b"] = _ln_bwd(s["z2"], p["ln2_g"], dys, coefs, name=n + "ln2")
    dh = _mm(dz2b, w["w_ffn_down"], tb=True, tm=TM_TOKENS, tn=1408, out_dtype=BF16, name=n + "d_h")
    gw["w_ffn_down"] = _mm(s["h"], dz2b, ta=True, tm=1408, tk=1024, name=n + "g_down")
    dcg, dcv, gw["conv_w"], gs["conv_b"] = _glu_bwd_a(s["ug"], s["uv"], w["conv_w"], p["conv_b"], dh, name=n + "glu_a")
    dug = _glu_bwd_b(dcg, w["conv_w"], 0, name=n + "glu_bg")
    duv = _glu_bwd_b(dcv, w["conv_w"], 1, name=n + "glu_bv")
    dx1_g = _mm(dug, w["wup_g"], tb=True, tm=TM_TOKENS, tk=1408, name=n + "d_x1g")
    dx1_v = _mm(duv, w["wup_v"], tb=True, tm=TM_TOKENS, tk=1408, name=n + "d_x1v")
    gw["w_ffn_up"] = jnp.concatenate([_mm(s["x1b"], dug, ta=True, tn=1408, tk=1024, name=n + "g_upg"),
                                      _mm(s["x1b"], duv, ta=True, tn=1408, tk=1024, name=n + "g_upv")], axis=1)

    dz1, dz1b, gs["ln1_g"], gs["ln1_b"] = _ln_bwd(s["z1"], p["ln1_g"], [dz2, dx1_g, dx1_v], [ALPHA, 1.0, 1.0], name=n + "ln1")
    dmerged = _mm(dz1b, w["w_out"], tb=True, out_dtype=BF16, name=n + "d_merged")
    gw["w_out"] = _mm(s["merged"], dz1b, ta=True, name=n + "g_out")
    dys_b, dgp, gs["b_gate"] = _merge_bwd(s["proj"], p["b_gate"], s["ys"], dmerged, name=n + "merge")
    dos = [_mm(dy, w["w_branch"][i], tb=True, out_dtype=BF16, name=n + f"d_o{i}") for i, dy in enumerate(dys_b)]
    gw["w_branch"] = jnp.stack([_mm(o, dy, ta=True, name=n + f"g_branch{i}") for i, (o, dy) in enumerate(zip(s["obs"], dys_b))])

    do_gs, dpr_gs = _combine_bwd(s["os"], s["lses"], dos[0], name=n + "combine")
    do_b, dpr_b = _delta(dos[1], s["o_b"], name=n + "delta_b")
    do_list = [_perm(a, d) for a, d in zip(do_gs, A_DILS)] + [do_b]
    dpr_list = [_perm(a, d) for a, d in zip(dpr_gs, A_DILS)] + [dpr_b]
    lse_list = s["lses_p"] + [s["lse_b"]]
    calls = _band_calls(t, s["proj"], s["projs_g"], p["sinks"])
    band = [_band_bwd(src, offs, do_list[i], lse_list[i], dpr_list[i], biases[i], sk, nb=nb, gqa=gqa, name=n + f"band{i}")
            for i, (src, offs, nb, gqa, sk) in enumerate(calls)]
    gs["sinks"] = band[3][4][:, 0, 0]
    ds_sum = jnp.concatenate([b_[3] for b_ in band], axis=0)

    do_c, delta_c = _delta(dos[2], s["o_c"], name=n + "delta_c")
    dq, dk, dv = _mla_bwd(s["q_full"], s["k_full"], s["kv_cp"], do_c, s["lse_c"], delta_c, name=n + "mla")
    dq_cp = _rope_slabs(dq, N_HEADS, rope_cs[0], -rope_cs[1], name=n + "rope_q")
    dk_sum = _rowwise(lambda i, nt, *vs: sum(vs[1:], vs[0]), [(dk[None], LANE, hh) for hh in range(N_HEADS)],
                      outs=[(LANE, LANE, 0, F32)], tm=1024, name=n + "krope_sum")[0][0]
    dkr = _rope_slabs(dk_sum, 1, rope_cs[0], -rope_cs[1], to_front=True, name=n + "rope_k")
    dkv_cp = jnp.concatenate([dk, dv], axis=1)
    d_rq = _mm(dq_cp, w["w_uq_p"], tb=True, name=n + "d_rq")
    d_rkv = _mm(dkv_cp, w["w_ukv_p"], tb=True, name=n + "d_rkv")
    g_uq = _mm(s["rq"], dq_cp, ta=True, name=n + "g_uq")
    g_ukv = _mm(s["rkv"], dkv_cp, ta=True, name=n + "g_ukv")
    gw["w_uq"] = g_uq.reshape(C_Q_RANK, N_HEADS, LANE)[:, :, :C_NOPE + C_ROPE].reshape(C_Q_RANK, -1)
    kw = N_HEADS * LANE
    gw["w_ukv"] = jnp.concatenate([g_ukv[:, :kw].reshape(C_KV_RANK, N_HEADS, LANE)[:, :, :C_NOPE],
                                   g_ukv[:, kw:].reshape(C_KV_RANK, N_HEADS, HEAD_DIM)], axis=2).reshape(C_KV_RANK, -1)
    dcq, gs["q_norm_g"] = _rms_bwd(s["proj"], C_Q_RANK, M_CQ // C_Q_RANK, p["q_norm_g"], d_rq, name=n + "rms_q")
    dckv, gs["kv_norm_g"] = _rms_bwd(s["proj"], C_KV_RANK, M_CDKV // C_KV_RANK, p["kv_norm_g"], d_rkv, name=n + "rms_kv")
    dcdkv = jnp.concatenate([dckv, dkr], axis=1)

    dproj = jnp.concatenate(dgp + list(band[0][:3]) + list(band[3][:3]) + [dcq, dcdkv], axis=1)
    dprojs_g = [jnp.concatenate(band[gi][:3], axis=1) for gi in (1, 2)]
    dx_terms = [_mm(dproj, w["wm_t"], tm=TM_TOKENS, tk=1024, name=n + "d_x")]
    dx_terms += [_unperm(_mm(dp, wg, tm=TM_TOKENS, tk=768, name=n + f"d_x_g{i + 1}"), d)
                 for i, (dp, wg, d) in enumerate(zip(dprojs_g, w["wg_t"], A_DILS[1:]))]
    g_main = _mm(dproj, s["xb"], ta=True, name=n + "g_in")
    g_groups = [_mm(dp, xp, ta=True, tm=768, name=n + f"g_in_g{i + 1}") for i, (xp, dp) in enumerate(zip(s["xps"], dprojs_g))]
    fold = lambda a, tag: _sum_rows([a.reshape(2, 2, HEAD_DIM, D_MODEL)[:, j] for j in range(2)], tm=HEAD_DIM,
                                    name=n + "g_fold_" + tag).reshape(2 * HEAD_DIM, D_MODEL)
    gw["w_in"] = jnp.concatenate([g_main[M_A0:M_BQ], g_groups[0], g_groups[1], g_main[M_BQ:M_BK], fold(g_main[M_BK:M_BV], "k"),
                                  fold(g_main[M_BV:M_CQ], "v"), g_main[M_CQ:M_CDKV + C_KV_RANK + C_ROPE], g_main[M_GATE:M_A0]], axis=0)
    return [dz1] + dx_terms, [ALPHA, 1.0, 1.0, 1.0], gw, gs, ds_sum


def _local_step(x, target, ws, rel_table, small):
    t = x.shape[0]
    ps = [{k: small[k][l] for k in SMALL} for l in range(DEPTH)]
    bucket = _bucket_index()
    bias_all = _bias_lookup(bucket, rel_table.T, name="bias_lookup").reshape(4, N_HEADS, BLK, 2 * BLK)
    step = BLK + jnp.arange(BLK)[:, None] - jnp.arange(2 * BLK)[None, :]
    biases = [jnp.where((step >= 0) & (step <= lim), bias_all[i], NEG) for i, lim in enumerate((BLK, BLK, BLK, BLK - 1))]
    rope_cs = _rope_tables(t)

    saved, h, hb = [], x, x.astype(BF16)
    for l in range(DEPTH):
        h, hb, s = _layer_fwd(l, h, hb, ws[l], ps[l], biases, rope_cs)
        saved.append(s)
    dy, loss_part = _loss_and_grad(h, target, name="loss")

    dys, coefs = [dy], [1.0]
    gws, gss, dss = [None] * DEPTH, [None] * DEPTH, [None] * DEPTH
    for l in reversed(range(DEPTH)):
        dys, coefs, gws[l], gss[l], dss[l] = _layer_bwd(l, saved[l], dys, coefs, ws[l], ps[l], biases, rope_cs)
    grad_x = _lincomb(dys, coefs, name="grad_x")
    npos = 2 * BLK * BLK
    g_rel = _bias_grad(bucket, dss[0].reshape(4 * N_HEADS, npos), dss[1].reshape(4 * N_HEADS, npos), name="bias_grad").T
    gsmall = {k: jnp.stack([gss[l][k] for l in range(DEPTH)]) for k in SMALL}
    return loss_part, grad_x, gws, gsmall, g_rel


def kernel(x, rel_table, w_in, b_gate, sinks, q_norm_g, kv_norm_g, w_uq, w_ukv, w_branch, w_out, ln1_g, ln1_b, w_ffn_up, conv_w, conv_b, w_ffn_down, ln2_g, ln2_b, loss_target, m_rel_table, m_w_in, m_b_gate, m_sinks, m_q_norm_g, m_kv_norm_g, m_w_uq, m_w_ukv, m_w_branch, m_w_out, m_ln1_g, m_ln1_b, m_w_ffn_up, m_conv_w, m_conv_b, m_w_ffn_down, m_ln2_g, m_ln2_b, v_rel_table, v_w_in, v_b_gate, v_sinks, v_q_norm_g, v_kv_norm_g, v_w_uq, v_w_ukv, v_w_branch, v_w_out, v_ln1_g, v_ln1_b, v_w_ffn_up, v_conv_w, v_conv_b, v_w_ffn_down, v_ln2_g, v_ln2_b):
    wts = dict(rel_table=rel_table, w_in=w_in, b_gate=b_gate, sinks=sinks, q_norm_g=q_norm_g, kv_norm_g=kv_norm_g, w_uq=w_uq,
               w_ukv=w_ukv, w_branch=w_branch, w_out=w_out, ln1_g=ln1_g, ln1_b=ln1_b, w_ffn_up=w_ffn_up, conv_w=conv_w,
               conv_b=conv_b, w_ffn_down=w_ffn_down, ln2_g=ln2_g, ln2_b=ln2_b)
    ms = dict(rel_table=m_rel_table, w_in=m_w_in, b_gate=m_b_gate, sinks=m_sinks, q_norm_g=m_q_norm_g, kv_norm_g=m_kv_norm_g,
              w_uq=m_w_uq, w_ukv=m_w_ukv, w_branch=m_w_branch, w_out=m_w_out, ln1_g=m_ln1_g, ln1_b=m_ln1_b, w_ffn_up=m_w_ffn_up,
              conv_w=m_conv_w, conv_b=m_conv_b, w_ffn_down=m_w_ffn_down, ln2_g=m_ln2_g, ln2_b=m_ln2_b)
    vs = dict(rel_table=v_rel_table, w_in=v_w_in, b_gate=v_b_gate, sinks=v_sinks, q_norm_g=v_q_norm_g, kv_norm_g=v_kv_norm_g,
              w_uq=v_w_uq, w_ukv=v_w_ukv, w_branch=v_w_branch, w_out=v_w_out, ln1_g=v_ln1_g, ln1_b=v_ln1_b, w_ffn_up=v_w_ffn_up,
              conv_w=v_conv_w, conv_b=v_conv_b, w_ffn_down=v_w_ffn_down, ln2_g=v_ln2_g, ln2_b=v_ln2_b)

    core = lax.axis_index("c")
    chip = 2 * lax.axis_index("x") + lax.axis_index("y")

    names = [name for name, _, _ in MATS]
    gathered = dict(zip(names, _allgather_weights([_weight_send(name, wts[name]) for name in names])))
    ws = [_full_weights(gathered, l) for l in range(DEPTH)]

    small = {k: wts[k] for k in SMALL}
    loss_part, grad_x, gws, gsmall, g_rel = _local_step(x[0], loss_target[0], ws, rel_table, small)

    rnames = [name for name, _, _ in REDUCED]
    gsend = [tuple(_grad_send(name, gws[l][name], shape, ax) for l in range(DEPTH)) for name, shape, ax in REDUCED]
    theirs = _sibling_swap(gsend)
    pairs = [_pair_add(g, t_, core, name="grad_pair_" + name) for name, g, t_ in zip(rnames, gsend, theirs)]
    arrived = _chip_scatter(pairs)
    reduced = [_chip_add(lax.dynamic_index_in_dim(p, chip, 0, keepdims=False), a, name="grad_chip_" + name)
               for name, p, a in zip(rnames, pairs, arrived)]
    others = _sibling_share(reduced)
    gshard = {}
    for name, mine, other in zip(rnames, reduced, others):
        layers = [jnp.where(core == l, mine, other) for l in range(DEPTH)]
        gshard[name] = jnp.stack([_grad_recv(name, a) for a in layers])

    conv_w_full = jnp.stack([gws[l]["conv_w"] for l in range(DEPTH)])
    small_red = _allreduce_small(_pack_small(g_rel, gsmall, conv_w_full, loss_part))
    g_rel_r, gsmall_r, conv_w_r, loss_vec = _unpack_small(small_red)
    loss = loss_vec[0]
    shard_w = 2 * D_FF // N_CHIP
    gshard["conv_w"] = lax.dynamic_slice_in_dim(conv_w_r, chip * shard_w, shard_w, axis=2)

    grads = dict(gshard)
    grads.update(gsmall_r)
    grads["rel_table"] = g_rel_r
    deltas, new_m, new_v = {}, {}, {}
    for name, _, _ in MATS:
        shp = wts[name].shape
        v2 = lambda a: a.reshape(-1, shp[-1])
        d_, m_, v_ = _adamw(v2(wts[name]), v2(grads[name]), v2(ms[name]), v2(vs[name]), name="adamw_" + name)
        deltas[name], new_m[name], new_v[name] = d_.reshape(shp), m_.reshape(shp), v_.reshape(shp)
    zero, none = jnp.zeros((LANE,), F32), jnp.zeros((0,), F32)
    sw = _pack_small(wts["rel_table"], {k: wts[k] for k in SMALL}, none, zero)
    sm = _pack_small(ms["rel_table"], {k: ms[k] for k in SMALL}, none, zero)
    sv = _pack_small(vs["rel_table"], {k: vs[k] for k in SMALL}, none, zero)
    sg = _pack_small(g_rel_r, gsmall_r, none, zero)
    sd, smn, svn = _adamw(sw, sg, sm, sv, name="adamw_small")
    for res, buf in ((deltas, sd), (new_m, smn), (new_v, svn)):
        rel_, sm_ = _unpack_small(jnp.pad(buf, ((0, small_red.shape[0] - buf.shape[0]), (0, 0))))[:2]
        res["rel_table"] = rel_
        res.update(sm_)

    return (loss, grad_x[None], *[grads[k] for k in WEIGHT_ORDER], *[deltas[k] for k in WEIGHT_ORDER],
            *[new_m[k] for k in WEIGHT_ORDER], *[new_v[k] for k in WEIGHT_ORDER])
```

```python
import math

import jax
import jax.numpy as jnp
from jax import lax
from jax.experimental import pallas as pl
from jax.experimental.pallas import tpu as pltpu

F32 = jnp.float32
BF16 = jnp.bfloat16
MESH = pl.DeviceIdType.MESH

D_MODEL = 1024
DEPTH = 2
HEAD_DIM = 64
N_HEADS = 8
A_DILS = (1, 4, 16)
C_Q_RANK = 256
C_KV_RANK = 128
C_NOPE = 64
C_ROPE = 32
ROPE_BASE = 10000.0
REL_BUCKETS = 32
REL_MAX_DIST = 2048
D_FF = 2816
ALPHA = (2 * DEPTH) ** 0.25
LN_EPS = 1e-5
RMS_EPS = 1e-6
NEG = -1e30
LOG2E, LN2 = math.log2(math.e), math.log(2.0)
ADAM_LR, ADAM_B1, ADAM_B2, ADAM_EPS, ADAM_WD, ADAM_STEP = 0.001, 0.9, 0.999, 1e-08, 0.01, 10

VMEM_LIMIT_BYTES = 56 * 1024 * 1024
LANE = 128
BLK = 128
TQ = 512
TM_TOKENS = 2048
HALO = 16
BAND_SCALE = HEAD_DIM ** -0.5
BAND_UNROLL = 16

D_IN = 8864
A_COLS = 3 * N_HEADS * HEAD_DIM
ORIG = {"a": 0, "bq": 4608, "bk": 5120, "bv": 5248, "cq": 5376, "cdkv": 5632, "gate": 5792}
M_GATE, M_A0, M_BQ, M_BK, M_BV, M_CQ, M_CDKV, M_COLS = 0, 3072, 4608, 5120, 5376, 5632, 5888, 6144

N_CHIP = 4
MATS = (
    ("w_in", (D_MODEL, D_IN), 1),
    ("w_uq", (C_Q_RANK, 768), 1),
    ("w_ukv", (C_KV_RANK, 1024), 1),
    ("w_branch", (3, 512, D_MODEL), 2),
    ("w_out", (D_MODEL, D_MODEL), 0),
    ("w_ffn_up", (D_MODEL, 2 * D_FF), 1),
    ("conv_w", (3, 2 * D_FF), 1),
    ("w_ffn_down", (D_FF, D_MODEL), 0),
)
SMALL = ("b_gate", "sinks", "q_norm_g", "kv_norm_g", "ln1_g", "ln1_b", "conv_b", "ln2_g", "ln2_b")
SMALL_SIZES = {"b_gate": 3072, "sinks": 8, "q_norm_g": 256, "kv_norm_g": 128, "ln1_g": 1024, "ln1_b": 1024,
               "conv_b": 5632, "ln2_g": 1024, "ln2_b": 1024}
WEIGHT_ORDER = ("rel_table", "w_in", "b_gate", "sinks", "q_norm_g", "kv_norm_g", "w_uq", "w_ukv", "w_branch",
                "w_out", "ln1_g", "ln1_b", "w_ffn_up", "conv_w", "conv_b", "w_ffn_down", "ln2_g", "ln2_b")


def _cparams(sem):
    return pltpu.CompilerParams(dimension_semantics=sem, vmem_limit_bytes=VMEM_LIMIT_BYTES)


def _ceil_to(n, m):
    return -(-n // m) * m


def _pick(n, target):
    if n <= target:
        return n
    best = None
    for t in range(LANE, target + 1, LANE):
        if n % t == 0:
            best = t
    assert best is not None, (n, target)
    return best


def _mm(a, b, *, ta=False, tb=False, out_dtype=F32, tm=1024, tn=1024, tk=2048, name):
    assert not (ta and tb)
    k, m = a.shape[::-1] if not ta else a.shape
    n = b.shape[0] if tb else b.shape[1]
    assert (b.shape[1] if tb else b.shape[0]) == k
    tm, tn, tk = _pick(m, tm), _pick(n, tn), _pick(k, tk)
    nk = k // tk
    dn = (((0 if ta else 1,), (1 if tb else 0,)), ((), ()))

    def body(a_ref, b_ref, o_ref, acc_ref):
        part = lax.dot_general(a_ref[...].astype(BF16), b_ref[...].astype(BF16), dn, preferred_element_type=F32)
        if nk == 1:
            o_ref[...] = part.astype(o_ref.dtype)
        else:
            kk = pl.program_id(2)

            @pl.when(kk == 0)
            def _():
                acc_ref[...] = part

            @pl.when(kk > 0)
            def _():
                acc_ref[...] += part

            @pl.when(kk == nk - 1)
            def _():
                o_ref[...] = acc_ref[...].astype(o_ref.dtype)

    a_spec = pl.BlockSpec((tk, tm), lambda i, j, kk: (kk, i)) if ta else pl.BlockSpec((tm, tk), lambda i, j, kk: (i, kk))
    b_spec = pl.BlockSpec((tn, tk), lambda i, j, kk: (j, kk)) if tb else pl.BlockSpec((tk, tn), lambda i, j, kk: (kk, j))
    return pl.pallas_call(
        body, name=name, grid=(m // tm, n // tn, nk),
        in_specs=[a_spec, b_spec],
        out_specs=pl.BlockSpec((tm, tn), lambda i, j, kk: (i, j)),
        out_shape=jax.ShapeDtypeStruct((m, n), out_dtype),
        scratch_shapes=[pltpu.VMEM((tm, tn) if nk > 1 else (8, LANE), F32)],
        compiler_params=_cparams(("parallel", "parallel", "arbitrary")),
    )(a, b)


def _rowwise(fn, rows, *, pars=(), halos=(), outs=(), accs=(), tm, name, ncol=1, t=None):
    nb = rows[0][0].shape[0]
    t = rows[0][0].shape[1] if t is None else t
    tm = min(tm, t)
    assert t % tm == 0 and tm % 8 == 0
    nt = t // tm
    in_specs, args = [], []
    for spec in rows:
        arr, c, off = spec[:3]
        rb = spec[3] if len(spec) > 3 else 0
        in_specs.append(pl.BlockSpec((1, tm, c), lambda b, cc, i, off=off, rb=rb: (b, i + rb, off + cc)))
        args.append(arr)
    for arr, c, off, kind in halos:
        if kind == "prev":
            im = lambda b, cc, i, off=off: (b, jnp.maximum(i * (tm // HALO) - 1, 0), off + cc)
        else:
            im = lambda b, cc, i, off=off: (b, jnp.minimum((i + 1) * (tm // HALO), t // HALO - 1), off + cc)
        in_specs.append(pl.BlockSpec((1, HALO, c), im))
        args.append(arr)
    for arr, c, off in pars:
        bp, r = arr.shape[:2]
        if bp > 1:
            im = lambda b, cc, i, off=off: (b, 0, off + cc)
        else:
            im = lambda b, cc, i, off=off: (0, 0, off + cc)
        in_specs.append(pl.BlockSpec((1, r, c), im))
        args.append(arr)
    out_specs, out_shapes = [], []
    for ctot, c, off, dt in outs:
        out_specs.append(pl.BlockSpec((1, tm, c), lambda b, cc, i, off=off: (b, i, off + cc)))
        out_shapes.append(jax.ShapeDtypeStruct((nb, t, ctot), dt))
    for r, ctot, c, off in accs:
        out_specs.append(pl.BlockSpec((1, r, c), lambda b, cc, i, off=off: (b, 0, off + cc)))
        out_shapes.append(jax.ShapeDtypeStruct((nb, r, ctot), F32))
    n_in, n_out = len(args), len(outs)

    def body(*refs):
        i = pl.program_id(2)
        res = fn(i, nt, *[r[0].astype(F32) for r in refs[:n_in]])
        if not isinstance(res, (tuple, list)):
            res = (res,)
        for o_ref, val in zip(refs[n_in:n_in + n_out], res[:n_out]):
            o_ref[0] = val.astype(o_ref.dtype)
        for a_ref, val in zip(refs[n_in + n_out:], res[n_out:]):
            @pl.when(i == 0)
            def _(a_ref=a_ref, val=val):
                a_ref[0] = val

            @pl.when(i > 0)
            def _(a_ref=a_ref, val=val):
                a_ref[0] += val

    res = pl.pallas_call(
        body, name=name, grid=(nb, ncol, nt), in_specs=in_specs, out_specs=out_specs, out_shape=out_shapes,
        compiler_params=_cparams(("parallel", "parallel", "arbitrary")),
    )(*args)
    return res


def _dot(a, b):
    return lax.dot_general(a, b, (((1,), (0,)), ((), ())), preferred_element_type=F32)


def _dot_nt(a, b):
    return lax.dot_general(a, b, (((1,), (1,)), ((), ())), preferred_element_type=F32)


def _dot_tn(a, b):
    return lax.dot_general(a, b, (((0,), (0,)), ((), ())), preferred_element_type=F32)


def _rows(parts):
    return jnp.concatenate(parts, axis=0)


def _lane_lo():
    return lax.broadcasted_iota(jnp.int32, (1, LANE), 1) < HEAD_DIM


def _blocks(a):
    return [a[i * BLK:(i + 1) * BLK] for i in range(a.shape[0] // BLK)]


def _band_geometry(t):
    rows = min(BAND_UNROLL, t // BLK) * BLK
    assert t % rows == 0
    return rows, t // rows


def _band_operands(g, k_ref, v_ref, rows):
    start = pl.multiple_of(g * rows, rows)
    pstart = pl.multiple_of(jnp.maximum(g * rows - BLK, 0), BLK)
    out = []
    for ref in (k_ref, v_ref):
        cur = _blocks(ref[pl.ds(start, rows), :])
        raw = ref[pl.ds(pstart, rows), :]
        shifted = _rows([jnp.zeros((BLK, LANE), raw.dtype), raw[:rows - BLK]])
        prev = _blocks(jnp.where(g == 0, shifted, raw))
        out.append([_rows([p, c]) for p, c in zip(prev, cur)])
    return out


def _band_scores(g, qa, kk, b_ref, a, nb):
    u = len(qa)
    assert nb % u == 0 or u % nb == 0
    firsts = []
    for i in range(u):
        if nb >= u:
            val = jnp.where(lax.rem(g * u, nb) == 0, NEG, 0.0).astype(F32) if i == 0 else 0.0
        else:
            val = NEG if i % nb == 0 else 0.0
        firsts.append(jnp.zeros((BLK, 1), F32) + val)
    prev_slots = lax.broadcasted_iota(jnp.int32, (1, 2 * BLK), 1) < BLK
    return (_rows([_dot_nt(q, k) for q, k in zip(qa, kk)]) + _rows([b_ref[a]] * u)
            + jnp.where(prev_slots, _rows(firsts), 0.0))


def _band_fwd(src, offs, bias, sinks, *, nb, gqa, name):
    t = src.shape[0]
    rows, nstep = _band_geometry(t)
    qo, ko, vo = offs
    share = 2 if gqa else 1

    def body(sink_ref, q_ref, k_ref, v_ref, b_ref, o_ref, lse_ref):
        hp, g = pl.program_id(0), pl.program_id(1)
        lo = _lane_lo()
        q2 = q_ref[...]
        kk, vv = _band_operands(g, k_ref, v_ref, rows)
        outs, lses = [], []
        for a in range(2):
            sink = sink_ref[2 * hp + a]
            qa = _blocks(jnp.where(lo if a == 0 else jnp.logical_not(lo), q2, jnp.zeros_like(q2)) * BAND_SCALE)
            s = _band_scores(g, qa, kk, b_ref, a, nb)
            m = jnp.maximum(jnp.max(s, axis=1, keepdims=True), sink)
            p = jnp.exp(s - m)
            l = jnp.sum(p, axis=1, keepdims=True) + jnp.exp(sink - m)
            p_b = _blocks((p * (1.0 / l)).astype(BF16))
            outs.append(_rows([_dot(pb, v) for pb, v in zip(p_b, vv)]))
            lses.append(m + jnp.log(l))
        o_ref[...] = jnp.where(lo, outs[0], outs[1])
        lse_ref[...] = jnp.where(lo, lses[0], lses[1])

    slab = lambda off: pl.BlockSpec((rows, LANE), lambda hp, g: (g, off + hp))
    whole = lambda off: pl.BlockSpec((t, LANE), lambda hp, g: (0, off + hp // share))
    return pl.pallas_call(
        body, name=name, grid=(N_HEADS // 2, nstep),
        in_specs=[pl.BlockSpec(memory_space=pltpu.SMEM), slab(qo), whole(ko), whole(vo),
                  pl.BlockSpec((2, BLK, 2 * BLK), lambda hp, g: (hp, 0, 0))],
        out_specs=[slab(0), slab(0)],
        out_shape=[jax.ShapeDtypeStruct((t, N_HEADS * HEAD_DIM), F32)] * 2,
        compiler_params=_cparams(("parallel", "parallel")),
    )(sinks, src, src, src, bias)


def _band_bwd(src, offs, do, lse, dpr, bias, sinks, *, nb, gqa, name):
    t = src.shape[0]
    rows, nstep = _band_geometry(t)
    qo, ko, vo = offs
    share = 2 if gqa else 1

    def fold(a):
        acc = a[0:BLK]
        for i in range(1, rows // BLK):
            acc = acc + a[i * BLK:(i + 1) * BLK]
        return acc

    def body(sink_ref, q_ref, k_ref, v_ref, do_ref, lse_ref, dpr_ref, b_ref,
             dq_ref, dk_ref, dv_ref, ds_ref, dsink_ref, dk_acc, dv_acc):
        hp, g = pl.program_id(0), pl.program_id(1)
        lo = _lane_lo()
        hi = jnp.logical_not(lo)

        @pl.when(jnp.logical_and(g == 0, lax.rem(hp, share) == 0))
        def _():
            dk_acc[...] = jnp.zeros_like(dk_acc)
            dv_acc[...] = jnp.zeros_like(dv_acc)

        @pl.when(g == 0)
        def _():
            ds_ref[...] = jnp.zeros_like(ds_ref)
            dsink_ref[...] = jnp.zeros_like(dsink_ref)

        q2, do2, lse2, dpr2 = q_ref[...], do_ref[...], lse_ref[...], dpr_ref[...]
        lse_sw, dpr_sw = pltpu.roll(lse2, HEAD_DIM, axis=1), pltpu.roll(dpr2, HEAD_DIM, axis=1)
        kk, vv = _band_operands(g, k_ref, v_ref, rows)
        dqs, per_head = [], []
        for a in range(2):
            sink = sink_ref[2 * hp + a]
            mine = lo if a == 0 else hi
            qa = _blocks(jnp.where(mine, q2, jnp.zeros_like(q2)) * BAND_SCALE)
            doa = _blocks(jnp.where(mine, do2, jnp.zeros_like(do2)))
            lse_a, dpr_a = jnp.where(mine, lse2, lse_sw), jnp.where(mine, dpr2, dpr_sw)
            wide = lambda x: jnp.concatenate([x, x], axis=1)
            p = jnp.exp(_band_scores(g, qa, kk, b_ref, a, nb) - wide(lse_a))
            ds = p * (_rows([_dot_nt(d, v) for d, v in zip(doa, vv)]) - wide(dpr_a))
            ds_ref[a] += fold(ds)
            dsink_ref[a] -= jnp.sum(jnp.exp(sink - lse_a) * dpr_a, axis=0, keepdims=True)
            ds_b, p_b = _blocks(ds.astype(BF16)), _blocks(p.astype(BF16))
            dqs.append(_rows([_dot(d, k) for d, k in zip(ds_b, kk)]))
            per_head.append((ds_b, p_b, qa, doa))
        dq_ref[...] = (jnp.where(lo, dqs[0], dqs[1]) * BAND_SCALE).astype(dq_ref.dtype)
        (ds0, p0, qa0, do0), (ds1, p1, qa1, do1) = per_head
        for i in range(len(qa0)):
            at = pl.ds(pl.multiple_of(g * rows + i * BLK, BLK), 2 * BLK)
            dk_acc[at, :] += _dot_tn(_rows([ds0[i], ds1[i]]), _rows([qa0[i], qa1[i]]))
            dv_acc[at, :] += _dot_tn(_rows([p0[i], p1[i]]), _rows([do0[i], do1[i]]))

        @pl.when(g == nstep - 1)
        def _():
            dk_ref[...] = dk_acc[BLK:, :].astype(dk_ref.dtype)
            dv_ref[...] = dv_acc[BLK:, :].astype(dv_ref.dtype)

    slab = lambda off: pl.BlockSpec((rows, LANE), lambda hp, g: (g, off + hp))
    whole = lambda off: pl.BlockSpec((t, LANE), lambda hp, g: (0, off + hp // share))
    per_pair = lambda shp: pl.BlockSpec((2,) + shp, lambda hp, g: (hp,) + (0,) * len(shp))
    kv_cols = N_HEADS * HEAD_DIM // share
    return pl.pallas_call(
        body, name=name, grid=(N_HEADS // 2, nstep),
        in_specs=[pl.BlockSpec(memory_space=pltpu.SMEM), slab(qo), whole(ko), whole(vo), slab(0), slab(0), slab(0),
                  per_pair((BLK, 2 * BLK))],
        out_specs=[slab(0), whole(0), whole(0), per_pair((BLK, 2 * BLK)), per_pair((1, LANE))],
        out_shape=[jax.ShapeDtypeStruct((t, N_HEADS * HEAD_DIM), BF16), jax.ShapeDtypeStruct((t, kv_cols), BF16),
                   jax.ShapeDtypeStruct((t, kv_cols), BF16), jax.ShapeDtypeStruct((N_HEADS, BLK, 2 * BLK), F32),
                   jax.ShapeDtypeStruct((N_HEADS, 1, LANE), F32)],
        scratch_shapes=[pltpu.VMEM((t + BLK, LANE), F32), pltpu.VMEM((t + BLK, LANE), F32)],
        compiler_params=_cparams(("arbitrary", "arbitrary")),
    )(sinks, src, src, src, do, lse, dpr, bias)


MLA_V_OFF = N_HEADS


def _diag_mask(keys_on_rows=False):
    rows, cols = lax.broadcasted_iota(jnp.int32, (TQ, TQ), 0), lax.broadcasted_iota(jnp.int32, (TQ, TQ), 1)
    return rows <= cols if keys_on_rows else cols <= rows


def _mla_specs(t):
    blk = lambda f: pl.BlockSpec((TQ, LANE), lambda hp, qi, f=f: (qi, f(hp)))
    whole = lambda f: pl.BlockSpec((t, LANE), lambda hp, qi, f=f: (0, f(hp)))
    return blk, whole


def _mla_fwd(q, k, vt, *, name):
    t = q.shape[0]
    n = t // TQ
    scale = (C_NOPE + C_ROPE) ** -0.5

    def body(q0_ref, q1_ref, k0_ref, k1_ref, vt_ref, o_ref, lse_ref, m_ref, acc_ref):
        qi = pl.program_id(1)
        qs, ks = (q0_ref[...], q1_ref[...]), (k0_ref, k1_ref)
        m_ref[...] = jnp.full_like(m_ref, NEG)
        acc_ref[...] = jnp.zeros_like(acc_ref)
        first = lax.broadcasted_iota(jnp.int32, (LANE, 1), 0) < HEAD_DIM

        def step(kj, diagonal):
            rows = pl.ds(pl.multiple_of(kj * TQ, TQ), TQ)
            vtb = vt_ref[0, kj]
            one = jnp.ones_like(vtb)
            vts = (jnp.where(first, vtb, one), jnp.where(first, one, vtb))
            for a in range(2):
                s = _dot_nt(ks[a][rows, :], qs[a]) * (scale * LOG2E)
                if diagonal:
                    s = jnp.where(_diag_mask(keys_on_rows=True), s, NEG)
                m_prev = m_ref[a]
                m_new = jnp.maximum(m_prev, jnp.max(s, axis=0, keepdims=True))
                acc_ref[a] = jnp.exp2(m_prev - m_new) * acc_ref[a] + _dot(vts[a], jnp.exp2(s - m_new).astype(BF16))
                m_ref[a] = m_new

        def kloop(kj, c2):
            step(kj, False)
            return c2

        lax.fori_loop(0, qi, kloop, 0)
        step(qi, True)
        l0, l1 = acc_ref[0, HEAD_DIM:HEAD_DIM + 1, :], acc_ref[1, 0:1, :]
        ot = jnp.where(first, acc_ref[0] * (1.0 / l0), acc_ref[1] * (1.0 / l1))
        lset = jnp.where(first, m_ref[0] * LN2 + jnp.log(l0), m_ref[1] * LN2 + jnp.log(l1))
        o_ref[...] = ot.T
        lse_ref[...] = lset.T

    blk, whole = _mla_specs(t)
    return pl.pallas_call(
        body, name=name, grid=(N_HEADS // 2, n),
        in_specs=[blk(lambda hp: 2 * hp), blk(lambda hp: 2 * hp + 1), whole(lambda hp: 2 * hp), whole(lambda hp: 2 * hp + 1),
                  pl.BlockSpec((1, n, LANE, TQ), lambda hp, qi: (hp, 0, 0, 0))],
        out_specs=[blk(lambda hp: hp), blk(lambda hp: hp)],
        out_shape=[jax.ShapeDtypeStruct((t, N_HEADS * HEAD_DIM), F32)] * 2,
        scratch_shapes=[pltpu.VMEM((2, 1, TQ), F32), pltpu.VMEM((2, LANE, TQ), F32)],
        compiler_params=_cparams(("parallel", "parallel")),
    )(q, q, k, k, vt)


def _mla_bwd(q, k, kv, do, lse, delta, *, name):
    t = q.shape[0]
    n = t // TQ
    scale = (C_NOPE + C_ROPE) ** -0.5

    def body(q0_ref, q1_ref, k0_ref, k1_ref, v_ref, do_ref, lse_ref, dl_ref,
             dq_ref, dk_ref, dv_ref, dq_acc, dk_acc, dv_acc):
        qi = pl.program_id(1)
        lo = _lane_lo()

        @pl.when(qi == 0)
        def _():
            dk_acc[...] = jnp.zeros_like(dk_acc)
            dv_acc[...] = jnp.zeros_like(dv_acc)

        dq_acc[...] = jnp.zeros_like(dq_acc)
        qs, ks = (q0_ref[...], q1_ref[...]), (k0_ref, k1_ref)
        do2, lse2, dl2 = do_ref[...], lse_ref[...], dl_ref[...]
        lse_sw, dl_sw = pltpu.roll(lse2, HEAD_DIM, axis=1), pltpu.roll(dl2, HEAD_DIM, axis=1)
        heads = []
        for a in range(2):
            mine = lo if a == 0 else jnp.logical_not(lo)
            heads.append((jnp.where(mine, do2, jnp.zeros_like(do2)), jnp.where(mine, lse2, lse_sw)[:, 0:1] * LOG2E,
                          jnp.where(mine, dl2, dl_sw)[:, 0:1]))

        def step(kj, diagonal):
            rows = pl.ds(pl.multiple_of(kj * TQ, TQ), TQ)
            vb = v_ref[rows, :]
            ps = []
            for a, (doa, lse_a, dl_a) in enumerate(heads):
                kb = ks[a][rows, :]
                s = _dot_nt(qs[a], kb) * (scale * LOG2E)
                if diagonal:
                    s = jnp.where(_diag_mask(), s, NEG)
                p = jnp.exp2(s - lse_a)
                ds = (p * (_dot_nt(doa, vb) - dl_a)).astype(BF16)
                dq_acc[a] += _dot(ds, kb)
                dk_acc[a, rows, :] += _dot_tn(ds, qs[a])
                ps.append(p.astype(BF16))
            dv_acc[rows, :] += _dot_tn(jnp.concatenate(ps, axis=0), jnp.concatenate([h[0] for h in heads], axis=0))

        def kloop(kj, c2):
            step(kj, False)
            return c2

        lax.fori_loop(0, qi, kloop, 0)
        step(qi, True)
        dq_ref[:, 0:LANE] = (dq_acc[0] * scale).astype(dq_ref.dtype)
        dq_ref[:, LANE:2 * LANE] = (dq_acc[1] * scale).astype(dq_ref.dtype)

        @pl.when(qi == n - 1)
        def _():
            dk_ref[:, 0:LANE] = (dk_acc[0] * scale).astype(dk_ref.dtype)
            dk_ref[:, LANE:2 * LANE] = (dk_acc[1] * scale).astype(dk_ref.dtype)
            dv_ref[...] = dv_acc[...].astype(dv_ref.dtype)

    blk, whole = _mla_specs(t)
    even, odd, pair = (lambda hp: 2 * hp), (lambda hp: 2 * hp + 1), (lambda hp: hp)
    wide = jax.ShapeDtypeStruct((t, N_HEADS * LANE), BF16)
    return pl.pallas_call(
        body, name=name, grid=(N_HEADS // 2, n),
        in_specs=[blk(even), blk(odd), whole(even), whole(odd), whole(lambda hp: MLA_V_OFF + hp), blk(pair), blk(pair), blk(pair)],
        out_specs=[pl.BlockSpec((TQ, 2 * LANE), lambda hp, qi: (qi, hp)), pl.BlockSpec((t, 2 * LANE), lambda hp, qi: (0, hp)), whole(pair)],
        out_shape=[wide, wide, jax.ShapeDtypeStruct((t, N_HEADS * HEAD_DIM), BF16)],
        scratch_shapes=[pltpu.VMEM((2, TQ, LANE), F32), pltpu.VMEM((2, t, LANE), F32), pltpu.VMEM((t, LANE), F32)],
        compiler_params=_cparams(("arbitrary", "arbitrary")),
    )(q, q, k, k, kv, do, lse, delta)


def _bias_lookup(bucket, table_t, *, name):
    nh, npos = bucket.shape
    tp = 4096

    def body(b_ref, t_ref, o_ref):
        bk, tab = b_ref[...], t_ref[...]
        acc = jnp.zeros(bk.shape, F32)
        for i in range(REL_BUCKETS):
            acc = jnp.where(bk == i, tab[:, i:i + 1], acc)
        o_ref[...] = acc

    return pl.pallas_call(
        body, name=name, grid=(npos // tp,),
        in_specs=[pl.BlockSpec((nh, tp), lambda i: (0, i)), pl.BlockSpec((nh, REL_BUCKETS), lambda i: (0, 0))],
        out_specs=pl.BlockSpec((nh, tp), lambda i: (0, i)),
        out_shape=jax.ShapeDtypeStruct((nh, npos), F32),
        compiler_params=_cparams(("parallel",)),
    )(bucket, table_t)


def _bias_grad(bucket, ds0, ds1, *, name):
    nh, npos = bucket.shape
    tp = 4096

    def body(b_ref, a_ref, c_ref, o_ref):
        i = pl.program_id(0)
        bk, ds = b_ref[...], a_ref[...] + c_ref[...]
        lane = lax.broadcasted_iota(jnp.int32, (nh, REL_BUCKETS), 1)
        acc = jnp.zeros((nh, REL_BUCKETS), F32)
        for j in range(REL_BUCKETS):
            col = jnp.sum(jnp.where(bk == j, ds, 0.0), axis=1, keepdims=True)
            acc = acc + jnp.where(lane == j, col, 0.0)

        @pl.when(i == 0)
        def _():
            o_ref[...] = acc

        @pl.when(i > 0)
        def _():
            o_ref[...] += acc

    return pl.pallas_call(
        body, name=name, grid=(npos // tp,),
        in_specs=[pl.BlockSpec((nh, tp), lambda i: (0, i))] * 3,
        out_specs=pl.BlockSpec((nh, REL_BUCKETS), lambda i: (0, 0)),
        out_shape=jax.ShapeDtypeStruct((nh, REL_BUCKETS), F32),
        compiler_params=_cparams(("arbitrary",)),
    )(bucket, ds0, ds1)


def _t5_bucket(dist):
    n = jnp.maximum(dist, 0)
    max_exact = REL_BUCKETS // 2
    scaled = jnp.log(jnp.maximum(n, 1).astype(F32) / max_exact) / math.log(REL_MAX_DIST / max_exact)
    large = max_exact + (scaled * (REL_BUCKETS - max_exact)).astype(jnp.int32)
    return jnp.where(n < max_exact, n, jnp.minimum(large, REL_BUCKETS - 1))


def _bucket_index():
    qi = jnp.arange(BLK)[:, None]
    ci = jnp.arange(2 * BLK)[None, :]
    step = BLK + qi - ci
    per_group = [_t5_bucket(step * d).reshape(1, -1) for d in A_DILS + (1,)]
    return jnp.concatenate([jnp.tile(b, (N_HEADS, 1)) for b in per_group], axis=0).astype(jnp.int32)


def _sigmoid(x):
    return 1.0 / (1.0 + jnp.exp(-x))


def _ln_stats(z):
    mu = jnp.mean(z, axis=-1, keepdims=True)
    zc = z - mu
    var = jnp.mean(zc * zc, axis=-1, keepdims=True)
    return zc * lax.rsqrt(var + LN_EPS)


def _ln_fwd(x, mix, g, b, *, name):
    def fn(i, nt, xv, mv, gv, bv):
        z = ALPHA * xv + mv
        y = _ln_stats(z) * gv + bv
        return y, y, z

    c = x.shape[-1]
    y, yb, z = _rowwise(fn, [(x[None], c, 0), (mix[None], c, 0)], pars=[(g.reshape(1, 1, c), c, 0), (b.reshape(1, 1, c), c, 0)],
                        outs=[(c, c, 0, F32), (c, c, 0, BF16), (c, c, 0, F32)], tm=512, name=name)
    return y[0], yb[0], z[0]


def _ln_bwd(z, g, dys, coefs, *, name):
    n = len(dys)

    def fn(i, nt, zv, *rest):
        gv = rest[n]
        dy = coefs[0] * rest[0]
        for cf, t in zip(coefs[1:], rest[1:n]):
            dy = dy + cf * t
        mu = jnp.mean(zv, axis=-1, keepdims=True)
        zc = zv - mu
        r = lax.rsqrt(jnp.mean(zc * zc, axis=-1, keepdims=True) + LN_EPS)
        xh = zc * r
        dxh = dy * gv
        dz = r * (dxh - jnp.mean(dxh, axis=-1, keepdims=True) - xh * jnp.mean(dxh * xh, axis=-1, keepdims=True))
        return dz, dz, jnp.sum(dy * xh, axis=0, keepdims=True), jnp.sum(dy, axis=0, keepdims=True)

    c = z.shape[-1]
    dz, dzb, dg, db = _rowwise(fn, [(z[None], c, 0)] + [(d[None], c, 0) for d in dys], pars=[(g.reshape(1, 1, c), c, 0)],
                               outs=[(c, c, 0, F32), (c, c, 0, BF16)], accs=[(1, c, c, 0), (1, c, c, 0)], tm=512, name=name)
    return dz[0], dzb[0], dg.reshape(c), db.reshape(c)


def _rms_fwd(src, c, off, g, *, name):
    def fn(i, nt, xv, gv):
        return xv * lax.rsqrt(jnp.mean(xv * xv, axis=-1, keepdims=True) + RMS_EPS) * gv

    return _rowwise(fn, [(src[None], c, off)], pars=[(g.reshape(1, 1, c), c, 0)], outs=[(c, c, 0, BF16)], tm=1024, name=name)[0][0]


def _rms_bwd(src, c, off, g, dy, *, name):
    def fn(i, nt, xv, dyv, gv):
        r = lax.rsqrt(jnp.mean(xv * xv, axis=-1, keepdims=True) + RMS_EPS)
        gd = gv * dyv
        dx = gd * r - xv * (r * r * r) * jnp.mean(gd * xv, axis=-1, keepdims=True)
        return dx, jnp.sum(dyv * xv * r, axis=0, keepdims=True)

    dx, dg = _rowwise(fn, [(src[None], c, off), (dy[None], c, 0)], pars=[(g.reshape(1, 1, c), c, 0)],
                      outs=[(c, c, 0, BF16)], accs=[(1, c, c, 0)], tm=1024, name=name)
    return dx[0], dg.reshape(c)


def _rope_slabs(x, n_slab, c, s, *, add=None, to_front=False, name):
    half = C_ROPE // 2

    def fn(i, nt, xv, cv, sv, *rest):
        lane = lax.broadcasted_iota(jnp.int32, (1, LANE), 1)
        extra = pltpu.roll(rest[0], C_NOPE, axis=1) if rest else None
        outs = []
        for h in range(n_slab):
            xs = xv[:, h * LANE:(h + 1) * LANE]
            if extra is not None:
                xs = xs + extra
            swapped = jnp.where(lane < C_NOPE + half, pltpu.roll(xs, LANE - half, axis=1), pltpu.roll(xs, half, axis=1))
            y = xs * cv + swapped * sv
            if to_front:
                y = jnp.where(lane < C_ROPE, pltpu.roll(y, LANE - C_NOPE, axis=1), 0.0)
            outs.append(y)
        return jnp.concatenate(outs, axis=1) if n_slab > 1 else outs[0]

    w = n_slab * LANE
    rows = [(x[None], w, 0), (c[None], LANE, 0), (s[None], LANE, 0)]
    if add is not None:
        rows.append((add[0][None], LANE, add[1]))
    return _rowwise(fn, rows, outs=[(w, w, 0, BF16)], tm=512, name=name)[0][0]


def _merge_fwd(proj, b_gate, ys, *, name):
    def fn(i, nt, g0, g1, g2, ya, yb, yc, bg):
        return (_sigmoid(g0 + bg[:, 0:1024]) * ya + _sigmoid(g1 + bg[:, 1024:2048]) * yb
                + _sigmoid(g2 + bg[:, 2048:3072]) * yc)

    rows = [(proj[None], 1024, j) for j in range(3)] + [(y[None], 1024, 0) for y in ys]
    return _rowwise(fn, rows, pars=[(b_gate.reshape(1, 1, 3072), 3072, 0)], outs=[(1024, 1024, 0, BF16)], tm=512, name=name)[0][0]


def _merge_bwd(proj, b_gate, ys, dm, *, name):
    def fn(i, nt, g0, g1, g2, ya, yb, yc, dmv, bg):
        outs, dgs = [], []
        for j, (gp, y) in enumerate(((g0, ya), (g1, yb), (g2, yc))):
            s = _sigmoid(gp + bg[:, j * 1024:(j + 1) * 1024])
            outs.append(s * dmv)
            dgs.append(dmv * y * s * (1.0 - s))
        return outs + dgs + [jnp.sum(d, axis=0, keepdims=True) for d in dgs]

    rows = [(proj[None], 1024, j) for j in range(3)] + [(y[None], 1024, 0) for y in ys] + [(dm[None], 1024, 0)]
    res = _rowwise(fn, rows, pars=[(b_gate.reshape(1, 1, 3072), 3072, 0)], outs=[(1024, 1024, 0, BF16)] * 6,
                   accs=[(1, 1024, 1024, 0)] * 3, tm=256, name=name)
    dys = [r[0] for r in res[0:3]]
    dgp = [r[0] for r in res[3:6]]
    dbg = jnp.concatenate([r.reshape(1024) for r in res[6:9]])
    return dys, dgp, dbg


def _shift_down(u, halo, i, k):
    ext = jnp.concatenate([jnp.where(i > 0, halo, 0.0), u], axis=0)
    return pltpu.roll(ext, k, axis=0)[HALO:]


def _shift_up(u, halo, i, nt, k):
    ext = jnp.concatenate([u, jnp.where(i < nt - 1, halo, 0.0)], axis=0)
    n = ext.shape[0]
    return pltpu.roll(ext, n - k, axis=0)[:n - HALO]


GLU_C = D_FF // 2


def _conv(u, halo, i, w, b):
    return w[0:1] * _shift_down(u, halo, i, 2) + w[1:2] * _shift_down(u, halo, i, 1) + w[2:3] * u + b


def _glu_fwd(ug, uv, conv_w, conv_b, *, name):
    def fn(i, nt, g, v, hg, hv, wg, wv, bg, bv):
        cg, cv = _conv(g, hg, i, wg, bg), _conv(v, hv, i, wv, bv)
        return cg * _sigmoid(cg) * cv

    w3, b3 = conv_w[None], conv_b.reshape(1, 1, -1)
    c = GLU_C
    return _rowwise(fn, [(ug[None], c, 0), (uv[None], c, 0)], halos=[(ug[None], c, 0, "prev"), (uv[None], c, 0, "prev")],
                    pars=[(w3, c, 0), (w3, c, 2), (b3, c, 0), (b3, c, 2)], outs=[(D_FF, c, 0, BF16)], tm=256, ncol=2, name=name)[0][0]


def _glu_bwd_a(ug, uv, conv_w, conv_b, dh, *, name):
    def fn(i, nt, g, v, dhv, hg, hv, wg, wv, bg, bv):
        g1, g2 = _shift_down(g, hg, i, 1), _shift_down(g, hg, i, 2)
        v1, v2 = _shift_down(v, hv, i, 1), _shift_down(v, hv, i, 2)
        cg = wg[0:1] * g2 + wg[1:2] * g1 + wg[2:3] * g + bg
        cv = wv[0:1] * v2 + wv[1:2] * v1 + wv[2:3] * v + bv
        s = _sigmoid(cg)
        dcv = dhv * cg * s
        dcg = dhv * cv * (s * (1.0 + cg * (1.0 - s)))
        red = lambda a: jnp.sum(a, axis=0, keepdims=True)
        return (dcg, dcv, red(dcg), red(dcv), red(dcg * g2), red(dcg * g1), red(dcg * g),
                red(dcv * v2), red(dcv * v1), red(dcv * v))

    w3, b3 = conv_w[None], conv_b.reshape(1, 1, -1)
    c = GLU_C
    res = _rowwise(fn, [(ug[None], c, 0), (uv[None], c, 0), (dh[None], c, 0)],
                   halos=[(ug[None], c, 0, "prev"), (uv[None], c, 0, "prev")],
                   pars=[(w3, c, 0), (w3, c, 2), (b3, c, 0), (b3, c, 2)],
                   outs=[(D_FF, c, 0, BF16), (D_FF, c, 0, BF16)], accs=[(1, D_FF, c, 0)] * 8, tm=256, ncol=2, name=name)
    dcg, dcv = res[0][0], res[1][0]
    dconv_b = jnp.concatenate([res[2].reshape(D_FF), res[3].reshape(D_FF)])
    dconv_w = jnp.concatenate([jnp.concatenate([res[4 + j].reshape(1, D_FF) for j in range(3)], axis=0),
                               jnp.concatenate([res[7 + j].reshape(1, D_FF) for j in range(3)], axis=0)], axis=1)
    return dcg, dcv, dconv_w, dconv_b


def _glu_bwd_b(dc, conv_w, half, *, name):
    def fn(i, nt, d, hd, w):
        return w[2:3] * d + w[1:2] * _shift_up(d, hd, i, nt, 1) + w[0:1] * _shift_up(d, hd, i, nt, 2)

    c = GLU_C
    return _rowwise(fn, [(dc[None], c, 0)], halos=[(dc[None], c, 0, "next")], pars=[(conv_w[None], c, 2 * half)],
                    outs=[(D_FF, c, 0, BF16)], tm=256, ncol=2, name=name)[0][0]


def _loss_and_grad(y, tgt, *, name):
    def fn(i, nt, yv, tv):
        err = yv - tv
        part = jnp.sum(jnp.sum(err * err, axis=0, keepdims=True), axis=1, keepdims=True) * (0.5 / D_MODEL)
        return err * (1.0 / D_MODEL), jnp.zeros((1, LANE), F32) + part

    dy, part = _rowwise(fn, [(y[None], D_MODEL, 0), (tgt[None], D_MODEL, 0)], outs=[(D_MODEL, D_MODEL, 0, F32)],
                        accs=[(1, LANE, LANE, 0)], tm=512, name=name)
    return dy[0], part.reshape(LANE)


def _lincomb(terms, coefs, *, name):
    def fn(i, nt, *vs):
        acc = coefs[0] * vs[0]
        for cf, v in zip(coefs[1:], vs[1:]):
            acc = acc + cf * v
        return acc

    c = terms[0].shape[-1]
    return _rowwise(fn, [(a[None], c, 0) for a in terms], outs=[(c, c, 0, F32)], tm=512, name=name)[0][0]


def _sum_rows(terms, *, tm, name, dtype=F32):
    def fn(i, nt, *vs):
        acc = vs[0]
        for v in vs[1:]:
            acc = acc + v
        return acc

    c = terms[0].shape[-1]
    return _rowwise(fn, [(t, c, 0) for t in terms], outs=[(c, c, 0, dtype)], tm=tm, name=name)[0]


def _head_sums(x):
    lo = _lane_lo()
    parts = []
    for j in range(x.shape[1] // LANE):
        blk = x[:, j * LANE:(j + 1) * LANE]
        s_lo = jnp.sum(jnp.where(lo, blk, 0.0), axis=1, keepdims=True)
        s_hi = jnp.sum(jnp.where(lo, 0.0, blk), axis=1, keepdims=True)
        parts.append(jnp.where(lo, s_lo, s_hi))
    return jnp.concatenate(parts, axis=1)


def _group_weights(l0, l1, l2):
    m = jnp.maximum(jnp.maximum(l0, l1), l2)
    es = [jnp.exp(l - m) for l in (l0, l1, l2)]
    inv = 1.0 / (es[0] + es[1] + es[2])
    return [e * inv for e in es]


def _combine_fwd(os_, lses, *, name):
    def fn(i, nt, o0, o1, o2, l0, l1, l2):
        w = _group_weights(l0, l1, l2)
        return w[0] * o0 + w[1] * o1 + w[2] * o2

    c = os_[0].shape[-1]
    return _rowwise(fn, [(a[None], c, 0) for a in list(os_) + list(lses)], outs=[(c, c, 0, BF16)], tm=512, name=name)[0][0]


def _combine_bwd(os_, lses, do_a, *, name):
    def fn(i, nt, o0, o1, o2, l0, l1, l2, da):
        ws = _group_weights(l0, l1, l2)
        dws = [_head_sums(da * o) for o in (o0, o1, o2)]
        mean = ws[0] * dws[0] + ws[1] * dws[1] + ws[2] * dws[2]
        return [w * da for w in ws] + [w * mean for w in ws]

    c = do_a.shape[-1]
    res = _rowwise(fn, [(a[None], c, 0) for a in list(os_) + list(lses) + [do_a]], outs=[(c, c, 0, BF16)] * 3 + [(c, c, 0, F32)] * 3,
                   tm=256, name=name)
    return [r[0] for r in res[0:3]], [r[0] for r in res[3:6]]


def _delta(do, o, *, name):
    def fn(i, nt, d, ov):
        return d, _head_sums(d * ov)

    c = do.shape[-1]
    res = _rowwise(fn, [(do[None], c, 0), (o[None], c, 0)], outs=[(c, c, 0, BF16), (c, c, 0, F32)], tm=512, name=name)
    return res[0][0], res[1][0]


def _adamw(w, g, m, v, *, name):
    c1 = 1.0 - ADAM_B1 ** ADAM_STEP
    c2 = 1.0 - ADAM_B2 ** ADAM_STEP

    def fn(i, nt, wv, gv, mv, vv):
        mn = ADAM_B1 * mv + (1.0 - ADAM_B1) * gv
        vn = ADAM_B2 * vv + (1.0 - ADAM_B2) * (gv * gv)
        delta = -ADAM_LR * ((mn / c1) / (jnp.sqrt(vn / c2) + ADAM_EPS) + ADAM_WD * wv)
        return delta, mn, vn

    r, c = w.shape
    rp = _ceil_to(r, 8)
    pad = lambda a: jnp.pad(a, ((0, rp - r), (0, 0))) if rp != r else a
    tm = rp
    for cand in (128, 64, 32, 16, 8):
        if rp % cand == 0:
            tm = cand
            break
    res = _rowwise(fn, [(pad(a)[None], c, 0) for a in (w, g, m, v)], outs=[(c, c, 0, F32)] * 3, tm=tm, name=name)
    return [x[0][:r] for x in res]


ANY = pl.BlockSpec(memory_space=pl.ANY)


def _place():
    x, y, c = lax.axis_index("x"), lax.axis_index("y"), lax.axis_index("c")
    chips = [(1 - x, y), (x, 1 - y), (1 - x, 1 - y)]
    return x, y, c, chips


def _allgather_weights(arrs):
    n = len(arrs)

    def body(*refs):
        ins, outs, send_sems, recv_sems = refs[:n], refs[n:2 * n], refs[2 * n], refs[2 * n + 1]
        x, y, c, chips = _place()
        j = 2 * x + y
        me, sibling = (x, y, c), (x, y, 1 - c)

        def cp(i, k, src, chip_idx, half, to):
            return pltpu.make_async_remote_copy(src_ref=src, dst_ref=outs[i].at[chip_idx, half], send_sem=send_sems.at[k],
                                                recv_sem=recv_sems.at[k], device_id=to, device_id_type=MESH)

        first, passed, own = [], [], []
        for i in range(n):
            for r, (cx, cy) in enumerate(chips):
                first.append(cp(i, 3 * i + r, ins[i].at[c], j, c, (cx, cy, c)))
                passed.append(cp(i, 3 * (n + i) + r, outs[i].at[2 * cx + cy, c], 2 * cx + cy, c, sibling))
            own += [cp(i, 6 * n + 2 * i + half, ins[i].at[half], j, half, sibling) for half in range(2)]
        for d in first + own:
            d.start()
        for i in range(n):
            for r, (cx, cy) in enumerate(chips):
                cp(i, 3 * i + r, ins[i].at[c], 2 * cx + cy, c, me).wait_recv()
                passed[3 * i + r].start()
        for i in range(n):
            for r, (cx, cy) in enumerate(chips):
                cp(i, 3 * (n + i) + r, ins[i].at[c], 2 * cx + cy, 1 - c, me).wait_recv()
        for d in own:
            d.wait_recv()
        for d in first + passed + own:
            d.wait_send()

    return pl.pallas_call(
        body, name="allgather_weights", in_specs=[ANY] * n, out_specs=[ANY] * n,
        out_shape=[jax.ShapeDtypeStruct((N_CHIP,) + a.shape, a.dtype) for a in arrs],
        scratch_shapes=[pltpu.SemaphoreType.DMA((8 * n,)), pltpu.SemaphoreType.DMA((8 * n,))],
    )(*arrs)


def _sibling_swap(gs):
    n = len(gs)

    def body(*refs):
        layers, outs, send_sems, recv_sems = (refs[:n], refs[n:2 * n]), refs[2 * n:3 * n], refs[3 * n], refs[3 * n + 1]
        x, y, c, _ = _place()

        def copies(srcs):
            return [pltpu.make_async_remote_copy(src_ref=srcs[i], dst_ref=outs[i], send_sem=send_sems.at[i], recv_sem=recv_sems.at[i],
                                                 device_id=(x, y, 1 - c), device_id_type=MESH) for i in range(n)]

        for layer in range(DEPTH):
            @pl.when(c == 1 - layer)
            def _(layer=layer):
                for d in copies(layers[layer]):
                    d.start()

        waits = copies(layers[0])
        for d in waits:
            d.wait_recv()
        for d in waits:
            d.wait_send()

    return pl.pallas_call(
        body, name="grad_sibling_swap", in_specs=[ANY] * (2 * n), out_specs=[ANY] * n,
        out_shape=[jax.ShapeDtypeStruct(g0.shape, g0.dtype) for g0, _ in gs],
        scratch_shapes=[pltpu.SemaphoreType.DMA((n,)), pltpu.SemaphoreType.DMA((n,))],
    )(*[g0 for g0, _ in gs], *[g1 for _, g1 in gs])


def _chip_scatter(ps):
    n = len(ps)

    def body(*refs):
        ins, outs, send_sems, recv_sems = refs[:n], refs[n:2 * n], refs[2 * n], refs[2 * n + 1]
        x, y, c, chips = _place()
        sends = []
        for i in range(n):
            for r, (cx, cy) in enumerate(chips):
                sends.append(pltpu.make_async_remote_copy(src_ref=ins[i].at[2 * cx + cy], dst_ref=outs[i].at[r], send_sem=send_sems.at[3 * i + r],
                                                          recv_sem=recv_sems.at[3 * i + r], device_id=(cx, cy, c), device_id_type=MESH))
        for d in sends:
            d.start()
        for d in sends:
            d.wait_recv()
        for d in sends:
            d.wait_send()

    return pl.pallas_call(
        body, name="grad_chip_scatter", in_specs=[ANY] * n, out_specs=[ANY] * n,
        out_shape=[jax.ShapeDtypeStruct((3,) + p.shape[1:], p.dtype) for p in ps],
        scratch_shapes=[pltpu.SemaphoreType.DMA((3 * n,)), pltpu.SemaphoreType.DMA((3 * n,))],
    )(*ps)


def _sibling_share(rs):
    n = len(rs)

    def body(*refs):
        ins, outs, send_sems, recv_sems = refs[:n], refs[n:2 * n], refs[2 * n], refs[2 * n + 1]
        x, y, c, _ = _place()
        cps = [pltpu.make_async_remote_copy(src_ref=ins[i], dst_ref=outs[i], send_sem=send_sems.at[i], recv_sem=recv_sems.at[i],
                                            device_id=(x, y, 1 - c), device_id_type=MESH) for i in range(n)]
        for d in cps:
            d.start()
        for d in cps:
            d.wait_recv()
        for d in cps:
            d.wait_send()

    return pl.pallas_call(
        body, name="grad_sibling_share", in_specs=[ANY] * n, out_specs=[ANY] * n,
        out_shape=[jax.ShapeDtypeStruct(r.shape, r.dtype) for r in rs],
        scratch_shapes=[pltpu.SemaphoreType.DMA((n,)), pltpu.SemaphoreType.DMA((n,))],
    )(*rs)


def _allreduce_small(s):
    rows, w = s.shape
    n_dev = 8

    def body(s_ref, out_ref, slots, send_sems, recv_sems):
        x, y, c, _ = _place()
        me = 4 * x + 2 * y + c
        slots[me] = s_ref[...]
        peers = []
        for r in range(1, n_dev):
            px = 1 - x if r & 4 else x
            py = 1 - y if r & 2 else y
            pc = 1 - c if r & 1 else c
            peers.append((px, py, pc))
        sends = [pltpu.make_async_remote_copy(src_ref=s_ref, dst_ref=slots.at[me], send_sem=send_sems.at[r], recv_sem=recv_sems.at[r],
                                              device_id=peer, device_id_type=MESH) for r, peer in enumerate(peers)]
        for d in sends:
            d.start()
        for r, (px, py, pc) in enumerate(peers):
            pltpu.make_async_remote_copy(src_ref=s_ref, dst_ref=slots.at[4 * px + 2 * py + pc], send_sem=send_sems.at[r],
                                         recv_sem=recv_sems.at[r], device_id=(x, y, c), device_id_type=MESH).wait_recv()
        for d in sends:
            d.wait_send()
        acc = slots[0]
        for k in range(1, n_dev):
            acc = acc + slots[k]
        out_ref[...] = acc

    vm = pl.BlockSpec(memory_space=pltpu.VMEM)
    return pl.pallas_call(
        body, name="allreduce_small", in_specs=[vm], out_specs=vm, out_shape=jax.ShapeDtypeStruct((rows, w), F32),
        scratch_shapes=[pltpu.VMEM((n_dev, rows, w), F32), pltpu.SemaphoreType.DMA((n_dev - 1,)), pltpu.SemaphoreType.DMA((n_dev - 1,))],
    )(s)


W_IN_SHARD = D_IN // N_CHIP
W_IN_ROWS_G = 2304
REDUCED = tuple(m for m in MATS if m[0] != "conv_w")
CONV_W_SIZE = 3 * 2 * D_FF


def _weight_send(name, a):
    if name == "w_in":
        return jnp.swapaxes(a, 1, 2).astype(BF16)
    return a if name == "conv_w" else a.astype(BF16)


def _full_weights(gathered, l):
    g = {k: v[:, l] for k, v in gathered.items()}
    s = g["w_in"].astype(F32).reshape(D_IN, D_MODEL)
    dup = lambda a: jnp.concatenate([a[0:64], a[0:64], a[64:128], a[64:128]], axis=0)
    o = ORIG
    wm_t = jnp.concatenate([s[o["gate"]:], s[o["a"]:o["a"] + A_COLS], s[o["bq"]:o["bk"]], dup(s[o["bk"]:o["bv"]]), dup(s[o["bv"]:o["cq"]]),
                            s[o["cq"]:o["gate"]], jnp.zeros((M_COLS - M_CDKV - (o["gate"] - o["cdkv"]), D_MODEL), F32)], axis=0).astype(BF16)
    wg_t = [s[o["a"] + gi * A_COLS:o["a"] + (gi + 1) * A_COLS].astype(BF16) for gi in (1, 2)]
    full = {name: jnp.moveaxis(g[name], 0, ax).reshape(shape) for name, shape, ax in MATS if name != "w_in"}
    uq = full["w_uq"].reshape(C_Q_RANK, N_HEADS, C_NOPE + C_ROPE)
    ukv = full["w_ukv"].reshape(C_KV_RANK, N_HEADS, 2 * C_NOPE)
    w_uq_p = _pad_lanes(uq).reshape(C_Q_RANK, N_HEADS * LANE)
    w_ukv_p = jnp.concatenate([_pad_lanes(ukv[:, :, :C_NOPE]).reshape(C_KV_RANK, N_HEADS * LANE),
                               ukv[:, :, C_NOPE:].reshape(C_KV_RANK, N_HEADS * HEAD_DIM)], axis=1)
    return {"wm_t": wm_t, "wg_t": wg_t, "w_uq_p": w_uq_p, "w_ukv_p": w_ukv_p, "w_branch": full["w_branch"], "w_out": full["w_out"],
            "wup_g": full["w_ffn_up"][:, :D_FF], "wup_v": full["w_ffn_up"][:, D_FF:], "conv_w": full["conv_w"],
            "w_ffn_down": full["w_ffn_down"]}


def _grad_send(name, g, shape, ax):
    if name == "w_in":
        return jnp.pad(g.reshape(N_CHIP, W_IN_SHARD, D_MODEL), ((0, 0), (0, W_IN_ROWS_G - W_IN_SHARD), (0, 0)))
    split = shape[:ax] + (N_CHIP, shape[ax] // N_CHIP) + shape[ax + 1:]
    return jnp.moveaxis(g.reshape(split), ax, 0)


def _grad_recv(name, r):
    return r[:W_IN_SHARD].T if name == "w_in" else r


def _pack_small(rel, small, conv_w, extra):
    parts = [rel.reshape(-1)]
    for l in range(DEPTH):
        for name in SMALL:
            parts.append(small[name][l].reshape(-1))
    parts += [conv_w.reshape(-1), extra]
    flat = jnp.concatenate(parts)
    rows = _ceil_to(-(-flat.shape[0] // LANE), 8)
    return jnp.pad(flat, (0, rows * LANE - flat.shape[0])).reshape(rows, LANE)


def _unpack_small(buf):
    flat = buf.reshape(-1)
    rel = flat[:REL_BUCKETS * 32].reshape(REL_BUCKETS, 32)
    off = REL_BUCKETS * 32
    small = {name: [] for name in SMALL}
    for l in range(DEPTH):
        for name in SMALL:
            n = SMALL_SIZES[name]
            small[name].append(flat[off:off + n])
            off += n
    conv_w = flat[off:off + DEPTH * CONV_W_SIZE].reshape(DEPTH, 3, 2 * D_FF)
    off += DEPTH * CONV_W_SIZE
    return rel, {k: jnp.stack(v) for k, v in small.items()}, conv_w, flat[off:off + LANE]


def _rows2d(a, lead):
    return a.reshape(a.shape[:lead] + (-1, a.shape[-1]))


def _row_tile(rows):
    for cand in (512, 256, 128, 64, 32, 16, 8):
        if rows % cand == 0:
            return cand
    raise ValueError(rows)


def _pair_add(g, got, core, *, name):
    g0, g1, got2 = _rows2d(g[0], 0), _rows2d(g[1], 0), _rows2d(got, 0)
    rows, c = got2.shape
    flag = jnp.zeros((1, 1, LANE), F32) + core.astype(F32)

    def fn(i, nt, a0, a1, b, f):
        return jnp.where(f[:, 0:1] == 0.0, a0, a1) + b

    out = _rowwise(fn, [(g0[None], c, 0), (g1[None], c, 0), (got2[None], c, 0)], pars=[(flag, LANE, 0)],
                   outs=[(c, c, 0, BF16)], tm=_row_tile(rows), name=name)[0][0]
    return out.reshape(got.shape)


def _chip_add(own, got, *, name):
    own2, got2 = _rows2d(own, 0), _rows2d(got, 1)
    rows, c = own2.shape
    tm = _row_tile(rows)

    def fn(i, nt, a, b0, b1, b2):
        return ((a.astype(F32) + b0.astype(F32)) + b1.astype(F32)) + b2.astype(F32)

    stacked = got2.reshape(1, 3 * rows, c)
    out = _rowwise(fn, [(own2[None], c, 0)] + [(stacked, c, 0, k * (rows // tm)) for k in range(3)],
                   outs=[(c, c, 0, F32)], tm=tm, t=rows, name=name)[0][0]
    return out.reshape(own.shape)


def _perm(a, d):
    if d == 1:
        return a
    t = a.shape[0]
    return jnp.swapaxes(a.reshape((t // d, d) + a.shape[1:]), 0, 1).reshape(a.shape)


def _unperm(a, d):
    if d == 1:
        return a
    t = a.shape[0]
    return jnp.swapaxes(a.reshape((d, t // d) + a.shape[1:]), 0, 1).reshape(a.shape)


def _pad_lanes(a, w=LANE):
    return jnp.pad(a, [(0, 0)] * (a.ndim - 1) + [(0, w - a.shape[-1])])


def _rope_tables(t):
    pos = jnp.arange(t, dtype=F32)
    inv_freq = ROPE_BASE ** (-jnp.arange(0, C_ROPE, 2, dtype=F32) / C_ROPE)
    ang = pos[:, None] * inv_freq[None, :]
    cos, sin = jnp.cos(ang), jnp.sin(ang)
    ones, zeros = jnp.ones((t, C_NOPE), F32), jnp.zeros((t, C_NOPE), F32)
    tail = LANE - C_NOPE - C_ROPE
    c = jnp.concatenate([ones, cos, cos, ones[:, :tail]], axis=1)
    s = jnp.concatenate([zeros, -sin, sin, zeros[:, :tail]], axis=1)
    return c, s


def _band_calls(t, proj, projs_g, sinks):
    none = jnp.full((N_HEADS,), NEG, F32)
    a0 = M_A0 // LANE
    calls = [(proj, (a0, a0 + 4, a0 + 8), t // BLK, False, none)]
    calls += [(pg, (0, 4, 8), t // (d * BLK), False, none) for pg, d in zip(projs_g, A_DILS[1:])]
    calls.append((proj, (M_BQ // LANE, M_BK // LANE, M_BV // LANE), t // BLK, True, sinks.astype(F32)))
    return calls


def _layer_fwd(l, x, xb, w, p, biases, rope_cs):
    t = x.shape[0]
    n = f"l{l}_"
    xps = [_perm(xb, d) for d in A_DILS[1:]]
    proj = _mm(xb, w["wm_t"], tb=True, out_dtype=BF16, tm=TM_TOKENS, name=n + "proj")
    projs_g = [_mm(xp, wg, tb=True, out_dtype=BF16, tm=TM_TOKENS, tn=768, name=n + f"proj_g{i + 1}")
               for i, (xp, wg) in enumerate(zip(xps, w["wg_t"]))]
    s = {"xb": xb, "xps": xps, "proj": proj, "projs_g": projs_g}

    calls = _band_calls(t, proj, projs_g, p["sinks"])
    outs = [_band_fwd(src, offs, biases[i], sk, nb=nb, gqa=gqa, name=n + f"band{i}")
            for i, (src, offs, nb, gqa, sk) in enumerate(calls)]
    os_ = [_unperm(outs[gi][0], d) for gi, d in enumerate(A_DILS)]
    lses = [_unperm(outs[gi][1], d) for gi, d in enumerate(A_DILS)]
    o_a = _combine_fwd(os_, lses, name=n + "combine_fwd")
    o_b_f, lse_b = outs[3]
    o_b = o_b_f.astype(BF16)
    s.update(os=os_, lses=lses, lses_p=[outs[gi][1] for gi in range(3)], o_b=o_b_f, lse_b=lse_b)

    rq = _rms_fwd(proj, C_Q_RANK, M_CQ // C_Q_RANK, p["q_norm_g"], name=n + "rms_q")
    rkv = _rms_fwd(proj, C_KV_RANK, M_CDKV // C_KV_RANK, p["kv_norm_g"], name=n + "rms_kv")
    q_cp = _mm(rq, w["w_uq_p"], out_dtype=BF16, name=n + "uq")
    kv_cp = _mm(rkv, w["w_ukv_p"], out_dtype=BF16, name=n + "ukv")
    q_full = _rope_slabs(q_cp, N_HEADS, rope_cs[0], rope_cs[1], name=n + "rope_q")
    k_full = _rope_slabs(kv_cp, N_HEADS, rope_cs[0], rope_cs[1], add=(proj, (M_CDKV + C_KV_RANK) // LANE), name=n + "rope_k")
    vt = jnp.transpose(kv_cp[:, N_HEADS * LANE:].T.reshape(N_HEADS // 2, LANE, t // TQ, TQ), (0, 2, 1, 3))
    o_c_f, lse_c = _mla_fwd(q_full, k_full, vt, name=n + "mla_fwd")
    o_c = o_c_f.astype(BF16)
    s.update(rq=rq, rkv=rkv, q_full=q_full, k_full=k_full, kv_cp=kv_cp, lse_c=lse_c, o_c=o_c_f)

    obs = [o_a, o_b, o_c]
    ys = [_mm(o, w["w_branch"][i], out_dtype=BF16, name=n + f"branch{i}") for i, o in enumerate(obs)]
    merged = _merge_fwd(proj, p["b_gate"], ys, name=n + "merge")
    mix = _mm(merged, w["w_out"], name=n + "out")
    x1f, x1b, z1 = _ln_fwd(x, mix, p["ln1_g"], p["ln1_b"], name=n + "ln1")
    s.update(obs=obs, ys=ys, merged=merged, z1=z1, x1b=x1b)

    ug = _mm(x1b, w["wup_g"], tm=TM_TOKENS, tn=1408, out_dtype=BF16, name=n + "up_g")
    uv = _mm(x1b, w["wup_v"], tm=TM_TOKENS, tn=1408, out_dtype=BF16, name=n + "up_v")
    h = _glu_fwd(ug, uv, w["conv_w"], p["conv_b"], name=n + "glu")
    ff = _mm(h, w["w_ffn_down"], tm=TM_TOKENS, tk=1408, name=n + "down")
    x2f, x2b, z2 = _ln_fwd(x1f, ff, p["ln2_g"], p["ln2_b"], name=n + "ln2")
    s.update(ug=ug, uv=uv, h=h, z2=z2)
    return x2f, x2b, s


def _layer_bwd(l, s, dys, coefs, w, p, biases, rope_cs):
    n = f"l{l}b_"
    t = s["z2"].shape[0]
    gw, gs = {}, {}

    dz2, dz2b, gs["ln2_g"], gs["ln2_b"] = _ln_bwd(s["z2"], p["ln2_g"], dys, coefs, name=n + "ln2")
    dh = _mm(dz2b, w["w_ffn_down"], tb=True, tm=TM_TOKENS, tn=1408, out_dtype=BF16, name=n + "d_h")
    gw["w_ffn_down"] = _mm(s["h"], dz2b, ta=True, tm=1408, tk=1024, name=n + "g_down")
    dcg, dcv, gw["conv_w"], gs["conv_b"] = _glu_bwd_a(s["ug"], s["uv"], w["conv_w"], p["conv_b"], dh, name=n + "glu_a")
    dug = _glu_bwd_b(dcg, w["conv_w"], 0, name=n + "glu_bg")
    duv = _glu_bwd_b(dcv, w["conv_w"], 1, name=n + "glu_bv")
    dx1_g = _mm(dug, w["wup_g"], tb=True, tm=TM_TOKENS, tk=1408, name=n + "d_x1g")
    dx1_v = _mm(duv, w["wup_v"], tb=True, tm=TM_TOKENS, tk=1408, name=n + "d_x1v")
    gw["w_ffn_up"] = jnp.concatenate([_mm(s["x1b"], dug, ta=True, tn=1408, tk=1024, name=n + "g_upg"),
                                      _mm(s["x1b"], duv, ta=True, tn=1408, tk=1024, name=n + "g_upv")], axis=1)

    dz1, dz1b, gs["ln1_g"], gs["ln1_b"] = _ln_bwd(s["z1"], p["ln1_g"], [dz2, dx1_g, dx1_v], [ALPHA, 1.0, 1.0], name=n + "ln1")
    dmerged = _mm(dz1b, w["w_out"], tb=True, out_dtype=BF16, name=n + "d_merged")
    gw["w_out"] = _mm(s["merged"], dz1b, ta=True, name=n + "g_out")
    dys_b, dgp, gs["b_gate"] = _merge_bwd(s["proj"], p["b_gate"], s["ys"], dmerged, name=n + "merge")
    dos = [_mm(dy, w["w_branch"][i], tb=True, out_dtype=BF16, name=n + f"d_o{i}") for i, dy in enumerate(dys_b)]
    gw["w_branch"] = jnp.stack([_mm(o, dy, ta=True, name=n + f"g_branch{i}") for i, (o, dy) in enumerate(zip(s["obs"], dys_b))])

    do_gs, dpr_gs = _combine_bwd(s["os"], s["lses"], dos[0], name=n + "combine")
    do_b, dpr_b = _delta(dos[1], s["o_b"], name=n + "delta_b")
    do_list = [_perm(a, d) for a, d in zip(do_gs, A_DILS)] + [do_b]
    dpr_list = [_perm(a, d) for a, d in zip(dpr_gs, A_DILS)] + [dpr_b]
    lse_list = s["lses_p"] + [s["lse_b"]]
    calls = _band_calls(t, s["proj"], s["projs_g"], p["sinks"])
    band = [_band_bwd(src, offs, do_list[i], lse_list[i], dpr_list[i], biases[i], sk, nb=nb, gqa=gqa, name=n + f"band{i}")
            for i, (src, offs, nb, gqa, sk) in enumerate(calls)]
    gs["sinks"] = band[3][4][:, 0, 0]
    ds_sum = jnp.concatenate([b_[3] for b_ in band], axis=0)

    do_c, delta_c = _delta(dos[2], s["o_c"], name=n + "delta_c")
    dq, dk, dv = _mla_bwd(s["q_full"], s["k_full"], s["kv_cp"], do_c, s["lse_c"], delta_c, name=n + "mla")
    dq_cp = _rope_slabs(dq, N_HEADS, rope_cs[0], -rope_cs[1], name=n + "rope_q")
    dk_sum = _rowwise(lambda i, nt, *vs: sum(vs[1:], vs[0]), [(dk[None], LANE, hh) for hh in range(N_HEADS)],
                      outs=[(LANE, LANE, 0, F32)], tm=1024, name=n + "krope_sum")[0][0]
    dkr = _rope_slabs(dk_sum, 1, rope_cs[0], -rope_cs[1], to_front=True, name=n + "rope_k")
    dkv_cp = jnp.concatenate([dk, dv], axis=1)
    d_rq = _mm(dq_cp, w["w_uq_p"], tb=True, name=n + "d_rq")
    d_rkv = _mm(dkv_cp, w["w_ukv_p"], tb=True, name=n + "d_rkv")
    g_uq = _mm(s["rq"], dq_cp, ta=True, name=n + "g_uq")
    g_ukv = _mm(s["rkv"], dkv_cp, ta=True, name=n + "g_ukv")
    gw["w_uq"] = g_uq.reshape(C_Q_RANK, N_HEADS, LANE)[:, :, :C_NOPE + C_ROPE].reshape(C_Q_RANK, -1)
    kw = N_HEADS * LANE
    gw["w_ukv"] = jnp.concatenate([g_ukv[:, :kw].reshape(C_KV_RANK, N_HEADS, LANE)[:, :, :C_NOPE],
                                   g_ukv[:, kw:].reshape(C_KV_RANK, N_HEADS, HEAD_DIM)], axis=2).reshape(C_KV_RANK, -1)
    dcq, gs["q_norm_g"] = _rms_bwd(s["proj"], C_Q_RANK, M_CQ // C_Q_RANK, p["q_norm_g"], d_rq, name=n + "rms_q")
    dckv, gs["kv_norm_g"] = _rms_bwd(s["proj"], C_KV_RANK, M_CDKV // C_KV_RANK, p["kv_norm_g"], d_rkv, name=n + "rms_kv")
    dcdkv = jnp.concatenate([dckv, dkr], axis=1)

    dproj = jnp.concatenate(dgp + list(band[0][:3]) + list(band[3][:3]) + [dcq, dcdkv], axis=1)
    dprojs_g = [jnp.concatenate(band[gi][:3], axis=1) for gi in (1, 2)]
    dx_terms = [_mm(dproj, w["wm_t"], tm=TM_TOKENS, tk=1024, name=n + "d_x")]
    dx_terms += [_unperm(_mm(dp, wg, tm=TM_TOKENS, tk=768, name=n + f"d_x_g{i + 1}"), d)
                 for i, (dp, wg, d) in enumerate(zip(dprojs_g, w["wg_t"], A_DILS[1:]))]
    g_main = _mm(dproj, s["xb"], ta=True, name=n + "g_in")
    g_groups = [_mm(dp, xp, ta=True, tm=768, name=n + f"g_in_g{i + 1}") for i, (xp, dp) in enumerate(zip(s["xps"], dprojs_g))]
    fold = lambda a, tag: _sum_rows([a.reshape(2, 2, HEAD_DIM, D_MODEL)[:, j] for j in range(2)], tm=HEAD_DIM,
                                    name=n + "g_fold_" + tag).reshape(2 * HEAD_DIM, D_MODEL)
    gw["w_in"] = jnp.concatenate([g_main[M_A0:M_BQ], g_groups[0], g_groups[1], g_main[M_BQ:M_BK], fold(g_main[M_BK:M_BV], "k"),
                                  fold(g_main[M_BV:M_CQ], "v"), g_main[M_CQ:M_CDKV + C_KV_RANK + C_ROPE], g_main[M_GATE:M_A0]], axis=0)
    return [dz1] + dx_terms, [ALPHA, 1.0, 1.0, 1.0], gw, gs, ds_sum


def _local_step(x, target, ws, rel_table, small):
    t = x.shape[0]
    ps = [{k: small[k][l] for k in SMALL} for l in range(DEPTH)]
    bucket = _bucket_index()
    bias_all = _bias_lookup(bucket, rel_table.T, name="bias_lookup").reshape(4, N_HEADS, BLK, 2 * BLK)
    step = BLK + jnp.arange(BLK)[:, None] - jnp.arange(2 * BLK)[None, :]
    biases = [jnp.where((step >= 0) & (step <= lim), bias_all[i], NEG) for i, lim in enumerate((BLK, BLK, BLK, BLK - 1))]
    rope_cs = _rope_tables(t)

    saved, h, hb = [], x, x.astype(BF16)
    for l in range(DEPTH):
        h, hb, s = _layer_fwd(l, h, hb, ws[l], ps[l], biases, rope_cs)
        saved.append(s)
    dy, loss_part = _loss_and_grad(h, target, name="loss")

    dys, coefs = [dy], [1.0]
    gws, gss, dss = [None] * DEPTH, [None] * DEPTH, [None] * DEPTH
    for l in reversed(range(DEPTH)):
        dys, coefs, gws[l], gss[l], dss[l] = _layer_bwd(l, saved[l], dys, coefs, ws[l], ps[l], biases, rope_cs)
    grad_x = _lincomb(dys, coefs, name="grad_x")
    npos = 2 * BLK * BLK
    g_rel = _bias_grad(bucket, dss[0].reshape(4 * N_HEADS, npos), dss[1].reshape(4 * N_HEADS, npos), name="bias_grad").T
    gsmall = {k: jnp.stack([gss[l][k] for l in range(DEPTH)]) for k in SMALL}
    return loss_part, grad_x, gws, gsmall, g_rel


def kernel(x, rel_table, w_in, b_gate, sinks, q_norm_g, kv_norm_g, w_uq, w_ukv, w_branch, w_out, ln1_g, ln1_b, w_ffn_up, conv_w, conv_b, w_ffn_down, ln2_g, ln2_b, loss_target, m_rel_table, m_w_in, m_b_gate, m_sinks, m_q_norm_g, m_kv_norm_g, m_w_uq, m_w_ukv, m_w_branch, m_w_out, m_ln1_g, m_ln1_b, m_w_ffn_up, m_conv_w, m_conv_b, m_w_ffn_down, m_ln2_g, m_ln2_b, v_rel_table, v_w_in, v_b_gate, v_sinks, v_q_norm_g, v_kv_norm_g, v_w_uq, v_w_ukv, v_w_branch, v_w_out, v_ln1_g, v_ln1_b, v_w_ffn_up, v_conv_w, v_conv_b, v_w_ffn_down, v_ln2_g, v_ln2_b):
    wts = dict(rel_table=rel_table, w_in=w_in, b_gate=b_gate, sinks=sinks, q_norm_g=q_norm_g, kv_norm_g=kv_norm_g, w_uq=w_uq,
               w_ukv=w_ukv, w_branch=w_branch, w_out=w_out, ln1_g=ln1_g, ln1_b=ln1_b, w_ffn_up=w_ffn_up, conv_w=conv_w,
               conv_b=conv_b, w_ffn_down=w_ffn_down, ln2_g=ln2_g, ln2_b=ln2_b)
    ms = dict(rel_table=m_rel_table, w_in=m_w_in, b_gate=m_b_gate, sinks=m_sinks, q_norm_g=m_q_norm_g, kv_norm_g=m_kv_norm_g,
              w_uq=m_w_uq, w_ukv=m_w_ukv, w_branch=m_w_branch, w_out=m_w_out, ln1_g=m_ln1_g, ln1_b=m_ln1_b, w_ffn_up=m_w_ffn_up,
              conv_w=m_conv_w, conv_b=m_conv_b, w_ffn_down=m_w_ffn_down, ln2_g=m_ln2_g, ln2_b=m_ln2_b)
    vs = dict(rel_table=v_rel_table, w_in=v_w_in, b_gate=v_b_gate, sinks=v_sinks, q_norm_g=v_q_norm_g, kv_norm_g=v_kv_norm_g,
              w_uq=v_w_uq, w_ukv=v_w_ukv, w_branch=v_w_branch, w_out=v_w_out, ln1_g=v_ln1_g, ln1_b=v_ln1_b, w_ffn_up=v_w_ffn_up,
              conv_w=v_conv_w, conv_b=v_conv_b, w_ffn_down=v_w_ffn_down, ln2_g=v_ln2_g, ln2_b=v_ln2_b)

    core = lax.axis_index("c")
    chip = 2 * lax.axis_index("x") + lax.axis_index("y")

    names = [name for name, _, _ in MATS]
    gathered = dict(zip(names, _allgather_weights([_weight_send(name, wts[name]) for name in names])))
    ws = [_full_weights(gathered, l) for l in range(DEPTH)]

    small = {k: wts[k] for k in SMALL}
    loss_part, grad_x, gws, gsmall, g_rel = _local_step(x[0], loss_target[0], ws, rel_table, small)

    rnames = [name for name, _, _ in REDUCED]
    gsend = [tuple(_grad_send(name, gws[l][name], shape, ax) for l in range(DEPTH)) for name, shape, ax in REDUCED]
    theirs = _sibling_swap(gsend)
    pairs = [_pair_add(g, t_, core, name="grad_pair_" + name) for name, g, t_ in zip(rnames, gsend, theirs)]
    arrived = _chip_scatter(pairs)
    reduced = [_chip_add(lax.dynamic_index_in_dim(p, chip, 0, keepdims=False), a, name="grad_chip_" + name)
               for name, p, a in zip(rnames, pairs, arrived)]
    others = _sibling_share(reduced)
    gshard = {}
    for name, mine, other in zip(rnames, reduced, others):
        layers = [jnp.where(core == l, mine, other) for l in range(DEPTH)]
        gshard[name] = jnp.stack([_grad_recv(name, a) for a in layers])

    conv_w_full = jnp.stack([gws[l]["conv_w"] for l in range(DEPTH)])
    small_red = _allreduce_small(_pack_small(g_rel, gsmall, conv_w_full, loss_part))
    g_rel_r, gsmall_r, conv_w_r, loss_vec = _unpack_small(small_red)
    loss = loss_vec[0]
    shard_w = 2 * D_FF // N_CHIP
    gshard["conv_w"] = lax.dynamic_slice_in_dim(conv_w_r, chip * shard_w, shard_w, axis=2)

    grads = dict(gshard)
    grads.update(gsmall_r)
    grads["rel_table"] = g_rel_r
    deltas, new_m, new_v = {}, {}, {}
    for name, _, _ in MATS:
        shp = wts[name].shape
        v2 = lambda a: a.reshape(-1, shp[-1])
        d_, m_, v_ = _adamw(v2(wts[name]), v2(grads[name]), v2(ms[name]), v2(vs[name]), name="adamw_" + name)
        deltas[name], new_m[name], new_v[name] = d_.reshape(shp), m_.reshape(shp), v_.reshape(shp)
    zero, none = jnp.zeros((LANE,), F32), jnp.zeros((0,), F32)
    sw = _pack_small(wts["rel_table"], {k: wts[k] for k in SMALL}, none, zero)
    sm = _pack_small(ms["rel_table"], {k: ms[k] for k in SMALL}, none, zero)
    sv = _pack_small(vs["rel_table"], {k: vs[k] for k in SMALL}, none, zero)
    sg = _pack_small(g_rel_r, gsmall_r, none, zero)
    sd, smn, svn = _adamw(sw, sg, sm, sv, name="adamw_small")
    for res, buf in ((deltas, sd), (new_m, smn), (new_v, svn)):
        rel_, sm_ = _unpack_small(jnp.pad(buf, ((0, small_red.shape[0] - buf.shape[0]), (0, 0))))[:2]
        res["rel_table"] = rel_
        res.update(sm_)

    return (loss, grad_x[None], *[grads[k] for k in WEIGHT_ORDER], *[deltas[k] for k in WEIGHT_ORDER],
            *[new_m[k] for k in WEIGHT_ORDER], *[new_v[k] for k in WEIGHT_ORDER])
```

```python
import math

import jax
import jax.numpy as jnp
from jax import lax
from jax.experimental import pallas as pl
from jax.experimental.pallas import tpu as pltpu

F32 = jnp.float32
BF16 = jnp.bfloat16
MESH = pl.DeviceIdType.MESH

D_MODEL = 1024
DEPTH = 2
HEAD_DIM = 64
N_HEADS = 8
A_DILS = (1, 4, 16)
C_Q_RANK = 256
C_KV_RANK = 128
C_NOPE = 64
C_ROPE = 32
ROPE_BASE = 10000.0
REL_BUCKETS = 32
REL_MAX_DIST = 2048
D_FF = 2816
ALPHA = (2 * DEPTH) ** 0.25
LN_EPS = 1e-5
RMS_EPS = 1e-6
NEG = -1e30
LOG2E, LN2 = math.log2(math.e), math.log(2.0)
ADAM_LR, ADAM_B1, ADAM_B2, ADAM_EPS, ADAM_WD, ADAM_STEP = 0.001, 0.9, 0.999, 1e-08, 0.01, 10

VMEM_LIMIT_BYTES = 56 * 1024 * 1024
LANE = 128
BLK = 128
TQ = 512
TQ_FWD = 1024
TM_TOKENS = 2048
HALO = 16
BAND_SCALE = HEAD_DIM ** -0.5
BAND_UNROLL = 16

D_IN = 8864
A_COLS = 3 * N_HEADS * HEAD_DIM
ORIG = {"a": 0, "bq": 4608, "bk": 5120, "bv": 5248, "cq": 5376, "cdkv": 5632, "gate": 5792}
M_GATE, M_A0, M_BQ, M_BK, M_BV, M_CQ, M_CDKV, M_COLS = 0, 3072, 4608, 5120, 5376, 5632, 5888, 6144

N_CHIP = 4
MATS = (
    ("w_in", (D_MODEL, D_IN), 1),
    ("w_uq", (C_Q_RANK, 768), 1),
    ("w_ukv", (C_KV_RANK, 1024), 1),
    ("w_branch", (3, 512, D_MODEL), 2),
    ("w_out", (D_MODEL, D_MODEL), 0),
    ("w_ffn_up", (D_MODEL, 2 * D_FF), 1),
    ("conv_w", (3, 2 * D_FF), 1),
    ("w_ffn_down", (D_FF, D_MODEL), 0),
)
SMALL = ("b_gate", "sinks", "q_norm_g", "kv_norm_g", "ln1_g", "ln1_b", "conv_b", "ln2_g", "ln2_b")
SMALL_SIZES = {"b_gate": 3072, "sinks": 8, "q_norm_g": 256, "kv_norm_g": 128, "ln1_g": 1024, "ln1_b": 1024,
               "conv_b": 5632, "ln2_g": 1024, "ln2_b": 1024}
WEIGHT_ORDER = ("rel_table", "w_in", "b_gate", "sinks", "q_norm_g", "kv_norm_g", "w_uq", "w_ukv", "w_branch",
                "w_out", "ln1_g", "ln1_b", "w_ffn_up", "conv_w", "conv_b", "w_ffn_down", "ln2_g", "ln2_b")


def _cparams(sem):
    return pltpu.CompilerParams(dimension_semantics=sem, vmem_limit_bytes=VMEM_LIMIT_BYTES)


def _ceil_to(n, m):
    return -(-n // m) * m


def _pick(n, target):
    if n <= target:
        return n
    best = None
    for t in range(LANE, target + 1, LANE):
        if n % t == 0:
            best = t
    assert best is not None, (n, target)
    return best


def _mm(a, b, *, ta=False, tb=False, out_dtype=F32, tm=1024, tn=1024, tk=2048, name):
    assert not (ta and tb)
    k, m = a.shape[::-1] if not ta else a.shape
    n = b.shape[0] if tb else b.shape[1]
    assert (b.shape[1] if tb else b.shape[0]) == k
    tm, tn, tk = _pick(m, tm), _pick(n, tn), _pick(k, tk)
    nk = k // tk
    dn = (((0 if ta else 1,), (1 if tb else 0,)), ((), ()))

    def body(a_ref, b_ref, o_ref, acc_ref):
        part = lax.dot_general(a_ref[...].astype(BF16), b_ref[...].astype(BF16), dn, preferred_element_type=F32)
        if nk == 1:
            o_ref[...] = part.astype(o_ref.dtype)
        else:
            kk = pl.program_id(2)

            @pl.when(kk == 0)
            def _():
                acc_ref[...] = part

            @pl.when(kk > 0)
            def _():
                acc_ref[...] += part

            @pl.when(kk == nk - 1)
            def _():
                o_ref[...] = acc_ref[...].astype(o_ref.dtype)

    a_spec = pl.BlockSpec((tk, tm), lambda i, j, kk: (kk, i)) if ta else pl.BlockSpec((tm, tk), lambda i, j, kk: (i, kk))
    b_spec = pl.BlockSpec((tn, tk), lambda i, j, kk: (j, kk)) if tb else pl.BlockSpec((tk, tn), lambda i, j, kk: (kk, j))
    return pl.pallas_call(
        body, name=name, grid=(m // tm, n // tn, nk),
        in_specs=[a_spec, b_spec],
        out_specs=pl.BlockSpec((tm, tn), lambda i, j, kk: (i, j)),
        out_shape=jax.ShapeDtypeStruct((m, n), out_dtype),
        scratch_shapes=[pltpu.VMEM((tm, tn) if nk > 1 else (8, LANE), F32)],
        compiler_params=_cparams(("parallel", "parallel", "arbitrary")),
    )(a, b)


def _rowwise(fn, rows, *, pars=(), halos=(), outs=(), accs=(), tm, name, ncol=1, t=None):
    nb = rows[0][0].shape[0]
    t = rows[0][0].shape[1] if t is None else t
    tm = min(tm, t)
    assert t % tm == 0 and tm % 8 == 0
    nt = t // tm
    in_specs, args = [], []
    for spec in rows:
        arr, c, off = spec[:3]
        rb = spec[3] if len(spec) > 3 else 0
        in_specs.append(pl.BlockSpec((1, tm, c), lambda b, cc, i, off=off, rb=rb: (b, i + rb, off + cc)))
        args.append(arr)
    for arr, c, off, kind in halos:
        if kind == "prev":
            im = lambda b, cc, i, off=off: (b, jnp.maximum(i * (tm // HALO) - 1, 0), off + cc)
        else:
            im = lambda b, cc, i, off=off: (b, jnp.minimum((i + 1) * (tm // HALO), t // HALO - 1), off + cc)
        in_specs.append(pl.BlockSpec((1, HALO, c), im))
        args.append(arr)
    for arr, c, off in pars:
        bp, r = arr.shape[:2]
        if bp > 1:
            im = lambda b, cc, i, off=off: (b, 0, off + cc)
        else:
            im = lambda b, cc, i, off=off: (0, 0, off + cc)
        in_specs.append(pl.BlockSpec((1, r, c), im))
        args.append(arr)
    out_specs, out_shapes = [], []
    for ctot, c, off, dt in outs:
        out_specs.append(pl.BlockSpec((1, tm, c), lambda b, cc, i, off=off: (b, i, off + cc)))
        out_shapes.append(jax.ShapeDtypeStruct((nb, t, ctot), dt))
    for r, ctot, c, off in accs:
        out_specs.append(pl.BlockSpec((1, r, c), lambda b, cc, i, off=off: (b, 0, off + cc)))
        out_shapes.append(jax.ShapeDtypeStruct((nb, r, ctot), F32))
    n_in, n_out = len(args), len(outs)

    def body(*refs):
        i = pl.program_id(2)
        res = fn(i, nt, *[r[0].astype(F32) for r in refs[:n_in]])
        if not isinstance(res, (tuple, list)):
            res = (res,)
        for o_ref, val in zip(refs[n_in:n_in + n_out], res[:n_out]):
            o_ref[0] = val.astype(o_ref.dtype)
        for a_ref, val in zip(refs[n_in + n_out:], res[n_out:]):
            @pl.when(i == 0)
            def _(a_ref=a_ref, val=val):
                a_ref[0] = val

            @pl.when(i > 0)
            def _(a_ref=a_ref, val=val):
                a_ref[0] += val

    res = pl.pallas_call(
        body, name=name, grid=(nb, ncol, nt), in_specs=in_specs, out_specs=out_specs, out_shape=out_shapes,
        compiler_params=_cparams(("parallel", "parallel", "arbitrary")),
    )(*args)
    return res


def _dot(a, b):
    return lax.dot_general(a, b, (((1,), (0,)), ((), ())), preferred_element_type=F32)


def _dot_nt(a, b):
    return lax.dot_general(a, b, (((1,), (1,)), ((), ())), preferred_element_type=F32)


def _dot_tn(a, b):
    return lax.dot_general(a, b, (((0,), (0,)), ((), ())), preferred_element_type=F32)


def _rows(parts):
    return jnp.concatenate(parts, axis=0)


def _lane_lo():
    return lax.broadcasted_iota(jnp.int32, (1, LANE), 1) < HEAD_DIM


def _blocks(a):
    return [a[i * BLK:(i + 1) * BLK] for i in range(a.shape[0] // BLK)]


def _band_geometry(t):
    rows = min(BAND_UNROLL, t // BLK) * BLK
    assert t % rows == 0
    return rows, t // rows


def _band_operands(g, k_ref, v_ref, rows):
    start = pl.multiple_of(g * rows, rows)
    pstart = pl.multiple_of(jnp.maximum(g * rows - BLK, 0), BLK)
    out = []
    for ref in (k_ref, v_ref):
        cur = _blocks(ref[pl.ds(start, rows), :])
        raw = ref[pl.ds(pstart, rows), :]
        shifted = _rows([jnp.zeros((BLK, LANE), raw.dtype), raw[:rows - BLK]])
        prev = _blocks(jnp.where(g == 0, shifted, raw))
        out.append([_rows([p, c]) for p, c in zip(prev, cur)])
    return out


def _band_scores(g, qa, kk, b_ref, a, nb):
    u = len(qa)
    assert nb % u == 0 or u % nb == 0
    firsts = []
    for i in range(u):
        if nb >= u:
            val = jnp.where(lax.rem(g * u, nb) == 0, NEG, 0.0).astype(F32) if i == 0 else 0.0
        else:
            val = NEG if i % nb == 0 else 0.0
        firsts.append(jnp.zeros((BLK, 1), F32) + val)
    prev_slots = lax.broadcasted_iota(jnp.int32, (1, 2 * BLK), 1) < BLK
    return (_rows([_dot_nt(q, k) for q, k in zip(qa, kk)]) + _rows([b_ref[a]] * u)
            + jnp.where(prev_slots, _rows(firsts), 0.0))


def _band_fwd(src, offs, bias, sinks, *, nb, gqa, name):
    t = src.shape[0]
    rows, nstep = _band_geometry(t)
    qo, ko, vo = offs
    share = 2 if gqa else 1

    def body(sink_ref, q_ref, k_ref, v_ref, b_ref, o_ref, lse_ref):
        hp, g = pl.program_id(0), pl.program_id(1)
        lo = _lane_lo()
        q2 = q_ref[...]
        kk, vv = _band_operands(g, k_ref, v_ref, rows)
        outs, lses = [], []
        for a in range(2):
            sink = sink_ref[2 * hp + a]
            qa = _blocks(jnp.where(lo if a == 0 else jnp.logical_not(lo), q2, jnp.zeros_like(q2)) * BAND_SCALE)
            s = _band_scores(g, qa, kk, b_ref, a, nb)
            m = jnp.maximum(jnp.max(s, axis=1, keepdims=True), sink)
            p = jnp.exp(s - m)
            l = jnp.sum(p, axis=1, keepdims=True) + jnp.exp(sink - m)
            p_b = _blocks((p * (1.0 / l)).astype(BF16))
            outs.append(_rows([_dot(pb, v) for pb, v in zip(p_b, vv)]))
            lses.append(m + jnp.log(l))
        o_ref[...] = jnp.where(lo, outs[0], outs[1])
        lse_ref[...] = jnp.where(lo, lses[0], lses[1])

    slab = lambda off: pl.BlockSpec((rows, LANE), lambda hp, g: (g, off + hp))
    whole = lambda off: pl.BlockSpec((t, LANE), lambda hp, g: (0, off + hp // share))
    return pl.pallas_call(
        body, name=name, grid=(N_HEADS // 2, nstep),
        in_specs=[pl.BlockSpec(memory_space=pltpu.SMEM), slab(qo), whole(ko), whole(vo),
                  pl.BlockSpec((2, BLK, 2 * BLK), lambda hp, g: (hp, 0, 0))],
        out_specs=[slab(0), slab(0)],
        out_shape=[jax.ShapeDtypeStruct((t, N_HEADS * HEAD_DIM), F32)] * 2,
        compiler_params=_cparams(("parallel", "parallel")),
    )(sinks, src, src, src, bias)


def _band_bwd(src, offs, do, lse, dpr, bias, sinks, *, nb, gqa, name):
    t = src.shape[0]
    rows, nstep = _band_geometry(t)
    qo, ko, vo = offs
    share = 2 if gqa else 1

    def fold(a):
        acc = a[0:BLK]
        for i in range(1, rows // BLK):
            acc = acc + a[i * BLK:(i + 1) * BLK]
        return acc

    def body(sink_ref, q_ref, k_ref, v_ref, do_ref, lse_ref, dpr_ref, b_ref,
             dq_ref, dk_ref, dv_ref, ds_ref, dsink_ref, dk_acc, dv_acc):
        hp, g = pl.program_id(0), pl.program_id(1)
        lo = _lane_lo()
        hi = jnp.logical_not(lo)

        @pl.when(jnp.logical_and(g == 0, lax.rem(hp, share) == 0))
        def _():
            dk_acc[...] = jnp.zeros_like(dk_acc)
            dv_acc[...] = jnp.zeros_like(dv_acc)

        @pl.when(g == 0)
        def _():
            ds_ref[...] = jnp.zeros_like(ds_ref)
            dsink_ref[...] = jnp.zeros_like(dsink_ref)

        q2, do2, lse2, dpr2 = q_ref[...], do_ref[...], lse_ref[...], dpr_ref[...]
        lse_sw, dpr_sw = pltpu.roll(lse2, HEAD_DIM, axis=1), pltpu.roll(dpr2, HEAD_DIM, axis=1)
        kk, vv = _band_operands(g, k_ref, v_ref, rows)
        dqs, per_head = [], []
        for a in range(2):
            sink = sink_ref[2 * hp + a]
            mine = lo if a == 0 else hi
            qa = _blocks(jnp.where(mine, q2, jnp.zeros_like(q2)) * BAND_SCALE)
            doa = _blocks(jnp.where(mine, do2, jnp.zeros_like(do2)))
            lse_a, dpr_a = jnp.where(mine, lse2, lse_sw), jnp.where(mine, dpr2, dpr_sw)
            wide = lambda x: jnp.concatenate([x, x], axis=1)
            p = jnp.exp(_band_scores(g, qa, kk, b_ref, a, nb) - wide(lse_a))
            ds = p * (_rows([_dot_nt(d, v) for d, v in zip(doa, vv)]) - wide(dpr_a))
            ds_ref[a] += fold(ds)
            dsink_ref[a] -= jnp.sum(jnp.exp(sink - lse_a) * dpr_a, axis=0, keepdims=True)
            ds_b, p_b = _blocks(ds.astype(BF16)), _blocks(p.astype(BF16))
            dqs.append(_rows([_dot(d, k) for d, k in zip(ds_b, kk)]))
            per_head.append((ds_b, p_b, qa, doa))
        dq_ref[...] = (jnp.where(lo, dqs[0], dqs[1]) * BAND_SCALE).astype(dq_ref.dtype)
        (ds0, p0, qa0, do0), (ds1, p1, qa1, do1) = per_head
        for i in range(len(qa0)):
            at = pl.ds(pl.multiple_of(g * rows + i * BLK, BLK), 2 * BLK)
            dk_acc[at, :] += _dot_tn(_rows([ds0[i], ds1[i]]), _rows([qa0[i], qa1[i]]))
            dv_acc[at, :] += _dot_tn(_rows([p0[i], p1[i]]), _rows([do0[i], do1[i]]))

        @pl.when(g == nstep - 1)
        def _():
            dk_ref[...] = dk_acc[BLK:, :].astype(dk_ref.dtype)
            dv_ref[...] = dv_acc[BLK:, :].astype(dv_ref.dtype)

    slab = lambda off: pl.BlockSpec((rows, LANE), lambda hp, g: (g, off + hp))
    whole = lambda off: pl.BlockSpec((t, LANE), lambda hp, g: (0, off + hp // share))
    per_pair = lambda shp: pl.BlockSpec((2,) + shp, lambda hp, g: (hp,) + (0,) * len(shp))
    kv_cols = N_HEADS * HEAD_DIM // share
    return pl.pallas_call(
        body, name=name, grid=(N_HEADS // 2, nstep),
        in_specs=[pl.BlockSpec(memory_space=pltpu.SMEM), slab(qo), whole(ko), whole(vo), slab(0), slab(0), slab(0),
                  per_pair((BLK, 2 * BLK))],
        out_specs=[slab(0), whole(0), whole(0), per_pair((BLK, 2 * BLK)), per_pair((1, LANE))],
        out_shape=[jax.ShapeDtypeStruct((t, N_HEADS * HEAD_DIM), BF16), jax.ShapeDtypeStruct((t, kv_cols), BF16),
                   jax.ShapeDtypeStruct((t, kv_cols), BF16), jax.ShapeDtypeStruct((N_HEADS, BLK, 2 * BLK), F32),
                   jax.ShapeDtypeStruct((N_HEADS, 1, LANE), F32)],
        scratch_shapes=[pltpu.VMEM((t + BLK, LANE), F32), pltpu.VMEM((t + BLK, LANE), F32)],
        compiler_params=_cparams(("arbitrary", "arbitrary")),
    )(sinks, src, src, src, do, lse, dpr, bias)


MLA_V_OFF = N_HEADS


def _diag_mask(size, keys_on_rows=False):
    rows, cols = lax.broadcasted_iota(jnp.int32, (size, size), 0), lax.broadcasted_iota(jnp.int32, (size, size), 1)
    return rows <= cols if keys_on_rows else cols <= rows


def _mla_specs(t, tq):
    blk = lambda f: pl.BlockSpec((tq, LANE), lambda hp, qi, f=f: (qi, f(hp)))
    whole = lambda f: pl.BlockSpec((t, LANE), lambda hp, qi, f=f: (0, f(hp)))
    return blk, whole


def _mla_fwd(q, k, vt, *, name):
    t = q.shape[0]
    n, tq = vt.shape[1], vt.shape[3]
    scale = (C_NOPE + C_ROPE) ** -0.5

    def body(q0_ref, q1_ref, k0_ref, k1_ref, vt_ref, o_ref, lse_ref, m_ref, acc_ref):
        qi = pl.program_id(1)
        qs, ks = (q0_ref[...], q1_ref[...]), (k0_ref, k1_ref)
        m_ref[...] = jnp.full_like(m_ref, NEG)
        acc_ref[...] = jnp.zeros_like(acc_ref)
        first = lax.broadcasted_iota(jnp.int32, (LANE, 1), 0) < HEAD_DIM

        def step(kj, diagonal):
            rows = pl.ds(pl.multiple_of(kj * tq, tq), tq)
            vtb = vt_ref[0, kj]
            one = jnp.ones_like(vtb)
            vts = (jnp.where(first, vtb, one), jnp.where(first, one, vtb))
            for a in range(2):
                s = _dot_nt(ks[a][rows, :], qs[a]) * (scale * LOG2E)
                if diagonal:
                    s = jnp.where(_diag_mask(tq, keys_on_rows=True), s, NEG)
                m_prev = m_ref[a]
                m_new = jnp.maximum(m_prev, jnp.max(s, axis=0, keepdims=True))
                acc_ref[a] = jnp.exp2(m_prev - m_new) * acc_ref[a] + _dot(vts[a], jnp.exp2(s - m_new).astype(BF16))
                m_ref[a] = m_new

        def kloop(kj, c2):
            step(kj, False)
            return c2

        lax.fori_loop(0, qi, kloop, 0)
        step(qi, True)
        l0, l1 = acc_ref[0, HEAD_DIM:HEAD_DIM + 1, :], acc_ref[1, 0:1, :]
        ot = jnp.where(first, acc_ref[0] * (1.0 / l0), acc_ref[1] * (1.0 / l1))
        lset = jnp.where(first, m_ref[0] * LN2 + jnp.log(l0), m_ref[1] * LN2 + jnp.log(l1))
        o_ref[...] = ot.T
        lse_ref[...] = lset.T

    blk, whole = _mla_specs(t, tq)
    return pl.pallas_call(
        body, name=name, grid=(N_HEADS // 2, n),
        in_specs=[blk(lambda hp: 2 * hp), blk(lambda hp: 2 * hp + 1), whole(lambda hp: 2 * hp), whole(lambda hp: 2 * hp + 1),
                  pl.BlockSpec((1, n, LANE, tq), lambda hp, qi: (hp, 0, 0, 0))],
        out_specs=[blk(lambda hp: hp), blk(lambda hp: hp)],
        out_shape=[jax.ShapeDtypeStruct((t, N_HEADS * HEAD_DIM), F32)] * 2,
        scratch_shapes=[pltpu.VMEM((2, 1, tq), F32), pltpu.VMEM((2, LANE, tq), F32)],
        compiler_params=_cparams(("parallel", "parallel")),
    )(q, q, k, k, vt)


def _mla_bwd(q, k, kv, do, lse, delta, *, name):
    t = q.shape[0]
    n = t // TQ
    scale = (C_NOPE + C_ROPE) ** -0.5

    def body(q0_ref, q1_ref, k0_ref, k1_ref, v_ref, do_ref, lse_ref, dl_ref,
             dq_ref, dk_ref, dv_ref, dq_acc, dk_acc, dv_acc):
        qi = pl.program_id(1)
        lo = _lane_lo()

        @pl.when(qi == 0)
        def _():
            dk_acc[...] = jnp.zeros_like(dk_acc)
            dv_acc[...] = jnp.zeros_like(dv_acc)

        dq_acc[...] = jnp.zeros_like(dq_acc)
        qs, ks = (q0_ref[...], q1_ref[...]), (k0_ref, k1_ref)
        do2, lse2, dl2 = do_ref[...], lse_ref[...], dl_ref[...]
        lse_sw, dl_sw = pltpu.roll(lse2, HEAD_DIM, axis=1), pltpu.roll(dl2, HEAD_DIM, axis=1)
        heads = []
        for a in range(2):
            mine = lo if a == 0 else jnp.logical_not(lo)
            heads.append((jnp.where(mine, do2, jnp.zeros_like(do2)), jnp.where(mine, lse2, lse_sw)[:, 0:1] * LOG2E,
                          jnp.where(mine, dl2, dl_sw)[:, 0:1]))

        def step(kj, diagonal):
            rows = pl.ds(pl.multiple_of(kj * TQ, TQ), TQ)
            vb = v_ref[rows, :]
            ps = []
            for a, (doa, lse_a, dl_a) in enumerate(heads):
                kb = ks[a][rows, :]
                s = _dot_nt(qs[a], kb) * (scale * LOG2E)
                if diagonal:
                    s = jnp.where(_diag_mask(TQ), s, NEG)
                p = jnp.exp2(s - lse_a)
                ds = (p * (_dot_nt(doa, vb) - dl_a)).astype(BF16)
                dq_acc[a] += _dot(ds, kb)
                dk_acc[a, rows, :] += _dot_tn(ds, qs[a])
                ps.append(p.astype(BF16))
            dv_acc[rows, :] += _dot_tn(jnp.concatenate(ps, axis=0), jnp.concatenate([h[0] for h in heads], axis=0))

        def kloop(kj, c2):
            step(kj, False)
            return c2

        lax.fori_loop(0, qi, kloop, 0)
        step(qi, True)
        dq_ref[:, 0:LANE] = (dq_acc[0] * scale).astype(dq_ref.dtype)
        dq_ref[:, LANE:2 * LANE] = (dq_acc[1] * scale).astype(dq_ref.dtype)

        @pl.when(qi == n - 1)
        def _():
            dk_ref[:, 0:LANE] = (dk_acc[0] * scale).astype(dk_ref.dtype)
            dk_ref[:, LANE:2 * LANE] = (dk_acc[1] * scale).astype(dk_ref.dtype)
            dv_ref[...] = dv_acc[...].astype(dv_ref.dtype)

    blk, whole = _mla_specs(t, TQ)
    even, odd, pair = (lambda hp: 2 * hp), (lambda hp: 2 * hp + 1), (lambda hp: hp)
    wide = jax.ShapeDtypeStruct((t, N_HEADS * LANE), BF16)
    return pl.pallas_call(
        body, name=name, grid=(N_HEADS // 2, n),
        in_specs=[blk(even), blk(odd), whole(even), whole(odd), whole(lambda hp: MLA_V_OFF + hp), blk(pair), blk(pair), blk(pair)],
        out_specs=[pl.BlockSpec((TQ, 2 * LANE), lambda hp, qi: (qi, hp)), pl.BlockSpec((t, 2 * LANE), lambda hp, qi: (0, hp)), whole(pair)],
        out_shape=[wide, wide, jax.ShapeDtypeStruct((t, N_HEADS * HEAD_DIM), BF16)],
        scratch_shapes=[pltpu.VMEM((2, TQ, LANE), F32), pltpu.VMEM((2, t, LANE), F32), pltpu.VMEM((t, LANE), F32)],
        compiler_params=_cparams(("arbitrary", "arbitrary")),
    )(q, q, k, k, kv, do, lse, delta)


def _bias_lookup(bucket, table_t, *, name):
    nh, npos = bucket.shape
    tp = 4096

    def body(b_ref, t_ref, o_ref):
        bk, tab = b_ref[...], t_ref[...]
        acc = jnp.zeros(bk.shape, F32)
        for i in range(REL_BUCKETS):
            acc = jnp.where(bk == i, tab[:, i:i + 1], acc)
        o_ref[...] = acc

    return pl.pallas_call(
        body, name=name, grid=(npos // tp,),
        in_specs=[pl.BlockSpec((nh, tp), lambda i: (0, i)), pl.BlockSpec((nh, REL_BUCKETS), lambda i: (0, 0))],
        out_specs=pl.BlockSpec((nh, tp), lambda i: (0, i)),
        out_shape=jax.ShapeDtypeStruct((nh, npos), F32),
        compiler_params=_cparams(("parallel",)),
    )(bucket, table_t)


def _bias_grad(bucket, ds0, ds1, *, name):
    nh, npos = bucket.shape
    tp = 4096

    def body(b_ref, a_ref, c_ref, o_ref):
        i = pl.program_id(0)
        bk, ds = b_ref[...], a_ref[...] + c_ref[...]
        lane = lax.broadcasted_iota(jnp.int32, (nh, REL_BUCKETS), 1)
        acc = jnp.zeros((nh, REL_BUCKETS), F32)
        for j in range(REL_BUCKETS):
            col = jnp.sum(jnp.where(bk == j, ds, 0.0), axis=1, keepdims=True)
            acc = acc + jnp.where(lane == j, col, 0.0)

        @pl.when(i == 0)
        def _():
            o_ref[...] = acc

        @pl.when(i > 0)
        def _():
            o_ref[...] += acc

    return pl.pallas_call(
        body, name=name, grid=(npos // tp,),
        in_specs=[pl.BlockSpec((nh, tp), lambda i: (0, i))] * 3,
        out_specs=pl.BlockSpec((nh, REL_BUCKETS), lambda i: (0, 0)),
        out_shape=jax.ShapeDtypeStruct((nh, REL_BUCKETS), F32),
        compiler_params=_cparams(("arbitrary",)),
    )(bucket, ds0, ds1)


def _t5_bucket(dist):
    n = jnp.maximum(dist, 0)
    max_exact = REL_BUCKETS // 2
    scaled = jnp.log(jnp.maximum(n, 1).astype(F32) / max_exact) / math.log(REL_MAX_DIST / max_exact)
    large = max_exact + (scaled * (REL_BUCKETS - max_exact)).astype(jnp.int32)
    return jnp.where(n < max_exact, n, jnp.minimum(large, REL_BUCKETS - 1))


def _bucket_index():
    qi = jnp.arange(BLK)[:, None]
    ci = jnp.arange(2 * BLK)[None, :]
    step = BLK + qi - ci
    per_group = [_t5_bucket(step * d).reshape(1, -1) for d in A_DILS + (1,)]
    return jnp.concatenate([jnp.tile(b, (N_HEADS, 1)) for b in per_group], axis=0).astype(jnp.int32)


def _sigmoid(x):
    return 1.0 / (1.0 + jnp.exp(-x))


def _ln_stats(z):
    mu = jnp.mean(z, axis=-1, keepdims=True)
    zc = z - mu
    var = jnp.mean(zc * zc, axis=-1, keepdims=True)
    return zc * lax.rsqrt(var + LN_EPS)


def _ln_fwd(x, mix, g, b, *, name):
    def fn(i, nt, xv, mv, gv, bv):
        z = ALPHA * xv + mv
        y = _ln_stats(z) * gv + bv
        return y, y, z

    c = x.shape[-1]
    y, yb, z = _rowwise(fn, [(x[None], c, 0), (mix[None], c, 0)], pars=[(g.reshape(1, 1, c), c, 0), (b.reshape(1, 1, c), c, 0)],
                        outs=[(c, c, 0, F32), (c, c, 0, BF16), (c, c, 0, F32)], tm=512, name=name)
    return y[0], yb[0], z[0]


def _ln_bwd(z, g, dys, coefs, *, name):
    n = len(dys)

    def fn(i, nt, zv, *rest):
        gv = rest[n]
        dy = coefs[0] * rest[0]
        for cf, t in zip(coefs[1:], rest[1:n]):
            dy = dy + cf * t
        mu = jnp.mean(zv, axis=-1, keepdims=True)
        zc = zv - mu
        r = lax.rsqrt(jnp.mean(zc * zc, axis=-1, keepdims=True) + LN_EPS)
        xh = zc * r
        dxh = dy * gv
        dz = r * (dxh - jnp.mean(dxh, axis=-1, keepdims=True) - xh * jnp.mean(dxh * xh, axis=-1, keepdims=True))
        return dz, dz, jnp.sum(dy * xh, axis=0, keepdims=True), jnp.sum(dy, axis=0, keepdims=True)

    c = z.shape[-1]
    dz, dzb, dg, db = _rowwise(fn, [(z[None], c, 0)] + [(d[None], c, 0) for d in dys], pars=[(g.reshape(1, 1, c), c, 0)],
                               outs=[(c, c, 0, F32), (c, c, 0, BF16)], accs=[(1, c, c, 0), (1, c, c, 0)], tm=512, name=name)
    return dz[0], dzb[0], dg.reshape(c), db.reshape(c)


def _rms_fwd(src, c, off, g, *, name):
    def fn(i, nt, xv, gv):
        return xv * lax.rsqrt(jnp.mean(xv * xv, axis=-1, keepdims=True) + RMS_EPS) * gv

    return _rowwise(fn, [(src[None], c, off)], pars=[(g.reshape(1, 1, c), c, 0)], outs=[(c, c, 0, BF16)], tm=1024, name=name)[0][0]


def _rms_bwd(src, c, off, g, dy, *, name):
    def fn(i, nt, xv, dyv, gv):
        r = lax.rsqrt(jnp.mean(xv * xv, axis=-1, keepdims=True) + RMS_EPS)
        gd = gv * dyv
        dx = gd * r - xv * (r * r * r) * jnp.mean(gd * xv, axis=-1, keepdims=True)
        return dx, jnp.sum(dyv * xv * r, axis=0, keepdims=True)

    dx, dg = _rowwise(fn, [(src[None], c, off), (dy[None], c, 0)], pars=[(g.reshape(1, 1, c), c, 0)],
                      outs=[(c, c, 0, BF16)], accs=[(1, c, c, 0)], tm=1024, name=name)
    return dx[0], dg.reshape(c)


def _rope_slabs(x, n_slab, c, s, *, add=None, to_front=False, name):
    half = C_ROPE // 2

    def fn(i, nt, xv, cv, sv, *rest):
        lane = lax.broadcasted_iota(jnp.int32, (1, LANE), 1)
        extra = pltpu.roll(rest[0], C_NOPE, axis=1) if rest else None
        outs = []
        for h in range(n_slab):
            xs = xv[:, h * LANE:(h + 1) * LANE]
            if extra is not None:
                xs = xs + extra
            swapped = jnp.where(lane < C_NOPE + half, pltpu.roll(xs, LANE - half, axis=1), pltpu.roll(xs, half, axis=1))
            y = xs * cv + swapped * sv
            if to_front:
                y = jnp.where(lane < C_ROPE, pltpu.roll(y, LANE - C_NOPE, axis=1), 0.0)
            outs.append(y)
        return jnp.concatenate(outs, axis=1) if n_slab > 1 else outs[0]

    w = n_slab * LANE
    rows = [(x[None], w, 0), (c[None], LANE, 0), (s[None], LANE, 0)]
    if add is not None:
        rows.append((add[0][None], LANE, add[1]))
    return _rowwise(fn, rows, outs=[(w, w, 0, BF16)], tm=512, name=name)[0][0]


def _merge_fwd(proj, b_gate, ys, *, name):
    def fn(i, nt, g0, g1, g2, ya, yb, yc, bg):
        return (_sigmoid(g0 + bg[:, 0:1024]) * ya + _sigmoid(g1 + bg[:, 1024:2048]) * yb
                + _sigmoid(g2 + bg[:, 2048:3072]) * yc)

    rows = [(proj[None], 1024, j) for j in range(3)] + [(y[None], 1024, 0) for y in ys]
    return _rowwise(fn, rows, pars=[(b_gate.reshape(1, 1, 3072), 3072, 0)], outs=[(1024, 1024, 0, BF16)], tm=512, name=name)[0][0]


def _merge_bwd(proj, b_gate, ys, dm, *, name):
    def fn(i, nt, g0, g1, g2, ya, yb, yc, dmv, bg):
        outs, dgs = [], []
        for j, (gp, y) in enumerate(((g0, ya), (g1, yb), (g2, yc))):
            s = _sigmoid(gp + bg[:, j * 1024:(j + 1) * 1024])
            outs.append(s * dmv)
            dgs.append(dmv * y * s * (1.0 - s))
        return outs + dgs + [jnp.sum(d, axis=0, keepdims=True) for d in dgs]

    rows = [(proj[None], 1024, j) for j in range(3)] + [(y[None], 1024, 0) for y in ys] + [(dm[None], 1024, 0)]
    res = _rowwise(fn, rows, pars=[(b_gate.reshape(1, 1, 3072), 3072, 0)], outs=[(1024, 1024, 0, BF16)] * 6,
                   accs=[(1, 1024, 1024, 0)] * 3, tm=256, name=name)
    dys = [r[0] for r in res[0:3]]
    dgp = [r[0] for r in res[3:6]]
    dbg = jnp.concatenate([r.reshape(1024) for r in res[6:9]])
    return dys, dgp, dbg


def _shift_down(u, halo, i, k):
    ext = jnp.concatenate([jnp.where(i > 0, halo, 0.0), u], axis=0)
    return pltpu.roll(ext, k, axis=0)[HALO:]


def _shift_up(u, halo, i, nt, k):
    ext = jnp.concatenate([u, jnp.where(i < nt - 1, halo, 0.0)], axis=0)
    n = ext.shape[0]
    return pltpu.roll(ext, n - k, axis=0)[:n - HALO]


GLU_C = D_FF // 2


def _conv(u, halo, i, w, b):
    return w[0:1] * _shift_down(u, halo, i, 2) + w[1:2] * _shift_down(u, halo, i, 1) + w[2:3] * u + b


def _glu_fwd(ug, uv, conv_w, conv_b, *, name):
    def fn(i, nt, g, v, hg, hv, wg, wv, bg, bv):
        cg, cv = _conv(g, hg, i, wg, bg), _conv(v, hv, i, wv, bv)
        return cg * _sigmoid(cg) * cv

    w3, b3 = conv_w[None], conv_b.reshape(1, 1, -1)
    c = GLU_C
    return _rowwise(fn, [(ug[None], c, 0), (uv[None], c, 0)], halos=[(ug[None], c, 0, "prev"), (uv[None], c, 0, "prev")],
                    pars=[(w3, c, 0), (w3, c, 2), (b3, c, 0), (b3, c, 2)], outs=[(D_FF, c, 0, BF16)], tm=256, ncol=2, name=name)[0][0]


def _glu_bwd_a(ug, uv, conv_w, conv_b, dh, *, name):
    def fn(i, nt, g, v, dhv, hg, hv, wg, wv, bg, bv):
        g1, g2 = _shift_down(g, hg, i, 1), _shift_down(g, hg, i, 2)
        v1, v2 = _shift_down(v, hv, i, 1), _shift_down(v, hv, i, 2)
        cg = wg[0:1] * g2 + wg[1:2] * g1 + wg[2:3] * g + bg
        cv = wv[0:1] * v2 + wv[1:2] * v1 + wv[2:3] * v + bv
        s = _sigmoid(cg)
        dcv = dhv * cg * s
        dcg = dhv * cv * (s * (1.0 + cg * (1.0 - s)))
        red = lambda a: jnp.sum(a, axis=0, keepdims=True)
        return (dcg, dcv, red(dcg), red(dcv), red(dcg * g2), red(dcg * g1), red(dcg * g),
                red(dcv * v2), red(dcv * v1), red(dcv * v))

    w3, b3 = conv_w[None], conv_b.reshape(1, 1, -1)
    c = GLU_C
    res = _rowwise(fn, [(ug[None], c, 0), (uv[None], c, 0), (dh[None], c, 0)],
                   halos=[(ug[None], c, 0, "prev"), (uv[None], c, 0, "prev")],
                   pars=[(w3, c, 0), (w3, c, 2), (b3, c, 0), (b3, c, 2)],
                   outs=[(D_FF, c, 0, BF16), (D_FF, c, 0, BF16)], accs=[(1, D_FF, c, 0)] * 8, tm=256, ncol=2, name=name)
    dcg, dcv = res[0][0], res[1][0]
    dconv_b = jnp.concatenate([res[2].reshape(D_FF), res[3].reshape(D_FF)])
    dconv_w = jnp.concatenate([jnp.concatenate([res[4 + j].reshape(1, D_FF) for j in range(3)], axis=0),
                               jnp.concatenate([res[7 + j].reshape(1, D_FF) for j in range(3)], axis=0)], axis=1)
    return dcg, dcv, dconv_w, dconv_b


def _glu_bwd_b(dc, conv_w, half, *, name):
    def fn(i, nt, d, hd, w):
        return w[2:3] * d + w[1:2] * _shift_up(d, hd, i, nt, 1) + w[0:1] * _shift_up(d, hd, i, nt, 2)

    c = GLU_C
    return _rowwise(fn, [(dc[None], c, 0)], halos=[(dc[None], c, 0, "next")], pars=[(conv_w[None], c, 2 * half)],
                    outs=[(D_FF, c, 0, BF16)], tm=256, ncol=2, name=name)[0][0]


def _loss_and_grad(y, tgt, *, name):
    def fn(i, nt, yv, tv):
        err = yv - tv
        part = jnp.sum(jnp.sum(err * err, axis=0, keepdims=True), axis=1, keepdims=True) * (0.5 / D_MODEL)
        return err * (1.0 / D_MODEL), jnp.zeros((1, LANE), F32) + part

    dy, part = _rowwise(fn, [(y[None], D_MODEL, 0), (tgt[None], D_MODEL, 0)], outs=[(D_MODEL, D_MODEL, 0, F32)],
                        accs=[(1, LANE, LANE, 0)], tm=512, name=name)
    return dy[0], part.reshape(LANE)


def _lincomb(terms, coefs, *, name):
    def fn(i, nt, *vs):
        acc = coefs[0] * vs[0]
        for cf, v in zip(coefs[1:], vs[1:]):
            acc = acc + cf * v
        return acc

    c = terms[0].shape[-1]
    return _rowwise(fn, [(a[None], c, 0) for a in terms], outs=[(c, c, 0, F32)], tm=512, name=name)[0][0]


def _sum_rows(terms, *, tm, name, dtype=F32):
    def fn(i, nt, *vs):
        acc = vs[0]
        for v in vs[1:]:
            acc = acc + v
        return acc

    c = terms[0].shape[-1]
    return _rowwise(fn, [(t, c, 0) for t in terms], outs=[(c, c, 0, dtype)], tm=tm, name=name)[0]


def _head_sums(x):
    lo = _lane_lo()
    parts = []
    for j in range(x.shape[1] // LANE):
        blk = x[:, j * LANE:(j + 1) * LANE]
        s_lo = jnp.sum(jnp.where(lo, blk, 0.0), axis=1, keepdims=True)
        s_hi = jnp.sum(jnp.where(lo, 0.0, blk), axis=1, keepdims=True)
        parts.append(jnp.where(lo, s_lo, s_hi))
    return jnp.concatenate(parts, axis=1)


def _group_weights(l0, l1, l2):
    m = jnp.maximum(jnp.maximum(l0, l1), l2)
    es = [jnp.exp(l - m) for l in (l0, l1, l2)]
    inv = 1.0 / (es[0] + es[1] + es[2])
    return [e * inv for e in es]


def _combine_fwd(os_, lses, *, name):
    def fn(i, nt, o0, o1, o2, l0, l1, l2):
        w = _group_weights(l0, l1, l2)
        return w[0] * o0 + w[1] * o1 + w[2] * o2

    c = os_[0].shape[-1]
    return _rowwise(fn, [(a[None], c, 0) for a in list(os_) + list(lses)], outs=[(c, c, 0, BF16)], tm=512, name=name)[0][0]


def _combine_bwd(os_, lses, do_a, *, name):
    def fn(i, nt, o0, o1, o2, l0, l1, l2, da):
        ws = _group_weights(l0, l1, l2)
        dws = [_head_sums(da * o) for o in (o0, o1, o2)]
        mean = ws[0] * dws[0] + ws[1] * dws[1] + ws[2] * dws[2]
        return [w * da for w in ws] + [w * mean for w in ws]

    c = do_a.shape[-1]
    res = _rowwise(fn, [(a[None], c, 0) for a in list(os_) + list(lses) + [do_a]], outs=[(c, c, 0, BF16)] * 3 + [(c, c, 0, F32)] * 3,
                   tm=256, name=name)
    return [r[0] for r in res[0:3]], [r[0] for r in res[3:6]]


def _delta(do, o, *, name):
    def fn(i, nt, d, ov):
        return d, _head_sums(d * ov)

    c = do.shape[-1]
    res = _rowwise(fn, [(do[None], c, 0), (o[None], c, 0)], outs=[(c, c, 0, BF16), (c, c, 0, F32)], tm=512, name=name)
    return res[0][0], res[1][0]


def _adamw(w, g, m, v, *, name):
    c1 = 1.0 - ADAM_B1 ** ADAM_STEP
    c2 = 1.0 - ADAM_B2 ** ADAM_STEP

    def fn(i, nt, wv, gv, mv, vv):
        mn = ADAM_B1 * mv + (1.0 - ADAM_B1) * gv
        vn = ADAM_B2 * vv + (1.0 - ADAM_B2) * (gv * gv)
        delta = -ADAM_LR * ((mn / c1) / (jnp.sqrt(vn / c2) + ADAM_EPS) + ADAM_WD * wv)
        return delta, mn, vn

    r, c = w.shape
    rp = _ceil_to(r, 8)
    pad = lambda a: jnp.pad(a, ((0, rp - r), (0, 0))) if rp != r else a
    tm = rp
    for cand in (128, 64, 32, 16, 8):
        if rp % cand == 0:
            tm = cand
            break
    res = _rowwise(fn, [(pad(a)[None], c, 0) for a in (w, g, m, v)], outs=[(c, c, 0, F32)] * 3, tm=tm, name=name)
    return [x[0][:r] for x in res]


ANY = pl.BlockSpec(memory_space=pl.ANY)


def _place():
    x, y, c = lax.axis_index("x"), lax.axis_index("y"), lax.axis_index("c")
    chips = [(1 - x, y), (x, 1 - y), (1 - x, 1 - y)]
    return x, y, c, chips


def _allgather_weights(arrs):
    n = len(arrs)

    def body(*refs):
        ins, outs, send_sems, recv_sems = refs[:n], refs[n:2 * n], refs[2 * n], refs[2 * n + 1]
        x, y, c, chips = _place()
        j = 2 * x + y
        me, sibling = (x, y, c), (x, y, 1 - c)

        def cp(i, k, src, chip_idx, half, to):
            return pltpu.make_async_remote_copy(src_ref=src, dst_ref=outs[i].at[chip_idx, half], send_sem=send_sems.at[k],
                                                recv_sem=recv_sems.at[k], device_id=to, device_id_type=MESH)

        first, passed, own = [], [], []
        for i in range(n):
            for r, (cx, cy) in enumerate(chips):
                first.append(cp(i, 3 * i + r, ins[i].at[c], j, c, (cx, cy, c)))
                passed.append(cp(i, 3 * (n + i) + r, outs[i].at[2 * cx + cy, c], 2 * cx + cy, c, sibling))
            own += [cp(i, 6 * n + 2 * i + half, ins[i].at[half], j, half, sibling) for half in range(2)]
        for d in first + own:
            d.start()
        for i in range(n):
            for r, (cx, cy) in enumerate(chips):
                cp(i, 3 * i + r, ins[i].at[c], 2 * cx + cy, c, me).wait_recv()
                passed[3 * i + r].start()
        for i in range(n):
            for r, (cx, cy) in enumerate(chips):
                cp(i, 3 * (n + i) + r, ins[i].at[c], 2 * cx + cy, 1 - c, me).wait_recv()
        for d in own:
            d.wait_recv()
        for d in first + passed + own:
            d.wait_send()

    return pl.pallas_call(
        body, name="allgather_weights", in_specs=[ANY] * n, out_specs=[ANY] * n,
        out_shape=[jax.ShapeDtypeStruct((N_CHIP,) + a.shape, a.dtype) for a in arrs],
        scratch_shapes=[pltpu.SemaphoreType.DMA((8 * n,)), pltpu.SemaphoreType.DMA((8 * n,))],
    )(*arrs)


def _sibling_swap(gs):
    n = len(gs)

    def body(*refs):
        layers, outs, send_sems, recv_sems = (refs[:n], refs[n:2 * n]), refs[2 * n:3 * n], refs[3 * n], refs[3 * n + 1]
        x, y, c, _ = _place()

        def copies(srcs):
            return [pltpu.make_async_remote_copy(src_ref=srcs[i], dst_ref=outs[i], send_sem=send_sems.at[i], recv_sem=recv_sems.at[i],
                                                 device_id=(x, y, 1 - c), device_id_type=MESH) for i in range(n)]

        for layer in range(DEPTH):
            @pl.when(c == 1 - layer)
            def _(layer=layer):
                for d in copies(layers[layer]):
                    d.start()

        waits = copies(layers[0])
        for d in waits:
            d.wait_recv()
        for d in waits:
            d.wait_send()

    return pl.pallas_call(
        body, name="grad_sibling_swap", in_specs=[ANY] * (2 * n), out_specs=[ANY] * n,
        out_shape=[jax.ShapeDtypeStruct(g0.shape, g0.dtype) for g0, _ in gs],
        scratch_shapes=[pltpu.SemaphoreType.DMA((n,)), pltpu.SemaphoreType.DMA((n,))],
    )(*[g0 for g0, _ in gs], *[g1 for _, g1 in gs])


def _chip_scatter(ps):
    n = len(ps)

    def body(*refs):
        ins, outs, send_sems, recv_sems = refs[:n], refs[n:2 * n], refs[2 * n], refs[2 * n + 1]
        x, y, c, chips = _place()
        sends = []
        for i in range(n):
            for r, (cx, cy) in enumerate(chips):
                sends.append(pltpu.make_async_remote_copy(src_ref=ins[i].at[2 * cx + cy], dst_ref=outs[i].at[r], send_sem=send_sems.at[3 * i + r],
                                                          recv_sem=recv_sems.at[3 * i + r], device_id=(cx, cy, c), device_id_type=MESH))
        for d in sends:
            d.start()
        for d in sends:
            d.wait_recv()
        for d in sends:
            d.wait_send()

    return pl.pallas_call(
        body, name="grad_chip_scatter", in_specs=[ANY] * n, out_specs=[ANY] * n,
        out_shape=[jax.ShapeDtypeStruct((3,) + p.shape[1:], p.dtype) for p in ps],
        scratch_shapes=[pltpu.SemaphoreType.DMA((3 * n,)), pltpu.SemaphoreType.DMA((3 * n,))],
    )(*ps)


def _sibling_share(rs):
    n = len(rs)

    def body(*refs):
        ins, outs, send_sems, recv_sems = refs[:n], refs[n:2 * n], refs[2 * n], refs[2 * n + 1]
        x, y, c, _ = _place()
        cps = [pltpu.make_async_remote_copy(src_ref=ins[i], dst_ref=outs[i], send_sem=send_sems.at[i], recv_sem=recv_sems.at[i],
                                            device_id=(x, y, 1 - c), device_id_type=MESH) for i in range(n)]
        for d in cps:
            d.start()
        for d in cps:
            d.wait_recv()
        for d in cps:
            d.wait_send()

    return pl.pallas_call(
        body, name="grad_sibling_share", in_specs=[ANY] * n, out_specs=[ANY] * n,
        out_shape=[jax.ShapeDtypeStruct(r.shape, r.dtype) for r in rs],
        scratch_shapes=[pltpu.SemaphoreType.DMA((n,)), pltpu.SemaphoreType.DMA((n,))],
    )(*rs)


def _allreduce_small(s):
    rows, w = s.shape
    n_dev = 8

    def body(s_ref, out_ref, slots, send_sems, recv_sems):
        x, y, c, _ = _place()
        me = 4 * x + 2 * y + c
        slots[me] = s_ref[...]
        peers = []
        for r in range(1, n_dev):
            px = 1 - x if r & 4 else x
            py = 1 - y if r & 2 else y
            pc = 1 - c if r & 1 else c
            peers.append((px, py, pc))
        sends = [pltpu.make_async_remote_copy(src_ref=s_ref, dst_ref=slots.at[me], send_sem=send_sems.at[r], recv_sem=recv_sems.at[r],
                                              device_id=peer, device_id_type=MESH) for r, peer in enumerate(peers)]
        for d in sends:
            d.start()
        for r, (px, py, pc) in enumerate(peers):
            pltpu.make_async_remote_copy(src_ref=s_ref, dst_ref=slots.at[4 * px + 2 * py + pc], send_sem=send_sems.at[r],
                                         recv_sem=recv_sems.at[r], device_id=(x, y, c), device_id_type=MESH).wait_recv()
        for d in sends:
            d.wait_send()
        acc = slots[0]
        for k in range(1, n_dev):
            acc = acc + slots[k]
        out_ref[...] = acc

    vm = pl.BlockSpec(memory_space=pltpu.VMEM)
    return pl.pallas_call(
        body, name="allreduce_small", in_specs=[vm], out_specs=vm, out_shape=jax.ShapeDtypeStruct((rows, w), F32),
        scratch_shapes=[pltpu.VMEM((n_dev, rows, w), F32), pltpu.SemaphoreType.DMA((n_dev - 1,)), pltpu.SemaphoreType.DMA((n_dev - 1,))],
    )(s)


W_IN_SHARD = D_IN // N_CHIP
W_IN_ROWS_G = 2304
REDUCED = tuple(m for m in MATS if m[0] != "conv_w")
CONV_W_SIZE = 3 * 2 * D_FF


def _weight_send(name, a):
    if name == "w_in":
        return jnp.swapaxes(a, 1, 2).astype(BF16)
    return a if name == "conv_w" else a.astype(BF16)


def _full_weights(gathered, l):
    g = {k: v[:, l] for k, v in gathered.items()}
    s = g["w_in"].astype(F32).reshape(D_IN, D_MODEL)
    dup = lambda a: jnp.concatenate([a[0:64], a[0:64], a[64:128], a[64:128]], axis=0)
    o = ORIG
    wm_t = jnp.concatenate([s[o["gate"]:], s[o["a"]:o["a"] + A_COLS], s[o["bq"]:o["bk"]], dup(s[o["bk"]:o["bv"]]), dup(s[o["bv"]:o["cq"]]),
                            s[o["cq"]:o["gate"]], jnp.zeros((M_COLS - M_CDKV - (o["gate"] - o["cdkv"]), D_MODEL), F32)], axis=0).astype(BF16)
    wg_t = [s[o["a"] + gi * A_COLS:o["a"] + (gi + 1) * A_COLS].astype(BF16) for gi in (1, 2)]
    full = {name: jnp.moveaxis(g[name], 0, ax).reshape(shape) for name, shape, ax in MATS if name != "w_in"}
    uq = full["w_uq"].reshape(C_Q_RANK, N_HEADS, C_NOPE + C_ROPE)
    ukv = full["w_ukv"].reshape(C_KV_RANK, N_HEADS, 2 * C_NOPE)
    w_uq_p = _pad_lanes(uq).reshape(C_Q_RANK, N_HEADS * LANE)
    w_ukv_p = jnp.concatenate([_pad_lanes(ukv[:, :, :C_NOPE]).reshape(C_KV_RANK, N_HEADS * LANE),
                               ukv[:, :, C_NOPE:].reshape(C_KV_RANK, N_HEADS * HEAD_DIM)], axis=1)
    return {"wm_t": wm_t, "wg_t": wg_t, "w_uq_p": w_uq_p, "w_ukv_p": w_ukv_p, "w_branch": full["w_branch"], "w_out": full["w_out"],
            "wup_g": full["w_ffn_up"][:, :D_FF], "wup_v": full["w_ffn_up"][:, D_FF:], "conv_w": full["conv_w"],
            "w_ffn_down": full["w_ffn_down"]}


def _grad_send(name, g, shape, ax):
    if name == "w_in":
        return jnp.pad(g.reshape(N_CHIP, W_IN_SHARD, D_MODEL), ((0, 0), (0, W_IN_ROWS_G - W_IN_SHARD), (0, 0)))
    split = shape[:ax] + (N_CHIP, shape[ax] // N_CHIP) + shape[ax + 1:]
    return jnp.moveaxis(g.reshape(split), ax, 0)


def _grad_recv(name, r):
    return r[:W_IN_SHARD].T if name == "w_in" else r


def _pack_small(rel, small, conv_w, extra):
    parts = [rel.reshape(-1)]
    for l in range(DEPTH):
        for name in SMALL:
            parts.append(small[name][l].reshape(-1))
    parts += [conv_w.reshape(-1), extra]
    flat = jnp.concatenate(parts)
    rows = _ceil_to(-(-flat.shape[0] // LANE), 8)
    return jnp.pad(flat, (0, rows * LANE - flat.shape[0])).reshape(rows, LANE)


def _unpack_small(buf):
    flat = buf.reshape(-1)
    rel = flat[:REL_BUCKETS * 32].reshape(REL_BUCKETS, 32)
    off = REL_BUCKETS * 32
    small = {name: [] for name in SMALL}
    for l in range(DEPTH):
        for name in SMALL:
            n = SMALL_SIZES[name]
            small[name].append(flat[off:off + n])
            off += n
    conv_w = flat[off:off + DEPTH * CONV_W_SIZE].reshape(DEPTH, 3, 2 * D_FF)
    off += DEPTH * CONV_W_SIZE
    return rel, {k: jnp.stack(v) for k, v in small.items()}, conv_w, flat[off:off + LANE]


def _rows2d(a, lead):
    return a.reshape(a.shape[:lead] + (-1, a.shape[-1]))


def _row_tile(rows):
    for cand in (512, 256, 128, 64, 32, 16, 8):
        if rows % cand == 0:
            return cand
    raise ValueError(rows)


def _pair_add(g, got, core, *, name):
    g0, g1, got2 = _rows2d(g[0], 0), _rows2d(g[1], 0), _rows2d(got, 0)
    rows, c = got2.shape
    flag = jnp.zeros((1, 1, LANE), F32) + core.astype(F32)

    def fn(i, nt, a0, a1, b, f):
        return jnp.where(f[:, 0:1] == 0.0, a0, a1) + b

    out = _rowwise(fn, [(g0[None], c, 0), (g1[None], c, 0), (got2[None], c, 0)], pars=[(flag, LANE, 0)],
                   outs=[(c, c, 0, BF16)], tm=_row_tile(rows), name=name)[0][0]
    return out.reshape(got.shape)


def _chip_add(own, got, *, name):
    own2, got2 = _rows2d(own, 0), _rows2d(got, 1)
    rows, c = own2.shape
    tm = _row_tile(rows)

    def fn(i, nt, a, b0, b1, b2):
        return ((a.astype(F32) + b0.astype(F32)) + b1.astype(F32)) + b2.astype(F32)

    stacked = got2.reshape(1, 3 * rows, c)
    out = _rowwise(fn, [(own2[None], c, 0)] + [(stacked, c, 0, k * (rows // tm)) for k in range(3)],
                   outs=[(c, c, 0, F32)], tm=tm, t=rows, name=name)[0][0]
    return out.reshape(own.shape)


def _perm(a, d):
    if d == 1:
        return a
    t = a.shape[0]
    return jnp.swapaxes(a.reshape((t // d, d) + a.shape[1:]), 0, 1).reshape(a.shape)


def _unperm(a, d):
    if d == 1:
        return a
    t = a.shape[0]
    return jnp.swapaxes(a.reshape((d, t // d) + a.shape[1:]), 0, 1).reshape(a.shape)


def _pad_lanes(a, w=LANE):
    return jnp.pad(a, [(0, 0)] * (a.ndim - 1) + [(0, w - a.shape[-1])])


def _rope_tables(t):
    pos = jnp.arange(t, dtype=F32)
    inv_freq = ROPE_BASE ** (-jnp.arange(0, C_ROPE, 2, dtype=F32) / C_ROPE)
    ang = pos[:, None] * inv_freq[None, :]
    cos, sin = jnp.cos(ang), jnp.sin(ang)
    ones, zeros = jnp.ones((t, C_NOPE), F32), jnp.zeros((t, C_NOPE), F32)
    tail = LANE - C_NOPE - C_ROPE
    c = jnp.concatenate([ones, cos, cos, ones[:, :tail]], axis=1)
    s = jnp.concatenate([zeros, -sin, sin, zeros[:, :tail]], axis=1)
    return c, s


def _band_calls(t, proj, projs_g, sinks):
    none = jnp.full((N_HEADS,), NEG, F32)
    a0 = M_A0 // LANE
    calls = [(proj, (a0, a0 + 4, a0 + 8), t // BLK, False, none)]
    calls += [(pg, (0, 4, 8), t // (d * BLK), False, none) for pg, d in zip(projs_g, A_DILS[1:])]
    calls.append((proj, (M_BQ // LANE, M_BK // LANE, M_BV // LANE), t // BLK, True, sinks.astype(F32)))
    return calls


def _layer_fwd(l, x, xb, w, p, biases, rope_cs):
    t = x.shape[0]
    n = f"l{l}_"
    xps = [_perm(xb, d) for d in A_DILS[1:]]
    proj = _mm(xb, w["wm_t"], tb=True, out_dtype=BF16, tm=TM_TOKENS, name=n + "proj")
    projs_g = [_mm(xp, wg, tb=True, out_dtype=BF16, tm=TM_TOKENS, tn=768, name=n + f"proj_g{i + 1}")
               for i, (xp, wg) in enumerate(zip(xps, w["wg_t"]))]
    s = {"xb": xb, "xps": xps, "proj": proj, "projs_g": projs_g}

    calls = _band_calls(t, proj, projs_g, p["sinks"])
    outs = [_band_fwd(src, offs, biases[i], sk, nb=nb, gqa=gqa, name=n + f"band{i}")
            for i, (src, offs, nb, gqa, sk) in enumerate(calls)]
    os_ = [_unperm(outs[gi][0], d) for gi, d in enumerate(A_DILS)]
    lses = [_unperm(outs[gi][1], d) for gi, d in enumerate(A_DILS)]
    o_a = _combine_fwd(os_, lses, name=n + "combine_fwd")
    o_b_f, lse_b = outs[3]
    o_b = o_b_f.astype(BF16)
    s.update(os=os_, lses=lses, lses_p=[outs[gi][1] for gi in range(3)], o_b=o_b_f, lse_b=lse_b)

    rq = _rms_fwd(proj, C_Q_RANK, M_CQ // C_Q_RANK, p["q_norm_g"], name=n + "rms_q")
    rkv = _rms_fwd(proj, C_KV_RANK, M_CDKV // C_KV_RANK, p["kv_norm_g"], name=n + "rms_kv")
    q_cp = _mm(rq, w["w_uq_p"], out_dtype=BF16, name=n + "uq")
    kv_cp = _mm(rkv, w["w_ukv_p"], out_dtype=BF16, name=n + "ukv")
    q_full = _rope_slabs(q_cp, N_HEADS, rope_cs[0], rope_cs[1], name=n + "rope_q")
    k_full = _rope_slabs(kv_cp, N_HEADS, rope_cs[0], rope_cs[1], add=(proj, (M_CDKV + C_KV_RANK) // LANE), name=n + "rope_k")
    tq = min(TQ_FWD, t)
    vt = jnp.transpose(kv_cp[:, N_HEADS * LANE:].T.reshape(N_HEADS // 2, LANE, t // tq, tq), (0, 2, 1, 3))
    o_c_f, lse_c = _mla_fwd(q_full, k_full, vt, name=n + "mla_fwd")
    o_c = o_c_f.astype(BF16)
    s.update(rq=rq, rkv=rkv, q_full=q_full, k_full=k_full, kv_cp=kv_cp, lse_c=lse_c, o_c=o_c_f)

    obs = [o_a, o_b, o_c]
    ys = [_mm(o, w["w_branch"][i], out_dtype=BF16, name=n + f"branch{i}") for i, o in enumerate(obs)]
    merged = _merge_fwd(proj, p["b_gate"], ys, name=n + "merge")
    mix = _mm(merged, w["w_out"], name=n + "out")
    x1f, x1b, z1 = _ln_fwd(x, mix, p["ln1_g"], p["ln1_b"], name=n + "ln1")
    s.update(obs=obs, ys=ys, merged=merged, z1=z1, x1b=x1b)

    ug = _mm(x1b, w["wup_g"], tm=TM_TOKENS, tn=1408, out_dtype=BF16, name=n + "up_g")
    uv = _mm(x1b, w["wup_v"], tm=TM_TOKENS, tn=1408, out_dtype=BF16, name=n + "up_v")
    h = _glu_fwd(ug, uv, w["conv_w"], p["conv_b"], name=n + "glu")
    ff = _mm(h, w["w_ffn_down"], tm=TM_TOKENS, tk=1408, name=n + "down")
    x2f, x2b, z2 = _ln_fwd(x1f, ff, p["ln2_g"], p["ln2_b"], name=n + "ln2")
    s.update(ug=ug, uv=uv, h=h, z2=z2)
    return x2f, x2b, s


def _layer_bwd(l, s, dys, coefs, w, p, biases, rope_cs):
    n = f"l{l}b_"
    t = s["z2"].shape[0]
    gw, gs = {}, {}

    dz2, dz2b, gs["ln2_g"], gs["ln2_b"] = _ln_bwd(s["z2"], p["ln2_g"], dys, coefs, name=n + "ln2")
    dh = _mm(dz2b, w["w_ffn_down"], tb=True, tm=TM_TOKENS, tn=1408, out_dtype=BF16, name=n + "d_h")
    gw["w_ffn_down"] = _mm(s["h"], dz2b, ta=True, tm=1408, tk=1024, name=n + "g_down")
    dcg, dcv, gw["conv_w"], gs["conv_b"] = _glu_bwd_a(s["ug"], s["uv"], w["conv_w"], p["conv_b"], dh, name=n + "glu_a")
    dug = _glu_bwd_b(dcg, w["conv_w"], 0, name=n + "glu_bg")
    duv = _glu_bwd_b(dcv, w["conv_w"], 1, name=n + "glu_bv")
    dx1_g = _mm(dug, w["wup_g"], tb=True, tm=TM_TOKENS, tk=1408, name=n + "d_x1g")
    dx1_v = _mm(duv, w["wup_v"], tb=True, tm=TM_TOKENS, tk=1408, name=n + "d_x1v")
    gw["w_ffn_up"] = jnp.concatenate([_mm(s["x1b"], dug, ta=True, tn=1408, tk=1024, name=n + "g_upg"),
                                      _mm(s["x1b"], duv, ta=True, tn=1408, tk=1024, name=n + "g_upv")], axis=1)

    dz1, dz1b, gs["ln1_g"], gs["ln1_b"] = _ln_bwd(s["z1"], p["ln1_g"], [dz2, dx1_g, dx1_v], [ALPHA, 1.0, 1.0], name=n + "ln1")
    dmerged = _mm(dz1b, w["w_out"], tb=True, out_dtype=BF16, name=n + "d_merged")
    gw["w_out"] = _mm(s["merged"], dz1b, ta=True, name=n + "g_out")
    dys_b, dgp, gs["b_gate"] = _merge_bwd(s["proj"], p["b_gate"], s["ys"], dmerged, name=n + "merge")
    dos = [_mm(dy, w["w_branch"][i], tb=True, out_dtype=BF16, name=n + f"d_o{i}") for i, dy in enumerate(dys_b)]
    gw["w_branch"] = jnp.stack([_mm(o, dy, ta=True, name=n + f"g_branch{i}") for i, (o, dy) in enumerate(zip(s["obs"], dys_b))])

    do_gs, dpr_gs = _combine_bwd(s["os"], s["lses"], dos[0], name=n + "combine")
    do_b, dpr_b = _delta(dos[1], s["o_b"], name=n + "delta_b")
    do_list = [_perm(a, d) for a, d in zip(do_gs, A_DILS)] + [do_b]
    dpr_list = [_perm(a, d) for a, d in zip(dpr_gs, A_DILS)] + [dpr_b]
    lse_list = s["lses_p"] + [s["lse_b"]]
    calls = _band_calls(t, s["proj"], s["projs_g"], p["sinks"])
    band = [_band_bwd(src, offs, do_list[i], lse_list[i], dpr_list[i], biases[i], sk, nb=nb, gqa=gqa, name=n + f"band{i}")
            for i, (src, offs, nb, gqa, sk) in enumerate(calls)]
    gs["sinks"] = band[3][4][:, 0, 0]
    ds_sum = jnp.concatenate([b_[3] for b_ in band], axis=0)

    do_c, delta_c = _delta(dos[2], s["o_c"], name=n + "delta_c")
    dq, dk, dv = _mla_bwd(s["q_full"], s["k_full"], s["kv_cp"], do_c, s["lse_c"], delta_c, name=n + "mla")
    dq_cp = _rope_slabs(dq, N_HEADS, rope_cs[0], -rope_cs[1], name=n + "rope_q")
    dk_sum = _rowwise(lambda i, nt, *vs: sum(vs[1:], vs[0]), [(dk[None], LANE, hh) for hh in range(N_HEADS)],
                      outs=[(LANE, LANE, 0, F32)], tm=1024, name=n + "krope_sum")[0][0]
    dkr = _rope_slabs(dk_sum, 1, rope_cs[0], -rope_cs[1], to_front=True, name=n + "rope_k")
    dkv_cp = jnp.concatenate([dk, dv], axis=1)
    d_rq = _mm(dq_cp, w["w_uq_p"], tb=True, name=n + "d_rq")
    d_rkv = _mm(dkv_cp, w["w_ukv_p"], tb=True, name=n + "d_rkv")
    g_uq = _mm(s["rq"], dq_cp, ta=True, name=n + "g_uq")
    g_ukv = _mm(s["rkv"], dkv_cp, ta=True, name=n + "g_ukv")
    gw["w_uq"] = g_uq.reshape(C_Q_RANK, N_HEADS, LANE)[:, :, :C_NOPE + C_ROPE].reshape(C_Q_RANK, -1)
    kw = N_HEADS * LANE
    gw["w_ukv"] = jnp.concatenate([g_ukv[:, :kw].reshape(C_KV_RANK, N_HEADS, LANE)[:, :, :C_NOPE],
                                   g_ukv[:, kw:].reshape(C_KV_RANK, N_HEADS, HEAD_DIM)], axis=2).reshape(C_KV_RANK, -1)
    dcq, gs["q_norm_g"] = _rms_bwd(s["proj"], C_Q_RANK, M_CQ // C_Q_RANK, p["q_norm_g"], d_rq, name=n + "rms_q")
    dckv, gs["kv_norm_g"] = _rms_bwd(s["proj"], C_KV_RANK, M_CDKV // C_KV_RANK, p["kv_norm_g"], d_rkv, name=n + "rms_kv")
    dcdkv = jnp.concatenate([dckv, dkr], axis=1)

    dproj = jnp.concatenate(dgp + list(band[0][:3]) + list(band[3][:3]) + [dcq, dcdkv], axis=1)
    dprojs_g = [jnp.concatenate(band[gi][:3], axis=1) for gi in (1, 2)]
    dx_terms = [_mm(dproj, w["wm_t"], tm=TM_TOKENS, tk=1024, name=n + "d_x")]
    dx_terms += [_unperm(_mm(dp, wg, tm=TM_TOKENS, tk=768, name=n + f"d_x_g{i + 1}"), d)
                 for i, (dp, wg, d) in enumerate(zip(dprojs_g, w["wg_t"], A_DILS[1:]))]
    g_main = _mm(dproj, s["xb"], ta=True, name=n + "g_in")
    g_groups = [_mm(dp, xp, ta=True, tm=768, name=n + f"g_in_g{i + 1}") for i, (xp, dp) in enumerate(zip(s["xps"], dprojs_g))]
    fold = lambda a, tag: _sum_rows([a.reshape(2, 2, HEAD_DIM, D_MODEL)[:, j] for j in range(2)], tm=HEAD_DIM,
                                    name=n + "g_fold_" + tag).reshape(2 * HEAD_DIM, D_MODEL)
    gw["w_in"] = jnp.concatenate([g_main[M_A0:M_BQ], g_groups[0], g_groups[1], g_main[M_BQ:M_BK], fold(g_main[M_BK:M_BV], "k"),
                                  fold(g_main[M_BV:M_CQ], "v"), g_main[M_CQ:M_CDKV + C_KV_RANK + C_ROPE], g_main[M_GATE:M_A0]], axis=0)
    return [dz1] + dx_terms, [ALPHA, 1.0, 1.0, 1.0], gw, gs, ds_sum


def _local_step(x, target, ws, rel_table, small):
    t = x.shape[0]
    ps = [{k: small[k][l] for k in SMALL} for l in range(DEPTH)]
    bucket = _bucket_index()
    bias_all = _bias_lookup(bucket, rel_table.T, name="bias_lookup").reshape(4, N_HEADS, BLK, 2 * BLK)
    step = BLK + jnp.arange(BLK)[:, None] - jnp.arange(2 * BLK)[None, :]
    biases = [jnp.where((step >= 0) & (step <= lim), bias_all[i], NEG) for i, lim in enumerate((BLK, BLK, BLK, BLK - 1))]
    rope_cs = _rope_tables(t)

    saved, h, hb = [], x, x.astype(BF16)
    for l in range(DEPTH):
        h, hb, s = _layer_fwd(l, h, hb, ws[l], ps[l], biases, rope_cs)
        saved.append(s)
    dy, loss_part = _loss_and_grad(h, target, name="loss")

    dys, coefs = [dy], [1.0]
    gws, gss, dss = [None] * DEPTH, [None] * DEPTH, [None] * DEPTH
    for l in reversed(range(DEPTH)):
        dys, coefs, gws[l], gss[l], dss[l] = _layer_bwd(l, saved[l], dys, coefs, ws[l], ps[l], biases, rope_cs)
    grad_x = _lincomb(dys, coefs, name="grad_x")
    npos = 2 * BLK * BLK
    g_rel = _bias_grad(bucket, dss[0].reshape(4 * N_HEADS, npos), dss[1].reshape(4 * N_HEADS, npos), name="bias_grad").T
    gsmall = {k: jnp.stack([gss[l][k] for l in range(DEPTH)]) for k in SMALL}
    return loss_part, grad_x, gws, gsmall, g_rel


def kernel(x, rel_table, w_in, b_gate, sinks, q_norm_g, kv_norm_g, w_uq, w_ukv, w_branch, w_out, ln1_g, ln1_b, w_ffn_up, conv_w, conv_b, w_ffn_down, ln2_g, ln2_b, loss_target, m_rel_table, m_w_in, m_b_gate, m_sinks, m_q_norm_g, m_kv_norm_g, m_w_uq, m_w_ukv, m_w_branch, m_w_out, m_ln1_g, m_ln1_b, m_w_ffn_up, m_conv_w, m_conv_b, m_w_ffn_down, m_ln2_g, m_ln2_b, v_rel_table, v_w_in, v_b_gate, v_sinks, v_q_norm_g, v_kv_norm_g, v_w_uq, v_w_ukv, v_w_branch, v_w_out, v_ln1_g, v_ln1_b, v_w_ffn_up, v_conv_w, v_conv_b, v_w_ffn_down, v_ln2_g, v_ln2_b):
    wts = dict(rel_table=rel_table, w_in=w_in, b_gate=b_gate, sinks=sinks, q_norm_g=q_norm_g, kv_norm_g=kv_norm_g, w_uq=w_uq,
               w_ukv=w_ukv, w_branch=w_branch, w_out=w_out, ln1_g=ln1_g, ln1_b=ln1_b, w_ffn_up=w_ffn_up, conv_w=conv_w,
               conv_b=conv_b, w_ffn_down=w_ffn_down, ln2_g=ln2_g, ln2_b=ln2_b)
    ms = dict(rel_table=m_rel_table, w_in=m_w_in, b_gate=m_b_gate, sinks=m_sinks, q_norm_g=m_q_norm_g, kv_norm_g=m_kv_norm_g,
              w_uq=m_w_uq, w_ukv=m_w_ukv, w_branch=m_w_branch, w_out=m_w_out, ln1_g=m_ln1_g, ln1_b=m_ln1_b, w_ffn_up=m_w_ffn_up,
              conv_w=m_conv_w, conv_b=m_conv_b, w_ffn_down=m_w_ffn_down, ln2_g=m_ln2_g, ln2_b=m_ln2_b)
    vs = dict(rel_table=v_rel_table, w_in=v_w_in, b_gate=v_b_gate, sinks=v_sinks, q_norm_g=v_q_norm_g, kv_norm_g=v_kv_norm_g,
              w_uq=v_w_uq, w_ukv=v_w_ukv, w_branch=v_w_branch, w_out=v_w_out, ln1_g=v_ln1_g, ln1_b=v_ln1_b, w_ffn_up=v_w_ffn_up,
              conv_w=v_conv_w, conv_b=v_conv_b, w_ffn_down=v_w_ffn_down, ln2_g=v_ln2_g, ln2_b=v_ln2_b)

    core = lax.axis_index("c")
    chip = 2 * lax.axis_index("x") + lax.axis_index("y")

    names = [name for name, _, _ in MATS]
    gathered = dict(zip(names, _allgather_weights([_weight_send(name, wts[name]) for name in names])))
    ws = [_full_weights(gathered, l) for l in range(DEPTH)]

    small = {k: wts[k] for k in SMALL}
    loss_part, grad_x, gws, gsmall, g_rel = _local_step(x[0], loss_target[0], ws, rel_table, small)

    rnames = [name for name, _, _ in REDUCED]
    gsend = [tuple(_grad_send(name, gws[l][name], shape, ax) for l in range(DEPTH)) for name, shape, ax in REDUCED]
    theirs = _sibling_swap(gsend)
    pairs = [_pair_add(g, t_, core, name="grad_pair_" + name) for name, g, t_ in zip(rnames, gsend, theirs)]
    arrived = _chip_scatter(pairs)
    reduced = [_chip_add(lax.dynamic_index_in_dim(p, chip, 0, keepdims=False), a, name="grad_chip_" + name)
               for name, p, a in zip(rnames, pairs, arrived)]
    others = _sibling_share(reduced)
    gshard = {}
    for name, mine, other in zip(rnames, reduced, others):
        layers = [jnp.where(core == l, mine, other) for l in range(DEPTH)]
        gshard[name] = jnp.stack([_grad_recv(name, a) for a in layers])

    conv_w_full = jnp.stack([gws[l]["conv_w"] for l in range(DEPTH)])
    small_red = _allreduce_small(_pack_small(g_rel, gsmall, conv_w_full, loss_part))
    g_rel_r, gsmall_r, conv_w_r, loss_vec = _unpack_small(small_red)
    loss = loss_vec[0]
    shard_w = 2 * D_FF // N_CHIP
    gshard["conv_w"] = lax.dynamic_slice_in_dim(conv_w_r, chip * shard_w, shard_w, axis=2)

    grads = dict(gshard)
    grads.update(gsmall_r)
    grads["rel_table"] = g_rel_r
    deltas, new_m, new_v = {}, {}, {}
    for name, _, _ in MATS:
        shp = wts[name].shape
        v2 = lambda a: a.reshape(-1, shp[-1])
        d_, m_, v_ = _adamw(v2(wts[name]), v2(grads[name]), v2(ms[name]), v2(vs[name]), name="adamw_" + name)
        deltas[name], new_m[name], new_v[name] = d_.reshape(shp), m_.reshape(shp), v_.reshape(shp)
    zero, none = jnp.zeros((LANE,), F32), jnp.zeros((0,), F32)
    sw = _pack_small(wts["rel_table"], {k: wts[k] for k in SMALL}, none, zero)
    sm = _pack_small(ms["rel_table"], {k: ms[k] for k in SMALL}, none, zero)
    sv = _pack_small(vs["rel_table"], {k: vs[k] for k in SMALL}, none, zero)
    sg = _pack_small(g_rel_r, gsmall_r, none, zero)
    sd, smn, svn = _adamw(sw, sg, sm, sv, name="adamw_small")
    for res, buf in ((deltas, sd), (new_m, smn), (new_v, svn)):
        rel_, sm_ = _unpack_small(jnp.pad(buf, ((0, small_red.shape[0] - buf.shape[0]), (0, 0))))[:2]
        res["rel_table"] = rel_
        res.update(sm_)

    return (loss, grad_x[None], *[grads[k] for k in WEIGHT_ORDER], *[deltas[k] for k in WEIGHT_ORDER],
            *[new_m[k] for k in WEIGHT_ORDER], *[new_v[k] for k in WEIGHT_ORDER])
```

```python
import math

import jax
import jax.numpy as jnp
from jax import lax
from jax.experimental import pallas as pl
from jax.experimental.pallas import tpu as pltpu

F32 = jnp.float32
BF16 = jnp.bfloat16
MESH = pl.DeviceIdType.MESH

D_MODEL = 1024
DEPTH = 2
HEAD_DIM = 64
N_HEADS = 8
A_DILS = (1, 4, 16)
C_Q_RANK = 256
C_KV_RANK = 128
C_NOPE = 64
C_ROPE = 32
ROPE_BASE = 10000.0
REL_BUCKETS = 32
REL_MAX_DIST = 2048
D_FF = 2816
ALPHA = (2 * DEPTH) ** 0.25
LN_EPS = 1e-5
RMS_EPS = 1e-6
NEG = -1e30
LOG2E, LN2 = math.log2(math.e), math.log(2.0)
ADAM_LR, ADAM_B1, ADAM_B2, ADAM_EPS, ADAM_WD, ADAM_STEP = 0.001, 0.9, 0.999, 1e-08, 0.01, 10

VMEM_LIMIT_BYTES = 56 * 1024 * 1024
LANE = 128
BLK = 128
TQ = 512
TQ_FWD = 1024
TM_TOKENS = 2048
HALO = 16
BAND_SCALE = HEAD_DIM ** -0.5
BAND_UNROLL = 16

D_IN = 8864
A_COLS = 3 * N_HEADS * HEAD_DIM
ORIG = {"a": 0, "bq": 4608, "bk": 5120, "bv": 5248, "cq": 5376, "cdkv": 5632, "gate": 5792}
M_GATE, M_A0, M_BQ, M_BK, M_BV, M_CQ, M_CDKV, M_COLS = 0, 3072, 4608, 5120, 5376, 5632, 5888, 6144

N_CHIP = 4
MATS = (
    ("w_in", (D_MODEL, D_IN), 1),
    ("w_uq", (C_Q_RANK, 768), 1),
    ("w_ukv", (C_KV_RANK, 1024), 1),
    ("w_branch", (3, 512, D_MODEL), 2),
    ("w_out", (D_MODEL, D_MODEL), 0),
    ("w_ffn_up", (D_MODEL, 2 * D_FF), 1),
    ("conv_w", (3, 2 * D_FF), 1),
    ("w_ffn_down", (D_FF, D_MODEL), 0),
)
SMALL = ("b_gate", "sinks", "q_norm_g", "kv_norm_g", "ln1_g", "ln1_b", "conv_b", "ln2_g", "ln2_b")
SMALL_SIZES = {"b_gate": 3072, "sinks": 8, "q_norm_g": 256, "kv_norm_g": 128, "ln1_g": 1024, "ln1_b": 1024,
               "conv_b": 5632, "ln2_g": 1024, "ln2_b": 1024}
WEIGHT_ORDER = ("rel_table", "w_in", "b_gate", "sinks", "q_norm_g", "kv_norm_g", "w_uq", "w_ukv", "w_branch",
                "w_out", "ln1_g", "ln1_b", "w_ffn_up", "conv_w", "conv_b", "w_ffn_down", "ln2_g", "ln2_b")


def _cparams(sem):
    return pltpu.CompilerParams(dimension_semantics=sem, vmem_limit_bytes=VMEM_LIMIT_BYTES)


def _ceil_to(n, m):
    return -(-n // m) * m


def _pick(n, target):
    if n <= target:
        return n
    best = None
    for t in range(LANE, target + 1, LANE):
        if n % t == 0:
            best = t
    assert best is not None, (n, target)
    return best


def _mm(a, b, *, ta=False, tb=False, out_dtype=F32, tm=1024, tn=1024, tk=2048, name):
    assert not (ta and tb)
    k, m = a.shape[::-1] if not ta else a.shape
    n = b.shape[0] if tb else b.shape[1]
    assert (b.shape[1] if tb else b.shape[0]) == k
    tm, tn, tk = _pick(m, tm), _pick(n, tn), _pick(k, tk)
    nk = k // tk
    dn = (((0 if ta else 1,), (1 if tb else 0,)), ((), ()))

    def body(a_ref, b_ref, o_ref, acc_ref):
        part = lax.dot_general(a_ref[...].astype(BF16), b_ref[...].astype(BF16), dn, preferred_element_type=F32)
        if nk == 1:
            o_ref[...] = part.astype(o_ref.dtype)
        else:
            kk = pl.program_id(2)

            @pl.when(kk == 0)
            def _():
                acc_ref[...] = part

            @pl.when(kk > 0)
            def _():
                acc_ref[...] += part

            @pl.when(kk == nk - 1)
            def _():
                o_ref[...] = acc_ref[...].astype(o_ref.dtype)

    a_spec = pl.BlockSpec((tk, tm), lambda i, j, kk: (kk, i)) if ta else pl.BlockSpec((tm, tk), lambda i, j, kk: (i, kk))
    b_spec = pl.BlockSpec((tn, tk), lambda i, j, kk: (j, kk)) if tb else pl.BlockSpec((tk, tn), lambda i, j, kk: (kk, j))
    return pl.pallas_call(
        body, name=name, grid=(m // tm, n // tn, nk),
        in_specs=[a_spec, b_spec],
        out_specs=pl.BlockSpec((tm, tn), lambda i, j, kk: (i, j)),
        out_shape=jax.ShapeDtypeStruct((m, n), out_dtype),
        scratch_shapes=[pltpu.VMEM((tm, tn) if nk > 1 else (8, LANE), F32)],
        compiler_params=_cparams(("parallel", "parallel", "arbitrary")),
    )(a, b)


def _rowwise(fn, rows, *, pars=(), halos=(), outs=(), accs=(), tm, name, ncol=1, t=None):
    nb = rows[0][0].shape[0]
    t = rows[0][0].shape[1] if t is None else t
    tm = min(tm, t)
    assert t % tm == 0 and tm % 8 == 0
    nt = t // tm
    in_specs, args = [], []
    for spec in rows:
        arr, c, off = spec[:3]
        rb = spec[3] if len(spec) > 3 else 0
        in_specs.append(pl.BlockSpec((1, tm, c), lambda b, cc, i, off=off, rb=rb: (b, i + rb, off + cc)))
        args.append(arr)
    for arr, c, off, kind in halos:
        if kind == "prev":
            im = lambda b, cc, i, off=off: (b, jnp.maximum(i * (tm // HALO) - 1, 0), off + cc)
        else:
            im = lambda b, cc, i, off=off: (b, jnp.minimum((i + 1) * (tm // HALO), t // HALO - 1), off + cc)
        in_specs.append(pl.BlockSpec((1, HALO, c), im))
        args.append(arr)
    for arr, c, off in pars:
        bp, r = arr.shape[:2]
        if bp > 1:
            im = lambda b, cc, i, off=off: (b, 0, off + cc)
        else:
            im = lambda b, cc, i, off=off: (0, 0, off + cc)
        in_specs.append(pl.BlockSpec((1, r, c), im))
        args.append(arr)
    out_specs, out_shapes = [], []
    for ctot, c, off, dt in outs:
        out_specs.append(pl.BlockSpec((1, tm, c), lambda b, cc, i, off=off: (b, i, off + cc)))
        out_shapes.append(jax.ShapeDtypeStruct((nb, t, ctot), dt))
    for r, ctot, c, off in accs:
        out_specs.append(pl.BlockSpec((1, r, c), lambda b, cc, i, off=off: (b, 0, off + cc)))
        out_shapes.append(jax.ShapeDtypeStruct((nb, r, ctot), F32))
    n_in, n_out = len(args), len(outs)

    def body(*refs):
        i = pl.program_id(2)
        res = fn(i, nt, *[r[0].astype(F32) for r in refs[:n_in]])
        if not isinstance(res, (tuple, list)):
            res = (res,)
        for o_ref, val in zip(refs[n_in:n_in + n_out], res[:n_out]):
            o_ref[0] = val.astype(o_ref.dtype)
        for a_ref, val in zip(refs[n_in + n_out:], res[n_out:]):
            @pl.when(i == 0)
            def _(a_ref=a_ref, val=val):
                a_ref[0] = val

            @pl.when(i > 0)
            def _(a_ref=a_ref, val=val):
                a_ref[0] += val

    res = pl.pallas_call(
        body, name=name, grid=(nb, ncol, nt), in_specs=in_specs, out_specs=out_specs, out_shape=out_shapes,
        compiler_params=_cparams(("parallel", "parallel", "arbitrary")),
    )(*args)
    return res


def _dot(a, b):
    return lax.dot_general(a, b, (((1,), (0,)), ((), ())), preferred_element_type=F32)


def _dot_nt(a, b):
    return lax.dot_general(a, b, (((1,), (1,)), ((), ())), preferred_element_type=F32)


def _dot_tn(a, b):
    return lax.dot_general(a, b, (((0,), (0,)), ((), ())), preferred_element_type=F32)


def _rows(parts):
    return jnp.concatenate(parts, axis=0)


def _lane_lo():
    return lax.broadcasted_iota(jnp.int32, (1, LANE), 1) < HEAD_DIM


def _blocks(a):
    return [a[i * BLK:(i + 1) * BLK] for i in range(a.shape[0] // BLK)]


def _band_geometry(t):
    rows = min(BAND_UNROLL, t // BLK) * BLK
    assert t % rows == 0
    return rows, t // rows


def _band_operands(g, k_ref, v_ref, rows):
    start = pl.multiple_of(g * rows, rows)
    pstart = pl.multiple_of(jnp.maximum(g * rows - BLK, 0), BLK)
    out = []
    for ref in (k_ref, v_ref):
        cur = _blocks(ref[pl.ds(start, rows), :])
        raw = ref[pl.ds(pstart, rows), :]
        shifted = _rows([jnp.zeros((BLK, LANE), raw.dtype), raw[:rows - BLK]])
        prev = _blocks(jnp.where(g == 0, shifted, raw))
        out.append([_rows([p, c]) for p, c in zip(prev, cur)])
    return out


def _band_scores(g, qa, kk, b_ref, a, nb):
    u = len(qa)
    assert nb % u == 0 or u % nb == 0
    firsts = []
    for i in range(u):
        if nb >= u:
            val = jnp.where(lax.rem(g * u, nb) == 0, NEG, 0.0).astype(F32) if i == 0 else 0.0
        else:
            val = NEG if i % nb == 0 else 0.0
        firsts.append(jnp.zeros((BLK, 1), F32) + val)
    prev_slots = lax.broadcasted_iota(jnp.int32, (1, 2 * BLK), 1) < BLK
    return (_rows([_dot_nt(q, k) for q, k in zip(qa, kk)]) + _rows([b_ref[a]] * u)
            + jnp.where(prev_slots, _rows(firsts), 0.0))


def _band_fwd(src, offs, bias, sinks, *, nb, gqa, name):
    t = src.shape[0]
    rows, nstep = _band_geometry(t)
    qo, ko, vo = offs
    share = 2 if gqa else 1

    def body(sink_ref, q_ref, k_ref, v_ref, b_ref, o_ref, lse_ref):
        hp, g = pl.program_id(0), pl.program_id(1)
        lo = _lane_lo()
        q2 = q_ref[...]
        kk, vv = _band_operands(g, k_ref, v_ref, rows)
        outs, lses = [], []
        for a in range(2):
            sink = sink_ref[2 * hp + a]
            qa = _blocks(jnp.where(lo if a == 0 else jnp.logical_not(lo), q2, jnp.zeros_like(q2)) * BAND_SCALE)
            s = _band_scores(g, qa, kk, b_ref, a, nb)
            m = jnp.maximum(jnp.max(s, axis=1, keepdims=True), sink)
            p = jnp.exp(s - m)
            l = jnp.sum(p, axis=1, keepdims=True) + jnp.exp(sink - m)
            p_b = _blocks((p * (1.0 / l)).astype(BF16))
            outs.append(_rows([_dot(pb, v) for pb, v in zip(p_b, vv)]))
            lses.append(m + jnp.log(l))
        o_ref[...] = jnp.where(lo, outs[0], outs[1])
        lse_ref[...] = jnp.where(lo, lses[0], lses[1])

    slab = lambda off: pl.BlockSpec((rows, LANE), lambda hp, g: (g, off + hp))
    whole = lambda off: pl.BlockSpec((t, LANE), lambda hp, g: (0, off + hp // share))
    return pl.pallas_call(
        body, name=name, grid=(N_HEADS // 2, nstep),
        in_specs=[pl.BlockSpec(memory_space=pltpu.SMEM), slab(qo), whole(ko), whole(vo),
                  pl.BlockSpec((2, BLK, 2 * BLK), lambda hp, g: (hp, 0, 0))],
        out_specs=[slab(0), slab(0)],
        out_shape=[jax.ShapeDtypeStruct((t, N_HEADS * HEAD_DIM), F32)] * 2,
        compiler_params=_cparams(("parallel", "parallel")),
    )(sinks, src, src, src, bias)


def _band_bwd(src, offs, do, lse, dpr, bias, sinks, *, nb, gqa, name):
    t = src.shape[0]
    rows, nstep = _band_geometry(t)
    qo, ko, vo = offs
    share = 2 if gqa else 1

    def fold(a):
        acc = a[0:BLK]
        for i in range(1, rows // BLK):
            acc = acc + a[i * BLK:(i + 1) * BLK]
        return acc

    def body(sink_ref, q_ref, k_ref, v_ref, do_ref, lse_ref, dpr_ref, b_ref,
             dq_ref, dk_ref, dv_ref, ds_ref, dsink_ref, dk_acc, dv_acc):
        hp, g = pl.program_id(0), pl.program_id(1)
        lo = _lane_lo()
        hi = jnp.logical_not(lo)

        @pl.when(jnp.logical_and(g == 0, lax.rem(hp, share) == 0))
        def _():
            dk_acc[...] = jnp.zeros_like(dk_acc)
            dv_acc[...] = jnp.zeros_like(dv_acc)

        @pl.when(g == 0)
        def _():
            ds_ref[...] = jnp.zeros_like(ds_ref)
            dsink_ref[...] = jnp.zeros_like(dsink_ref)

        q2, do2, lse2, dpr2 = q_ref[...], do_ref[...], lse_ref[...], dpr_ref[...]
        lse_sw, dpr_sw = pltpu.roll(lse2, HEAD_DIM, axis=1), pltpu.roll(dpr2, HEAD_DIM, axis=1)
        kk, vv = _band_operands(g, k_ref, v_ref, rows)
        dqs, per_head = [], []
        for a in range(2):
            sink = sink_ref[2 * hp + a]
            mine = lo if a == 0 else hi
            qa = _blocks(jnp.where(mine, q2, jnp.zeros_like(q2)) * BAND_SCALE)
            doa = _blocks(jnp.where(mine, do2, jnp.zeros_like(do2)))
            lse_a, dpr_a = jnp.where(mine, lse2, lse_sw), jnp.where(mine, dpr2, dpr_sw)
            wide = lambda x: jnp.concatenate([x, x], axis=1)
            p = jnp.exp(_band_scores(g, qa, kk, b_ref, a, nb) - wide(lse_a))
            ds = p * (_rows([_dot_nt(d, v) for d, v in zip(doa, vv)]) - wide(dpr_a))
            ds_ref[a] += fold(ds)
            dsink_ref[a] -= jnp.sum(jnp.exp(sink - lse_a) * dpr_a, axis=0, keepdims=True)
            ds_b, p_b = _blocks(ds.astype(BF16)), _blocks(p.astype(BF16))
            dqs.append(_rows([_dot(d, k) for d, k in zip(ds_b, kk)]))
            per_head.append((ds_b, p_b, qa, doa))
        dq_ref[...] = (jnp.where(lo, dqs[0], dqs[1]) * BAND_SCALE).astype(dq_ref.dtype)
        (ds0, p0, qa0, do0), (ds1, p1, qa1, do1) = per_head
        for i in range(len(qa0)):
            at = pl.ds(pl.multiple_of(g * rows + i * BLK, BLK), 2 * BLK)
            dk_acc[at, :] += _dot_tn(_rows([ds0[i], ds1[i]]), _rows([qa0[i], qa1[i]]))
            dv_acc[at, :] += _dot_tn(_rows([p0[i], p1[i]]), _rows([do0[i], do1[i]]))

        @pl.when(g == nstep - 1)
        def _():
            dk_ref[...] = dk_acc[BLK:, :].astype(dk_ref.dtype)
            dv_ref[...] = dv_acc[BLK:, :].astype(dv_ref.dtype)

    slab = lambda off: pl.BlockSpec((rows, LANE), lambda hp, g: (g, off + hp))
    whole = lambda off: pl.BlockSpec((t, LANE), lambda hp, g: (0, off + hp // share))
    per_pair = lambda shp: pl.BlockSpec((2,) + shp, lambda hp, g: (hp,) + (0,) * len(shp))
    kv_cols = N_HEADS * HEAD_DIM // share
    return pl.pallas_call(
        body, name=name, grid=(N_HEADS // 2, nstep),
        in_specs=[pl.BlockSpec(memory_space=pltpu.SMEM), slab(qo), whole(ko), whole(vo), slab(0), slab(0), slab(0),
                  per_pair((BLK, 2 * BLK))],
        out_specs=[slab(0), whole(0), whole(0), per_pair((BLK, 2 * BLK)), per_pair((1, LANE))],
        out_shape=[jax.ShapeDtypeStruct((t, N_HEADS * HEAD_DIM), BF16), jax.ShapeDtypeStruct((t, kv_cols), BF16),
                   jax.ShapeDtypeStruct((t, kv_cols), BF16), jax.ShapeDtypeStruct((N_HEADS, BLK, 2 * BLK), F32),
                   jax.ShapeDtypeStruct((N_HEADS, 1, LANE), F32)],
        scratch_shapes=[pltpu.VMEM((t + BLK, LANE), F32), pltpu.VMEM((t + BLK, LANE), F32)],
        compiler_params=_cparams(("arbitrary", "arbitrary")),
    )(sinks, src, src, src, do, lse, dpr, bias)


MLA_V_OFF = N_HEADS


def _diag_mask(size, keys_on_rows=False):
    rows, cols = lax.broadcasted_iota(jnp.int32, (size, size), 0), lax.broadcasted_iota(jnp.int32, (size, size), 1)
    return rows <= cols if keys_on_rows else cols <= rows


def _mla_specs(t, tq):
    blk = lambda f: pl.BlockSpec((tq, LANE), lambda hp, qi, f=f: (qi, f(hp)))
    whole = lambda f: pl.BlockSpec((t, LANE), lambda hp, qi, f=f: (0, f(hp)))
    return blk, whole


def _mla_fwd(q, k, vt, *, name):
    t = q.shape[0]
    n, tq = vt.shape[1], vt.shape[3]
    scale = (C_NOPE + C_ROPE) ** -0.5

    def body(q0_ref, q1_ref, k0_ref, k1_ref, vt_ref, o_ref, lse_ref, m_ref, acc_ref):
        qi = pl.program_id(1)
        qs, ks = (q0_ref[...], q1_ref[...]), (k0_ref, k1_ref)
        m_ref[...] = jnp.full_like(m_ref, NEG)
        acc_ref[...] = jnp.zeros_like(acc_ref)
        first = lax.broadcasted_iota(jnp.int32, (LANE, 1), 0) < HEAD_DIM

        def step(kj, diagonal):
            rows = pl.ds(pl.multiple_of(kj * tq, tq), tq)
            vtb = vt_ref[0, kj]
            one = jnp.ones_like(vtb)
            vts = (jnp.where(first, vtb, one), jnp.where(first, one, vtb))
            for a in range(2):
                s = _dot_nt(ks[a][rows, :], qs[a]) * (scale * LOG2E)
                if diagonal:
                    s = jnp.where(_diag_mask(tq, keys_on_rows=True), s, NEG)
                m_prev = m_ref[a]
                m_new = jnp.maximum(m_prev, jnp.max(s, axis=0, keepdims=True))
                acc_ref[a] = jnp.exp2(m_prev - m_new) * acc_ref[a] + _dot(vts[a], jnp.exp2(s - m_new).astype(BF16))
                m_ref[a] = m_new

        def kloop(kj, c2):
            step(kj, False)
            return c2

        lax.fori_loop(0, qi, kloop, 0)
        step(qi, True)
        l0, l1 = acc_ref[0, HEAD_DIM:HEAD_DIM + 1, :], acc_ref[1, 0:1, :]
        ot = jnp.where(first, acc_ref[0] * (1.0 / l0), acc_ref[1] * (1.0 / l1))
        lset = jnp.where(first, m_ref[0] * LN2 + jnp.log(l0), m_ref[1] * LN2 + jnp.log(l1))
        o_ref[...] = ot.T
        lse_ref[...] = lset.T

    blk, whole = _mla_specs(t, tq)
    return pl.pallas_call(
        body, name=name, grid=(N_HEADS // 2, n),
        in_specs=[blk(lambda hp: 2 * hp), blk(lambda hp: 2 * hp + 1), whole(lambda hp: 2 * hp), whole(lambda hp: 2 * hp + 1),
                  pl.BlockSpec((1, n, LANE, tq), lambda hp, qi: (hp, 0, 0, 0))],
        out_specs=[blk(lambda hp: hp), blk(lambda hp: hp)],
        out_shape=[jax.ShapeDtypeStruct((t, N_HEADS * HEAD_DIM), F32)] * 2,
        scratch_shapes=[pltpu.VMEM((2, 1, tq), F32), pltpu.VMEM((2, LANE, tq), F32)],
        compiler_params=_cparams(("parallel", "parallel")),
    )(q, q, k, k, vt)


def _mla_bwd(q, k, kv, do, lse, delta, *, name):
    t = q.shape[0]
    n = t // TQ
    scale = (C_NOPE + C_ROPE) ** -0.5

    def body(q0_ref, q1_ref, k0_ref, k1_ref, v_ref, do_ref, lse_ref, dl_ref,
             dq_ref, dk_ref, dv_ref, dq_acc, dk_acc, dv_acc):
        qi = pl.program_id(1)
        lo = _lane_lo()

        @pl.when(qi == 0)
        def _():
            dk_acc[...] = jnp.zeros_like(dk_acc)
            dv_acc[...] = jnp.zeros_like(dv_acc)

        dq_acc[...] = jnp.zeros_like(dq_acc)
        qs, ks = (q0_ref[...], q1_ref[...]), (k0_ref, k1_ref)
        do2, lse2, dl2 = do_ref[...], lse_ref[...], dl_ref[...]
        lse_sw, dl_sw = pltpu.roll(lse2, HEAD_DIM, axis=1), pltpu.roll(dl2, HEAD_DIM, axis=1)
        heads = []
        for a in range(2):
            mine = lo if a == 0 else jnp.logical_not(lo)
            heads.append((jnp.where(mine, do2, jnp.zeros_like(do2)), jnp.where(mine, lse2, lse_sw)[:, 0:1] * LOG2E,
                          jnp.where(mine, dl2, dl_sw)[:, 0:1]))

        def step(start, size, diagonal):
            rows = pl.ds(pl.multiple_of(start, TQ), size)
            vb = v_ref[rows, :]
            ps = []
            for a, (doa, lse_a, dl_a) in enumerate(heads):
                kb = ks[a][rows, :]
                s = _dot_nt(qs[a], kb) * (scale * LOG2E)
                if diagonal:
                    s = jnp.where(_diag_mask(TQ), s, NEG)
                p = jnp.exp2(s - lse_a)
                ds = (p * (_dot_nt(doa, vb) - dl_a)).astype(BF16)
                dq_acc[a] += _dot(ds, kb)
                dk_acc[a, rows, :] += _dot_tn(ds, qs[a])
                ps.append(p.astype(BF16))
            dv_acc[rows, :] += _dot_tn(jnp.concatenate(ps, axis=0), jnp.concatenate([h[0] for h in heads], axis=0))

        def kloop(kj, c2):
            step(kj * (2 * TQ), 2 * TQ, False)
            return c2

        lax.fori_loop(0, qi // 2, kloop, 0)

        @pl.when(lax.rem(qi, 2) == 1)
        def _():
            step((qi - 1) * TQ, TQ, False)

        step(qi * TQ, TQ, True)
        dq_ref[:, 0:LANE] = (dq_acc[0] * scale).astype(dq_ref.dtype)
        dq_ref[:, LANE:2 * LANE] = (dq_acc[1] * scale).astype(dq_ref.dtype)

        @pl.when(qi == n - 1)
        def _():
            dk_ref[:, 0:LANE] = (dk_acc[0] * scale).astype(dk_ref.dtype)
            dk_ref[:, LANE:2 * LANE] = (dk_acc[1] * scale).astype(dk_ref.dtype)
            dv_ref[...] = dv_acc[...].astype(dv_ref.dtype)

    blk, whole = _mla_specs(t, TQ)
    even, odd, pair = (lambda hp: 2 * hp), (lambda hp: 2 * hp + 1), (lambda hp: hp)
    wide = jax.ShapeDtypeStruct((t, N_HEADS * LANE), BF16)
    return pl.pallas_call(
        body, name=name, grid=(N_HEADS // 2, n),
        in_specs=[blk(even), blk(odd), whole(even), whole(odd), whole(lambda hp: MLA_V_OFF + hp), blk(pair), blk(pair), blk(pair)],
        out_specs=[pl.BlockSpec((TQ, 2 * LANE), lambda hp, qi: (qi, hp)), pl.BlockSpec((t, 2 * LANE), lambda hp, qi: (0, hp)), whole(pair)],
        out_shape=[wide, wide, jax.ShapeDtypeStruct((t, N_HEADS * HEAD_DIM), BF16)],
        scratch_shapes=[pltpu.VMEM((2, TQ, LANE), F32), pltpu.VMEM((2, t, LANE), F32), pltpu.VMEM((t, LANE), F32)],
        compiler_params=_cparams(("arbitrary", "arbitrary")),
    )(q, q, k, k, kv, do, lse, delta)


def _bias_lookup(bucket, table_t, *, name):
    nh, npos = bucket.shape
    tp = 4096

    def body(b_ref, t_ref, o_ref):
        bk, tab = b_ref[...], t_ref[...]
        acc = jnp.zeros(bk.shape, F32)
        for i in range(REL_BUCKETS):
            acc = jnp.where(bk == i, tab[:, i:i + 1], acc)
        o_ref[...] = acc

    return pl.pallas_call(
        body, name=name, grid=(npos // tp,),
        in_specs=[pl.BlockSpec((nh, tp), lambda i: (0, i)), pl.BlockSpec((nh, REL_BUCKETS), lambda i: (0, 0))],
        out_specs=pl.BlockSpec((nh, tp), lambda i: (0, i)),
        out_shape=jax.ShapeDtypeStruct((nh, npos), F32),
        compiler_params=_cparams(("parallel",)),
    )(bucket, table_t)


def _bias_grad(bucket, ds0, ds1, *, name):
    nh, npos = bucket.shape
    tp = 4096

    def body(b_ref, a_ref, c_ref, o_ref):
        i = pl.program_id(0)
        bk, ds = b_ref[...], a_ref[...] + c_ref[...]
        lane = lax.broadcasted_iota(jnp.int32, (nh, REL_BUCKETS), 1)
        acc = jnp.zeros((nh, REL_BUCKETS), F32)
        for j in range(REL_BUCKETS):
            col = jnp.sum(jnp.where(bk == j, ds, 0.0), axis=1, keepdims=True)
            acc = acc + jnp.where(lane == j, col, 0.0)

        @pl.when(i == 0)
        def _():
            o_ref[...] = acc

        @pl.when(i > 0)
        def _():
            o_ref[...] += acc

    return pl.pallas_call(
        body, name=name, grid=(npos // tp,),
        in_specs=[pl.BlockSpec((nh, tp), lambda i: (0, i))] * 3,
        out_specs=pl.BlockSpec((nh, REL_BUCKETS), lambda i: (0, 0)),
        out_shape=jax.ShapeDtypeStruct((nh, REL_BUCKETS), F32),
        compiler_params=_cparams(("arbitrary",)),
    )(bucket, ds0, ds1)


def _t5_bucket(dist):
    n = jnp.maximum(dist, 0)
    max_exact = REL_BUCKETS // 2
    scaled = jnp.log(jnp.maximum(n, 1).astype(F32) / max_exact) / math.log(REL_MAX_DIST / max_exact)
    large = max_exact + (scaled * (REL_BUCKETS - max_exact)).astype(jnp.int32)
    return jnp.where(n < max_exact, n, jnp.minimum(large, REL_BUCKETS - 1))


def _bucket_index():
    qi = jnp.arange(BLK)[:, None]
    ci = jnp.arange(2 * BLK)[None, :]
    step = BLK + qi - ci
    per_group = [_t5_bucket(step * d).reshape(1, -1) for d in A_DILS + (1,)]
    return jnp.concatenate([jnp.tile(b, (N_HEADS, 1)) for b in per_group], axis=0).astype(jnp.int32)


def _sigmoid(x):
    return 1.0 / (1.0 + jnp.exp(-x))


def _ln_stats(z):
    mu = jnp.mean(z, axis=-1, keepdims=True)
    zc = z - mu
    var = jnp.mean(zc * zc, axis=-1, keepdims=True)
    return zc * lax.rsqrt(var + LN_EPS)


def _ln_fwd(x, mix, g, b, *, name):
    def fn(i, nt, xv, mv, gv, bv):
        z = ALPHA * xv + mv
        y = _ln_stats(z) * gv + bv
        return y, y, z

    c = x.shape[-1]
    y, yb, z = _rowwise(fn, [(x[None], c, 0), (mix[None], c, 0)], pars=[(g.reshape(1, 1, c), c, 0), (b.reshape(1, 1, c), c, 0)],
                        outs=[(c, c, 0, F32), (c, c, 0, BF16), (c, c, 0, F32)], tm=512, name=name)
    return y[0], yb[0], z[0]


def _ln_bwd(z, g, dys, coefs, *, name):
    n = len(dys)

    def fn(i, nt, zv, *rest):
        gv = rest[n]
        dy = coefs[0] * rest[0]
        for cf, t in zip(coefs[1:], rest[1:n]):
            dy = dy + cf * t
        mu = jnp.mean(zv, axis=-1, keepdims=True)
        zc = zv - mu
        r = lax.rsqrt(jnp.mean(zc * zc, axis=-1, keepdims=True) + LN_EPS)
        xh = zc * r
        dxh = dy * gv
        dz = r * (dxh - jnp.mean(dxh, axis=-1, keepdims=True) - xh * jnp.mean(dxh * xh, axis=-1, keepdims=True))
        return dz, dz, jnp.sum(dy * xh, axis=0, keepdims=True), jnp.sum(dy, axis=0, keepdims=True)

    c = z.shape[-1]
    dz, dzb, dg, db = _rowwise(fn, [(z[None], c, 0)] + [(d[None], c, 0) for d in dys], pars=[(g.reshape(1, 1, c), c, 0)],
                               outs=[(c, c, 0, F32), (c, c, 0, BF16)], accs=[(1, c, c, 0), (1, c, c, 0)], tm=512, name=name)
    return dz[0], dzb[0], dg.reshape(c), db.reshape(c)


def _rms_fwd(src, c, off, g, *, name):
    def fn(i, nt, xv, gv):
        return xv * lax.rsqrt(jnp.mean(xv * xv, axis=-1, keepdims=True) + RMS_EPS) * gv

    return _rowwise(fn, [(src[None], c, off)], pars=[(g.reshape(1, 1, c), c, 0)], outs=[(c, c, 0, BF16)], tm=1024, name=name)[0][0]


def _rms_bwd(src, c, off, g, dy, *, name):
    def fn(i, nt, xv, dyv, gv):
        r = lax.rsqrt(jnp.mean(xv * xv, axis=-1, keepdims=True) + RMS_EPS)
        gd = gv * dyv
        dx = gd * r - xv * (r * r * r) * jnp.mean(gd * xv, axis=-1, keepdims=True)
        return dx, jnp.sum(dyv * xv * r, axis=0, keepdims=True)

    dx, dg = _rowwise(fn, [(src[None], c, off), (dy[None], c, 0)], pars=[(g.reshape(1, 1, c), c, 0)],
                      outs=[(c, c, 0, BF16)], accs=[(1, c, c, 0)], tm=1024, name=name)
    return dx[0], dg.reshape(c)


def _rope_slabs(x, n_slab, c, s, *, add=None, to_front=False, name):
    half = C_ROPE // 2

    def fn(i, nt, xv, cv, sv, *rest):
        lane = lax.broadcasted_iota(jnp.int32, (1, LANE), 1)
        extra = pltpu.roll(rest[0], C_NOPE, axis=1) if rest else None
        outs = []
        for h in range(n_slab):
            xs = xv[:, h * LANE:(h + 1) * LANE]
            if extra is not None:
                xs = xs + extra
            swapped = jnp.where(lane < C_NOPE + half, pltpu.roll(xs, LANE - half, axis=1), pltpu.roll(xs, half, axis=1))
            y = xs * cv + swapped * sv
            if to_front:
                y = jnp.where(lane < C_ROPE, pltpu.roll(y, LANE - C_NOPE, axis=1), 0.0)
            outs.append(y)
        return jnp.concatenate(outs, axis=1) if n_slab > 1 else outs[0]

    w = n_slab * LANE
    rows = [(x[None], w, 0), (c[None], LANE, 0), (s[None], LANE, 0)]
    if add is not None:
        rows.append((add[0][None], LANE, add[1]))
    return _rowwise(fn, rows, outs=[(w, w, 0, BF16)], tm=512, name=name)[0][0]


def _merge_fwd(proj, b_gate, ys, *, name):
    def fn(i, nt, g0, g1, g2, ya, yb, yc, bg):
        return (_sigmoid(g0 + bg[:, 0:1024]) * ya + _sigmoid(g1 + bg[:, 1024:2048]) * yb
                + _sigmoid(g2 + bg[:, 2048:3072]) * yc)

    rows = [(proj[None], 1024, j) for j in range(3)] + [(y[None], 1024, 0) for y in ys]
    return _rowwise(fn, rows, pars=[(b_gate.reshape(1, 1, 3072), 3072, 0)], outs=[(1024, 1024, 0, BF16)], tm=512, name=name)[0][0]


def _merge_bwd(proj, b_gate, ys, dm, *, name):
    def fn(i, nt, g0, g1, g2, ya, yb, yc, dmv, bg):
        outs, dgs = [], []
        for j, (gp, y) in enumerate(((g0, ya), (g1, yb), (g2, yc))):
            s = _sigmoid(gp + bg[:, j * 1024:(j + 1) * 1024])
            outs.append(s * dmv)
            dgs.append(dmv * y * s * (1.0 - s))
        return outs + dgs + [jnp.sum(d, axis=0, keepdims=True) for d in dgs]

    rows = [(proj[None], 1024, j) for j in range(3)] + [(y[None], 1024, 0) for y in ys] + [(dm[None], 1024, 0)]
    res = _rowwise(fn, rows, pars=[(b_gate.reshape(1, 1, 3072), 3072, 0)], outs=[(1024, 1024, 0, BF16)] * 6,
                   accs=[(1, 1024, 1024, 0)] * 3, tm=256, name=name)
    dys = [r[0] for r in res[0:3]]
    dgp = [r[0] for r in res[3:6]]
    dbg = jnp.concatenate([r.reshape(1024) for r in res[6:9]])
    return dys, dgp, dbg


def _shift_down(u, halo, i, k):
    ext = jnp.concatenate([jnp.where(i > 0, halo, 0.0), u], axis=0)
    return pltpu.roll(ext, k, axis=0)[HALO:]


def _shift_up(u, halo, i, nt, k):
    ext = jnp.concatenate([u, jnp.where(i < nt - 1, halo, 0.0)], axis=0)
    n = ext.shape[0]
    return pltpu.roll(ext, n - k, axis=0)[:n - HALO]


GLU_C = D_FF // 2


def _conv(u, halo, i, w, b):
    return w[0:1] * _shift_down(u, halo, i, 2) + w[1:2] * _shift_down(u, halo, i, 1) + w[2:3] * u + b


def _glu_fwd(ug, uv, conv_w, conv_b, *, name):
    def fn(i, nt, g, v, hg, hv, wg, wv, bg, bv):
        cg, cv = _conv(g, hg, i, wg, bg), _conv(v, hv, i, wv, bv)
        return cg * _sigmoid(cg) * cv

    w3, b3 = conv_w[None], conv_b.reshape(1, 1, -1)
    c = GLU_C
    return _rowwise(fn, [(ug[None], c, 0), (uv[None], c, 0)], halos=[(ug[None], c, 0, "prev"), (uv[None], c, 0, "prev")],
                    pars=[(w3, c, 0), (w3, c, 2), (b3, c, 0), (b3, c, 2)], outs=[(D_FF, c, 0, BF16)], tm=256, ncol=2, name=name)[0][0]


def _glu_bwd_a(ug, uv, conv_w, conv_b, dh, *, name):
    def fn(i, nt, g, v, dhv, hg, hv, wg, wv, bg, bv):
        g1, g2 = _shift_down(g, hg, i, 1), _shift_down(g, hg, i, 2)
        v1, v2 = _shift_down(v, hv, i, 1), _shift_down(v, hv, i, 2)
        cg = wg[0:1] * g2 + wg[1:2] * g1 + wg[2:3] * g + bg
        cv = wv[0:1] * v2 + wv[1:2] * v1 + wv[2:3] * v + bv
        s = _sigmoid(cg)
        dcv = dhv * cg * s
        dcg = dhv * cv * (s * (1.0 + cg * (1.0 - s)))
        red = lambda a: jnp.sum(a, axis=0, keepdims=True)
        return (dcg, dcv, red(dcg), red(dcv), red(dcg * g2), red(dcg * g1), red(dcg * g),
                red(dcv * v2), red(dcv * v1), red(dcv * v))

    w3, b3 = conv_w[None], conv_b.reshape(1, 1, -1)
    c = GLU_C
    res = _rowwise(fn, [(ug[None], c, 0), (uv[None], c, 0), (dh[None], c, 0)],
                   halos=[(ug[None], c, 0, "prev"), (uv[None], c, 0, "prev")],
                   pars=[(w3, c, 0), (w3, c, 2), (b3, c, 0), (b3, c, 2)],
                   outs=[(D_FF, c, 0, BF16), (D_FF, c, 0, BF16)], accs=[(1, D_FF, c, 0)] * 8, tm=256, ncol=2, name=name)
    dcg, dcv = res[0][0], res[1][0]
    dconv_b = jnp.concatenate([res[2].reshape(D_FF), res[3].reshape(D_FF)])
    dconv_w = jnp.concatenate([jnp.concatenate([res[4 + j].reshape(1, D_FF) for j in range(3)], axis=0),
                               jnp.concatenate([res[7 + j].reshape(1, D_FF) for j in range(3)], axis=0)], axis=1)
    return dcg, dcv, dconv_w, dconv_b


def _glu_bwd_b(dc, conv_w, half, *, name):
    def fn(i, nt, d, hd, w):
        return w[2:3] * d + w[1:2] * _shift_up(d, hd, i, nt, 1) + w[0:1] * _shift_up(d, hd, i, nt, 2)

    c = GLU_C
    return _rowwise(fn, [(dc[None], c, 0)], halos=[(dc[None], c, 0, "next")], pars=[(conv_w[None], c, 2 * half)],
                    outs=[(D_FF, c, 0, BF16)], tm=256, ncol=2, name=name)[0][0]


def _loss_and_grad(y, tgt, *, name):
    def fn(i, nt, yv, tv):
        err = yv - tv
        part = jnp.sum(jnp.sum(err * err, axis=0, keepdims=True), axis=1, keepdims=True) * (0.5 / D_MODEL)
        return err * (1.0 / D_MODEL), jnp.zeros((1, LANE), F32) + part

    dy, part = _rowwise(fn, [(y[None], D_MODEL, 0), (tgt[None], D_MODEL, 0)], outs=[(D_MODEL, D_MODEL, 0, F32)],
                        accs=[(1, LANE, LANE, 0)], tm=512, name=name)
    return dy[0], part.reshape(LANE)


def _lincomb(terms, coefs, *, name):
    def fn(i, nt, *vs):
        acc = coefs[0] * vs[0]
        for cf, v in zip(coefs[1:], vs[1:]):
            acc = acc + cf * v
        return acc

    c = terms[0].shape[-1]
    return _rowwise(fn, [(a[None], c, 0) for a in terms], outs=[(c, c, 0, F32)], tm=512, name=name)[0][0]


def _sum_rows(terms, *, tm, name, dtype=F32):
    def fn(i, nt, *vs):
        acc = vs[0]
        for v in vs[1:]:
            acc = acc + v
        return acc

    c = terms[0].shape[-1]
    return _rowwise(fn, [(t, c, 0) for t in terms], outs=[(c, c, 0, dtype)], tm=tm, name=name)[0]


def _head_sums(x):
    lo = _lane_lo()
    parts = []
    for j in range(x.shape[1] // LANE):
        blk = x[:, j * LANE:(j + 1) * LANE]
        s_lo = jnp.sum(jnp.where(lo, blk, 0.0), axis=1, keepdims=True)
        s_hi = jnp.sum(jnp.where(lo, 0.0, blk), axis=1, keepdims=True)
        parts.append(jnp.where(lo, s_lo, s_hi))
    return jnp.concatenate(parts, axis=1)


def _group_weights(l0, l1, l2):
    m = jnp.maximum(jnp.maximum(l0, l1), l2)
    es = [jnp.exp(l - m) for l in (l0, l1, l2)]
    inv = 1.0 / (es[0] + es[1] + es[2])
    return [e * inv for e in es]


def _combine_fwd(os_, lses, *, name):
    def fn(i, nt, o0, o1, o2, l0, l1, l2):
        w = _group_weights(l0, l1, l2)
        return w[0] * o0 + w[1] * o1 + w[2] * o2

    c = os_[0].shape[-1]
    return _rowwise(fn, [(a[None], c, 0) for a in list(os_) + list(lses)], outs=[(c, c, 0, BF16)], tm=512, name=name)[0][0]


def _combine_bwd(os_, lses, do_a, *, name):
    def fn(i, nt, o0, o1, o2, l0, l1, l2, da):
        ws = _group_weights(l0, l1, l2)
        dws = [_head_sums(da * o) for o in (o0, o1, o2)]
        mean = ws[0] * dws[0] + ws[1] * dws[1] + ws[2] * dws[2]
        return [w * da for w in ws] + [w * mean for w in ws]

    c = do_a.shape[-1]
    res = _rowwise(fn, [(a[None], c, 0) for a in list(os_) + list(lses) + [do_a]], outs=[(c, c, 0, BF16)] * 3 + [(c, c, 0, F32)] * 3,
                   tm=256, name=name)
    return [r[0] for r in res[0:3]], [r[0] for r in res[3:6]]


def _delta(do, o, *, name):
    def fn(i, nt, d, ov):
        return d, _head_sums(d * ov)

    c = do.shape[-1]
    res = _rowwise(fn, [(do[None], c, 0), (o[None], c, 0)], outs=[(c, c, 0, BF16), (c, c, 0, F32)], tm=512, name=name)
    return res[0][0], res[1][0]


def _adamw(w, g, m, v, *, name):
    c1 = 1.0 - ADAM_B1 ** ADAM_STEP
    c2 = 1.0 - ADAM_B2 ** ADAM_STEP

    def fn(i, nt, wv, gv, mv, vv):
        mn = ADAM_B1 * mv + (1.0 - ADAM_B1) * gv
        vn = ADAM_B2 * vv + (1.0 - ADAM_B2) * (gv * gv)
        delta = -ADAM_LR * ((mn / c1) / (jnp.sqrt(vn / c2) + ADAM_EPS) + ADAM_WD * wv)
        return delta, mn, vn

    r, c = w.shape
    rp = _ceil_to(r, 8)
    pad = lambda a: jnp.pad(a, ((0, rp - r), (0, 0))) if rp != r else a
    tm = rp
    for cand in (128, 64, 32, 16, 8):
        if rp % cand == 0:
            tm = cand
            break
    res = _rowwise(fn, [(pad(a)[None], c, 0) for a in (w, g, m, v)], outs=[(c, c, 0, F32)] * 3, tm=tm, name=name)
    return [x[0][:r] for x in res]


ANY = pl.BlockSpec(memory_space=pl.ANY)


def _place():
    x, y, c = lax.axis_index("x"), lax.axis_index("y"), lax.axis_index("c")
    chips = [(1 - x, y), (x, 1 - y), (1 - x, 1 - y)]
    return x, y, c, chips


def _allgather_weights(arrs):
    n = len(arrs)

    def body(*refs):
        ins, outs, send_sems, recv_sems = refs[:n], refs[n:2 * n], refs[2 * n], refs[2 * n + 1]
        x, y, c, chips = _place()
        j = 2 * x + y
        me, sibling = (x, y, c), (x, y, 1 - c)

        def cp(i, k, src, chip_idx, half, to):
            return pltpu.make_async_remote_copy(src_ref=src, dst_ref=outs[i].at[chip_idx, half], send_sem=send_sems.at[k],
                                                recv_sem=recv_sems.at[k], device_id=to, device_id_type=MESH)

        first, passed, own = [], [], []
        for i in range(n):
            for r, (cx, cy) in enumerate(chips):
                first.append(cp(i, 3 * i + r, ins[i].at[c], j, c, (cx, cy, c)))
                passed.append(cp(i, 3 * (n + i) + r, outs[i].at[2 * cx + cy, c], 2 * cx + cy, c, sibling))
            own += [cp(i, 6 * n + 2 * i + half, ins[i].at[half], j, half, sibling) for half in range(2)]
        for d in first + own:
            d.start()
        for i in range(n):
            for r, (cx, cy) in enumerate(chips):
                cp(i, 3 * i + r, ins[i].at[c], 2 * cx + cy, c, me).wait_recv()
                passed[3 * i + r].start()
        for i in range(n):
            for r, (cx, cy) in enumerate(chips):
                cp(i, 3 * (n + i) + r, ins[i].at[c], 2 * cx + cy, 1 - c, me).wait_recv()
        for d in own:
            d.wait_recv()
        for d in first + passed + own:
            d.wait_send()

    return pl.pallas_call(
        body, name="allgather_weights", in_specs=[ANY] * n, out_specs=[ANY] * n,
        out_shape=[jax.ShapeDtypeStruct((N_CHIP,) + a.shape, a.dtype) for a in arrs],
        scratch_shapes=[pltpu.SemaphoreType.DMA((8 * n,)), pltpu.SemaphoreType.DMA((8 * n,))],
    )(*arrs)


def _sibling_swap(gs):
    n = len(gs)

    def body(*refs):
        layers, outs, send_sems, recv_sems = (refs[:n], refs[n:2 * n]), refs[2 * n:3 * n], refs[3 * n], refs[3 * n + 1]
        x, y, c, _ = _place()

        def copies(srcs):
            return [pltpu.make_async_remote_copy(src_ref=srcs[i], dst_ref=outs[i], send_sem=send_sems.at[i], recv_sem=recv_sems.at[i],
                                                 device_id=(x, y, 1 - c), device_id_type=MESH) for i in range(n)]

        for layer in range(DEPTH):
            @pl.when(c == 1 - layer)
            def _(layer=layer):
                for d in copies(layers[layer]):
                    d.start()

        waits = copies(layers[0])
        for d in waits:
            d.wait_recv()
        for d in waits:
            d.wait_send()

    return pl.pallas_call(
        body, name="grad_sibling_swap", in_specs=[ANY] * (2 * n), out_specs=[ANY] * n,
        out_shape=[jax.ShapeDtypeStruct(g0.shape, g0.dtype) for g0, _ in gs],
        scratch_shapes=[pltpu.SemaphoreType.DMA((n,)), pltpu.SemaphoreType.DMA((n,))],
    )(*[g0 for g0, _ in gs], *[g1 for _, g1 in gs])


def _chip_scatter(ps):
    n = len(ps)

    def body(*refs):
        ins, outs, send_sems, recv_sems = refs[:n], refs[n:2 * n], refs[2 * n], refs[2 * n + 1]
        x, y, c, chips = _place()
        sends = []
        for i in range(n):
            for r, (cx, cy) in enumerate(chips):
                sends.append(pltpu.make_async_remote_copy(src_ref=ins[i].at[2 * cx + cy], dst_ref=outs[i].at[r], send_sem=send_sems.at[3 * i + r],
                                                          recv_sem=recv_sems.at[3 * i + r], device_id=(cx, cy, c), device_id_type=MESH))
        for d in sends:
            d.start()
        for d in sends:
            d.wait_recv()
        for d in sends:
            d.wait_send()

    return pl.pallas_call(
        body, name="grad_chip_scatter", in_specs=[ANY] * n, out_specs=[ANY] * n,
        out_shape=[jax.ShapeDtypeStruct((3,) + p.shape[1:], p.dtype) for p in ps],
        scratch_shapes=[pltpu.SemaphoreType.DMA((3 * n,)), pltpu.SemaphoreType.DMA((3 * n,))],
    )(*ps)


def _sibling_share(rs):
    n = len(rs)

    def body(*refs):
        ins, outs, send_sems, recv_sems = refs[:n], refs[n:2 * n], refs[2 * n], refs[2 * n + 1]
        x, y, c, _ = _place()
        cps = [pltpu.make_async_remote_copy(src_ref=ins[i], dst_ref=outs[i], send_sem=send_sems.at[i], recv_sem=recv_sems.at[i],
                                            device_id=(x, y, 1 - c), device_id_type=MESH) for i in range(n)]
        for d in cps:
            d.start()
        for d in cps:
            d.wait_recv()
        for d in cps:
            d.wait_send()

    return pl.pallas_call(
        body, name="grad_sibling_share", in_specs=[ANY] * n, out_specs=[ANY] * n,
        out_shape=[jax.ShapeDtypeStruct(r.shape, r.dtype) for r in rs],
        scratch_shapes=[pltpu.SemaphoreType.DMA((n,)), pltpu.SemaphoreType.DMA((n,))],
    )(*rs)


def _allreduce_small(s):
    rows, w = s.shape
    n_dev = 8

    def body(s_ref, out_ref, slots, send_sems, recv_sems):
        x, y, c, _ = _place()
        me = 4 * x + 2 * y + c
        slots[me] = s_ref[...]
        peers = []
        for r in range(1, n_dev):
            px = 1 - x if r & 4 else x
            py = 1 - y if r & 2 else y
            pc = 1 - c if r & 1 else c
            peers.append((px, py, pc))
        sends = [pltpu.make_async_remote_copy(src_ref=s_ref, dst_ref=slots.at[me], send_sem=send_sems.at[r], recv_sem=recv_sems.at[r],
                                              device_id=peer, device_id_type=MESH) for r, peer in enumerate(peers)]
        for d in sends:
            d.start()
        for r, (px, py, pc) in enumerate(peers):
            pltpu.make_async_remote_copy(src_ref=s_ref, dst_ref=slots.at[4 * px + 2 * py + pc], send_sem=send_sems.at[r],
                                         recv_sem=recv_sems.at[r], device_id=(x, y, c), device_id_type=MESH).wait_recv()
        for d in sends:
            d.wait_send()
        acc = slots[0]
        for k in range(1, n_dev):
            acc = acc + slots[k]
        out_ref[...] = acc

    vm = pl.BlockSpec(memory_space=pltpu.VMEM)
    return pl.pallas_call(
        body, name="allreduce_small", in_specs=[vm], out_specs=vm, out_shape=jax.ShapeDtypeStruct((rows, w), F32),
        scratch_shapes=[pltpu.VMEM((n_dev, rows, w), F32), pltpu.SemaphoreType.DMA((n_dev - 1,)), pltpu.SemaphoreType.DMA((n_dev - 1,))],
    )(s)


W_IN_SHARD = D_IN // N_CHIP
W_IN_ROWS_G = 2304
REDUCED = tuple(m for m in MATS if m[0] != "conv_w")
CONV_W_SIZE = 3 * 2 * D_FF


def _weight_send(name, a):
    if name == "w_in":
        return jnp.swapaxes(a, 1, 2).astype(BF16)
    return a if name == "conv_w" else a.astype(BF16)


def _full_weights(gathered, l):
    g = {k: v[:, l] for k, v in gathered.items()}
    s = g["w_in"].astype(F32).reshape(D_IN, D_MODEL)
    dup = lambda a: jnp.concatenate([a[0:64], a[0:64], a[64:128], a[64:128]], axis=0)
    o = ORIG
    wm_t = jnp.concatenate([s[o["gate"]:], s[o["a"]:o["a"] + A_COLS], s[o["bq"]:o["bk"]], dup(s[o["bk"]:o["bv"]]), dup(s[o["bv"]:o["cq"]]),
                            s[o["cq"]:o["gate"]], jnp.zeros((M_COLS - M_CDKV - (o["gate"] - o["cdkv"]), D_MODEL), F32)], axis=0).astype(BF16)
    wg_t = [s[o["a"] + gi * A_COLS:o["a"] + (gi + 1) * A_COLS].astype(BF16) for gi in (1, 2)]
    full = {name: jnp.moveaxis(g[name], 0, ax).reshape(shape) for name, shape, ax in MATS if name != "w_in"}
    uq = full["w_uq"].reshape(C_Q_RANK, N_HEADS, C_NOPE + C_ROPE)
    ukv = full["w_ukv"].reshape(C_KV_RANK, N_HEADS, 2 * C_NOPE)
    w_uq_p = _pad_lanes(uq).reshape(C_Q_RANK, N_HEADS * LANE)
    w_ukv_p = jnp.concatenate([_pad_lanes(ukv[:, :, :C_NOPE]).reshape(C_KV_RANK, N_HEADS * LANE),
                               ukv[:, :, C_NOPE:].reshape(C_KV_RANK, N_HEADS * HEAD_DIM)], axis=1)
    return {"wm_t": wm_t, "wg_t": wg_t, "w_uq_p": w_uq_p, "w_ukv_p": w_ukv_p, "w_branch": full["w_branch"], "w_out": full["w_out"],
            "wup_g": full["w_ffn_up"][:, :D_FF], "wup_v": full["w_ffn_up"][:, D_FF:], "conv_w": full["conv_w"],
            "w_ffn_down": full["w_ffn_down"]}


def _grad_send(name, g, shape, ax):
    if name == "w_in":
        return jnp.pad(g.reshape(N_CHIP, W_IN_SHARD, D_MODEL), ((0, 0), (0, W_IN_ROWS_G - W_IN_SHARD), (0, 0)))
    split = shape[:ax] + (N_CHIP, shape[ax] // N_CHIP) + shape[ax + 1:]
    return jnp.moveaxis(g.reshape(split), ax, 0)


def _grad_recv(name, r):
    return r[:W_IN_SHARD].T if name == "w_in" else r


def _pack_small(rel, small, conv_w, extra):
    parts = [rel.reshape(-1)]
    for l in range(DEPTH):
        for name in SMALL:
            parts.append(small[name][l].reshape(-1))
    parts += [conv_w.reshape(-1), extra]
    flat = jnp.concatenate(parts)
    rows = _ceil_to(-(-flat.shape[0] // LANE), 8)
    return jnp.pad(flat, (0, rows * LANE - flat.shape[0])).reshape(rows, LANE)


def _unpack_small(buf):
    flat = buf.reshape(-1)
    rel = flat[:REL_BUCKETS * 32].reshape(REL_BUCKETS, 32)
    off = REL_BUCKETS * 32
    small = {name: [] for name in SMALL}
    for l in range(DEPTH):
        for name in SMALL:
            n = SMALL_SIZES[name]
            small[name].append(flat[off:off + n])
            off += n
    conv_w = flat[off:off + DEPTH * CONV_W_SIZE].reshape(DEPTH, 3, 2 * D_FF)
    off += DEPTH * CONV_W_SIZE
    return rel, {k: jnp.stack(v) for k, v in small.items()}, conv_w, flat[off:off + LANE]


def _rows2d(a, lead):
    return a.reshape(a.shape[:lead] + (-1, a.shape[-1]))


def _row_tile(rows):
    for cand in (512, 256, 128, 64, 32, 16, 8):
        if rows % cand == 0:
            return cand
    raise ValueError(rows)


def _pair_add(g, got, core, *, name):
    g0, g1, got2 = _rows2d(g[0], 0), _rows2d(g[1], 0), _rows2d(got, 0)
    rows, c = got2.shape
    flag = jnp.zeros((1, 1, LANE), F32) + core.astype(F32)

    def fn(i, nt, a0, a1, b, f):
        return jnp.where(f[:, 0:1] == 0.0, a0, a1) + b

    out = _rowwise(fn, [(g0[None], c, 0), (g1[None], c, 0), (got2[None], c, 0)], pars=[(flag, LANE, 0)],
                   outs=[(c, c, 0, BF16)], tm=_row_tile(rows), name=name)[0][0]
    return out.reshape(got.shape)


def _chip_add(own, got, *, name):
    own2, got2 = _rows2d(own, 0), _rows2d(got, 1)
    rows, c = own2.shape
    tm = _row_tile(rows)

    def fn(i, nt, a, b0, b1, b2):
        return ((a.astype(F32) + b0.astype(F32)) + b1.astype(F32)) + b2.astype(F32)

    stacked = got2.reshape(1, 3 * rows, c)
    out = _rowwise(fn, [(own2[None], c, 0)] + [(stacked, c, 0, k * (rows // tm)) for k in range(3)],
                   outs=[(c, c, 0, F32)], tm=tm, t=rows, name=name)[0][0]
    return out.reshape(own.shape)


def _perm(a, d):
    if d == 1:
        return a
    t = a.shape[0]
    return jnp.swapaxes(a.reshape((t // d, d) + a.shape[1:]), 0, 1).reshape(a.shape)


def _unperm(a, d):
    if d == 1:
        return a
    t = a.shape[0]
    return jnp.swapaxes(a.reshape((d, t // d) + a.shape[1:]), 0, 1).reshape(a.shape)


def _pad_lanes(a, w=LANE):
    return jnp.pad(a, [(0, 0)] * (a.ndim - 1) + [(0, w - a.shape[-1])])


def _rope_tables(t):
    pos = jnp.arange(t, dtype=F32)
    inv_freq = ROPE_BASE ** (-jnp.arange(0, C_ROPE, 2, dtype=F32) / C_ROPE)
    ang = pos[:, None] * inv_freq[None, :]
    cos, sin = jnp.cos(ang), jnp.sin(ang)
    ones, zeros = jnp.ones((t, C_NOPE), F32), jnp.zeros((t, C_NOPE), F32)
    tail = LANE - C_NOPE - C_ROPE
    c = jnp.concatenate([ones, cos, cos, ones[:, :tail]], axis=1)
    s = jnp.concatenate([zeros, -sin, sin, zeros[:, :tail]], axis=1)
    return c, s


def _band_calls(t, proj, projs_g, sinks):
    none = jnp.full((N_HEADS,), NEG, F32)
    a0 = M_A0 // LANE
    calls = [(proj, (a0, a0 + 4, a0 + 8), t // BLK, False, none)]
    calls += [(pg, (0, 4, 8), t // (d * BLK), False, none) for pg, d in zip(projs_g, A_DILS[1:])]
    calls.append((proj, (M_BQ // LANE, M_BK // LANE, M_BV // LANE), t // BLK, True, sinks.astype(F32)))
    return calls


def _layer_fwd(l, x, xb, w, p, biases, rope_cs):
    t = x.shape[0]
    n = f"l{l}_"
    xps = [_perm(xb, d) for d in A_DILS[1:]]
    proj = _mm(xb, w["wm_t"], tb=True, out_dtype=BF16, tm=TM_TOKENS, name=n + "proj")
    projs_g = [_mm(xp, wg, tb=True, out_dtype=BF16, tm=TM_TOKENS, tn=768, name=n + f"proj_g{i + 1}")
               for i, (xp, wg) in enumerate(zip(xps, w["wg_t"]))]
    s = {"xb": xb, "xps": xps, "proj": proj, "projs_g": projs_g}

    calls = _band_calls(t, proj, projs_g, p["sinks"])
    outs = [_band_fwd(src, offs, biases[i], sk, nb=nb, gqa=gqa, name=n + f"band{i}")
            for i, (src, offs, nb, gqa, sk) in enumerate(calls)]
    os_ = [_unperm(outs[gi][0], d) for gi, d in enumerate(A_DILS)]
    lses = [_unperm(outs[gi][1], d) for gi, d in enumerate(A_DILS)]
    o_a = _combine_fwd(os_, lses, name=n + "combine_fwd")
    o_b_f, lse_b = outs[3]
    o_b = o_b_f.astype(BF16)
    s.update(os=os_, lses=lses, lses_p=[outs[gi][1] for gi in range(3)], o_b=o_b_f, lse_b=lse_b)

    rq = _rms_fwd(proj, C_Q_RANK, M_CQ // C_Q_RANK, p["q_norm_g"], name=n + "rms_q")
    rkv = _rms_fwd(proj, C_KV_RANK, M_CDKV // C_KV_RANK, p["kv_norm_g"], name=n + "rms_kv")
    q_cp = _mm(rq, w["w_uq_p"], out_dtype=BF16, name=n + "uq")
    kv_cp = _mm(rkv, w["w_ukv_p"], out_dtype=BF16, name=n + "ukv")
    q_full = _rope_slabs(q_cp, N_HEADS, rope_cs[0], rope_cs[1], name=n + "rope_q")
    k_full = _rope_slabs(kv_cp, N_HEADS, rope_cs[0], rope_cs[1], add=(proj, (M_CDKV + C_KV_RANK) // LANE), name=n + "rope_k")
    tq = min(TQ_FWD, t)
    vt = jnp.transpose(kv_cp[:, N_HEADS * LANE:].T.reshape(N_HEADS // 2, LANE, t // tq, tq), (0, 2, 1, 3))
    o_c_f, lse_c = _mla_fwd(q_full, k_full, vt, name=n + "mla_fwd")
    o_c = o_c_f.astype(BF16)
    s.update(rq=rq, rkv=rkv, q_full=q_full, k_full=k_full, kv_cp=kv_cp, lse_c=lse_c, o_c=o_c_f)

    obs = [o_a, o_b, o_c]
    ys = [_mm(o, w["w_branch"][i], out_dtype=BF16, name=n + f"branch{i}") for i, o in enumerate(obs)]
    merged = _merge_fwd(proj, p["b_gate"], ys, name=n + "merge")
    mix = _mm(merged, w["w_out"], name=n + "out")
    x1f, x1b, z1 = _ln_fwd(x, mix, p["ln1_g"], p["ln1_b"], name=n + "ln1")
    s.update(obs=obs, ys=ys, merged=merged, z1=z1, x1b=x1b)

    ug = _mm(x1b, w["wup_g"], tm=TM_TOKENS, tn=1408, out_dtype=BF16, name=n + "up_g")
    uv = _mm(x1b, w["wup_v"], tm=TM_TOKENS, tn=1408, out_dtype=BF16, name=n + "up_v")
    h = _glu_fwd(ug, uv, w["conv_w"], p["conv_b"], name=n + "glu")
    ff = _mm(h, w["w_ffn_down"], tm=TM_TOKENS, tk=1408, name=n + "down")
    x2f, x2b, z2 = _ln_fwd(x1f, ff, p["ln2_g"], p["ln2_b"], name=n + "ln2")
    s.update(ug=ug, uv=uv, h=h, z2=z2)
    return x2f, x2b, s


def _layer_bwd(l, s, dys, coefs, w, p, biases, rope_cs):
    n = f"l{l}b_"
    t = s["z2"].shape[0]
    gw, gs = {}, {}

    dz2, dz2b, gs["ln2_g"], gs["ln2_b"] = _ln_bwd(s["z2"], p["ln2_g"], dys, coefs, name=n + "ln2")
    dh = _mm(dz2b, w["w_ffn_down"], tb=True, tm=TM_TOKENS, tn=1408, out_dtype=BF16, name=n + "d_h")
    gw["w_ffn_down"] = _mm(s["h"], dz2b, ta=True, tm=1408, tk=1024, name=n + "g_down")
    dcg, dcv, gw["conv_w"], gs["conv_b"] = _glu_bwd_a(s["ug"], s["uv"], w["conv_w"], p["conv_b"], dh, name=n + "glu_a")
    dug = _glu_bwd_b(dcg, w["conv_w"], 0, name=n + "glu_bg")
    duv = _glu_bwd_b(dcv, w["conv_w"], 1, name=n + "glu_bv")
    dx1_g = _mm(dug, w["wup_g"], tb=True, tm=TM_TOKENS, tk=1408, name=n + "d_x1g")
    dx1_v = _mm(duv, w["wup_v"], tb=True, tm=TM_TOKENS, tk=1408, name=n + "d_x1v")
    gw["w_ffn_up"] = jnp.concatenate([_mm(s["x1b"], dug, ta=True, tn=1408, tk=1024, name=n + "g_upg"),
                                      _mm(s["x1b"], duv, ta=True, tn=1408, tk=1024, name=n + "g_upv")], axis=1)

    dz1, dz1b, gs["ln1_g"], gs["ln1_b"] = _ln_bwd(s["z1"], p["ln1_g"], [dz2, dx1_g, dx1_v], [ALPHA, 1.0, 1.0], name=n + "ln1")
    dmerged = _mm(dz1b, w["w_out"], tb=True, out_dtype=BF16, name=n + "d_merged")
    gw["w_out"] = _mm(s["merged"], dz1b, ta=True, name=n + "g_out")
    dys_b, dgp, gs["b_gate"] = _merge_bwd(s["proj"], p["b_gate"], s["ys"], dmerged, name=n + "merge")
    dos = [_mm(dy, w["w_branch"][i], tb=True, out_dtype=BF16, name=n + f"d_o{i}") for i, dy in enumerate(dys_b)]
    gw["w_branch"] = jnp.stack([_mm(o, dy, ta=True, name=n + f"g_branch{i}") for i, (o, dy) in enumerate(zip(s["obs"], dys_b))])

    do_gs, dpr_gs = _combine_bwd(s["os"], s["lses"], dos[0], name=n + "combine")
    do_b, dpr_b = _delta(dos[1], s["o_b"], name=n + "delta_b")
    do_list = [_perm(a, d) for a, d in zip(do_gs, A_DILS)] + [do_b]
    dpr_list = [_perm(a, d) for a, d in zip(dpr_gs, A_DILS)] + [dpr_b]
    lse_list = s["lses_p"] + [s["lse_b"]]
    calls = _band_calls(t, s["proj"], s["projs_g"], p["sinks"])
    band = [_band_bwd(src, offs, do_list[i], lse_list[i], dpr_list[i], biases[i], sk, nb=nb, gqa=gqa, name=n + f"band{i}")
            for i, (src, offs, nb, gqa, sk) in enumerate(calls)]
    gs["sinks"] = band[3][4][:, 0, 0]
    ds_sum = jnp.concatenate([b_[3] for b_ in band], axis=0)

    do_c, delta_c = _delta(dos[2], s["o_c"], name=n + "delta_c")
    dq, dk, dv = _mla_bwd(s["q_full"], s["k_full"], s["kv_cp"], do_c, s["lse_c"], delta_c, name=n + "mla")
    dq_cp = _rope_slabs(dq, N_HEADS, rope_cs[0], -rope_cs[1], name=n + "rope_q")
    dk_sum = _rowwise(lambda i, nt, *vs: sum(vs[1:], vs[0]), [(dk[None], LANE, hh) for hh in range(N_HEADS)],
                      outs=[(LANE, LANE, 0, F32)], tm=1024, name=n + "krope_sum")[0][0]
    dkr = _rope_slabs(dk_sum, 1, rope_cs[0], -rope_cs[1], to_front=True, name=n + "rope_k")
    dkv_cp = jnp.concatenate([dk, dv], axis=1)
    d_rq = _mm(dq_cp, w["w_uq_p"], tb=True, name=n + "d_rq")
    d_rkv = _mm(dkv_cp, w["w_ukv_p"], tb=True, name=n + "d_rkv")
    g_uq = _mm(s["rq"], dq_cp, ta=True, name=n + "g_uq")
    g_ukv = _mm(s["rkv"], dkv_cp, ta=True, name=n + "g_ukv")
    gw["w_uq"] = g_uq.reshape(C_Q_RANK, N_HEADS, LANE)[:, :, :C_NOPE + C_ROPE].reshape(C_Q_RANK, -1)
    kw = N_HEADS * LANE
    gw["w_ukv"] = jnp.concatenate([g_ukv[:, :kw].reshape(C_KV_RANK, N_HEADS, LANE)[:, :, :C_NOPE],
                                   g_ukv[:, kw:].reshape(C_KV_RANK, N_HEADS, HEAD_DIM)], axis=2).reshape(C_KV_RANK, -1)
    dcq, gs["q_norm_g"] = _rms_bwd(s["proj"], C_Q_RANK, M_CQ // C_Q_RANK, p["q_norm_g"], d_rq, name=n + "rms_q")
    dckv, gs["kv_norm_g"] = _rms_bwd(s["proj"], C_KV_RANK, M_CDKV // C_KV_RANK, p["kv_norm_g"], d_rkv, name=n + "rms_kv")
    dcdkv = jnp.concatenate([dckv, dkr], axis=1)

    dproj = jnp.concatenate(dgp + list(band[0][:3]) + list(band[3][:3]) + [dcq, dcdkv], axis=1)
    dprojs_g = [jnp.concatenate(band[gi][:3], axis=1) for gi in (1, 2)]
    dx_terms = [_mm(dproj, w["wm_t"], tm=TM_TOKENS, tk=1024, name=n + "d_x")]
    dx_terms += [_unperm(_mm(dp, wg, tm=TM_TOKENS, tk=768, name=n + f"d_x_g{i + 1}"), d)
                 for i, (dp, wg, d) in enumerate(zip(dprojs_g, w["wg_t"], A_DILS[1:]))]
    g_main = _mm(dproj, s["xb"], ta=True, name=n + "g_in")
    g_groups = [_mm(dp, xp, ta=True, tm=768, name=n + f"g_in_g{i + 1}") for i, (xp, dp) in enumerate(zip(s["xps"], dprojs_g))]
    fold = lambda a, tag: _sum_rows([a.reshape(2, 2, HEAD_DIM, D_MODEL)[:, j] for j in range(2)], tm=HEAD_DIM,
                                    name=n + "g_fold_" + tag).reshape(2 * HEAD_DIM, D_MODEL)
    gw["w_in"] = jnp.concatenate([g_main[M_A0:M_BQ], g_groups[0], g_groups[1], g_main[M_BQ:M_BK], fold(g_main[M_BK:M_BV], "k"),
                                  fold(g_main[M_BV:M_CQ], "v"), g_main[M_CQ:M_CDKV + C_KV_RANK + C_ROPE], g_main[M_GATE:M_A0]], axis=0)
    return [dz1] + dx_terms, [ALPHA, 1.0, 1.0, 1.0], gw, gs, ds_sum


def _local_step(x, target, ws, rel_table, small):
    t = x.shape[0]
    ps = [{k: small[k][l] for k in SMALL} for l in range(DEPTH)]
    bucket = _bucket_index()
    bias_all = _bias_lookup(bucket, rel_table.T, name="bias_lookup").reshape(4, N_HEADS, BLK, 2 * BLK)
    step = BLK + jnp.arange(BLK)[:, None] - jnp.arange(2 * BLK)[None, :]
    biases = [jnp.where((step >= 0) & (step <= lim), bias_all[i], NEG) for i, lim in enumerate((BLK, BLK, BLK, BLK - 1))]
    rope_cs = _rope_tables(t)

    saved, h, hb = [], x, x.astype(BF16)
    for l in range(DEPTH):
        h, hb, s = _layer_fwd(l, h, hb, ws[l], ps[l], biases, rope_cs)
        saved.append(s)
    dy, loss_part = _loss_and_grad(h, target, name="loss")

    dys, coefs = [dy], [1.0]
    gws, gss, dss = [None] * DEPTH, [None] * DEPTH, [None] * DEPTH
    for l in reversed(range(DEPTH)):
        dys, coefs, gws[l], gss[l], dss[l] = _layer_bwd(l, saved[l], dys, coefs, ws[l], ps[l], biases, rope_cs)
    grad_x = _lincomb(dys, coefs, name="grad_x")
    npos = 2 * BLK * BLK
    g_rel = _bias_grad(bucket, dss[0].reshape(4 * N_HEADS, npos), dss[1].reshape(4 * N_HEADS, npos), name="bias_grad").T
    gsmall = {k: jnp.stack([gss[l][k] for l in range(DEPTH)]) for k in SMALL}
    return loss_part, grad_x, gws, gsmall, g_rel


def kernel(x, rel_table, w_in, b_gate, sinks, q_norm_g, kv_norm_g, w_uq, w_ukv, w_branch, w_out, ln1_g, ln1_b, w_ffn_up, conv_w, conv_b, w_ffn_down, ln2_g, ln2_b, loss_target, m_rel_table, m_w_in, m_b_gate, m_sinks, m_q_norm_g, m_kv_norm_g, m_w_uq, m_w_ukv, m_w_branch, m_w_out, m_ln1_g, m_ln1_b, m_w_ffn_up, m_conv_w, m_conv_b, m_w_ffn_down, m_ln2_g, m_ln2_b, v_rel_table, v_w_in, v_b_gate, v_sinks, v_q_norm_g, v_kv_norm_g, v_w_uq, v_w_ukv, v_w_branch, v_w_out, v_ln1_g, v_ln1_b, v_w_ffn_up, v_conv_w, v_conv_b, v_w_ffn_down, v_ln2_g, v_ln2_b):
    wts = dict(rel_table=rel_table, w_in=w_in, b_gate=b_gate, sinks=sinks, q_norm_g=q_norm_g, kv_norm_g=kv_norm_g, w_uq=w_uq,
               w_ukv=w_ukv, w_branch=w_branch, w_out=w_out, ln1_g=ln1_g, ln1_b=ln1_b, w_ffn_up=w_ffn_up, conv_w=conv_w,
               conv_b=conv_b, w_ffn_down=w_ffn_down, ln2_g=ln2_g, ln2_b=ln2_b)
    ms = dict(rel_table=m_rel_table, w_in=m_w_in, b_gate=m_b_gate, sinks=m_sinks, q_norm_g=m_q_norm_g, kv_norm_g=m_kv_norm_g,
              w_uq=m_w_uq, w_ukv=m_w_ukv, w_branch=m_w_branch, w_out=m_w_out, ln1_g=m_ln1_g, ln1_b=m_ln1_b, w_ffn_up=m_w_ffn_up,
              conv_w=m_conv_w, conv_b=m_conv_b, w_ffn_down=m_w_ffn_down, ln2_g=m_ln2_g, ln2_b=m_ln2_b)
    vs = dict(rel_table=v_rel_table, w_in=v_w_in, b_gate=v_b_gate, sinks=v_sinks, q_norm_g=v_q_norm_g, kv_norm_g=v_kv_norm_g,
              w_uq=v_w_uq, w_ukv=v_w_ukv, w_branch=v_w_branch, w_out=v_w_out, ln1_g=v_ln1_g, ln1_b=v_ln1_b, w_ffn_up=v_w_ffn_up,
              conv_w=v_conv_w, conv_b=v_conv_b, w_ffn_down=v_w_ffn_down, ln2_g=v_ln2_g, ln2_b=v_ln2_b)

    core = lax.axis_index("c")
    chip = 2 * lax.axis_index("x") + lax.axis_index("y")

    names = [name for name, _, _ in MATS]
    gathered = dict(zip(names, _allgather_weights([_weight_send(name, wts[name]) for name in names])))
    ws = [_full_weights(gathered, l) for l in range(DEPTH)]

    small = {k: wts[k] for k in SMALL}
    loss_part, grad_x, gws, gsmall, g_rel = _local_step(x[0], loss_target[0], ws, rel_table, small)

    rnames = [name for name, _, _ in REDUCED]
    gsend = [tuple(_grad_send(name, gws[l][name], shape, ax) for l in range(DEPTH)) for name, shape, ax in REDUCED]
    theirs = _sibling_swap(gsend)
    pairs = [_pair_add(g, t_, core, name="grad_pair_" + name) for name, g, t_ in zip(rnames, gsend, theirs)]
    arrived = _chip_scatter(pairs)
    reduced = [_chip_add(lax.dynamic_index_in_dim(p, chip, 0, keepdims=False), a, name="grad_chip_" + name)
               for name, p, a in zip(rnames, pairs, arrived)]
    others = _sibling_share(reduced)
    gshard = {}
    for name, mine, other in zip(rnames, reduced, others):
        layers = [jnp.where(core == l, mine, other) for l in range(DEPTH)]
        gshard[name] = jnp.stack([_grad_recv(name, a) for a in layers])

    conv_w_full = jnp.stack([gws[l]["conv_w"] for l in range(DEPTH)])
    small_red = _allreduce_small(_pack_small(g_rel, gsmall, conv_w_full, loss_part))
    g_rel_r, gsmall_r, conv_w_r, loss_vec = _unpack_small(small_red)
    loss = loss_vec[0]
    shard_w = 2 * D_FF // N_CHIP
    gshard["conv_w"] = lax.dynamic_slice_in_dim(conv_w_r, chip * shard_w, shard_w, axis=2)

    grads = dict(gshard)
    grads.update(gsmall_r)
    grads["rel_table"] = g_rel_r
    deltas, new_m, new_v = {}, {}, {}
    for name, _, _ in MATS:
        shp = wts[name].shape
        v2 = lambda a: a.reshape(-1, shp[-1])
        d_, m_, v_ = _adamw(v2(wts[name]), v2(grads[name]), v2(ms[name]), v2(vs[name]), name="adamw_" + name)
        deltas[name], new_m[name], new_v[name] = d_.reshape(shp), m_.reshape(shp), v_.reshape(shp)
    zero, none = jnp.zeros((LANE,), F32), jnp.zeros((0,), F32)
    sw = _pack_small(wts["rel_table"], {k: wts[k] for k in SMALL}, none, zero)
    sm = _pack_small(ms["rel_table"], {k: ms[k] for k in SMALL}, none, zero)
    sv = _pack_small(vs["rel_table"], {k: vs[k] for k in SMALL}, none, zero)
    sg = _pack_small(g_rel_r, gsmall_r, none, zero)
    sd, smn, svn = _adamw(sw, sg, sm, sv, name="adamw_small")
    for res, buf in ((deltas, sd), (new_m, smn), (new_v, svn)):
        rel_, sm_ = _unpack_small(jnp.pad(buf, ((0, small_red.shape[0] - buf.shape[0]), (0, 0))))[:2]
        res["rel_table"] = rel_
        res.update(sm_)

    return (loss, grad_x[None], *[grads[k] for k in WEIGHT_ORDER], *[deltas[k] for k in WEIGHT_ORDER],
            *[new_m[k] for k in WEIGHT_ORDER], *[new_v[k] for k in WEIGHT_ORDER])
```

```python
import math

import jax
import jax.numpy as jnp
from jax import lax
from jax.experimental import pallas as pl
from jax.experimental.pallas import tpu as pltpu

F32 = jnp.float32
BF16 = jnp.bfloat16
MESH = pl.DeviceIdType.MESH

D_MODEL = 1024
DEPTH = 2
HEAD_DIM = 64
N_HEADS = 8
A_DILS = (1, 4, 16)
C_Q_RANK = 256
C_KV_RANK = 128
C_NOPE = 64
C_ROPE = 32
ROPE_BASE = 10000.0
REL_BUCKETS = 32
REL_MAX_DIST = 2048
D_FF = 2816
ALPHA = (2 * DEPTH) ** 0.25
LN_EPS = 1e-5
RMS_EPS = 1e-6
NEG = -1e30
LOG2E, LN2 = math.log2(math.e), math.log(2.0)
ADAM_LR, ADAM_B1, ADAM_B2, ADAM_EPS, ADAM_WD, ADAM_STEP = 0.001, 0.9, 0.999, 1e-08, 0.01, 10

VMEM_LIMIT_BYTES = 56 * 1024 * 1024
LANE = 128
BLK = 128
TQ = 512
TQ_FWD = 1024
TM_TOKENS = 2048
HALO = 16
BAND_SCALE = HEAD_DIM ** -0.5
BAND_UNROLL = 16

D_IN = 8864
A_COLS = 3 * N_HEADS * HEAD_DIM
ORIG = {"a": 0, "bq": 4608, "bk": 5120, "bv": 5248, "cq": 5376, "cdkv": 5632, "gate": 5792}
M_GATE, M_A0, M_BQ, M_BK, M_BV, M_CQ, M_CDKV, M_COLS = 0, 3072, 4608, 5120, 5376, 5632, 5888, 6144

N_CHIP = 4
MATS = (
    ("w_in", (D_MODEL, D_IN), 1),
    ("w_uq", (C_Q_RANK, 768), 1),
    ("w_ukv", (C_KV_RANK, 1024), 1),
    ("w_branch", (3, 512, D_MODEL), 2),
    ("w_out", (D_MODEL, D_MODEL), 0),
    ("w_ffn_up", (D_MODEL, 2 * D_FF), 1),
    ("conv_w", (3, 2 * D_FF), 1),
    ("w_ffn_down", (D_FF, D_MODEL), 0),
)
SMALL = ("b_gate", "sinks", "q_norm_g", "kv_norm_g", "ln1_g", "ln1_b", "conv_b", "ln2_g", "ln2_b")
SMALL_SIZES = {"b_gate": 3072, "sinks": 8, "q_norm_g": 256, "kv_norm_g": 128, "ln1_g": 1024, "ln1_b": 1024,
               "conv_b": 5632, "ln2_g": 1024, "ln2_b": 1024}
WEIGHT_ORDER = ("rel_table", "w_in", "b_gate", "sinks", "q_norm_g", "kv_norm_g", "w_uq", "w_ukv", "w_branch",
                "w_out", "ln1_g", "ln1_b", "w_ffn_up", "conv_w", "conv_b", "w_ffn_down", "ln2_g", "ln2_b")


def _cparams(sem):
    return pltpu.CompilerParams(dimension_semantics=sem, vmem_limit_bytes=VMEM_LIMIT_BYTES)


def _ceil_to(n, m):
    return -(-n // m) * m


def _pick(n, target):
    if n <= target:
        return n
    best = None
    for t in range(LANE, target + 1, LANE):
        if n % t == 0:
            best = t
    assert best is not None, (n, target)
    return best


def _mm(a, b, *, ta=False, tb=False, out_dtype=F32, tm=1024, tn=1024, tk=2048, name):
    assert not (ta and tb)
    k, m = a.shape[::-1] if not ta else a.shape
    n = b.shape[0] if tb else b.shape[1]
    assert (b.shape[1] if tb else b.shape[0]) == k
    tm, tn, tk = _pick(m, tm), _pick(n, tn), _pick(k, tk)
    nk = k // tk
    dn = (((0 if ta else 1,), (1 if tb else 0,)), ((), ()))

    def body(a_ref, b_ref, o_ref, acc_ref):
        part = lax.dot_general(a_ref[...].astype(BF16), b_ref[...].astype(BF16), dn, preferred_element_type=F32)
        if nk == 1:
            o_ref[...] = part.astype(o_ref.dtype)
        else:
            kk = pl.program_id(2)

            @pl.when(kk == 0)
            def _():
                acc_ref[...] = part

            @pl.when(kk > 0)
            def _():
                acc_ref[...] += part

            @pl.when(kk == nk - 1)
            def _():
                o_ref[...] = acc_ref[...].astype(o_ref.dtype)

    a_spec = pl.BlockSpec((tk, tm), lambda i, j, kk: (kk, i)) if ta else pl.BlockSpec((tm, tk), lambda i, j, kk: (i, kk))
    b_spec = pl.BlockSpec((tn, tk), lambda i, j, kk: (j, kk)) if tb else pl.BlockSpec((tk, tn), lambda i, j, kk: (kk, j))
    return pl.pallas_call(
        body, name=name, grid=(m // tm, n // tn, nk),
        in_specs=[a_spec, b_spec],
        out_specs=pl.BlockSpec((tm, tn), lambda i, j, kk: (i, j)),
        out_shape=jax.ShapeDtypeStruct((m, n), out_dtype),
        scratch_shapes=[pltpu.VMEM((tm, tn) if nk > 1 else (8, LANE), F32)],
        compiler_params=_cparams(("parallel", "parallel", "arbitrary")),
    )(a, b)


def _rowwise(fn, rows, *, pars=(), halos=(), outs=(), accs=(), tm, name, ncol=1, t=None):
    nb = rows[0][0].shape[0]
    t = rows[0][0].shape[1] if t is None else t
    tm = min(tm, t)
    assert t % tm == 0 and tm % 8 == 0
    nt = t // tm
    in_specs, args = [], []
    for spec in rows:
        arr, c, off = spec[:3]
        rb = spec[3] if len(spec) > 3 else 0
        in_specs.append(pl.BlockSpec((1, tm, c), lambda b, cc, i, off=off, rb=rb: (b, i + rb, off + cc)))
        args.append(arr)
    for arr, c, off, kind in halos:
        if kind == "prev":
            im = lambda b, cc, i, off=off: (b, jnp.maximum(i * (tm // HALO) - 1, 0), off + cc)
        else:
            im = lambda b, cc, i, off=off: (b, jnp.minimum((i + 1) * (tm // HALO), t // HALO - 1), off + cc)
        in_specs.append(pl.BlockSpec((1, HALO, c), im))
        args.append(arr)
    for arr, c, off in pars:
        bp, r = arr.shape[:2]
        if bp > 1:
            im = lambda b, cc, i, off=off: (b, 0, off + cc)
        else:
            im = lambda b, cc, i, off=off: (0, 0, off + cc)
        in_specs.append(pl.BlockSpec((1, r, c), im))
        args.append(arr)
    out_specs, out_shapes = [], []
    for ctot, c, off, dt in outs:
        out_specs.append(pl.BlockSpec((1, tm, c), lambda b, cc, i, off=off: (b, i, off + cc)))
        out_shapes.append(jax.ShapeDtypeStruct((nb, t, ctot), dt))
    for r, ctot, c, off in accs:
        out_specs.append(pl.BlockSpec((1, r, c), lambda b, cc, i, off=off: (b, 0, off + cc)))
        out_shapes.append(jax.ShapeDtypeStruct((nb, r, ctot), F32))
    n_in, n_out = len(args), len(outs)

    def body(*refs):
        i = pl.program_id(2)
        res = fn(i, nt, *[r[0].astype(F32) for r in refs[:n_in]])
        if not isinstance(res, (tuple, list)):
            res = (res,)
        for o_ref, val in zip(refs[n_in:n_in + n_out], res[:n_out]):
            o_ref[0] = val.astype(o_ref.dtype)
        for a_ref, val in zip(refs[n_in + n_out:], res[n_out:]):
            @pl.when(i == 0)
            def _(a_ref=a_ref, val=val):
                a_ref[0] = val

            @pl.when(i > 0)
            def _(a_ref=a_ref, val=val):
                a_ref[0] += val

    res = pl.pallas_call(
        body, name=name, grid=(nb, ncol, nt), in_specs=in_specs, out_specs=out_specs, out_shape=out_shapes,
        compiler_params=_cparams(("parallel", "parallel", "arbitrary")),
    )(*args)
    return res


def _dot(a, b):
    return lax.dot_general(a, b, (((1,), (0,)), ((), ())), preferred_element_type=F32)


def _dot_nt(a, b):
    return lax.dot_general(a, b, (((1,), (1,)), ((), ())), preferred_element_type=F32)


def _dot_tn(a, b):
    return lax.dot_general(a, b, (((0,), (0,)), ((), ())), preferred_element_type=F32)


def _rows(parts):
    return jnp.concatenate(parts, axis=0)


def _lane_lo():
    return lax.broadcasted_iota(jnp.int32, (1, LANE), 1) < HEAD_DIM


def _blocks(a):
    return [a[i * BLK:(i + 1) * BLK] for i in range(a.shape[0] // BLK)]


def _band_geometry(t):
    rows = min(BAND_UNROLL, t // BLK) * BLK
    assert t % rows == 0
    return rows, t // rows


def _band_operands(g, k_ref, v_ref, rows):
    start = pl.multiple_of(g * rows, rows)
    pstart = pl.multiple_of(jnp.maximum(g * rows - BLK, 0), BLK)
    out = []
    for ref in (k_ref, v_ref):
        cur = _blocks(ref[pl.ds(start, rows), :])
        raw = ref[pl.ds(pstart, rows), :]
        shifted = _rows([jnp.zeros((BLK, LANE), raw.dtype), raw[:rows - BLK]])
        prev = _blocks(jnp.where(g == 0, shifted, raw))
        out.append([_rows([p, c]) for p, c in zip(prev, cur)])
    return out


def _band_scores(g, qa, kk, b_ref, a, nb):
    u = len(qa)
    assert nb % u == 0 or u % nb == 0
    firsts = []
    for i in range(u):
        if nb >= u:
            val = jnp.where(lax.rem(g * u, nb) == 0, NEG, 0.0).astype(F32) if i == 0 else 0.0
        else:
            val = NEG if i % nb == 0 else 0.0
        firsts.append(jnp.zeros((BLK, 1), F32) + val)
    prev_slots = lax.broadcasted_iota(jnp.int32, (1, 2 * BLK), 1) < BLK
    return (_rows([_dot_nt(q, k) for q, k in zip(qa, kk)]) + _rows([b_ref[a]] * u)
            + jnp.where(prev_slots, _rows(firsts), 0.0))


def _band_fwd(src, offs, bias, sinks, *, nb, gqa, name):
    t = src.shape[0]
    rows, nstep = _band_geometry(t)
    qo, ko, vo = offs
    share = 2 if gqa else 1

    def body(sink_ref, q_ref, k_ref, v_ref, b_ref, o_ref, lse_ref):
        hp, g = pl.program_id(0), pl.program_id(1)
        lo = _lane_lo()
        q2 = q_ref[...]
        kk, vv = _band_operands(g, k_ref, v_ref, rows)
        outs, lses = [], []
        for a in range(2):
            sink = sink_ref[2 * hp + a]
            qa = _blocks(jnp.where(lo if a == 0 else jnp.logical_not(lo), q2, jnp.zeros_like(q2)) * BAND_SCALE)
            s = _band_scores(g, qa, kk, b_ref, a, nb)
            m = jnp.maximum(jnp.max(s, axis=1, keepdims=True), sink)
            p = jnp.exp(s - m)
            l = jnp.sum(p, axis=1, keepdims=True) + jnp.exp(sink - m)
            p_b = _blocks((p * (1.0 / l)).astype(BF16))
            outs.append(_rows([_dot(pb, v) for pb, v in zip(p_b, vv)]))
            lses.append(m + jnp.log(l))
        o_ref[...] = jnp.where(lo, outs[0], outs[1])
        lse_ref[...] = jnp.where(lo, lses[0], lses[1])

    slab = lambda off: pl.BlockSpec((rows, LANE), lambda hp, g: (g, off + hp))
    whole = lambda off: pl.BlockSpec((t, LANE), lambda hp, g: (0, off + hp // share))
    return pl.pallas_call(
        body, name=name, grid=(N_HEADS // 2, nstep),
        in_specs=[pl.BlockSpec(memory_space=pltpu.SMEM), slab(qo), whole(ko), whole(vo),
                  pl.BlockSpec((2, BLK, 2 * BLK), lambda hp, g: (hp, 0, 0))],
        out_specs=[slab(0), slab(0)],
        out_shape=[jax.ShapeDtypeStruct((t, N_HEADS * HEAD_DIM), F32)] * 2,
        compiler_params=_cparams(("parallel", "parallel")),
    )(sinks, src, src, src, bias)


def _band_bwd(src, offs, do, lse, dpr, bias, sinks, *, nb, gqa, name):
    t = src.shape[0]
    rows, nstep = _band_geometry(t)
    qo, ko, vo = offs
    share = 2 if gqa else 1

    def fold(a):
        acc = a[0:BLK]
        for i in range(1, rows // BLK):
            acc = acc + a[i * BLK:(i + 1) * BLK]
        return acc

    def body(sink_ref, q_ref, k_ref, v_ref, do_ref, lse_ref, dpr_ref, b_ref,
             dq_ref, dk_ref, dv_ref, ds_ref, dsink_ref, dk_acc, dv_acc):
        hp, g = pl.program_id(0), pl.program_id(1)
        lo = _lane_lo()
        hi = jnp.logical_not(lo)

        @pl.when(jnp.logical_and(g == 0, lax.rem(hp, share) == 0))
        def _():
            dk_acc[...] = jnp.zeros_like(dk_acc)
            dv_acc[...] = jnp.zeros_like(dv_acc)

        @pl.when(g == 0)
        def _():
            ds_ref[...] = jnp.zeros_like(ds_ref)
            dsink_ref[...] = jnp.zeros_like(dsink_ref)

        q2, do2, lse2, dpr2 = q_ref[...], do_ref[...], lse_ref[...], dpr_ref[...]
        lse_sw, dpr_sw = pltpu.roll(lse2, HEAD_DIM, axis=1), pltpu.roll(dpr2, HEAD_DIM, axis=1)
        kk, vv = _band_operands(g, k_ref, v_ref, rows)
        dqs, per_head = [], []
        for a in range(2):
            sink = sink_ref[2 * hp + a]
            mine = lo if a == 0 else hi
            qa = _blocks(jnp.where(mine, q2, jnp.zeros_like(q2)) * BAND_SCALE)
            doa = _blocks(jnp.where(mine, do2, jnp.zeros_like(do2)))
            lse_a, dpr_a = jnp.where(mine, lse2, lse_sw), jnp.where(mine, dpr2, dpr_sw)
            wide = lambda x: jnp.concatenate([x, x], axis=1)
            p = jnp.exp(_band_scores(g, qa, kk, b_ref, a, nb) - wide(lse_a))
            ds = p * (_rows([_dot_nt(d, v) for d, v in zip(doa, vv)]) - wide(dpr_a))
            ds_ref[a] += fold(ds)
            dsink_ref[a] -= jnp.sum(jnp.exp(sink - lse_a) * dpr_a, axis=0, keepdims=True)
            ds_b, p_b = _blocks(ds.astype(BF16)), _blocks(p.astype(BF16))
            dqs.append(_rows([_dot(d, k) for d, k in zip(ds_b, kk)]))
            per_head.append((ds_b, p_b, qa, doa))
        dq_ref[...] = (jnp.where(lo, dqs[0], dqs[1]) * BAND_SCALE).astype(dq_ref.dtype)
        (ds0, p0, qa0, do0), (ds1, p1, qa1, do1) = per_head
        for i in range(len(qa0)):
            at = pl.ds(pl.multiple_of(g * rows + i * BLK, BLK), 2 * BLK)
            dk_acc[at, :] += _dot_tn(_rows([ds0[i], ds1[i]]), _rows([qa0[i], qa1[i]]))
            dv_acc[at, :] += _dot_tn(_rows([p0[i], p1[i]]), _rows([do0[i], do1[i]]))

        @pl.when(g == nstep - 1)
        def _():
            dk_ref[...] = dk_acc[BLK:, :].astype(dk_ref.dtype)
            dv_ref[...] = dv_acc[BLK:, :].astype(dv_ref.dtype)

    slab = lambda off: pl.BlockSpec((rows, LANE), lambda hp, g: (g, off + hp))
    whole = lambda off: pl.BlockSpec((t, LANE), lambda hp, g: (0, off + hp // share))
    per_pair = lambda shp: pl.BlockSpec((2,) + shp, lambda hp, g: (hp,) + (0,) * len(shp))
    kv_cols = N_HEADS * HEAD_DIM // share
    return pl.pallas_call(
        body, name=name, grid=(N_HEADS // 2, nstep),
        in_specs=[pl.BlockSpec(memory_space=pltpu.SMEM), slab(qo), whole(ko), whole(vo), slab(0), slab(0), slab(0),
                  per_pair((BLK, 2 * BLK))],
        out_specs=[slab(0), whole(0), whole(0), per_pair((BLK, 2 * BLK)), per_pair((1, LANE))],
        out_shape=[jax.ShapeDtypeStruct((t, N_HEADS * HEAD_DIM), BF16), jax.ShapeDtypeStruct((t, kv_cols), BF16),
                   jax.ShapeDtypeStruct((t, kv_cols), BF16), jax.ShapeDtypeStruct((N_HEADS, BLK, 2 * BLK), F32),
                   jax.ShapeDtypeStruct((N_HEADS, 1, LANE), F32)],
        scratch_shapes=[pltpu.VMEM((t + BLK, LANE), F32), pltpu.VMEM((t + BLK, LANE), F32)],
        compiler_params=_cparams(("arbitrary", "arbitrary")),
    )(sinks, src, src, src, do, lse, dpr, bias)


MLA_V_OFF = N_HEADS


def _diag_mask(size, keys_on_rows=False):
    rows, cols = lax.broadcasted_iota(jnp.int32, (size, size), 0), lax.broadcasted_iota(jnp.int32, (size, size), 1)
    return rows <= cols if keys_on_rows else cols <= rows


def _mla_specs(t, tq):
    blk = lambda f: pl.BlockSpec((tq, LANE), lambda hp, qi, f=f: (qi, f(hp)))
    whole = lambda f: pl.BlockSpec((t, LANE), lambda hp, qi, f=f: (0, f(hp)))
    return blk, whole


def _mla_fwd(q, k, vt, *, name):
    t = q.shape[0]
    n, tq = vt.shape[1], vt.shape[3]
    scale = (C_NOPE + C_ROPE) ** -0.5

    def body(q0_ref, q1_ref, k0_ref, k1_ref, vt_ref, o_ref, lse_ref, m_ref, acc_ref):
        qi = pl.program_id(1)
        qs, ks = (q0_ref[...], q1_ref[...]), (k0_ref, k1_ref)
        m_ref[...] = jnp.full_like(m_ref, NEG)
        acc_ref[...] = jnp.zeros_like(acc_ref)
        first = lax.broadcasted_iota(jnp.int32, (LANE, 1), 0) < HEAD_DIM

        def step(kj, diagonal):
            rows = pl.ds(pl.multiple_of(kj * tq, tq), tq)
            vtb = vt_ref[0, kj]
            one = jnp.ones_like(vtb)
            vts = (jnp.where(first, vtb, one), jnp.where(first, one, vtb))
            for a in range(2):
                s = _dot_nt(ks[a][rows, :], qs[a]) * (scale * LOG2E)
                if diagonal:
                    s = jnp.where(_diag_mask(tq, keys_on_rows=True), s, NEG)
                m_prev = m_ref[a]
                m_new = jnp.maximum(m_prev, jnp.max(s, axis=0, keepdims=True))
                acc_ref[a] = jnp.exp2(m_prev - m_new) * acc_ref[a] + _dot(vts[a], jnp.exp2(s - m_new).astype(BF16))
                m_ref[a] = m_new

        def kloop(kj, c2):
            step(kj, False)
            return c2

        lax.fori_loop(0, qi, kloop, 0)
        step(qi, True)
        l0, l1 = acc_ref[0, HEAD_DIM:HEAD_DIM + 1, :], acc_ref[1, 0:1, :]
        ot = jnp.where(first, acc_ref[0] * (1.0 / l0), acc_ref[1] * (1.0 / l1))
        lset = jnp.where(first, m_ref[0] * LN2 + jnp.log(l0), m_ref[1] * LN2 + jnp.log(l1))
        o_ref[...] = ot.T
        lse_ref[...] = lset.T

    blk, whole = _mla_specs(t, tq)
    return pl.pallas_call(
        body, name=name, grid=(N_HEADS // 2, n),
        in_specs=[blk(lambda hp: 2 * hp), blk(lambda hp: 2 * hp + 1), whole(lambda hp: 2 * hp), whole(lambda hp: 2 * hp + 1),
                  pl.BlockSpec((1, n, LANE, tq), lambda hp, qi: (hp, 0, 0, 0))],
        out_specs=[blk(lambda hp: hp), blk(lambda hp: hp)],
        out_shape=[jax.ShapeDtypeStruct((t, N_HEADS * HEAD_DIM), F32)] * 2,
        scratch_shapes=[pltpu.VMEM((2, 1, tq), F32), pltpu.VMEM((2, LANE, tq), F32)],
        compiler_params=_cparams(("parallel", "parallel")),
    )(q, q, k, k, vt)


def _mla_bwd(q, k, kv, do, lse, delta, *, name):
    t = q.shape[0]
    n = t // TQ
    scale = (C_NOPE + C_ROPE) ** -0.5

    def body(q0_ref, q1_ref, k0_ref, k1_ref, v_ref, do_ref, lse_ref, dl_ref,
             dq_ref, dk_ref, dv_ref, dq_acc, dk_acc, dv_acc):
        qi = pl.program_id(1)
        lo = _lane_lo()

        @pl.when(qi == 0)
        def _():
            dk_acc[...] = jnp.zeros_like(dk_acc)
            dv_acc[...] = jnp.zeros_like(dv_acc)

        dq_acc[...] = jnp.zeros_like(dq_acc)
        qs, ks = (q0_ref[...], q1_ref[...]), (k0_ref, k1_ref)
        do2, lse2, dl2 = do_ref[...], lse_ref[...], dl_ref[...]
        lse_sw, dl_sw = pltpu.roll(lse2, HEAD_DIM, axis=1), pltpu.roll(dl2, HEAD_DIM, axis=1)
        heads = []
        for a in range(2):
            mine = lo if a == 0 else jnp.logical_not(lo)
            heads.append((jnp.where(mine, do2, jnp.zeros_like(do2)), jnp.where(mine, lse2, lse_sw)[:, 0:1] * LOG2E,
                          jnp.where(mine, dl2, dl_sw)[:, 0:1]))

        def step(start, size, diagonal):
            rows = pl.ds(pl.multiple_of(start, TQ), size)
            vb = v_ref[rows, :]
            ps = []
            for a, (doa, lse_a, dl_a) in enumerate(heads):
                kb = ks[a][rows, :]
                s = _dot_nt(qs[a], kb) * (scale * LOG2E)
                if diagonal:
                    s = jnp.where(_diag_mask(TQ), s, NEG)
                p = jnp.exp2(s - lse_a)
                ds = (p * (_dot_nt(doa, vb) - dl_a)).astype(BF16)
                dq_acc[a] += _dot(ds, kb)
                dk_acc[a, rows, :] += _dot_tn(ds, qs[a])
                ps.append(p.astype(BF16))
            dv_acc[rows, :] += _dot_tn(jnp.concatenate(ps, axis=0), jnp.concatenate([h[0] for h in heads], axis=0))

        def kloop(kj, c2):
            step(kj * (2 * TQ), 2 * TQ, False)
            return c2

        lax.fori_loop(0, qi // 2, kloop, 0)

        @pl.when(lax.rem(qi, 2) == 1)
        def _():
            step((qi - 1) * TQ, TQ, False)

        step(qi * TQ, TQ, True)
        dq_ref[:, 0:LANE] = (dq_acc[0] * scale).astype(dq_ref.dtype)
        dq_ref[:, LANE:2 * LANE] = (dq_acc[1] * scale).astype(dq_ref.dtype)

        @pl.when(qi == n - 1)
        def _():
            dk_ref[:, 0:LANE] = (dk_acc[0] * scale).astype(dk_ref.dtype)
            dk_ref[:, LANE:2 * LANE] = (dk_acc[1] * scale).astype(dk_ref.dtype)
            dv_ref[...] = dv_acc[...].astype(dv_ref.dtype)

    blk, whole = _mla_specs(t, TQ)
    even, odd, pair = (lambda hp: 2 * hp), (lambda hp: 2 * hp + 1), (lambda hp: hp)
    wide = jax.ShapeDtypeStruct((t, N_HEADS * LANE), BF16)
    return pl.pallas_call(
        body, name=name, grid=(N_HEADS // 2, n),
        in_specs=[blk(even), blk(odd), whole(even), whole(odd), whole(lambda hp: MLA_V_OFF + hp), blk(pair), blk(pair), blk(pair)],
        out_specs=[pl.BlockSpec((TQ, 2 * LANE), lambda hp, qi: (qi, hp)), pl.BlockSpec((t, 2 * LANE), lambda hp, qi: (0, hp)), whole(pair)],
        out_shape=[wide, wide, jax.ShapeDtypeStruct((t, N_HEADS * HEAD_DIM), BF16)],
        scratch_shapes=[pltpu.VMEM((2, TQ, LANE), F32), pltpu.VMEM((2, t, LANE), F32), pltpu.VMEM((t, LANE), F32)],
        compiler_params=_cparams(("arbitrary", "arbitrary")),
    )(q, q, k, k, kv, do, lse, delta)


def _bias_lookup(bucket, table_t, *, name):
    nh, npos = bucket.shape
    tp = 4096

    def body(b_ref, t_ref, o_ref):
        bk, tab = b_ref[...], t_ref[...]
        acc = jnp.zeros(bk.shape, F32)
        for i in range(REL_BUCKETS):
            acc = jnp.where(bk == i, tab[:, i:i + 1], acc)
        o_ref[...] = acc

    return pl.pallas_call(
        body, name=name, grid=(npos // tp,),
        in_specs=[pl.BlockSpec((nh, tp), lambda i: (0, i)), pl.BlockSpec((nh, REL_BUCKETS), lambda i: (0, 0))],
        out_specs=pl.BlockSpec((nh, tp), lambda i: (0, i)),
        out_shape=jax.ShapeDtypeStruct((nh, npos), F32),
        compiler_params=_cparams(("parallel",)),
    )(bucket, table_t)


def _bias_grad(bucket, ds0, ds1, *, name):
    nh, npos = bucket.shape
    tp = 4096

    def body(b_ref, a_ref, c_ref, o_ref):
        i = pl.program_id(0)
        bk, ds = b_ref[...], a_ref[...] + c_ref[...]
        lane = lax.broadcasted_iota(jnp.int32, (nh, REL_BUCKETS), 1)
        acc = jnp.zeros((nh, REL_BUCKETS), F32)
        for j in range(REL_BUCKETS):
            col = jnp.sum(jnp.where(bk == j, ds, 0.0), axis=1, keepdims=True)
            acc = acc + jnp.where(lane == j, col, 0.0)

        @pl.when(i == 0)
        def _():
            o_ref[...] = acc

        @pl.when(i > 0)
        def _():
            o_ref[...] += acc

    return pl.pallas_call(
        body, name=name, grid=(npos // tp,),
        in_specs=[pl.BlockSpec((nh, tp), lambda i: (0, i))] * 3,
        out_specs=pl.BlockSpec((nh, REL_BUCKETS), lambda i: (0, 0)),
        out_shape=jax.ShapeDtypeStruct((nh, REL_BUCKETS), F32),
        compiler_params=_cparams(("arbitrary",)),
    )(bucket, ds0, ds1)


def _t5_bucket(dist):
    n = jnp.maximum(dist, 0)
    max_exact = REL_BUCKETS // 2
    scaled = jnp.log(jnp.maximum(n, 1).astype(F32) / max_exact) / math.log(REL_MAX_DIST / max_exact)
    large = max_exact + (scaled * (REL_BUCKETS - max_exact)).astype(jnp.int32)
    return jnp.where(n < max_exact, n, jnp.minimum(large, REL_BUCKETS - 1))


def _bucket_index():
    qi = jnp.arange(BLK)[:, None]
    ci = jnp.arange(2 * BLK)[None, :]
    step = BLK + qi - ci
    per_group = [_t5_bucket(step * d).reshape(1, -1) for d in A_DILS + (1,)]
    return jnp.concatenate([jnp.tile(b, (N_HEADS, 1)) for b in per_group], axis=0).astype(jnp.int32)


def _sigmoid(x):
    return 1.0 / (1.0 + jnp.exp(-x))


def _ln_stats(z):
    mu = jnp.mean(z, axis=-1, keepdims=True)
    zc = z - mu
    var = jnp.mean(zc * zc, axis=-1, keepdims=True)
    return zc * lax.rsqrt(var + LN_EPS)


def _ln_fwd(x, mix, g, b, *, name):
    def fn(i, nt, xv, mv, gv, bv):
        z = ALPHA * xv + mv
        y = _ln_stats(z) * gv + bv
        return y, y, z

    c = x.shape[-1]
    y, yb, z = _rowwise(fn, [(x[None], c, 0), (mix[None], c, 0)], pars=[(g.reshape(1, 1, c), c, 0), (b.reshape(1, 1, c), c, 0)],
                        outs=[(c, c, 0, F32), (c, c, 0, BF16), (c, c, 0, F32)], tm=512, name=name)
    return y[0], yb[0], z[0]


def _ln_bwd(z, g, dys, coefs, *, name):
    n = len(dys)

    def fn(i, nt, zv, *rest):
        gv = rest[n]
        dy = coefs[0] * rest[0]
        for cf, t in zip(coefs[1:], rest[1:n]):
            dy = dy + cf * t
        mu = jnp.mean(zv, axis=-1, keepdims=True)
        zc = zv - mu
        r = lax.rsqrt(jnp.mean(zc * zc, axis=-1, keepdims=True) + LN_EPS)
        xh = zc * r
        dxh = dy * gv
        dz = r * (dxh - jnp.mean(dxh, axis=-1, keepdims=True) - xh * jnp.mean(dxh * xh, axis=-1, keepdims=True))
        return dz, dz, jnp.sum(dy * xh, axis=0, keepdims=True), jnp.sum(dy, axis=0, keepdims=True)

    c = z.shape[-1]
    dz, dzb, dg, db = _rowwise(fn, [(z[None], c, 0)] + [(d[None], c, 0) for d in dys], pars=[(g.reshape(1, 1, c), c, 0)],
                               outs=[(c, c, 0, F32), (c, c, 0, BF16)], accs=[(1, c, c, 0), (1, c, c, 0)], tm=512, name=name)
    return dz[0], dzb[0], dg.reshape(c), db.reshape(c)


def _rms_fwd(src, c, off, g, *, name):
    def fn(i, nt, xv, gv):
        return xv * lax.rsqrt(jnp.mean(xv * xv, axis=-1, keepdims=True) + RMS_EPS) * gv

    return _rowwise(fn, [(src[None], c, off)], pars=[(g.reshape(1, 1, c), c, 0)], outs=[(c, c, 0, BF16)], tm=1024, name=name)[0][0]


def _rms_bwd(src, c, off, g, dy, *, name):
    def fn(i, nt, xv, dyv, gv):
        r = lax.rsqrt(jnp.mean(xv * xv, axis=-1, keepdims=True) + RMS_EPS)
        gd = gv * dyv
        dx = gd * r - xv * (r * r * r) * jnp.mean(gd * xv, axis=-1, keepdims=True)
        return dx, jnp.sum(dyv * xv * r, axis=0, keepdims=True)

    dx, dg = _rowwise(fn, [(src[None], c, off), (dy[None], c, 0)], pars=[(g.reshape(1, 1, c), c, 0)],
                      outs=[(c, c, 0, BF16)], accs=[(1, c, c, 0)], tm=1024, name=name)
    return dx[0], dg.reshape(c)


def _rope_slabs(x, n_slab, c, s, *, add=None, to_front=False, name):
    half = C_ROPE // 2

    def fn(i, nt, xv, cv, sv, *rest):
        lane = lax.broadcasted_iota(jnp.int32, (1, LANE), 1)
        extra = pltpu.roll(rest[0], C_NOPE, axis=1) if rest else None
        outs = []
        for h in range(n_slab):
            xs = xv[:, h * LANE:(h + 1) * LANE]
            if extra is not None:
                xs = xs + extra
            swapped = jnp.where(lane < C_NOPE + half, pltpu.roll(xs, LANE - half, axis=1), pltpu.roll(xs, half, axis=1))
            y = xs * cv + swapped * sv
            if to_front:
                y = jnp.where(lane < C_ROPE, pltpu.roll(y, LANE - C_NOPE, axis=1), 0.0)
            outs.append(y)
        return jnp.concatenate(outs, axis=1) if n_slab > 1 else outs[0]

    w = n_slab * LANE
    rows = [(x[None], w, 0), (c[None], LANE, 0), (s[None], LANE, 0)]
    if add is not None:
        rows.append((add[0][None], LANE, add[1]))
    return _rowwise(fn, rows, outs=[(w, w, 0, BF16)], tm=512, name=name)[0][0]


def _merge_fwd(proj, b_gate, ys, *, name):
    def fn(i, nt, g0, g1, g2, ya, yb, yc, bg):
        return (_sigmoid(g0 + bg[:, 0:1024]) * ya + _sigmoid(g1 + bg[:, 1024:2048]) * yb
                + _sigmoid(g2 + bg[:, 2048:3072]) * yc)

    rows = [(proj[None], 1024, j) for j in range(3)] + [(y[None], 1024, 0) for y in ys]
    return _rowwise(fn, rows, pars=[(b_gate.reshape(1, 1, 3072), 3072, 0)], outs=[(1024, 1024, 0, BF16)], tm=512, name=name)[0][0]


def _merge_bwd(proj, b_gate, ys, dm, *, name):
    def fn(i, nt, g0, g1, g2, ya, yb, yc, dmv, bg):
        outs, dgs = [], []
        for j, (gp, y) in enumerate(((g0, ya), (g1, yb), (g2, yc))):
            s = _sigmoid(gp + bg[:, j * 1024:(j + 1) * 1024])
            outs.append(s * dmv)
            dgs.append(dmv * y * s * (1.0 - s))
        return outs + dgs + [jnp.sum(d, axis=0, keepdims=True) for d in dgs]

    rows = [(proj[None], 1024, j) for j in range(3)] + [(y[None], 1024, 0) for y in ys] + [(dm[None], 1024, 0)]
    res = _rowwise(fn, rows, pars=[(b_gate.reshape(1, 1, 3072), 3072, 0)], outs=[(1024, 1024, 0, BF16)] * 6,
                   accs=[(1, 1024, 1024, 0)] * 3, tm=256, name=name)
    dys = [r[0] for r in res[0:3]]
    dgp = [r[0] for r in res[3:6]]
    dbg = jnp.concatenate([r.reshape(1024) for r in res[6:9]])
    return dys, dgp, dbg


def _shift_down(u, halo, i, k):
    ext = jnp.concatenate([jnp.where(i > 0, halo, 0.0), u], axis=0)
    return pltpu.roll(ext, k, axis=0)[HALO:]


GLU_C = D_FF // 2


def _conv(u, halo, i, w, b):
    return w[0:1] * _shift_down(u, halo, i, 2) + w[1:2] * _shift_down(u, halo, i, 1) + w[2:3] * u + b


def _glu_fwd(ug, uv, conv_w, conv_b, *, name):
    def fn(i, nt, g, v, hg, hv, wg, wv, bg, bv):
        cg, cv = _conv(g, hg, i, wg, bg), _conv(v, hv, i, wv, bv)
        return cg * _sigmoid(cg) * cv

    w3, b3 = conv_w[None], conv_b.reshape(1, 1, -1)
    c = GLU_C
    return _rowwise(fn, [(ug[None], c, 0), (uv[None], c, 0)], halos=[(ug[None], c, 0, "prev"), (uv[None], c, 0, "prev")],
                    pars=[(w3, c, 0), (w3, c, 2), (b3, c, 0), (b3, c, 2)], outs=[(D_FF, c, 0, BF16)], tm=256, ncol=2, name=name)[0][0]


def _glu_bwd(ug, uv, conv_w, conv_b, dh, *, name):
    def fn(i, nt, g, v, dhv, pg, pv, ng, nv, ndh, wg, wv, bg, bv):
        tm = g.shape[0]
        n = tm + HALO

        def taps(u, prev, nxt):
            ext = jnp.concatenate([jnp.where(i > 0, prev, 0.0), u, nxt], axis=0)
            return pltpu.roll(ext, 2, axis=0)[HALO:], pltpu.roll(ext, 1, axis=0)[HALO:], ext[HALO:]

        g2, g1, g0 = taps(g, pg, ng)
        v2, v1, v0 = taps(v, pv, nv)
        dhe = jnp.concatenate([dhv, jnp.where(i < nt - 1, ndh, 0.0)], axis=0)
        cg = wg[0:1] * g2 + wg[1:2] * g1 + wg[2:3] * g0 + bg
        cv = wv[0:1] * v2 + wv[1:2] * v1 + wv[2:3] * v0 + bv
        s = _sigmoid(cg)
        dcv = dhe * cg * s
        dcg = dhe * cv * (s * (1.0 + cg * (1.0 - s)))
        up = lambda d, w: (w[2:3] * d + w[1:2] * pltpu.roll(d, n - 1, axis=0) + w[0:1] * pltpu.roll(d, n - 2, axis=0))[:tm]
        red = lambda a, b=None: jnp.sum(a[:tm] if b is None else a[:tm] * b[:tm], axis=0, keepdims=True)
        return (up(dcg, wg), up(dcv, wv), red(dcg), red(dcv), red(dcg, g2), red(dcg, g1), red(dcg, g0),
                red(dcv, v2), red(dcv, v1), red(dcv, v0))

    w3, b3 = conv_w[None], conv_b.reshape(1, 1, -1)
    c = GLU_C
    res = _rowwise(fn, [(ug[None], c, 0), (uv[None], c, 0), (dh[None], c, 0)],
                   halos=[(ug[None], c, 0, "prev"), (uv[None], c, 0, "prev"), (ug[None], c, 0, "next"), (uv[None], c, 0, "next"),
                          (dh[None], c, 0, "next")],
                   pars=[(w3, c, 0), (w3, c, 2), (b3, c, 0), (b3, c, 2)],
                   outs=[(D_FF, c, 0, BF16), (D_FF, c, 0, BF16)], accs=[(1, D_FF, c, 0)] * 8, tm=256, ncol=2, name=name)
    dug, duv = res[0][0], res[1][0]
    dconv_b = jnp.concatenate([res[2].reshape(D_FF), res[3].reshape(D_FF)])
    dconv_w = jnp.concatenate([jnp.concatenate([res[4 + j].reshape(1, D_FF) for j in range(3)], axis=0),
                               jnp.concatenate([res[7 + j].reshape(1, D_FF) for j in range(3)], axis=0)], axis=1)
    return dug, duv, dconv_w, dconv_b


def _loss_and_grad(y, tgt, *, name):
    def fn(i, nt, yv, tv):
        err = yv - tv
        part = jnp.sum(jnp.sum(err * err, axis=0, keepdims=True), axis=1, keepdims=True) * (0.5 / D_MODEL)
        return err * (1.0 / D_MODEL), jnp.zeros((1, LANE), F32) + part

    dy, part = _rowwise(fn, [(y[None], D_MODEL, 0), (tgt[None], D_MODEL, 0)], outs=[(D_MODEL, D_MODEL, 0, F32)],
                        accs=[(1, LANE, LANE, 0)], tm=512, name=name)
    return dy[0], part.reshape(LANE)


def _lincomb(terms, coefs, *, name):
    def fn(i, nt, *vs):
        acc = coefs[0] * vs[0]
        for cf, v in zip(coefs[1:], vs[1:]):
            acc = acc + cf * v
        return acc

    c = terms[0].shape[-1]
    return _rowwise(fn, [(a[None], c, 0) for a in terms], outs=[(c, c, 0, F32)], tm=512, name=name)[0][0]


def _sum_rows(terms, *, tm, name, dtype=F32):
    def fn(i, nt, *vs):
        acc = vs[0]
        for v in vs[1:]:
            acc = acc + v
        return acc

    c = terms[0].shape[-1]
    return _rowwise(fn, [(t, c, 0) for t in terms], outs=[(c, c, 0, dtype)], tm=tm, name=name)[0]


def _head_sums(x):
    lo = _lane_lo()
    parts = []
    for j in range(x.shape[1] // LANE):
        blk = x[:, j * LANE:(j + 1) * LANE]
        s_lo = jnp.sum(jnp.where(lo, blk, 0.0), axis=1, keepdims=True)
        s_hi = jnp.sum(jnp.where(lo, 0.0, blk), axis=1, keepdims=True)
        parts.append(jnp.where(lo, s_lo, s_hi))
    return jnp.concatenate(parts, axis=1)


def _group_weights(l0, l1, l2):
    m = jnp.maximum(jnp.maximum(l0, l1), l2)
    es = [jnp.exp(l - m) for l in (l0, l1, l2)]
    inv = 1.0 / (es[0] + es[1] + es[2])
    return [e * inv for e in es]


def _combine_fwd(os_, lses, *, name):
    def fn(i, nt, o0, o1, o2, l0, l1, l2):
        w = _group_weights(l0, l1, l2)
        return w[0] * o0 + w[1] * o1 + w[2] * o2

    c = os_[0].shape[-1]
    return _rowwise(fn, [(a[None], c, 0) for a in list(os_) + list(lses)], outs=[(c, c, 0, BF16)], tm=512, name=name)[0][0]


def _combine_bwd(os_, lses, do_a, *, name):
    def fn(i, nt, o0, o1, o2, l0, l1, l2, da):
        ws = _group_weights(l0, l1, l2)
        dws = [_head_sums(da * o) for o in (o0, o1, o2)]
        mean = ws[0] * dws[0] + ws[1] * dws[1] + ws[2] * dws[2]
        return [w * da for w in ws] + [w * mean for w in ws]

    c = do_a.shape[-1]
    res = _rowwise(fn, [(a[None], c, 0) for a in list(os_) + list(lses) + [do_a]], outs=[(c, c, 0, BF16)] * 3 + [(c, c, 0, F32)] * 3,
                   tm=256, name=name)
    return [r[0] for r in res[0:3]], [r[0] for r in res[3:6]]


def _delta(do, o, *, name):
    def fn(i, nt, d, ov):
        return d, _head_sums(d * ov)

    c = do.shape[-1]
    res = _rowwise(fn, [(do[None], c, 0), (o[None], c, 0)], outs=[(c, c, 0, BF16), (c, c, 0, F32)], tm=512, name=name)
    return res[0][0], res[1][0]


def _adamw(w, g, m, v, *, name):
    c1 = 1.0 - ADAM_B1 ** ADAM_STEP
    c2 = 1.0 - ADAM_B2 ** ADAM_STEP

    def fn(i, nt, wv, gv, mv, vv):
        mn = ADAM_B1 * mv + (1.0 - ADAM_B1) * gv
        vn = ADAM_B2 * vv + (1.0 - ADAM_B2) * (gv * gv)
        delta = -ADAM_LR * ((mn / c1) / (jnp.sqrt(vn / c2) + ADAM_EPS) + ADAM_WD * wv)
        return delta, mn, vn

    r, c = w.shape
    rp = _ceil_to(r, 8)
    pad = lambda a: jnp.pad(a, ((0, rp - r), (0, 0))) if rp != r else a
    tm = rp
    for cand in (128, 64, 32, 16, 8):
        if rp % cand == 0:
            tm = cand
            break
    res = _rowwise(fn, [(pad(a)[None], c, 0) for a in (w, g, m, v)], outs=[(c, c, 0, F32)] * 3, tm=tm, name=name)
    return [x[0][:r] for x in res]


ANY = pl.BlockSpec(memory_space=pl.ANY)


def _place():
    x, y, c = lax.axis_index("x"), lax.axis_index("y"), lax.axis_index("c")
    chips = [(1 - x, y), (x, 1 - y), (1 - x, 1 - y)]
    return x, y, c, chips


def _allgather_weights(arrs):
    n = len(arrs)

    def body(*refs):
        ins, outs, send_sems, recv_sems = refs[:n], refs[n:2 * n], refs[2 * n], refs[2 * n + 1]
        x, y, c, chips = _place()
        j = 2 * x + y
        me, sibling = (x, y, c), (x, y, 1 - c)

        def cp(i, k, src, chip_idx, half, to):
            return pltpu.make_async_remote_copy(src_ref=src, dst_ref=outs[i].at[chip_idx, half], send_sem=send_sems.at[k],
                                                recv_sem=recv_sems.at[k], device_id=to, device_id_type=MESH)

        first, passed, own = [], [], []
        for i in range(n):
            for r, (cx, cy) in enumerate(chips):
                first.append(cp(i, 3 * i + r, ins[i].at[c], j, c, (cx, cy, c)))
                passed.append(cp(i, 3 * (n + i) + r, outs[i].at[2 * cx + cy, c], 2 * cx + cy, c, sibling))
            own += [cp(i, 6 * n + 2 * i + half, ins[i].at[half], j, half, sibling) for half in range(2)]
        for d in first + own:
            d.start()
        for i in range(n):
            for r, (cx, cy) in enumerate(chips):
                cp(i, 3 * i + r, ins[i].at[c], 2 * cx + cy, c, me).wait_recv()
                passed[3 * i + r].start()
        for i in range(n):
            for r, (cx, cy) in enumerate(chips):
                cp(i, 3 * (n + i) + r, ins[i].at[c], 2 * cx + cy, 1 - c, me).wait_recv()
        for d in own:
            d.wait_recv()
        for d in first + passed + own:
            d.wait_send()

    return pl.pallas_call(
        body, name="allgather_weights", in_specs=[ANY] * n, out_specs=[ANY] * n,
        out_shape=[jax.ShapeDtypeStruct((N_CHIP,) + a.shape, a.dtype) for a in arrs],
        scratch_shapes=[pltpu.SemaphoreType.DMA((8 * n,)), pltpu.SemaphoreType.DMA((8 * n,))],
    )(*arrs)


def _sibling_swap(gs):
    n = len(gs)

    def body(*refs):
        layers, outs, send_sems, recv_sems = (refs[:n], refs[n:2 * n]), refs[2 * n:3 * n], refs[3 * n], refs[3 * n + 1]
        x, y, c, _ = _place()

        def copies(srcs):
            return [pltpu.make_async_remote_copy(src_ref=srcs[i], dst_ref=outs[i], send_sem=send_sems.at[i], recv_sem=recv_sems.at[i],
                                                 device_id=(x, y, 1 - c), device_id_type=MESH) for i in range(n)]

        for layer in range(DEPTH):
            @pl.when(c == 1 - layer)
            def _(layer=layer):
                for d in copies(layers[layer]):
                    d.start()

        waits = copies(layers[0])
        for d in waits:
            d.wait_recv()
        for d in waits:
            d.wait_send()

    return pl.pallas_call(
        body, name="grad_sibling_swap", in_specs=[ANY] * (2 * n), out_specs=[ANY] * n,
        out_shape=[jax.ShapeDtypeStruct(g0.shape, g0.dtype) for g0, _ in gs],
        scratch_shapes=[pltpu.SemaphoreType.DMA((n,)), pltpu.SemaphoreType.DMA((n,))],
    )(*[g0 for g0, _ in gs], *[g1 for _, g1 in gs])


def _chip_scatter(ps):
    n = len(ps)

    def body(*refs):
        ins, outs, send_sems, recv_sems = refs[:n], refs[n:2 * n], refs[2 * n], refs[2 * n + 1]
        x, y, c, chips = _place()
        sends = []
        for i in range(n):
            for r, (cx, cy) in enumerate(chips):
                sends.append(pltpu.make_async_remote_copy(src_ref=ins[i].at[2 * cx + cy], dst_ref=outs[i].at[r], send_sem=send_sems.at[3 * i + r],
                                                          recv_sem=recv_sems.at[3 * i + r], device_id=(cx, cy, c), device_id_type=MESH))
        for d in sends:
            d.start()
        for d in sends:
            d.wait_recv()
        for d in sends:
            d.wait_send()

    return pl.pallas_call(
        body, name="grad_chip_scatter", in_specs=[ANY] * n, out_specs=[ANY] * n,
        out_shape=[jax.ShapeDtypeStruct((3,) + p.shape[1:], p.dtype) for p in ps],
        scratch_shapes=[pltpu.SemaphoreType.DMA((3 * n,)), pltpu.SemaphoreType.DMA((3 * n,))],
    )(*ps)


def _sibling_share(rs):
    n = len(rs)

    def body(*refs):
        ins, outs, send_sems, recv_sems = refs[:n], refs[n:2 * n], refs[2 * n], refs[2 * n + 1]
        x, y, c, _ = _place()
        cps = [pltpu.make_async_remote_copy(src_ref=ins[i], dst_ref=outs[i], send_sem=send_sems.at[i], recv_sem=recv_sems.at[i],
                                            device_id=(x, y, 1 - c), device_id_type=MESH) for i in range(n)]
        for d in cps:
            d.start()
        for d in cps:
            d.wait_recv()
        for d in cps:
            d.wait_send()

    return pl.pallas_call(
        body, name="grad_sibling_share", in_specs=[ANY] * n, out_specs=[ANY] * n,
        out_shape=[jax.ShapeDtypeStruct(r.shape, r.dtype) for r in rs],
        scratch_shapes=[pltpu.SemaphoreType.DMA((n,)), pltpu.SemaphoreType.DMA((n,))],
    )(*rs)


def _allreduce_small(s):
    rows, w = s.shape
    n_dev = 8

    def body(s_ref, out_ref, slots, send_sems, recv_sems):
        x, y, c, _ = _place()
        me = 4 * x + 2 * y + c
        slots[me] = s_ref[...]
        peers = []
        for r in range(1, n_dev):
            px = 1 - x if r & 4 else x
            py = 1 - y if r & 2 else y
            pc = 1 - c if r & 1 else c
            peers.append((px, py, pc))
        sends = [pltpu.make_async_remote_copy(src_ref=s_ref, dst_ref=slots.at[me], send_sem=send_sems.at[r], recv_sem=recv_sems.at[r],
                                              device_id=peer, device_id_type=MESH) for r, peer in enumerate(peers)]
        for d in sends:
            d.start()
        for r, (px, py, pc) in enumerate(peers):
            pltpu.make_async_remote_copy(src_ref=s_ref, dst_ref=slots.at[4 * px + 2 * py + pc], send_sem=send_sems.at[r],
                                         recv_sem=recv_sems.at[r], device_id=(x, y, c), device_id_type=MESH).wait_recv()
        for d in sends:
            d.wait_send()
        acc = slots[0]
        for k in range(1, n_dev):
            acc = acc + slots[k]
        out_ref[...] = acc

    vm = pl.BlockSpec(memory_space=pltpu.VMEM)
    return pl.pallas_call(
        body, name="allreduce_small", in_specs=[vm], out_specs=vm, out_shape=jax.ShapeDtypeStruct((rows, w), F32),
        scratch_shapes=[pltpu.VMEM((n_dev, rows, w), F32), pltpu.SemaphoreType.DMA((n_dev - 1,)), pltpu.SemaphoreType.DMA((n_dev - 1,))],
    )(s)


W_IN_SHARD = D_IN // N_CHIP
W_IN_ROWS_G = 2304
REDUCED = tuple(m for m in MATS if m[0] != "conv_w")
CONV_W_SIZE = 3 * 2 * D_FF


def _weight_send(name, a):
    if name == "w_in":
        return jnp.swapaxes(a, 1, 2).astype(BF16)
    return a if name == "conv_w" else a.astype(BF16)


def _full_weights(gathered, l):
    g = {k: v[:, l] for k, v in gathered.items()}
    s = g["w_in"].astype(F32).reshape(D_IN, D_MODEL)
    dup = lambda a: jnp.concatenate([a[0:64], a[0:64], a[64:128], a[64:128]], axis=0)
    o = ORIG
    wm_t = jnp.concatenate([s[o["gate"]:], s[o["a"]:o["a"] + A_COLS], s[o["bq"]:o["bk"]], dup(s[o["bk"]:o["bv"]]), dup(s[o["bv"]:o["cq"]]),
                            s[o["cq"]:o["gate"]], jnp.zeros((M_COLS - M_CDKV - (o["gate"] - o["cdkv"]), D_MODEL), F32)], axis=0).astype(BF16)
    wg_t = [s[o["a"] + gi * A_COLS:o["a"] + (gi + 1) * A_COLS].astype(BF16) for gi in (1, 2)]
    full = {name: jnp.moveaxis(g[name], 0, ax).reshape(shape) for name, shape, ax in MATS if name != "w_in"}
    uq = full["w_uq"].reshape(C_Q_RANK, N_HEADS, C_NOPE + C_ROPE)
    ukv = full["w_ukv"].reshape(C_KV_RANK, N_HEADS, 2 * C_NOPE)
    w_uq_p = _pad_lanes(uq).reshape(C_Q_RANK, N_HEADS * LANE)
    w_ukv_p = jnp.concatenate([_pad_lanes(ukv[:, :, :C_NOPE]).reshape(C_KV_RANK, N_HEADS * LANE),
                               ukv[:, :, C_NOPE:].reshape(C_KV_RANK, N_HEADS * HEAD_DIM)], axis=1)
    return {"wm_t": wm_t, "wg_t": wg_t, "w_uq_p": w_uq_p, "w_ukv_p": w_ukv_p, "w_branch": full["w_branch"], "w_out": full["w_out"],
            "wup_g": full["w_ffn_up"][:, :D_FF], "wup_v": full["w_ffn_up"][:, D_FF:], "conv_w": full["conv_w"],
            "w_ffn_down": full["w_ffn_down"]}


def _grad_send(name, g, shape, ax):
    if name == "w_in":
        return jnp.pad(g.reshape(N_CHIP, W_IN_SHARD, D_MODEL), ((0, 0), (0, W_IN_ROWS_G - W_IN_SHARD), (0, 0)))
    split = shape[:ax] + (N_CHIP, shape[ax] // N_CHIP) + shape[ax + 1:]
    return jnp.moveaxis(g.reshape(split), ax, 0)


def _grad_recv(name, r):
    return r[:W_IN_SHARD].T if name == "w_in" else r


def _pack_small(rel, small, conv_w, extra):
    parts = [rel.reshape(-1)]
    for l in range(DEPTH):
        for name in SMALL:
            parts.append(small[name][l].reshape(-1))
    parts += [conv_w.reshape(-1), extra]
    flat = jnp.concatenate(parts)
    rows = _ceil_to(-(-flat.shape[0] // LANE), 8)
    return jnp.pad(flat, (0, rows * LANE - flat.shape[0])).reshape(rows, LANE)


def _unpack_small(buf):
    flat = buf.reshape(-1)
    rel = flat[:REL_BUCKETS * 32].reshape(REL_BUCKETS, 32)
    off = REL_BUCKETS * 32
    small = {name: [] for name in SMALL}
    for l in range(DEPTH):
        for name in SMALL:
            n = SMALL_SIZES[name]
            small[name].append(flat[off:off + n])
            off += n
    conv_w = flat[off:off + DEPTH * CONV_W_SIZE].reshape(DEPTH, 3, 2 * D_FF)
    off += DEPTH * CONV_W_SIZE
    return rel, {k: jnp.stack(v) for k, v in small.items()}, conv_w, flat[off:off + LANE]


def _rows2d(a, lead):
    return a.reshape(a.shape[:lead] + (-1, a.shape[-1]))


def _row_tile(rows):
    for cand in (512, 256, 128, 64, 32, 16, 8):
        if rows % cand == 0:
            return cand
    raise ValueError(rows)


def _pair_add(g, got, core, *, name):
    g0, g1, got2 = _rows2d(g[0], 0), _rows2d(g[1], 0), _rows2d(got, 0)
    rows, c = got2.shape
    flag = jnp.zeros((1, 1, LANE), F32) + core.astype(F32)

    def fn(i, nt, a0, a1, b, f):
        return jnp.where(f[:, 0:1] == 0.0, a0, a1) + b

    out = _rowwise(fn, [(g0[None], c, 0), (g1[None], c, 0), (got2[None], c, 0)], pars=[(flag, LANE, 0)],
                   outs=[(c, c, 0, BF16)], tm=_row_tile(rows), name=name)[0][0]
    return out.reshape(got.shape)


def _chip_add(own, got, *, name):
    own2, got2 = _rows2d(own, 0), _rows2d(got, 1)
    rows, c = own2.shape
    tm = _row_tile(rows)

    def fn(i, nt, a, b0, b1, b2):
        return ((a.astype(F32) + b0.astype(F32)) + b1.astype(F32)) + b2.astype(F32)

    stacked = got2.reshape(1, 3 * rows, c)
    out = _rowwise(fn, [(own2[None], c, 0)] + [(stacked, c, 0, k * (rows // tm)) for k in range(3)],
                   outs=[(c, c, 0, F32)], tm=tm, t=rows, name=name)[0][0]
    return out.reshape(own.shape)


def _perm(a, d):
    if d == 1:
        return a
    t = a.shape[0]
    return jnp.swapaxes(a.reshape((t // d, d) + a.shape[1:]), 0, 1).reshape(a.shape)


def _unperm(a, d):
    if d == 1:
        return a
    t = a.shape[0]
    return jnp.swapaxes(a.reshape((d, t // d) + a.shape[1:]), 0, 1).reshape(a.shape)


def _pad_lanes(a, w=LANE):
    return jnp.pad(a, [(0, 0)] * (a.ndim - 1) + [(0, w - a.shape[-1])])


def _rope_tables(t):
    pos = jnp.arange(t, dtype=F32)
    inv_freq = ROPE_BASE ** (-jnp.arange(0, C_ROPE, 2, dtype=F32) / C_ROPE)
    ang = pos[:, None] * inv_freq[None, :]
    cos, sin = jnp.cos(ang), jnp.sin(ang)
    ones, zeros = jnp.ones((t, C_NOPE), F32), jnp.zeros((t, C_NOPE), F32)
    tail = LANE - C_NOPE - C_ROPE
    c = jnp.concatenate([ones, cos, cos, ones[:, :tail]], axis=1)
    s = jnp.concatenate([zeros, -sin, sin, zeros[:, :tail]], axis=1)
    return c, s


def _band_calls(t, proj, projs_g, sinks):
    none = jnp.full((N_HEADS,), NEG, F32)
    a0 = M_A0 // LANE
    calls = [(proj, (a0, a0 + 4, a0 + 8), t // BLK, False, none)]
    calls += [(pg, (0, 4, 8), t // (d * BLK), False, none) for pg, d in zip(projs_g, A_DILS[1:])]
    calls.append((proj, (M_BQ // LANE, M_BK // LANE, M_BV // LANE), t // BLK, True, sinks.astype(F32)))
    return calls


def _layer_fwd(l, x, xb, w, p, biases, rope_cs):
    t = x.shape[0]
    n = f"l{l}_"
    xps = [_perm(xb, d) for d in A_DILS[1:]]
    proj = _mm(xb, w["wm_t"], tb=True, out_dtype=BF16, tm=TM_TOKENS, name=n + "proj")
    projs_g = [_mm(xp, wg, tb=True, out_dtype=BF16, tm=TM_TOKENS, tn=768, name=n + f"proj_g{i + 1}")
               for i, (xp, wg) in enumerate(zip(xps, w["wg_t"]))]
    s = {"xb": xb, "xps": xps, "proj": proj, "projs_g": projs_g}

    calls = _band_calls(t, proj, projs_g, p["sinks"])
    outs = [_band_fwd(src, offs, biases[i], sk, nb=nb, gqa=gqa, name=n + f"band{i}")
            for i, (src, offs, nb, gqa, sk) in enumerate(calls)]
    os_ = [_unperm(outs[gi][0], d) for gi, d in enumerate(A_DILS)]
    lses = [_unperm(outs[gi][1], d) for gi, d in enumerate(A_DILS)]
    o_a = _combine_fwd(os_, lses, name=n + "combine_fwd")
    o_b_f, lse_b = outs[3]
    o_b = o_b_f.astype(BF16)
    s.update(os=os_, lses=lses, lses_p=[outs[gi][1] for gi in range(3)], o_b=o_b_f, lse_b=lse_b)

    rq = _rms_fwd(proj, C_Q_RANK, M_CQ // C_Q_RANK, p["q_norm_g"], name=n + "rms_q")
    rkv = _rms_fwd(proj, C_KV_RANK, M_CDKV // C_KV_RANK, p["kv_norm_g"], name=n + "rms_kv")
    q_cp = _mm(rq, w["w_uq_p"], out_dtype=BF16, name=n + "uq")
    kv_cp = _mm(rkv, w["w_ukv_p"], out_dtype=BF16, name=n + "ukv")
    q_full = _rope_slabs(q_cp, N_HEADS, rope_cs[0], rope_cs[1], name=n + "rope_q")
    k_full = _rope_slabs(kv_cp, N_HEADS, rope_cs[0], rope_cs[1], add=(proj, (M_CDKV + C_KV_RANK) // LANE), name=n + "rope_k")
    tq = min(TQ_FWD, t)
    vt = jnp.transpose(kv_cp[:, N_HEADS * LANE:].T.reshape(N_HEADS // 2, LANE, t // tq, tq), (0, 2, 1, 3))
    o_c_f, lse_c = _mla_fwd(q_full, k_full, vt, name=n + "mla_fwd")
    o_c = o_c_f.astype(BF16)
    s.update(rq=rq, rkv=rkv, q_full=q_full, k_full=k_full, kv_cp=kv_cp, lse_c=lse_c, o_c=o_c_f)

    obs = [o_a, o_b, o_c]
    ys = [_mm(o, w["w_branch"][i], out_dtype=BF16, name=n + f"branch{i}") for i, o in enumerate(obs)]
    merged = _merge_fwd(proj, p["b_gate"], ys, name=n + "merge")
    mix = _mm(merged, w["w_out"], name=n + "out")
    x1f, x1b, z1 = _ln_fwd(x, mix, p["ln1_g"], p["ln1_b"], name=n + "ln1")
    s.update(obs=obs, ys=ys, merged=merged, z1=z1, x1b=x1b)

    ug = _mm(x1b, w["wup_g"], tm=TM_TOKENS, tn=1408, out_dtype=BF16, name=n + "up_g")
    uv = _mm(x1b, w["wup_v"], tm=TM_TOKENS, tn=1408, out_dtype=BF16, name=n + "up_v")
    h = _glu_fwd(ug, uv, w["conv_w"], p["conv_b"], name=n + "glu")
    ff = _mm(h, w["w_ffn_down"], tm=TM_TOKENS, tk=1408, name=n + "down")
    x2f, x2b, z2 = _ln_fwd(x1f, ff, p["ln2_g"], p["ln2_b"], name=n + "ln2")
    s.update(ug=ug, uv=uv, h=h, z2=z2)
    return x2f, x2b, s


def _layer_bwd(l, s, dys, coefs, w, p, biases, rope_cs):
    n = f"l{l}b_"
    t = s["z2"].shape[0]
    gw, gs = {}, {}

    dz2, dz2b, gs["ln2_g"], gs["ln2_b"] = _ln_bwd(s["z2"], p["ln2_g"], dys, coefs, name=n + "ln2")
    dh = _mm(dz2b, w["w_ffn_down"], tb=True, tm=TM_TOKENS, tn=1408, out_dtype=BF16, name=n + "d_h")
    gw["w_ffn_down"] = _mm(s["h"], dz2b, ta=True, tm=1408, tk=1024, name=n + "g_down")
    dug, duv, gw["conv_w"], gs["conv_b"] = _glu_bwd(s["ug"], s["uv"], w["conv_w"], p["conv_b"], dh, name=n + "glu")
    dx1_g = _mm(dug, w["wup_g"], tb=True, tm=TM_TOKENS, tk=1408, name=n + "d_x1g")
    dx1_v = _mm(duv, w["wup_v"], tb=True, tm=TM_TOKENS, tk=1408, name=n + "d_x1v")
    gw["w_ffn_up"] = jnp.concatenate([_mm(s["x1b"], dug, ta=True, tn=1408, tk=1024, name=n + "g_upg"),
                                      _mm(s["x1b"], duv, ta=True, tn=1408, tk=1024, name=n + "g_upv")], axis=1)

    dz1, dz1b, gs["ln1_g"], gs["ln1_b"] = _ln_bwd(s["z1"], p["ln1_g"], [dz2, dx1_g, dx1_v], [ALPHA, 1.0, 1.0], name=n + "ln1")
    dmerged = _mm(dz1b, w["w_out"], tb=True, out_dtype=BF16, name=n + "d_merged")
    gw["w_out"] = _mm(s["merged"], dz1b, ta=True, name=n + "g_out")
    dys_b, dgp, gs["b_gate"] = _merge_bwd(s["proj"], p["b_gate"], s["ys"], dmerged, name=n + "merge")
    dos = [_mm(dy, w["w_branch"][i], tb=True, out_dtype=BF16, name=n + f"d_o{i}") for i, dy in enumerate(dys_b)]
    gw["w_branch"] = jnp.stack([_mm(o, dy, ta=True, name=n + f"g_branch{i}") for i, (o, dy) in enumerate(zip(s["obs"], dys_b))])

    do_gs, dpr_gs = _combine_bwd(s["os"], s["lses"], dos[0], name=n + "combine")
    do_b, dpr_b = _delta(dos[1], s["o_b"], name=n + "delta_b")
    do_list = [_perm(a, d) for a, d in zip(do_gs, A_DILS)] + [do_b]
    dpr_list = [_perm(a, d) for a, d in zip(dpr_gs, A_DILS)] + [dpr_b]
    lse_list = s["lses_p"] + [s["lse_b"]]
    calls = _band_calls(t, s["proj"], s["projs_g"], p["sinks"])
    band = [_band_bwd(src, offs, do_list[i], lse_list[i], dpr_list[i], biases[i], sk, nb=nb, gqa=gqa, name=n + f"band{i}")
            for i, (src, offs, nb, gqa, sk) in enumerate(calls)]
    gs["sinks"] = band[3][4][:, 0, 0]
    ds_sum = jnp.concatenate([b_[3] for b_ in band], axis=0)

    do_c, delta_c = _delta(dos[2], s["o_c"], name=n + "delta_c")
    dq, dk, dv = _mla_bwd(s["q_full"], s["k_full"], s["kv_cp"], do_c, s["lse_c"], delta_c, name=n + "mla")
    dq_cp = _rope_slabs(dq, N_HEADS, rope_cs[0], -rope_cs[1], name=n + "rope_q")
    dk_sum = _rowwise(lambda i, nt, *vs: sum(vs[1:], vs[0]), [(dk[None], LANE, hh) for hh in range(N_HEADS)],
                      outs=[(LANE, LANE, 0, F32)], tm=1024, name=n + "krope_sum")[0][0]
    dkr = _rope_slabs(dk_sum, 1, rope_cs[0], -rope_cs[1], to_front=True, name=n + "rope_k")
    dkv_cp = jnp.concatenate([dk, dv], axis=1)
    d_rq = _mm(dq_cp, w["w_uq_p"], tb=True, name=n + "d_rq")
    d_rkv = _mm(dkv_cp, w["w_ukv_p"], tb=True, name=n + "d_rkv")
    g_uq = _mm(s["rq"], dq_cp, ta=True, name=n + "g_uq")
    g_ukv = _mm(s["rkv"], dkv_cp, ta=True, name=n + "g_ukv")
    gw["w_uq"] = g_uq.reshape(C_Q_RANK, N_HEADS, LANE)[:, :, :C_NOPE + C_ROPE].reshape(C_Q_RANK, -1)
    kw = N_HEADS * LANE
    gw["w_ukv"] = jnp.concatenate([g_ukv[:, :kw].reshape(C_KV_RANK, N_HEADS, LANE)[:, :, :C_NOPE],
                                   g_ukv[:, kw:].reshape(C_KV_RANK, N_HEADS, HEAD_DIM)], axis=2).reshape(C_KV_RANK, -1)
    dcq, gs["q_norm_g"] = _rms_bwd(s["proj"], C_Q_RANK, M_CQ // C_Q_RANK, p["q_norm_g"], d_rq, name=n + "rms_q")
    dckv, gs["kv_norm_g"] = _rms_bwd(s["proj"], C_KV_RANK, M_CDKV // C_KV_RANK, p["kv_norm_g"], d_rkv, name=n + "rms_kv")
    dcdkv = jnp.concatenate([dckv, dkr], axis=1)

    dproj = jnp.concatenate(dgp + list(band[0][:3]) + list(band[3][:3]) + [dcq, dcdkv], axis=1)
    dprojs_g = [jnp.concatenate(band[gi][:3], axis=1) for gi in (1, 2)]
    dx_terms = [_mm(dproj, w["wm_t"], tm=TM_TOKENS, tk=1024, name=n + "d_x")]
    dx_terms += [_unperm(_mm(dp, wg, tm=TM_TOKENS, tk=768, name=n + f"d_x_g{i + 1}"), d)
                 for i, (dp, wg, d) in enumerate(zip(dprojs_g, w["wg_t"], A_DILS[1:]))]
    g_main = _mm(dproj, s["xb"], ta=True, name=n + "g_in")
    g_groups = [_mm(dp, xp, ta=True, tm=768, name=n + f"g_in_g{i + 1}") for i, (xp, dp) in enumerate(zip(s["xps"], dprojs_g))]
    fold = lambda a, tag: _sum_rows([a.reshape(2, 2, HEAD_DIM, D_MODEL)[:, j] for j in range(2)], tm=HEAD_DIM,
                                    name=n + "g_fold_" + tag).reshape(2 * HEAD_DIM, D_MODEL)
    gw["w_in"] = jnp.concatenate([g_main[M_A0:M_BQ], g_groups[0], g_groups[1], g_main[M_BQ:M_BK], fold(g_main[M_BK:M_BV], "k"),
                                  fold(g_main[M_BV:M_CQ], "v"), g_main[M_CQ:M_CDKV + C_KV_RANK + C_ROPE], g_main[M_GATE:M_A0]], axis=0)
    return [dz1] + dx_terms, [ALPHA, 1.0, 1.0, 1.0], gw, gs, ds_sum


def _local_step(x, target, ws, rel_table, small):
    t = x.shape[0]
    ps = [{k: small[k][l] for k in SMALL} for l in range(DEPTH)]
    bucket = _bucket_index()
    bias_all = _bias_lookup(bucket, rel_table.T, name="bias_lookup").reshape(4, N_HEADS, BLK, 2 * BLK)
    step = BLK + jnp.arange(BLK)[:, None] - jnp.arange(2 * BLK)[None, :]
    biases = [jnp.where((step >= 0) & (step <= lim), bias_all[i], NEG) for i, lim in enumerate((BLK, BLK, BLK, BLK - 1))]
    rope_cs = _rope_tables(t)

    saved, h, hb = [], x, x.astype(BF16)
    for l in range(DEPTH):
        h, hb, s = _layer_fwd(l, h, hb, ws[l], ps[l], biases, rope_cs)
        saved.append(s)
    dy, loss_part = _loss_and_grad(h, target, name="loss")

    dys, coefs = [dy], [1.0]
    gws, gss, dss = [None] * DEPTH, [None] * DEPTH, [None] * DEPTH
    for l in reversed(range(DEPTH)):
        dys, coefs, gws[l], gss[l], dss[l] = _layer_bwd(l, saved[l], dys, coefs, ws[l], ps[l], biases, rope_cs)
    grad_x = _lincomb(dys, coefs, name="grad_x")
    npos = 2 * BLK * BLK
    g_rel = _bias_grad(bucket, dss[0].reshape(4 * N_HEADS, npos), dss[1].reshape(4 * N_HEADS, npos), name="bias_grad").T
    gsmall = {k: jnp.stack([gss[l][k] for l in range(DEPTH)]) for k in SMALL}
    return loss_part, grad_x, gws, gsmall, g_rel


def kernel(x, rel_table, w_in, b_gate, sinks, q_norm_g, kv_norm_g, w_uq, w_ukv, w_branch, w_out, ln1_g, ln1_b, w_ffn_up, conv_w, conv_b, w_ffn_down, ln2_g, ln2_b, loss_target, m_rel_table, m_w_in, m_b_gate, m_sinks, m_q_norm_g, m_kv_norm_g, m_w_uq, m_w_ukv, m_w_branch, m_w_out, m_ln1_g, m_ln1_b, m_w_ffn_up, m_conv_w, m_conv_b, m_w_ffn_down, m_ln2_g, m_ln2_b, v_rel_table, v_w_in, v_b_gate, v_sinks, v_q_norm_g, v_kv_norm_g, v_w_uq, v_w_ukv, v_w_branch, v_w_out, v_ln1_g, v_ln1_b, v_w_ffn_up, v_conv_w, v_conv_b, v_w_ffn_down, v_ln2_g, v_ln2_b):
    wts = dict(rel_table=rel_table, w_in=w_in, b_gate=b_gate, sinks=sinks, q_norm_g=q_norm_g, kv_norm_g=kv_norm_g, w_uq=w_uq,
               w_ukv=w_ukv, w_branch=w_branch, w_out=w_out, ln1_g=ln1_g, ln1_b=ln1_b, w_ffn_up=w_ffn_up, conv_w=conv_w,
               conv_b=conv_b, w_ffn_down=w_ffn_down, ln2_g=ln2_g, ln2_b=ln2_b)
    ms = dict(rel_table=m_rel_table, w_in=m_w_in, b_gate=m_b_gate, sinks=m_sinks, q_norm_g=m_q_norm_g, kv_norm_g=m_kv_norm_g,
              w_uq=m_w_uq, w_ukv=m_w_ukv, w_branch=m_w_branch, w_out=m_w_out, ln1_g=m_ln1_g, ln1_b=m_ln1_b, w_ffn_up=m_w_ffn_up,
              conv_w=m_conv_w, conv_b=m_conv_b, w_ffn_down=m_w_ffn_down, ln2_g=m_ln2_g, ln2_b=m_ln2_b)
    vs = dict(rel_table=v_rel_table, w_in=v_w_in, b_gate=v_b_gate, sinks=v_sinks, q_norm_g=v_q_norm_g, kv_norm_g=v_kv_norm_g,
              w_uq=v_w_uq, w_ukv=v_w_ukv, w_branch=v_w_branch, w_out=v_w_out, ln1_g=v_ln1_g, ln1_b=v_ln1_b, w_ffn_up=v_w_ffn_up,
              conv_w=v_conv_w, conv_b=v_conv_b, w_ffn_down=v_w_ffn_down, ln2_g=v_ln2_g, ln2_b=v_ln2_b)

    core = lax.axis_index("c")
    chip = 2 * lax.axis_index("x") + lax.axis_index("y")

    names = [name for name, _, _ in MATS]
    gathered = dict(zip(names, _allgather_weights([_weight_send(name, wts[name]) for name in names])))
    ws = [_full_weights(gathered, l) for l in range(DEPTH)]

    small = {k: wts[k] for k in SMALL}
    loss_part, grad_x, gws, gsmall, g_rel = _local_step(x[0], loss_target[0], ws, rel_table, small)

    rnames = [name for name, _, _ in REDUCED]
    gsend = [tuple(_grad_send(name, gws[l][name], shape, ax) for l in range(DEPTH)) for name, shape, ax in REDUCED]
    theirs = _sibling_swap(gsend)
    pairs = [_pair_add(g, t_, core, name="grad_pair_" + name) for name, g, t_ in zip(rnames, gsend, theirs)]
    arrived = _chip_scatter(pairs)
    reduced = [_chip_add(lax.dynamic_index_in_dim(p, chip, 0, keepdims=False), a, name="grad_chip_" + name)
               for name, p, a in zip(rnames, pairs, arrived)]
    others = _sibling_share(reduced)
    gshard = {}
    for name, mine, other in zip(rnames, reduced, others):
        layers = [jnp.where(core == l, mine, other) for l in range(DEPTH)]
        gshard[name] = jnp.stack([_grad_recv(name, a) for a in layers])

    conv_w_full = jnp.stack([gws[l]["conv_w"] for l in range(DEPTH)])
    small_red = _allreduce_small(_pack_small(g_rel, gsmall, conv_w_full, loss_part))
    g_rel_r, gsmall_r, conv_w_r, loss_vec = _unpack_small(small_red)
    loss = loss_vec[0]
    shard_w = 2 * D_FF // N_CHIP
    gshard["conv_w"] = lax.dynamic_slice_in_dim(conv_w_r, chip * shard_w, shard_w, axis=2)

    grads = dict(gshard)
    grads.update(gsmall_r)
    grads["rel_table"] = g_rel_r
    deltas, new_m, new_v = {}, {}, {}
    for name, _, _ in MATS:
        shp = wts[name].shape
        v2 = lambda a: a.reshape(-1, shp[-1])
        d_, m_, v_ = _adamw(v2(wts[name]), v2(grads[name]), v2(ms[name]), v2(vs[name]), name="adamw_" + name)
        deltas[name], new_m[name], new_v[name] = d_.reshape(shp), m_.reshape(shp), v_.reshape(shp)
    zero, none = jnp.zeros((LANE,), F32), jnp.zeros((0,), F32)
    sw = _pack_small(wts["rel_table"], {k: wts[k] for k in SMALL}, none, zero)
    sm = _pack_small(ms["rel_table"], {k: ms[k] for k in SMALL}, none, zero)
    sv = _pack_small(vs["rel_table"], {k: vs[k] for k in SMALL}, none, zero)
    sg = _pack_small(g_rel_r, gsmall_r, none, zero)
    sd, smn, svn = _adamw(sw, sg, sm, sv, name="adamw_small")
    for res, buf in ((deltas, sd), (new_m, smn), (new_v, svn)):
        rel_, sm_ = _unpack_small(jnp.pad(buf, ((0, small_red.shape[0] - buf.shape[0]), (0, 0))))[:2]
        res["rel_table"] = rel_
        res.update(sm_)

    return (loss, grad_x[None], *[grads[k] for k in WEIGHT_ORDER], *[deltas[k] for k in WEIGHT_ORDER],
            *[new_m[k] for k in WEIGHT_ORDER], *[new_v[k] for k in WEIGHT_ORDER])
```
